```python
import jax, jax.numpy as jnp
from jax import lax
import numpy as np

D_MODEL = 1024
BATCH = 8
SEQ = 4096
DEPTH = 4

N_MIXERS = 4
D_FF = ((8 * D_MODEL // 3 + 127) // 128) * 128
PLE_DIM = 256
CHUNK = 128
A_HALF = 3 * D_MODEL
A_GROUPS = A_HALF // 256
POOL_WINDOWS = (2, 4, 8, 16)
POOL_GROUP = D_MODEL // len(POOL_WINDOWS)
CONV_WIDTH = 31
SHORT_CONV_WIDTH = 3
EPS = 1e-6

kernel_name = "hybrid_interleaved_gmlp_pool_conformer_shortconv"


def n_layers_of(m):
    return len(range(m, DEPTH, N_MIXERS))


def rms_norm(x, g):
    xf = x.astype(jnp.float32)
    y = xf * lax.rsqrt(jnp.mean(xf * xf, axis=-1, keepdims=True) + EPS)
    return (y * g.astype(jnp.float32)).astype(x.dtype)


def layer_norm(x, g, b):
    xf = x.astype(jnp.float32)
    mu = jnp.mean(xf, axis=-1, keepdims=True)
    xc = xf - mu
    y = xc * lax.rsqrt(jnp.mean(xc * xc, axis=-1, keepdims=True) + EPS)
    return (y * g.astype(jnp.float32) + b.astype(jnp.float32)).astype(x.dtype)


def swiglu(x, w_gate, w_up, w_down):
    return (jax.nn.silu(x @ w_gate) * (x @ w_up)) @ w_down


def causal_depthwise_conv(x, w, b=None):
    k, c = w.shape
    y = lax.conv_general_dilated(
        x, w[:, None, :].astype(x.dtype), window_strides=(1,),
        padding=((k - 1, 0),), dimension_numbers=('NWC', 'WIO', 'NWC'),
        feature_group_count=c)
    return y if b is None else y + b


def mixer_gmlp_chunked(h, w_in, v_g, v_b, w_s, b_s, w_out):
    B, S, _ = h.shape
    z = jax.nn.gelu(h @ w_in)
    u, v = jnp.split(z, 2, axis=-1)
    v = layer_norm(v, v_g, v_b)
    n_chunks = S // CHUNK
    v = v.reshape(B, n_chunks, CHUNK, A_GROUPS, A_HALF // A_GROUPS)
    mask = jnp.tril(jnp.ones((CHUNK, CHUNK), dtype=bool))
    ws = jnp.where(mask[None], w_s, jnp.zeros((), w_s.dtype)).astype(v.dtype)
    sv = jnp.einsum('gts,bnsgc->bntgc', ws, v) + b_s.T[None, None, :, :, None]
    y = u * sv.reshape(B, S, A_HALF)
    return y @ w_out


def mixer_multiscale_pool(h, w_grp, scale):
    B, S, D = h.shape
    hf = h.astype(jnp.float32)
    cs = jnp.cumsum(hf, axis=1)
    t = jnp.arange(1, S + 1, dtype=jnp.float32)[None, :, None]
    outs = []
    for g, w in enumerate(POOL_WINDOWS):
        sl = slice(g * POOL_GROUP, (g + 1) * POOL_GROUP)
        c = cs[..., sl]
        prev = jnp.pad(c, ((0, 0), (w, 0), (0, 0)))[:, :S]
        mean = (c - prev) / jnp.minimum(t, w)
        outs.append(mean - hf[..., sl])
    pooled = jnp.stack(outs, axis=2).astype(h.dtype)
    y = jnp.einsum('bsgc,gcd->bsgd', pooled, w_grp).reshape(B, S, D)
    return y * scale


def mixer_conformer_conv(h, w_pw1, w_dw, b_dw, n_g, n_b, w_pw2):
    a, g = jnp.split(h @ w_pw1, 2, axis=-1)
    z = a * jax.nn.sigmoid(g)
    z = causal_depthwise_conv(z, w_dw, b_dw)
    z = jax.nn.silu(layer_norm(z, n_g, n_b))
    return z @ w_pw2


def mixer_short_gated_conv(h, w_in, w_conv, w_out):
    bg, cg, xv = jnp.split(h @ w_in, 3, axis=-1)
    return (bg * causal_depthwise_conv(cg * xv, w_conv)) @ w_out


def _fwd_setup_inputs(seed: int = 0) -> dict:
    key = jax.random.key(seed)
    keys = iter(jax.random.split(key, 64))
    f32 = jnp.float32

    def nrm(shape, scale):
        return jax.random.normal(next(keys), shape, f32) * scale

    def dense(shape, fan_in):
        return nrm(shape, fan_in ** -0.5)

    def gain(shape):
        return 1.0 + nrm(shape, 0.05)

    L, D, F = DEPTH, D_MODEL, D_FF
    nA, nB, nC, nD = n_layers_of(0), n_layers_of(1), n_layers_of(2), n_layers_of(3)
    Gc = A_HALF // A_GROUPS
    return {
        "x": nrm((BATCH, SEQ, D), 1.0),
        "p": nrm((DEPTH, BATCH, SEQ, PLE_DIM), 1.0),
        "ff1_pre_g": gain((L, D)),
        "ff1_w_gate": dense((L, D, F), D),
        "ff1_w_up": dense((L, D, F), D),
        "ff1_w_down": dense((L, F, D), F),
        "ff1_post_g": gain((L, D)),
        "mix_pre_g": gain((L, D)),
        "mix_post_g": gain((L, D)),
        "ff2_pre_g": gain((L, D)),
        "ff2_w_gate": dense((L, D, F), D),
        "ff2_w_up": dense((L, D, F), D),
        "ff2_w_down": dense((L, F, D), F),
        "ff2_post_g": gain((L, D)),
        "ple_gate_norm_g": gain((L, D)),
        "ple_w_gate": dense((L, D, D), D),
        "ple_w_proj": dense((L, PLE_DIM, D), PLE_DIM),
        "ple_post_g": gain((L, D)),
        "a_w_in": dense((nA, D, 2 * A_HALF), D),
        "a_v_norm_g": gain((nA, A_HALF)),
        "a_v_norm_b": nrm((nA, A_HALF), 0.02),
        "a_w_s": dense((nA, A_GROUPS, CHUNK, CHUNK), CHUNK),
        "a_b_s": 1.0 + nrm((nA, A_GROUPS, CHUNK), 0.05),
        "a_w_out": dense((nA, A_HALF, D), A_HALF),
        "b_w_grp": dense((nB, len(POOL_WINDOWS), POOL_GROUP, POOL_GROUP), POOL_GROUP),
        "b_scale": gain((nB, D)) + nrm((nB, D), 0.05),
        "c_w_pw1": dense((nC, D, 2 * D), D),
        "c_w_dw": dense((nC, CONV_WIDTH, D), CONV_WIDTH),
        "c_b_dw": nrm((nC, D), 0.02),
        "c_norm_g": gain((nC, D)),
        "c_norm_b": nrm((nC, D), 0.02),
        "c_w_pw2": dense((nC, D, D), D),
        "d_w_in": dense((nD, D, 3 * D), D),
        "d_w_conv": dense((nD, SHORT_CONV_WIDTH, D), SHORT_CONV_WIDTH),
        "d_w_out": dense((nD, D, D), D),
    }


def _fwd_reference(x, p,
              ff1_pre_g, ff1_w_gate, ff1_w_up, ff1_w_down, ff1_post_g,
              mix_pre_g, mix_post_g,
              ff2_pre_g, ff2_w_gate, ff2_w_up, ff2_w_down, ff2_post_g,
              ple_gate_norm_g, ple_w_gate, ple_w_proj, ple_post_g,
              a_w_in, a_v_norm_g, a_v_norm_b, a_w_s, a_b_s, a_w_out,
              b_w_grp, b_scale,
              c_w_pw1, c_w_dw, c_b_dw, c_norm_g, c_norm_b, c_w_pw2,
              d_w_in, d_w_conv, d_w_out):
    h = x
    for i in range(DEPTH):
        f = swiglu(rms_norm(h, ff1_pre_g[i]), ff1_w_gate[i], ff1_w_up[i], ff1_w_down[i])
        h = h + 0.5 * rms_norm(f, ff1_post_g[i])
        hn = rms_norm(h, mix_pre_g[i])
        m, j = i % N_MIXERS, i // N_MIXERS
        if m == 0:
            y = mixer_gmlp_chunked(hn, a_w_in[j], a_v_norm_g[j], a_v_norm_b[j],
                                   a_w_s[j], a_b_s[j], a_w_out[j])
        elif m == 1:
            y = mixer_multiscale_pool(hn, b_w_grp[j], b_scale[j])
        elif m == 2:
            y = mixer_conformer_conv(hn, c_w_pw1[j], c_w_dw[j], c_b_dw[j],
                                     c_norm_g[j], c_norm_b[j], c_w_pw2[j])
        else:
            y = mixer_short_gated_conv(hn, d_w_in[j], d_w_conv[j], d_w_out[j])
        h = h + rms_norm(y, mix_post_g[i])
        f = swiglu(rms_norm(h, ff2_pre_g[i]), ff2_w_gate[i], ff2_w_up[i], ff2_w_down[i])
        h = h + 0.5 * rms_norm(f, ff2_post_g[i])
        gate = jax.nn.sigmoid(rms_norm(h, ple_gate_norm_g[i]) @ ple_w_gate[i])
        e = (p[i].astype(h.dtype) @ ple_w_proj[i]) * gate
        h = h + rms_norm(e, ple_post_g[i])
    return h


import jax as _jax
import jax.numpy as _jnp

TWIN_FORMAT = 'train_step'
FWD_PARAMS = ['x', 'p', 'ff1_pre_g', 'ff1_w_gate', 'ff1_w_up', 'ff1_w_down', 'ff1_post_g', 'mix_pre_g', 'mix_post_g', 'ff2_pre_g', 'ff2_w_gate', 'ff2_w_up', 'ff2_w_down', 'ff2_post_g', 'ple_gate_norm_g', 'ple_w_gate', 'ple_w_proj', 'ple_post_g', 'a_w_in', 'a_v_norm_g', 'a_v_norm_b', 'a_w_s', 'a_b_s', 'a_w_out', 'b_w_grp', 'b_scale', 'c_w_pw1', 'c_w_dw', 'c_b_dw', 'c_norm_g', 'c_norm_b', 'c_w_pw2', 'd_w_in', 'd_w_conv', 'd_w_out']
TWIN_WEIGHTS = ['ff1_pre_g', 'ff1_w_gate', 'ff1_w_up', 'ff1_w_down', 'ff1_post_g', 'mix_pre_g', 'mix_post_g', 'ff2_pre_g', 'ff2_w_gate', 'ff2_w_up', 'ff2_w_down', 'ff2_post_g', 'ple_gate_norm_g', 'ple_w_gate', 'ple_w_proj', 'ple_post_g', 'a_w_in', 'a_v_norm_g', 'a_v_norm_b', 'a_w_s', 'a_b_s', 'a_w_out', 'b_w_grp', 'b_scale', 'c_w_pw1', 'c_w_dw', 'c_b_dw', 'c_norm_g', 'c_norm_b', 'c_w_pw2', 'd_w_in', 'd_w_conv', 'd_w_out']
TWIN_DIFF_INPUT = 'x'
TWIN_INPUTS = ['x', 'p', 'ff1_pre_g', 'ff1_w_gate', 'ff1_w_up', 'ff1_w_down', 'ff1_post_g', 'mix_pre_g', 'mix_post_g', 'ff2_pre_g', 'ff2_w_gate', 'ff2_w_up', 'ff2_w_down', 'ff2_post_g', 'ple_gate_norm_g', 'ple_w_gate', 'ple_w_proj', 'ple_post_g', 'a_w_in', 'a_v_norm_g', 'a_v_norm_b', 'a_w_s', 'a_b_s', 'a_w_out', 'b_w_grp', 'b_scale', 'c_w_pw1', 'c_w_dw', 'c_b_dw', 'c_norm_g', 'c_norm_b', 'c_w_pw2', 'd_w_in', 'd_w_conv', 'd_w_out', 'loss_target', 'm_ff1_pre_g', 'm_ff1_w_gate', 'm_ff1_w_up', 'm_ff1_w_down', 'm_ff1_post_g', 'm_mix_pre_g', 'm_mix_post_g', 'm_ff2_pre_g', 'm_ff2_w_gate', 'm_ff2_w_up', 'm_ff2_w_down', 'm_ff2_post_g', 'm_ple_gate_norm_g', 'm_ple_w_gate', 'm_ple_w_proj', 'm_ple_post_g', 'm_a_w_in', 'm_a_v_norm_g', 'm_a_v_norm_b', 'm_a_w_s', 'm_a_b_s', 'm_a_w_out', 'm_b_w_grp', 'm_b_scale', 'm_c_w_pw1', 'm_c_w_dw', 'm_c_b_dw', 'm_c_norm_g', 'm_c_norm_b', 'm_c_w_pw2', 'm_d_w_in', 'm_d_w_conv', 'm_d_w_out', 'v_ff1_pre_g', 'v_ff1_w_gate', 'v_ff1_w_up', 'v_ff1_w_down', 'v_ff1_post_g', 'v_mix_pre_g', 'v_mix_post_g', 'v_ff2_pre_g', 'v_ff2_w_gate', 'v_ff2_w_up', 'v_ff2_w_down', 'v_ff2_post_g', 'v_ple_gate_norm_g', 'v_ple_w_gate', 'v_ple_w_proj', 'v_ple_post_g', 'v_a_w_in', 'v_a_v_norm_g', 'v_a_v_norm_b', 'v_a_w_s', 'v_a_b_s', 'v_a_w_out', 'v_b_w_grp', 'v_b_scale', 'v_c_w_pw1', 'v_c_w_dw', 'v_c_b_dw', 'v_c_norm_g', 'v_c_norm_b', 'v_c_w_pw2', 'v_d_w_in', 'v_d_w_conv', 'v_d_w_out']
TWIN_OUTPUTS = ['loss', 'grad_x', 'grad_ff1_pre_g', 'grad_ff1_w_gate', 'grad_ff1_w_up', 'grad_ff1_w_down', 'grad_ff1_post_g', 'grad_mix_pre_g', 'grad_mix_post_g', 'grad_ff2_pre_g', 'grad_ff2_w_gate', 'grad_ff2_w_up', 'grad_ff2_w_down', 'grad_ff2_post_g', 'grad_ple_gate_norm_g', 'grad_ple_w_gate', 'grad_ple_w_proj', 'grad_ple_post_g', 'grad_a_w_in', 'grad_a_v_norm_g', 'grad_a_v_norm_b', 'grad_a_w_s', 'grad_a_b_s', 'grad_a_w_out', 'grad_b_w_grp', 'grad_b_scale', 'grad_c_w_pw1', 'grad_c_w_dw', 'grad_c_b_dw', 'grad_c_norm_g', 'grad_c_norm_b', 'grad_c_w_pw2', 'grad_d_w_in', 'grad_d_w_conv', 'grad_d_w_out', 'delta_ff1_pre_g', 'delta_ff1_w_gate', 'delta_ff1_w_up', 'delta_ff1_w_down', 'delta_ff1_post_g', 'delta_mix_pre_g', 'delta_mix_post_g', 'delta_ff2_pre_g', 'delta_ff2_w_gate', 'delta_ff2_w_up', 'delta_ff2_w_down', 'delta_ff2_post_g', 'delta_ple_gate_norm_g', 'delta_ple_w_gate', 'delta_ple_w_proj', 'delta_ple_post_g', 'delta_a_w_in', 'delta_a_v_norm_g', 'delta_a_v_norm_b', 'delta_a_w_s', 'delta_a_b_s', 'delta_a_w_out', 'delta_b_w_grp', 'delta_b_scale', 'delta_c_w_pw1', 'delta_c_w_dw', 'delta_c_b_dw', 'delta_c_norm_g', 'delta_c_norm_b', 'delta_c_w_pw2', 'delta_d_w_in', 'delta_d_w_conv', 'delta_d_w_out', 'new_m_ff1_pre_g', 'new_m_ff1_w_gate', 'new_m_ff1_w_up', 'new_m_ff1_w_down', 'new_m_ff1_post_g', 'new_m_mix_pre_g', 'new_m_mix_post_g', 'new_m_ff2_pre_g', 'new_m_ff2_w_gate', 'new_m_ff2_w_up', 'new_m_ff2_w_down', 'new_m_ff2_post_g', 'new_m_ple_gate_norm_g', 'new_m_ple_w_gate', 'new_m_ple_w_proj', 'new_m_ple_post_g', 'new_m_a_w_in', 'new_m_a_v_norm_g', 'new_m_a_v_norm_b', 'new_m_a_w_s', 'new_m_a_b_s', 'new_m_a_w_out', 'new_m_b_w_grp', 'new_m_b_scale', 'new_m_c_w_pw1', 'new_m_c_w_dw', 'new_m_c_b_dw', 'new_m_c_norm_g', 'new_m_c_norm_b', 'new_m_c_w_pw2', 'new_m_d_w_in', 'new_m_d_w_conv', 'new_m_d_w_out', 'new_v_ff1_pre_g', 'new_v_ff1_w_gate', 'new_v_ff1_w_up', 'new_v_ff1_w_down', 'new_v_ff1_post_g', 'new_v_mix_pre_g', 'new_v_mix_post_g', 'new_v_ff2_pre_g', 'new_v_ff2_w_gate', 'new_v_ff2_w_up', 'new_v_ff2_w_down', 'new_v_ff2_post_g', 'new_v_ple_gate_norm_g', 'new_v_ple_w_gate', 'new_v_ple_w_proj', 'new_v_ple_post_g', 'new_v_a_w_in', 'new_v_a_v_norm_g', 'new_v_a_v_norm_b', 'new_v_a_w_s', 'new_v_a_b_s', 'new_v_a_w_out', 'new_v_b_w_grp', 'new_v_b_scale', 'new_v_c_w_pw1', 'new_v_c_w_dw', 'new_v_c_b_dw', 'new_v_c_norm_g', 'new_v_c_norm_b', 'new_v_c_w_pw2', 'new_v_d_w_in', 'new_v_d_w_conv', 'new_v_d_w_out']
TWIN_LEAF_KINDS = {'loss': 'loss', 'grad_x': 'grad_x', 'grad_ff1_pre_g': 'grad_w', 'grad_ff1_w_gate': 'grad_w', 'grad_ff1_w_up': 'grad_w', 'grad_ff1_w_down': 'grad_w', 'grad_ff1_post_g': 'grad_w', 'grad_mix_pre_g': 'grad_w', 'grad_mix_post_g': 'grad_w', 'grad_ff2_pre_g': 'grad_w', 'grad_ff2_w_gate': 'grad_w', 'grad_ff2_w_up': 'grad_w', 'grad_ff2_w_down': 'grad_w', 'grad_ff2_post_g': 'grad_w', 'grad_ple_gate_norm_g': 'grad_w', 'grad_ple_w_gate': 'grad_w', 'grad_ple_w_proj': 'grad_w', 'grad_ple_post_g': 'grad_w', 'grad_a_w_in': 'grad_w', 'grad_a_v_norm_g': 'grad_w', 'grad_a_v_norm_b': 'grad_w', 'grad_a_w_s': 'grad_w', 'grad_a_b_s': 'grad_w', 'grad_a_w_out': 'grad_w', 'grad_b_w_grp': 'grad_w', 'grad_b_scale': 'grad_w', 'grad_c_w_pw1': 'grad_w', 'grad_c_w_dw': 'grad_w', 'grad_c_b_dw': 'grad_w', 'grad_c_norm_g': 'grad_w', 'grad_c_norm_b': 'grad_w', 'grad_c_w_pw2': 'grad_w', 'grad_d_w_in': 'grad_w', 'grad_d_w_conv': 'grad_w', 'grad_d_w_out': 'grad_w', 'delta_ff1_pre_g': 'delta_w', 'delta_ff1_w_gate': 'delta_w', 'delta_ff1_w_up': 'delta_w', 'delta_ff1_w_down': 'delta_w', 'delta_ff1_post_g': 'delta_w', 'delta_mix_pre_g': 'delta_w', 'delta_mix_post_g': 'delta_w', 'delta_ff2_pre_g': 'delta_w', 'delta_ff2_w_gate': 'delta_w', 'delta_ff2_w_up': 'delta_w', 'delta_ff2_w_down': 'delta_w', 'delta_ff2_post_g': 'delta_w', 'delta_ple_gate_norm_g': 'delta_w', 'delta_ple_w_gate': 'delta_w', 'delta_ple_w_proj': 'delta_w', 'delta_ple_post_g': 'delta_w', 'delta_a_w_in': 'delta_w', 'delta_a_v_norm_g': 'delta_w', 'delta_a_v_norm_b': 'delta_w', 'delta_a_w_s': 'delta_w', 'delta_a_b_s': 'delta_w', 'delta_a_w_out': 'delta_w', 'delta_b_w_grp': 'delta_w', 'delta_b_scale': 'delta_w', 'delta_c_w_pw1': 'delta_w', 'delta_c_w_dw': 'delta_w', 'delta_c_b_dw': 'delta_w', 'delta_c_norm_g': 'delta_w', 'delta_c_norm_b': 'delta_w', 'delta_c_w_pw2': 'delta_w', 'delta_d_w_in': 'delta_w', 'delta_d_w_conv': 'delta_w', 'delta_d_w_out': 'delta_w', 'new_m_ff1_pre_g': 'new_m', 'new_m_ff1_w_gate': 'new_m', 'new_m_ff1_w_up': 'new_m', 'new_m_ff1_w_down': 'new_m', 'new_m_ff1_post_g': 'new_m', 'new_m_mix_pre_g': 'new_m', 'new_m_mix_post_g': 'new_m', 'new_m_ff2_pre_g': 'new_m', 'new_m_ff2_w_gate': 'new_m', 'new_m_ff2_w_up': 'new_m', 'new_m_ff2_w_down': 'new_m', 'new_m_ff2_post_g': 'new_m', 'new_m_ple_gate_norm_g': 'new_m', 'new_m_ple_w_gate': 'new_m', 'new_m_ple_w_proj': 'new_m', 'new_m_ple_post_g': 'new_m', 'new_m_a_w_in': 'new_m', 'new_m_a_v_norm_g': 'new_m', 'new_m_a_v_norm_b': 'new_m', 'new_m_a_w_s': 'new_m', 'new_m_a_b_s': 'new_m', 'new_m_a_w_out': 'new_m', 'new_m_b_w_grp': 'new_m', 'new_m_b_scale': 'new_m', 'new_m_c_w_pw1': 'new_m', 'new_m_c_w_dw': 'new_m', 'new_m_c_b_dw': 'new_m', 'new_m_c_norm_g': 'new_m', 'new_m_c_norm_b': 'new_m', 'new_m_c_w_pw2': 'new_m', 'new_m_d_w_in': 'new_m', 'new_m_d_w_conv': 'new_m', 'new_m_d_w_out': 'new_m', 'new_v_ff1_pre_g': 'new_v', 'new_v_ff1_w_gate': 'new_v', 'new_v_ff1_w_up': 'new_v', 'new_v_ff1_w_down': 'new_v', 'new_v_ff1_post_g': 'new_v', 'new_v_mix_pre_g': 'new_v', 'new_v_mix_post_g': 'new_v', 'new_v_ff2_pre_g': 'new_v', 'new_v_ff2_w_gate': 'new_v', 'new_v_ff2_w_up': 'new_v', 'new_v_ff2_w_down': 'new_v', 'new_v_ff2_post_g': 'new_v', 'new_v_ple_gate_norm_g': 'new_v', 'new_v_ple_w_gate': 'new_v', 'new_v_ple_w_proj': 'new_v', 'new_v_ple_post_g': 'new_v', 'new_v_a_w_in': 'new_v', 'new_v_a_v_norm_g': 'new_v', 'new_v_a_v_norm_b': 'new_v', 'new_v_a_w_s': 'new_v', 'new_v_a_b_s': 'new_v', 'new_v_a_w_out': 'new_v', 'new_v_b_w_grp': 'new_v', 'new_v_b_scale': 'new_v', 'new_v_c_w_pw1': 'new_v', 'new_v_c_w_dw': 'new_v', 'new_v_c_b_dw': 'new_v', 'new_v_c_norm_g': 'new_v', 'new_v_c_norm_b': 'new_v', 'new_v_c_w_pw2': 'new_v', 'new_v_d_w_in': 'new_v', 'new_v_d_w_conv': 'new_v', 'new_v_d_w_out': 'new_v'}


def _forward(args):
    return _fwd_reference(*[args[k] for k in FWD_PARAMS])


def _output_shape():
    out = _jax.eval_shape(lambda: _forward(_fwd_setup_inputs(0)))
    return out.shape, out.dtype

N_MICROBATCH = 1
ADAM_LR = 0.001
ADAM_B1 = 0.9
ADAM_B2 = 0.999
ADAM_EPS = 1e-08
ADAM_WD = 0.01
ADAM_STEP = 10
PER_EXAMPLE_BATCH_AXIS = {'x': 0, 'p': 1, 'loss_target': 0}
SHARED_INPUTS = []
_WEIGHT_DTYPES = {'ff1_pre_g': _jnp.float32, 'ff1_w_gate': _jnp.float32, 'ff1_w_up': _jnp.float32, 'ff1_w_down': _jnp.float32, 'ff1_post_g': _jnp.float32, 'mix_pre_g': _jnp.float32, 'mix_post_g': _jnp.float32, 'ff2_pre_g': _jnp.float32, 'ff2_w_gate': _jnp.float32, 'ff2_w_up': _jnp.float32, 'ff2_w_down': _jnp.float32, 'ff2_post_g': _jnp.float32, 'ple_gate_norm_g': _jnp.float32, 'ple_w_gate': _jnp.float32, 'ple_w_proj': _jnp.float32, 'ple_post_g': _jnp.float32, 'a_w_in': _jnp.float32, 'a_v_norm_g': _jnp.float32, 'a_v_norm_b': _jnp.float32, 'a_w_s': _jnp.float32, 'a_b_s': _jnp.float32, 'a_w_out': _jnp.float32, 'b_w_grp': _jnp.float32, 'b_scale': _jnp.float32, 'c_w_pw1': _jnp.float32, 'c_w_dw': _jnp.float32, 'c_b_dw': _jnp.float32, 'c_norm_g': _jnp.float32, 'c_norm_b': _jnp.float32, 'c_w_pw2': _jnp.float32, 'd_w_in': _jnp.float32, 'd_w_conv': _jnp.float32, 'd_w_out': _jnp.float32}
MOMENT_SCALE = {'ff1_pre_g': 1.003140e+00, 'ff1_w_gate': 4.110755e-01, 'ff1_w_up': 4.594331e-01, 'ff1_w_down': 7.608106e-01, 'ff1_post_g': 7.594167e+00, 'mix_pre_g': 1.546609e+00, 'mix_post_g': 3.257549e+01, 'ff2_pre_g': 8.920107e-01, 'ff2_w_gate': 3.265734e-01, 'ff2_w_up': 4.290292e-01, 'ff2_w_down': 7.143372e-01, 'ff2_post_g': 7.774761e+00, 'ple_gate_norm_g': 3.521530e-01, 'ple_w_gate': 3.572761e-01, 'ple_w_proj': 9.478716e-01, 'ple_post_g': 3.227658e+01, 'a_w_in': 6.851620e-01, 'a_v_norm_g': 4.281974e-01, 'a_v_norm_b': 4.613923e-01, 'a_w_s': 5.986226e-01, 'a_b_s': 9.051445e-01, 'a_w_out': 3.927700e+00, 'b_w_grp': 2.674003e+00, 'b_scale': 3.970847e+00, 'c_w_pw1': 8.867405e-01, 'c_w_dw': 1.483611e+00, 'c_b_dw': 1.710138e+01, 'c_norm_g': 6.643310e+00, 'c_norm_b': 1.011104e+01, 'c_w_pw2': 3.835776e+00, 'd_w_in': 6.371048e-01, 'd_w_conv': 6.587678e-01, 'd_w_out': 6.301455e-01}


def _to_microbatches(a, axis):
    t = _jnp.moveaxis(a, axis, 0)
    t = t.reshape((N_MICROBATCH, t.shape[0] // N_MICROBATCH) + t.shape[1:])
    return _jnp.moveaxis(t, 1, axis + 1)


def setup_inputs(seed: int = 0) -> dict:
    inp = _fwd_setup_inputs(seed)
    key = _jax.random.fold_in(_jax.random.key(seed), 7919)
    shape, _ = _output_shape()
    out = dict(inp)
    out["loss_target"] = _jax.random.normal(_jax.random.fold_in(key, 0), shape, _jnp.float32)
    for i, name in enumerate(TWIN_WEIGHTS):
        w = inp[name].astype(_jnp.float32)
        if MOMENT_SCALE is None:
            s = _jnp.sqrt(_jnp.mean(_jnp.square(w)) + 1e-30)
        else:
            s = MOMENT_SCALE[name]
        km, kv = _jax.random.split(_jax.random.fold_in(key, i + 1))
        out[name] = w
        out["m_" + name] = s * _jax.random.normal(km, w.shape, _jnp.float32)
        out["v_" + name] = (s * s) * _jax.random.uniform(kv, w.shape, _jnp.float32, 0.5, 1.5)
    if N_MICROBATCH > 1:
        for name, axis in PER_EXAMPLE_BATCH_AXIS.items():
            out[name] = _to_microbatches(out[name], axis)
    return {'x': out['x'], 'p': out['p'], 'ff1_pre_g': out['ff1_pre_g'], 'ff1_w_gate': out['ff1_w_gate'], 'ff1_w_up': out['ff1_w_up'], 'ff1_w_down': out['ff1_w_down'], 'ff1_post_g': out['ff1_post_g'], 'mix_pre_g': out['mix_pre_g'], 'mix_post_g': out['mix_post_g'], 'ff2_pre_g': out['ff2_pre_g'], 'ff2_w_gate': out['ff2_w_gate'], 'ff2_w_up': out['ff2_w_up'], 'ff2_w_down': out['ff2_w_down'], 'ff2_post_g': out['ff2_post_g'], 'ple_gate_norm_g': out['ple_gate_norm_g'], 'ple_w_gate': out['ple_w_gate'], 'ple_w_proj': out['ple_w_proj'], 'ple_post_g': out['ple_post_g'], 'a_w_in': out['a_w_in'], 'a_v_norm_g': out['a_v_norm_g'], 'a_v_norm_b': out['a_v_norm_b'], 'a_w_s': out['a_w_s'], 'a_b_s': out['a_b_s'], 'a_w_out': out['a_w_out'], 'b_w_grp': out['b_w_grp'], 'b_scale': out['b_scale'], 'c_w_pw1': out['c_w_pw1'], 'c_w_dw': out['c_w_dw'], 'c_b_dw': out['c_b_dw'], 'c_norm_g': out['c_norm_g'], 'c_norm_b': out['c_norm_b'], 'c_w_pw2': out['c_w_pw2'], 'd_w_in': out['d_w_in'], 'd_w_conv': out['d_w_conv'], 'd_w_out': out['d_w_out'], 'loss_target': out['loss_target'], 'm_ff1_pre_g': out['m_ff1_pre_g'], 'm_ff1_w_gate': out['m_ff1_w_gate'], 'm_ff1_w_up': out['m_ff1_w_up'], 'm_ff1_w_down': out['m_ff1_w_down'], 'm_ff1_post_g': out['m_ff1_post_g'], 'm_mix_pre_g': out['m_mix_pre_g'], 'm_mix_post_g': out['m_mix_post_g'], 'm_ff2_pre_g': out['m_ff2_pre_g'], 'm_ff2_w_gate': out['m_ff2_w_gate'], 'm_ff2_w_up': out['m_ff2_w_up'], 'm_ff2_w_down': out['m_ff2_w_down'], 'm_ff2_post_g': out['m_ff2_post_g'], 'm_ple_gate_norm_g': out['m_ple_gate_norm_g'], 'm_ple_w_gate': out['m_ple_w_gate'], 'm_ple_w_proj': out['m_ple_w_proj'], 'm_ple_post_g': out['m_ple_post_g'], 'm_a_w_in': out['m_a_w_in'], 'm_a_v_norm_g': out['m_a_v_norm_g'], 'm_a_v_norm_b': out['m_a_v_norm_b'], 'm_a_w_s': out['m_a_w_s'], 'm_a_b_s': out['m_a_b_s'], 'm_a_w_out': out['m_a_w_out'], 'm_b_w_grp': out['m_b_w_grp'], 'm_b_scale': out['m_b_scale'], 'm_c_w_pw1': out['m_c_w_pw1'], 'm_c_w_dw': out['m_c_w_dw'], 'm_c_b_dw': out['m_c_b_dw'], 'm_c_norm_g': out['m_c_norm_g'], 'm_c_norm_b': out['m_c_norm_b'], 'm_c_w_pw2': out['m_c_w_pw2'], 'm_d_w_in': out['m_d_w_in'], 'm_d_w_conv': out['m_d_w_conv'], 'm_d_w_out': out['m_d_w_out'], 'v_ff1_pre_g': out['v_ff1_pre_g'], 'v_ff1_w_gate': out['v_ff1_w_gate'], 'v_ff1_w_up': out['v_ff1_w_up'], 'v_ff1_w_down': out['v_ff1_w_down'], 'v_ff1_post_g': out['v_ff1_post_g'], 'v_mix_pre_g': out['v_mix_pre_g'], 'v_mix_post_g': out['v_mix_post_g'], 'v_ff2_pre_g': out['v_ff2_pre_g'], 'v_ff2_w_gate': out['v_ff2_w_gate'], 'v_ff2_w_up': out['v_ff2_w_up'], 'v_ff2_w_down': out['v_ff2_w_down'], 'v_ff2_post_g': out['v_ff2_post_g'], 'v_ple_gate_norm_g': out['v_ple_gate_norm_g'], 'v_ple_w_gate': out['v_ple_w_gate'], 'v_ple_w_proj': out['v_ple_w_proj'], 'v_ple_post_g': out['v_ple_post_g'], 'v_a_w_in': out['v_a_w_in'], 'v_a_v_norm_g': out['v_a_v_norm_g'], 'v_a_v_norm_b': out['v_a_v_norm_b'], 'v_a_w_s': out['v_a_w_s'], 'v_a_b_s': out['v_a_b_s'], 'v_a_w_out': out['v_a_w_out'], 'v_b_w_grp': out['v_b_w_grp'], 'v_b_scale': out['v_b_scale'], 'v_c_w_pw1': out['v_c_w_pw1'], 'v_c_w_dw': out['v_c_w_dw'], 'v_c_b_dw': out['v_c_b_dw'], 'v_c_norm_g': out['v_c_norm_g'], 'v_c_norm_b': out['v_c_norm_b'], 'v_c_w_pw2': out['v_c_w_pw2'], 'v_d_w_in': out['v_d_w_in'], 'v_d_w_conv': out['v_d_w_conv'], 'v_d_w_out': out['v_d_w_out']}


def _loss(weights, diff, rest, loss_target):
    with _jax.named_scope("forward"):
        args = {**rest, TWIN_DIFF_INPUT: diff, **{k: w.astype(_WEIGHT_DTYPES[k]) for k, w in weights.items()}}
        y = _forward(args)
    with _jax.named_scope("loss_head"):
        err = _jnp.square(y.astype(_jnp.float32) - loss_target)
        return 0.5 * _jnp.sum(_jnp.mean(err, axis=-1)) if err.ndim else 0.5 * err


def _adamw(w, g, m, v):
    m = ADAM_B1 * m + (1.0 - ADAM_B1) * g
    v = ADAM_B2 * v + (1.0 - ADAM_B2) * _jnp.square(g)
    m_hat = m / (1.0 - ADAM_B1 ** ADAM_STEP)
    v_hat = v / (1.0 - ADAM_B2 ** ADAM_STEP)
    delta = -ADAM_LR * (m_hat / (_jnp.sqrt(v_hat) + ADAM_EPS) + ADAM_WD * w)
    return delta, m, v


def reference(x, p, ff1_pre_g, ff1_w_gate, ff1_w_up, ff1_w_down, ff1_post_g, mix_pre_g, mix_post_g, ff2_pre_g, ff2_w_gate, ff2_w_up, ff2_w_down, ff2_post_g, ple_gate_norm_g, ple_w_gate, ple_w_proj, ple_post_g, a_w_in, a_v_norm_g, a_v_norm_b, a_w_s, a_b_s, a_w_out, b_w_grp, b_scale, c_w_pw1, c_w_dw, c_b_dw, c_norm_g, c_norm_b, c_w_pw2, d_w_in, d_w_conv, d_w_out, loss_target, m_ff1_pre_g, m_ff1_w_gate, m_ff1_w_up, m_ff1_w_down, m_ff1_post_g, m_mix_pre_g, m_mix_post_g, m_ff2_pre_g, m_ff2_w_gate, m_ff2_w_up, m_ff2_w_down, m_ff2_post_g, m_ple_gate_norm_g, m_ple_w_gate, m_ple_w_proj, m_ple_post_g, m_a_w_in, m_a_v_norm_g, m_a_v_norm_b, m_a_w_s, m_a_b_s, m_a_w_out, m_b_w_grp, m_b_scale, m_c_w_pw1, m_c_w_dw, m_c_b_dw, m_c_norm_g, m_c_norm_b, m_c_w_pw2, m_d_w_in, m_d_w_conv, m_d_w_out, v_ff1_pre_g, v_ff1_w_gate, v_ff1_w_up, v_ff1_w_down, v_ff1_post_g, v_mix_pre_g, v_mix_post_g, v_ff2_pre_g, v_ff2_w_gate, v_ff2_w_up, v_ff2_w_down, v_ff2_post_g, v_ple_gate_norm_g, v_ple_w_gate, v_ple_w_proj, v_ple_post_g, v_a_w_in, v_a_v_norm_g, v_a_v_norm_b, v_a_w_s, v_a_b_s, v_a_w_out, v_b_w_grp, v_b_scale, v_c_w_pw1, v_c_w_dw, v_c_b_dw, v_c_norm_g, v_c_norm_b, v_c_w_pw2, v_d_w_in, v_d_w_conv, v_d_w_out):
    given = dict(x=x, p=p, ff1_pre_g=ff1_pre_g, ff1_w_gate=ff1_w_gate, ff1_w_up=ff1_w_up, ff1_w_down=ff1_w_down, ff1_post_g=ff1_post_g, mix_pre_g=mix_pre_g, mix_post_g=mix_post_g, ff2_pre_g=ff2_pre_g, ff2_w_gate=ff2_w_gate, ff2_w_up=ff2_w_up, ff2_w_down=ff2_w_down, ff2_post_g=ff2_post_g, ple_gate_norm_g=ple_gate_norm_g, ple_w_gate=ple_w_gate, ple_w_proj=ple_w_proj, ple_post_g=ple_post_g, a_w_in=a_w_in, a_v_norm_g=a_v_norm_g, a_v_norm_b=a_v_norm_b, a_w_s=a_w_s, a_b_s=a_b_s, a_w_out=a_w_out, b_w_grp=b_w_grp, b_scale=b_scale, c_w_pw1=c_w_pw1, c_w_dw=c_w_dw, c_b_dw=c_b_dw, c_norm_g=c_norm_g, c_norm_b=c_norm_b, c_w_pw2=c_w_pw2, d_w_in=d_w_in, d_w_conv=d_w_conv, d_w_out=d_w_out, loss_target=loss_target, m_ff1_pre_g=m_ff1_pre_g, m_ff1_w_gate=m_ff1_w_gate, m_ff1_w_up=m_ff1_w_up, m_ff1_w_down=m_ff1_w_down, m_ff1_post_g=m_ff1_post_g, m_mix_pre_g=m_mix_pre_g, m_mix_post_g=m_mix_post_g, m_ff2_pre_g=m_ff2_pre_g, m_ff2_w_gate=m_ff2_w_gate, m_ff2_w_up=m_ff2_w_up, m_ff2_w_down=m_ff2_w_down, m_ff2_post_g=m_ff2_post_g, m_ple_gate_norm_g=m_ple_gate_norm_g, m_ple_w_gate=m_ple_w_gate, m_ple_w_proj=m_ple_w_proj, m_ple_post_g=m_ple_post_g, m_a_w_in=m_a_w_in, m_a_v_norm_g=m_a_v_norm_g, m_a_v_norm_b=m_a_v_norm_b, m_a_w_s=m_a_w_s, m_a_b_s=m_a_b_s, m_a_w_out=m_a_w_out, m_b_w_grp=m_b_w_grp, m_b_scale=m_b_scale, m_c_w_pw1=m_c_w_pw1, m_c_w_dw=m_c_w_dw, m_c_b_dw=m_c_b_dw, m_c_norm_g=m_c_norm_g, m_c_norm_b=m_c_norm_b, m_c_w_pw2=m_c_w_pw2, m_d_w_in=m_d_w_in, m_d_w_conv=m_d_w_conv, m_d_w_out=m_d_w_out, v_ff1_pre_g=v_ff1_pre_g, v_ff1_w_gate=v_ff1_w_gate, v_ff1_w_up=v_ff1_w_up, v_ff1_w_down=v_ff1_w_down, v_ff1_post_g=v_ff1_post_g, v_mix_pre_g=v_mix_pre_g, v_mix_post_g=v_mix_post_g, v_ff2_pre_g=v_ff2_pre_g, v_ff2_w_gate=v_ff2_w_gate, v_ff2_w_up=v_ff2_w_up, v_ff2_w_down=v_ff2_w_down, v_ff2_post_g=v_ff2_post_g, v_ple_gate_norm_g=v_ple_gate_norm_g, v_ple_w_gate=v_ple_w_gate, v_ple_w_proj=v_ple_w_proj, v_ple_post_g=v_ple_post_g, v_a_w_in=v_a_w_in, v_a_v_norm_g=v_a_v_norm_g, v_a_v_norm_b=v_a_v_norm_b, v_a_w_s=v_a_w_s, v_a_b_s=v_a_b_s, v_a_w_out=v_a_w_out, v_b_w_grp=v_b_w_grp, v_b_scale=v_b_scale, v_c_w_pw1=v_c_w_pw1, v_c_w_dw=v_c_w_dw, v_c_b_dw=v_c_b_dw, v_c_norm_g=v_c_norm_g, v_c_norm_b=v_c_norm_b, v_c_w_pw2=v_c_w_pw2, v_d_w_in=v_d_w_in, v_d_w_conv=v_d_w_conv, v_d_w_out=v_d_w_out)
    weights = {n: given[n] for n in TWIN_WEIGHTS}
    shared = {n: given[n] for n in SHARED_INPUTS}
    per_example = {n: given[n] for n in ['x', 'p']}
    grad_fn = _jax.value_and_grad(_loss, argnums=(0, 1))

    def one_microbatch(ex, loss_target):
        ex = dict(ex)
        diff = ex.pop(TWIN_DIFF_INPUT)
        return grad_fn(weights, diff, {**shared, **ex}, loss_target)

    if N_MICROBATCH == 1:
        loss, (grad_w, grad_x) = one_microbatch(per_example, given["loss_target"])
    else:
        def body(carry, xs):
            loss_sum, grad_sum = carry
            l_k, (gw_k, gx_k) = one_microbatch(xs[0], xs[1])
            with _jax.named_scope("update"):
                return (loss_sum + l_k, _jax.tree.map(_jnp.add, grad_sum, gw_k)), gx_k

        init = (_jnp.zeros((), _jnp.float32), _jax.tree.map(_jnp.zeros_like, weights))
        (loss, grad_w), grad_x = _jax.lax.scan(body, init, (per_example, given["loss_target"]))
    with _jax.named_scope("update"):
        delta_w, new_m, new_v = {}, {}, {}
        for n in TWIN_WEIGHTS:
            delta_w[n], new_m[n], new_v[n] = _adamw(weights[n], grad_w[n], given["m_" + n], given["v_" + n])
    return (loss, grad_x, *[grad_w[n] for n in TWIN_WEIGHTS], *[delta_w[n] for n in TWIN_WEIGHTS],
            *[new_m[n] for n in TWIN_WEIGHTS], *[new_v[n] for n in TWIN_WEIGHTS])
```

```python
import functools
import math

import jax
import jax.numpy as jnp
from jax import lax
from jax.experimental import pallas as pl
from jax.experimental.pallas import tpu as pltpu

F32, BF16 = jnp.float32, jnp.bfloat16
EPS = 1e-6
DM = 1024
FW = 704
FB = 768
NCHIP = 4
VMEM_LIMIT = 56 * 1024 * 1024
ANY = pl.BlockSpec(memory_space=pl.ANY)
MESH = pl.DeviceIdType.MESH

A_FF = lambda l, j: (4 * l + j) * FB
A_AIN, A_CIN, A_DIN, A_COLS = 12288, 13824, 14336, 15104
B_FF1D = lambda l: l * FB
B_FF2D = lambda l: 3072 + l * FB
B_PLEG = lambda l: 6144 + l * 256
B_AOUT, B_CPW2, B_DOUT, B_ROWS = 7168, 7936, 8192, 8448
C_PROJ = lambda l: l * 256
C_BGRP, C_ROWS = 1024, 1280
V_BSCALE, V_CBDW, V_CNG, V_CNB, V_DCONV, V_CDW, V_ROWS = 0, 8, 16, 24, 32, 40, 80
E_ROWS = 320

ADAM_LR, ADAM_B1, ADAM_B2, ADAM_EPS, ADAM_WD, ADAM_STEP = 0.001, 0.9, 0.999, 1e-08, 0.01, 10


def _cp(sem):
    return pltpu.CompilerParams(dimension_semantics=sem, vmem_limit_bytes=VMEM_LIMIT)


def _sig(x):
    return 1.0 / (1.0 + jnp.exp(-x))


_GC = math.sqrt(2.0 / math.pi)


def _gelu(x):
    return 0.5 * x * (1.0 + jnp.tanh(_GC * (x + 0.044715 * x * x * x)))


def _gelu_grad(x):
    t = jnp.tanh(_GC * (x + 0.044715 * x * x * x))
    return 0.5 * (1.0 + t) + 0.5 * x * (1.0 - t * t) * _GC * (1.0 + 3.0 * 0.044715 * x * x)


def _dot_nn(a, b):
    return lax.dot_general(a, b, (((1,), (0,)), ((), ())), preferred_element_type=F32)


def _dot_nt(a, b):
    return lax.dot_general(a, b, (((1,), (1,)), ((), ())), preferred_element_type=F32)


def _dot_tn(a, b):
    return lax.dot_general(a, b, (((0,), (0,)), ((), ())), preferred_element_type=F32)


def mm_cs(x, G, off, nb, tn, out_dtype, name, roff=0):
    T, K = x.shape
    tm = min(1024, T)
    nj, ob, rb_ = nb // tn, off // tn, roff // K
    assert nb % tn == 0 and off % tn == 0 and roff % K == 0

    def body(x_ref, w_ref, o_ref):
        o_ref[...] = _dot_nn(x_ref[...], w_ref[...]).astype(o_ref.dtype)

    return pl.pallas_call(
        body, name=name, grid=(T // tm, NCHIP, nj),
        in_specs=[pl.BlockSpec((tm, K), lambda i, s, j: (i, 0)),
                  pl.BlockSpec((None, K, tn), lambda i, s, j: (s, rb_, ob + j))],
        out_specs=pl.BlockSpec((tm, tn), lambda i, s, j: (i, s * nj + j)),
        out_shape=jax.ShapeDtypeStruct((T, NCHIP * nb), out_dtype),
        compiler_params=_cp(("parallel", "arbitrary", "arbitrary")))(x, G)


def mm_cs_t(dy, G, off, nb, tn, name):
    T = dy.shape[0]
    K = G.shape[1]
    tm = min(1024, T)
    nj, ob = nb // tn, off // tn
    nk = NCHIP * nj

    def body(dy_ref, w_ref, o_ref, acc):
        k = pl.program_id(1)

        @pl.when(k == 0)
        def _():
            acc[...] = jnp.zeros_like(acc)

        acc[...] += _dot_nt(dy_ref[...], w_ref[...])

        @pl.when(k == nk - 1)
        def _():
            o_ref[...] = acc[...]

    return pl.pallas_call(
        body, name=name, grid=(T // tm, nk),
        in_specs=[pl.BlockSpec((tm, tn), lambda i, k: (i, k)),
                  pl.BlockSpec((None, K, tn), lambda i, k: (k // nj, 0, ob + k % nj))],
        out_specs=pl.BlockSpec((tm, K), lambda i, k: (i, 0)),
        out_shape=jax.ShapeDtypeStruct((T, K), F32),
        scratch_shapes=[pltpu.VMEM((tm, K), F32)],
        compiler_params=_cp(("parallel", "arbitrary")))(dy, G)


def mm_rs(a, G, off, rb, tk, name):
    T = a.shape[0]
    N = G.shape[2]
    tm = min(1024, T)
    nkk, ob = rb // tk, off // tk
    nk = NCHIP * nkk
    assert rb % tk == 0 and off % tk == 0

    def body(a_ref, w_ref, o_ref, acc):
        k = pl.program_id(1)

        @pl.when(k == 0)
        def _():
            acc[...] = jnp.zeros_like(acc)

        acc[...] += _dot_nn(a_ref[...], w_ref[...])

        @pl.when(k == nk - 1)
        def _():
            o_ref[...] = acc[...]

    return pl.pallas_call(
        body, name=name, grid=(T // tm, nk),
        in_specs=[pl.BlockSpec((tm, tk), lambda i, k: (i, k)),
                  pl.BlockSpec((None, tk, N), lambda i, k: (k // nkk, ob + k % nkk, 0))],
        out_specs=pl.BlockSpec((tm, N), lambda i, k: (i, 0)),
        out_shape=jax.ShapeDtypeStruct((T, N), F32),
        scratch_shapes=[pltpu.VMEM((tm, N), F32)],
        compiler_params=_cp(("parallel", "arbitrary")))(a, G)


def mm_rs_t(dy, G, off, rb, tk, name):
    T, N = dy.shape
    tm = min(1024, T)
    nkk, ob = rb // tk, off // tk
    nk = NCHIP * nkk

    def body(dy_ref, w_ref, o_ref):
        o_ref[...] = _dot_nt(dy_ref[...], w_ref[...]).astype(o_ref.dtype)

    return pl.pallas_call(
        body, name=name, grid=(T // tm, nk),
        in_specs=[pl.BlockSpec((tm, N), lambda i, k: (i, 0)),
                  pl.BlockSpec((None, tk, N), lambda i, k: (k // nkk, ob + k % nkk, 0))],
        out_specs=pl.BlockSpec((tm, tk), lambda i, k: (i, k)),
        out_shape=jax.ShapeDtypeStruct((T, NCHIP * rb), BF16),
        compiler_params=_cp(("parallel", "arbitrary")))(dy, G)


def mm_tn(a, b, tmm, tn, out_shape, out_map, name):
    T, M = a.shape
    N = b.shape[1]
    tt = min(1024, T)
    nt = T // tt

    def body(a_ref, b_ref, o_ref, acc):
        t = pl.program_id(2)

        @pl.when(t == 0)
        def _():
            acc[...] = jnp.zeros_like(acc)

        acc[...] += _dot_tn(a_ref[...], b_ref[...])

        @pl.when(t == nt - 1)
        def _():
            o_ref[...] = acc[...].astype(o_ref.dtype)

    return pl.pallas_call(
        body, name=name, grid=(M // tmm, N // tn, nt),
        in_specs=[pl.BlockSpec((tt, tmm), lambda i, j, t: (t, i)),
                  pl.BlockSpec((tt, tn), lambda i, j, t: (t, j))],
        out_specs=pl.BlockSpec((None, tmm, tn), lambda i, j, t: out_map(i, j)),
        out_shape=jax.ShapeDtypeStruct(out_shape, BF16),
        scratch_shapes=[pltpu.VMEM((tmm, tn), F32)],
        compiler_params=_cp(("parallel", "parallel", "arbitrary")))(a, b)


def dw_cs(x, dy, nb, tn, name):
    K = x.shape[1]
    nj = nb // tn
    return mm_tn(x, dy, K, tn, (NCHIP, K, nb), lambda i, j: (j // nj, 0, j % nj), name)


def dw_rs(a, dy, rb, tr, name):
    N = dy.shape[1]
    ni = rb // tr
    return mm_tn(a, dy, tr, N, (NCHIP, rb, N), lambda i, j: (i // ni, i % ni, 0), name)


def _rows(tm, C):
    return pl.BlockSpec((tm, C), lambda i: (i, 0))


def _vec(C):
    return pl.BlockSpec((1, C), lambda i: (0, 0))


def _acc_rows(ref, i, val):
    @pl.when(i == 0)
    def _():
        ref[...] = val

    @pl.when(i > 0)
    def _():
        ref[...] += val


def rms_fwd(h, g, out_dtype, name):
    T, C = h.shape
    tm = min(512, T)

    def body(h_ref, g_ref, o_ref):
        x = h_ref[...]
        r = lax.rsqrt(jnp.mean(x * x, axis=-1, keepdims=True) + EPS)
        o_ref[...] = (x * r * g_ref[...]).astype(o_ref.dtype)

    return pl.pallas_call(
        body, name=name, grid=(T // tm,), in_specs=[_rows(tm, C), _vec(C)], out_specs=_rows(tm, C),
        out_shape=jax.ShapeDtypeStruct((T, C), out_dtype), compiler_params=_cp(("parallel",)))(h, g)


def rms_bwd(dxn, h, g, dh_skip, name):
    T, C = h.shape
    tm = min(512, T)

    def body(d_ref, h_ref, g_ref, s_ref, o_ref, dg_ref):
        i = pl.program_id(0)
        x = h_ref[...]
        r = lax.rsqrt(jnp.mean(x * x, axis=-1, keepdims=True) + EPS)
        xh = x * r
        d = d_ref[...].astype(F32)
        t = d * g_ref[...]
        o_ref[...] = s_ref[...] + r * (t - xh * jnp.mean(t * xh, axis=-1, keepdims=True))
        _acc_rows(dg_ref, i, jnp.sum(d * xh, axis=0, keepdims=True))

    return pl.pallas_call(
        body, name=name, grid=(T // tm,),
        in_specs=[_rows(tm, C), _rows(tm, C), _vec(C), _rows(tm, C)],
        out_specs=[_rows(tm, C), _vec(C)],
        out_shape=[jax.ShapeDtypeStruct((T, C), F32), jax.ShapeDtypeStruct((1, C), F32)],
        compiler_params=_cp(("arbitrary",)))(dxn, h, g, dh_skip)


def post_res(h, f, g, scale, name):
    T, C = h.shape
    tm = min(512, T)

    def body(h_ref, f_ref, g_ref, o_ref):
        f = f_ref[...]
        r = lax.rsqrt(jnp.mean(f * f, axis=-1, keepdims=True) + EPS)
        o_ref[...] = h_ref[...] + scale * (f * r * g_ref[...])

    return pl.pallas_call(
        body, name=name, grid=(T // tm,), in_specs=[_rows(tm, C), _rows(tm, C), _vec(C)],
        out_specs=_rows(tm, C), out_shape=jax.ShapeDtypeStruct((T, C), F32),
        compiler_params=_cp(("parallel",)))(h, f, g)


def post_res_bwd(dh, f, g, scale, out_dtype, name):
    T, C = dh.shape
    tm = min(512, T)

    def body(d_ref, f_ref, g_ref, o_ref, dg_ref):
        i = pl.program_id(0)
        f = f_ref[...]
        r = lax.rsqrt(jnp.mean(f * f, axis=-1, keepdims=True) + EPS)
        d = scale * d_ref[...]
        t = d * g_ref[...]
        o_ref[...] = (r * t - f * (r * r * r * jnp.mean(t * f, axis=-1, keepdims=True))).astype(o_ref.dtype)
        _acc_rows(dg_ref, i, jnp.sum(d * f * r, axis=0, keepdims=True))

    return pl.pallas_call(
        body, name=name, grid=(T // tm,), in_specs=[_rows(tm, C), _rows(tm, C), _vec(C)],
        out_specs=[_rows(tm, C), _vec(C)],
        out_shape=[jax.ShapeDtypeStruct((T, C), out_dtype), jax.ShapeDtypeStruct((1, C), F32)],
        compiler_params=_cp(("arbitrary",)))(dh, f, g)


def ff_act(gu, name):
    T = gu.shape[0]
    tm = min(512, T)

    def body(gu_ref, o_ref):
        g = gu_ref[:, :FB].astype(F32)
        u = gu_ref[:, FB:].astype(F32)
        o_ref[...] = (g * _sig(g) * u).astype(o_ref.dtype)

    return pl.pallas_call(
        body, name=name, grid=(T // tm, NCHIP),
        in_specs=[pl.BlockSpec((tm, 2 * FB), lambda i, s: (i, s))],
        out_specs=pl.BlockSpec((tm, FB), lambda i, s: (i, s)),
        out_shape=jax.ShapeDtypeStruct((T, NCHIP * FB), BF16),
        compiler_params=_cp(("parallel", "parallel")))(gu)


def ff_act_bwd(da, gu, name):
    T = gu.shape[0]
    tm = min(512, T)

    def body(da_ref, gu_ref, o_ref):
        g = gu_ref[:, :FB].astype(F32)
        u = gu_ref[:, FB:].astype(F32)
        da = da_ref[...].astype(F32)
        s = _sig(g)
        o_ref[:, :FB] = (da * u * (s * (1.0 + g * (1.0 - s)))).astype(o_ref.dtype)
        o_ref[:, FB:] = (da * (g * s)).astype(o_ref.dtype)

    return pl.pallas_call(
        body, name=name, grid=(T // tm, NCHIP),
        in_specs=[pl.BlockSpec((tm, FB), lambda i, s: (i, s)), pl.BlockSpec((tm, 2 * FB), lambda i, s: (i, s))],
        out_specs=pl.BlockSpec((tm, 2 * FB), lambda i, s: (i, s)),
        out_shape=jax.ShapeDtypeStruct((T, NCHIP * 2 * FB), BF16),
        compiler_params=_cp(("parallel", "parallel")))(da, gu)


def ple_post(h, zg, pe, g, name):
    T, C = h.shape
    tm = min(512, T)

    def body(h_ref, z_ref, p_ref, g_ref, o_ref):
        e = p_ref[...] * _sig(z_ref[...])
        r = lax.rsqrt(jnp.mean(e * e, axis=-1, keepdims=True) + EPS)
        o_ref[...] = h_ref[...] + e * r * g_ref[...]

    return pl.pallas_call(
        body, name=name, grid=(T // tm,), in_specs=[_rows(tm, C), _rows(tm, C), _rows(tm, C), _vec(C)],
        out_specs=_rows(tm, C), out_shape=jax.ShapeDtypeStruct((T, C), F32),
        compiler_params=_cp(("parallel",)))(h, zg, pe, g)


def ple_post_bwd(dh, zg, pe, g, name):
    T, C = dh.shape
    tm = min(512, T)

    def body(d_ref, z_ref, p_ref, g_ref, dz_ref, dp_ref, dg_ref):
        i = pl.program_id(0)
        s = _sig(z_ref[...])
        pe_ = p_ref[...]
        e = pe_ * s
        r = lax.rsqrt(jnp.mean(e * e, axis=-1, keepdims=True) + EPS)
        d = d_ref[...]
        t = d * g_ref[...]
        de = r * t - e * (r * r * r * jnp.mean(t * e, axis=-1, keepdims=True))
        dp_ref[...] = (de * s).astype(dp_ref.dtype)
        dz_ref[...] = (de * pe_ * s * (1.0 - s)).astype(dz_ref.dtype)
        _acc_rows(dg_ref, i, jnp.sum(d * e * r, axis=0, keepdims=True))

    return pl.pallas_call(
        body, name=name, grid=(T // tm,), in_specs=[_rows(tm, C), _rows(tm, C), _rows(tm, C), _vec(C)],
        out_specs=[_rows(tm, C), _rows(tm, C), _vec(C)],
        out_shape=[jax.ShapeDtypeStruct((T, C), BF16), jax.ShapeDtypeStruct((T, C), BF16),
                   jax.ShapeDtypeStruct((1, C), F32)],
        compiler_params=_cp(("arbitrary",)))(dh, zg, pe, g)


def loss_head(h, tgt, name):
    T, C = h.shape
    tm = min(512, T)

    def body(h_ref, t_ref, d_ref, l_ref):
        i = pl.program_id(0)
        e = h_ref[...] - t_ref[...]
        d_ref[...] = e * (1.0 / C)
        _acc_rows(l_ref, i, jnp.sum(e * e, axis=0, keepdims=True))

    return pl.pallas_call(
        body, name=name, grid=(T // tm,), in_specs=[_rows(tm, C), _rows(tm, C)],
        out_specs=[_rows(tm, C), _vec(C)],
        out_shape=[jax.ShapeDtypeStruct((T, C), F32), jax.ShapeDtypeStruct((1, C), F32)],
        compiler_params=_cp(("arbitrary",)))(h, tgt)


AH, AG_N, AGW, CHUNK = 3072, 12, 256, 128


def _tril_bf16(w):
    r = lax.broadcasted_iota(jnp.int32, (CHUNK, CHUNK), 0)
    c = lax.broadcasted_iota(jnp.int32, (CHUNK, CHUNK), 1)
    return jnp.where(r >= c, w, 0.0).astype(BF16)


def _ln_stats(vs_ref, width):
    v = vs_ref[...]
    mu = jnp.sum(v, axis=-1, keepdims=True) * (1.0 / width)
    vc = v - mu
    var = jnp.sum(vc * vc, axis=-1, keepdims=True) * (1.0 / width)
    return mu, lax.rsqrt(var + EPS)


def gmlp_mid_fwd(zpre, vg, vb, ws, bsf, name):
    T = zpre.shape[0]

    def body(z_ref, vg_ref, vb_ref, ws_ref, bs_ref, y_ref, vs_ref):
        for g in range(AG_N):
            vs_ref[:, g * AGW:(g + 1) * AGW] = _gelu(z_ref[:, AH + g * AGW:AH + (g + 1) * AGW].astype(F32))
        mu, rstd = _ln_stats(vs_ref, AH)
        for g in range(AG_N):
            sl = slice(g * AGW, (g + 1) * AGW)
            vn = ((vs_ref[:, sl] - mu) * rstd * vg_ref[:, sl] + vb_ref[:, sl]).astype(BF16)
            sv = _dot_nn(_tril_bf16(ws_ref[g]), vn) + bs_ref[g]
            u = _gelu(z_ref[:, sl].astype(F32))
            y_ref[:, sl] = (u * sv).astype(y_ref.dtype)

    return pl.pallas_call(
        body, name=name, grid=(T // CHUNK,),
        in_specs=[_rows(CHUNK, 2 * AH), _vec(AH), _vec(AH),
                  pl.BlockSpec((AG_N, CHUNK, CHUNK), lambda i: (0, 0, 0)),
                  pl.BlockSpec((AG_N, CHUNK, AGW), lambda i: (0, 0, 0))],
        out_specs=_rows(CHUNK, AH), out_shape=jax.ShapeDtypeStruct((T, AH), BF16),
        scratch_shapes=[pltpu.VMEM((CHUNK, AH), F32)],
        compiler_params=_cp(("parallel",)))(zpre, vg, vb, ws, bsf)


def gmlp_mid_bwd(zpre, dy, vg, vb, ws, bsf, name):
    T = zpre.shape[0]

    def body(z_ref, dy_ref, vg_ref, vb_ref, ws_ref, bs_ref, dz_ref, dws_ref, dbs_ref, dvg_ref, dvb_ref,
             vs_ref, dvn_ref):
        i = pl.program_id(0)

        @pl.when(i == 0)
        def _():
            dws_ref[...] = jnp.zeros_like(dws_ref)
            dbs_ref[...] = jnp.zeros_like(dbs_ref)
            dvg_ref[...] = jnp.zeros_like(dvg_ref)
            dvb_ref[...] = jnp.zeros_like(dvb_ref)

        for g in range(AG_N):
            vs_ref[:, g * AGW:(g + 1) * AGW] = _gelu(z_ref[:, AH + g * AGW:AH + (g + 1) * AGW].astype(F32))
        mu, rstd = _ln_stats(vs_ref, AH)
        r_i = lax.broadcasted_iota(jnp.int32, (CHUNK, CHUNK), 0)
        c_i = lax.broadcasted_iota(jnp.int32, (CHUNK, CHUNK), 1)
        ones8 = jnp.ones((8, AGW), F32)
        m1 = jnp.zeros((CHUNK, 1), F32)
        m2 = jnp.zeros((CHUNK, 1), F32)
        for g in range(AG_N):
            sl = slice(g * AGW, (g + 1) * AGW)
            vh = (vs_ref[:, sl] - mu) * rstd
            vn = (vh * vg_ref[:, sl] + vb_ref[:, sl]).astype(BF16)
            wm = _tril_bf16(ws_ref[g])
            sv = _dot_nn(wm, vn) + bs_ref[g]
            zu = z_ref[:, sl].astype(F32)
            u = _gelu(zu)
            dyg = dy_ref[:, sl].astype(F32)
            dz_ref[:, sl] = (dyg * sv * _gelu_grad(zu)).astype(dz_ref.dtype)
            dsv = dyg * u
            dsv_b = dsv.astype(BF16)
            dws_ref[g] += jnp.where(r_i >= c_i, _dot_nt(dsv_b, vn), 0.0)
            dbs_ref[g] += _dot_nt(ones8, dsv)
            dvn = _dot_tn(wm, dsv_b)
            dvn_ref[:, sl] = dvn
            dvh = dvn * vg_ref[:, sl]
            m1 = m1 + jnp.sum(dvh, axis=-1, keepdims=True)
            m2 = m2 + jnp.sum(dvh * vh, axis=-1, keepdims=True)
            dvg_ref[:, sl] += jnp.sum(dvn * vh, axis=0, keepdims=True)
            dvb_ref[:, sl] += jnp.sum(dvn, axis=0, keepdims=True)
        m1 = m1 * (1.0 / AH)
        m2 = m2 * (1.0 / AH)
        for g in range(AG_N):
            sl = slice(g * AGW, (g + 1) * AGW)
            vh = (vs_ref[:, sl] - mu) * rstd
            dv = rstd * (dvn_ref[:, sl] * vg_ref[:, sl] - m1 - vh * m2)
            zv = z_ref[:, AH + g * AGW:AH + (g + 1) * AGW].astype(F32)
            dz_ref[:, AH + g * AGW:AH + (g + 1) * AGW] = (dv * _gelu_grad(zv)).astype(dz_ref.dtype)

    full3 = lambda a, b, c: pl.BlockSpec((a, b, c), lambda i: (0, 0, 0))
    return pl.pallas_call(
        body, name=name, grid=(T // CHUNK,),
        in_specs=[_rows(CHUNK, 2 * AH), _rows(CHUNK, AH), _vec(AH), _vec(AH),
                  full3(AG_N, CHUNK, CHUNK), full3(AG_N, CHUNK, AGW)],
        out_specs=[_rows(CHUNK, 2 * AH), full3(AG_N, CHUNK, CHUNK), full3(AG_N, 8, CHUNK), _vec(AH), _vec(AH)],
        out_shape=[jax.ShapeDtypeStruct((T, 2 * AH), BF16), jax.ShapeDtypeStruct((AG_N, CHUNK, CHUNK), F32),
                   jax.ShapeDtypeStruct((AG_N, 8, CHUNK), F32), jax.ShapeDtypeStruct((1, AH), F32),
                   jax.ShapeDtypeStruct((1, AH), F32)],
        scratch_shapes=[pltpu.VMEM((CHUNK, AH), F32), pltpu.VMEM((CHUNK, AH), F32)],
        compiler_params=_cp(("arbitrary",)))(zpre, dy, vg, vb, ws, bsf)


SLAB = 256
NSLAB = DM // SLAB
RC = 256
PAD = 32


def _col(T, j):
    return pl.BlockSpec((T, SLAB), lambda c: (0, j * NSLAB + c))


def _chunks(T, fn):
    def step(i, carry):
        fn(pl.multiple_of(i * RC, RC))
        return carry
    lax.fori_loop(0, T // RC, step, 0)


def _conv_taps(K):
    return [(r, [q for q in range(4) if 8 * q + r < K]) for r in range(min(8, K))]


def _causal_conv(zpad_ref, wrow, K, r0):
    acc = None
    for r, qs in _conv_taps(K):
        a = None
        for q in qs:
            term = wrow(8 * q + r) * zpad_ref[pl.ds(r0 + (PAD - 8 - 8 * q), RC + 8), :]
            a = term if a is None else a + term
        a = a if r == 0 else pltpu.roll(a, r, 0)
        acc = a if acc is None else acc + a
    return acc[8:, :]


def _anticausal_conv(gpad_ref, wrow, K, r0):
    acc = None
    for r, qs in _conv_taps(K):
        b = None
        for q in qs:
            term = wrow(8 * q + r) * gpad_ref[pl.ds(r0 + 8 * q, RC + 8), :]
            b = term if b is None else b + term
        b = b if r == 0 else pltpu.roll(b, RC + 8 - r, 0)
        acc = b if acc is None else acc + b
    return acc[:RC, :]


def _conv_dw(gpad_ref, zpad_ref, dw_ref, K, r0):
    for r, qs in _conv_taps(K):
        gw = gpad_ref[pl.ds(r0, RC + 8), :]
        p = (gw if r == 0 else pltpu.roll(gw, RC + 8 - r, 0))[:RC, :]
        for q in qs:
            z = zpad_ref[pl.ds(r0 + (PAD - 8 * q), RC), :]
            dw_ref[8 * q + r] += jnp.sum((p * z).reshape(RC // 8, 8, SLAB), axis=0)


def _zero_rows(ref, start, n):
    ref[pl.ds(start, n), :] = jnp.zeros((n, SLAB), F32)


def pool_fwd(hn, wg, sc, name):
    T = hn.shape[0]

    def body(h_ref, w_ref, s_ref, p_ref, yp_ref, y_ref, xpad):
        g = pl.program_id(0)
        wf = jnp.left_shift(2, g).astype(F32)
        _zero_rows(xpad, 0, PAD)

        def fill(r0):
            xpad[pl.ds(r0 + PAD, RC), :] = h_ref[pl.ds(r0, RC), :]
        _chunks(T, fill)

        def step(r0):
            w = xpad[pl.ds(r0 + (PAD - 16), RC + 16), :]
            s2 = w + pltpu.roll(w, 1, 0)
            s4 = s2 + pltpu.roll(s2, 2, 0)
            s8 = s4 + pltpu.roll(s4, 4, 0)
            s16 = s8 + pltpu.roll(s8, 8, 0)
            sel = jnp.where(g == 0, s2, jnp.where(g == 1, s4, jnp.where(g == 2, s8, s16)))[16:, :]
            t1 = (r0 + 1 + lax.broadcasted_iota(jnp.int32, (RC, SLAB), 0)).astype(F32)
            pooled = (sel / jnp.minimum(t1, wf) - w[16:, :]).astype(BF16)
            p_ref[pl.ds(r0, RC), :] = pooled
            yp = _dot_nn(pooled, w_ref[...])
            yp_ref[pl.ds(r0, RC), :] = yp
            y_ref[pl.ds(r0, RC), :] = yp * s_ref[...]
        _chunks(T, step)

    slab = pl.BlockSpec((T, SLAB), lambda c: (0, c))
    return pl.pallas_call(
        body, name=name, grid=(NSLAB,),
        in_specs=[slab, pl.BlockSpec((None, SLAB, SLAB), lambda c: (c, 0, 0)), pl.BlockSpec((1, SLAB), lambda c: (0, c))],
        out_specs=[slab, slab, slab],
        out_shape=[jax.ShapeDtypeStruct((T, DM), BF16), jax.ShapeDtypeStruct((T, DM), F32),
                   jax.ShapeDtypeStruct((T, DM), F32)],
        scratch_shapes=[pltpu.VMEM((T + PAD, SLAB), F32)],
        compiler_params=_cp(("parallel",)))(hn, wg, sc)


def pool_bwd(dy, ypre, pooled, wg, sc, name):
    T = dy.shape[0]

    def body(d_ref, yp_ref, p_ref, w_ref, s_ref, dh_ref, dw_ref, ds_ref, qpad, dwacc, dsacc):
        g = pl.program_id(0)
        wf = jnp.left_shift(2, g).astype(F32)
        dwacc[...] = jnp.zeros_like(dwacc)
        dsacc[...] = jnp.zeros_like(dsacc)
        _zero_rows(qpad, T, PAD)

        def first(r0):
            d = d_ref[pl.ds(r0, RC), :]
            dsacc[...] += jnp.sum((d * yp_ref[pl.ds(r0, RC), :]).reshape(RC // 8, 8, SLAB), axis=0)
            dyp = (d * s_ref[...]).astype(BF16)
            dpool = _dot_nt(dyp, w_ref[...])
            dwacc[...] += _dot_tn(p_ref[pl.ds(r0, RC), :], dyp)
            t1 = (r0 + 1 + lax.broadcasted_iota(jnp.int32, (RC, SLAB), 0)).astype(F32)
            qpad[pl.ds(r0, RC), :] = dpool / jnp.minimum(t1, wf)
            dh_ref[pl.ds(r0, RC), :] = dpool
        _chunks(T, first)

        def second(r0):
            w = qpad[pl.ds(r0, RC + 16), :]
            n = RC + 16
            a2 = w + pltpu.roll(w, n - 1, 0)
            a4 = a2 + pltpu.roll(a2, n - 2, 0)
            a8 = a4 + pltpu.roll(a4, n - 4, 0)
            a16 = a8 + pltpu.roll(a8, n - 8, 0)
            sel = jnp.where(g == 0, a2, jnp.where(g == 1, a4, jnp.where(g == 2, a8, a16)))[:RC, :]
            dh_ref[pl.ds(r0, RC), :] = sel - dh_ref[pl.ds(r0, RC), :]
        _chunks(T, second)
        dw_ref[...] = dwacc[...]
        ds_ref[...] = jnp.sum(dsacc[...], axis=0, keepdims=True)

    slab = pl.BlockSpec((T, SLAB), lambda c: (0, c))
    wspec = pl.BlockSpec((None, SLAB, SLAB), lambda c: (c, 0, 0))
    vec = pl.BlockSpec((1, SLAB), lambda c: (0, c))
    return pl.pallas_call(
        body, name=name, grid=(NSLAB,),
        in_specs=[slab, slab, slab, wspec, vec],
        out_specs=[slab, wspec, vec],
        out_shape=[jax.ShapeDtypeStruct((T, DM), F32), jax.ShapeDtypeStruct((NSLAB, SLAB, SLAB), F32),
                   jax.ShapeDtypeStruct((1, DM), F32)],
        scratch_shapes=[pltpu.VMEM((T + PAD, SLAB), F32), pltpu.VMEM((SLAB, SLAB), F32), pltpu.VMEM((8, SLAB), F32)],
        compiler_params=_cp(("parallel",)))(dy, ypre, pooled, wg, sc)


KC = 31
KD = 3


def conf_conv_fwd(ag, wdw, bdw, name):
    T = ag.shape[0]

    def body(a_ref, g_ref, w_ref, b_ref, o_ref, zpad):
        _zero_rows(zpad, 0, PAD)

        def fill(r0):
            a = a_ref[pl.ds(r0, RC), :].astype(F32)
            gt = g_ref[pl.ds(r0, RC), :].astype(F32)
            zpad[pl.ds(r0 + PAD, RC), :] = a * _sig(gt)
        _chunks(T, fill)
        wrow = lambda j: w_ref[KC - 1 - j:KC - j, :]

        def step(r0):
            o_ref[pl.ds(r0, RC), :] = _causal_conv(zpad, wrow, KC, r0) + b_ref[...]
        _chunks(T, step)

    vec = pl.BlockSpec((1, SLAB), lambda c: (0, c))
    return pl.pallas_call(
        body, name=name, grid=(NSLAB,),
        in_specs=[_col(T, 0), _col(T, 1), pl.BlockSpec((32, SLAB), lambda c: (0, c)), vec],
        out_specs=pl.BlockSpec((T, SLAB), lambda c: (0, c)),
        out_shape=jax.ShapeDtypeStruct((T, DM), F32),
        scratch_shapes=[pltpu.VMEM((T + PAD, SLAB), F32)],
        compiler_params=_cp(("parallel",)))(ag, ag, wdw, bdw)


def conf_conv_bwd(dzc, ag, wdw, name):
    T = ag.shape[0]

    def body(d_ref, a_ref, g_ref, w_ref, da_ref, dg_ref, dw_ref, db_ref, zpad, gpad, dwacc, dbacc):
        _zero_rows(zpad, 0, PAD)
        _zero_rows(gpad, T, PAD)
        dwacc[...] = jnp.zeros_like(dwacc)
        dbacc[...] = jnp.zeros_like(dbacc)

        def fill(r0):
            a = a_ref[pl.ds(r0, RC), :].astype(F32)
            gt = g_ref[pl.ds(r0, RC), :].astype(F32)
            zpad[pl.ds(r0 + PAD, RC), :] = a * _sig(gt)
            d = d_ref[pl.ds(r0, RC), :]
            gpad[pl.ds(r0, RC), :] = d
            dbacc[...] += jnp.sum(d.reshape(RC // 8, 8, SLAB), axis=0)
        _chunks(T, fill)
        wrow = lambda j: w_ref[KC - 1 - j:KC - j, :]

        def step(r0):
            dz = _anticausal_conv(gpad, wrow, KC, r0)
            a = a_ref[pl.ds(r0, RC), :].astype(F32)
            s = _sig(g_ref[pl.ds(r0, RC), :].astype(F32))
            da_ref[pl.ds(r0, RC), :] = (dz * s).astype(da_ref.dtype)
            dg_ref[pl.ds(r0, RC), :] = (dz * a * s * (1.0 - s)).astype(dg_ref.dtype)
            _conv_dw(gpad, zpad, dwacc, KC, r0)
        _chunks(T, step)
        dw_ref[...] = jnp.zeros_like(dw_ref)
        for k in range(KC):
            dw_ref[k:k + 1, :] = jnp.sum(dwacc[KC - 1 - k], axis=0, keepdims=True)
        db_ref[...] = jnp.sum(dbacc[...], axis=0, keepdims=True)

    vec = pl.BlockSpec((1, SLAB), lambda c: (0, c))
    w32 = pl.BlockSpec((32, SLAB), lambda c: (0, c))
    return pl.pallas_call(
        body, name=name, grid=(NSLAB,),
        in_specs=[pl.BlockSpec((T, SLAB), lambda c: (0, c)), _col(T, 0), _col(T, 1), w32],
        out_specs=[_col(T, 0), _col(T, 0), w32, vec],
        out_shape=[jax.ShapeDtypeStruct((T, DM), BF16), jax.ShapeDtypeStruct((T, DM), BF16),
                   jax.ShapeDtypeStruct((32, DM), F32), jax.ShapeDtypeStruct((1, DM), F32)],
        scratch_shapes=[pltpu.VMEM((T + PAD, SLAB), F32), pltpu.VMEM((T + PAD, SLAB), F32),
                        pltpu.VMEM((32, 8, SLAB), F32), pltpu.VMEM((8, SLAB), F32)],
        compiler_params=_cp(("parallel",)))(dzc, ag, ag, wdw)


def conf_ln_fwd(zc, g, b, name):
    T, C = zc.shape
    tm = min(512, T)

    def body(z_ref, g_ref, b_ref, o_ref):
        x = z_ref[...]
        xc = x - jnp.mean(x, axis=-1, keepdims=True)
        r = lax.rsqrt(jnp.mean(xc * xc, axis=-1, keepdims=True) + EPS)
        zl = xc * r * g_ref[...] + b_ref[...]
        o_ref[...] = (zl * _sig(zl)).astype(o_ref.dtype)

    return pl.pallas_call(
        body, name=name, grid=(T // tm,), in_specs=[_rows(tm, C), _vec(C), _vec(C)], out_specs=_rows(tm, C),
        out_shape=jax.ShapeDtypeStruct((T, C), BF16), compiler_params=_cp(("parallel",)))(zc, g, b)


def conf_ln_bwd(dzs, zc, g, b, name):
    T, C = zc.shape
    tm = min(512, T)

    def body(d_ref, z_ref, g_ref, b_ref, o_ref, dg_ref, db_ref):
        i = pl.program_id(0)
        x = z_ref[...]
        xc = x - jnp.mean(x, axis=-1, keepdims=True)
        r = lax.rsqrt(jnp.mean(xc * xc, axis=-1, keepdims=True) + EPS)
        xh = xc * r
        zl = xh * g_ref[...] + b_ref[...]
        s = _sig(zl)
        dzl = d_ref[...].astype(F32) * (s * (1.0 + zl * (1.0 - s)))
        t = dzl * g_ref[...]
        o_ref[...] = r * (t - jnp.mean(t, axis=-1, keepdims=True) - xh * jnp.mean(t * xh, axis=-1, keepdims=True))
        _acc_rows(dg_ref, i, jnp.sum(dzl * xh, axis=0, keepdims=True))
        _acc_rows(db_ref, i, jnp.sum(dzl, axis=0, keepdims=True))

    return pl.pallas_call(
        body, name=name, grid=(T // tm,), in_specs=[_rows(tm, C), _rows(tm, C), _vec(C), _vec(C)],
        out_specs=[_rows(tm, C), _vec(C), _vec(C)],
        out_shape=[jax.ShapeDtypeStruct((T, C), F32), jax.ShapeDtypeStruct((1, C), F32),
                   jax.ShapeDtypeStruct((1, C), F32)],
        compiler_params=_cp(("arbitrary",)))(dzs, zc, g, b)


def sconv_fwd(bgx, wc, name):
    T = bgx.shape[0]

    def body(b_ref, c_ref, x_ref, w_ref, o_ref, zpad):
        _zero_rows(zpad, 0, PAD)

        def fill(r0):
            zpad[pl.ds(r0 + PAD, RC), :] = c_ref[pl.ds(r0, RC), :].astype(F32) * x_ref[pl.ds(r0, RC), :].astype(F32)
        _chunks(T, fill)
        wrow = lambda j: w_ref[KD - 1 - j:KD - j, :]

        def step(r0):
            qc = _causal_conv(zpad, wrow, KD, r0)
            o_ref[pl.ds(r0, RC), :] = (b_ref[pl.ds(r0, RC), :].astype(F32) * qc).astype(o_ref.dtype)
        _chunks(T, step)

    return pl.pallas_call(
        body, name=name, grid=(NSLAB,),
        in_specs=[_col(T, 0), _col(T, 1), _col(T, 2), pl.BlockSpec((8, SLAB), lambda c: (0, c))],
        out_specs=pl.BlockSpec((T, SLAB), lambda c: (0, c)),
        out_shape=jax.ShapeDtypeStruct((T, DM), BF16),
        scratch_shapes=[pltpu.VMEM((T + PAD, SLAB), F32)],
        compiler_params=_cp(("parallel",)))(bgx, bgx, bgx, wc)


def sconv_bwd(dy, bgx, wc, name):
    T = bgx.shape[0]

    def body(d_ref, b_ref, c_ref, x_ref, w_ref, db_ref, dc_ref, dx_ref, dw_ref, zpad, gpad, dwacc):
        _zero_rows(zpad, 0, PAD)
        _zero_rows(gpad, T, PAD)
        dwacc[...] = jnp.zeros_like(dwacc)

        def fill(r0):
            zpad[pl.ds(r0 + PAD, RC), :] = c_ref[pl.ds(r0, RC), :].astype(F32) * x_ref[pl.ds(r0, RC), :].astype(F32)
            gpad[pl.ds(r0, RC), :] = d_ref[pl.ds(r0, RC), :].astype(F32) * b_ref[pl.ds(r0, RC), :].astype(F32)
        _chunks(T, fill)
        wrow = lambda j: w_ref[KD - 1 - j:KD - j, :]

        def step(r0):
            qc = _causal_conv(zpad, wrow, KD, r0)
            db_ref[pl.ds(r0, RC), :] = (d_ref[pl.ds(r0, RC), :].astype(F32) * qc).astype(db_ref.dtype)
            dq = _anticausal_conv(gpad, wrow, KD, r0)
            dc_ref[pl.ds(r0, RC), :] = (dq * x_ref[pl.ds(r0, RC), :].astype(F32)).astype(dc_ref.dtype)
            dx_ref[pl.ds(r0, RC), :] = (dq * c_ref[pl.ds(r0, RC), :].astype(F32)).astype(dx_ref.dtype)
            _conv_dw(gpad, zpad, dwacc, KD, r0)
        _chunks(T, step)
        dw_ref[...] = jnp.zeros_like(dw_ref)
        for k in range(KD):
            dw_ref[k:k + 1, :] = jnp.sum(dwacc[KD - 1 - k], axis=0, keepdims=True)

    w8 = pl.BlockSpec((8, SLAB), lambda c: (0, c))
    return pl.pallas_call(
        body, name=name, grid=(NSLAB,),
        in_specs=[pl.BlockSpec((T, SLAB), lambda c: (0, c)), _col(T, 0), _col(T, 1), _col(T, 2), w8],
        out_specs=[_col(T, 0), _col(T, 0), _col(T, 0), w8],
        out_shape=[jax.ShapeDtypeStruct((T, DM), BF16)] * 3 + [jax.ShapeDtypeStruct((8, DM), F32)],
        scratch_shapes=[pltpu.VMEM((T + PAD, SLAB), F32), pltpu.VMEM((T + PAD, SLAB), F32),
                        pltpu.VMEM((8, 8, SLAB), F32)],
        compiler_params=_cp(("parallel",)))(dy, bgx, bgx, bgx, wc)


def merge_cols(parts, name):
    T = parts[0].shape[0]
    n = len(parts)
    C = n * DM
    tm = min(512, T)

    def body(*refs):
        o_ref = refs[n]
        for j in range(n):
            o_ref[:, j * DM:(j + 1) * DM] = refs[j][...]

    return pl.pallas_call(
        body, name=name, grid=(T // tm,),
        in_specs=[_rows(tm, DM) for j in range(n)],
        out_specs=_rows(tm, C), out_shape=jax.ShapeDtypeStruct((T, C), parts[0].dtype),
        compiler_params=_cp(("parallel",)))(*parts)


def adamw(w, g, m, v, name):
    shape = w.shape
    C = shape[-1]
    R = w.size // C
    w2, g2, m2, v2 = (a.reshape(R, C) for a in (w, g, m, v))
    tr = R
    while tr * C > 512 * 1024 and tr % 16 == 0:
        tr //= 2
    bc1 = 1.0 - ADAM_B1 ** ADAM_STEP
    bc2 = 1.0 - ADAM_B2 ** ADAM_STEP

    def body(w_ref, g_ref, m_ref, v_ref, d_ref, nm_ref, nv_ref):
        gg = g_ref[...]
        nm = ADAM_B1 * m_ref[...] + (1.0 - ADAM_B1) * gg
        nv = ADAM_B2 * v_ref[...] + (1.0 - ADAM_B2) * (gg * gg)
        nm_ref[...] = nm
        nv_ref[...] = nv
        d_ref[...] = -ADAM_LR * ((nm / bc1) / (jnp.sqrt(nv / bc2) + ADAM_EPS) + ADAM_WD * w_ref[...])

    spec = pl.BlockSpec((tr, C), lambda i: (i, 0))
    outs = pl.pallas_call(
        body, name=name, grid=(R // tr,), in_specs=[spec] * 4, out_specs=[spec] * 3,
        out_shape=[jax.ShapeDtypeStruct((R, C), F32)] * 3, compiler_params=_cp(("parallel",)))(w2, g2, m2, v2)
    return tuple(o.reshape(shape) for o in outs)


def _place():
    x, y, c = lax.axis_index("x"), lax.axis_index("y"), lax.axis_index("c")
    return x, y, c


def all_gather_chips(bufs, name):
    n = len(bufs)

    def body(*refs):
        src, dst = refs[:n], refs[n:2 * n]
        send, recv, lsem = refs[2 * n:]
        x, y, c = _place()
        me = 2 * x + y
        sib = (x, y, 1 - c)
        chips = [(1 - x, y), (x, 1 - y), (1 - x, 1 - y)]

        def half(b, slot, hc):
            rows = bufs[b].shape[0] // 2
            return dst[b].at[slot, pl.ds(hc * rows, rows), :]

        def remote(k, s, d, to):
            return pltpu.make_async_remote_copy(src_ref=s, dst_ref=d, send_sem=send.at[k], recv_sem=recv.at[k],
                                                device_id=to, device_id_type=MESH)

        local = [pltpu.make_async_copy(src[b], dst[b].at[me], lsem.at[b]) for b in range(n)]
        for cp in local:
            cp.start()
        first = []
        for b in range(n):
            rows = bufs[b].shape[0] // 2
            mine = src[b].at[pl.ds(c * rows, rows), :]
            for j, (cx, cy) in enumerate(chips):
                first.append(remote(b * 6 + j, mine, half(b, me, c), (cx, cy, c)))
        for cp in first:
            cp.start()
        passed = []
        for b in range(n):
            for j, (cx, cy) in enumerate(chips):
                slot = 2 * cx + cy
                remote(b * 6 + j, half(b, slot, c), half(b, slot, c), (cx, cy, c)).wait_recv()
                fwd = remote(b * 6 + 3 + j, half(b, slot, c), half(b, slot, c), sib)
                fwd.start()
                passed.append(fwd)
        for b in range(n):
            for j, (cx, cy) in enumerate(chips):
                slot = 2 * cx + cy
                remote(b * 6 + 3 + j, half(b, slot, 1 - c), half(b, slot, 1 - c), sib).wait_recv()
        for cp in first + passed:
            cp.wait_send()
        for cp in local:
            cp.wait()

    return pl.pallas_call(
        body, name=name, in_specs=[ANY] * n, out_specs=[ANY] * n,
        out_shape=[jax.ShapeDtypeStruct((NCHIP,) + b.shape, b.dtype) for b in bufs],
        scratch_shapes=[pltpu.SemaphoreType.DMA((6 * n,)), pltpu.SemaphoreType.DMA((6 * n,)),
                        pltpu.SemaphoreType.DMA((n,))],
        compiler_params=pltpu.CompilerParams())(*bufs)


def pair_exchange(bufs, name):
    n = len(bufs)

    def body(*refs):
        src, dst = refs[:n], refs[n:2 * n]
        send, recv = refs[2 * n:]
        x, y, c = _place()
        cps = []
        for b in range(n):
            rows = bufs[b].shape[1] // 2
            cp = pltpu.make_async_remote_copy(
                src_ref=src[b].at[:, pl.ds((1 - c) * rows, rows), :], dst_ref=dst[b],
                send_sem=send.at[b], recv_sem=recv.at[b], device_id=(x, y, 1 - c), device_id_type=MESH)
            cp.start()
            cps.append(cp)
        for cp in cps:
            cp.wait()

    return pl.pallas_call(
        body, name=name, in_specs=[ANY] * n, out_specs=[ANY] * n,
        out_shape=[jax.ShapeDtypeStruct((NCHIP, b.shape[1] // 2, b.shape[2]), b.dtype) for b in bufs],
        scratch_shapes=[pltpu.SemaphoreType.DMA((n,)), pltpu.SemaphoreType.DMA((n,))],
        compiler_params=pltpu.CompilerParams())(*bufs)


def add_half(full, got, tr, tc, name):
    _, R, C = full.shape
    rows = R // 2
    nr = rows // tr
    c_arr = lax.axis_index("c").astype(jnp.int32).reshape(1)

    def body(c_ref, a_ref, b_ref, o_ref):
        o_ref[...] = (a_ref[...].astype(F32) + b_ref[...].astype(F32)).astype(o_ref.dtype)

    return pl.pallas_call(
        body, name=name,
        grid_spec=pltpu.PrefetchScalarGridSpec(
            num_scalar_prefetch=1, grid=(NCHIP, nr, C // tc),
            in_specs=[pl.BlockSpec((None, tr, tc), lambda s, i, j, c_ref: (s, c_ref[0] * nr + i, j)),
                      pl.BlockSpec((None, tr, tc), lambda s, i, j, c_ref: (s, i, j))],
            out_specs=pl.BlockSpec((None, tr, tc), lambda s, i, j, c_ref: (s, i, j))),
        out_shape=jax.ShapeDtypeStruct((NCHIP, rows, C), full.dtype),
        compiler_params=_cp(("parallel", "parallel", "parallel")))(c_arr, full, got)


def chip_exchange(bufs, name):
    n = len(bufs)

    def body(*refs):
        src, dst = refs[:n], refs[n:2 * n]
        send, recv, lsem = refs[2 * n:]
        x, y, c = _place()
        me = 2 * x + y
        chips = [(1 - x, y), (x, 1 - y), (1 - x, 1 - y)]
        local = [pltpu.make_async_copy(src[b].at[me], dst[b].at[me], lsem.at[b]) for b in range(n)]
        for cp in local:
            cp.start()
        cps = []
        for b in range(n):
            for j, (cx, cy) in enumerate(chips):
                cp = pltpu.make_async_remote_copy(
                    src_ref=src[b].at[2 * cx + cy], dst_ref=dst[b].at[me],
                    send_sem=send.at[b * 3 + j], recv_sem=recv.at[b * 3 + j],
                    device_id=(cx, cy, c), device_id_type=MESH)
                cp.start()
                cps.append((cp, b, cx, cy, j))
        for cp, b, cx, cy, j in cps:
            cp.wait_send()
            pltpu.make_async_remote_copy(
                src_ref=src[b].at[me], dst_ref=dst[b].at[2 * cx + cy],
                send_sem=send.at[b * 3 + j], recv_sem=recv.at[b * 3 + j],
                device_id=(cx, cy, c), device_id_type=MESH).wait_recv()
        for cp in local:
            cp.wait()

    return pl.pallas_call(
        body, name=name, in_specs=[ANY] * n, out_specs=[ANY] * n,
        out_shape=[jax.ShapeDtypeStruct(b.shape, b.dtype) for b in bufs],
        scratch_shapes=[pltpu.SemaphoreType.DMA((3 * n,)), pltpu.SemaphoreType.DMA((3 * n,)),
                        pltpu.SemaphoreType.DMA((n,))],
        compiler_params=pltpu.CompilerParams())(*bufs)


def sum_slots(buf, tr, tc, name):
    _, r, C = buf.shape

    def body(a_ref, o_ref):
        o_ref[...] = ((a_ref[0].astype(F32) + a_ref[1].astype(F32)) + a_ref[2].astype(F32)) + a_ref[3].astype(F32)

    return pl.pallas_call(
        body, name=name, grid=(r // tr, C // tc),
        in_specs=[pl.BlockSpec((NCHIP, tr, tc), lambda i, j: (0, i, j))],
        out_specs=pl.BlockSpec((tr, tc), lambda i, j: (i, j)),
        out_shape=jax.ShapeDtypeStruct((r, C), F32),
        compiler_params=_cp(("parallel", "parallel")))(buf)


def pair_share(bufs, name):
    n = len(bufs)

    def body(*refs):
        src, dst = refs[:n], refs[n:2 * n]
        send, recv, lsem = refs[2 * n:]
        x, y, c = _place()
        cps = []
        for b in range(n):
            rows = bufs[b].shape[0]
            here = dst[b].at[pl.ds(c * rows, rows), :]
            lc = pltpu.make_async_copy(src[b], here, lsem.at[b])
            lc.start()
            cp = pltpu.make_async_remote_copy(src_ref=src[b], dst_ref=here, send_sem=send.at[b], recv_sem=recv.at[b],
                                              device_id=(x, y, 1 - c), device_id_type=MESH)
            cp.start()
            cps.append((lc, cp, b))
        for lc, cp, b in cps:
            rows = bufs[b].shape[0]
            there = dst[b].at[pl.ds((1 - c) * rows, rows), :]
            cp.wait_send()
            pltpu.make_async_remote_copy(src_ref=src[b], dst_ref=there, send_sem=send.at[b], recv_sem=recv.at[b],
                                         device_id=(x, y, 1 - c), device_id_type=MESH).wait_recv()
            lc.wait()

    return pl.pallas_call(
        body, name=name, in_specs=[ANY] * n, out_specs=[ANY] * n,
        out_shape=[jax.ShapeDtypeStruct((2 * b.shape[0], b.shape[1]), b.dtype) for b in bufs],
        scratch_shapes=[pltpu.SemaphoreType.DMA((n,)), pltpu.SemaphoreType.DMA((n,)), pltpu.SemaphoreType.DMA((n,))],
        compiler_params=pltpu.CompilerParams())(*bufs)


_TILES = {"A": (64, A_COLS), "B": (128, 1024), "C": (640, 256), "V": (40, 256), "E": (40, 1024)}


def reduce_scatter(parts):
    names = list(parts)
    got = pair_exchange([parts[k] for k in names], "rs_pair_exchange")
    sums = [add_half(parts[k], got[i], *_TILES[k], name="rs_add_pair_" + k) for i, k in enumerate(names)]
    landed = chip_exchange(sums, "rs_chip_exchange")
    halves = [sum_slots(landed[i], *_TILES[k], name="rs_sum_chips_" + k) for i, k in enumerate(names)]
    full = pair_share(halves, "rs_pair_share")
    return dict(zip(names, full))


def _row(a, l):
    return a[l:l + 1]


def local_step(x, p, tgt, small, GA, GB, GC, vecs, a_ws, a_bs):
    T = x.shape[0]
    bgrp = GC[:, C_BGRP:C_BGRP + 256, :].reshape(NCHIP, 4, 64, 256).transpose(1, 0, 2, 3).reshape(4, 256, 256)
    bsf = jnp.broadcast_to(a_bs[:, :, None], (AG_N, CHUNK, AGW))
    vrow = lambda r: vecs[r:r + 1]
    saved = []
    h = x

    def ff_fwd(h, l, which, pre, post):
        j0 = 0 if which == 1 else 2
        down = B_FF1D(l) if which == 1 else B_FF2D(l)
        tag = "ff%d_l%d" % (which, l)
        xn = rms_fwd(h, _row(pre, l), BF16, tag + "_pre")
        gu = mm_cs(xn, GA, A_FF(l, j0), 2 * FB, FB, BF16, tag + "_gateup")
        a = ff_act(gu, tag + "_act")
        f = mm_rs(a, GB, down, FB, FB, tag + "_down")
        hn = post_res(h, f, _row(post, l), 0.5, tag + "_post")
        return hn, (h, xn, gu, a, f)

    for l in range(4):
        rec = {}
        h, rec["ff1"] = ff_fwd(h, l, 1, small["ff1_pre_g"], small["ff1_post_g"])
        tag = "mix_l%d" % l
        h_in = h
        if l == 1:
            hn = rms_fwd(h, _row(small["mix_pre_g"], l), F32, tag + "_pre")
            pooled, ypre, f = pool_fwd(hn, bgrp, vrow(V_BSCALE), tag + "_pool")
            rec["mix"] = (h_in, pooled, ypre, f)
        else:
            hn = rms_fwd(h, _row(small["mix_pre_g"], l), BF16, tag + "_pre")
            if l == 0:
                zpre = mm_cs(hn, GA, A_AIN, 1536, 512, BF16, tag + "_in")
                y = gmlp_mid_fwd(zpre, small["a_v_norm_g"], small["a_v_norm_b"], a_ws, bsf, tag + "_gate")
                f = mm_rs(y, GB, B_AOUT, 768, 256, tag + "_out")
                rec["mix"] = (h_in, hn, zpre, y, f)
            elif l == 2:
                ag = mm_cs(hn, GA, A_CIN, 512, 512, BF16, tag + "_pw1")
                zc = conf_conv_fwd(ag, vecs[V_CDW:V_CDW + 32], vrow(V_CBDW), tag + "_conv")
                zs = conf_ln_fwd(zc, vrow(V_CNG), vrow(V_CNB), tag + "_ln")
                f = mm_rs(zs, GB, B_CPW2, 256, 256, tag + "_pw2")
                rec["mix"] = (h_in, hn, ag, zc, zs, f)
            else:
                bgx = mm_cs(hn, GA, A_DIN, 768, 256, BF16, tag + "_in")
                y = sconv_fwd(bgx, vecs[V_DCONV:V_DCONV + 8], tag + "_conv")
                f = mm_rs(y, GB, B_DOUT, 256, 256, tag + "_out")
                rec["mix"] = (h_in, hn, bgx, y, f)
        h = post_res(h, f, _row(small["mix_post_g"], l), 1.0, tag + "_post")
        h, rec["ff2"] = ff_fwd(h, l, 2, small["ff2_pre_g"], small["ff2_post_g"])
        tag = "ple_l%d" % l
        xn = rms_fwd(h, _row(small["ple_gate_norm_g"], l), BF16, tag + "_pre")
        zg = mm_rs(xn, GB, B_PLEG(l), 256, 256, tag + "_gate")
        pb = p[l].astype(BF16)
        pe = mm_cs(pb, GC, 0, 256, 256, F32, tag + "_proj", roff=C_PROJ(l))
        rec["ple"] = (h, xn, zg, pe, pb)
        h = ple_post(h, zg, pe, _row(small["ple_post_g"], l), tag + "_post")
        saved.append(rec)

    dh, loss_cols = loss_head(h, tgt, "loss_head")

    gA = {}
    gB = {}
    gC = {}
    gV = {}
    gains = {k: [None] * 4 for k in ("ff1_pre_g", "ff1_post_g", "mix_pre_g", "mix_post_g", "ff2_pre_g", "ff2_post_g",
                                      "ple_gate_norm_g", "ple_post_g")}
    extra = {}

    def ff_bwd(dh, l, which, pre, post, rec):
        j0 = 0 if which == 1 else 2
        down = B_FF1D(l) if which == 1 else B_FF2D(l)
        tag = "ff%d_l%d_b" % (which, l)
        h_in, xn, gu, a, f = rec
        df, dpost = post_res_bwd(dh, f, _row(post, l), 0.5, BF16, tag + "_post")
        da = mm_rs_t(df, GB, down, FB, FB, tag + "_down")
        gB[down] = dw_rs(a, df, FB, FB, tag + "_dwdown")
        dgu = ff_act_bwd(da, gu, tag + "_act")
        dxn = mm_cs_t(dgu, GA, A_FF(l, j0), 2 * FB, FB, tag + "_gateup")
        gA[A_FF(l, j0)] = dw_cs(xn, dgu, 2 * FB, FB, tag + "_dwgateup")
        dh_in, dpre = rms_bwd(dxn, h_in, _row(pre, l), dh, tag + "_pre")
        return dh_in, dpre, dpost

    for l in reversed(range(4)):
        rec = saved[l]
        tag = "ple_l%d_b" % l
        h_in, xn, zg, pe, pb = rec["ple"]
        dzg, dpe, gains["ple_post_g"][l] = ple_post_bwd(dh, zg, pe, _row(small["ple_post_g"], l), tag + "_post")
        gC[C_PROJ(l)] = dw_cs(pb, dpe, 256, 256, tag + "_dwproj")
        dxn = mm_rs_t(dzg, GB, B_PLEG(l), 256, 256, tag + "_gate")
        gB[B_PLEG(l)] = dw_rs(xn, dzg, 256, 256, tag + "_dwgate")
        dh, gains["ple_gate_norm_g"][l] = rms_bwd(dxn, h_in, _row(small["ple_gate_norm_g"], l), dh, tag + "_pre")

        dh, gains["ff2_pre_g"][l], gains["ff2_post_g"][l] = ff_bwd(
            dh, l, 2, small["ff2_pre_g"], small["ff2_post_g"], rec["ff2"])

        tag = "mix_l%d_b" % l
        mix = rec["mix"]
        h_in, f = mix[0], mix[-1]
        if l == 1:
            _, pooled, ypre, _ = mix
            df, gains["mix_post_g"][l] = post_res_bwd(dh, f, _row(small["mix_post_g"], l), 1.0, F32, tag + "_post")
            dhn, dwg, dsc = pool_bwd(df, ypre, pooled, bgrp, vrow(V_BSCALE), tag + "_pool")
            gC[C_BGRP] = dwg.astype(BF16).reshape(4, NCHIP, 64, 256).transpose(1, 0, 2, 3).reshape(NCHIP, 256, 256)
            gV[V_BSCALE] = jnp.pad(dsc, ((0, 7), (0, 0)))
        else:
            df, gains["mix_post_g"][l] = post_res_bwd(dh, f, _row(small["mix_post_g"], l), 1.0, BF16, tag + "_post")
            if l == 0:
                _, hn, zpre, y, _ = mix
                dy = mm_rs_t(df, GB, B_AOUT, 768, 256, tag + "_out")
                gB[B_AOUT] = dw_rs(y, df, 768, 256, tag + "_dwout")
                dz, dws, dbs, dvg, dvb = gmlp_mid_bwd(zpre, dy, small["a_v_norm_g"], small["a_v_norm_b"], a_ws, bsf,
                                                      tag + "_gate")
                extra.update(a_w_s=dws, a_b_s=dbs[:, 0, :], a_v_norm_g=dvg, a_v_norm_b=dvb)
                dhn = mm_cs_t(dz, GA, A_AIN, 1536, 512, tag + "_in")
                gA[A_AIN] = dw_cs(hn, dz, 1536, 512, tag + "_dwin")
            elif l == 2:
                _, hn, ag, zc, zs, _ = mix
                dzs = mm_rs_t(df, GB, B_CPW2, 256, 256, tag + "_pw2")
                gB[B_CPW2] = dw_rs(zs, df, 256, 256, tag + "_dwpw2")
                dzc, dng, dnb = conf_ln_bwd(dzs, zc, vrow(V_CNG), vrow(V_CNB), tag + "_ln")
                da_, dg_, dwdw, dbdw = conf_conv_bwd(dzc, ag, vecs[V_CDW:V_CDW + 32], tag + "_conv")
                dag = merge_cols([da_, dg_], tag + "_merge")
                gV[V_CDW] = dwdw
                gV[V_CBDW] = jnp.pad(dbdw, ((0, 7), (0, 0)))
                gV[V_CNG] = jnp.pad(dng, ((0, 7), (0, 0)))
                gV[V_CNB] = jnp.pad(dnb, ((0, 7), (0, 0)))
                dhn = mm_cs_t(dag, GA, A_CIN, 512, 512, tag + "_pw1")
                gA[A_CIN] = dw_cs(hn, dag, 512, 512, tag + "_dwpw1")
            else:
                _, hn, bgx, y, _ = mix
                dy = mm_rs_t(df, GB, B_DOUT, 256, 256, tag + "_out")
                gB[B_DOUT] = dw_rs(y, df, 256, 256, tag + "_dwout")
                db_, dc_, dx_, dwc = sconv_bwd(dy, bgx, vecs[V_DCONV:V_DCONV + 8], tag + "_conv")
                dbgx = merge_cols([db_, dc_, dx_], tag + "_merge")
                gV[V_DCONV] = dwc
                dhn = mm_cs_t(dbgx, GA, A_DIN, 768, 256, tag + "_in")
                gA[A_DIN] = dw_cs(hn, dbgx, 768, 256, tag + "_dwin")
        dh, gains["mix_pre_g"][l] = rms_bwd(dhn, h_in, _row(small["mix_pre_g"], l), dh, tag + "_pre")

        dh, gains["ff1_pre_g"][l], gains["ff1_post_g"][l] = ff_bwd(
            dh, l, 1, small["ff1_pre_g"], small["ff1_post_g"], rec["ff1"])

    return loss_cols, dh, gA, gB, gC, gV, gains, extra


GAIN_NAMES = ("ff1_pre_g", "ff1_post_g", "mix_pre_g", "mix_post_g", "ff2_pre_g", "ff2_post_g", "ple_gate_norm_g",
              "ple_post_g")


def _pad_rows(a, rows):
    return jnp.pad(a, ((0, rows - a.shape[0]), (0, 0)))


def kernel(x, p, ff1_pre_g, ff1_w_gate, ff1_w_up, ff1_w_down, ff1_post_g, mix_pre_g, mix_post_g, ff2_pre_g, ff2_w_gate, ff2_w_up, ff2_w_down, ff2_post_g, ple_gate_norm_g, ple_w_gate, ple_w_proj, ple_post_g, a_w_in, a_v_norm_g, a_v_norm_b, a_w_s, a_b_s, a_w_out, b_w_grp, b_scale, c_w_pw1, c_w_dw, c_b_dw, c_norm_g, c_norm_b, c_w_pw2, d_w_in, d_w_conv, d_w_out, loss_target, m_ff1_pre_g, m_ff1_w_gate, m_ff1_w_up, m_ff1_w_down, m_ff1_post_g, m_mix_pre_g, m_mix_post_g, m_ff2_pre_g, m_ff2_w_gate, m_ff2_w_up, m_ff2_w_down, m_ff2_post_g, m_ple_gate_norm_g, m_ple_w_gate, m_ple_w_proj, m_ple_post_g, m_a_w_in, m_a_v_norm_g, m_a_v_norm_b, m_a_w_s, m_a_b_s, m_a_w_out, m_b_w_grp, m_b_scale, m_c_w_pw1, m_c_w_dw, m_c_b_dw, m_c_norm_g, m_c_norm_b, m_c_w_pw2, m_d_w_in, m_d_w_conv, m_d_w_out, v_ff1_pre_g, v_ff1_w_gate, v_ff1_w_up, v_ff1_w_down, v_ff1_post_g, v_mix_pre_g, v_mix_post_g, v_ff2_pre_g, v_ff2_w_gate, v_ff2_w_up, v_ff2_w_down, v_ff2_post_g, v_ple_gate_norm_g, v_ple_w_gate, v_ple_w_proj, v_ple_post_g, v_a_w_in, v_a_v_norm_g, v_a_v_norm_b, v_a_w_s, v_a_b_s, v_a_w_out, v_b_w_grp, v_b_scale, v_c_w_pw1, v_c_w_dw, v_c_b_dw, v_c_norm_g, v_c_norm_b, v_c_w_pw2, v_d_w_in, v_d_w_conv, v_d_w_out):
    args = dict(locals())
    wnames = ["ff1_pre_g", "ff1_w_gate", "ff1_w_up", "ff1_w_down", "ff1_post_g", "mix_pre_g", "mix_post_g",
              "ff2_pre_g", "ff2_w_gate", "ff2_w_up", "ff2_w_down", "ff2_post_g", "ple_gate_norm_g", "ple_w_gate",
              "ple_w_proj", "ple_post_g", "a_w_in", "a_v_norm_g", "a_v_norm_b", "a_w_s", "a_b_s", "a_w_out",
              "b_w_grp", "b_scale", "c_w_pw1", "c_w_dw", "c_b_dw", "c_norm_g", "c_norm_b", "c_w_pw2", "d_w_in",
              "d_w_conv", "d_w_out"]

    PA, PB, PC, PV = pack_weights(args)
    GA, GB, GC, GV = all_gather_chips([PA, PB, PC, PV], "gather_weights")
    vecs = GV.transpose(1, 0, 2).reshape(V_ROWS, DM)

    small = {k: args[k] for k in GAIN_NAMES}
    small["a_v_norm_g"] = a_v_norm_g
    small["a_v_norm_b"] = a_v_norm_b
    loss_cols, grad_x, gA, gB, gC, gV, gains, extra = local_step(
        x[0], p[:, 0], loss_target[0], small, GA, GB, GC, vecs, a_w_s[0], a_b_s[0])

    loss = lax.psum((0.5 / DM) * jnp.sum(loss_cols), ("x", "y", "c"))

    dA, dB, dC, dV, dE = pack_grads(gA, gB, gC, gV, gains, extra)
    red = reduce_scatter({"A": dA, "B": dB, "C": dC, "V": dV, "E": dE})
    (gE,) = all_gather_chips([red["E"]], "gather_replicated_grads")
    grads = unpack_grads(red["A"], red["B"], red["C"], red["V"], gE.reshape(E_ROWS, DM))

    deltas, new_m, new_v = {}, {}, {}
    for k in wnames:
        deltas[k], new_m[k], new_v[k] = adamw(args[k], grads[k], args["m_" + k], args["v_" + k], "adamw_" + k)
    return (loss, grad_x[None], *[grads[k] for k in wnames], *[deltas[k] for k in wnames],
            *[new_m[k] for k in wnames], *[new_v[k] for k in wnames])


def pack_weights(w):
    padc = lambda a: jnp.pad(a, ((0, 0), (0, FB - FW)))
    colsA = []
    for l in range(4):
        colsA += [padc(w["ff1_w_gate"][l]), padc(w["ff1_w_up"][l]), padc(w["ff2_w_gate"][l]), padc(w["ff2_w_up"][l])]
    colsA += [w["a_w_in"][0], w["c_w_pw1"][0], w["d_w_in"][0]]
    PA = jnp.concatenate(colsA, axis=1).astype(BF16)
    rowsB = [_pad_rows(w["ff1_w_down"][l], FB) for l in range(4)] + [_pad_rows(w["ff2_w_down"][l], FB) for l in range(4)]
    rowsB += [w["ple_w_gate"][l] for l in range(4)] + [w["a_w_out"][0], w["c_w_pw2"][0], w["d_w_out"][0]]
    PB = jnp.concatenate(rowsB, axis=0).astype(BF16)
    PC = jnp.concatenate([w["ple_w_proj"][l] for l in range(4)] + [w["b_w_grp"][0].reshape(256, 256)],
                         axis=0).astype(BF16)
    PV = jnp.concatenate([_pad_rows(w["b_scale"], 8), _pad_rows(w["c_b_dw"], 8), _pad_rows(w["c_norm_g"], 8),
                          _pad_rows(w["c_norm_b"], 8), _pad_rows(w["d_w_conv"][0], 8), _pad_rows(w["c_w_dw"][0], 40)],
                         axis=0)
    return PA, PB, PC, PV


def pack_grads(gA, gB, gC, gV, gains, extra):
    dA = jnp.concatenate([gA[A_FF(l, j)] for l in range(4) for j in (0, 2)] + [gA[A_AIN], gA[A_CIN], gA[A_DIN]], axis=2)
    dB = jnp.concatenate([gB[B_FF1D(l)] for l in range(4)] + [gB[B_FF2D(l)] for l in range(4)]
                         + [gB[B_PLEG(l)] for l in range(4)] + [gB[B_AOUT], gB[B_CPW2], gB[B_DOUT]], axis=1)
    dC = jnp.concatenate([gC[C_PROJ(l)] for l in range(4)] + [gC[C_BGRP]], axis=1)
    dVt = jnp.concatenate([gV[V_BSCALE], gV[V_CBDW], gV[V_CNG], gV[V_CNB], gV[V_DCONV], gV[V_CDW],
                           jnp.zeros((8, DM), F32)], axis=0)
    dV = dVt.reshape(V_ROWS, NCHIP, 256).transpose(1, 0, 2)
    rowsE = [_pad_rows(jnp.concatenate(gains[k], axis=0), 8) for k in GAIN_NAMES]
    rowsE += [_pad_rows(extra["a_v_norm_g"].reshape(3, DM), 8), _pad_rows(extra["a_v_norm_b"].reshape(3, DM), 8),
              jnp.pad(extra["a_b_s"].reshape(1536), (0, 8 * DM - 1536)).reshape(8, DM),
              extra["a_w_s"].reshape(192, DM)]
    dE = _pad_rows(jnp.concatenate(rowsE, axis=0), E_ROWS).reshape(NCHIP, E_ROWS // NCHIP, DM)
    return dA, dB, dC, dV, dE


def unpack_grads(RA, RB, RCc, RV, gE):
    grads = {}
    for i, k in enumerate(GAIN_NAMES):
        grads[k] = gE[8 * i:8 * i + 4]
    grads["a_v_norm_g"] = gE[64:67].reshape(1, 3072)
    grads["a_v_norm_b"] = gE[72:75].reshape(1, 3072)
    grads["a_b_s"] = gE[80:88].reshape(8 * DM)[:1536].reshape(1, 12, 128)
    grads["a_w_s"] = gE[88:280].reshape(1, 12, 128, 128)
    colA = lambda off, n: RA[:, off:off + n]
    grads["ff1_w_gate"] = jnp.stack([colA(A_FF(l, 0), FW) for l in range(4)])
    grads["ff1_w_up"] = jnp.stack([colA(A_FF(l, 1), FW) for l in range(4)])
    grads["ff2_w_gate"] = jnp.stack([colA(A_FF(l, 2), FW) for l in range(4)])
    grads["ff2_w_up"] = jnp.stack([colA(A_FF(l, 3), FW) for l in range(4)])
    grads["a_w_in"] = colA(A_AIN, 1536)[None]
    grads["c_w_pw1"] = colA(A_CIN, 512)[None]
    grads["d_w_in"] = colA(A_DIN, 768)[None]
    rowB = lambda off, n: RB[off:off + n]
    grads["ff1_w_down"] = jnp.stack([rowB(B_FF1D(l), FW) for l in range(4)])
    grads["ff2_w_down"] = jnp.stack([rowB(B_FF2D(l), FW) for l in range(4)])
    grads["ple_w_gate"] = jnp.stack([rowB(B_PLEG(l), 256) for l in range(4)])
    grads["a_w_out"] = rowB(B_AOUT, 768)[None]
    grads["c_w_pw2"] = rowB(B_CPW2, 256)[None]
    grads["d_w_out"] = rowB(B_DOUT, 256)[None]
    grads["ple_w_proj"] = jnp.stack([RCc[C_PROJ(l):C_PROJ(l) + 256] for l in range(4)])
    grads["b_w_grp"] = RCc[C_BGRP:C_BGRP + 256].reshape(1, 4, 64, 256)
    grads["b_scale"] = RV[V_BSCALE:V_BSCALE + 1]
    grads["c_b_dw"] = RV[V_CBDW:V_CBDW + 1]
    grads["c_norm_g"] = RV[V_CNG:V_CNG + 1]
    grads["c_norm_b"] = RV[V_CNB:V_CNB + 1]
    grads["d_w_conv"] = RV[V_DCONV:V_DCONV + 3][None]
    grads["c_w_dw"] = RV[V_CDW:V_CDW + 31][None]
    return grads
```

```python
import functools
import math

import jax
import jax.numpy as jnp
from jax import lax
from jax.experimental import pallas as pl
from jax.experimental.pallas import tpu as pltpu

F32, BF16 = jnp.float32, jnp.bfloat16
EPS = 1e-6
DM = 1024
FW = 704
FB = 768
NCHIP = 4
VMEM_LIMIT = 56 * 1024 * 1024
ANY = pl.BlockSpec(memory_space=pl.ANY)
MESH = pl.DeviceIdType.MESH

A_FF = lambda l, j: (4 * l + j) * FB
A_AIN, A_CIN, A_DIN, A_COLS = 12288, 13824, 14336, 15104
B_FF1D = lambda l: l * FB
B_FF2D = lambda l: 3072 + l * FB
B_PLEG = lambda l: 6144 + l * 256
B_AOUT, B_CPW2, B_DOUT, B_ROWS = 7168, 7936, 8192, 8448
C_PROJ = lambda l: l * 256
C_BGRP, C_ROWS = 1024, 1280
V_BSCALE, V_CBDW, V_CNG, V_CNB, V_DCONV, V_CDW, V_ROWS = 0, 8, 16, 24, 32, 40, 80
E_ROWS = 320

ADAM_LR, ADAM_B1, ADAM_B2, ADAM_EPS, ADAM_WD, ADAM_STEP = 0.001, 0.9, 0.999, 1e-08, 0.01, 10


def _cp(sem):
    return pltpu.CompilerParams(dimension_semantics=sem, vmem_limit_bytes=VMEM_LIMIT)


def _sig(x):
    return 1.0 / (1.0 + jnp.exp(-x))


_GC = math.sqrt(2.0 / math.pi)


def _gelu(x):
    return 0.5 * x * (1.0 + jnp.tanh(_GC * (x + 0.044715 * x * x * x)))


def _gelu_grad(x):
    t = jnp.tanh(_GC * (x + 0.044715 * x * x * x))
    return 0.5 * (1.0 + t) + 0.5 * x * (1.0 - t * t) * _GC * (1.0 + 3.0 * 0.044715 * x * x)


def _dot_nn(a, b):
    return lax.dot_general(a, b, (((1,), (0,)), ((), ())), preferred_element_type=F32)


def _dot_nt(a, b):
    return lax.dot_general(a, b, (((1,), (1,)), ((), ())), preferred_element_type=F32)


def _dot_tn(a, b):
    return lax.dot_general(a, b, (((0,), (0,)), ((), ())), preferred_element_type=F32)


def mm_cs(x, G, off, nb, tn, out_dtype, name, roff=0):
    T, K = x.shape
    tm = min(1024, T)
    nj, ob, rb_ = nb // tn, off // tn, roff // K
    assert nb % tn == 0 and off % tn == 0 and roff % K == 0

    def body(x_ref, w_ref, o_ref):
        o_ref[...] = _dot_nn(x_ref[...], w_ref[...]).astype(o_ref.dtype)

    return pl.pallas_call(
        body, name=name, grid=(T // tm, NCHIP, nj),
        in_specs=[pl.BlockSpec((tm, K), lambda i, s, j: (i, 0)),
                  pl.BlockSpec((None, K, tn), lambda i, s, j: (s, rb_, ob + j))],
        out_specs=pl.BlockSpec((tm, tn), lambda i, s, j: (i, s * nj + j)),
        out_shape=jax.ShapeDtypeStruct((T, NCHIP * nb), out_dtype),
        compiler_params=_cp(("parallel", "arbitrary", "arbitrary")))(x, G)


def mm_cs_t(dy, G, off, nb, tn, name):
    T = dy.shape[0]
    K = G.shape[1]
    tm = min(1024, T)
    nj, ob = nb // tn, off // tn
    nk = NCHIP * nj

    def body(dy_ref, w_ref, o_ref, acc):
        k = pl.program_id(1)

        @pl.when(k == 0)
        def _():
            acc[...] = jnp.zeros_like(acc)

        acc[...] += _dot_nt(dy_ref[...], w_ref[...])

        @pl.when(k == nk - 1)
        def _():
            o_ref[...] = acc[...]

    return pl.pallas_call(
        body, name=name, grid=(T // tm, nk),
        in_specs=[pl.BlockSpec((tm, tn), lambda i, k: (i, k)),
                  pl.BlockSpec((None, K, tn), lambda i, k: (k // nj, 0, ob + k % nj))],
        out_specs=pl.BlockSpec((tm, K), lambda i, k: (i, 0)),
        out_shape=jax.ShapeDtypeStruct((T, K), F32),
        scratch_shapes=[pltpu.VMEM((tm, K), F32)],
        compiler_params=_cp(("parallel", "arbitrary")))(dy, G)


def mm_rs(a, G, off, rb, tk, name):
    T = a.shape[0]
    N = G.shape[2]
    tm = min(1024, T)
    nkk, ob = rb // tk, off // tk
    nk = NCHIP * nkk
    assert rb % tk == 0 and off % tk == 0

    def body(a_ref, w_ref, o_ref, acc):
        k = pl.program_id(1)

        @pl.when(k == 0)
        def _():
            acc[...] = jnp.zeros_like(acc)

        acc[...] += _dot_nn(a_ref[...], w_ref[...])

        @pl.when(k == nk - 1)
        def _():
            o_ref[...] = acc[...]

    return pl.pallas_call(
        body, name=name, grid=(T // tm, nk),
        in_specs=[pl.BlockSpec((tm, tk), lambda i, k: (i, k)),
                  pl.BlockSpec((None, tk, N), lambda i, k: (k // nkk, ob + k % nkk, 0))],
        out_specs=pl.BlockSpec((tm, N), lambda i, k: (i, 0)),
        out_shape=jax.ShapeDtypeStruct((T, N), F32),
        scratch_shapes=[pltpu.VMEM((tm, N), F32)],
        compiler_params=_cp(("parallel", "arbitrary")))(a, G)


def mm_rs_t(dy, G, off, rb, tk, name):
    T, N = dy.shape
    tm = min(1024, T)
    nkk, ob = rb // tk, off // tk
    nk = NCHIP * nkk

    def body(dy_ref, w_ref, o_ref):
        o_ref[...] = _dot_nt(dy_ref[...], w_ref[...]).astype(o_ref.dtype)

    return pl.pallas_call(
        body, name=name, grid=(T // tm, nk),
        in_specs=[pl.BlockSpec((tm, N), lambda i, k: (i, 0)),
                  pl.BlockSpec((None, tk, N), lambda i, k: (k // nkk, ob + k % nkk, 0))],
        out_specs=pl.BlockSpec((tm, tk), lambda i, k: (i, k)),
        out_shape=jax.ShapeDtypeStruct((T, NCHIP * rb), BF16),
        compiler_params=_cp(("parallel", "arbitrary")))(dy, G)


def mm_tn(a, b, tmm, tn, out_shape, out_map, name):
    T, M = a.shape
    N = b.shape[1]
    tt = min(1024, T)
    nt = T // tt

    def body(a_ref, b_ref, o_ref, acc):
        t = pl.program_id(2)

        @pl.when(t == 0)
        def _():
            acc[...] = jnp.zeros_like(acc)

        acc[...] += _dot_tn(a_ref[...], b_ref[...])

        @pl.when(t == nt - 1)
        def _():
            o_ref[...] = acc[...].astype(o_ref.dtype)

    return pl.pallas_call(
        body, name=name, grid=(M // tmm, N // tn, nt),
        in_specs=[pl.BlockSpec((tt, tmm), lambda i, j, t: (t, i)),
                  pl.BlockSpec((tt, tn), lambda i, j, t: (t, j))],
        out_specs=pl.BlockSpec((None, tmm, tn), lambda i, j, t: out_map(i, j)),
        out_shape=jax.ShapeDtypeStruct(out_shape, BF16),
        scratch_shapes=[pltpu.VMEM((tmm, tn), F32)],
        compiler_params=_cp(("parallel", "parallel", "arbitrary")))(a, b)


def dw_cs(x, dy, nb, tn, name):
    K = x.shape[1]
    nj = nb // tn
    return mm_tn(x, dy, K, tn, (NCHIP, K, nb), lambda i, j: (j // nj, 0, j % nj), name)


def dw_rs(a, dy, rb, tr, name):
    N = dy.shape[1]
    ni = rb // tr
    return mm_tn(a, dy, tr, N, (NCHIP, rb, N), lambda i, j: (i // ni, i % ni, 0), name)


def ff_gateup(xn, GA, off, name):
    T, K = xn.shape
    tm = min(512, T)
    ob = off // (2 * FB)
    assert off % (2 * FB) == 0

    def body(x_ref, w_ref, gu_ref, a_ref):
        r = _dot_nn(x_ref[...], w_ref[...])
        g, u = r[:, :FB], r[:, FB:]
        gu_ref[...] = r.astype(gu_ref.dtype)
        a_ref[...] = (g * _sig(g) * u).astype(a_ref.dtype)

    return pl.pallas_call(
        body, name=name, grid=(T // tm, NCHIP),
        in_specs=[pl.BlockSpec((tm, K), lambda i, s: (i, 0)),
                  pl.BlockSpec((None, K, 2 * FB), lambda i, s: (s, 0, ob))],
        out_specs=[pl.BlockSpec((tm, 2 * FB), lambda i, s: (i, s)), pl.BlockSpec((tm, FB), lambda i, s: (i, s))],
        out_shape=[jax.ShapeDtypeStruct((T, NCHIP * 2 * FB), BF16), jax.ShapeDtypeStruct((T, NCHIP * FB), BF16)],
        compiler_params=_cp(("parallel", "arbitrary")))(xn, GA)


def mm_rs_post(a, G, off, rb, tk, h, g, scale, name):
    T = a.shape[0]
    N = G.shape[2]
    tm = min(512, T)
    nkk, ob = rb // tk, off // tk
    nk = NCHIP * nkk
    assert rb % tk == 0 and off % tk == 0

    def body(a_ref, w_ref, h_ref, g_ref, f_ref, o_ref, acc):
        k = pl.program_id(1)

        @pl.when(k == 0)
        def _():
            acc[...] = jnp.zeros_like(acc)

        acc[...] += _dot_nn(a_ref[...], w_ref[...])

        @pl.when(k == nk - 1)
        def _():
            f = acc[...]
            f_ref[...] = f
            r = lax.rsqrt(jnp.mean(f * f, axis=-1, keepdims=True) + EPS)
            o_ref[...] = h_ref[...] + scale * (f * r * g_ref[...])

    row = pl.BlockSpec((tm, N), lambda i, k: (i, 0))
    return pl.pallas_call(
        body, name=name, grid=(T // tm, nk),
        in_specs=[pl.BlockSpec((tm, tk), lambda i, k: (i, k)),
                  pl.BlockSpec((None, tk, N), lambda i, k: (k // nkk, ob + k % nkk, 0)),
                  row, pl.BlockSpec((1, N), lambda i, k: (0, 0))],
        out_specs=[row, row],
        out_shape=[jax.ShapeDtypeStruct((T, N), F32), jax.ShapeDtypeStruct((T, N), F32)],
        scratch_shapes=[pltpu.VMEM((tm, N), F32)],
        compiler_params=_cp(("parallel", "arbitrary")))(a, G, h, g)


def ff_bwd_down(dh, f, g, GB, down, gu, name):
    T, N = dh.shape
    tm = min(512, T)
    ob = down // FB

    def body(d_ref, f_ref, g_ref, w_ref, gu_ref, df_ref, dg_ref, dgu_ref):
        i, s = pl.program_id(0), pl.program_id(1)

        @pl.when(s == 0)
        def _():
            f = f_ref[...]
            r = lax.rsqrt(jnp.mean(f * f, axis=-1, keepdims=True) + EPS)
            d = 0.5 * d_ref[...]
            t = d * g_ref[...]
            df_ref[...] = (r * t - f * (r * r * r * jnp.mean(t * f, axis=-1, keepdims=True))).astype(df_ref.dtype)
            _acc_rows(dg_ref, i, jnp.sum(d * f * r, axis=0, keepdims=True))

        da = _dot_nt(df_ref[...], w_ref[...])
        gt = gu_ref[:, :FB].astype(F32)
        u = gu_ref[:, FB:].astype(F32)
        sg = _sig(gt)
        dgu_ref[:, :FB] = (da * u * (sg * (1.0 + gt * (1.0 - sg)))).astype(dgu_ref.dtype)
        dgu_ref[:, FB:] = (da * (gt * sg)).astype(dgu_ref.dtype)

    row = pl.BlockSpec((tm, N), lambda i, s: (i, 0))
    vec = pl.BlockSpec((1, N), lambda i, s: (0, 0))
    return pl.pallas_call(
        body, name=name, grid=(T // tm, NCHIP),
        in_specs=[row, row, vec, pl.BlockSpec((None, FB, N), lambda i, s: (s, ob, 0)),
                  pl.BlockSpec((tm, 2 * FB), lambda i, s: (i, s))],
        out_specs=[row, vec, pl.BlockSpec((tm, 2 * FB), lambda i, s: (i, s))],
        out_shape=[jax.ShapeDtypeStruct((T, N), BF16), jax.ShapeDtypeStruct((1, N), F32),
                   jax.ShapeDtypeStruct((T, NCHIP * 2 * FB), BF16)],
        compiler_params=_cp(("arbitrary", "arbitrary")))(dh, f, g, GB, gu)


def mm_cs_t_rms(dy, G, off, nb, tn, h, g, skip, name):
    T = dy.shape[0]
    K = G.shape[1]
    tm = min(512, T)
    nj, ob = nb // tn, off // tn
    nk = NCHIP * nj
    assert nb % tn == 0 and off % tn == 0

    def body(dy_ref, w_ref, h_ref, g_ref, s_ref, o_ref, dg_ref, acc):
        i, k = pl.program_id(0), pl.program_id(1)

        @pl.when(k == 0)
        def _():
            acc[...] = jnp.zeros_like(acc)

        acc[...] += _dot_nt(dy_ref[...], w_ref[...])

        @pl.when(k == nk - 1)
        def _():
            d = acc[...]
            x = h_ref[...]
            r = lax.rsqrt(jnp.mean(x * x, axis=-1, keepdims=True) + EPS)
            xh = x * r
            t = d * g_ref[...]
            o_ref[...] = s_ref[...] + r * (t - xh * jnp.mean(t * xh, axis=-1, keepdims=True))
            _acc_rows(dg_ref, i, jnp.sum(d * xh, axis=0, keepdims=True))

    row = pl.BlockSpec((tm, K), lambda i, k: (i, 0))
    vec = pl.BlockSpec((1, K), lambda i, k: (0, 0))
    return pl.pallas_call(
        body, name=name, grid=(T // tm, nk),
        in_specs=[pl.BlockSpec((tm, tn), lambda i, k: (i, k)),
                  pl.BlockSpec((None, K, tn), lambda i, k: (k // nj, 0, ob + k % nj)), row, vec, row],
        out_specs=[row, vec],
        out_shape=[jax.ShapeDtypeStruct((T, K), F32), jax.ShapeDtypeStruct((1, K), F32)],
        scratch_shapes=[pltpu.VMEM((tm, K), F32)],
        compiler_params=_cp(("arbitrary", "arbitrary")))(dy, G, h, g, skip)


def _rows(tm, C):
    return pl.BlockSpec((tm, C), lambda i: (i, 0))


def _vec(C):
    return pl.BlockSpec((1, C), lambda i: (0, 0))


def _acc_rows(ref, i, val):
    @pl.when(i == 0)
    def _():
        ref[...] = val

    @pl.when(i > 0)
    def _():
        ref[...] += val


def rms_fwd(h, g, out_dtype, name):
    T, C = h.shape
    tm = min(512, T)

    def body(h_ref, g_ref, o_ref):
        x = h_ref[...]
        r = lax.rsqrt(jnp.mean(x * x, axis=-1, keepdims=True) + EPS)
        o_ref[...] = (x * r * g_ref[...]).astype(o_ref.dtype)

    return pl.pallas_call(
        body, name=name, grid=(T // tm,), in_specs=[_rows(tm, C), _vec(C)], out_specs=_rows(tm, C),
        out_shape=jax.ShapeDtypeStruct((T, C), out_dtype), compiler_params=_cp(("parallel",)))(h, g)


def rms_bwd(dxn, h, g, dh_skip, name):
    T, C = h.shape
    tm = min(512, T)

    def body(d_ref, h_ref, g_ref, s_ref, o_ref, dg_ref):
        i = pl.program_id(0)
        x = h_ref[...]
        r = lax.rsqrt(jnp.mean(x * x, axis=-1, keepdims=True) + EPS)
        xh = x * r
        d = d_ref[...].astype(F32)
        t = d * g_ref[...]
        o_ref[...] = s_ref[...] + r * (t - xh * jnp.mean(t * xh, axis=-1, keepdims=True))
        _acc_rows(dg_ref, i, jnp.sum(d * xh, axis=0, keepdims=True))

    return pl.pallas_call(
        body, name=name, grid=(T // tm,),
        in_specs=[_rows(tm, C), _rows(tm, C), _vec(C), _rows(tm, C)],
        out_specs=[_rows(tm, C), _vec(C)],
        out_shape=[jax.ShapeDtypeStruct((T, C), F32), jax.ShapeDtypeStruct((1, C), F32)],
        compiler_params=_cp(("arbitrary",)))(dxn, h, g, dh_skip)


def post_res(h, f, g, scale, name):
    T, C = h.shape
    tm = min(512, T)

    def body(h_ref, f_ref, g_ref, o_ref):
        f = f_ref[...]
        r = lax.rsqrt(jnp.mean(f * f, axis=-1, keepdims=True) + EPS)
        o_ref[...] = h_ref[...] + scale * (f * r * g_ref[...])

    return pl.pallas_call(
        body, name=name, grid=(T // tm,), in_specs=[_rows(tm, C), _rows(tm, C), _vec(C)],
        out_specs=_rows(tm, C), out_shape=jax.ShapeDtypeStruct((T, C), F32),
        compiler_params=_cp(("parallel",)))(h, f, g)


def post_res_bwd(dh, f, g, scale, out_dtype, name):
    T, C = dh.shape
    tm = min(512, T)

    def body(d_ref, f_ref, g_ref, o_ref, dg_ref):
        i = pl.program_id(0)
        f = f_ref[...]
        r = lax.rsqrt(jnp.mean(f * f, axis=-1, keepdims=True) + EPS)
        d = scale * d_ref[...]
        t = d * g_ref[...]
        o_ref[...] = (r * t - f * (r * r * r * jnp.mean(t * f, axis=-1, keepdims=True))).astype(o_ref.dtype)
        _acc_rows(dg_ref, i, jnp.sum(d * f * r, axis=0, keepdims=True))

    return pl.pallas_call(
        body, name=name, grid=(T // tm,), in_specs=[_rows(tm, C), _rows(tm, C), _vec(C)],
        out_specs=[_rows(tm, C), _vec(C)],
        out_shape=[jax.ShapeDtypeStruct((T, C), out_dtype), jax.ShapeDtypeStruct((1, C), F32)],
        compiler_params=_cp(("arbitrary",)))(dh, f, g)


def ff_act(gu, name):
    T = gu.shape[0]
    tm = min(512, T)

    def body(gu_ref, o_ref):
        g = gu_ref[:, :FB].astype(F32)
        u = gu_ref[:, FB:].astype(F32)
        o_ref[...] = (g * _sig(g) * u).astype(o_ref.dtype)

    return pl.pallas_call(
        body, name=name, grid=(T // tm, NCHIP),
        in_specs=[pl.BlockSpec((tm, 2 * FB), lambda i, s: (i, s))],
        out_specs=pl.BlockSpec((tm, FB), lambda i, s: (i, s)),
        out_shape=jax.ShapeDtypeStruct((T, NCHIP * FB), BF16),
        compiler_params=_cp(("parallel", "parallel")))(gu)


def ff_act_bwd(da, gu, name):
    T = gu.shape[0]
    tm = min(512, T)

    def body(da_ref, gu_ref, o_ref):
        g = gu_ref[:, :FB].astype(F32)
        u = gu_ref[:, FB:].astype(F32)
        da = da_ref[...].astype(F32)
        s = _sig(g)
        o_ref[:, :FB] = (da * u * (s * (1.0 + g * (1.0 - s)))).astype(o_ref.dtype)
        o_ref[:, FB:] = (da * (g * s)).astype(o_ref.dtype)

    return pl.pallas_call(
        body, name=name, grid=(T // tm, NCHIP),
        in_specs=[pl.BlockSpec((tm, FB), lambda i, s: (i, s)), pl.BlockSpec((tm, 2 * FB), lambda i, s: (i, s))],
        out_specs=pl.BlockSpec((tm, 2 * FB), lambda i, s: (i, s)),
        out_shape=jax.ShapeDtypeStruct((T, NCHIP * 2 * FB), BF16),
        compiler_params=_cp(("parallel", "parallel")))(da, gu)


def ple_post(h, zg, pe, g, name):
    T, C = h.shape
    tm = min(512, T)

    def body(h_ref, z_ref, p_ref, g_ref, o_ref):
        e = p_ref[...] * _sig(z_ref[...])
        r = lax.rsqrt(jnp.mean(e * e, axis=-1, keepdims=True) + EPS)
        o_ref[...] = h_ref[...] + e * r * g_ref[...]

    return pl.pallas_call(
        body, name=name, grid=(T // tm,), in_specs=[_rows(tm, C), _rows(tm, C), _rows(tm, C), _vec(C)],
        out_specs=_rows(tm, C), out_shape=jax.ShapeDtypeStruct((T, C), F32),
        compiler_params=_cp(("parallel",)))(h, zg, pe, g)


def ple_post_bwd(dh, zg, pe, g, name):
    T, C = dh.shape
    tm = min(512, T)

    def body(d_ref, z_ref, p_ref, g_ref, dz_ref, dp_ref, dg_ref):
        i = pl.program_id(0)
        s = _sig(z_ref[...])
        pe_ = p_ref[...]
        e = pe_ * s
        r = lax.rsqrt(jnp.mean(e * e, axis=-1, keepdims=True) + EPS)
        d = d_ref[...]
        t = d * g_ref[...]
        de = r * t - e * (r * r * r * jnp.mean(t * e, axis=-1, keepdims=True))
        dp_ref[...] = (de * s).astype(dp_ref.dtype)
        dz_ref[...] = (de * pe_ * s * (1.0 - s)).astype(dz_ref.dtype)
        _acc_rows(dg_ref, i, jnp.sum(d * e * r, axis=0, keepdims=True))

    return pl.pallas_call(
        body, name=name, grid=(T // tm,), in_specs=[_rows(tm, C), _rows(tm, C), _rows(tm, C), _vec(C)],
        out_specs=[_rows(tm, C), _rows(tm, C), _vec(C)],
        out_shape=[jax.ShapeDtypeStruct((T, C), BF16), jax.ShapeDtypeStruct((T, C), BF16),
                   jax.ShapeDtypeStruct((1, C), F32)],
        compiler_params=_cp(("arbitrary",)))(dh, zg, pe, g)


def loss_head(h, tgt, name):
    T, C = h.shape
    tm = min(512, T)

    def body(h_ref, t_ref, d_ref, l_ref):
        i = pl.program_id(0)
        e = h_ref[...] - t_ref[...]
        d_ref[...] = e * (1.0 / C)
        _acc_rows(l_ref, i, jnp.sum(e * e, axis=0, keepdims=True))

    return pl.pallas_call(
        body, name=name, grid=(T // tm,), in_specs=[_rows(tm, C), _rows(tm, C)],
        out_specs=[_rows(tm, C), _vec(C)],
        out_shape=[jax.ShapeDtypeStruct((T, C), F32), jax.ShapeDtypeStruct((1, C), F32)],
        compiler_params=_cp(("arbitrary",)))(h, tgt)


AH, AG_N, AGW, CHUNK = 3072, 12, 256, 128


def _tril_bf16(w):
    r = lax.broadcasted_iota(jnp.int32, (CHUNK, CHUNK), 0)
    c = lax.broadcasted_iota(jnp.int32, (CHUNK, CHUNK), 1)
    return jnp.where(r >= c, w, 0.0).astype(BF16)


def _ln_stats(vs_ref, width):
    v = vs_ref[...]
    mu = jnp.sum(v, axis=-1, keepdims=True) * (1.0 / width)
    vc = v - mu
    var = jnp.sum(vc * vc, axis=-1, keepdims=True) * (1.0 / width)
    return mu, lax.rsqrt(var + EPS)


def gmlp_mid_fwd(zpre, vg, vb, ws, bsf, name):
    T = zpre.shape[0]

    def body(z_ref, vg_ref, vb_ref, ws_ref, bs_ref, y_ref, vs_ref):
        for g in range(AG_N):
            vs_ref[:, g * AGW:(g + 1) * AGW] = _gelu(z_ref[:, AH + g * AGW:AH + (g + 1) * AGW].astype(F32))
        mu, rstd = _ln_stats(vs_ref, AH)
        for g in range(AG_N):
            sl = slice(g * AGW, (g + 1) * AGW)
            vn = ((vs_ref[:, sl] - mu) * rstd * vg_ref[:, sl] + vb_ref[:, sl]).astype(BF16)
            sv = _dot_nn(_tril_bf16(ws_ref[g]), vn) + bs_ref[g]
            u = _gelu(z_ref[:, sl].astype(F32))
            y_ref[:, sl] = (u * sv).astype(y_ref.dtype)

    return pl.pallas_call(
        body, name=name, grid=(T // CHUNK,),
        in_specs=[_rows(CHUNK, 2 * AH), _vec(AH), _vec(AH),
                  pl.BlockSpec((AG_N, CHUNK, CHUNK), lambda i: (0, 0, 0)),
                  pl.BlockSpec((AG_N, CHUNK, AGW), lambda i: (0, 0, 0))],
        out_specs=_rows(CHUNK, AH), out_shape=jax.ShapeDtypeStruct((T, AH), BF16),
        scratch_shapes=[pltpu.VMEM((CHUNK, AH), F32)],
        compiler_params=_cp(("parallel",)))(zpre, vg, vb, ws, bsf)


def gmlp_mid_bwd(zpre, dy, vg, vb, ws, bsf, name):
    T = zpre.shape[0]

    def body(z_ref, dy_ref, vg_ref, vb_ref, ws_ref, bs_ref, dz_ref, dws_ref, dbs_ref, dvg_ref, dvb_ref,
             vs_ref, dvn_ref):
        i = pl.program_id(0)

        @pl.when(i == 0)
        def _():
            dws_ref[...] = jnp.zeros_like(dws_ref)
            dbs_ref[...] = jnp.zeros_like(dbs_ref)
            dvg_ref[...] = jnp.zeros_like(dvg_ref)
            dvb_ref[...] = jnp.zeros_like(dvb_ref)

        for g in range(AG_N):
            vs_ref[:, g * AGW:(g + 1) * AGW] = _gelu(z_ref[:, AH + g * AGW:AH + (g + 1) * AGW].astype(F32))
        mu, rstd = _ln_stats(vs_ref, AH)
        r_i = lax.broadcasted_iota(jnp.int32, (CHUNK, CHUNK), 0)
        c_i = lax.broadcasted_iota(jnp.int32, (CHUNK, CHUNK), 1)
        ones8 = jnp.ones((8, AGW), F32)
        m1 = jnp.zeros((CHUNK, 1), F32)
        m2 = jnp.zeros((CHUNK, 1), F32)
        for g in range(AG_N):
            sl = slice(g * AGW, (g + 1) * AGW)
            vh = (vs_ref[:, sl] - mu) * rstd
            vn = (vh * vg_ref[:, sl] + vb_ref[:, sl]).astype(BF16)
            wm = _tril_bf16(ws_ref[g])
            sv = _dot_nn(wm, vn) + bs_ref[g]
            zu = z_ref[:, sl].astype(F32)
            u = _gelu(zu)
            dyg = dy_ref[:, sl].astype(F32)
            dz_ref[:, sl] = (dyg * sv * _gelu_grad(zu)).astype(dz_ref.dtype)
            dsv = dyg * u
            dsv_b = dsv.astype(BF16)
            dws_ref[g] += jnp.where(r_i >= c_i, _dot_nt(dsv_b, vn), 0.0)
            dbs_ref[g] += _dot_nt(ones8, dsv)
            dvn = _dot_tn(wm, dsv_b)
            dvn_ref[:, sl] = dvn
            dvh = dvn * vg_ref[:, sl]
            m1 = m1 + jnp.sum(dvh, axis=-1, keepdims=True)
            m2 = m2 + jnp.sum(dvh * vh, axis=-1, keepdims=True)
            dvg_ref[:, sl] += jnp.sum(dvn * vh, axis=0, keepdims=True)
            dvb_ref[:, sl] += jnp.sum(dvn, axis=0, keepdims=True)
        m1 = m1 * (1.0 / AH)
        m2 = m2 * (1.0 / AH)
        for g in range(AG_N):
            sl = slice(g * AGW, (g + 1) * AGW)
            vh = (vs_ref[:, sl] - mu) * rstd
            dv = rstd * (dvn_ref[:, sl] * vg_ref[:, sl] - m1 - vh * m2)
            zv = z_ref[:, AH + g * AGW:AH + (g + 1) * AGW].astype(F32)
            dz_ref[:, AH + g * AGW:AH + (g + 1) * AGW] = (dv * _gelu_grad(zv)).astype(dz_ref.dtype)

    full3 = lambda a, b, c: pl.BlockSpec((a, b, c), lambda i: (0, 0, 0))
    return pl.pallas_call(
        body, name=name, grid=(T // CHUNK,),
        in_specs=[_rows(CHUNK, 2 * AH), _rows(CHUNK, AH), _vec(AH), _vec(AH),
                  full3(AG_N, CHUNK, CHUNK), full3(AG_N, CHUNK, AGW)],
        out_specs=[_rows(CHUNK, 2 * AH), full3(AG_N, CHUNK, CHUNK), full3(AG_N, 8, CHUNK), _vec(AH), _vec(AH)],
        out_shape=[jax.ShapeDtypeStruct((T, 2 * AH), BF16), jax.ShapeDtypeStruct((AG_N, CHUNK, CHUNK), F32),
                   jax.ShapeDtypeStruct((AG_N, 8, CHUNK), F32), jax.ShapeDtypeStruct((1, AH), F32),
                   jax.ShapeDtypeStruct((1, AH), F32)],
        scratch_shapes=[pltpu.VMEM((CHUNK, AH), F32), pltpu.VMEM((CHUNK, AH), F32)],
        compiler_params=_cp(("arbitrary",)))(zpre, dy, vg, vb, ws, bsf)


SLAB = 256
NSLAB = DM // SLAB
RC = 256
PAD = 32


def _col(T, j):
    return pl.BlockSpec((T, SLAB), lambda c: (0, j * NSLAB + c))


def _chunks(T, fn):
    def step(i, carry):
        fn(pl.multiple_of(i * RC, RC))
        return carry
    lax.fori_loop(0, T // RC, step, 0)


def _conv_taps(K):
    return [(r, [q for q in range(4) if 8 * q + r < K]) for r in range(min(8, K))]


def _causal_conv(zpad_ref, wrow, K, r0):
    acc = None
    for r, qs in _conv_taps(K):
        a = None
        for q in qs:
            term = wrow(8 * q + r) * zpad_ref[pl.ds(r0 + (PAD - 8 - 8 * q), RC + 8), :]
            a = term if a is None else a + term
        a = a if r == 0 else pltpu.roll(a, r, 0)
        acc = a if acc is None else acc + a
    return acc[8:, :]


def _anticausal_conv(gpad_ref, wrow, K, r0):
    acc = None
    for r, qs in _conv_taps(K):
        b = None
        for q in qs:
            term = wrow(8 * q + r) * gpad_ref[pl.ds(r0 + 8 * q, RC + 8), :]
            b = term if b is None else b + term
        b = b if r == 0 else pltpu.roll(b, RC + 8 - r, 0)
        acc = b if acc is None else acc + b
    return acc[:RC, :]


def _conv_dw(gpad_ref, zpad_ref, dw_ref, K, r0):
    for r, qs in _conv_taps(K):
        gw = gpad_ref[pl.ds(r0, RC + 8), :]
        p = (gw if r == 0 else pltpu.roll(gw, RC + 8 - r, 0))[:RC, :]
        for q in qs:
            z = zpad_ref[pl.ds(r0 + (PAD - 8 * q), RC), :]
            dw_ref[8 * q + r] += jnp.sum((p * z).reshape(RC // 8, 8, SLAB), axis=0)


def _zero_rows(ref, start, n):
    ref[pl.ds(start, n), :] = jnp.zeros((n, SLAB), F32)


def pool_fwd(hn, wg, sc, name):
    T = hn.shape[0]

    def body(h_ref, w_ref, s_ref, p_ref, yp_ref, y_ref, xpad):
        g = pl.program_id(0)
        wf = jnp.left_shift(2, g).astype(F32)
        _zero_rows(xpad, 0, PAD)

        def fill(r0):
            xpad[pl.ds(r0 + PAD, RC), :] = h_ref[pl.ds(r0, RC), :]
        _chunks(T, fill)

        def step(r0):
            w = xpad[pl.ds(r0 + (PAD - 16), RC + 16), :]
            s2 = w + pltpu.roll(w, 1, 0)
            s4 = s2 + pltpu.roll(s2, 2, 0)
            s8 = s4 + pltpu.roll(s4, 4, 0)
            s16 = s8 + pltpu.roll(s8, 8, 0)
            sel = jnp.where(g == 0, s2, jnp.where(g == 1, s4, jnp.where(g == 2, s8, s16)))[16:, :]
            t1 = (r0 + 1 + lax.broadcasted_iota(jnp.int32, (RC, SLAB), 0)).astype(F32)
            pooled = (sel / jnp.minimum(t1, wf) - w[16:, :]).astype(BF16)
            p_ref[pl.ds(r0, RC), :] = pooled
            yp = _dot_nn(pooled, w_ref[...])
            yp_ref[pl.ds(r0, RC), :] = yp
            y_ref[pl.ds(r0, RC), :] = yp * s_ref[...]
        _chunks(T, step)

    slab = pl.BlockSpec((T, SLAB), lambda c: (0, c))
    return pl.pallas_call(
        body, name=name, grid=(NSLAB,),
        in_specs=[slab, pl.BlockSpec((None, SLAB, SLAB), lambda c: (c, 0, 0)), pl.BlockSpec((1, SLAB), lambda c: (0, c))],
        out_specs=[slab, slab, slab],
        out_shape=[jax.ShapeDtypeStruct((T, DM), BF16), jax.ShapeDtypeStruct((T, DM), F32),
                   jax.ShapeDtypeStruct((T, DM), F32)],
        scratch_shapes=[pltpu.VMEM((T + PAD, SLAB), F32)],
        compiler_params=_cp(("parallel",)))(hn, wg, sc)


def pool_bwd(dy, ypre, pooled, wg, sc, name):
    T = dy.shape[0]

    def body(d_ref, yp_ref, p_ref, w_ref, s_ref, dh_ref, dw_ref, ds_ref, qpad, dwacc, dsacc):
        g = pl.program_id(0)
        wf = jnp.left_shift(2, g).astype(F32)
        dwacc[...] = jnp.zeros_like(dwacc)
        dsacc[...] = jnp.zeros_like(dsacc)
        _zero_rows(qpad, T, PAD)

        def first(r0):
            d = d_ref[pl.ds(r0, RC), :]
            dsacc[...] += jnp.sum((d * yp_ref[pl.ds(r0, RC), :]).reshape(RC // 8, 8, SLAB), axis=0)
            dyp = (d * s_ref[...]).astype(BF16)
            dpool = _dot_nt(dyp, w_ref[...])
            dwacc[...] += _dot_tn(p_ref[pl.ds(r0, RC), :], dyp)
            t1 = (r0 + 1 + lax.broadcasted_iota(jnp.int32, (RC, SLAB), 0)).astype(F32)
            qpad[pl.ds(r0, RC), :] = dpool / jnp.minimum(t1, wf)
            dh_ref[pl.ds(r0, RC), :] = dpool
        _chunks(T, first)

        def second(r0):
            w = qpad[pl.ds(r0, RC + 16), :]
            n = RC + 16
            a2 = w + pltpu.roll(w, n - 1, 0)
            a4 = a2 + pltpu.roll(a2, n - 2, 0)
            a8 = a4 + pltpu.roll(a4, n - 4, 0)
            a16 = a8 + pltpu.roll(a8, n - 8, 0)
            sel = jnp.where(g == 0, a2, jnp.where(g == 1, a4, jnp.where(g == 2, a8, a16)))[:RC, :]
            dh_ref[pl.ds(r0, RC), :] = sel - dh_ref[pl.ds(r0, RC), :]
        _chunks(T, second)
        dw_ref[...] = dwacc[...]
        ds_ref[...] = jnp.sum(dsacc[...], axis=0, keepdims=True)

    slab = pl.BlockSpec((T, SLAB), lambda c: (0, c))
    wspec = pl.BlockSpec((None, SLAB, SLAB), lambda c: (c, 0, 0))
    vec = pl.BlockSpec((1, SLAB), lambda c: (0, c))
    return pl.pallas_call(
        body, name=name, grid=(NSLAB,),
        in_specs=[slab, slab, slab, wspec, vec],
        out_specs=[slab, wspec, vec],
        out_shape=[jax.ShapeDtypeStruct((T, DM), F32), jax.ShapeDtypeStruct((NSLAB, SLAB, SLAB), F32),
                   jax.ShapeDtypeStruct((1, DM), F32)],
        scratch_shapes=[pltpu.VMEM((T + PAD, SLAB), F32), pltpu.VMEM((SLAB, SLAB), F32), pltpu.VMEM((8, SLAB), F32)],
        compiler_params=_cp(("parallel",)))(dy, ypre, pooled, wg, sc)


KC = 31
KD = 3


def conf_conv_fwd(ag, wdw, bdw, name):
    T = ag.shape[0]

    def body(a_ref, g_ref, w_ref, b_ref, o_ref, zpad):
        _zero_rows(zpad, 0, PAD)

        def fill(r0):
            a = a_ref[pl.ds(r0, RC), :].astype(F32)
            gt = g_ref[pl.ds(r0, RC), :].astype(F32)
            zpad[pl.ds(r0 + PAD, RC), :] = a * _sig(gt)
        _chunks(T, fill)
        wrow = lambda j: w_ref[KC - 1 - j:KC - j, :]

        def step(r0):
            o_ref[pl.ds(r0, RC), :] = _causal_conv(zpad, wrow, KC, r0) + b_ref[...]
        _chunks(T, step)

    vec = pl.BlockSpec((1, SLAB), lambda c: (0, c))
    return pl.pallas_call(
        body, name=name, grid=(NSLAB,),
        in_specs=[_col(T, 0), _col(T, 1), pl.BlockSpec((32, SLAB), lambda c: (0, c)), vec],
        out_specs=pl.BlockSpec((T, SLAB), lambda c: (0, c)),
        out_shape=jax.ShapeDtypeStruct((T, DM), F32),
        scratch_shapes=[pltpu.VMEM((T + PAD, SLAB), F32)],
        compiler_params=_cp(("parallel",)))(ag, ag, wdw, bdw)


def conf_conv_bwd(dzc, ag, wdw, name):
    T = ag.shape[0]

    def body(d_ref, a_ref, g_ref, w_ref, da_ref, dg_ref, dw_ref, db_ref, zpad, gpad, dwacc, dbacc):
        _zero_rows(zpad, 0, PAD)
        _zero_rows(gpad, T, PAD)
        dwacc[...] = jnp.zeros_like(dwacc)
        dbacc[...] = jnp.zeros_like(dbacc)

        def fill(r0):
            a = a_ref[pl.ds(r0, RC), :].astype(F32)
            gt = g_ref[pl.ds(r0, RC), :].astype(F32)
            zpad[pl.ds(r0 + PAD, RC), :] = a * _sig(gt)
            d = d_ref[pl.ds(r0, RC), :]
            gpad[pl.ds(r0, RC), :] = d
            dbacc[...] += jnp.sum(d.reshape(RC // 8, 8, SLAB), axis=0)
        _chunks(T, fill)
        wrow = lambda j: w_ref[KC - 1 - j:KC - j, :]

        def step(r0):
            dz = _anticausal_conv(gpad, wrow, KC, r0)
            a = a_ref[pl.ds(r0, RC), :].astype(F32)
            s = _sig(g_ref[pl.ds(r0, RC), :].astype(F32))
            da_ref[pl.ds(r0, RC), :] = (dz * s).astype(da_ref.dtype)
            dg_ref[pl.ds(r0, RC), :] = (dz * a * s * (1.0 - s)).astype(dg_ref.dtype)
            _conv_dw(gpad, zpad, dwacc, KC, r0)
        _chunks(T, step)
        dw_ref[...] = jnp.zeros_like(dw_ref)
        for k in range(KC):
            dw_ref[k:k + 1, :] = jnp.sum(dwacc[KC - 1 - k], axis=0, keepdims=True)
        db_ref[...] = jnp.sum(dbacc[...], axis=0, keepdims=True)

    vec = pl.BlockSpec((1, SLAB), lambda c: (0, c))
    w32 = pl.BlockSpec((32, SLAB), lambda c: (0, c))
    return pl.pallas_call(
        body, name=name, grid=(NSLAB,),
        in_specs=[pl.BlockSpec((T, SLAB), lambda c: (0, c)), _col(T, 0), _col(T, 1), w32],
        out_specs=[_col(T, 0), _col(T, 0), w32, vec],
        out_shape=[jax.ShapeDtypeStruct((T, DM), BF16), jax.ShapeDtypeStruct((T, DM), BF16),
                   jax.ShapeDtypeStruct((32, DM), F32), jax.ShapeDtypeStruct((1, DM), F32)],
        scratch_shapes=[pltpu.VMEM((T + PAD, SLAB), F32), pltpu.VMEM((T + PAD, SLAB), F32),
                        pltpu.VMEM((32, 8, SLAB), F32), pltpu.VMEM((8, SLAB), F32)],
        compiler_params=_cp(("parallel",)))(dzc, ag, ag, wdw)


def conf_ln_fwd(zc, g, b, name):
    T, C = zc.shape
    tm = min(512, T)

    def body(z_ref, g_ref, b_ref, o_ref):
        x = z_ref[...]
        xc = x - jnp.mean(x, axis=-1, keepdims=True)
        r = lax.rsqrt(jnp.mean(xc * xc, axis=-1, keepdims=True) + EPS)
        zl = xc * r * g_ref[...] + b_ref[...]
        o_ref[...] = (zl * _sig(zl)).astype(o_ref.dtype)

    return pl.pallas_call(
        body, name=name, grid=(T // tm,), in_specs=[_rows(tm, C), _vec(C), _vec(C)], out_specs=_rows(tm, C),
        out_shape=jax.ShapeDtypeStruct((T, C), BF16), compiler_params=_cp(("parallel",)))(zc, g, b)


def conf_ln_bwd(dzs, zc, g, b, name):
    T, C = zc.shape
    tm = min(512, T)

    def body(d_ref, z_ref, g_ref, b_ref, o_ref, dg_ref, db_ref):
        i = pl.program_id(0)
        x = z_ref[...]
        xc = x - jnp.mean(x, axis=-1, keepdims=True)
        r = lax.rsqrt(jnp.mean(xc * xc, axis=-1, keepdims=True) + EPS)
        xh = xc * r
        zl = xh * g_ref[...] + b_ref[...]
        s = _sig(zl)
        dzl = d_ref[...].astype(F32) * (s * (1.0 + zl * (1.0 - s)))
        t = dzl * g_ref[...]
        o_ref[...] = r * (t - jnp.mean(t, axis=-1, keepdims=True) - xh * jnp.mean(t * xh, axis=-1, keepdims=True))
        _acc_rows(dg_ref, i, jnp.sum(dzl * xh, axis=0, keepdims=True))
        _acc_rows(db_ref, i, jnp.sum(dzl, axis=0, keepdims=True))

    return pl.pallas_call(
        body, name=name, grid=(T // tm,), in_specs=[_rows(tm, C), _rows(tm, C), _vec(C), _vec(C)],
        out_specs=[_rows(tm, C), _vec(C), _vec(C)],
        out_shape=[jax.ShapeDtypeStruct((T, C), F32), jax.ShapeDtypeStruct((1, C), F32),
                   jax.ShapeDtypeStruct((1, C), F32)],
        compiler_params=_cp(("arbitrary",)))(dzs, zc, g, b)


def sconv_fwd(bgx, wc, name):
    T = bgx.shape[0]

    def body(b_ref, c_ref, x_ref, w_ref, o_ref, zpad):
        _zero_rows(zpad, 0, PAD)

        def fill(r0):
            zpad[pl.ds(r0 + PAD, RC), :] = c_ref[pl.ds(r0, RC), :].astype(F32) * x_ref[pl.ds(r0, RC), :].astype(F32)
        _chunks(T, fill)
        wrow = lambda j: w_ref[KD - 1 - j:KD - j, :]

        def step(r0):
            qc = _causal_conv(zpad, wrow, KD, r0)
            o_ref[pl.ds(r0, RC), :] = (b_ref[pl.ds(r0, RC), :].astype(F32) * qc).astype(o_ref.dtype)
        _chunks(T, step)

    return pl.pallas_call(
        body, name=name, grid=(NSLAB,),
        in_specs=[_col(T, 0), _col(T, 1), _col(T, 2), pl.BlockSpec((8, SLAB), lambda c: (0, c))],
        out_specs=pl.BlockSpec((T, SLAB), lambda c: (0, c)),
        out_shape=jax.ShapeDtypeStruct((T, DM), BF16),
        scratch_shapes=[pltpu.VMEM((T + PAD, SLAB), F32)],
        compiler_params=_cp(("parallel",)))(bgx, bgx, bgx, wc)


def sconv_bwd(dy, bgx, wc, name):
    T = bgx.shape[0]

    def body(d_ref, b_ref, c_ref, x_ref, w_ref, db_ref, dc_ref, dx_ref, dw_ref, zpad, gpad, dwacc):
        _zero_rows(zpad, 0, PAD)
        _zero_rows(gpad, T, PAD)
        dwacc[...] = jnp.zeros_like(dwacc)

        def fill(r0):
            zpad[pl.ds(r0 + PAD, RC), :] = c_ref[pl.ds(r0, RC), :].astype(F32) * x_ref[pl.ds(r0, RC), :].astype(F32)
            gpad[pl.ds(r0, RC), :] = d_ref[pl.ds(r0, RC), :].astype(F32) * b_ref[pl.ds(r0, RC), :].astype(F32)
        _chunks(T, fill)
        wrow = lambda j: w_ref[KD - 1 - j:KD - j, :]

        def step(r0):
            qc = _causal_conv(zpad, wrow, KD, r0)
            db_ref[pl.ds(r0, RC), :] = (d_ref[pl.ds(r0, RC), :].astype(F32) * qc).astype(db_ref.dtype)
            dq = _anticausal_conv(gpad, wrow, KD, r0)
            dc_ref[pl.ds(r0, RC), :] = (dq * x_ref[pl.ds(r0, RC), :].astype(F32)).astype(dc_ref.dtype)
            dx_ref[pl.ds(r0, RC), :] = (dq * c_ref[pl.ds(r0, RC), :].astype(F32)).astype(dx_ref.dtype)
            _conv_dw(gpad, zpad, dwacc, KD, r0)
        _chunks(T, step)
        dw_ref[...] = jnp.zeros_like(dw_ref)
        for k in range(KD):
            dw_ref[k:k + 1, :] = jnp.sum(dwacc[KD - 1 - k], axis=0, keepdims=True)

    w8 = pl.BlockSpec((8, SLAB), lambda c: (0, c))
    return pl.pallas_call(
        body, name=name, grid=(NSLAB,),
        in_specs=[pl.BlockSpec((T, SLAB), lambda c: (0, c)), _col(T, 0), _col(T, 1), _col(T, 2), w8],
        out_specs=[_col(T, 0), _col(T, 0), _col(T, 0), w8],
        out_shape=[jax.ShapeDtypeStruct((T, DM), BF16)] * 3 + [jax.ShapeDtypeStruct((8, DM), F32)],
        scratch_shapes=[pltpu.VMEM((T + PAD, SLAB), F32), pltpu.VMEM((T + PAD, SLAB), F32),
                        pltpu.VMEM((8, 8, SLAB), F32)],
        compiler_params=_cp(("parallel",)))(dy, bgx, bgx, bgx, wc)


def merge_cols(parts, name):
    T = parts[0].shape[0]
    n = len(parts)
    C = n * DM
    tm = min(512, T)

    def body(*refs):
        o_ref = refs[n]
        for j in range(n):
            o_ref[:, j * DM:(j + 1) * DM] = refs[j][...]

    return pl.pallas_call(
        body, name=name, grid=(T // tm,),
        in_specs=[_rows(tm, DM) for j in range(n)],
        out_specs=_rows(tm, C), out_shape=jax.ShapeDtypeStruct((T, C), parts[0].dtype),
        compiler_params=_cp(("parallel",)))(*parts)


def adamw(w, g, m, v, name):
    shape = w.shape
    C = shape[-1]
    R = w.size // C
    w2, g2, m2, v2 = (a.reshape(R, C) for a in (w, g, m, v))
    tr = R
    while tr * C > 512 * 1024 and tr % 16 == 0:
        tr //= 2
    bc1 = 1.0 - ADAM_B1 ** ADAM_STEP
    bc2 = 1.0 - ADAM_B2 ** ADAM_STEP

    def body(w_ref, g_ref, m_ref, v_ref, d_ref, nm_ref, nv_ref):
        gg = g_ref[...]
        nm = ADAM_B1 * m_ref[...] + (1.0 - ADAM_B1) * gg
        nv = ADAM_B2 * v_ref[...] + (1.0 - ADAM_B2) * (gg * gg)
        nm_ref[...] = nm
        nv_ref[...] = nv
        d_ref[...] = -ADAM_LR * ((nm / bc1) / (jnp.sqrt(nv / bc2) + ADAM_EPS) + ADAM_WD * w_ref[...])

    spec = pl.BlockSpec((tr, C), lambda i: (i, 0))
    outs = pl.pallas_call(
        body, name=name, grid=(R // tr,), in_specs=[spec] * 4, out_specs=[spec] * 3,
        out_shape=[jax.ShapeDtypeStruct((R, C), F32)] * 3, compiler_params=_cp(("parallel",)))(w2, g2, m2, v2)
    return tuple(o.reshape(shape) for o in outs)


def _place():
    x, y, c = lax.axis_index("x"), lax.axis_index("y"), lax.axis_index("c")
    return x, y, c


def all_gather_chips(bufs, name):
    n = len(bufs)
    me_ = 2 * lax.axis_index("x") + lax.axis_index("y")
    slots = [lax.dynamic_update_slice(lax.empty((NCHIP,) + b.shape, b.dtype), b[None], (me_, 0, 0)) for b in bufs]

    def body(*refs):
        dst = refs[n:2 * n]
        send, recv = refs[2 * n:]
        x, y, c = _place()
        me = 2 * x + y
        sib = (x, y, 1 - c)
        chips = [(1 - x, y), (x, 1 - y), (1 - x, 1 - y)]

        def half(b, slot, hc):
            rows = bufs[b].shape[0] // 2
            return dst[b].at[slot, pl.ds(hc * rows, rows), :]

        def remote(k, s, d, to):
            return pltpu.make_async_remote_copy(src_ref=s, dst_ref=d, send_sem=send.at[k], recv_sem=recv.at[k],
                                                device_id=to, device_id_type=MESH)

        first = []
        for b in range(n):
            for j, (cx, cy) in enumerate(chips):
                first.append(remote(b * 6 + j, half(b, me, c), half(b, me, c), (cx, cy, c)))
        for cp in first:
            cp.start()
        passed = []
        for b in range(n):
            for j, (cx, cy) in enumerate(chips):
                slot = 2 * cx + cy
                remote(b * 6 + j, half(b, slot, c), half(b, slot, c), (cx, cy, c)).wait_recv()
                fwd = remote(b * 6 + 3 + j, half(b, slot, c), half(b, slot, c), sib)
                fwd.start()
                passed.append(fwd)
        for b in range(n):
            for j, (cx, cy) in enumerate(chips):
                slot = 2 * cx + cy
                remote(b * 6 + 3 + j, half(b, slot, 1 - c), half(b, slot, 1 - c), sib).wait_recv()
        for cp in first + passed:
            cp.wait_send()

    return pl.pallas_call(
        body, name=name, in_specs=[ANY] * n, out_specs=[ANY] * n,
        out_shape=[jax.ShapeDtypeStruct(s.shape, s.dtype) for s in slots],
        input_output_aliases={b: b for b in range(n)},
        scratch_shapes=[pltpu.SemaphoreType.DMA((6 * n,)), pltpu.SemaphoreType.DMA((6 * n,))],
        compiler_params=pltpu.CompilerParams())(*slots)


def pair_exchange(bufs, name):
    n = len(bufs)

    def body(*refs):
        src, dst = refs[:n], refs[n:2 * n]
        send, recv = refs[2 * n:]
        x, y, c = _place()
        cps = []
        for b in range(n):
            rows = bufs[b].shape[1] // 2
            cp = pltpu.make_async_remote_copy(
                src_ref=src[b].at[:, pl.ds((1 - c) * rows, rows), :], dst_ref=dst[b],
                send_sem=send.at[b], recv_sem=recv.at[b], device_id=(x, y, 1 - c), device_id_type=MESH)
            cp.start()
            cps.append(cp)
        for cp in cps:
            cp.wait()

    return pl.pallas_call(
        body, name=name, in_specs=[ANY] * n, out_specs=[ANY] * n,
        out_shape=[jax.ShapeDtypeStruct((NCHIP, b.shape[1] // 2, b.shape[2]), b.dtype) for b in bufs],
        scratch_shapes=[pltpu.SemaphoreType.DMA((n,)), pltpu.SemaphoreType.DMA((n,))],
        compiler_params=pltpu.CompilerParams())(*bufs)


def add_half(full, got, tr, tc, name):
    _, R, C = full.shape
    rows = R // 2
    nr = rows // tr
    c_arr = lax.axis_index("c").astype(jnp.int32).reshape(1)

    def body(c_ref, a_ref, b_ref, o_ref):
        o_ref[...] = (a_ref[...].astype(F32) + b_ref[...].astype(F32)).astype(o_ref.dtype)

    return pl.pallas_call(
        body, name=name,
        grid_spec=pltpu.PrefetchScalarGridSpec(
            num_scalar_prefetch=1, grid=(NCHIP, nr, C // tc),
            in_specs=[pl.BlockSpec((None, tr, tc), lambda s, i, j, c_ref: (s, c_ref[0] * nr + i, j)),
                      pl.BlockSpec((None, tr, tc), lambda s, i, j, c_ref: (s, i, j))],
            out_specs=pl.BlockSpec((None, tr, tc), lambda s, i, j, c_ref: (s, i, j))),
        out_shape=jax.ShapeDtypeStruct((NCHIP, rows, C), full.dtype),
        compiler_params=_cp(("parallel", "parallel", "parallel")))(c_arr, full, got)


def chip_exchange(bufs, name):
    n = len(bufs)

    def body(*refs):
        src, dst = refs[:n], refs[n:2 * n]
        send, recv, lsem = refs[2 * n:]
        x, y, c = _place()
        me = 2 * x + y
        chips = [(1 - x, y), (x, 1 - y), (1 - x, 1 - y)]
        local = [pltpu.make_async_copy(src[b].at[me], dst[b].at[me], lsem.at[b]) for b in range(n)]
        for cp in local:
            cp.start()
        cps = []
        for b in range(n):
            for j, (cx, cy) in enumerate(chips):
                cp = pltpu.make_async_remote_copy(
                    src_ref=src[b].at[2 * cx + cy], dst_ref=dst[b].at[me],
                    send_sem=send.at[b * 3 + j], recv_sem=recv.at[b * 3 + j],
                    device_id=(cx, cy, c), device_id_type=MESH)
                cp.start()
                cps.append((cp, b, cx, cy, j))
        for cp, b, cx, cy, j in cps:
            cp.wait_send()
            pltpu.make_async_remote_copy(
                src_ref=src[b].at[me], dst_ref=dst[b].at[2 * cx + cy],
                send_sem=send.at[b * 3 + j], recv_sem=recv.at[b * 3 + j],
                device_id=(cx, cy, c), device_id_type=MESH).wait_recv()
        for cp in local:
            cp.wait()

    return pl.pallas_call(
        body, name=name, in_specs=[ANY] * n, out_specs=[ANY] * n,
        out_shape=[jax.ShapeDtypeStruct(b.shape, b.dtype) for b in bufs],
        scratch_shapes=[pltpu.SemaphoreType.DMA((3 * n,)), pltpu.SemaphoreType.DMA((3 * n,)),
                        pltpu.SemaphoreType.DMA((n,))],
        compiler_params=pltpu.CompilerParams())(*bufs)


def sum_slots(buf, tr, tc, name):
    _, r, C = buf.shape
    nr = r // tr
    c_arr = lax.axis_index("c").astype(jnp.int32).reshape(1)

    def body(c_ref, a_ref, o_ref):
        o_ref[...] = ((a_ref[0].astype(F32) + a_ref[1].astype(F32)) + a_ref[2].astype(F32)) + a_ref[3].astype(F32)

    return pl.pallas_call(
        body, name=name,
        grid_spec=pltpu.PrefetchScalarGridSpec(
            num_scalar_prefetch=1, grid=(nr, C // tc),
            in_specs=[pl.BlockSpec((NCHIP, tr, tc), lambda i, j, c_ref: (0, i, j))],
            out_specs=pl.BlockSpec((tr, tc), lambda i, j, c_ref: (c_ref[0] * nr + i, j))),
        out_shape=jax.ShapeDtypeStruct((2 * r, C), F32),
        compiler_params=_cp(("parallel", "parallel")))(c_arr, buf)


def pair_share(bufs, name):
    n = len(bufs)

    def body(*refs):
        dst = refs[n:2 * n]
        send, recv = refs[2 * n:]
        x, y, c = _place()
        cps = []
        for b in range(n):
            rows = bufs[b].shape[0] // 2
            here = dst[b].at[pl.ds(c * rows, rows), :]
            cp = pltpu.make_async_remote_copy(src_ref=here, dst_ref=here, send_sem=send.at[b], recv_sem=recv.at[b],
                                              device_id=(x, y, 1 - c), device_id_type=MESH)
            cp.start()
            cps.append((cp, b))
        for cp, b in cps:
            rows = bufs[b].shape[0] // 2
            there = dst[b].at[pl.ds((1 - c) * rows, rows), :]
            cp.wait_send()
            pltpu.make_async_remote_copy(src_ref=there, dst_ref=there, send_sem=send.at[b], recv_sem=recv.at[b],
                                         device_id=(x, y, 1 - c), device_id_type=MESH).wait_recv()

    return pl.pallas_call(
        body, name=name, in_specs=[ANY] * n, out_specs=[ANY] * n,
        out_shape=[jax.ShapeDtypeStruct(b.shape, b.dtype) for b in bufs],
        input_output_aliases={b: b for b in range(n)},
        scratch_shapes=[pltpu.SemaphoreType.DMA((n,)), pltpu.SemaphoreType.DMA((n,))],
        compiler_params=pltpu.CompilerParams())(*bufs)


_TILES = {"A": (64, A_COLS), "B": (128, 1024), "C": (640, 256), "V": (40, 256), "E": (40, 1024)}


def reduce_scatter(parts):
    names = list(parts)
    got = pair_exchange([parts[k] for k in names], "rs_pair_exchange")
    sums = [add_half(parts[k], got[i], *_TILES[k], name="rs_add_pair_" + k) for i, k in enumerate(names)]
    landed = chip_exchange(sums, "rs_chip_exchange")
    halves = [sum_slots(landed[i], *_TILES[k], name="rs_sum_chips_" + k) for i, k in enumerate(names)]
    full = pair_share(halves, "rs_pair_share")
    return dict(zip(names, full))


def _row(a, l):
    return a[l:l + 1]


def local_step(x, p, tgt, small, GA, GB, GC, vecs, a_ws, a_bs):
    T = x.shape[0]
    bgrp = GC[:, C_BGRP:C_BGRP + 256, :].reshape(NCHIP, 4, 64, 256).transpose(1, 0, 2, 3).reshape(4, 256, 256)
    bsf = jnp.broadcast_to(a_bs[:, :, None], (AG_N, CHUNK, AGW))
    vrow = lambda r: vecs[r:r + 1]
    saved = []
    h = x

    def ff_fwd(h, l, which, pre, post):
        j0 = 0 if which == 1 else 2
        down = B_FF1D(l) if which == 1 else B_FF2D(l)
        tag = "ff%d_l%d" % (which, l)
        xn = rms_fwd(h, _row(pre, l), BF16, tag + "_pre")
        gu, a = ff_gateup(xn, GA, A_FF(l, j0), tag + "_gateup")
        f, hn = mm_rs_post(a, GB, down, FB, FB, h, _row(post, l), 0.5, tag + "_down")
        return hn, (h, xn, gu, a, f)

    for l in range(4):
        rec = {}
        h, rec["ff1"] = ff_fwd(h, l, 1, small["ff1_pre_g"], small["ff1_post_g"])
        tag = "mix_l%d" % l
        h_in = h
        if l == 1:
            hn = rms_fwd(h, _row(small["mix_pre_g"], l), F32, tag + "_pre")
            pooled, ypre, f = pool_fwd(hn, bgrp, vrow(V_BSCALE), tag + "_pool")
            rec["mix"] = (h_in, pooled, ypre, f)
            h = post_res(h, f, _row(small["mix_post_g"], l), 1.0, tag + "_post")
        else:
            gpost = _row(small["mix_post_g"], l)
            hn = rms_fwd(h, _row(small["mix_pre_g"], l), BF16, tag + "_pre")
            if l == 0:
                zpre = mm_cs(hn, GA, A_AIN, 1536, 512, BF16, tag + "_in")
                y = gmlp_mid_fwd(zpre, small["a_v_norm_g"], small["a_v_norm_b"], a_ws, bsf, tag + "_gate")
                f, h = mm_rs_post(y, GB, B_AOUT, 768, 256, h, gpost, 1.0, tag + "_out")
                rec["mix"] = (h_in, hn, zpre, y, f)
            elif l == 2:
                ag = mm_cs(hn, GA, A_CIN, 512, 512, BF16, tag + "_pw1")
                zc = conf_conv_fwd(ag, vecs[V_CDW:V_CDW + 32], vrow(V_CBDW), tag + "_conv")
                zs = conf_ln_fwd(zc, vrow(V_CNG), vrow(V_CNB), tag + "_ln")
                f, h = mm_rs_post(zs, GB, B_CPW2, 256, 256, h, gpost, 1.0, tag + "_pw2")
                rec["mix"] = (h_in, hn, ag, zc, zs, f)
            else:
                bgx = mm_cs(hn, GA, A_DIN, 768, 256, BF16, tag + "_in")
                y = sconv_fwd(bgx, vecs[V_DCONV:V_DCONV + 8], tag + "_conv")
                f, h = mm_rs_post(y, GB, B_DOUT, 256, 256, h, gpost, 1.0, tag + "_out")
                rec["mix"] = (h_in, hn, bgx, y, f)
        h, rec["ff2"] = ff_fwd(h, l, 2, small["ff2_pre_g"], small["ff2_post_g"])
        tag = "ple_l%d" % l
        xn = rms_fwd(h, _row(small["ple_gate_norm_g"], l), BF16, tag + "_pre")
        zg = mm_rs(xn, GB, B_PLEG(l), 256, 256, tag + "_gate")
        pb = p[l].astype(BF16)
        pe = mm_cs(pb, GC, 0, 256, 256, F32, tag + "_proj", roff=C_PROJ(l))
        rec["ple"] = (h, xn, zg, pe, pb)
        h = ple_post(h, zg, pe, _row(small["ple_post_g"], l), tag + "_post")
        saved.append(rec)

    dh, loss_cols = loss_head(h, tgt, "loss_head")

    gA = {}
    gB = {}
    gC = {}
    gV = {}
    gains = {k: [None] * 4 for k in ("ff1_pre_g", "ff1_post_g", "mix_pre_g", "mix_post_g", "ff2_pre_g", "ff2_post_g",
                                      "ple_gate_norm_g", "ple_post_g")}
    extra = {}

    def ff_bwd(dh, l, which, pre, post, rec):
        j0 = 0 if which == 1 else 2
        down = B_FF1D(l) if which == 1 else B_FF2D(l)
        tag = "ff%d_l%d_b" % (which, l)
        h_in, xn, gu, a, f = rec
        df, dpost, dgu = ff_bwd_down(dh, f, _row(post, l), GB, down, gu, tag + "_down")
        gB[down] = dw_rs(a, df, FB, FB, tag + "_dwdown")
        dh_in, dpre = mm_cs_t_rms(dgu, GA, A_FF(l, j0), 2 * FB, 2 * FB, h_in, _row(pre, l), dh, tag + "_gateup")
        gA[A_FF(l, j0)] = dw_cs(xn, dgu, 2 * FB, FB, tag + "_dwgateup")
        return dh_in, dpre, dpost

    for l in reversed(range(4)):
        rec = saved[l]
        tag = "ple_l%d_b" % l
        h_in, xn, zg, pe, pb = rec["ple"]
        dzg, dpe, gains["ple_post_g"][l] = ple_post_bwd(dh, zg, pe, _row(small["ple_post_g"], l), tag + "_post")
        gC[C_PROJ(l)] = dw_cs(pb, dpe, 256, 256, tag + "_dwproj")
        dxn = mm_rs_t(dzg, GB, B_PLEG(l), 256, 256, tag + "_gate")
        gB[B_PLEG(l)] = dw_rs(xn, dzg, 256, 256, tag + "_dwgate")
        dh, gains["ple_gate_norm_g"][l] = rms_bwd(dxn, h_in, _row(small["ple_gate_norm_g"], l), dh, tag + "_pre")

        dh, gains["ff2_pre_g"][l], gains["ff2_post_g"][l] = ff_bwd(
            dh, l, 2, small["ff2_pre_g"], small["ff2_post_g"], rec["ff2"])

        tag = "mix_l%d_b" % l
        mix = rec["mix"]
        h_in, f = mix[0], mix[-1]
        if l == 1:
            _, pooled, ypre, _ = mix
            df, gains["mix_post_g"][l] = post_res_bwd(dh, f, _row(small["mix_post_g"], l), 1.0, F32, tag + "_post")
            dhn, dwg, dsc = pool_bwd(df, ypre, pooled, bgrp, vrow(V_BSCALE), tag + "_pool")
            gC[C_BGRP] = dwg.astype(BF16).reshape(4, NCHIP, 64, 256).transpose(1, 0, 2, 3).reshape(NCHIP, 256, 256)
            gV[V_BSCALE] = jnp.pad(dsc, ((0, 7), (0, 0)))
            dh, gains["mix_pre_g"][l] = rms_bwd(dhn, h_in, _row(small["mix_pre_g"], l), dh, tag + "_pre")
        else:
            gpre = _row(small["mix_pre_g"], l)
            df, gains["mix_post_g"][l] = post_res_bwd(dh, f, _row(small["mix_post_g"], l), 1.0, BF16, tag + "_post")
            if l == 0:
                _, hn, zpre, y, _ = mix
                dy = mm_rs_t(df, GB, B_AOUT, 768, 256, tag + "_out")
                gB[B_AOUT] = dw_rs(y, df, 768, 256, tag + "_dwout")
                dz, dws, dbs, dvg, dvb = gmlp_mid_bwd(zpre, dy, small["a_v_norm_g"], small["a_v_norm_b"], a_ws, bsf,
                                                      tag + "_gate")
                extra.update(a_w_s=dws, a_b_s=dbs[:, 0, :], a_v_norm_g=dvg, a_v_norm_b=dvb)
                gA[A_AIN] = dw_cs(hn, dz, 1536, 512, tag + "_dwin")
                dh, gains["mix_pre_g"][l] = mm_cs_t_rms(dz, GA, A_AIN, 1536, 512, h_in, gpre, dh, tag + "_in")
            elif l == 2:
                _, hn, ag, zc, zs, _ = mix
                dzs = mm_rs_t(df, GB, B_CPW2, 256, 256, tag + "_pw2")
                gB[B_CPW2] = dw_rs(zs, df, 256, 256, tag + "_dwpw2")
                dzc, dng, dnb = conf_ln_bwd(dzs, zc, vrow(V_CNG), vrow(V_CNB), tag + "_ln")
                da_, dg_, dwdw, dbdw = conf_conv_bwd(dzc, ag, vecs[V_CDW:V_CDW + 32], tag + "_conv")
                dag = merge_cols([da_, dg_], tag + "_merge")
                gV[V_CDW] = dwdw
                gV[V_CBDW] = jnp.pad(dbdw, ((0, 7), (0, 0)))
                gV[V_CNG] = jnp.pad(dng, ((0, 7), (0, 0)))
                gV[V_CNB] = jnp.pad(dnb, ((0, 7), (0, 0)))
                gA[A_CIN] = dw_cs(hn, dag, 512, 512, tag + "_dwpw1")
                dh, gains["mix_pre_g"][l] = mm_cs_t_rms(dag, GA, A_CIN, 512, 512, h_in, gpre, dh, tag + "_pw1")
            else:
                _, hn, bgx, y, _ = mix
                dy = mm_rs_t(df, GB, B_DOUT, 256, 256, tag + "_out")
                gB[B_DOUT] = dw_rs(y, df, 256, 256, tag + "_dwout")
                db_, dc_, dx_, dwc = sconv_bwd(dy, bgx, vecs[V_DCONV:V_DCONV + 8], tag + "_conv")
                dbgx = merge_cols([db_, dc_, dx_], tag + "_merge")
                gV[V_DCONV] = dwc
                gA[A_DIN] = dw_cs(hn, dbgx, 768, 256, tag + "_dwin")
                dh, gains["mix_pre_g"][l] = mm_cs_t_rms(dbgx, GA, A_DIN, 768, 256, h_in, gpre, dh, tag + "_in")

        dh, gains["ff1_pre_g"][l], gains["ff1_post_g"][l] = ff_bwd(
            dh, l, 1, small["ff1_pre_g"], small["ff1_post_g"], rec["ff1"])

    return loss_cols, dh, gA, gB, gC, gV, gains, extra


GAIN_NAMES = ("ff1_pre_g", "ff1_post_g", "mix_pre_g", "mix_post_g", "ff2_pre_g", "ff2_post_g", "ple_gate_norm_g",
              "ple_post_g")


def _pad_rows(a, rows):
    return jnp.pad(a, ((0, rows - a.shape[0]), (0, 0)))


def kernel(x, p, ff1_pre_g, ff1_w_gate, ff1_w_up, ff1_w_down, ff1_post_g, mix_pre_g, mix_post_g, ff2_pre_g, ff2_w_gate, ff2_w_up, ff2_w_down, ff2_post_g, ple_gate_norm_g, ple_w_gate, ple_w_proj, ple_post_g, a_w_in, a_v_norm_g, a_v_norm_b, a_w_s, a_b_s, a_w_out, b_w_grp, b_scale, c_w_pw1, c_w_dw, c_b_dw, c_norm_g, c_norm_b, c_w_pw2, d_w_in, d_w_conv, d_w_out, loss_target, m_ff1_pre_g, m_ff1_w_gate, m_ff1_w_up, m_ff1_w_down, m_ff1_post_g, m_mix_pre_g, m_mix_post_g, m_ff2_pre_g, m_ff2_w_gate, m_ff2_w_up, m_ff2_w_down, m_ff2_post_g, m_ple_gate_norm_g, m_ple_w_gate, m_ple_w_proj, m_ple_post_g, m_a_w_in, m_a_v_norm_g, m_a_v_norm_b, m_a_w_s, m_a_b_s, m_a_w_out, m_b_w_grp, m_b_scale, m_c_w_pw1, m_c_w_dw, m_c_b_dw, m_c_norm_g, m_c_norm_b, m_c_w_pw2, m_d_w_in, m_d_w_conv, m_d_w_out, v_ff1_pre_g, v_ff1_w_gate, v_ff1_w_up, v_ff1_w_down, v_ff1_post_g, v_mix_pre_g, v_mix_post_g, v_ff2_pre_g, v_ff2_w_gate, v_ff2_w_up, v_ff2_w_down, v_ff2_post_g, v_ple_gate_norm_g, v_ple_w_gate, v_ple_w_proj, v_ple_post_g, v_a_w_in, v_a_v_norm_g, v_a_v_norm_b, v_a_w_s, v_a_b_s, v_a_w_out, v_b_w_grp, v_b_scale, v_c_w_pw1, v_c_w_dw, v_c_b_dw, v_c_norm_g, v_c_norm_b, v_c_w_pw2, v_d_w_in, v_d_w_conv, v_d_w_out):
    args = dict(locals())
    wnames = ["ff1_pre_g", "ff1_w_gate", "ff1_w_up", "ff1_w_down", "ff1_post_g", "mix_pre_g", "mix_post_g",
              "ff2_pre_g", "ff2_w_gate", "ff2_w_up", "ff2_w_down", "ff2_post_g", "ple_gate_norm_g", "ple_w_gate",
              "ple_w_proj", "ple_post_g", "a_w_in", "a_v_norm_g", "a_v_norm_b", "a_w_s", "a_b_s", "a_w_out",
              "b_w_grp", "b_scale", "c_w_pw1", "c_w_dw", "c_b_dw", "c_norm_g", "c_norm_b", "c_w_pw2", "d_w_in",
              "d_w_conv", "d_w_out"]

    PA, PB, PC, PV = pack_weights(args)
    GA, GB, GC, GV = all_gather_chips([PA, PB, PC, PV], "gather_weights")
    vecs = GV.transpose(1, 0, 2).reshape(V_ROWS, DM)

    small = {k: args[k] for k in GAIN_NAMES}
    small["a_v_norm_g"] = a_v_norm_g
    small["a_v_norm_b"] = a_v_norm_b
    loss_cols, grad_x, gA, gB, gC, gV, gains, extra = local_step(
        x[0], p[:, 0], loss_target[0], small, GA, GB, GC, vecs, a_w_s[0], a_b_s[0])

    loss = lax.psum((0.5 / DM) * jnp.sum(loss_cols), ("x", "y", "c"))

    dA, dB, dC, dV, dE = pack_grads(gA, gB, gC, gV, gains, extra)
    red = reduce_scatter({"A": dA, "B": dB, "C": dC, "V": dV, "E": dE})
    (gE,) = all_gather_chips([red["E"]], "gather_replicated_grads")
    grads = unpack_grads(red["A"], red["B"], red["C"], red["V"], gE.reshape(E_ROWS, DM))

    deltas, new_m, new_v = {}, {}, {}
    for k in wnames:
        deltas[k], new_m[k], new_v[k] = adamw(args[k], grads[k], args["m_" + k], args["v_" + k], "adamw_" + k)
    return (loss, grad_x[None], *[grads[k] for k in wnames], *[deltas[k] for k in wnames],
            *[new_m[k] for k in wnames], *[new_v[k] for k in wnames])


def pack_weights(w):
    padc = lambda a: jnp.pad(a, ((0, 0), (0, FB - FW)))
    colsA = []
    for l in range(4):
        colsA += [padc(w["ff1_w_gate"][l]), padc(w["ff1_w_up"][l]), padc(w["ff2_w_gate"][l]), padc(w["ff2_w_up"][l])]
    colsA += [w["a_w_in"][0], w["c_w_pw1"][0], w["d_w_in"][0]]
    PA = jnp.concatenate(colsA, axis=1).astype(BF16)
    rowsB = [_pad_rows(w["ff1_w_down"][l], FB) for l in range(4)] + [_pad_rows(w["ff2_w_down"][l], FB) for l in range(4)]
    rowsB += [w["ple_w_gate"][l] for l in range(4)] + [w["a_w_out"][0], w["c_w_pw2"][0], w["d_w_out"][0]]
    PB = jnp.concatenate(rowsB, axis=0).astype(BF16)
    PC = jnp.concatenate([w["ple_w_proj"][l] for l in range(4)] + [w["b_w_grp"][0].reshape(256, 256)],
                         axis=0).astype(BF16)
    PV = jnp.concatenate([_pad_rows(w["b_scale"], 8), _pad_rows(w["c_b_dw"], 8), _pad_rows(w["c_norm_g"], 8),
                          _pad_rows(w["c_norm_b"], 8), _pad_rows(w["d_w_conv"][0], 8), _pad_rows(w["c_w_dw"][0], 40)],
                         axis=0)
    return PA, PB, PC, PV


def pack_grads(gA, gB, gC, gV, gains, extra):
    dA = jnp.concatenate([gA[A_FF(l, j)] for l in range(4) for j in (0, 2)] + [gA[A_AIN], gA[A_CIN], gA[A_DIN]], axis=2)
    dB = jnp.concatenate([gB[B_FF1D(l)] for l in range(4)] + [gB[B_FF2D(l)] for l in range(4)]
                         + [gB[B_PLEG(l)] for l in range(4)] + [gB[B_AOUT], gB[B_CPW2], gB[B_DOUT]], axis=1)
    dC = jnp.concatenate([gC[C_PROJ(l)] for l in range(4)] + [gC[C_BGRP]], axis=1)
    dVt = jnp.concatenate([gV[V_BSCALE], gV[V_CBDW], gV[V_CNG], gV[V_CNB], gV[V_DCONV], gV[V_CDW],
                           jnp.zeros((8, DM), F32)], axis=0)
    dV = dVt.reshape(V_ROWS, NCHIP, 256).transpose(1, 0, 2)
    rowsE = [_pad_rows(jnp.concatenate(gains[k], axis=0), 8) for k in GAIN_NAMES]
    rowsE += [_pad_rows(extra["a_v_norm_g"].reshape(3, DM), 8), _pad_rows(extra["a_v_norm_b"].reshape(3, DM), 8),
              jnp.pad(extra["a_b_s"].reshape(1536), (0, 8 * DM - 1536)).reshape(8, DM),
              extra["a_w_s"].reshape(192, DM)]
    dE = _pad_rows(jnp.concatenate(rowsE, axis=0), E_ROWS).reshape(NCHIP, E_ROWS // NCHIP, DM)
    return dA, dB, dC, dV, dE


def unpack_grads(RA, RB, RCc, RV, gE):
    grads = {}
    for i, k in enumerate(GAIN_NAMES):
        grads[k] = gE[8 * i:8 * i + 4]
    grads["a_v_norm_g"] = gE[64:67].reshape(1, 3072)
    grads["a_v_norm_b"] = gE[72:75].reshape(1, 3072)
    grads["a_b_s"] = gE[80:88].reshape(8 * DM)[:1536].reshape(1, 12, 128)
    grads["a_w_s"] = gE[88:280].reshape(1, 12, 128, 128)
    colA = lambda off, n: RA[:, off:off + n]
    grads["ff1_w_gate"] = jnp.stack([colA(A_FF(l, 0), FW) for l in range(4)])
    grads["ff1_w_up"] = jnp.stack([colA(A_FF(l, 1), FW) for l in range(4)])
    grads["ff2_w_gate"] = jnp.stack([colA(A_FF(l, 2), FW) for l in range(4)])
    grads["ff2_w_up"] = jnp.stack([colA(A_FF(l, 3), FW) for l in range(4)])
    grads["a_w_in"] = colA(A_AIN, 1536)[None]
    grads["c_w_pw1"] = colA(A_CIN, 512)[None]
    grads["d_w_in"] = colA(A_DIN, 768)[None]
    rowB = lambda off, n: RB[off:off + n]
    grads["ff1_w_down"] = jnp.stack([rowB(B_FF1D(l), FW) for l in range(4)])
    grads["ff2_w_down"] = jnp.stack([rowB(B_FF2D(l), FW) for l in range(4)])
    grads["ple_w_gate"] = jnp.stack([rowB(B_PLEG(l), 256) for l in range(4)])
    grads["a_w_out"] = rowB(B_AOUT, 768)[None]
    grads["c_w_pw2"] = rowB(B_CPW2, 256)[None]
    grads["d_w_out"] = rowB(B_DOUT, 256)[None]
    grads["ple_w_proj"] = jnp.stack([RCc[C_PROJ(l):C_PROJ(l) + 256] for l in range(4)])
    grads["b_w_grp"] = RCc[C_BGRP:C_BGRP + 256].reshape(1, 4, 64, 256)
    grads["b_scale"] = RV[V_BSCALE:V_BSCALE + 1]
    grads["c_b_dw"] = RV[V_CBDW:V_CBDW + 1]
    grads["c_norm_g"] = RV[V_CNG:V_CNG + 1]
    grads["c_norm_b"] = RV[V_CNB:V_CNB + 1]
    grads["d_w_conv"] = RV[V_DCONV:V_DCONV + 3][None]
    grads["c_w_dw"] = RV[V_CDW:V_CDW + 31][None]
    return grads
```

```python
import functools
import math

import jax
import jax.numpy as jnp
from jax import lax
from jax.experimental import pallas as pl
from jax.experimental.pallas import tpu as pltpu

F32, BF16 = jnp.float32, jnp.bfloat16
EPS = 1e-6
DM = 1024
FW = 704
FB = 768
NCHIP = 4
VMEM_LIMIT = 56 * 1024 * 1024
ANY = pl.BlockSpec(memory_space=pl.ANY)
MESH = pl.DeviceIdType.MESH

A_FF = lambda l, j: j * FB
A_AIN = A_CIN = A_DIN = 4 * FB
B_FF1D = lambda l: 0
B_FF2D = lambda l: FB
B_PLEG = lambda l: 2 * FB
B_AOUT = B_CPW2 = B_DOUT = 2 * FB + 256
C_PROJ = lambda l: 0
C_BGRP = 256
V_BSCALE, V_CBDW, V_CNG, V_CNB, V_DCONV, V_CDW, V_ROWS = 0, 8, 16, 24, 32, 40, 80
E_ROWS = 320

ADAM_LR, ADAM_B1, ADAM_B2, ADAM_EPS, ADAM_WD, ADAM_STEP = 0.001, 0.9, 0.999, 1e-08, 0.01, 10


def _cp(sem):
    return pltpu.CompilerParams(dimension_semantics=sem, vmem_limit_bytes=VMEM_LIMIT)


def _sig(x):
    return 1.0 / (1.0 + jnp.exp(-x))


_GC = math.sqrt(2.0 / math.pi)


def _gelu(x):
    return 0.5 * x * (1.0 + jnp.tanh(_GC * (x + 0.044715 * x * x * x)))


def _gelu_grad(x):
    t = jnp.tanh(_GC * (x + 0.044715 * x * x * x))
    return 0.5 * (1.0 + t) + 0.5 * x * (1.0 - t * t) * _GC * (1.0 + 3.0 * 0.044715 * x * x)


def _dot_nn(a, b):
    return lax.dot_general(a, b, (((1,), (0,)), ((), ())), preferred_element_type=F32)


def _dot_nt(a, b):
    return lax.dot_general(a, b, (((1,), (1,)), ((), ())), preferred_element_type=F32)


def _dot_tn(a, b):
    return lax.dot_general(a, b, (((0,), (0,)), ((), ())), preferred_element_type=F32)


def mm_cs(x, G, off, nb, tn, out_dtype, name, roff=0):
    T, K = x.shape
    tm = min(1024, T)
    nj, ob, rb_ = nb // tn, off // tn, roff // K
    assert nb % tn == 0 and off % tn == 0 and roff % K == 0

    def body(x_ref, w_ref, o_ref):
        o_ref[...] = _dot_nn(x_ref[...], w_ref[...]).astype(o_ref.dtype)

    return pl.pallas_call(
        body, name=name, grid=(T // tm, NCHIP, nj),
        in_specs=[pl.BlockSpec((tm, K), lambda i, s, j: (i, 0)),
                  pl.BlockSpec((None, K, tn), lambda i, s, j: (s, rb_, ob + j))],
        out_specs=pl.BlockSpec((tm, tn), lambda i, s, j: (i, s * nj + j)),
        out_shape=jax.ShapeDtypeStruct((T, NCHIP * nb), out_dtype),
        compiler_params=_cp(("parallel", "arbitrary", "arbitrary")))(x, G)


def mm_cs_t(dy, G, off, nb, tn, name):
    T = dy.shape[0]
    K = G.shape[1]
    tm = min(1024, T)
    nj, ob = nb // tn, off // tn
    nk = NCHIP * nj

    def body(dy_ref, w_ref, o_ref, acc):
        k = pl.program_id(1)

        @pl.when(k == 0)
        def _():
            acc[...] = jnp.zeros_like(acc)

        acc[...] += _dot_nt(dy_ref[...], w_ref[...])

        @pl.when(k == nk - 1)
        def _():
            o_ref[...] = acc[...]

    return pl.pallas_call(
        body, name=name, grid=(T // tm, nk),
        in_specs=[pl.BlockSpec((tm, tn), lambda i, k: (i, k)),
                  pl.BlockSpec((None, K, tn), lambda i, k: (k // nj, 0, ob + k % nj))],
        out_specs=pl.BlockSpec((tm, K), lambda i, k: (i, 0)),
        out_shape=jax.ShapeDtypeStruct((T, K), F32),
        scratch_shapes=[pltpu.VMEM((tm, K), F32)],
        compiler_params=_cp(("parallel", "arbitrary")))(dy, G)


def mm_rs(a, G, off, rb, tk, name):
    T = a.shape[0]
    N = G.shape[2]
    tm = min(1024, T)
    nkk, ob = rb // tk, off // tk
    nk = NCHIP * nkk
    assert rb % tk == 0 and off % tk == 0

    def body(a_ref, w_ref, o_ref, acc):
        k = pl.program_id(1)

        @pl.when(k == 0)
        def _():
            acc[...] = jnp.zeros_like(acc)

        acc[...] += _dot_nn(a_ref[...], w_ref[...])

        @pl.when(k == nk - 1)
        def _():
            o_ref[...] = acc[...]

    return pl.pallas_call(
        body, name=name, grid=(T // tm, nk),
        in_specs=[pl.BlockSpec((tm, tk), lambda i, k: (i, k)),
                  pl.BlockSpec((None, tk, N), lambda i, k: (k // nkk, ob + k % nkk, 0))],
        out_specs=pl.BlockSpec((tm, N), lambda i, k: (i, 0)),
        out_shape=jax.ShapeDtypeStruct((T, N), F32),
        scratch_shapes=[pltpu.VMEM((tm, N), F32)],
        compiler_params=_cp(("parallel", "arbitrary")))(a, G)


def mm_rs_t(dy, G, off, rb, tk, name):
    T, N = dy.shape
    tm = min(1024, T)
    nkk, ob = rb // tk, off // tk
    nk = NCHIP * nkk

    def body(dy_ref, w_ref, o_ref):
        o_ref[...] = _dot_nt(dy_ref[...], w_ref[...]).astype(o_ref.dtype)

    return pl.pallas_call(
        body, name=name, grid=(T // tm, nk),
        in_specs=[pl.BlockSpec((tm, N), lambda i, k: (i, 0)),
                  pl.BlockSpec((None, tk, N), lambda i, k: (k // nkk, ob + k % nkk, 0))],
        out_specs=pl.BlockSpec((tm, tk), lambda i, k: (i, k)),
        out_shape=jax.ShapeDtypeStruct((T, NCHIP * rb), BF16),
        compiler_params=_cp(("parallel", "arbitrary")))(dy, G)


def mm_tn(a, b, tmm, tn, out_shape, out_map, name):
    T, M = a.shape
    N = b.shape[1]
    tt = min(1024, T)
    nt = T // tt

    def body(a_ref, b_ref, o_ref, acc):
        t = pl.program_id(2)

        @pl.when(t == 0)
        def _():
            acc[...] = jnp.zeros_like(acc)

        acc[...] += _dot_tn(a_ref[...], b_ref[...])

        @pl.when(t == nt - 1)
        def _():
            o_ref[...] = acc[...].astype(o_ref.dtype)

    return pl.pallas_call(
        body, name=name, grid=(M // tmm, N // tn, nt),
        in_specs=[pl.BlockSpec((tt, tmm), lambda i, j, t: (t, i)),
                  pl.BlockSpec((tt, tn), lambda i, j, t: (t, j))],
        out_specs=pl.BlockSpec((None, tmm, tn), lambda i, j, t: out_map(i, j)),
        out_shape=jax.ShapeDtypeStruct(out_shape, BF16),
        scratch_shapes=[pltpu.VMEM((tmm, tn), F32)],
        compiler_params=_cp(("parallel", "parallel", "arbitrary")))(a, b)


def dw_cs(x, dy, nb, tn, name):
    K = x.shape[1]
    nj = nb // tn
    return mm_tn(x, dy, K, tn, (NCHIP, K, nb), lambda i, j: (j // nj, 0, j % nj), name)


def dw_rs(a, dy, rb, tr, name):
    N = dy.shape[1]
    ni = rb // tr
    return mm_tn(a, dy, tr, N, (NCHIP, rb, N), lambda i, j: (i // ni, i % ni, 0), name)


def ff_gateup(xn, GA, off, name):
    T, K = xn.shape
    tm = min(512, T)
    ob = off // (2 * FB)
    assert off % (2 * FB) == 0

    def body(x_ref, w_ref, gu_ref, a_ref):
        r = _dot_nn(x_ref[...], w_ref[...])
        g, u = r[:, :FB], r[:, FB:]
        gu_ref[...] = r.astype(gu_ref.dtype)
        a_ref[...] = (g * _sig(g) * u).astype(a_ref.dtype)

    return pl.pallas_call(
        body, name=name, grid=(T // tm, NCHIP),
        in_specs=[pl.BlockSpec((tm, K), lambda i, s: (i, 0)),
                  pl.BlockSpec((None, K, 2 * FB), lambda i, s: (s, 0, ob))],
        out_specs=[pl.BlockSpec((tm, 2 * FB), lambda i, s: (i, s)), pl.BlockSpec((tm, FB), lambda i, s: (i, s))],
        out_shape=[jax.ShapeDtypeStruct((T, NCHIP * 2 * FB), BF16), jax.ShapeDtypeStruct((T, NCHIP * FB), BF16)],
        compiler_params=_cp(("parallel", "arbitrary")))(xn, GA)


def mm_rs_post(a, G, off, rb, tk, h, g, scale, name):
    T = a.shape[0]
    N = G.shape[2]
    tm = min(512, T)
    nkk, ob = rb // tk, off // tk
    nk = NCHIP * nkk
    assert rb % tk == 0 and off % tk == 0

    def body(a_ref, w_ref, h_ref, g_ref, f_ref, o_ref, acc):
        k = pl.program_id(1)

        @pl.when(k == 0)
        def _():
            acc[...] = jnp.zeros_like(acc)

        acc[...] += _dot_nn(a_ref[...], w_ref[...])

        @pl.when(k == nk - 1)
        def _():
            f = acc[...]
            f_ref[...] = f
            r = lax.rsqrt(jnp.mean(f * f, axis=-1, keepdims=True) + EPS)
            o_ref[...] = h_ref[...] + scale * (f * r * g_ref[...])

    row = pl.BlockSpec((tm, N), lambda i, k: (i, 0))
    return pl.pallas_call(
        body, name=name, grid=(T // tm, nk),
        in_specs=[pl.BlockSpec((tm, tk), lambda i, k: (i, k)),
                  pl.BlockSpec((None, tk, N), lambda i, k: (k // nkk, ob + k % nkk, 0)),
                  row, pl.BlockSpec((1, N), lambda i, k: (0, 0))],
        out_specs=[row, row],
        out_shape=[jax.ShapeDtypeStruct((T, N), F32), jax.ShapeDtypeStruct((T, N), F32)],
        scratch_shapes=[pltpu.VMEM((tm, N), F32)],
        compiler_params=_cp(("parallel", "arbitrary")))(a, G, h, g)


def ff_bwd_down(dh, f, g, GB, down, gu, name):
    T, N = dh.shape
    tm = min(512, T)
    ob = down // FB

    def body(d_ref, f_ref, g_ref, w_ref, gu_ref, df_ref, dg_ref, dgu_ref):
        i, s = pl.program_id(0), pl.program_id(1)

        @pl.when(s == 0)
        def _():
            f = f_ref[...]
            r = lax.rsqrt(jnp.mean(f * f, axis=-1, keepdims=True) + EPS)
            d = 0.5 * d_ref[...]
            t = d * g_ref[...]
            df_ref[...] = (r * t - f * (r * r * r * jnp.mean(t * f, axis=-1, keepdims=True))).astype(df_ref.dtype)
            _acc_rows(dg_ref, i, jnp.sum(d * f * r, axis=0, keepdims=True))

        da = _dot_nt(df_ref[...], w_ref[...])
        gt = gu_ref[:, :FB].astype(F32)
        u = gu_ref[:, FB:].astype(F32)
        sg = _sig(gt)
        dgu_ref[:, :FB] = (da * u * (sg * (1.0 + gt * (1.0 - sg)))).astype(dgu_ref.dtype)
        dgu_ref[:, FB:] = (da * (gt * sg)).astype(dgu_ref.dtype)

    row = pl.BlockSpec((tm, N), lambda i, s: (i, 0))
    vec = pl.BlockSpec((1, N), lambda i, s: (0, 0))
    return pl.pallas_call(
        body, name=name, grid=(T // tm, NCHIP),
        in_specs=[row, row, vec, pl.BlockSpec((None, FB, N), lambda i, s: (s, ob, 0)),
                  pl.BlockSpec((tm, 2 * FB), lambda i, s: (i, s))],
        out_specs=[row, vec, pl.BlockSpec((tm, 2 * FB), lambda i, s: (i, s))],
        out_shape=[jax.ShapeDtypeStruct((T, N), BF16), jax.ShapeDtypeStruct((1, N), F32),
                   jax.ShapeDtypeStruct((T, NCHIP * 2 * FB), BF16)],
        compiler_params=_cp(("arbitrary", "arbitrary")))(dh, f, g, GB, gu)


def mm_cs_t_rms(dy, G, off, nb, tn, h, g, skip, name):
    T = dy.shape[0]
    K = G.shape[1]
    tm = min(512, T)
    nj, ob = nb // tn, off // tn
    nk = NCHIP * nj
    assert nb % tn == 0 and off % tn == 0

    def body(dy_ref, w_ref, h_ref, g_ref, s_ref, o_ref, dg_ref, acc):
        i, k = pl.program_id(0), pl.program_id(1)

        @pl.when(k == 0)
        def _():
            acc[...] = jnp.zeros_like(acc)

        acc[...] += _dot_nt(dy_ref[...], w_ref[...])

        @pl.when(k == nk - 1)
        def _():
            d = acc[...]
            x = h_ref[...]
            r = lax.rsqrt(jnp.mean(x * x, axis=-1, keepdims=True) + EPS)
            xh = x * r
            t = d * g_ref[...]
            o_ref[...] = s_ref[...] + r * (t - xh * jnp.mean(t * xh, axis=-1, keepdims=True))
            _acc_rows(dg_ref, i, jnp.sum(d * xh, axis=0, keepdims=True))

    row = pl.BlockSpec((tm, K), lambda i, k: (i, 0))
    vec = pl.BlockSpec((1, K), lambda i, k: (0, 0))
    return pl.pallas_call(
        body, name=name, grid=(T // tm, nk),
        in_specs=[pl.BlockSpec((tm, tn), lambda i, k: (i, k)),
                  pl.BlockSpec((None, K, tn), lambda i, k: (k // nj, 0, ob + k % nj)), row, vec, row],
        out_specs=[row, vec],
        out_shape=[jax.ShapeDtypeStruct((T, K), F32), jax.ShapeDtypeStruct((1, K), F32)],
        scratch_shapes=[pltpu.VMEM((tm, K), F32)],
        compiler_params=_cp(("arbitrary", "arbitrary")))(dy, G, h, g, skip)


def _rows(tm, C):
    return pl.BlockSpec((tm, C), lambda i: (i, 0))


def _vec(C):
    return pl.BlockSpec((1, C), lambda i: (0, 0))


def _acc_rows(ref, i, val):
    @pl.when(i == 0)
    def _():
        ref[...] = val

    @pl.when(i > 0)
    def _():
        ref[...] += val


def rms_fwd(h, g, out_dtype, name):
    T, C = h.shape
    tm = min(512, T)

    def body(h_ref, g_ref, o_ref):
        x = h_ref[...]
        r = lax.rsqrt(jnp.mean(x * x, axis=-1, keepdims=True) + EPS)
        o_ref[...] = (x * r * g_ref[...]).astype(o_ref.dtype)

    return pl.pallas_call(
        body, name=name, grid=(T // tm,), in_specs=[_rows(tm, C), _vec(C)], out_specs=_rows(tm, C),
        out_shape=jax.ShapeDtypeStruct((T, C), out_dtype), compiler_params=_cp(("parallel",)))(h, g)


def rms_bwd(dxn, h, g, dh_skip, name):
    T, C = h.shape
    tm = min(512, T)

    def body(d_ref, h_ref, g_ref, s_ref, o_ref, dg_ref):
        i = pl.program_id(0)
        x = h_ref[...]
        r = lax.rsqrt(jnp.mean(x * x, axis=-1, keepdims=True) + EPS)
        xh = x * r
        d = d_ref[...].astype(F32)
        t = d * g_ref[...]
        o_ref[...] = s_ref[...] + r * (t - xh * jnp.mean(t * xh, axis=-1, keepdims=True))
        _acc_rows(dg_ref, i, jnp.sum(d * xh, axis=0, keepdims=True))

    return pl.pallas_call(
        body, name=name, grid=(T // tm,),
        in_specs=[_rows(tm, C), _rows(tm, C), _vec(C), _rows(tm, C)],
        out_specs=[_rows(tm, C), _vec(C)],
        out_shape=[jax.ShapeDtypeStruct((T, C), F32), jax.ShapeDtypeStruct((1, C), F32)],
        compiler_params=_cp(("arbitrary",)))(dxn, h, g, dh_skip)


def post_res(h, f, g, scale, name):
    T, C = h.shape
    tm = min(512, T)

    def body(h_ref, f_ref, g_ref, o_ref):
        f = f_ref[...]
        r = lax.rsqrt(jnp.mean(f * f, axis=-1, keepdims=True) + EPS)
        o_ref[...] = h_ref[...] + scale * (f * r * g_ref[...])

    return pl.pallas_call(
        body, name=name, grid=(T // tm,), in_specs=[_rows(tm, C), _rows(tm, C), _vec(C)],
        out_specs=_rows(tm, C), out_shape=jax.ShapeDtypeStruct((T, C), F32),
        compiler_params=_cp(("parallel",)))(h, f, g)


def post_res_bwd(dh, f, g, scale, out_dtype, name):
    T, C = dh.shape
    tm = min(512, T)

    def body(d_ref, f_ref, g_ref, o_ref, dg_ref):
        i = pl.program_id(0)
        f = f_ref[...]
        r = lax.rsqrt(jnp.mean(f * f, axis=-1, keepdims=True) + EPS)
        d = scale * d_ref[...]
        t = d * g_ref[...]
        o_ref[...] = (r * t - f * (r * r * r * jnp.mean(t * f, axis=-1, keepdims=True))).astype(o_ref.dtype)
        _acc_rows(dg_ref, i, jnp.sum(d * f * r, axis=0, keepdims=True))

    return pl.pallas_call(
        body, name=name, grid=(T // tm,), in_specs=[_rows(tm, C), _rows(tm, C), _vec(C)],
        out_specs=[_rows(tm, C), _vec(C)],
        out_shape=[jax.ShapeDtypeStruct((T, C), out_dtype), jax.ShapeDtypeStruct((1, C), F32)],
        compiler_params=_cp(("arbitrary",)))(dh, f, g)


def ff_act(gu, name):
    T = gu.shape[0]
    tm = min(512, T)

    def body(gu_ref, o_ref):
        g = gu_ref[:, :FB].astype(F32)
        u = gu_ref[:, FB:].astype(F32)
        o_ref[...] = (g * _sig(g) * u).astype(o_ref.dtype)

    return pl.pallas_call(
        body, name=name, grid=(T // tm, NCHIP),
        in_specs=[pl.BlockSpec((tm, 2 * FB), lambda i, s: (i, s))],
        out_specs=pl.BlockSpec((tm, FB), lambda i, s: (i, s)),
        out_shape=jax.ShapeDtypeStruct((T, NCHIP * FB), BF16),
        compiler_params=_cp(("parallel", "parallel")))(gu)


def ff_act_bwd(da, gu, name):
    T = gu.shape[0]
    tm = min(512, T)

    def body(da_ref, gu_ref, o_ref):
        g = gu_ref[:, :FB].astype(F32)
        u = gu_ref[:, FB:].astype(F32)
        da = da_ref[...].astype(F32)
        s = _sig(g)
        o_ref[:, :FB] = (da * u * (s * (1.0 + g * (1.0 - s)))).astype(o_ref.dtype)
        o_ref[:, FB:] = (da * (g * s)).astype(o_ref.dtype)

    return pl.pallas_call(
        body, name=name, grid=(T // tm, NCHIP),
        in_specs=[pl.BlockSpec((tm, FB), lambda i, s: (i, s)), pl.BlockSpec((tm, 2 * FB), lambda i, s: (i, s))],
        out_specs=pl.BlockSpec((tm, 2 * FB), lambda i, s: (i, s)),
        out_shape=jax.ShapeDtypeStruct((T, NCHIP * 2 * FB), BF16),
        compiler_params=_cp(("parallel", "parallel")))(da, gu)


def ple_post(h, zg, pe, g, name):
    T, C = h.shape
    tm = min(512, T)

    def body(h_ref, z_ref, p_ref, g_ref, o_ref):
        e = p_ref[...] * _sig(z_ref[...])
        r = lax.rsqrt(jnp.mean(e * e, axis=-1, keepdims=True) + EPS)
        o_ref[...] = h_ref[...] + e * r * g_ref[...]

    return pl.pallas_call(
        body, name=name, grid=(T // tm,), in_specs=[_rows(tm, C), _rows(tm, C), _rows(tm, C), _vec(C)],
        out_specs=_rows(tm, C), out_shape=jax.ShapeDtypeStruct((T, C), F32),
        compiler_params=_cp(("parallel",)))(h, zg, pe, g)


def ple_post_bwd(dh, zg, pe, g, name):
    T, C = dh.shape
    tm = min(512, T)

    def body(d_ref, z_ref, p_ref, g_ref, dz_ref, dp_ref, dg_ref):
        i = pl.program_id(0)
        s = _sig(z_ref[...])
        pe_ = p_ref[...]
        e = pe_ * s
        r = lax.rsqrt(jnp.mean(e * e, axis=-1, keepdims=True) + EPS)
        d = d_ref[...]
        t = d * g_ref[...]
        de = r * t - e * (r * r * r * jnp.mean(t * e, axis=-1, keepdims=True))
        dp_ref[...] = (de * s).astype(dp_ref.dtype)
        dz_ref[...] = (de * pe_ * s * (1.0 - s)).astype(dz_ref.dtype)
        _acc_rows(dg_ref, i, jnp.sum(d * e * r, axis=0, keepdims=True))

    return pl.pallas_call(
        body, name=name, grid=(T // tm,), in_specs=[_rows(tm, C), _rows(tm, C), _rows(tm, C), _vec(C)],
        out_specs=[_rows(tm, C), _rows(tm, C), _vec(C)],
        out_shape=[jax.ShapeDtypeStruct((T, C), BF16), jax.ShapeDtypeStruct((T, C), BF16),
                   jax.ShapeDtypeStruct((1, C), F32)],
        compiler_params=_cp(("arbitrary",)))(dh, zg, pe, g)


def loss_head(h, tgt, name):
    T, C = h.shape
    tm = min(512, T)

    def body(h_ref, t_ref, d_ref, l_ref):
        i = pl.program_id(0)
        e = h_ref[...] - t_ref[...]
        d_ref[...] = e * (1.0 / C)
        _acc_rows(l_ref, i, jnp.sum(e * e, axis=0, keepdims=True))

    return pl.pallas_call(
        body, name=name, grid=(T // tm,), in_specs=[_rows(tm, C), _rows(tm, C)],
        out_specs=[_rows(tm, C), _vec(C)],
        out_shape=[jax.ShapeDtypeStruct((T, C), F32), jax.ShapeDtypeStruct((1, C), F32)],
        compiler_params=_cp(("arbitrary",)))(h, tgt)


AH, AG_N, AGW, CHUNK = 3072, 12, 256, 128


def _tril_bf16(w):
    r = lax.broadcasted_iota(jnp.int32, (CHUNK, CHUNK), 0)
    c = lax.broadcasted_iota(jnp.int32, (CHUNK, CHUNK), 1)
    return jnp.where(r >= c, w, 0.0).astype(BF16)


def _ln_stats(vs_ref, width):
    v = vs_ref[...]
    mu = jnp.sum(v, axis=-1, keepdims=True) * (1.0 / width)
    vc = v - mu
    var = jnp.sum(vc * vc, axis=-1, keepdims=True) * (1.0 / width)
    return mu, lax.rsqrt(var + EPS)


def gmlp_mid_fwd(zpre, vg, vb, ws, bsf, name):
    T = zpre.shape[0]

    def body(z_ref, vg_ref, vb_ref, ws_ref, bs_ref, y_ref, vs_ref):
        for g in range(AG_N):
            vs_ref[:, g * AGW:(g + 1) * AGW] = _gelu(z_ref[:, AH + g * AGW:AH + (g + 1) * AGW].astype(F32))
        mu, rstd = _ln_stats(vs_ref, AH)
        for g in range(AG_N):
            sl = slice(g * AGW, (g + 1) * AGW)
            vn = ((vs_ref[:, sl] - mu) * rstd * vg_ref[:, sl] + vb_ref[:, sl]).astype(BF16)
            sv = _dot_nn(_tril_bf16(ws_ref[g]), vn) + bs_ref[g]
            u = _gelu(z_ref[:, sl].astype(F32))
            y_ref[:, sl] = (u * sv).astype(y_ref.dtype)

    return pl.pallas_call(
        body, name=name, grid=(T // CHUNK,),
        in_specs=[_rows(CHUNK, 2 * AH), _vec(AH), _vec(AH),
                  pl.BlockSpec((AG_N, CHUNK, CHUNK), lambda i: (0, 0, 0)),
                  pl.BlockSpec((AG_N, CHUNK, AGW), lambda i: (0, 0, 0))],
        out_specs=_rows(CHUNK, AH), out_shape=jax.ShapeDtypeStruct((T, AH), BF16),
        scratch_shapes=[pltpu.VMEM((CHUNK, AH), F32)],
        compiler_params=_cp(("parallel",)))(zpre, vg, vb, ws, bsf)


def gmlp_mid_bwd(zpre, dy, vg, vb, ws, bsf, name):
    T = zpre.shape[0]

    def body(z_ref, dy_ref, vg_ref, vb_ref, ws_ref, bs_ref, dz_ref, dws_ref, dbs_ref, dvg_ref, dvb_ref,
             vs_ref, dvn_ref):
        i = pl.program_id(0)

        @pl.when(i == 0)
        def _():
            dws_ref[...] = jnp.zeros_like(dws_ref)
            dbs_ref[...] = jnp.zeros_like(dbs_ref)
            dvg_ref[...] = jnp.zeros_like(dvg_ref)
            dvb_ref[...] = jnp.zeros_like(dvb_ref)

        for g in range(AG_N):
            vs_ref[:, g * AGW:(g + 1) * AGW] = _gelu(z_ref[:, AH + g * AGW:AH + (g + 1) * AGW].astype(F32))
        mu, rstd = _ln_stats(vs_ref, AH)
        r_i = lax.broadcasted_iota(jnp.int32, (CHUNK, CHUNK), 0)
        c_i = lax.broadcasted_iota(jnp.int32, (CHUNK, CHUNK), 1)
        ones8 = jnp.ones((8, AGW), F32)
        m1 = jnp.zeros((CHUNK, 1), F32)
        m2 = jnp.zeros((CHUNK, 1), F32)
        for g in range(AG_N):
            sl = slice(g * AGW, (g + 1) * AGW)
            vh = (vs_ref[:, sl] - mu) * rstd
            vn = (vh * vg_ref[:, sl] + vb_ref[:, sl]).astype(BF16)
            wm = _tril_bf16(ws_ref[g])
            sv = _dot_nn(wm, vn) + bs_ref[g]
            zu = z_ref[:, sl].astype(F32)
            u = _gelu(zu)
            dyg = dy_ref[:, sl].astype(F32)
            dz_ref[:, sl] = (dyg * sv * _gelu_grad(zu)).astype(dz_ref.dtype)
            dsv = dyg * u
            dsv_b = dsv.astype(BF16)
            dws_ref[g] += jnp.where(r_i >= c_i, _dot_nt(dsv_b, vn), 0.0)
            dbs_ref[g] += _dot_nt(ones8, dsv)
            dvn = _dot_tn(wm, dsv_b)
            dvn_ref[:, sl] = dvn
            dvh = dvn * vg_ref[:, sl]
            m1 = m1 + jnp.sum(dvh, axis=-1, keepdims=True)
            m2 = m2 + jnp.sum(dvh * vh, axis=-1, keepdims=True)
            dvg_ref[:, sl] += jnp.sum(dvn * vh, axis=0, keepdims=True)
            dvb_ref[:, sl] += jnp.sum(dvn, axis=0, keepdims=True)
        m1 = m1 * (1.0 / AH)
        m2 = m2 * (1.0 / AH)
        for g in range(AG_N):
            sl = slice(g * AGW, (g + 1) * AGW)
            vh = (vs_ref[:, sl] - mu) * rstd
            dv = rstd * (dvn_ref[:, sl] * vg_ref[:, sl] - m1 - vh * m2)
            zv = z_ref[:, AH + g * AGW:AH + (g + 1) * AGW].astype(F32)
            dz_ref[:, AH + g * AGW:AH + (g + 1) * AGW] = (dv * _gelu_grad(zv)).astype(dz_ref.dtype)

    full3 = lambda a, b, c: pl.BlockSpec((a, b, c), lambda i: (0, 0, 0))
    return pl.pallas_call(
        body, name=name, grid=(T // CHUNK,),
        in_specs=[_rows(CHUNK, 2 * AH), _rows(CHUNK, AH), _vec(AH), _vec(AH),
                  full3(AG_N, CHUNK, CHUNK), full3(AG_N, CHUNK, AGW)],
        out_specs=[_rows(CHUNK, 2 * AH), full3(AG_N, CHUNK, CHUNK), full3(AG_N, 8, CHUNK), _vec(AH), _vec(AH)],
        out_shape=[jax.ShapeDtypeStruct((T, 2 * AH), BF16), jax.ShapeDtypeStruct((AG_N, CHUNK, CHUNK), F32),
                   jax.ShapeDtypeStruct((AG_N, 8, CHUNK), F32), jax.ShapeDtypeStruct((1, AH), F32),
                   jax.ShapeDtypeStruct((1, AH), F32)],
        scratch_shapes=[pltpu.VMEM((CHUNK, AH), F32), pltpu.VMEM((CHUNK, AH), F32)],
        compiler_params=_cp(("arbitrary",)))(zpre, dy, vg, vb, ws, bsf)


SLAB = 256
NSLAB = DM // SLAB
RC = 256
PAD = 32


def _col(T, j):
    return pl.BlockSpec((T, SLAB), lambda c: (0, j * NSLAB + c))


def _chunks(T, fn):
    def step(i, carry):
        fn(pl.multiple_of(i * RC, RC))
        return carry
    lax.fori_loop(0, T // RC, step, 0)


def _conv_taps(K):
    return [(r, [q for q in range(4) if 8 * q + r < K]) for r in range(min(8, K))]


def _causal_conv(zpad_ref, wrow, K, r0):
    acc = None
    for r, qs in _conv_taps(K):
        a = None
        for q in qs:
            term = wrow(8 * q + r) * zpad_ref[pl.ds(r0 + (PAD - 8 - 8 * q), RC + 8), :]
            a = term if a is None else a + term
        a = a if r == 0 else pltpu.roll(a, r, 0)
        acc = a if acc is None else acc + a
    return acc[8:, :]


def _anticausal_conv(gpad_ref, wrow, K, r0):
    acc = None
    for r, qs in _conv_taps(K):
        b = None
        for q in qs:
            term = wrow(8 * q + r) * gpad_ref[pl.ds(r0 + 8 * q, RC + 8), :]
            b = term if b is None else b + term
        b = b if r == 0 else pltpu.roll(b, RC + 8 - r, 0)
        acc = b if acc is None else acc + b
    return acc[:RC, :]


def _conv_dw(gpad_ref, zpad_ref, dw_ref, K, r0):
    for r, qs in _conv_taps(K):
        gw = gpad_ref[pl.ds(r0, RC + 8), :]
        p = (gw if r == 0 else pltpu.roll(gw, RC + 8 - r, 0))[:RC, :]
        for q in qs:
            z = zpad_ref[pl.ds(r0 + (PAD - 8 * q), RC), :]
            dw_ref[8 * q + r] += jnp.sum((p * z).reshape(RC // 8, 8, SLAB), axis=0)


def _zero_rows(ref, start, n):
    ref[pl.ds(start, n), :] = jnp.zeros((n, SLAB), F32)


def pool_fwd(hn, wg, sc, name):
    T = hn.shape[0]

    def body(h_ref, w_ref, s_ref, p_ref, yp_ref, y_ref, xpad):
        g = pl.program_id(0)
        wf = jnp.left_shift(2, g).astype(F32)
        _zero_rows(xpad, 0, PAD)

        def fill(r0):
            xpad[pl.ds(r0 + PAD, RC), :] = h_ref[pl.ds(r0, RC), :]
        _chunks(T, fill)

        def step(r0):
            w = xpad[pl.ds(r0 + (PAD - 16), RC + 16), :]
            s2 = w + pltpu.roll(w, 1, 0)
            s4 = s2 + pltpu.roll(s2, 2, 0)
            s8 = s4 + pltpu.roll(s4, 4, 0)
            s16 = s8 + pltpu.roll(s8, 8, 0)
            sel = jnp.where(g == 0, s2, jnp.where(g == 1, s4, jnp.where(g == 2, s8, s16)))[16:, :]
            t1 = (r0 + 1 + lax.broadcasted_iota(jnp.int32, (RC, SLAB), 0)).astype(F32)
            pooled = (sel / jnp.minimum(t1, wf) - w[16:, :]).astype(BF16)
            p_ref[pl.ds(r0, RC), :] = pooled
            yp = _dot_nn(pooled, w_ref[...])
            yp_ref[pl.ds(r0, RC), :] = yp
            y_ref[pl.ds(r0, RC), :] = yp * s_ref[...]
        _chunks(T, step)

    slab = pl.BlockSpec((T, SLAB), lambda c: (0, c))
    return pl.pallas_call(
        body, name=name, grid=(NSLAB,),
        in_specs=[slab, pl.BlockSpec((None, SLAB, SLAB), lambda c: (c, 0, 0)), pl.BlockSpec((1, SLAB), lambda c: (0, c))],
        out_specs=[slab, slab, slab],
        out_shape=[jax.ShapeDtypeStruct((T, DM), BF16), jax.ShapeDtypeStruct((T, DM), F32),
                   jax.ShapeDtypeStruct((T, DM), F32)],
        scratch_shapes=[pltpu.VMEM((T + PAD, SLAB), F32)],
        compiler_params=_cp(("parallel",)))(hn, wg, sc)


def pool_bwd(dy, ypre, pooled, wg, sc, name):
    T = dy.shape[0]

    def body(d_ref, yp_ref, p_ref, w_ref, s_ref, dh_ref, dw_ref, ds_ref, qpad, dwacc, dsacc):
        g = pl.program_id(0)
        wf = jnp.left_shift(2, g).astype(F32)
        dwacc[...] = jnp.zeros_like(dwacc)
        dsacc[...] = jnp.zeros_like(dsacc)
        _zero_rows(qpad, T, PAD)

        def first(r0):
            d = d_ref[pl.ds(r0, RC), :]
            dsacc[...] += jnp.sum((d * yp_ref[pl.ds(r0, RC), :]).reshape(RC // 8, 8, SLAB), axis=0)
            dyp = (d * s_ref[...]).astype(BF16)
            dpool = _dot_nt(dyp, w_ref[...])
            dwacc[...] += _dot_tn(p_ref[pl.ds(r0, RC), :], dyp)
            t1 = (r0 + 1 + lax.broadcasted_iota(jnp.int32, (RC, SLAB), 0)).astype(F32)
            qpad[pl.ds(r0, RC), :] = dpool / jnp.minimum(t1, wf)
            dh_ref[pl.ds(r0, RC), :] = dpool
        _chunks(T, first)

        def second(r0):
            w = qpad[pl.ds(r0, RC + 16), :]
            n = RC + 16
            a2 = w + pltpu.roll(w, n - 1, 0)
            a4 = a2 + pltpu.roll(a2, n - 2, 0)
            a8 = a4 + pltpu.roll(a4, n - 4, 0)
            a16 = a8 + pltpu.roll(a8, n - 8, 0)
            sel = jnp.where(g == 0, a2, jnp.where(g == 1, a4, jnp.where(g == 2, a8, a16)))[:RC, :]
            dh_ref[pl.ds(r0, RC), :] = sel - dh_ref[pl.ds(r0, RC), :]
        _chunks(T, second)
        dw_ref[...] = dwacc[...]
        ds_ref[...] = jnp.sum(dsacc[...], axis=0, keepdims=True)

    slab = pl.BlockSpec((T, SLAB), lambda c: (0, c))
    wspec = pl.BlockSpec((None, SLAB, SLAB), lambda c: (c, 0, 0))
    vec = pl.BlockSpec((1, SLAB), lambda c: (0, c))
    return pl.pallas_call(
        body, name=name, grid=(NSLAB,),
        in_specs=[slab, slab, slab, wspec, vec],
        out_specs=[slab, wspec, vec],
        out_shape=[jax.ShapeDtypeStruct((T, DM), F32), jax.ShapeDtypeStruct((NSLAB, SLAB, SLAB), F32),
                   jax.ShapeDtypeStruct((1, DM), F32)],
        scratch_shapes=[pltpu.VMEM((T + PAD, SLAB), F32), pltpu.VMEM((SLAB, SLAB), F32), pltpu.VMEM((8, SLAB), F32)],
        compiler_params=_cp(("parallel",)))(dy, ypre, pooled, wg, sc)


KC = 31
KD = 3


def conf_conv_fwd(ag, wdw, bdw, name):
    T = ag.shape[0]

    def body(a_ref, g_ref, w_ref, b_ref, o_ref, zpad):
        _zero_rows(zpad, 0, PAD)

        def fill(r0):
            a = a_ref[pl.ds(r0, RC), :].astype(F32)
            gt = g_ref[pl.ds(r0, RC), :].astype(F32)
            zpad[pl.ds(r0 + PAD, RC), :] = a * _sig(gt)
        _chunks(T, fill)
        wrow = lambda j: w_ref[KC - 1 - j:KC - j, :]

        def step(r0):
            o_ref[pl.ds(r0, RC), :] = _causal_conv(zpad, wrow, KC, r0) + b_ref[...]
        _chunks(T, step)

    vec = pl.BlockSpec((1, SLAB), lambda c: (0, c))
    return pl.pallas_call(
        body, name=name, grid=(NSLAB,),
        in_specs=[_col(T, 0), _col(T, 1), pl.BlockSpec((32, SLAB), lambda c: (0, c)), vec],
        out_specs=pl.BlockSpec((T, SLAB), lambda c: (0, c)),
        out_shape=jax.ShapeDtypeStruct((T, DM), F32),
        scratch_shapes=[pltpu.VMEM((T + PAD, SLAB), F32)],
        compiler_params=_cp(("parallel",)))(ag, ag, wdw, bdw)


def conf_conv_bwd(dzc, ag, wdw, name):
    T = ag.shape[0]

    def body(d_ref, a_ref, g_ref, w_ref, da_ref, dg_ref, dw_ref, db_ref, zpad, gpad, dwacc, dbacc):
        _zero_rows(zpad, 0, PAD)
        _zero_rows(gpad, T, PAD)
        dwacc[...] = jnp.zeros_like(dwacc)
        dbacc[...] = jnp.zeros_like(dbacc)

        def fill(r0):
            a = a_ref[pl.ds(r0, RC), :].astype(F32)
            gt = g_ref[pl.ds(r0, RC), :].astype(F32)
            zpad[pl.ds(r0 + PAD, RC), :] = a * _sig(gt)
            d = d_ref[pl.ds(r0, RC), :]
            gpad[pl.ds(r0, RC), :] = d
            dbacc[...] += jnp.sum(d.reshape(RC // 8, 8, SLAB), axis=0)
        _chunks(T, fill)
        wrow = lambda j: w_ref[KC - 1 - j:KC - j, :]

        def step(r0):
            dz = _anticausal_conv(gpad, wrow, KC, r0)
            a = a_ref[pl.ds(r0, RC), :].astype(F32)
            s = _sig(g_ref[pl.ds(r0, RC), :].astype(F32))
            da_ref[pl.ds(r0, RC), :] = (dz * s).astype(da_ref.dtype)
            dg_ref[pl.ds(r0, RC), :] = (dz * a * s * (1.0 - s)).astype(dg_ref.dtype)
            _conv_dw(gpad, zpad, dwacc, KC, r0)
        _chunks(T, step)
        dw_ref[...] = jnp.zeros_like(dw_ref)
        for k in range(KC):
            dw_ref[k:k + 1, :] = jnp.sum(dwacc[KC - 1 - k], axis=0, keepdims=True)
        db_ref[...] = jnp.sum(dbacc[...], axis=0, keepdims=True)

    vec = pl.BlockSpec((1, SLAB), lambda c: (0, c))
    w32 = pl.BlockSpec((32, SLAB), lambda c: (0, c))
    return pl.pallas_call(
        body, name=name, grid=(NSLAB,),
        in_specs=[pl.BlockSpec((T, SLAB), lambda c: (0, c)), _col(T, 0), _col(T, 1), w32],
        out_specs=[_col(T, 0), _col(T, 0), w32, vec],
        out_shape=[jax.ShapeDtypeStruct((T, DM), BF16), jax.ShapeDtypeStruct((T, DM), BF16),
                   jax.ShapeDtypeStruct((32, DM), F32), jax.ShapeDtypeStruct((1, DM), F32)],
        scratch_shapes=[pltpu.VMEM((T + PAD, SLAB), F32), pltpu.VMEM((T + PAD, SLAB), F32),
                        pltpu.VMEM((32, 8, SLAB), F32), pltpu.VMEM((8, SLAB), F32)],
        compiler_params=_cp(("parallel",)))(dzc, ag, ag, wdw)


def conf_ln_fwd(zc, g, b, name):
    T, C = zc.shape
    tm = min(512, T)

    def body(z_ref, g_ref, b_ref, o_ref):
        x = z_ref[...]
        xc = x - jnp.mean(x, axis=-1, keepdims=True)
        r = lax.rsqrt(jnp.mean(xc * xc, axis=-1, keepdims=True) + EPS)
        zl = xc * r * g_ref[...] + b_ref[...]
        o_ref[...] = (zl * _sig(zl)).astype(o_ref.dtype)

    return pl.pallas_call(
        body, name=name, grid=(T // tm,), in_specs=[_rows(tm, C), _vec(C), _vec(C)], out_specs=_rows(tm, C),
        out_shape=jax.ShapeDtypeStruct((T, C), BF16), compiler_params=_cp(("parallel",)))(zc, g, b)


def conf_ln_bwd(dzs, zc, g, b, name):
    T, C = zc.shape
    tm = min(512, T)

    def body(d_ref, z_ref, g_ref, b_ref, o_ref, dg_ref, db_ref):
        i = pl.program_id(0)
        x = z_ref[...]
        xc = x - jnp.mean(x, axis=-1, keepdims=True)
        r = lax.rsqrt(jnp.mean(xc * xc, axis=-1, keepdims=True) + EPS)
        xh = xc * r
        zl = xh * g_ref[...] + b_ref[...]
        s = _sig(zl)
        dzl = d_ref[...].astype(F32) * (s * (1.0 + zl * (1.0 - s)))
        t = dzl * g_ref[...]
        o_ref[...] = r * (t - jnp.mean(t, axis=-1, keepdims=True) - xh * jnp.mean(t * xh, axis=-1, keepdims=True))
        _acc_rows(dg_ref, i, jnp.sum(dzl * xh, axis=0, keepdims=True))
        _acc_rows(db_ref, i, jnp.sum(dzl, axis=0, keepdims=True))

    return pl.pallas_call(
        body, name=name, grid=(T // tm,), in_specs=[_rows(tm, C), _rows(tm, C), _vec(C), _vec(C)],
        out_specs=[_rows(tm, C), _vec(C), _vec(C)],
        out_shape=[jax.ShapeDtypeStruct((T, C), F32), jax.ShapeDtypeStruct((1, C), F32),
                   jax.ShapeDtypeStruct((1, C), F32)],
        compiler_params=_cp(("arbitrary",)))(dzs, zc, g, b)


def sconv_fwd(bgx, wc, name):
    T = bgx.shape[0]

    def body(b_ref, c_ref, x_ref, w_ref, o_ref, zpad):
        _zero_rows(zpad, 0, PAD)

        def fill(r0):
            zpad[pl.ds(r0 + PAD, RC), :] = c_ref[pl.ds(r0, RC), :].astype(F32) * x_ref[pl.ds(r0, RC), :].astype(F32)
        _chunks(T, fill)
        wrow = lambda j: w_ref[KD - 1 - j:KD - j, :]

        def step(r0):
            qc = _causal_conv(zpad, wrow, KD, r0)
            o_ref[pl.ds(r0, RC), :] = (b_ref[pl.ds(r0, RC), :].astype(F32) * qc).astype(o_ref.dtype)
        _chunks(T, step)

    return pl.pallas_call(
        body, name=name, grid=(NSLAB,),
        in_specs=[_col(T, 0), _col(T, 1), _col(T, 2), pl.BlockSpec((8, SLAB), lambda c: (0, c))],
        out_specs=pl.BlockSpec((T, SLAB), lambda c: (0, c)),
        out_shape=jax.ShapeDtypeStruct((T, DM), BF16),
        scratch_shapes=[pltpu.VMEM((T + PAD, SLAB), F32)],
        compiler_params=_cp(("parallel",)))(bgx, bgx, bgx, wc)


def sconv_bwd(dy, bgx, wc, name):
    T = bgx.shape[0]

    def body(d_ref, b_ref, c_ref, x_ref, w_ref, db_ref, dc_ref, dx_ref, dw_ref, zpad, gpad, dwacc):
        _zero_rows(zpad, 0, PAD)
        _zero_rows(gpad, T, PAD)
        dwacc[...] = jnp.zeros_like(dwacc)

        def fill(r0):
            zpad[pl.ds(r0 + PAD, RC), :] = c_ref[pl.ds(r0, RC), :].astype(F32) * x_ref[pl.ds(r0, RC), :].astype(F32)
            gpad[pl.ds(r0, RC), :] = d_ref[pl.ds(r0, RC), :].astype(F32) * b_ref[pl.ds(r0, RC), :].astype(F32)
        _chunks(T, fill)
        wrow = lambda j: w_ref[KD - 1 - j:KD - j, :]

        def step(r0):
            qc = _causal_conv(zpad, wrow, KD, r0)
            db_ref[pl.ds(r0, RC), :] = (d_ref[pl.ds(r0, RC), :].astype(F32) * qc).astype(db_ref.dtype)
            dq = _anticausal_conv(gpad, wrow, KD, r0)
            dc_ref[pl.ds(r0, RC), :] = (dq * x_ref[pl.ds(r0, RC), :].astype(F32)).astype(dc_ref.dtype)
            dx_ref[pl.ds(r0, RC), :] = (dq * c_ref[pl.ds(r0, RC), :].astype(F32)).astype(dx_ref.dtype)
            _conv_dw(gpad, zpad, dwacc, KD, r0)
        _chunks(T, step)
        dw_ref[...] = jnp.zeros_like(dw_ref)
        for k in range(KD):
            dw_ref[k:k + 1, :] = jnp.sum(dwacc[KD - 1 - k], axis=0, keepdims=True)

    w8 = pl.BlockSpec((8, SLAB), lambda c: (0, c))
    return pl.pallas_call(
        body, name=name, grid=(NSLAB,),
        in_specs=[pl.BlockSpec((T, SLAB), lambda c: (0, c)), _col(T, 0), _col(T, 1), _col(T, 2), w8],
        out_specs=[_col(T, 0), _col(T, 0), _col(T, 0), w8],
        out_shape=[jax.ShapeDtypeStruct((T, DM), BF16)] * 3 + [jax.ShapeDtypeStruct((8, DM), F32)],
        scratch_shapes=[pltpu.VMEM((T + PAD, SLAB), F32), pltpu.VMEM((T + PAD, SLAB), F32),
                        pltpu.VMEM((8, 8, SLAB), F32)],
        compiler_params=_cp(("parallel",)))(dy, bgx, bgx, bgx, wc)


def merge_cols(parts, name):
    T = parts[0].shape[0]
    n = len(parts)
    C = n * DM
    tm = min(512, T)

    def body(*refs):
        o_ref = refs[n]
        for j in range(n):
            o_ref[:, j * DM:(j + 1) * DM] = refs[j][...]

    return pl.pallas_call(
        body, name=name, grid=(T // tm,),
        in_specs=[_rows(tm, DM) for j in range(n)],
        out_specs=_rows(tm, C), out_shape=jax.ShapeDtypeStruct((T, C), parts[0].dtype),
        compiler_params=_cp(("parallel",)))(*parts)


def adamw(w, g, m, v, name):
    shape = w.shape
    C = shape[-1]
    R = w.size // C
    w2, g2, m2, v2 = (a.reshape(R, C) for a in (w, g, m, v))
    tr = R
    while tr * C > 512 * 1024 and tr % 16 == 0:
        tr //= 2
    bc1 = 1.0 - ADAM_B1 ** ADAM_STEP
    bc2 = 1.0 - ADAM_B2 ** ADAM_STEP

    def body(w_ref, g_ref, m_ref, v_ref, d_ref, nm_ref, nv_ref):
        gg = g_ref[...]
        nm = ADAM_B1 * m_ref[...] + (1.0 - ADAM_B1) * gg
        nv = ADAM_B2 * v_ref[...] + (1.0 - ADAM_B2) * (gg * gg)
        nm_ref[...] = nm
        nv_ref[...] = nv
        d_ref[...] = -ADAM_LR * ((nm / bc1) / (jnp.sqrt(nv / bc2) + ADAM_EPS) + ADAM_WD * w_ref[...])

    spec = pl.BlockSpec((tr, C), lambda i: (i, 0))
    outs = pl.pallas_call(
        body, name=name, grid=(R // tr,), in_specs=[spec] * 4, out_specs=[spec] * 3,
        out_shape=[jax.ShapeDtypeStruct((R, C), F32)] * 3, compiler_params=_cp(("parallel",)))(w2, g2, m2, v2)
    return tuple(o.reshape(shape) for o in outs)


def _place():
    x, y, c = lax.axis_index("x"), lax.axis_index("y"), lax.axis_index("c")
    return x, y, c


def all_gather_chips(bufs, name):
    n = len(bufs)
    me_ = 2 * lax.axis_index("x") + lax.axis_index("y")
    slots = [lax.dynamic_update_slice(lax.empty((NCHIP,) + b.shape, b.dtype), b[None], (me_, 0, 0)) for b in bufs]

    def body(*refs):
        dst = refs[n:2 * n]
        send, recv = refs[2 * n:]
        x, y, c = _place()
        me = 2 * x + y
        sib = (x, y, 1 - c)
        chips = [(1 - x, y), (x, 1 - y), (1 - x, 1 - y)]

        def half(b, slot, hc):
            rows = bufs[b].shape[0] // 2
            return dst[b].at[slot, pl.ds(hc * rows, rows), :]

        def remote(k, s, d, to):
            return pltpu.make_async_remote_copy(src_ref=s, dst_ref=d, send_sem=send.at[k], recv_sem=recv.at[k],
                                                device_id=to, device_id_type=MESH)

        first = []
        for b in range(n):
            for j, (cx, cy) in enumerate(chips):
                first.append(remote(b * 6 + j, half(b, me, c), half(b, me, c), (cx, cy, c)))
        for cp in first:
            cp.start()
        passed = []
        for b in range(n):
            for j, (cx, cy) in enumerate(chips):
                slot = 2 * cx + cy
                remote(b * 6 + j, half(b, slot, c), half(b, slot, c), (cx, cy, c)).wait_recv()
                fwd = remote(b * 6 + 3 + j, half(b, slot, c), half(b, slot, c), sib)
                fwd.start()
                passed.append(fwd)
        for b in range(n):
            for j, (cx, cy) in enumerate(chips):
                slot = 2 * cx + cy
                remote(b * 6 + 3 + j, half(b, slot, 1 - c), half(b, slot, 1 - c), sib).wait_recv()
        for cp in first + passed:
            cp.wait_send()

    return pl.pallas_call(
        body, name=name, in_specs=[ANY] * n, out_specs=[ANY] * n,
        out_shape=[jax.ShapeDtypeStruct(s.shape, s.dtype) for s in slots],
        input_output_aliases={b: b for b in range(n)},
        scratch_shapes=[pltpu.SemaphoreType.DMA((6 * n,)), pltpu.SemaphoreType.DMA((6 * n,))],
        compiler_params=pltpu.CompilerParams())(*slots)


def pair_exchange(bufs, name):
    n = len(bufs)

    def body(*refs):
        src, dst = refs[:n], refs[n:2 * n]
        send, recv = refs[2 * n:]
        x, y, c = _place()
        cps = []
        for b in range(n):
            rows = bufs[b].shape[1] // 2
            cp = pltpu.make_async_remote_copy(
                src_ref=src[b].at[:, pl.ds((1 - c) * rows, rows), :], dst_ref=dst[b],
                send_sem=send.at[b], recv_sem=recv.at[b], device_id=(x, y, 1 - c), device_id_type=MESH)
            cp.start()
            cps.append(cp)
        for cp in cps:
            cp.wait()

    return pl.pallas_call(
        body, name=name, in_specs=[ANY] * n, out_specs=[ANY] * n,
        out_shape=[jax.ShapeDtypeStruct((NCHIP, b.shape[1] // 2, b.shape[2]), b.dtype) for b in bufs],
        scratch_shapes=[pltpu.SemaphoreType.DMA((n,)), pltpu.SemaphoreType.DMA((n,))],
        compiler_params=pltpu.CompilerParams())(*bufs)


def add_half(full, got, tr, tc, name):
    _, R, C = full.shape
    rows = R // 2
    nr = rows // tr
    c_arr = lax.axis_index("c").astype(jnp.int32).reshape(1)

    def body(c_ref, a_ref, b_ref, o_ref):
        o_ref[...] = (a_ref[...].astype(F32) + b_ref[...].astype(F32)).astype(o_ref.dtype)

    return pl.pallas_call(
        body, name=name,
        grid_spec=pltpu.PrefetchScalarGridSpec(
            num_scalar_prefetch=1, grid=(NCHIP, nr, C // tc),
            in_specs=[pl.BlockSpec((None, tr, tc), lambda s, i, j, c_ref: (s, c_ref[0] * nr + i, j)),
                      pl.BlockSpec((None, tr, tc), lambda s, i, j, c_ref: (s, i, j))],
            out_specs=pl.BlockSpec((None, tr, tc), lambda s, i, j, c_ref: (s, i, j))),
        out_shape=jax.ShapeDtypeStruct((NCHIP, rows, C), full.dtype),
        compiler_params=_cp(("parallel", "parallel", "parallel")))(c_arr, full, got)


def chip_exchange(bufs, name):
    n = len(bufs)

    def body(*refs):
        src, dst = refs[:n], refs[n:2 * n]
        send, recv, lsem = refs[2 * n:]
        x, y, c = _place()
        me = 2 * x + y
        chips = [(1 - x, y), (x, 1 - y), (1 - x, 1 - y)]
        local = [pltpu.make_async_copy(src[b].at[me], dst[b].at[me], lsem.at[b]) for b in range(n)]
        for cp in local:
            cp.start()
        cps = []
        for b in range(n):
            for j, (cx, cy) in enumerate(chips):
                cp = pltpu.make_async_remote_copy(
                    src_ref=src[b].at[2 * cx + cy], dst_ref=dst[b].at[me],
                    send_sem=send.at[b * 3 + j], recv_sem=recv.at[b * 3 + j],
                    device_id=(cx, cy, c), device_id_type=MESH)
                cp.start()
                cps.append((cp, b, cx, cy, j))
        for cp, b, cx, cy, j in cps:
            cp.wait_send()
            pltpu.make_async_remote_copy(
                src_ref=src[b].at[me], dst_ref=dst[b].at[2 * cx + cy],
                send_sem=send.at[b * 3 + j], recv_sem=recv.at[b * 3 + j],
                device_id=(cx, cy, c), device_id_type=MESH).wait_recv()
        for cp in local:
            cp.wait()

    return pl.pallas_call(
        body, name=name, in_specs=[ANY] * n, out_specs=[ANY] * n,
        out_shape=[jax.ShapeDtypeStruct(b.shape, b.dtype) for b in bufs],
        scratch_shapes=[pltpu.SemaphoreType.DMA((3 * n,)), pltpu.SemaphoreType.DMA((3 * n,)),
                        pltpu.SemaphoreType.DMA((n,))],
        compiler_params=pltpu.CompilerParams())(*bufs)


def sum_slots(buf, tr, tc, name):
    _, r, C = buf.shape
    nr = r // tr
    c_arr = lax.axis_index("c").astype(jnp.int32).reshape(1)

    def body(c_ref, a_ref, o_ref):
        o_ref[...] = ((a_ref[0].astype(F32) + a_ref[1].astype(F32)) + a_ref[2].astype(F32)) + a_ref[3].astype(F32)

    return pl.pallas_call(
        body, name=name,
        grid_spec=pltpu.PrefetchScalarGridSpec(
            num_scalar_prefetch=1, grid=(nr, C // tc),
            in_specs=[pl.BlockSpec((NCHIP, tr, tc), lambda i, j, c_ref: (0, i, j))],
            out_specs=pl.BlockSpec((tr, tc), lambda i, j, c_ref: (c_ref[0] * nr + i, j))),
        out_shape=jax.ShapeDtypeStruct((2 * r, C), F32),
        compiler_params=_cp(("parallel", "parallel")))(c_arr, buf)


def pair_share(bufs, name):
    n = len(bufs)

    def body(*refs):
        dst = refs[n:2 * n]
        send, recv = refs[2 * n:]
        x, y, c = _place()
        cps = []
        for b in range(n):
            rows = bufs[b].shape[0] // 2
            here = dst[b].at[pl.ds(c * rows, rows), :]
            cp = pltpu.make_async_remote_copy(src_ref=here, dst_ref=here, send_sem=send.at[b], recv_sem=recv.at[b],
                                              device_id=(x, y, 1 - c), device_id_type=MESH)
            cp.start()
            cps.append((cp, b))
        for cp, b in cps:
            rows = bufs[b].shape[0] // 2
            there = dst[b].at[pl.ds((1 - c) * rows, rows), :]
            cp.wait_send()
            pltpu.make_async_remote_copy(src_ref=there, dst_ref=there, send_sem=send.at[b], recv_sem=recv.at[b],
                                         device_id=(x, y, 1 - c), device_id_type=MESH).wait_recv()

    return pl.pallas_call(
        body, name=name, in_specs=[ANY] * n, out_specs=[ANY] * n,
        out_shape=[jax.ShapeDtypeStruct(b.shape, b.dtype) for b in bufs],
        input_output_aliases={b: b for b in range(n)},
        scratch_shapes=[pltpu.SemaphoreType.DMA((n,)), pltpu.SemaphoreType.DMA((n,))],
        compiler_params=pltpu.CompilerParams())(*bufs)


def _tile(kind, buf):
    return {"A": (64, buf.shape[2]), "B": (128, 1024), "C": (128, 256), "V": (40, 256), "E": (40, 1024)}[kind]


def reduce_scatter(parts, tag):
    names = list(parts)
    got = pair_exchange([parts[k] for k in names], tag + "_pair_exchange")
    sums = [add_half(parts[k], got[i], *_tile(k[0], got[i]), name=tag + "_add_pair_" + k) for i, k in enumerate(names)]
    landed = chip_exchange(sums, tag + "_chip_exchange")
    halves = [sum_slots(landed[i], *_tile(k[0], landed[i]), name=tag + "_sum_chips_" + k) for i, k in enumerate(names)]
    full = pair_share(halves, tag + "_pair_share")
    return dict(zip(names, full))


HBM = pl.BlockSpec(memory_space=pltpu.HBM)
SEMS = pl.BlockSpec(memory_space=pltpu.SEMAPHORE)
FLOWS = pltpu.SideEffectType.DATAFLOW_SIDE_EFFECTING


def _in_hbm(a):
    return pltpu.with_memory_space_constraint(a, pltpu.HBM)


def _other_chips():
    x, y, c = _place()
    return 2 * x + y, c, [(1 - x, y), (x, 1 - y), (1 - x, 1 - y)]


def own_slots(bufs):
    me = 2 * lax.axis_index("x") + lax.axis_index("y")
    return [lax.dynamic_update_slice(lax.empty((NCHIP,) + b.shape, b.dtype), b[None], (me, 0, 0)) for b in bufs]


def gather_start(slots, name):
    n = len(slots)

    def body(*refs):
        ins = refs[:n]
        send, recv = refs[n], refs[n + 1]
        token = refs[2 * n + 2]
        me, c, chips = _other_chips()
        for b in range(n):
            rows = slots[b].shape[1] // 2
            own = ins[b].at[me, pl.ds(c * rows, rows), :]
            for j, (cx, cy) in enumerate(chips):
                pltpu.make_async_remote_copy(src_ref=own, dst_ref=own, send_sem=send.at[3 * b + j],
                                             recv_sem=recv.at[3 * b + j], device_id=(cx, cy, c),
                                             device_id_type=MESH).start()
        token[...] = jnp.zeros_like(token)

    out = pl.pallas_call(
        body, name=name, in_specs=[HBM] * n,
        out_specs=[SEMS, SEMS] + [HBM] * n + [pl.BlockSpec(memory_space=pltpu.VMEM)],
        out_shape=[pltpu.SemaphoreType.DMA((3 * n,)), pltpu.SemaphoreType.DMA((3 * n,))]
        + [pltpu.HBM(s.shape, s.dtype) for s in slots] + [jax.ShapeDtypeStruct((8, 128), F32)],
        input_output_aliases={b: b + 2 for b in range(n)},
        compiler_params=pltpu.CompilerParams(has_side_effects=FLOWS))(*[_in_hbm(s) for s in slots])
    return out[0], out[1], list(out[2:2 + n]), out[2 + n]


def gather_wait(send, recv, slots, picks, after, name):
    n = len(slots)

    def body(*refs):
        ins = refs[:n]
        send_, recv_ = refs[n], refs[n + 1]
        me, c, chips = _other_chips()
        for i, b in enumerate(picks):
            rows = slots[i].shape[1] // 2
            own = ins[i].at[me, pl.ds(c * rows, rows), :]
            for j, (cx, cy) in enumerate(chips):
                got = ins[i].at[2 * cx + cy, pl.ds(c * rows, rows), :]
                pltpu.make_async_remote_copy(src_ref=own, dst_ref=own, send_sem=send_.at[3 * b + j],
                                             recv_sem=recv_.at[3 * b + j], device_id=(cx, cy, c),
                                             device_id_type=MESH).wait_send()
                pltpu.make_async_remote_copy(src_ref=got, dst_ref=got, send_sem=send_.at[3 * b + j],
                                             recv_sem=recv_.at[3 * b + j], device_id=(cx, cy, c),
                                             device_id_type=MESH).wait_recv()

    return pl.pallas_call(
        body, name=name, in_specs=[HBM] * n + [SEMS, SEMS, ANY], out_specs=[HBM] * n,
        out_shape=[pltpu.HBM(s.shape, s.dtype) for s in slots],
        input_output_aliases={b: b for b in range(n)},
        compiler_params=pltpu.CompilerParams(has_side_effects=FLOWS))(*slots, send, recv, after)


def gather_pass(slots, name):
    n = len(slots)

    def body(*refs):
        dst = refs[n:2 * n]
        send, recv = refs[2 * n:]
        x, y, c = _place()
        sib = (x, y, 1 - c)
        chips = [(1 - x, y), (x, 1 - y), (1 - x, 1 - y)]

        def half(b, slot, hc):
            rows = slots[b].shape[1] // 2
            return dst[b].at[slot, pl.ds(hc * rows, rows), :]

        passed = []
        for b in range(n):
            for j, (cx, cy) in enumerate(chips):
                slot = 2 * cx + cy
                cp = pltpu.make_async_remote_copy(src_ref=half(b, slot, c), dst_ref=half(b, slot, c),
                                                  send_sem=send.at[3 * b + j], recv_sem=recv.at[3 * b + j],
                                                  device_id=sib, device_id_type=MESH)
                cp.start()
                passed.append(cp)
        for b in range(n):
            for j, (cx, cy) in enumerate(chips):
                slot = 2 * cx + cy
                pltpu.make_async_remote_copy(src_ref=half(b, slot, 1 - c), dst_ref=half(b, slot, 1 - c),
                                             send_sem=send.at[3 * b + j], recv_sem=recv.at[3 * b + j],
                                             device_id=sib, device_id_type=MESH).wait_recv()
        for cp in passed:
            cp.wait_send()

    return pl.pallas_call(
        body, name=name, in_specs=[ANY] * n, out_specs=[ANY] * n,
        out_shape=[jax.ShapeDtypeStruct(s.shape, s.dtype) for s in slots],
        input_output_aliases={b: b for b in range(n)},
        scratch_shapes=[pltpu.SemaphoreType.DMA((3 * n,)), pltpu.SemaphoreType.DMA((3 * n,))],
        compiler_params=pltpu.CompilerParams())(*slots)


def chip_exchange_start(sums, name):
    n = len(sums)
    me_ = 2 * lax.axis_index("x") + lax.axis_index("y")
    landing = [lax.dynamic_update_slice(lax.empty(s.shape, s.dtype),
                                        lax.dynamic_slice(s, (me_, 0, 0), (1,) + s.shape[1:]), (me_, 0, 0)) for s in sums]

    def body(*refs):
        src, land = refs[:n], refs[n:2 * n]
        send, recv = refs[2 * n], refs[2 * n + 1]
        token = refs[4 * n + 2]
        me, c, chips = _other_chips()
        for b in range(n):
            for j, (cx, cy) in enumerate(chips):
                pltpu.make_async_remote_copy(src_ref=src[b].at[2 * cx + cy], dst_ref=land[b].at[me],
                                             send_sem=send.at[3 * b + j], recv_sem=recv.at[3 * b + j],
                                             device_id=(cx, cy, c), device_id_type=MESH).start()
        token[...] = jnp.zeros_like(token)

    out = pl.pallas_call(
        body, name=name, in_specs=[HBM] * (2 * n),
        out_specs=[SEMS, SEMS] + [HBM] * (2 * n) + [pl.BlockSpec(memory_space=pltpu.VMEM)],
        out_shape=[pltpu.SemaphoreType.DMA((3 * n,)), pltpu.SemaphoreType.DMA((3 * n,))]
        + [pltpu.HBM(s.shape, s.dtype) for s in sums + landing] + [jax.ShapeDtypeStruct((8, 128), F32)],
        input_output_aliases={b: b + 2 for b in range(2 * n)},
        compiler_params=pltpu.CompilerParams(has_side_effects=FLOWS))(*[_in_hbm(s) for s in sums + landing])
    return out[0], out[1], list(out[2:2 + n]), list(out[2 + n:2 + 2 * n]), out[2 + 2 * n]


def chip_exchange_wait(send, recv, sums, landing, after, name):
    n = len(sums)

    def body(*refs):
        src, land = refs[:n], refs[n:2 * n]
        send_, recv_ = refs[2 * n], refs[2 * n + 1]
        me, c, chips = _other_chips()
        for b in range(n):
            for j, (cx, cy) in enumerate(chips):
                slot = 2 * cx + cy
                pltpu.make_async_remote_copy(src_ref=src[b].at[slot], dst_ref=land[b].at[me],
                                             send_sem=send_.at[3 * b + j], recv_sem=recv_.at[3 * b + j],
                                             device_id=(cx, cy, c), device_id_type=MESH).wait_send()
                pltpu.make_async_remote_copy(src_ref=src[b].at[me], dst_ref=land[b].at[slot],
                                             send_sem=send_.at[3 * b + j], recv_sem=recv_.at[3 * b + j],
                                             device_id=(cx, cy, c), device_id_type=MESH).wait_recv()

    out = pl.pallas_call(
        body, name=name, in_specs=[HBM] * (2 * n) + [SEMS, SEMS, ANY], out_specs=[HBM] * (2 * n),
        out_shape=[pltpu.HBM(s.shape, s.dtype) for s in sums + landing],
        input_output_aliases={b: b for b in range(2 * n)},
        compiler_params=pltpu.CompilerParams(has_side_effects=FLOWS))(*sums, *landing, send, recv, after)
    return list(out[n:])


def _row(a, l):
    return a[l:l + 1]


def local_step(x, p, tgt, small, weights_of, vecs, a_ws, a_bs, grads_ready):
    T = x.shape[0]
    bsf = jnp.broadcast_to(a_bs[:, :, None], (AG_N, CHUNK, AGW))
    vrow = lambda r: vecs[r:r + 1]
    saved = []
    W = []
    h = x
    GA = GB = GC = bgrp = None

    def ff_fwd(h, l, which, pre, post):
        j0 = 0 if which == 1 else 2
        down = B_FF1D(l) if which == 1 else B_FF2D(l)
        tag = "ff%d_l%d" % (which, l)
        xn = rms_fwd(h, _row(pre, l), BF16, tag + "_pre")
        gu, a = ff_gateup(xn, GA, A_FF(l, j0), tag + "_gateup")
        f, hn = mm_rs_post(a, GB, down, FB, FB, h, _row(post, l), 0.5, tag + "_down")
        return hn, (h, xn, gu, a, f)

    for l in range(4):
        rec = {}
        GA, GB, GC = weights_of(l, h)
        W.append((GA, GB, GC))
        if l == 1:
            bgrp = GC[:, C_BGRP:C_BGRP + 256, :].reshape(NCHIP, 4, 64, 256).transpose(1, 0, 2, 3).reshape(4, 256, 256)
        h, rec["ff1"] = ff_fwd(h, l, 1, small["ff1_pre_g"], small["ff1_post_g"])
        tag = "mix_l%d" % l
        h_in = h
        if l == 1:
            hn = rms_fwd(h, _row(small["mix_pre_g"], l), F32, tag + "_pre")
            pooled, ypre, f = pool_fwd(hn, bgrp, vrow(V_BSCALE), tag + "_pool")
            rec["mix"] = (h_in, pooled, ypre, f)
            h = post_res(h, f, _row(small["mix_post_g"], l), 1.0, tag + "_post")
        else:
            gpost = _row(small["mix_post_g"], l)
            hn = rms_fwd(h, _row(small["mix_pre_g"], l), BF16, tag + "_pre")
            if l == 0:
                zpre = mm_cs(hn, GA, A_AIN, 1536, 512, BF16, tag + "_in")
                y = gmlp_mid_fwd(zpre, small["a_v_norm_g"], small["a_v_norm_b"], a_ws, bsf, tag + "_gate")
                f, h = mm_rs_post(y, GB, B_AOUT, 768, 256, h, gpost, 1.0, tag + "_out")
                rec["mix"] = (h_in, hn, zpre, y, f)
            elif l == 2:
                ag = mm_cs(hn, GA, A_CIN, 512, 512, BF16, tag + "_pw1")
                zc = conf_conv_fwd(ag, vecs[V_CDW:V_CDW + 32], vrow(V_CBDW), tag + "_conv")
                zs = conf_ln_fwd(zc, vrow(V_CNG), vrow(V_CNB), tag + "_ln")
                f, h = mm_rs_post(zs, GB, B_CPW2, 256, 256, h, gpost, 1.0, tag + "_pw2")
                rec["mix"] = (h_in, hn, ag, zc, zs, f)
            else:
                bgx = mm_cs(hn, GA, A_DIN, 768, 256, BF16, tag + "_in")
                y = sconv_fwd(bgx, vecs[V_DCONV:V_DCONV + 8], tag + "_conv")
                f, h = mm_rs_post(y, GB, B_DOUT, 256, 256, h, gpost, 1.0, tag + "_out")
                rec["mix"] = (h_in, hn, bgx, y, f)
        h, rec["ff2"] = ff_fwd(h, l, 2, small["ff2_pre_g"], small["ff2_post_g"])
        tag = "ple_l%d" % l
        xn = rms_fwd(h, _row(small["ple_gate_norm_g"], l), BF16, tag + "_pre")
        zg = mm_rs(xn, GB, B_PLEG(l), 256, 256, tag + "_gate")
        pb = p[l].astype(BF16)
        pe = mm_cs(pb, GC, 0, 256, 256, F32, tag + "_proj", roff=C_PROJ(l))
        rec["ple"] = (h, xn, zg, pe, pb)
        h = ple_post(h, zg, pe, _row(small["ple_post_g"], l), tag + "_post")
        saved.append(rec)

    dh, loss_cols = loss_head(h, tgt, "loss_head")

    gA = gB = gC = None
    layer_grads = [None] * 4
    tok = None
    gV = {}
    gains = {k: [None] * 4 for k in ("ff1_pre_g", "ff1_post_g", "mix_pre_g", "mix_post_g", "ff2_pre_g", "ff2_post_g",
                                      "ple_gate_norm_g", "ple_post_g")}
    extra = {}

    def ff_bwd(dh, l, which, pre, post, rec):
        j0 = 0 if which == 1 else 2
        down = B_FF1D(l) if which == 1 else B_FF2D(l)
        tag = "ff%d_l%d_b" % (which, l)
        h_in, xn, gu, a, f = rec
        df, dpost, dgu = ff_bwd_down(dh, f, _row(post, l), GB, down, gu, tag + "_down")
        gB[down] = dw_rs(a, df, FB, FB, tag + "_dwdown")
        dh_in, dpre = mm_cs_t_rms(dgu, GA, A_FF(l, j0), 2 * FB, 2 * FB, h_in, _row(pre, l), dh, tag + "_gateup")
        gA[A_FF(l, j0)] = dw_cs(xn, dgu, 2 * FB, FB, tag + "_dwgateup")
        return dh_in, dpre, dpost

    for l in reversed(range(4)):
        rec = saved[l]
        GA, GB, GC = W[l]
        gA, gB, gC = {}, {}, {}
        tag = "ple_l%d_b" % l
        h_in, xn, zg, pe, pb = rec["ple"]
        gpost = _row(small["ple_post_g"], l)
        if tok is not None:
            gpost = gpost + tok
        dzg, dpe, gains["ple_post_g"][l] = ple_post_bwd(dh, zg, pe, gpost, tag + "_post")
        gC[C_PROJ(l)] = dw_cs(pb, dpe, 256, 256, tag + "_dwproj")
        dxn = mm_rs_t(dzg, GB, B_PLEG(l), 256, 256, tag + "_gate")
        gB[B_PLEG(l)] = dw_rs(xn, dzg, 256, 256, tag + "_dwgate")
        dh, gains["ple_gate_norm_g"][l] = rms_bwd(dxn, h_in, _row(small["ple_gate_norm_g"], l), dh, tag + "_pre")

        dh, gains["ff2_pre_g"][l], gains["ff2_post_g"][l] = ff_bwd(
            dh, l, 2, small["ff2_pre_g"], small["ff2_post_g"], rec["ff2"])

        tag = "mix_l%d_b" % l
        mix = rec["mix"]
        h_in, f = mix[0], mix[-1]
        if l == 1:
            _, pooled, ypre, _ = mix
            df, gains["mix_post_g"][l] = post_res_bwd(dh, f, _row(small["mix_post_g"], l), 1.0, F32, tag + "_post")
            dhn, dwg, dsc = pool_bwd(df, ypre, pooled, bgrp, vrow(V_BSCALE), tag + "_pool")
            gC[C_BGRP] = dwg.astype(BF16).reshape(4, NCHIP, 64, 256).transpose(1, 0, 2, 3).reshape(NCHIP, 256, 256)
            gV[V_BSCALE] = jnp.pad(dsc, ((0, 7), (0, 0)))
            dh, gains["mix_pre_g"][l] = rms_bwd(dhn, h_in, _row(small["mix_pre_g"], l), dh, tag + "_pre")
        else:
            gpre = _row(small["mix_pre_g"], l)
            df, gains["mix_post_g"][l] = post_res_bwd(dh, f, _row(small["mix_post_g"], l), 1.0, BF16, tag + "_post")
            if l == 0:
                _, hn, zpre, y, _ = mix
                dy = mm_rs_t(df, GB, B_AOUT, 768, 256, tag + "_out")
                gB[B_AOUT] = dw_rs(y, df, 768, 256, tag + "_dwout")
                dz, dws, dbs, dvg, dvb = gmlp_mid_bwd(zpre, dy, small["a_v_norm_g"], small["a_v_norm_b"], a_ws, bsf,
                                                      tag + "_gate")
                extra.update(a_w_s=dws, a_b_s=dbs[:, 0, :], a_v_norm_g=dvg, a_v_norm_b=dvb)
                gA[A_AIN] = dw_cs(hn, dz, 1536, 512, tag + "_dwin")
                dh, gains["mix_pre_g"][l] = mm_cs_t_rms(dz, GA, A_AIN, 1536, 512, h_in, gpre, dh, tag + "_in")
            elif l == 2:
                _, hn, ag, zc, zs, _ = mix
                dzs = mm_rs_t(df, GB, B_CPW2, 256, 256, tag + "_pw2")
                gB[B_CPW2] = dw_rs(zs, df, 256, 256, tag + "_dwpw2")
                dzc, dng, dnb = conf_ln_bwd(dzs, zc, vrow(V_CNG), vrow(V_CNB), tag + "_ln")
                da_, dg_, dwdw, dbdw = conf_conv_bwd(dzc, ag, vecs[V_CDW:V_CDW + 32], tag + "_conv")
                dag = merge_cols([da_, dg_], tag + "_merge")
                gV[V_CDW] = dwdw
                gV[V_CBDW] = jnp.pad(dbdw, ((0, 7), (0, 0)))
                gV[V_CNG] = jnp.pad(dng, ((0, 7), (0, 0)))
                gV[V_CNB] = jnp.pad(dnb, ((0, 7), (0, 0)))
                gA[A_CIN] = dw_cs(hn, dag, 512, 512, tag + "_dwpw1")
                dh, gains["mix_pre_g"][l] = mm_cs_t_rms(dag, GA, A_CIN, 512, 512, h_in, gpre, dh, tag + "_pw1")
            else:
                _, hn, bgx, y, _ = mix
                dy = mm_rs_t(df, GB, B_DOUT, 256, 256, tag + "_out")
                gB[B_DOUT] = dw_rs(y, df, 256, 256, tag + "_dwout")
                db_, dc_, dx_, dwc = sconv_bwd(dy, bgx, vecs[V_DCONV:V_DCONV + 8], tag + "_conv")
                dbgx = merge_cols([db_, dc_, dx_], tag + "_merge")
                gV[V_DCONV] = dwc
                gA[A_DIN] = dw_cs(hn, dbgx, 768, 256, tag + "_dwin")
                dh, gains["mix_pre_g"][l] = mm_cs_t_rms(dbgx, GA, A_DIN, 768, 256, h_in, gpre, dh, tag + "_in")

        dh, gains["ff1_pre_g"][l], gains["ff1_post_g"][l] = ff_bwd(
            dh, l, 1, small["ff1_pre_g"], small["ff1_post_g"], rec["ff1"])
        layer_grads[l] = (gA, gB, gC)
        tok = grads_ready(l, gA, gB, gC, dh)

    return loss_cols, dh, layer_grads, gV, gains, extra


GAIN_NAMES = ("ff1_pre_g", "ff1_post_g", "mix_pre_g", "mix_post_g", "ff2_pre_g", "ff2_post_g", "ple_gate_norm_g",
              "ple_post_g")


def _pad_rows(a, rows):
    return jnp.pad(a, ((0, rows - a.shape[0]), (0, 0)))


def kernel(x, p, ff1_pre_g, ff1_w_gate, ff1_w_up, ff1_w_down, ff1_post_g, mix_pre_g, mix_post_g, ff2_pre_g, ff2_w_gate, ff2_w_up, ff2_w_down, ff2_post_g, ple_gate_norm_g, ple_w_gate, ple_w_proj, ple_post_g, a_w_in, a_v_norm_g, a_v_norm_b, a_w_s, a_b_s, a_w_out, b_w_grp, b_scale, c_w_pw1, c_w_dw, c_b_dw, c_norm_g, c_norm_b, c_w_pw2, d_w_in, d_w_conv, d_w_out, loss_target, m_ff1_pre_g, m_ff1_w_gate, m_ff1_w_up, m_ff1_w_down, m_ff1_post_g, m_mix_pre_g, m_mix_post_g, m_ff2_pre_g, m_ff2_w_gate, m_ff2_w_up, m_ff2_w_down, m_ff2_post_g, m_ple_gate_norm_g, m_ple_w_gate, m_ple_w_proj, m_ple_post_g, m_a_w_in, m_a_v_norm_g, m_a_v_norm_b, m_a_w_s, m_a_b_s, m_a_w_out, m_b_w_grp, m_b_scale, m_c_w_pw1, m_c_w_dw, m_c_b_dw, m_c_norm_g, m_c_norm_b, m_c_w_pw2, m_d_w_in, m_d_w_conv, m_d_w_out, v_ff1_pre_g, v_ff1_w_gate, v_ff1_w_up, v_ff1_w_down, v_ff1_post_g, v_mix_pre_g, v_mix_post_g, v_ff2_pre_g, v_ff2_w_gate, v_ff2_w_up, v_ff2_w_down, v_ff2_post_g, v_ple_gate_norm_g, v_ple_w_gate, v_ple_w_proj, v_ple_post_g, v_a_w_in, v_a_v_norm_g, v_a_v_norm_b, v_a_w_s, v_a_b_s, v_a_w_out, v_b_w_grp, v_b_scale, v_c_w_pw1, v_c_w_dw, v_c_b_dw, v_c_norm_g, v_c_norm_b, v_c_w_pw2, v_d_w_in, v_d_w_conv, v_d_w_out):
    args = dict(locals())
    wnames = ["ff1_pre_g", "ff1_w_gate", "ff1_w_up", "ff1_w_down", "ff1_post_g", "mix_pre_g", "mix_post_g",
              "ff2_pre_g", "ff2_w_gate", "ff2_w_up", "ff2_w_down", "ff2_post_g", "ple_gate_norm_g", "ple_w_gate",
              "ple_w_proj", "ple_post_g", "a_w_in", "a_v_norm_g", "a_v_norm_b", "a_w_s", "a_b_s", "a_w_out",
              "b_w_grp", "b_scale", "c_w_pw1", "c_w_dw", "c_b_dw", "c_norm_g", "c_norm_b", "c_w_pw2", "d_w_in",
              "d_w_conv", "d_w_out"]

    PA, PB, PC, PV = pack_weights(args)
    G0 = all_gather_chips([PA[0], PB[0], PC[0], PV], "gather_l0")
    vecs = G0[3].transpose(1, 0, 2).reshape(V_ROWS, DM)
    flying = {}
    tok = 0.0
    for l in (1, 2, 3):
        send, recv, slots, token = gather_start(own_slots([PA[l], PB[l], PC[l]]), "gather_start_l%d" % l)
        flying[l] = (send, recv, slots)
        tok = tok + token[0, 0]

    def weights_of(l, h):
        if l == 0:
            return G0[0], G0[1], G0[2]
        send, recv, slots = flying[l]
        landed = gather_wait(send, recv, slots, list(range(3)), h, "gather_wait_l%d" % l)
        return tuple(gather_pass(landed, "gather_pass_l%d" % l))

    pending = {}
    reduced = {}

    def finish(l, after):
        send, recv, sums, landing = pending.pop(l)
        landed = chip_exchange_wait(send, recv, sums, landing, after, "rs_wait_l%d" % l)
        halves = [sum_slots(landed[i], *_tile(k, landed[i]), name="rs_sum_chips_l%d_%s" % (l, k))
                  for i, k in enumerate("ABC")]
        reduced[l] = pair_share(halves, "rs_pair_share_l%d" % l)

    def grads_ready(l, gA, gB, gC, dh):
        if l + 1 in pending:
            finish(l + 1, dh)
        if l == 0:
            return None
        parts = pack_layer_grads(l, gA, gB, gC)
        got = pair_exchange(list(parts), "rs_pair_exchange_l%d" % l)
        sums = [add_half(parts[i], got[i], *_tile(k, got[i]), name="rs_add_pair_l%d_%s" % (l, k))
                for i, k in enumerate("ABC")]
        send, recv, sums, landing, token = chip_exchange_start(sums, "rs_start_l%d" % l)
        pending[l] = (send, recv, sums, landing)
        return token[0, 0]

    small = {k: args[k] for k in GAIN_NAMES}
    small["ff1_pre_g"] = ff1_pre_g + tok
    small["a_v_norm_g"] = a_v_norm_g
    small["a_v_norm_b"] = a_v_norm_b
    loss_cols, grad_x, layer_grads, gV, gains, extra = local_step(
        x[0], p[:, 0], loss_target[0], small, weights_of, vecs, a_w_s[0], a_b_s[0], grads_ready)

    loss = lax.psum((0.5 / DM) * jnp.sum(loss_cols), ("x", "y", "c"))

    dA0, dB0, dC0 = pack_layer_grads(0, *layer_grads[0])
    dV, dE = pack_small_grads(gV, gains, extra)
    red = reduce_scatter({"A": dA0, "B": dB0, "C": dC0, "V": dV, "E": dE}, "rs_l0")
    reduced[0] = [red["A"], red["B"], red["C"]]
    (gE,) = all_gather_chips([red["E"]], "gather_replicated_grads")
    grads = unpack_grads([reduced[l][0] for l in range(4)], [reduced[l][1] for l in range(4)],
                         [reduced[l][2] for l in range(4)], red["V"], gE.reshape(E_ROWS, DM))

    deltas, new_m, new_v = {}, {}, {}
    for k in wnames:
        deltas[k], new_m[k], new_v[k] = adamw(args[k], grads[k], args["m_" + k], args["v_" + k], "adamw_" + k)
    return (loss, grad_x[None], *[grads[k] for k in wnames], *[deltas[k] for k in wnames],
            *[new_m[k] for k in wnames], *[new_v[k] for k in wnames])


def pack_weights(w):
    padc = lambda a: jnp.pad(a, ((0, 0), (0, FB - FW)))
    mix_in = [w["a_w_in"][0], None, w["c_w_pw1"][0], w["d_w_in"][0]]
    mix_out = [w["a_w_out"][0], None, w["c_w_pw2"][0], w["d_w_out"][0]]
    PA, PB, PC = [], [], []
    for l in range(4):
        cols = [padc(w["ff1_w_gate"][l]), padc(w["ff1_w_up"][l]), padc(w["ff2_w_gate"][l]), padc(w["ff2_w_up"][l])]
        rows = [_pad_rows(w["ff1_w_down"][l], FB), _pad_rows(w["ff2_w_down"][l], FB), w["ple_w_gate"][l]]
        if l != 1:
            cols.append(mix_in[l])
            rows.append(mix_out[l])
        PA.append(jnp.concatenate(cols, axis=1).astype(BF16))
        PB.append(jnp.concatenate(rows, axis=0).astype(BF16))
        proj = [w["ple_w_proj"][l]] + ([w["b_w_grp"][0].reshape(256, 256)] if l == 1 else [])
        PC.append(jnp.concatenate(proj, axis=0).astype(BF16))
    PV = jnp.concatenate([_pad_rows(w["b_scale"], 8), _pad_rows(w["c_b_dw"], 8), _pad_rows(w["c_norm_g"], 8),
                          _pad_rows(w["c_norm_b"], 8), _pad_rows(w["d_w_conv"][0], 8), _pad_rows(w["c_w_dw"][0], 40)],
                         axis=0)
    return PA, PB, PC, PV


def pack_layer_grads(l, gA, gB, gC):
    a = [gA[A_FF(l, 0)], gA[A_FF(l, 2)]] + ([gA[A_AIN]] if l != 1 else [])
    b = [gB[B_FF1D(l)], gB[B_FF2D(l)], gB[B_PLEG(l)]] + ([gB[B_AOUT]] if l != 1 else [])
    c = [gC[C_PROJ(l)]] + ([gC[C_BGRP]] if l == 1 else [])
    return jnp.concatenate(a, axis=2), jnp.concatenate(b, axis=1), jnp.concatenate(c, axis=1)


def pack_small_grads(gV, gains, extra):
    dVt = jnp.concatenate([gV[V_BSCALE], gV[V_CBDW], gV[V_CNG], gV[V_CNB], gV[V_DCONV], gV[V_CDW],
                           jnp.zeros((8, DM), F32)], axis=0)
    dV = dVt.reshape(V_ROWS, NCHIP, 256).transpose(1, 0, 2)
    rowsE = [_pad_rows(jnp.concatenate(gains[k], axis=0), 8) for k in GAIN_NAMES]
    rowsE += [_pad_rows(extra["a_v_norm_g"].reshape(3, DM), 8), _pad_rows(extra["a_v_norm_b"].reshape(3, DM), 8),
              jnp.pad(extra["a_b_s"].reshape(1536), (0, 8 * DM - 1536)).reshape(8, DM),
              extra["a_w_s"].reshape(192, DM)]
    dE = _pad_rows(jnp.concatenate(rowsE, axis=0), E_ROWS).reshape(NCHIP, E_ROWS // NCHIP, DM)
    return dV, dE


def unpack_grads(RAs, RBs, RCs, RV, gE):
    grads = {}
    RA = RB = RCc = None
    for i, k in enumerate(GAIN_NAMES):
        grads[k] = gE[8 * i:8 * i + 4]
    grads["a_v_norm_g"] = gE[64:67].reshape(1, 3072)
    grads["a_v_norm_b"] = gE[72:75].reshape(1, 3072)
    grads["a_b_s"] = gE[80:88].reshape(8 * DM)[:1536].reshape(1, 12, 128)
    grads["a_w_s"] = gE[88:280].reshape(1, 12, 128, 128)
    colA = lambda l, off, n: RAs[l][:, off:off + n]
    grads["ff1_w_gate"] = jnp.stack([colA(l, A_FF(l, 0), FW) for l in range(4)])
    grads["ff1_w_up"] = jnp.stack([colA(l, A_FF(l, 1), FW) for l in range(4)])
    grads["ff2_w_gate"] = jnp.stack([colA(l, A_FF(l, 2), FW) for l in range(4)])
    grads["ff2_w_up"] = jnp.stack([colA(l, A_FF(l, 3), FW) for l in range(4)])
    grads["a_w_in"] = colA(0, A_AIN, 1536)[None]
    grads["c_w_pw1"] = colA(2, A_CIN, 512)[None]
    grads["d_w_in"] = colA(3, A_DIN, 768)[None]
    rowB = lambda l, off, n: RBs[l][off:off + n]
    grads["ff1_w_down"] = jnp.stack([rowB(l, B_FF1D(l), FW) for l in range(4)])
    grads["ff2_w_down"] = jnp.stack([rowB(l, B_FF2D(l), FW) for l in range(4)])
    grads["ple_w_gate"] = jnp.stack([rowB(l, B_PLEG(l), 256) for l in range(4)])
    grads["a_w_out"] = rowB(0, B_AOUT, 768)[None]
    grads["c_w_pw2"] = rowB(2, B_CPW2, 256)[None]
    grads["d_w_out"] = rowB(3, B_DOUT, 256)[None]
    grads["ple_w_proj"] = jnp.stack([RCs[l][C_PROJ(l):C_PROJ(l) + 256] for l in range(4)])
    grads["b_w_grp"] = RCs[1][C_BGRP:C_BGRP + 256].reshape(1, 4, 64, 256)
    grads["b_scale"] = RV[V_BSCALE:V_BSCALE + 1]
    grads["c_b_dw"] = RV[V_CBDW:V_CBDW + 1]
    grads["c_norm_g"] = RV[V_CNG:V_CNG + 1]
    grads["c_norm_b"] = RV[V_CNB:V_CNB + 1]
    grads["d_w_conv"] = RV[V_DCONV:V_DCONV + 3][None]
    grads["c_w_dw"] = RV[V_CDW:V_CDW + 31][None]
    return grads
```

```python
import functools
import math

import jax
import jax.numpy as jnp
from jax import lax
from jax.experimental import pallas as pl
from jax.experimental.pallas import tpu as pltpu

F32, BF16 = jnp.float32, jnp.bfloat16
EPS = 1e-6
DM = 1024
FW = 704
FB = 768
NCHIP = 4
VMEM_LIMIT = 56 * 1024 * 1024
ANY = pl.BlockSpec(memory_space=pl.ANY)
MESH = pl.DeviceIdType.MESH

A_FF = lambda l, j: j * FB
A_AIN = A_CIN = A_DIN = 4 * FB
B_FF1D = lambda l: 0
B_FF2D = lambda l: FB
B_AOUT = B_CPW2 = B_DOUT = 2 * FB
B_PLEG = lambda l: 2 * FB + (768, 0, 256, 256)[l]
C_PROJ = lambda l: 0
C_BGRP = 256
V_BSCALE, V_CBDW, V_CNG, V_CNB, V_DCONV, V_CDW, V_ROWS = 0, 8, 16, 24, 32, 40, 80
E_ROWS = 320

ADAM_LR, ADAM_B1, ADAM_B2, ADAM_EPS, ADAM_WD, ADAM_STEP = 0.001, 0.9, 0.999, 1e-08, 0.01, 10


def _cp(sem):
    return pltpu.CompilerParams(dimension_semantics=sem, vmem_limit_bytes=VMEM_LIMIT)


def _sig(x):
    return 1.0 / (1.0 + jnp.exp(-x))


_GC = math.sqrt(2.0 / math.pi)


def _gelu(x):
    return 0.5 * x * (1.0 + jnp.tanh(_GC * (x + 0.044715 * x * x * x)))


def _gelu_grad(x):
    t = jnp.tanh(_GC * (x + 0.044715 * x * x * x))
    return 0.5 * (1.0 + t) + 0.5 * x * (1.0 - t * t) * _GC * (1.0 + 3.0 * 0.044715 * x * x)


def _dot_nn(a, b):
    return lax.dot_general(a, b, (((1,), (0,)), ((), ())), preferred_element_type=F32)


def _dot_nt(a, b):
    return lax.dot_general(a, b, (((1,), (1,)), ((), ())), preferred_element_type=F32)


def _dot_tn(a, b):
    return lax.dot_general(a, b, (((0,), (0,)), ((), ())), preferred_element_type=F32)


def mm_cs(x, G, off, nb, tn, out_dtype, name, roff=0):
    T, K = x.shape
    tm = min(1024, T)
    nj, ob, rb_ = nb // tn, off // tn, roff // K
    assert nb % tn == 0 and off % tn == 0 and roff % K == 0

    def body(x_ref, w_ref, o_ref):
        o_ref[...] = _dot_nn(x_ref[...], w_ref[...]).astype(o_ref.dtype)

    return pl.pallas_call(
        body, name=name, grid=(T // tm, NCHIP, nj),
        in_specs=[pl.BlockSpec((tm, K), lambda i, s, j: (i, 0)),
                  pl.BlockSpec((None, K, tn), lambda i, s, j: (s, rb_, ob + j))],
        out_specs=pl.BlockSpec((tm, tn), lambda i, s, j: (i, s * nj + j)),
        out_shape=jax.ShapeDtypeStruct((T, NCHIP * nb), out_dtype),
        compiler_params=_cp(("parallel", "arbitrary", "arbitrary")))(x, G)


def mm_cs_t(dy, G, off, nb, tn, name):
    T = dy.shape[0]
    K = G.shape[1]
    tm = min(1024, T)
    nj, ob = nb // tn, off // tn
    nk = NCHIP * nj

    def body(dy_ref, w_ref, o_ref, acc):
        k = pl.program_id(1)

        @pl.when(k == 0)
        def _():
            acc[...] = jnp.zeros_like(acc)

        acc[...] += _dot_nt(dy_ref[...], w_ref[...])

        @pl.when(k == nk - 1)
        def _():
            o_ref[...] = acc[...]

    return pl.pallas_call(
        body, name=name, grid=(T // tm, nk),
        in_specs=[pl.BlockSpec((tm, tn), lambda i, k: (i, k)),
                  pl.BlockSpec((None, K, tn), lambda i, k: (k // nj, 0, ob + k % nj))],
        out_specs=pl.BlockSpec((tm, K), lambda i, k: (i, 0)),
        out_shape=jax.ShapeDtypeStruct((T, K), F32),
        scratch_shapes=[pltpu.VMEM((tm, K), F32)],
        compiler_params=_cp(("parallel", "arbitrary")))(dy, G)


def mm_rs(a, G, off, rb, tk, name):
    T = a.shape[0]
    N = G.shape[2]
    tm = min(1024, T)
    nkk, ob = rb // tk, off // tk
    nk = NCHIP * nkk
    assert rb % tk == 0 and off % tk == 0

    def body(a_ref, w_ref, o_ref, acc):
        k = pl.program_id(1)

        @pl.when(k == 0)
        def _():
            acc[...] = jnp.zeros_like(acc)

        acc[...] += _dot_nn(a_ref[...], w_ref[...])

        @pl.when(k == nk - 1)
        def _():
            o_ref[...] = acc[...]

    return pl.pallas_call(
        body, name=name, grid=(T // tm, nk),
        in_specs=[pl.BlockSpec((tm, tk), lambda i, k: (i, k)),
                  pl.BlockSpec((None, tk, N), lambda i, k: (k // nkk, ob + k % nkk, 0))],
        out_specs=pl.BlockSpec((tm, N), lambda i, k: (i, 0)),
        out_shape=jax.ShapeDtypeStruct((T, N), F32),
        scratch_shapes=[pltpu.VMEM((tm, N), F32)],
        compiler_params=_cp(("parallel", "arbitrary")))(a, G)


def mm_rs_t(dy, G, off, rb, tk, name):
    T, N = dy.shape
    tm = min(1024, T)
    nkk, ob = rb // tk, off // tk
    nk = NCHIP * nkk

    def body(dy_ref, w_ref, o_ref):
        o_ref[...] = _dot_nt(dy_ref[...], w_ref[...]).astype(o_ref.dtype)

    return pl.pallas_call(
        body, name=name, grid=(T // tm, nk),
        in_specs=[pl.BlockSpec((tm, N), lambda i, k: (i, 0)),
                  pl.BlockSpec((None, tk, N), lambda i, k: (k // nkk, ob + k % nkk, 0))],
        out_specs=pl.BlockSpec((tm, tk), lambda i, k: (i, k)),
        out_shape=jax.ShapeDtypeStruct((T, NCHIP * rb), BF16),
        compiler_params=_cp(("parallel", "arbitrary")))(dy, G)


def mm_tn(a, b, tmm, tn, out_shape, out_map, name):
    T, M = a.shape
    N = b.shape[1]
    tt = min(2048, T)
    nt = T // tt

    def body(a_ref, b_ref, o_ref, acc):
        t = pl.program_id(2)

        @pl.when(t == 0)
        def _():
            acc[...] = jnp.zeros_like(acc)

        acc[...] += _dot_tn(a_ref[...], b_ref[...])

        @pl.when(t == nt - 1)
        def _():
            o_ref[...] = acc[...].astype(o_ref.dtype)

    return pl.pallas_call(
        body, name=name, grid=(M // tmm, N // tn, nt),
        in_specs=[pl.BlockSpec((tt, tmm), lambda i, j, t: (t, i)),
                  pl.BlockSpec((tt, tn), lambda i, j, t: (t, j))],
        out_specs=pl.BlockSpec((None, tmm, tn), lambda i, j, t: out_map(i, j)),
        out_shape=jax.ShapeDtypeStruct(out_shape, BF16),
        scratch_shapes=[pltpu.VMEM((tmm, tn), F32)],
        compiler_params=_cp(("parallel", "parallel", "arbitrary")))(a, b)


def dw_cs(x, dy, nb, tn, name):
    K = x.shape[1]
    nj = nb // tn
    return mm_tn(x, dy, K, tn, (NCHIP, K, nb), lambda i, j: (j // nj, 0, j % nj), name)


def dw_rs(a, dy, rb, tr, name):
    N = dy.shape[1]
    ni = rb // tr
    return mm_tn(a, dy, tr, N, (NCHIP, rb, N), lambda i, j: (i // ni, i % ni, 0), name)


def ff_gateup(xn, GA, off, name):
    T, K = xn.shape
    tm = min(1024, T)
    ob = off // (2 * FB)
    assert off % (2 * FB) == 0

    sub = min(512, tm)

    def body(x_ref, w_ref, gu_ref, a_ref):
        for r0 in range(0, tm, sub):
            r = _dot_nn(x_ref[r0:r0 + sub, :], w_ref[...])
            g, u = r[:, :FB], r[:, FB:]
            gu_ref[r0:r0 + sub, :] = r.astype(gu_ref.dtype)
            a_ref[r0:r0 + sub, :] = (g * _sig(g) * u).astype(a_ref.dtype)

    return pl.pallas_call(
        body, name=name, grid=(T // tm, NCHIP),
        in_specs=[pl.BlockSpec((tm, K), lambda i, s: (i, 0)),
                  pl.BlockSpec((None, K, 2 * FB), lambda i, s: (s, 0, ob))],
        out_specs=[pl.BlockSpec((tm, 2 * FB), lambda i, s: (i, s)), pl.BlockSpec((tm, FB), lambda i, s: (i, s))],
        out_shape=[jax.ShapeDtypeStruct((T, NCHIP * 2 * FB), BF16), jax.ShapeDtypeStruct((T, NCHIP * FB), BF16)],
        compiler_params=_cp(("parallel", "arbitrary")))(xn, GA)


def mm_rs_post(a, G, off, rb, tk, h, g, scale, name):
    T = a.shape[0]
    N = G.shape[2]
    tm = min(1024, T)
    sub = min(512, tm)
    nkk, ob = rb // tk, off // tk
    nk = NCHIP * nkk
    assert rb % tk == 0 and off % tk == 0

    def body(a_ref, w_ref, h_ref, g_ref, f_ref, o_ref, acc):
        k = pl.program_id(1)

        @pl.when(k == 0)
        def _():
            acc[...] = jnp.zeros_like(acc)

        acc[...] += _dot_nn(a_ref[...], w_ref[...])

        @pl.when(k == nk - 1)
        def _():
            for r0 in range(0, tm, sub):
                f = acc[r0:r0 + sub, :]
                f_ref[r0:r0 + sub, :] = f
                r = lax.rsqrt(jnp.mean(f * f, axis=-1, keepdims=True) + EPS)
                o_ref[r0:r0 + sub, :] = h_ref[r0:r0 + sub, :] + scale * (f * r * g_ref[...])

    row = pl.BlockSpec((tm, N), lambda i, k: (i, 0))
    row1 = pl.BlockSpec((tm, N), lambda i, k: (i, 0), pipeline_mode=pl.Buffered(1))
    return pl.pallas_call(
        body, name=name, grid=(T // tm, nk),
        in_specs=[pl.BlockSpec((tm, tk), lambda i, k: (i, k)),
                  pl.BlockSpec((None, tk, N), lambda i, k: (k // nkk, ob + k % nkk, 0)),
                  row1, pl.BlockSpec((1, N), lambda i, k: (0, 0))],
        out_specs=[row, row],
        out_shape=[jax.ShapeDtypeStruct((T, N), F32), jax.ShapeDtypeStruct((T, N), F32)],
        scratch_shapes=[pltpu.VMEM((tm, N), F32)],
        compiler_params=_cp(("parallel", "arbitrary")))(a, G, h, g)


def ff_bwd_down(dh, f, g, GB, down, gu, name):
    T, N = dh.shape
    tm = min(1024, T)
    sub = min(512, tm)
    ob = down // FB

    def body(d_ref, f_ref, g_ref, w_ref, gu_ref, df_ref, dg_ref, dgu_ref):
        i, s = pl.program_id(0), pl.program_id(1)

        @pl.when(s == 0)
        def _():
            dg = jnp.zeros((1, N), F32)
            for r0 in range(0, tm, sub):
                f = f_ref[r0:r0 + sub, :]
                r = lax.rsqrt(jnp.mean(f * f, axis=-1, keepdims=True) + EPS)
                d = 0.5 * d_ref[r0:r0 + sub, :]
                t = d * g_ref[...]
                df_ref[r0:r0 + sub, :] = (
                    r * t - f * (r * r * r * jnp.mean(t * f, axis=-1, keepdims=True))).astype(df_ref.dtype)
                dg = dg + jnp.sum(d * f * r, axis=0, keepdims=True)
            _acc_rows(dg_ref, i, dg)

        for r0 in range(0, tm, sub):
            da = _dot_nt(df_ref[r0:r0 + sub, :], w_ref[...])
            gt = gu_ref[r0:r0 + sub, :FB].astype(F32)
            u = gu_ref[r0:r0 + sub, FB:].astype(F32)
            sg = _sig(gt)
            dgu_ref[r0:r0 + sub, :FB] = (da * u * (sg * (1.0 + gt * (1.0 - sg)))).astype(dgu_ref.dtype)
            dgu_ref[r0:r0 + sub, FB:] = (da * (gt * sg)).astype(dgu_ref.dtype)

    row1 = pl.BlockSpec((tm, N), lambda i, s: (i, 0), pipeline_mode=pl.Buffered(1))
    row = pl.BlockSpec((tm, N), lambda i, s: (i, 0))
    vec = pl.BlockSpec((1, N), lambda i, s: (0, 0))
    return pl.pallas_call(
        body, name=name, grid=(T // tm, NCHIP),
        in_specs=[row1, row1, vec, pl.BlockSpec((None, FB, N), lambda i, s: (s, ob, 0)),
                  pl.BlockSpec((tm, 2 * FB), lambda i, s: (i, s))],
        out_specs=[row, vec, pl.BlockSpec((tm, 2 * FB), lambda i, s: (i, s))],
        out_shape=[jax.ShapeDtypeStruct((T, N), BF16), jax.ShapeDtypeStruct((1, N), F32),
                   jax.ShapeDtypeStruct((T, NCHIP * 2 * FB), BF16)],
        compiler_params=_cp(("arbitrary", "arbitrary")))(dh, f, g, GB, gu)


def mm_cs_t_rms(dy, G, off, nb, tn, h, g, skip, name):
    T = dy.shape[0]
    K = G.shape[1]
    tm = min(1024, T)
    sub = min(512, tm)
    nj, ob = nb // tn, off // tn
    nk = NCHIP * nj
    assert nb % tn == 0 and off % tn == 0

    def body(dy_ref, w_ref, h_ref, g_ref, s_ref, o_ref, dg_ref, acc):
        i, k = pl.program_id(0), pl.program_id(1)

        @pl.when(k == 0)
        def _():
            acc[...] = jnp.zeros_like(acc)

        acc[...] += _dot_nt(dy_ref[...], w_ref[...])

        @pl.when(k == nk - 1)
        def _():
            dg = jnp.zeros((1, K), F32)
            for r0 in range(0, tm, sub):
                d = acc[r0:r0 + sub, :]
                x = h_ref[r0:r0 + sub, :]
                r = lax.rsqrt(jnp.mean(x * x, axis=-1, keepdims=True) + EPS)
                xh = x * r
                t = d * g_ref[...]
                o_ref[r0:r0 + sub, :] = s_ref[r0:r0 + sub, :] + r * (t - xh * jnp.mean(t * xh, axis=-1, keepdims=True))
                dg = dg + jnp.sum(d * xh, axis=0, keepdims=True)
            _acc_rows(dg_ref, i, dg)

    row1 = pl.BlockSpec((tm, K), lambda i, k: (i, 0), pipeline_mode=pl.Buffered(1))
    row = pl.BlockSpec((tm, K), lambda i, k: (i, 0))
    vec = pl.BlockSpec((1, K), lambda i, k: (0, 0))
    return pl.pallas_call(
        body, name=name, grid=(T // tm, nk),
        in_specs=[pl.BlockSpec((tm, tn), lambda i, k: (i, k)),
                  pl.BlockSpec((None, K, tn), lambda i, k: (k // nj, 0, ob + k % nj)), row1, vec, row1],
        out_specs=[row, vec],
        out_shape=[jax.ShapeDtypeStruct((T, K), F32), jax.ShapeDtypeStruct((1, K), F32)],
        scratch_shapes=[pltpu.VMEM((tm, K), F32)],
        compiler_params=_cp(("arbitrary", "arbitrary")))(dy, G, h, g, skip)


def _rows(tm, C):
    return pl.BlockSpec((tm, C), lambda i: (i, 0))


def _vec(C):
    return pl.BlockSpec((1, C), lambda i: (0, 0))


def _acc_rows(ref, i, val):
    @pl.when(i == 0)
    def _():
        ref[...] = val

    @pl.when(i > 0)
    def _():
        ref[...] += val


def rms_fwd(h, g, out_dtype, name):
    T, C = h.shape
    tm = min(512, T)

    def body(h_ref, g_ref, o_ref):
        x = h_ref[...]
        r = lax.rsqrt(jnp.mean(x * x, axis=-1, keepdims=True) + EPS)
        o_ref[...] = (x * r * g_ref[...]).astype(o_ref.dtype)

    return pl.pallas_call(
        body, name=name, grid=(T // tm,), in_specs=[_rows(tm, C), _vec(C)], out_specs=_rows(tm, C),
        out_shape=jax.ShapeDtypeStruct((T, C), out_dtype), compiler_params=_cp(("parallel",)))(h, g)


def rms_bwd(dxn, h, g, dh_skip, name):
    T, C = h.shape
    tm = min(512, T)

    def body(d_ref, h_ref, g_ref, s_ref, o_ref, dg_ref):
        i = pl.program_id(0)
        x = h_ref[...]
        r = lax.rsqrt(jnp.mean(x * x, axis=-1, keepdims=True) + EPS)
        xh = x * r
        d = d_ref[...].astype(F32)
        t = d * g_ref[...]
        o_ref[...] = s_ref[...] + r * (t - xh * jnp.mean(t * xh, axis=-1, keepdims=True))
        _acc_rows(dg_ref, i, jnp.sum(d * xh, axis=0, keepdims=True))

    return pl.pallas_call(
        body, name=name, grid=(T // tm,),
        in_specs=[_rows(tm, C), _rows(tm, C), _vec(C), _rows(tm, C)],
        out_specs=[_rows(tm, C), _vec(C)],
        out_shape=[jax.ShapeDtypeStruct((T, C), F32), jax.ShapeDtypeStruct((1, C), F32)],
        compiler_params=_cp(("arbitrary",)))(dxn, h, g, dh_skip)


def post_res(h, f, g, scale, name):
    T, C = h.shape
    tm = min(512, T)

    def body(h_ref, f_ref, g_ref, o_ref):
        f = f_ref[...]
        r = lax.rsqrt(jnp.mean(f * f, axis=-1, keepdims=True) + EPS)
        o_ref[...] = h_ref[...] + scale * (f * r * g_ref[...])

    return pl.pallas_call(
        body, name=name, grid=(T // tm,), in_specs=[_rows(tm, C), _rows(tm, C), _vec(C)],
        out_specs=_rows(tm, C), out_shape=jax.ShapeDtypeStruct((T, C), F32),
        compiler_params=_cp(("parallel",)))(h, f, g)


def post_res_bwd(dh, f, g, scale, out_dtype, name):
    T, C = dh.shape
    tm = min(512, T)

    def body(d_ref, f_ref, g_ref, o_ref, dg_ref):
        i = pl.program_id(0)
        f = f_ref[...]
        r = lax.rsqrt(jnp.mean(f * f, axis=-1, keepdims=True) + EPS)
        d = scale * d_ref[...]
        t = d * g_ref[...]
        o_ref[...] = (r * t - f * (r * r * r * jnp.mean(t * f, axis=-1, keepdims=True))).astype(o_ref.dtype)
        _acc_rows(dg_ref, i, jnp.sum(d * f * r, axis=0, keepdims=True))

    return pl.pallas_call(
        body, name=name, grid=(T // tm,), in_specs=[_rows(tm, C), _rows(tm, C), _vec(C)],
        out_specs=[_rows(tm, C), _vec(C)],
        out_shape=[jax.ShapeDtypeStruct((T, C), out_dtype), jax.ShapeDtypeStruct((1, C), F32)],
        compiler_params=_cp(("arbitrary",)))(dh, f, g)


def ff_act(gu, name):
    T = gu.shape[0]
    tm = min(512, T)

    def body(gu_ref, o_ref):
        g = gu_ref[:, :FB].astype(F32)
        u = gu_ref[:, FB:].astype(F32)
        o_ref[...] = (g * _sig(g) * u).astype(o_ref.dtype)

    return pl.pallas_call(
        body, name=name, grid=(T // tm, NCHIP),
        in_specs=[pl.BlockSpec((tm, 2 * FB), lambda i, s: (i, s))],
        out_specs=pl.BlockSpec((tm, FB), lambda i, s: (i, s)),
        out_shape=jax.ShapeDtypeStruct((T, NCHIP * FB), BF16),
        compiler_params=_cp(("parallel", "parallel")))(gu)


def ff_act_bwd(da, gu, name):
    T = gu.shape[0]
    tm = min(512, T)

    def body(da_ref, gu_ref, o_ref):
        g = gu_ref[:, :FB].astype(F32)
        u = gu_ref[:, FB:].astype(F32)
        da = da_ref[...].astype(F32)
        s = _sig(g)
        o_ref[:, :FB] = (da * u * (s * (1.0 + g * (1.0 - s)))).astype(o_ref.dtype)
        o_ref[:, FB:] = (da * (g * s)).astype(o_ref.dtype)

    return pl.pallas_call(
        body, name=name, grid=(T // tm, NCHIP),
        in_specs=[pl.BlockSpec((tm, FB), lambda i, s: (i, s)), pl.BlockSpec((tm, 2 * FB), lambda i, s: (i, s))],
        out_specs=pl.BlockSpec((tm, 2 * FB), lambda i, s: (i, s)),
        out_shape=jax.ShapeDtypeStruct((T, NCHIP * 2 * FB), BF16),
        compiler_params=_cp(("parallel", "parallel")))(da, gu)


def ple_post(h, zg, pe, g, name):
    T, C = h.shape
    tm = min(512, T)

    def body(h_ref, z_ref, p_ref, g_ref, o_ref):
        e = p_ref[...] * _sig(z_ref[...])
        r = lax.rsqrt(jnp.mean(e * e, axis=-1, keepdims=True) + EPS)
        o_ref[...] = h_ref[...] + e * r * g_ref[...]

    return pl.pallas_call(
        body, name=name, grid=(T // tm,), in_specs=[_rows(tm, C), _rows(tm, C), _rows(tm, C), _vec(C)],
        out_specs=_rows(tm, C), out_shape=jax.ShapeDtypeStruct((T, C), F32),
        compiler_params=_cp(("parallel",)))(h, zg, pe, g)


def ple_post_bwd(dh, zg, pe, g, name):
    T, C = dh.shape
    tm = min(512, T)

    def body(d_ref, z_ref, p_ref, g_ref, dz_ref, dp_ref, dg_ref):
        i = pl.program_id(0)
        s = _sig(z_ref[...])
        pe_ = p_ref[...]
        e = pe_ * s
        r = lax.rsqrt(jnp.mean(e * e, axis=-1, keepdims=True) + EPS)
        d = d_ref[...]
        t = d * g_ref[...]
        de = r * t - e * (r * r * r * jnp.mean(t * e, axis=-1, keepdims=True))
        dp_ref[...] = (de * s).astype(dp_ref.dtype)
        dz_ref[...] = (de * pe_ * s * (1.0 - s)).astype(dz_ref.dtype)
        _acc_rows(dg_ref, i, jnp.sum(d * e * r, axis=0, keepdims=True))

    return pl.pallas_call(
        body, name=name, grid=(T // tm,), in_specs=[_rows(tm, C), _rows(tm, C), _rows(tm, C), _vec(C)],
        out_specs=[_rows(tm, C), _rows(tm, C), _vec(C)],
        out_shape=[jax.ShapeDtypeStruct((T, C), BF16), jax.ShapeDtypeStruct((T, C), BF16),
                   jax.ShapeDtypeStruct((1, C), F32)],
        compiler_params=_cp(("arbitrary",)))(dh, zg, pe, g)


def loss_head(h, tgt, name):
    T, C = h.shape
    tm = min(512, T)

    def body(h_ref, t_ref, d_ref, l_ref):
        i = pl.program_id(0)
        e = h_ref[...] - t_ref[...]
        d_ref[...] = e * (1.0 / C)
        _acc_rows(l_ref, i, jnp.sum(e * e, axis=0, keepdims=True))

    return pl.pallas_call(
        body, name=name, grid=(T // tm,), in_specs=[_rows(tm, C), _rows(tm, C)],
        out_specs=[_rows(tm, C), _vec(C)],
        out_shape=[jax.ShapeDtypeStruct((T, C), F32), jax.ShapeDtypeStruct((1, C), F32)],
        compiler_params=_cp(("arbitrary",)))(h, tgt)


AH, AG_N, AGW, CHUNK = 3072, 12, 256, 128


def _tril_bf16(w):
    r = lax.broadcasted_iota(jnp.int32, (CHUNK, CHUNK), 0)
    c = lax.broadcasted_iota(jnp.int32, (CHUNK, CHUNK), 1)
    return jnp.where(r >= c, w, 0.0).astype(BF16)


def _ln_stats(vs_ref, width):
    v = vs_ref[...]
    mu = jnp.sum(v, axis=-1, keepdims=True) * (1.0 / width)
    vc = v - mu
    var = jnp.sum(vc * vc, axis=-1, keepdims=True) * (1.0 / width)
    return mu, lax.rsqrt(var + EPS)


def gmlp_mid_fwd(zpre, vg, vb, ws, bsf, name):
    T = zpre.shape[0]

    def body(z_ref, vg_ref, vb_ref, ws_ref, bs_ref, y_ref, vs_ref):
        for g in range(AG_N):
            vs_ref[:, g * AGW:(g + 1) * AGW] = _gelu(z_ref[:, AH + g * AGW:AH + (g + 1) * AGW].astype(F32))
        mu, rstd = _ln_stats(vs_ref, AH)
        for g in range(AG_N):
            sl = slice(g * AGW, (g + 1) * AGW)
            vn = ((vs_ref[:, sl] - mu) * rstd * vg_ref[:, sl] + vb_ref[:, sl]).astype(BF16)
            sv = _dot_nn(_tril_bf16(ws_ref[g]), vn) + bs_ref[g]
            u = _gelu(z_ref[:, sl].astype(F32))
            y_ref[:, sl] = (u * sv).astype(y_ref.dtype)

    return pl.pallas_call(
        body, name=name, grid=(T // CHUNK,),
        in_specs=[_rows(CHUNK, 2 * AH), _vec(AH), _vec(AH),
                  pl.BlockSpec((AG_N, CHUNK, CHUNK), lambda i: (0, 0, 0)),
                  pl.BlockSpec((AG_N, CHUNK, AGW), lambda i: (0, 0, 0))],
        out_specs=_rows(CHUNK, AH), out_shape=jax.ShapeDtypeStruct((T, AH), BF16),
        scratch_shapes=[pltpu.VMEM((CHUNK, AH), F32)],
        compiler_params=_cp(("parallel",)))(zpre, vg, vb, ws, bsf)


def gmlp_mid_bwd(zpre, dy, vg, vb, ws, bsf, name):
    T = zpre.shape[0]

    def body(z_ref, dy_ref, vg_ref, vb_ref, ws_ref, bs_ref, dz_ref, dws_ref, dbs_ref, dvg_ref, dvb_ref,
             vs_ref, dvn_ref):
        i = pl.program_id(0)

        @pl.when(i == 0)
        def _():
            dws_ref[...] = jnp.zeros_like(dws_ref)
            dbs_ref[...] = jnp.zeros_like(dbs_ref)
            dvg_ref[...] = jnp.zeros_like(dvg_ref)
            dvb_ref[...] = jnp.zeros_like(dvb_ref)

        for g in range(AG_N):
            vs_ref[:, g * AGW:(g + 1) * AGW] = _gelu(z_ref[:, AH + g * AGW:AH + (g + 1) * AGW].astype(F32))
        mu, rstd = _ln_stats(vs_ref, AH)
        r_i = lax.broadcasted_iota(jnp.int32, (CHUNK, CHUNK), 0)
        c_i = lax.broadcasted_iota(jnp.int32, (CHUNK, CHUNK), 1)
        ones8 = jnp.ones((8, AGW), F32)
        m1 = jnp.zeros((CHUNK, 1), F32)
        m2 = jnp.zeros((CHUNK, 1), F32)
        for g in range(AG_N):
            sl = slice(g * AGW, (g + 1) * AGW)
            vh = (vs_ref[:, sl] - mu) * rstd
            vn = (vh * vg_ref[:, sl] + vb_ref[:, sl]).astype(BF16)
            wm = _tril_bf16(ws_ref[g])
            sv = _dot_nn(wm, vn) + bs_ref[g]
            zu = z_ref[:, sl].astype(F32)
            u = _gelu(zu)
            dyg = dy_ref[:, sl].astype(F32)
            dz_ref[:, sl] = (dyg * sv * _gelu_grad(zu)).astype(dz_ref.dtype)
            dsv = dyg * u
            dsv_b = dsv.astype(BF16)
            dws_ref[g] += jnp.where(r_i >= c_i, _dot_nt(dsv_b, vn), 0.0)
            dbs_ref[g] += _dot_nt(ones8, dsv)
            dvn = _dot_tn(wm, dsv_b)
            dvn_ref[:, sl] = dvn
            dvh = dvn * vg_ref[:, sl]
            m1 = m1 + jnp.sum(dvh, axis=-1, keepdims=True)
            m2 = m2 + jnp.sum(dvh * vh, axis=-1, keepdims=True)
            dvg_ref[:, sl] += jnp.sum(dvn * vh, axis=0, keepdims=True)
            dvb_ref[:, sl] += jnp.sum(dvn, axis=0, keepdims=True)
        m1 = m1 * (1.0 / AH)
        m2 = m2 * (1.0 / AH)
        for g in range(AG_N):
            sl = slice(g * AGW, (g + 1) * AGW)
            vh = (vs_ref[:, sl] - mu) * rstd
            dv = rstd * (dvn_ref[:, sl] * vg_ref[:, sl] - m1 - vh * m2)
            zv = z_ref[:, AH + g * AGW:AH + (g + 1) * AGW].astype(F32)
            dz_ref[:, AH + g * AGW:AH + (g + 1) * AGW] = (dv * _gelu_grad(zv)).astype(dz_ref.dtype)

    full3 = lambda a, b, c: pl.BlockSpec((a, b, c), lambda i: (0, 0, 0))
    return pl.pallas_call(
        body, name=name, grid=(T // CHUNK,),
        in_specs=[_rows(CHUNK, 2 * AH), _rows(CHUNK, AH), _vec(AH), _vec(AH),
                  full3(AG_N, CHUNK, CHUNK), full3(AG_N, CHUNK, AGW)],
        out_specs=[_rows(CHUNK, 2 * AH), full3(AG_N, CHUNK, CHUNK), full3(AG_N, 8, CHUNK), _vec(AH), _vec(AH)],
        out_shape=[jax.ShapeDtypeStruct((T, 2 * AH), BF16), jax.ShapeDtypeStruct((AG_N, CHUNK, CHUNK), F32),
                   jax.ShapeDtypeStruct((AG_N, 8, CHUNK), F32), jax.ShapeDtypeStruct((1, AH), F32),
                   jax.ShapeDtypeStruct((1, AH), F32)],
        scratch_shapes=[pltpu.VMEM((CHUNK, AH), F32), pltpu.VMEM((CHUNK, AH), F32)],
        compiler_params=_cp(("arbitrary",)))(zpre, dy, vg, vb, ws, bsf)


SLAB = 256
NSLAB = DM // SLAB
RC = 256
PAD = 32


def _col(T, j):
    return pl.BlockSpec((T, SLAB), lambda c: (0, j * NSLAB + c))


def _chunks(T, fn):
    def step(i, carry):
        fn(pl.multiple_of(i * RC, RC))
        return carry
    lax.fori_loop(0, T // RC, step, 0)


def _conv_taps(K):
    return [(r, [q for q in range(4) if 8 * q + r < K]) for r in range(min(8, K))]


def _causal_conv(zpad_ref, wrow, K, r0):
    acc = None
    for r, qs in _conv_taps(K):
        a = None
        for q in qs:
            term = wrow(8 * q + r) * zpad_ref[pl.ds(r0 + (PAD - 8 - 8 * q), RC + 8), :]
            a = term if a is None else a + term
        a = a if r == 0 else pltpu.roll(a, r, 0)
        acc = a if acc is None else acc + a
    return acc[8:, :]


def _anticausal_conv(gpad_ref, wrow, K, r0):
    acc = None
    for r, qs in _conv_taps(K):
        b = None
        for q in qs:
            term = wrow(8 * q + r) * gpad_ref[pl.ds(r0 + 8 * q, RC + 8), :]
            b = term if b is None else b + term
        b = b if r == 0 else pltpu.roll(b, RC + 8 - r, 0)
        acc = b if acc is None else acc + b
    return acc[:RC, :]


def _conv_dw(gpad_ref, zpad_ref, dw_ref, K, r0):
    for r, qs in _conv_taps(K):
        gw = gpad_ref[pl.ds(r0, RC + 8), :]
        p = (gw if r == 0 else pltpu.roll(gw, RC + 8 - r, 0))[:RC, :]
        for q in qs:
            z = zpad_ref[pl.ds(r0 + (PAD - 8 * q), RC), :]
            dw_ref[8 * q + r] += jnp.sum((p * z).reshape(RC // 8, 8, SLAB), axis=0)


def _zero_rows(ref, start, n):
    ref[pl.ds(start, n), :] = jnp.zeros((n, SLAB), F32)


def pool_fwd(hn, wg, sc, name):
    T = hn.shape[0]

    def body(h_ref, w_ref, s_ref, p_ref, yp_ref, y_ref, xpad):
        g = pl.program_id(0)
        wf = jnp.left_shift(2, g).astype(F32)
        _zero_rows(xpad, 0, PAD)

        def fill(r0):
            xpad[pl.ds(r0 + PAD, RC), :] = h_ref[pl.ds(r0, RC), :]
        _chunks(T, fill)

        def step(r0):
            w = xpad[pl.ds(r0 + (PAD - 16), RC + 16), :]
            s2 = w + pltpu.roll(w, 1, 0)
            s4 = s2 + pltpu.roll(s2, 2, 0)
            s8 = s4 + pltpu.roll(s4, 4, 0)
            s16 = s8 + pltpu.roll(s8, 8, 0)
            sel = jnp.where(g == 0, s2, jnp.where(g == 1, s4, jnp.where(g == 2, s8, s16)))[16:, :]
            t1 = (r0 + 1 + lax.broadcasted_iota(jnp.int32, (RC, SLAB), 0)).astype(F32)
            pooled = (sel / jnp.minimum(t1, wf) - w[16:, :]).astype(BF16)
            p_ref[pl.ds(r0, RC), :] = pooled
            yp = _dot_nn(pooled, w_ref[...])
            yp_ref[pl.ds(r0, RC), :] = yp
            y_ref[pl.ds(r0, RC), :] = yp * s_ref[...]
        _chunks(T, step)

    slab = pl.BlockSpec((T, SLAB), lambda c: (0, c))
    return pl.pallas_call(
        body, name=name, grid=(NSLAB,),
        in_specs=[slab, pl.BlockSpec((None, SLAB, SLAB), lambda c: (c, 0, 0)), pl.BlockSpec((1, SLAB), lambda c: (0, c))],
        out_specs=[slab, slab, slab],
        out_shape=[jax.ShapeDtypeStruct((T, DM), BF16), jax.ShapeDtypeStruct((T, DM), F32),
                   jax.ShapeDtypeStruct((T, DM), F32)],
        scratch_shapes=[pltpu.VMEM((T + PAD, SLAB), F32)],
        compiler_params=_cp(("parallel",)))(hn, wg, sc)


def pool_bwd(dy, ypre, pooled, wg, sc, name):
    T = dy.shape[0]

    def body(d_ref, yp_ref, p_ref, w_ref, s_ref, dh_ref, dw_ref, ds_ref, qpad, dwacc, dsacc):
        g = pl.program_id(0)
        wf = jnp.left_shift(2, g).astype(F32)
        dwacc[...] = jnp.zeros_like(dwacc)
        dsacc[...] = jnp.zeros_like(dsacc)
        _zero_rows(qpad, T, PAD)

        def first(r0):
            d = d_ref[pl.ds(r0, RC), :]
            dsacc[...] += jnp.sum((d * yp_ref[pl.ds(r0, RC), :]).reshape(RC // 8, 8, SLAB), axis=0)
            dyp = (d * s_ref[...]).astype(BF16)
            dpool = _dot_nt(dyp, w_ref[...])
            dwacc[...] += _dot_tn(p_ref[pl.ds(r0, RC), :], dyp)
            t1 = (r0 + 1 + lax.broadcasted_iota(jnp.int32, (RC, SLAB), 0)).astype(F32)
            qpad[pl.ds(r0, RC), :] = dpool / jnp.minimum(t1, wf)
            dh_ref[pl.ds(r0, RC), :] = dpool
        _chunks(T, first)

        def second(r0):
            w = qpad[pl.ds(r0, RC + 16), :]
            n = RC + 16
            a2 = w + pltpu.roll(w, n - 1, 0)
            a4 = a2 + pltpu.roll(a2, n - 2, 0)
            a8 = a4 + pltpu.roll(a4, n - 4, 0)
            a16 = a8 + pltpu.roll(a8, n - 8, 0)
            sel = jnp.where(g == 0, a2, jnp.where(g == 1, a4, jnp.where(g == 2, a8, a16)))[:RC, :]
            dh_ref[pl.ds(r0, RC), :] = sel - dh_ref[pl.ds(r0, RC), :]
        _chunks(T, second)
        dw_ref[...] = dwacc[...]
        ds_ref[...] = jnp.sum(dsacc[...], axis=0, keepdims=True)

    slab = pl.BlockSpec((T, SLAB), lambda c: (0, c))
    wspec = pl.BlockSpec((None, SLAB, SLAB), lambda c: (c, 0, 0))
    vec = pl.BlockSpec((1, SLAB), lambda c: (0, c))
    return pl.pallas_call(
        body, name=name, grid=(NSLAB,),
        in_specs=[slab, slab, slab, wspec, vec],
        out_specs=[slab, wspec, vec],
        out_shape=[jax.ShapeDtypeStruct((T, DM), F32), jax.ShapeDtypeStruct((NSLAB, SLAB, SLAB), F32),
                   jax.ShapeDtypeStruct((1, DM), F32)],
        scratch_shapes=[pltpu.VMEM((T + PAD, SLAB), F32), pltpu.VMEM((SLAB, SLAB), F32), pltpu.VMEM((8, SLAB), F32)],
        compiler_params=_cp(("parallel",)))(dy, ypre, pooled, wg, sc)


KC = 31
KD = 3


def conf_conv_fwd(ag, wdw, bdw, name):
    T = ag.shape[0]

    def body(a_ref, g_ref, w_ref, b_ref, o_ref, zpad):
        _zero_rows(zpad, 0, PAD)

        def fill(r0):
            a = a_ref[pl.ds(r0, RC), :].astype(F32)
            gt = g_ref[pl.ds(r0, RC), :].astype(F32)
            zpad[pl.ds(r0 + PAD, RC), :] = a * _sig(gt)
        _chunks(T, fill)
        wrow = lambda j: w_ref[KC - 1 - j:KC - j, :]

        def step(r0):
            o_ref[pl.ds(r0, RC), :] = _causal_conv(zpad, wrow, KC, r0) + b_ref[...]
        _chunks(T, step)

    vec = pl.BlockSpec((1, SLAB), lambda c: (0, c))
    return pl.pallas_call(
        body, name=name, grid=(NSLAB,),
        in_specs=[_col(T, 0), _col(T, 1), pl.BlockSpec((32, SLAB), lambda c: (0, c)), vec],
        out_specs=pl.BlockSpec((T, SLAB), lambda c: (0, c)),
        out_shape=jax.ShapeDtypeStruct((T, DM), F32),
        scratch_shapes=[pltpu.VMEM((T + PAD, SLAB), F32)],
        compiler_params=_cp(("parallel",)))(ag, ag, wdw, bdw)


def conf_conv_bwd(dzc, ag, wdw, name):
    T = ag.shape[0]

    def body(d_ref, a_ref, g_ref, w_ref, da_ref, dg_ref, dw_ref, db_ref, zpad, gpad, dwacc, dbacc):
        _zero_rows(zpad, 0, PAD)
        _zero_rows(gpad, T, PAD)
        dwacc[...] = jnp.zeros_like(dwacc)
        dbacc[...] = jnp.zeros_like(dbacc)

        def fill(r0):
            a = a_ref[pl.ds(r0, RC), :].astype(F32)
            gt = g_ref[pl.ds(r0, RC), :].astype(F32)
            zpad[pl.ds(r0 + PAD, RC), :] = a * _sig(gt)
            d = d_ref[pl.ds(r0, RC), :]
            gpad[pl.ds(r0, RC), :] = d
            dbacc[...] += jnp.sum(d.reshape(RC // 8, 8, SLAB), axis=0)
        _chunks(T, fill)
        wrow = lambda j: w_ref[KC - 1 - j:KC - j, :]

        def step(r0):
            dz = _anticausal_conv(gpad, wrow, KC, r0)
            a = a_ref[pl.ds(r0, RC), :].astype(F32)
            s = _sig(g_ref[pl.ds(r0, RC), :].astype(F32))
            da_ref[pl.ds(r0, RC), :] = (dz * s).astype(da_ref.dtype)
            dg_ref[pl.ds(r0, RC), :] = (dz * a * s * (1.0 - s)).astype(dg_ref.dtype)
            _conv_dw(gpad, zpad, dwacc, KC, r0)
        _chunks(T, step)
        dw_ref[...] = jnp.zeros_like(dw_ref)
        for k in range(KC):
            dw_ref[k:k + 1, :] = jnp.sum(dwacc[KC - 1 - k], axis=0, keepdims=True)
        db_ref[...] = jnp.sum(dbacc[...], axis=0, keepdims=True)

    vec = pl.BlockSpec((1, SLAB), lambda c: (0, c))
    w32 = pl.BlockSpec((32, SLAB), lambda c: (0, c))
    return pl.pallas_call(
        body, name=name, grid=(NSLAB,),
        in_specs=[pl.BlockSpec((T, SLAB), lambda c: (0, c)), _col(T, 0), _col(T, 1), w32],
        out_specs=[_col(T, 0), _col(T, 0), w32, vec],
        out_shape=[jax.ShapeDtypeStruct((T, DM), BF16), jax.ShapeDtypeStruct((T, DM), BF16),
                   jax.ShapeDtypeStruct((32, DM), F32), jax.ShapeDtypeStruct((1, DM), F32)],
        scratch_shapes=[pltpu.VMEM((T + PAD, SLAB), F32), pltpu.VMEM((T + PAD, SLAB), F32),
                        pltpu.VMEM((32, 8, SLAB), F32), pltpu.VMEM((8, SLAB), F32)],
        compiler_params=_cp(("parallel",)))(dzc, ag, ag, wdw)


def conf_ln_fwd(zc, g, b, name):
    T, C = zc.shape
    tm = min(512, T)

    def body(z_ref, g_ref, b_ref, o_ref):
        x = z_ref[...]
        xc = x - jnp.mean(x, axis=-1, keepdims=True)
        r = lax.rsqrt(jnp.mean(xc * xc, axis=-1, keepdims=True) + EPS)
        zl = xc * r * g_ref[...] + b_ref[...]
        o_ref[...] = (zl * _sig(zl)).astype(o_ref.dtype)

    return pl.pallas_call(
        body, name=name, grid=(T // tm,), in_specs=[_rows(tm, C), _vec(C), _vec(C)], out_specs=_rows(tm, C),
        out_shape=jax.ShapeDtypeStruct((T, C), BF16), compiler_params=_cp(("parallel",)))(zc, g, b)


def conf_ln_bwd(dzs, zc, g, b, name):
    T, C = zc.shape
    tm = min(512, T)

    def body(d_ref, z_ref, g_ref, b_ref, o_ref, dg_ref, db_ref):
        i = pl.program_id(0)
        x = z_ref[...]
        xc = x - jnp.mean(x, axis=-1, keepdims=True)
        r = lax.rsqrt(jnp.mean(xc * xc, axis=-1, keepdims=True) + EPS)
        xh = xc * r
        zl = xh * g_ref[...] + b_ref[...]
        s = _sig(zl)
        dzl = d_ref[...].astype(F32) * (s * (1.0 + zl * (1.0 - s)))
        t = dzl * g_ref[...]
        o_ref[...] = r * (t - jnp.mean(t, axis=-1, keepdims=True) - xh * jnp.mean(t * xh, axis=-1, keepdims=True))
        _acc_rows(dg_ref, i, jnp.sum(dzl * xh, axis=0, keepdims=True))
        _acc_rows(db_ref, i, jnp.sum(dzl, axis=0, keepdims=True))

    return pl.pallas_call(
        body, name=name, grid=(T // tm,), in_specs=[_rows(tm, C), _rows(tm, C), _vec(C), _vec(C)],
        out_specs=[_rows(tm, C), _vec(C), _vec(C)],
        out_shape=[jax.ShapeDtypeStruct((T, C), F32), jax.ShapeDtypeStruct((1, C), F32),
                   jax.ShapeDtypeStruct((1, C), F32)],
        compiler_params=_cp(("arbitrary",)))(dzs, zc, g, b)


def sconv_fwd(bgx, wc, name):
    T = bgx.shape[0]

    def body(b_ref, c_ref, x_ref, w_ref, o_ref, zpad):
        _zero_rows(zpad, 0, PAD)

        def fill(r0):
            zpad[pl.ds(r0 + PAD, RC), :] = c_ref[pl.ds(r0, RC), :].astype(F32) * x_ref[pl.ds(r0, RC), :].astype(F32)
        _chunks(T, fill)
        wrow = lambda j: w_ref[KD - 1 - j:KD - j, :]

        def step(r0):
            qc = _causal_conv(zpad, wrow, KD, r0)
            o_ref[pl.ds(r0, RC), :] = (b_ref[pl.ds(r0, RC), :].astype(F32) * qc).astype(o_ref.dtype)
        _chunks(T, step)

    return pl.pallas_call(
        body, name=name, grid=(NSLAB,),
        in_specs=[_col(T, 0), _col(T, 1), _col(T, 2), pl.BlockSpec((8, SLAB), lambda c: (0, c))],
        out_specs=pl.BlockSpec((T, SLAB), lambda c: (0, c)),
        out_shape=jax.ShapeDtypeStruct((T, DM), BF16),
        scratch_shapes=[pltpu.VMEM((T + PAD, SLAB), F32)],
        compiler_params=_cp(("parallel",)))(bgx, bgx, bgx, wc)


def sconv_bwd(dy, bgx, wc, name):
    T = bgx.shape[0]

    def body(d_ref, b_ref, c_ref, x_ref, w_ref, db_ref, dc_ref, dx_ref, dw_ref, zpad, gpad, dwacc):
        _zero_rows(zpad, 0, PAD)
        _zero_rows(gpad, T, PAD)
        dwacc[...] = jnp.zeros_like(dwacc)

        def fill(r0):
            zpad[pl.ds(r0 + PAD, RC), :] = c_ref[pl.ds(r0, RC), :].astype(F32) * x_ref[pl.ds(r0, RC), :].astype(F32)
            gpad[pl.ds(r0, RC), :] = d_ref[pl.ds(r0, RC), :].astype(F32) * b_ref[pl.ds(r0, RC), :].astype(F32)
        _chunks(T, fill)
        wrow = lambda j: w_ref[KD - 1 - j:KD - j, :]

        def step(r0):
            qc = _causal_conv(zpad, wrow, KD, r0)
            db_ref[pl.ds(r0, RC), :] = (d_ref[pl.ds(r0, RC), :].astype(F32) * qc).astype(db_ref.dtype)
            dq = _anticausal_conv(gpad, wrow, KD, r0)
            dc_ref[pl.ds(r0, RC), :] = (dq * x_ref[pl.ds(r0, RC), :].astype(F32)).astype(dc_ref.dtype)
            dx_ref[pl.ds(r0, RC), :] = (dq * c_ref[pl.ds(r0, RC), :].astype(F32)).astype(dx_ref.dtype)
            _conv_dw(gpad, zpad, dwacc, KD, r0)
        _chunks(T, step)
        dw_ref[...] = jnp.zeros_like(dw_ref)
        for k in range(KD):
            dw_ref[k:k + 1, :] = jnp.sum(dwacc[KD - 1 - k], axis=0, keepdims=True)

    w8 = pl.BlockSpec((8, SLAB), lambda c: (0, c))
    return pl.pallas_call(
        body, name=name, grid=(NSLAB,),
        in_specs=[pl.BlockSpec((T, SLAB), lambda c: (0, c)), _col(T, 0), _col(T, 1), _col(T, 2), w8],
        out_specs=[_col(T, 0), _col(T, 0), _col(T, 0), w8],
        out_shape=[jax.ShapeDtypeStruct((T, DM), BF16)] * 3 + [jax.ShapeDtypeStruct((8, DM), F32)],
        scratch_shapes=[pltpu.VMEM((T + PAD, SLAB), F32), pltpu.VMEM((T + PAD, SLAB), F32),
                        pltpu.VMEM((8, 8, SLAB), F32)],
        compiler_params=_cp(("parallel",)))(dy, bgx, bgx, bgx, wc)


def merge_cols(parts, name):
    T = parts[0].shape[0]
    n = len(parts)
    C = n * DM
    tm = min(512, T)

    def body(*refs):
        o_ref = refs[n]
        for j in range(n):
            o_ref[:, j * DM:(j + 1) * DM] = refs[j][...]

    return pl.pallas_call(
        body, name=name, grid=(T // tm,),
        in_specs=[_rows(tm, DM) for j in range(n)],
        out_specs=_rows(tm, C), out_shape=jax.ShapeDtypeStruct((T, C), parts[0].dtype),
        compiler_params=_cp(("parallel",)))(*parts)


def adamw(w, g, m, v, name):
    shape = w.shape
    C = shape[-1]
    R = w.size // C
    w2, g2, m2, v2 = (a.reshape(R, C) for a in (w, g, m, v))
    tr = R
    while tr * C > 512 * 1024 and tr % 16 == 0:
        tr //= 2
    bc1 = 1.0 - ADAM_B1 ** ADAM_STEP
    bc2 = 1.0 - ADAM_B2 ** ADAM_STEP

    def body(w_ref, g_ref, m_ref, v_ref, d_ref, nm_ref, nv_ref):
        gg = g_ref[...]
        nm = ADAM_B1 * m_ref[...] + (1.0 - ADAM_B1) * gg
        nv = ADAM_B2 * v_ref[...] + (1.0 - ADAM_B2) * (gg * gg)
        nm_ref[...] = nm
        nv_ref[...] = nv
        d_ref[...] = -ADAM_LR * ((nm / bc1) / (jnp.sqrt(nv / bc2) + ADAM_EPS) + ADAM_WD * w_ref[...])

    spec = pl.BlockSpec((tr, C), lambda i: (i, 0))
    outs = pl.pallas_call(
        body, name=name, grid=(R // tr,), in_specs=[spec] * 4, out_specs=[spec] * 3,
        out_shape=[jax.ShapeDtypeStruct((R, C), F32)] * 3, compiler_params=_cp(("parallel",)))(w2, g2, m2, v2)
    return tuple(o.reshape(shape) for o in outs)


def _place():
    x, y, c = lax.axis_index("x"), lax.axis_index("y"), lax.axis_index("c")
    return x, y, c


def all_gather_chips(bufs, name):
    n = len(bufs)
    me_ = 2 * lax.axis_index("x") + lax.axis_index("y")
    slots = [lax.dynamic_update_slice(lax.empty((NCHIP,) + b.shape, b.dtype), b[None], (me_, 0, 0)) for b in bufs]

    def body(*refs):
        dst = refs[n:2 * n]
        send, recv = refs[2 * n:]
        x, y, c = _place()
        me = 2 * x + y
        sib = (x, y, 1 - c)
        chips = [(1 - x, y), (x, 1 - y), (1 - x, 1 - y)]

        def half(b, slot, hc):
            rows = bufs[b].shape[0] // 2
            return dst[b].at[slot, pl.ds(hc * rows, rows), :]

        def remote(k, s, d, to):
            return pltpu.make_async_remote_copy(src_ref=s, dst_ref=d, send_sem=send.at[k], recv_sem=recv.at[k],
                                                device_id=to, device_id_type=MESH)

        first = []
        for b in range(n):
            for j, (cx, cy) in enumerate(chips):
                first.append(remote(b * 6 + j, half(b, me, c), half(b, me, c), (cx, cy, c)))
        for cp in first:
            cp.start()
        passed = []
        for b in range(n):
            for j, (cx, cy) in enumerate(chips):
                slot = 2 * cx + cy
                remote(b * 6 + j, half(b, slot, c), half(b, slot, c), (cx, cy, c)).wait_recv()
                fwd = remote(b * 6 + 3 + j, half(b, slot, c), half(b, slot, c), sib)
                fwd.start()
                passed.append(fwd)
        for b in range(n):
            for j, (cx, cy) in enumerate(chips):
                slot = 2 * cx + cy
                remote(b * 6 + 3 + j, half(b, slot, 1 - c), half(b, slot, 1 - c), sib).wait_recv()
        for cp in first + passed:
            cp.wait_send()

    return pl.pallas_call(
        body, name=name, in_specs=[ANY] * n, out_specs=[ANY] * n,
        out_shape=[jax.ShapeDtypeStruct(s.shape, s.dtype) for s in slots],
        input_output_aliases={b: b for b in range(n)},
        scratch_shapes=[pltpu.SemaphoreType.DMA((6 * n,)), pltpu.SemaphoreType.DMA((6 * n,))],
        compiler_params=pltpu.CompilerParams())(*slots)


def pair_exchange(bufs, name):
    n = len(bufs)

    def body(*refs):
        src, dst = refs[:n], refs[n:2 * n]
        send, recv = refs[2 * n:]
        x, y, c = _place()
        cps = []
        for b in range(n):
            rows = bufs[b].shape[1] // 2
            cp = pltpu.make_async_remote_copy(
                src_ref=src[b].at[:, pl.ds((1 - c) * rows, rows), :], dst_ref=dst[b],
                send_sem=send.at[b], recv_sem=recv.at[b], device_id=(x, y, 1 - c), device_id_type=MESH)
            cp.start()
            cps.append(cp)
        for cp in cps:
            cp.wait()

    return pl.pallas_call(
        body, name=name, in_specs=[ANY] * n, out_specs=[ANY] * n,
        out_shape=[jax.ShapeDtypeStruct((NCHIP, b.shape[1] // 2, b.shape[2]), b.dtype) for b in bufs],
        scratch_shapes=[pltpu.SemaphoreType.DMA((n,)), pltpu.SemaphoreType.DMA((n,))],
        compiler_params=pltpu.CompilerParams())(*bufs)


def add_half(full, got, tr, tc, name):
    _, R, C = full.shape
    rows = R // 2
    nr = rows // tr
    c_arr = lax.axis_index("c").astype(jnp.int32).reshape(1)

    def body(c_ref, a_ref, b_ref, o_ref):
        o_ref[...] = (a_ref[...].astype(F32) + b_ref[...].astype(F32)).astype(o_ref.dtype)

    return pl.pallas_call(
        body, name=name,
        grid_spec=pltpu.PrefetchScalarGridSpec(
            num_scalar_prefetch=1, grid=(NCHIP, nr, C // tc),
            in_specs=[pl.BlockSpec((None, tr, tc), lambda s, i, j, c_ref: (s, c_ref[0] * nr + i, j)),
                      pl.BlockSpec((None, tr, tc), lambda s, i, j, c_ref: (s, i, j))],
            out_specs=pl.BlockSpec((None, tr, tc), lambda s, i, j, c_ref: (s, i, j))),
        out_shape=jax.ShapeDtypeStruct((NCHIP, rows, C), full.dtype),
        compiler_params=_cp(("parallel", "parallel", "parallel")))(c_arr, full, got)


def chip_exchange(bufs, name):
    n = len(bufs)

    def body(*refs):
        src, dst = refs[:n], refs[n:2 * n]
        send, recv, lsem = refs[2 * n:]
        x, y, c = _place()
        me = 2 * x + y
        chips = [(1 - x, y), (x, 1 - y), (1 - x, 1 - y)]
        local = [pltpu.make_async_copy(src[b].at[me], dst[b].at[me], lsem.at[b]) for b in range(n)]
        for cp in local:
            cp.start()
        cps = []
        for b in range(n):
            for j, (cx, cy) in enumerate(chips):
                cp = pltpu.make_async_remote_copy(
                    src_ref=src[b].at[2 * cx + cy], dst_ref=dst[b].at[me],
                    send_sem=send.at[b * 3 + j], recv_sem=recv.at[b * 3 + j],
                    device_id=(cx, cy, c), device_id_type=MESH)
                cp.start()
                cps.append((cp, b, cx, cy, j))
        for cp, b, cx, cy, j in cps:
            cp.wait_send()
            pltpu.make_async_remote_copy(
                src_ref=src[b].at[me], dst_ref=dst[b].at[2 * cx + cy],
                send_sem=send.at[b * 3 + j], recv_sem=recv.at[b * 3 + j],
                device_id=(cx, cy, c), device_id_type=MESH).wait_recv()
        for cp in local:
            cp.wait()

    return pl.pallas_call(
        body, name=name, in_specs=[ANY] * n, out_specs=[ANY] * n,
        out_shape=[jax.ShapeDtypeStruct(b.shape, b.dtype) for b in bufs],
        scratch_shapes=[pltpu.SemaphoreType.DMA((3 * n,)), pltpu.SemaphoreType.DMA((3 * n,)),
                        pltpu.SemaphoreType.DMA((n,))],
        compiler_params=pltpu.CompilerParams())(*bufs)


def sum_slots(buf, tr, tc, name):
    _, r, C = buf.shape
    nr = r // tr
    c_arr = lax.axis_index("c").astype(jnp.int32).reshape(1)

    def body(c_ref, a_ref, o_ref):
        o_ref[...] = ((a_ref[0].astype(F32) + a_ref[1].astype(F32)) + a_ref[2].astype(F32)) + a_ref[3].astype(F32)

    return pl.pallas_call(
        body, name=name,
        grid_spec=pltpu.PrefetchScalarGridSpec(
            num_scalar_prefetch=1, grid=(nr, C // tc),
            in_specs=[pl.BlockSpec((NCHIP, tr, tc), lambda i, j, c_ref: (0, i, j))],
            out_specs=pl.BlockSpec((tr, tc), lambda i, j, c_ref: (c_ref[0] * nr + i, j))),
        out_shape=jax.ShapeDtypeStruct((2 * r, C), F32),
        compiler_params=_cp(("parallel", "parallel")))(c_arr, buf)


def pair_share(bufs, name):
    n = len(bufs)

    def body(*refs):
        dst = refs[n:2 * n]
        send, recv = refs[2 * n:]
        x, y, c = _place()
        cps = []
        for b in range(n):
            rows = bufs[b].shape[0] // 2
            here = dst[b].at[pl.ds(c * rows, rows), :]
            cp = pltpu.make_async_remote_copy(src_ref=here, dst_ref=here, send_sem=send.at[b], recv_sem=recv.at[b],
                                              device_id=(x, y, 1 - c), device_id_type=MESH)
            cp.start()
            cps.append((cp, b))
        for cp, b in cps:
            rows = bufs[b].shape[0] // 2
            there = dst[b].at[pl.ds((1 - c) * rows, rows), :]
            cp.wait_send()
            pltpu.make_async_remote_copy(src_ref=there, dst_ref=there, send_sem=send.at[b], recv_sem=recv.at[b],
                                         device_id=(x, y, 1 - c), device_id_type=MESH).wait_recv()

    return pl.pallas_call(
        body, name=name, in_specs=[ANY] * n, out_specs=[ANY] * n,
        out_shape=[jax.ShapeDtypeStruct(b.shape, b.dtype) for b in bufs],
        input_output_aliases={b: b for b in range(n)},
        scratch_shapes=[pltpu.SemaphoreType.DMA((n,)), pltpu.SemaphoreType.DMA((n,))],
        compiler_params=pltpu.CompilerParams())(*bufs)


def _tile(kind, buf):
    return {"A": (64, buf.shape[2]), "B": (128, 1024), "C": (128, 256), "V": (40, 256), "E": (40, 1024)}[kind]


def reduce_scatter(parts, tag):
    names = list(parts)
    got = pair_exchange([parts[k] for k in names], tag + "_pair_exchange")
    sums = [add_half(parts[k], got[i], *_tile(k[0], got[i]), name=tag + "_add_pair_" + k) for i, k in enumerate(names)]
    landed = chip_exchange(sums, tag + "_chip_exchange")
    halves = [sum_slots(landed[i], *_tile(k[0], landed[i]), name=tag + "_sum_chips_" + k) for i, k in enumerate(names)]
    full = pair_share(halves, tag + "_pair_share")
    return dict(zip(names, full))


HBM = pl.BlockSpec(memory_space=pltpu.HBM)
SEMS = pl.BlockSpec(memory_space=pltpu.SEMAPHORE)
FLOWS = pltpu.SideEffectType.DATAFLOW_SIDE_EFFECTING


def _in_hbm(a):
    return pltpu.with_memory_space_constraint(a, pltpu.HBM)


def _other_chips():
    x, y, c = _place()
    return 2 * x + y, c, [(1 - x, y), (x, 1 - y), (1 - x, 1 - y)]


def own_slots(bufs):
    me = 2 * lax.axis_index("x") + lax.axis_index("y")
    return [lax.dynamic_update_slice(lax.empty((NCHIP,) + b.shape, b.dtype), b[None], (me, 0, 0)) for b in bufs]


def gather_start(slots, name):
    n = len(slots)

    def body(*refs):
        ins = refs[:n]
        send, recv = refs[n], refs[n + 1]
        token = refs[2 * n + 2]
        me, c, chips = _other_chips()
        for b in range(n):
            rows = slots[b].shape[1] // 2
            own = ins[b].at[me, pl.ds(c * rows, rows), :]
            for j, (cx, cy) in enumerate(chips):
                pltpu.make_async_remote_copy(src_ref=own, dst_ref=own, send_sem=send.at[3 * b + j],
                                             recv_sem=recv.at[3 * b + j], device_id=(cx, cy, c),
                                             device_id_type=MESH).start()
        token[...] = jnp.zeros_like(token)

    out = pl.pallas_call(
        body, name=name, in_specs=[HBM] * n,
        out_specs=[SEMS, SEMS] + [HBM] * n + [pl.BlockSpec(memory_space=pltpu.VMEM)],
        out_shape=[pltpu.SemaphoreType.DMA((3 * n,)), pltpu.SemaphoreType.DMA((3 * n,))]
        + [pltpu.HBM(s.shape, s.dtype) for s in slots] + [jax.ShapeDtypeStruct((8, 128), F32)],
        input_output_aliases={b: b + 2 for b in range(n)},
        compiler_params=pltpu.CompilerParams(has_side_effects=FLOWS))(*[_in_hbm(s) for s in slots])
    return out[0], out[1], list(out[2:2 + n]), out[2 + n]


def gather_wait(send, recv, slots, picks, after, name):
    n = len(slots)

    def body(*refs):
        ins = refs[:n]
        send_, recv_ = refs[n], refs[n + 1]
        me, c, chips = _other_chips()
        for i, b in enumerate(picks):
            rows = slots[i].shape[1] // 2
            own = ins[i].at[me, pl.ds(c * rows, rows), :]
            for j, (cx, cy) in enumerate(chips):
                got = ins[i].at[2 * cx + cy, pl.ds(c * rows, rows), :]
                pltpu.make_async_remote_copy(src_ref=own, dst_ref=own, send_sem=send_.at[3 * b + j],
                                             recv_sem=recv_.at[3 * b + j], device_id=(cx, cy, c),
                                             device_id_type=MESH).wait_send()
                pltpu.make_async_remote_copy(src_ref=got, dst_ref=got, send_sem=send_.at[3 * b + j],
                                             recv_sem=recv_.at[3 * b + j], device_id=(cx, cy, c),
                                             device_id_type=MESH).wait_recv()

    return pl.pallas_call(
        body, name=name, in_specs=[HBM] * n + [SEMS, SEMS, ANY], out_specs=[HBM] * n,
        out_shape=[pltpu.HBM(s.shape, s.dtype) for s in slots],
        input_output_aliases={b: b for b in range(n)},
        compiler_params=pltpu.CompilerParams(has_side_effects=FLOWS))(*slots, send, recv, after)


def gather_pass(slots, name):
    n = len(slots)

    def body(*refs):
        dst = refs[n:2 * n]
        send, recv = refs[2 * n:]
        x, y, c = _place()
        sib = (x, y, 1 - c)
        chips = [(1 - x, y), (x, 1 - y), (1 - x, 1 - y)]

        def half(b, slot, hc):
            rows = slots[b].shape[1] // 2
            return dst[b].at[slot, pl.ds(hc * rows, rows), :]

        passed = []
        for b in range(n):
            for j, (cx, cy) in enumerate(chips):
                slot = 2 * cx + cy
                cp = pltpu.make_async_remote_copy(src_ref=half(b, slot, c), dst_ref=half(b, slot, c),
                                                  send_sem=send.at[3 * b + j], recv_sem=recv.at[3 * b + j],
                                                  device_id=sib, device_id_type=MESH)
                cp.start()
                passed.append(cp)
        for b in range(n):
            for j, (cx, cy) in enumerate(chips):
                slot = 2 * cx + cy
                pltpu.make_async_remote_copy(src_ref=half(b, slot, 1 - c), dst_ref=half(b, slot, 1 - c),
                                             send_sem=send.at[3 * b + j], recv_sem=recv.at[3 * b + j],
                                             device_id=sib, device_id_type=MESH).wait_recv()
        for cp in passed:
            cp.wait_send()

    return pl.pallas_call(
        body, name=name, in_specs=[ANY] * n, out_specs=[ANY] * n,
        out_shape=[jax.ShapeDtypeStruct(s.shape, s.dtype) for s in slots],
        input_output_aliases={b: b for b in range(n)},
        scratch_shapes=[pltpu.SemaphoreType.DMA((3 * n,)), pltpu.SemaphoreType.DMA((3 * n,))],
        compiler_params=pltpu.CompilerParams())(*slots)


def chip_exchange_start(sums, name):
    n = len(sums)
    me_ = 2 * lax.axis_index("x") + lax.axis_index("y")
    landing = [lax.dynamic_update_slice(lax.empty(s.shape, s.dtype),
                                        lax.dynamic_slice(s, (me_, 0, 0), (1,) + s.shape[1:]), (me_, 0, 0)) for s in sums]

    def body(*refs):
        src, land = refs[:n], refs[n:2 * n]
        send, recv = refs[2 * n], refs[2 * n + 1]
        token = refs[4 * n + 2]
        me, c, chips = _other_chips()
        for b in range(n):
            for j, (cx, cy) in enumerate(chips):
                pltpu.make_async_remote_copy(src_ref=src[b].at[2 * cx + cy], dst_ref=land[b].at[me],
                                             send_sem=send.at[3 * b + j], recv_sem=recv.at[3 * b + j],
                                             device_id=(cx, cy, c), device_id_type=MESH).start()
        token[...] = jnp.zeros_like(token)

    out = pl.pallas_call(
        body, name=name, in_specs=[HBM] * (2 * n),
        out_specs=[SEMS, SEMS] + [HBM] * (2 * n) + [pl.BlockSpec(memory_space=pltpu.VMEM)],
        out_shape=[pltpu.SemaphoreType.DMA((3 * n,)), pltpu.SemaphoreType.DMA((3 * n,))]
        + [pltpu.HBM(s.shape, s.dtype) for s in sums + landing] + [jax.ShapeDtypeStruct((8, 128), F32)],
        input_output_aliases={b: b + 2 for b in range(2 * n)},
        compiler_params=pltpu.CompilerParams(has_side_effects=FLOWS))(*[_in_hbm(s) for s in sums + landing])
    return out[0], out[1], list(out[2:2 + n]), list(out[2 + n:2 + 2 * n]), out[2 + 2 * n]


def chip_exchange_wait(send, recv, sums, landing, after, name):
    n = len(sums)

    def body(*refs):
        src, land = refs[:n], refs[n:2 * n]
        send_, recv_ = refs[2 * n], refs[2 * n + 1]
        me, c, chips = _other_chips()
        for b in range(n):
            for j, (cx, cy) in enumerate(chips):
                slot = 2 * cx + cy
                pltpu.make_async_remote_copy(src_ref=src[b].at[slot], dst_ref=land[b].at[me],
                                             send_sem=send_.at[3 * b + j], recv_sem=recv_.at[3 * b + j],
                                             device_id=(cx, cy, c), device_id_type=MESH).wait_send()
                pltpu.make_async_remote_copy(src_ref=src[b].at[me], dst_ref=land[b].at[slot],
                                             send_sem=send_.at[3 * b + j], recv_sem=recv_.at[3 * b + j],
                                             device_id=(cx, cy, c), device_id_type=MESH).wait_recv()

    out = pl.pallas_call(
        body, name=name, in_specs=[HBM] * (2 * n) + [SEMS, SEMS, ANY], out_specs=[HBM] * (2 * n),
        out_shape=[pltpu.HBM(s.shape, s.dtype) for s in sums + landing],
        input_output_aliases={b: b for b in range(2 * n)},
        compiler_params=pltpu.CompilerParams(has_side_effects=FLOWS))(*sums, *landing, send, recv, after)
    return list(out[n:])


def _row(a, l):
    return a[l:l + 1]


def local_step(x, p, tgt, small, weights_of, vecs, a_ws, a_bs, grads_ready):
    T = x.shape[0]
    bsf = jnp.broadcast_to(a_bs[:, :, None], (AG_N, CHUNK, AGW))
    vrow = lambda r: vecs[r:r + 1]
    saved = []
    W = []
    h = x
    GA = GB = GC = bgrp = None

    def ff_fwd(h, l, which, pre, post):
        j0 = 0 if which == 1 else 2
        down = B_FF1D(l) if which == 1 else B_FF2D(l)
        tag = "ff%d_l%d" % (which, l)
        xn = rms_fwd(h, _row(pre, l), BF16, tag + "_pre")
        gu, a = ff_gateup(xn, GA, A_FF(l, j0), tag + "_gateup")
        f, hn = mm_rs_post(a, GB, down, FB, FB, h, _row(post, l), 0.5, tag + "_down")
        return hn, (h, xn, gu, a, f)

    for l in range(4):
        rec = {}
        GA, GB, GC = weights_of(l, h)
        W.append((GA, GB, GC))
        if l == 1:
            bgrp = GC[:, C_BGRP:C_BGRP + 256, :].reshape(NCHIP, 4, 64, 256).transpose(1, 0, 2, 3).reshape(4, 256, 256)
        h, rec["ff1"] = ff_fwd(h, l, 1, small["ff1_pre_g"], small["ff1_post_g"])
        tag = "mix_l%d" % l
        h_in = h
        if l == 1:
            hn = rms_fwd(h, _row(small["mix_pre_g"], l), F32, tag + "_pre")
            pooled, ypre, f = pool_fwd(hn, bgrp, vrow(V_BSCALE), tag + "_pool")
            rec["mix"] = (h_in, pooled, ypre, f)
            h = post_res(h, f, _row(small["mix_post_g"], l), 1.0, tag + "_post")
        else:
            gpost = _row(small["mix_post_g"], l)
            hn = rms_fwd(h, _row(small["mix_pre_g"], l), BF16, tag + "_pre")
            if l == 0:
                zpre = mm_cs(hn, GA, A_AIN, 1536, 1536, BF16, tag + "_in")
                y = gmlp_mid_fwd(zpre, small["a_v_norm_g"], small["a_v_norm_b"], a_ws, bsf, tag + "_gate")
                f, h = mm_rs_post(y, GB, B_AOUT, 768, 768, h, gpost, 1.0, tag + "_out")
                rec["mix"] = (h_in, hn, zpre, y, f)
            elif l == 2:
                ag = mm_cs(hn, GA, A_CIN, 512, 512, BF16, tag + "_pw1")
                zc = conf_conv_fwd(ag, vecs[V_CDW:V_CDW + 32], vrow(V_CBDW), tag + "_conv")
                zs = conf_ln_fwd(zc, vrow(V_CNG), vrow(V_CNB), tag + "_ln")
                f, h = mm_rs_post(zs, GB, B_CPW2, 256, 256, h, gpost, 1.0, tag + "_pw2")
                rec["mix"] = (h_in, hn, ag, zc, zs, f)
            else:
                bgx = mm_cs(hn, GA, A_DIN, 768, 768, BF16, tag + "_in")
                y = sconv_fwd(bgx, vecs[V_DCONV:V_DCONV + 8], tag + "_conv")
                f, h = mm_rs_post(y, GB, B_DOUT, 256, 256, h, gpost, 1.0, tag + "_out")
                rec["mix"] = (h_in, hn, bgx, y, f)
        h, rec["ff2"] = ff_fwd(h, l, 2, small["ff2_pre_g"], small["ff2_post_g"])
        tag = "ple_l%d" % l
        xn = rms_fwd(h, _row(small["ple_gate_norm_g"], l), BF16, tag + "_pre")
        zg = mm_rs(xn, GB, B_PLEG(l), 256, 256, tag + "_gate")
        pb = p[l].astype(BF16)
        pe = mm_cs(pb, GC, 0, 256, 256, F32, tag + "_proj", roff=C_PROJ(l))
        rec["ple"] = (h, xn, zg, pe, pb)
        h = ple_post(h, zg, pe, _row(small["ple_post_g"], l), tag + "_post")
        saved.append(rec)

    dh, loss_cols = loss_head(h, tgt, "loss_head")

    gA = gB = gC = None
    layer_grads = [None] * 4
    tok = None
    gV = {}
    gains = {k: [None] * 4 for k in ("ff1_pre_g", "ff1_post_g", "mix_pre_g", "mix_post_g", "ff2_pre_g", "ff2_post_g",
                                      "ple_gate_norm_g", "ple_post_g")}
    extra = {}

    def ff_bwd(dh, l, which, pre, post, rec):
        j0 = 0 if which == 1 else 2
        down = B_FF1D(l) if which == 1 else B_FF2D(l)
        tag = "ff%d_l%d_b" % (which, l)
        h_in, xn, gu, a, f = rec
        df, dpost, dgu = ff_bwd_down(dh, f, _row(post, l), GB, down, gu, tag + "_down")
        gB[down] = dw_rs(a, df, FB, FB, tag + "_dwdown")
        dh_in, dpre = mm_cs_t_rms(dgu, GA, A_FF(l, j0), 2 * FB, 2 * FB, h_in, _row(pre, l), dh, tag + "_gateup")
        gA[A_FF(l, j0)] = dw_cs(xn, dgu, 2 * FB, 2 * FB, tag + "_dwgateup")
        return dh_in, dpre, dpost

    for l in reversed(range(4)):
        rec = saved[l]
        GA, GB, GC = W[l]
        gA, gB, gC = {}, {}, {}
        tag = "ple_l%d_b" % l
        h_in, xn, zg, pe, pb = rec["ple"]
        gpost = _row(small["ple_post_g"], l)
        if tok is not None:
            gpost = gpost + tok
        dzg, dpe, gains["ple_post_g"][l] = ple_post_bwd(dh, zg, pe, gpost, tag + "_post")
        gC[C_PROJ(l)] = dw_cs(pb, dpe, 256, 256, tag + "_dwproj")
        dxn = mm_rs_t(dzg, GB, B_PLEG(l), 256, 256, tag + "_gate")
        gB[B_PLEG(l)] = dw_rs(xn, dzg, 256, 256, tag + "_dwgate")
        dh, gains["ple_gate_norm_g"][l] = rms_bwd(dxn, h_in, _row(small["ple_gate_norm_g"], l), dh, tag + "_pre")

        dh, gains["ff2_pre_g"][l], gains["ff2_post_g"][l] = ff_bwd(
            dh, l, 2, small["ff2_pre_g"], small["ff2_post_g"], rec["ff2"])

        tag = "mix_l%d_b" % l
        mix = rec["mix"]
        h_in, f = mix[0], mix[-1]
        if l == 1:
            _, pooled, ypre, _ = mix
            df, gains["mix_post_g"][l] = post_res_bwd(dh, f, _row(small["mix_post_g"], l), 1.0, F32, tag + "_post")
            dhn, dwg, dsc = pool_bwd(df, ypre, pooled, bgrp, vrow(V_BSCALE), tag + "_pool")
            gC[C_BGRP] = dwg.astype(BF16).reshape(4, NCHIP, 64, 256).transpose(1, 0, 2, 3).reshape(NCHIP, 256, 256)
            gV[V_BSCALE] = jnp.pad(dsc, ((0, 7), (0, 0)))
            dh, gains["mix_pre_g"][l] = rms_bwd(dhn, h_in, _row(small["mix_pre_g"], l), dh, tag + "_pre")
        else:
            gpre = _row(small["mix_pre_g"], l)
            df, gains["mix_post_g"][l] = post_res_bwd(dh, f, _row(small["mix_post_g"], l), 1.0, BF16, tag + "_post")
            if l == 0:
                _, hn, zpre, y, _ = mix
                dy = mm_rs_t(df, GB, B_AOUT, 768, 768, tag + "_out")
                gB[B_AOUT] = dw_rs(y, df, 768, 768, tag + "_dwout")
                dz, dws, dbs, dvg, dvb = gmlp_mid_bwd(zpre, dy, small["a_v_norm_g"], small["a_v_norm_b"], a_ws, bsf,
                                                      tag + "_gate")
                extra.update(a_w_s=dws, a_b_s=dbs[:, 0, :], a_v_norm_g=dvg, a_v_norm_b=dvb)
                gA[A_AIN] = dw_cs(hn, dz, 1536, 1536, tag + "_dwin")
                dh, gains["mix_pre_g"][l] = mm_cs_t_rms(dz, GA, A_AIN, 1536, 1536, h_in, gpre, dh, tag + "_in")
            elif l == 2:
                _, hn, ag, zc, zs, _ = mix
                dzs = mm_rs_t(df, GB, B_CPW2, 256, 256, tag + "_pw2")
                gB[B_CPW2] = dw_rs(zs, df, 256, 256, tag + "_dwpw2")
                dzc, dng, dnb = conf_ln_bwd(dzs, zc, vrow(V_CNG), vrow(V_CNB), tag + "_ln")
                da_, dg_, dwdw, dbdw = conf_conv_bwd(dzc, ag, vecs[V_CDW:V_CDW + 32], tag + "_conv")
                dag = merge_cols([da_, dg_], tag + "_merge")
                gV[V_CDW] = dwdw
                gV[V_CBDW] = jnp.pad(dbdw, ((0, 7), (0, 0)))
                gV[V_CNG] = jnp.pad(dng, ((0, 7), (0, 0)))
                gV[V_CNB] = jnp.pad(dnb, ((0, 7), (0, 0)))
                gA[A_CIN] = dw_cs(hn, dag, 512, 512, tag + "_dwpw1")
                dh, gains["mix_pre_g"][l] = mm_cs_t_rms(dag, GA, A_CIN, 512, 512, h_in, gpre, dh, tag + "_pw1")
            else:
                _, hn, bgx, y, _ = mix
                dy = mm_rs_t(df, GB, B_DOUT, 256, 256, tag + "_out")
                gB[B_DOUT] = dw_rs(y, df, 256, 256, tag + "_dwout")
                db_, dc_, dx_, dwc = sconv_bwd(dy, bgx, vecs[V_DCONV:V_DCONV + 8], tag + "_conv")
                dbgx = merge_cols([db_, dc_, dx_], tag + "_merge")
                gV[V_DCONV] = dwc
                gA[A_DIN] = dw_cs(hn, dbgx, 768, 768, tag + "_dwin")
                dh, gains["mix_pre_g"][l] = mm_cs_t_rms(dbgx, GA, A_DIN, 768, 768, h_in, gpre, dh, tag + "_in")

        dh, gains["ff1_pre_g"][l], gains["ff1_post_g"][l] = ff_bwd(
            dh, l, 1, small["ff1_pre_g"], small["ff1_post_g"], rec["ff1"])
        layer_grads[l] = (gA, gB, gC)
        tok = grads_ready(l, gA, gB, gC, dh)

    return loss_cols, dh, layer_grads, gV, gains, extra


GAIN_NAMES = ("ff1_pre_g", "ff1_post_g", "mix_pre_g", "mix_post_g", "ff2_pre_g", "ff2_post_g", "ple_gate_norm_g",
              "ple_post_g")


def _pad_rows(a, rows):
    return jnp.pad(a, ((0, rows - a.shape[0]), (0, 0)))


def kernel(x, p, ff1_pre_g, ff1_w_gate, ff1_w_up, ff1_w_down, ff1_post_g, mix_pre_g, mix_post_g, ff2_pre_g, ff2_w_gate, ff2_w_up, ff2_w_down, ff2_post_g, ple_gate_norm_g, ple_w_gate, ple_w_proj, ple_post_g, a_w_in, a_v_norm_g, a_v_norm_b, a_w_s, a_b_s, a_w_out, b_w_grp, b_scale, c_w_pw1, c_w_dw, c_b_dw, c_norm_g, c_norm_b, c_w_pw2, d_w_in, d_w_conv, d_w_out, loss_target, m_ff1_pre_g, m_ff1_w_gate, m_ff1_w_up, m_ff1_w_down, m_ff1_post_g, m_mix_pre_g, m_mix_post_g, m_ff2_pre_g, m_ff2_w_gate, m_ff2_w_up, m_ff2_w_down, m_ff2_post_g, m_ple_gate_norm_g, m_ple_w_gate, m_ple_w_proj, m_ple_post_g, m_a_w_in, m_a_v_norm_g, m_a_v_norm_b, m_a_w_s, m_a_b_s, m_a_w_out, m_b_w_grp, m_b_scale, m_c_w_pw1, m_c_w_dw, m_c_b_dw, m_c_norm_g, m_c_norm_b, m_c_w_pw2, m_d_w_in, m_d_w_conv, m_d_w_out, v_ff1_pre_g, v_ff1_w_gate, v_ff1_w_up, v_ff1_w_down, v_ff1_post_g, v_mix_pre_g, v_mix_post_g, v_ff2_pre_g, v_ff2_w_gate, v_ff2_w_up, v_ff2_w_down, v_ff2_post_g, v_ple_gate_norm_g, v_ple_w_gate, v_ple_w_proj, v_ple_post_g, v_a_w_in, v_a_v_norm_g, v_a_v_norm_b, v_a_w_s, v_a_b_s, v_a_w_out, v_b_w_grp, v_b_scale, v_c_w_pw1, v_c_w_dw, v_c_b_dw, v_c_norm_g, v_c_norm_b, v_c_w_pw2, v_d_w_in, v_d_w_conv, v_d_w_out):
    args = dict(locals())
    wnames = ["ff1_pre_g", "ff1_w_gate", "ff1_w_up", "ff1_w_down", "ff1_post_g", "mix_pre_g", "mix_post_g",
              "ff2_pre_g", "ff2_w_gate", "ff2_w_up", "ff2_w_down", "ff2_post_g", "ple_gate_norm_g", "ple_w_gate",
              "ple_w_proj", "ple_post_g", "a_w_in", "a_v_norm_g", "a_v_norm_b", "a_w_s", "a_b_s", "a_w_out",
              "b_w_grp", "b_scale", "c_w_pw1", "c_w_dw", "c_b_dw", "c_norm_g", "c_norm_b", "c_w_pw2", "d_w_in",
              "d_w_conv", "d_w_out"]

    PA, PB, PC, PV = pack_weights(args)
    G0 = all_gather_chips([PA[0], PB[0], PC[0], PV], "gather_l0")
    vecs = G0[3].transpose(1, 0, 2).reshape(V_ROWS, DM)
    flying = {}
    tok = 0.0
    for l in (1, 2, 3):
        send, recv, slots, token = gather_start(own_slots([PA[l], PB[l], PC[l]]), "gather_start_l%d" % l)
        flying[l] = (send, recv, slots)
        tok = tok + token[0, 0]

    def weights_of(l, h):
        if l == 0:
            return G0[0], G0[1], G0[2]
        send, recv, slots = flying[l]
        landed = gather_wait(send, recv, slots, list(range(3)), h, "gather_wait_l%d" % l)
        return tuple(gather_pass(landed, "gather_pass_l%d" % l))

    pending = {}
    reduced = {}

    def finish(l, after):
        send, recv, sums, landing = pending.pop(l)
        landed = chip_exchange_wait(send, recv, sums, landing, after, "rs_wait_l%d" % l)
        halves = [sum_slots(landed[i], *_tile(k, landed[i]), name="rs_sum_chips_l%d_%s" % (l, k))
                  for i, k in enumerate("ABC")]
        reduced[l] = pair_share(halves, "rs_pair_share_l%d" % l)

    def grads_ready(l, gA, gB, gC, dh):
        if l + 1 in pending:
            finish(l + 1, dh)
        if l == 0:
            return None
        parts = pack_layer_grads(l, gA, gB, gC)
        got = pair_exchange(list(parts), "rs_pair_exchange_l%d" % l)
        sums = [add_half(parts[i], got[i], *_tile(k, got[i]), name="rs_add_pair_l%d_%s" % (l, k))
                for i, k in enumerate("ABC")]
        send, recv, sums, landing, token = chip_exchange_start(sums, "rs_start_l%d" % l)
        pending[l] = (send, recv, sums, landing)
        return token[0, 0]

    small = {k: args[k] for k in GAIN_NAMES}
    small["ff1_pre_g"] = ff1_pre_g + tok
    small["a_v_norm_g"] = a_v_norm_g
    small["a_v_norm_b"] = a_v_norm_b
    loss_cols, grad_x, layer_grads, gV, gains, extra = local_step(
        x[0], p[:, 0], loss_target[0], small, weights_of, vecs, a_w_s[0], a_b_s[0], grads_ready)

    loss = lax.psum((0.5 / DM) * jnp.sum(loss_cols), ("x", "y", "c"))

    dA0, dB0, dC0 = pack_layer_grads(0, *layer_grads[0])
    dV, dE = pack_small_grads(gV, gains, extra)
    red = reduce_scatter({"A": dA0, "B": dB0, "C": dC0, "V": dV, "E": dE}, "rs_l0")
    reduced[0] = [red["A"], red["B"], red["C"]]
    (gE,) = all_gather_chips([red["E"]], "gather_replicated_grads")
    grads = unpack_grads([reduced[l][0] for l in range(4)], [reduced[l][1] for l in range(4)],
                         [reduced[l][2] for l in range(4)], red["V"], gE.reshape(E_ROWS, DM))

    deltas, new_m, new_v = {}, {}, {}
    for k in wnames:
        deltas[k], new_m[k], new_v[k] = adamw(args[k], grads[k], args["m_" + k], args["v_" + k], "adamw_" + k)
    return (loss, grad_x[None], *[grads[k] for k in wnames], *[deltas[k] for k in wnames],
            *[new_m[k] for k in wnames], *[new_v[k] for k in wnames])


def pack_weights(w):
    padc = lambda a: jnp.pad(a, ((0, 0), (0, FB - FW)))
    mix_in = [w["a_w_in"][0], None, w["c_w_pw1"][0], w["d_w_in"][0]]
    mix_out = [w["a_w_out"][0], None, w["c_w_pw2"][0], w["d_w_out"][0]]
    PA, PB, PC = [], [], []
    for l in range(4):
        cols = [padc(w["ff1_w_gate"][l]), padc(w["ff1_w_up"][l]), padc(w["ff2_w_gate"][l]), padc(w["ff2_w_up"][l])]
        rows = [_pad_rows(w["ff1_w_down"][l], FB), _pad_rows(w["ff2_w_down"][l], FB)]
        if l != 1:
            cols.append(mix_in[l])
            rows.append(mix_out[l])
        rows.append(w["ple_w_gate"][l])
        PA.append(jnp.concatenate(cols, axis=1).astype(BF16))
        PB.append(jnp.concatenate(rows, axis=0).astype(BF16))
        proj = [w["ple_w_proj"][l]] + ([w["b_w_grp"][0].reshape(256, 256)] if l == 1 else [])
        PC.append(jnp.concatenate(proj, axis=0).astype(BF16))
    PV = jnp.concatenate([_pad_rows(w["b_scale"], 8), _pad_rows(w["c_b_dw"], 8), _pad_rows(w["c_norm_g"], 8),
                          _pad_rows(w["c_norm_b"], 8), _pad_rows(w["d_w_conv"][0], 8), _pad_rows(w["c_w_dw"][0], 40)],
                         axis=0)
    return PA, PB, PC, PV


def pack_layer_grads(l, gA, gB, gC):
    a = [gA[A_FF(l, 0)], gA[A_FF(l, 2)]] + ([gA[A_AIN]] if l != 1 else [])
    b = [gB[B_FF1D(l)], gB[B_FF2D(l)]] + ([gB[B_AOUT]] if l != 1 else []) + [gB[B_PLEG(l)]]
    c = [gC[C_PROJ(l)]] + ([gC[C_BGRP]] if l == 1 else [])
    return jnp.concatenate(a, axis=2), jnp.concatenate(b, axis=1), jnp.concatenate(c, axis=1)


def pack_small_grads(gV, gains, extra):
    dVt = jnp.concatenate([gV[V_BSCALE], gV[V_CBDW], gV[V_CNG], gV[V_CNB], gV[V_DCONV], gV[V_CDW],
                           jnp.zeros((8, DM), F32)], axis=0)
    dV = dVt.reshape(V_ROWS, NCHIP, 256).transpose(1, 0, 2)
    rowsE = [_pad_rows(jnp.concatenate(gains[k], axis=0), 8) for k in GAIN_NAMES]
    rowsE += [_pad_rows(extra["a_v_norm_g"].reshape(3, DM), 8), _pad_rows(extra["a_v_norm_b"].reshape(3, DM), 8),
              jnp.pad(extra["a_b_s"].reshape(1536), (0, 8 * DM - 1536)).reshape(8, DM),
              extra["a_w_s"].reshape(192, DM)]
    dE = _pad_rows(jnp.concatenate(rowsE, axis=0), E_ROWS).reshape(NCHIP, E_ROWS // NCHIP, DM)
    return dV, dE


def unpack_grads(RAs, RBs, RCs, RV, gE):
    grads = {}
    RA = RB = RCc = None
    for i, k in enumerate(GAIN_NAMES):
        grads[k] = gE[8 * i:8 * i + 4]
    grads["a_v_norm_g"] = gE[64:67].reshape(1, 3072)
    grads["a_v_norm_b"] = gE[72:75].reshape(1, 3072)
    grads["a_b_s"] = gE[80:88].reshape(8 * DM)[:1536].reshape(1, 12, 128)
    grads["a_w_s"] = gE[88:280].reshape(1, 12, 128, 128)
    colA = lambda l, off, n: RAs[l][:, off:off + n]
    grads["ff1_w_gate"] = jnp.stack([colA(l, A_FF(l, 0), FW) for l in range(4)])
    grads["ff1_w_up"] = jnp.stack([colA(l, A_FF(l, 1), FW) for l in range(4)])
    grads["ff2_w_gate"] = jnp.stack([colA(l, A_FF(l, 2), FW) for l in range(4)])
    grads["ff2_w_up"] = jnp.stack([colA(l, A_FF(l, 3), FW) for l in range(4)])
    grads["a_w_in"] = colA(0, A_AIN, 1536)[None]
    grads["c_w_pw1"] = colA(2, A_CIN, 512)[None]
    grads["d_w_in"] = colA(3, A_DIN, 768)[None]
    rowB = lambda l, off, n: RBs[l][off:off + n]
    grads["ff1_w_down"] = jnp.stack([rowB(l, B_FF1D(l), FW) for l in range(4)])
    grads["ff2_w_down"] = jnp.stack([rowB(l, B_FF2D(l), FW) for l in range(4)])
    grads["ple_w_gate"] = jnp.stack([rowB(l, B_PLEG(l), 256) for l in range(4)])
    grads["a_w_out"] = rowB(0, B_AOUT, 768)[None]
    grads["c_w_pw2"] = rowB(2, B_CPW2, 256)[None]
    grads["d_w_out"] = rowB(3, B_DOUT, 256)[None]
    grads["ple_w_proj"] = jnp.stack([RCs[l][C_PROJ(l):C_PROJ(l) + 256] for l in range(4)])
    grads["b_w_grp"] = RCs[1][C_BGRP:C_BGRP + 256].reshape(1, 4, 64, 256)
    grads["b_scale"] = RV[V_BSCALE:V_BSCALE + 1]
    grads["c_b_dw"] = RV[V_CBDW:V_CBDW + 1]
    grads["c_norm_g"] = RV[V_CNG:V_CNG + 1]
    grads["c_norm_b"] = RV[V_CNB:V_CNB + 1]
    grads["d_w_conv"] = RV[V_DCONV:V_DCONV + 3][None]
    grads["c_w_dw"] = RV[V_CDW:V_CDW + 31][None]
    return grads
```

```python
import functools
import math

import jax
import jax.numpy as jnp
from jax import lax
from jax.experimental import pallas as pl
from jax.experimental.pallas import tpu as pltpu

F32, BF16 = jnp.float32, jnp.bfloat16
EPS = 1e-6
DM = 1024
FW = 704
FB = 768
NCHIP = 4
VMEM_LIMIT = 56 * 1024 * 1024
ANY = pl.BlockSpec(memory_space=pl.ANY)
MESH = pl.DeviceIdType.MESH

A_FF = lambda l, j: (j % 2) * FB
A_AIN = A_CIN = A_DIN = 2 * FB
A2_COLS = lambda l: 2 * FB + (1536, 0, 512, 768)[l]
B_FF1D = lambda l: 0
B_FF2D = lambda l: 0
B_AOUT = B_CPW2 = B_DOUT = FB
B_PLEG = lambda l: FB + (768, 0, 256, 256)[l]
B2_ROWS = lambda l: B_PLEG(l) + 256
C_PROJ = lambda l: 0
C_BGRP = 256
V_BSCALE, V_CBDW, V_CNG, V_CNB, V_DCONV, V_CDW, V_ROWS = 0, 8, 16, 24, 32, 40, 80
E_ROWS = 320

ADAM_LR, ADAM_B1, ADAM_B2, ADAM_EPS, ADAM_WD, ADAM_STEP = 0.001, 0.9, 0.999, 1e-08, 0.01, 10


def _cp(sem):
    return pltpu.CompilerParams(dimension_semantics=sem, vmem_limit_bytes=VMEM_LIMIT)


def _sig(x):
    return 1.0 / (1.0 + jnp.exp(-x))


_GC = math.sqrt(2.0 / math.pi)


def _gelu(x):
    return 0.5 * x * (1.0 + jnp.tanh(_GC * (x + 0.044715 * x * x * x)))


def _gelu_grad(x):
    t = jnp.tanh(_GC * (x + 0.044715 * x * x * x))
    return 0.5 * (1.0 + t) + 0.5 * x * (1.0 - t * t) * _GC * (1.0 + 3.0 * 0.044715 * x * x)


def _dot_nn(a, b):
    return lax.dot_general(a, b, (((1,), (0,)), ((), ())), preferred_element_type=F32)


def _dot_nt(a, b):
    return lax.dot_general(a, b, (((1,), (1,)), ((), ())), preferred_element_type=F32)


def _dot_tn(a, b):
    return lax.dot_general(a, b, (((0,), (0,)), ((), ())), preferred_element_type=F32)


def mm_cs(x, G, off, nb, tn, out_dtype, name, roff=0):
    T, K = x.shape
    tm = min(1024, T)
    nj, ob, rb_ = nb // tn, off // tn, roff // K
    assert nb % tn == 0 and off % tn == 0 and roff % K == 0

    def body(x_ref, w_ref, o_ref):
        o_ref[...] = _dot_nn(x_ref[...], w_ref[...]).astype(o_ref.dtype)

    return pl.pallas_call(
        body, name=name, grid=(T // tm, NCHIP, nj),
        in_specs=[pl.BlockSpec((tm, K), lambda i, s, j: (i, 0)),
                  pl.BlockSpec((None, K, tn), lambda i, s, j: (s, rb_, ob + j))],
        out_specs=pl.BlockSpec((tm, tn), lambda i, s, j: (i, s * nj + j)),
        out_shape=jax.ShapeDtypeStruct((T, NCHIP * nb), out_dtype),
        compiler_params=_cp(("parallel", "arbitrary", "arbitrary")))(x, G)


def mm_cs_t(dy, G, off, nb, tn, name):
    T = dy.shape[0]
    K = G.shape[1]
    tm = min(1024, T)
    nj, ob = nb // tn, off // tn
    nk = NCHIP * nj

    def body(dy_ref, w_ref, o_ref, acc):
        k = pl.program_id(1)

        @pl.when(k == 0)
        def _():
            acc[...] = jnp.zeros_like(acc)

        acc[...] += _dot_nt(dy_ref[...], w_ref[...])

        @pl.when(k == nk - 1)
        def _():
            o_ref[...] = acc[...]

    return pl.pallas_call(
        body, name=name, grid=(T // tm, nk),
        in_specs=[pl.BlockSpec((tm, tn), lambda i, k: (i, k)),
                  pl.BlockSpec((None, K, tn), lambda i, k: (k // nj, 0, ob + k % nj))],
        out_specs=pl.BlockSpec((tm, K), lambda i, k: (i, 0)),
        out_shape=jax.ShapeDtypeStruct((T, K), F32),
        scratch_shapes=[pltpu.VMEM((tm, K), F32)],
        compiler_params=_cp(("parallel", "arbitrary")))(dy, G)


def mm_rs(a, G, off, rb, tk, name):
    T = a.shape[0]
    N = G.shape[2]
    tm = min(1024, T)
    nkk, ob = rb // tk, off // tk
    nk = NCHIP * nkk
    assert rb % tk == 0 and off % tk == 0

    def body(a_ref, w_ref, o_ref, acc):
        k = pl.program_id(1)

        @pl.when(k == 0)
        def _():
            acc[...] = jnp.zeros_like(acc)

        acc[...] += _dot_nn(a_ref[...], w_ref[...])

        @pl.when(k == nk - 1)
        def _():
            o_ref[...] = acc[...]

    return pl.pallas_call(
        body, name=name, grid=(T // tm, nk),
        in_specs=[pl.BlockSpec((tm, tk), lambda i, k: (i, k)),
                  pl.BlockSpec((None, tk, N), lambda i, k: (k // nkk, ob + k % nkk, 0))],
        out_specs=pl.BlockSpec((tm, N), lambda i, k: (i, 0)),
        out_shape=jax.ShapeDtypeStruct((T, N), F32),
        scratch_shapes=[pltpu.VMEM((tm, N), F32)],
        compiler_params=_cp(("parallel", "arbitrary")))(a, G)


def mm_rs_t(dy, G, off, rb, tk, name):
    T, N = dy.shape
    tm = min(1024, T)
    nkk, ob = rb // tk, off // tk
    nk = NCHIP * nkk

    def body(dy_ref, w_ref, o_ref):
        o_ref[...] = _dot_nt(dy_ref[...], w_ref[...]).astype(o_ref.dtype)

    return pl.pallas_call(
        body, name=name, grid=(T // tm, nk),
        in_specs=[pl.BlockSpec((tm, N), lambda i, k: (i, 0)),
                  pl.BlockSpec((None, tk, N), lambda i, k: (k // nkk, ob + k % nkk, 0))],
        out_specs=pl.BlockSpec((tm, tk), lambda i, k: (i, k)),
        out_shape=jax.ShapeDtypeStruct((T, NCHIP * rb), BF16),
        compiler_params=_cp(("parallel", "arbitrary")))(dy, G)


def mm_tn(a, b, tmm, tn, out_shape, out_map, name, into=None):
    T, M = a.shape
    N = b.shape[1]
    tt = min(2048, T)
    nt = T // tt

    def body(a_ref, b_ref, o_ref, acc):
        t = pl.program_id(2)

        @pl.when(t == 0)
        def _():
            acc[...] = jnp.zeros_like(acc)

        acc[...] += _dot_tn(a_ref[...], b_ref[...])

        @pl.when(t == nt - 1)
        def _():
            o_ref[...] = acc[...].astype(o_ref.dtype)

    in_specs = [pl.BlockSpec((tt, tmm), lambda i, j, t: (t, i)), pl.BlockSpec((tt, tn), lambda i, j, t: (t, j))]
    operands = (a, b)
    if into is None:
        def kern(a_ref, b_ref, o_ref, acc):
            body(a_ref, b_ref, o_ref, acc)
        aliases = {}
    else:
        def kern(a_ref, b_ref, into_ref, o_ref, acc):
            body(a_ref, b_ref, o_ref, acc)
        in_specs.append(ANY)
        operands = (a, b, into)
        aliases = {2: 0}
        out_shape = into.shape
    return pl.pallas_call(
        kern, name=name, grid=(M // tmm, N // tn, nt), in_specs=in_specs,
        out_specs=pl.BlockSpec((None, tmm, tn), lambda i, j, t: out_map(i, j)),
        out_shape=jax.ShapeDtypeStruct(out_shape, BF16), input_output_aliases=aliases,
        scratch_shapes=[pltpu.VMEM((tmm, tn), F32)],
        compiler_params=_cp(("parallel", "parallel", "arbitrary")))(*operands)


def dw_cs(x, dy, nb, tn, name, width=None, off=0, into=None):
    K = x.shape[1]
    nj, ob = nb // tn, off // tn
    assert off % tn == 0
    return mm_tn(x, dy, K, tn, (NCHIP, K, width or nb), lambda i, j: (j // nj, 0, ob + j % nj), name, into)


def dw_rs(a, dy, rb, tr, name, height=None, off=0, into=None):
    N = dy.shape[1]
    ni, ob = rb // tr, off // tr
    assert off % tr == 0
    return mm_tn(a, dy, tr, N, (NCHIP, height or rb, N), lambda i, j: (i // ni, ob + i % ni, 0), name, into)


def ff_gateup(xn, GA, off, name):
    T, K = xn.shape
    tm = min(1024, T)
    ob = off // (2 * FB)
    assert off % (2 * FB) == 0

    sub = min(512, tm)

    def body(x_ref, w_ref, gu_ref, a_ref):
        for r0 in range(0, tm, sub):
            r = _dot_nn(x_ref[r0:r0 + sub, :], w_ref[...])
            g, u = r[:, :FB], r[:, FB:]
            gu_ref[r0:r0 + sub, :] = r.astype(gu_ref.dtype)
            a_ref[r0:r0 + sub, :] = (g * _sig(g) * u).astype(a_ref.dtype)

    return pl.pallas_call(
        body, name=name, grid=(T // tm, NCHIP),
        in_specs=[pl.BlockSpec((tm, K), lambda i, s: (i, 0)),
                  pl.BlockSpec((None, K, 2 * FB), lambda i, s: (s, 0, ob))],
        out_specs=[pl.BlockSpec((tm, 2 * FB), lambda i, s: (i, s)), pl.BlockSpec((tm, FB), lambda i, s: (i, s))],
        out_shape=[jax.ShapeDtypeStruct((T, NCHIP * 2 * FB), BF16), jax.ShapeDtypeStruct((T, NCHIP * FB), BF16)],
        compiler_params=_cp(("parallel", "arbitrary")))(xn, GA)


def mm_rs_post(a, G, off, rb, tk, h, g, scale, name):
    T = a.shape[0]
    N = G.shape[2]
    tm = min(1024, T)
    sub = min(512, tm)
    nkk, ob = rb // tk, off // tk
    nk = NCHIP * nkk
    assert rb % tk == 0 and off % tk == 0

    def body(a_ref, w_ref, h_ref, g_ref, f_ref, o_ref, acc):
        k = pl.program_id(1)

        @pl.when(k == 0)
        def _():
            acc[...] = jnp.zeros_like(acc)

        acc[...] += _dot_nn(a_ref[...], w_ref[...])

        @pl.when(k == nk - 1)
        def _():
            for r0 in range(0, tm, sub):
                f = acc[r0:r0 + sub, :]
                f_ref[r0:r0 + sub, :] = f
                r = lax.rsqrt(jnp.mean(f * f, axis=-1, keepdims=True) + EPS)
                o_ref[r0:r0 + sub, :] = h_ref[r0:r0 + sub, :] + scale * (f * r * g_ref[...])

    row = pl.BlockSpec((tm, N), lambda i, k: (i, 0))
    row1 = pl.BlockSpec((tm, N), lambda i, k: (i, 0), pipeline_mode=pl.Buffered(1))
    return pl.pallas_call(
        body, name=name, grid=(T // tm, nk),
        in_specs=[pl.BlockSpec((tm, tk), lambda i, k: (i, k)),
                  pl.BlockSpec((None, tk, N), lambda i, k: (k // nkk, ob + k % nkk, 0)),
                  row1, pl.BlockSpec((1, N), lambda i, k: (0, 0))],
        out_specs=[row, row],
        out_shape=[jax.ShapeDtypeStruct((T, N), F32), jax.ShapeDtypeStruct((T, N), F32)],
        scratch_shapes=[pltpu.VMEM((tm, N), F32)],
        compiler_params=_cp(("parallel", "arbitrary")))(a, G, h, g)


def ff_bwd_down(dh, f, g, GB, down, gu, name):
    T, N = dh.shape
    tm = min(1024, T)
    sub = min(512, tm)
    ob = down // FB

    def body(d_ref, f_ref, g_ref, w_ref, gu_ref, df_ref, dg_ref, dgu_ref):
        i, s = pl.program_id(0), pl.program_id(1)

        @pl.when(s == 0)
        def _():
            dg = jnp.zeros((1, N), F32)
            for r0 in range(0, tm, sub):
                f = f_ref[r0:r0 + sub, :]
                r = lax.rsqrt(jnp.mean(f * f, axis=-1, keepdims=True) + EPS)
                d = 0.5 * d_ref[r0:r0 + sub, :]
                t = d * g_ref[...]
                df_ref[r0:r0 + sub, :] = (
                    r * t - f * (r * r * r * jnp.mean(t * f, axis=-1, keepdims=True))).astype(df_ref.dtype)
                dg = dg + jnp.sum(d * f * r, axis=0, keepdims=True)
            _acc_rows(dg_ref, i, dg)

        for r0 in range(0, tm, sub):
            da = _dot_nt(df_ref[r0:r0 + sub, :], w_ref[...])
            gt = gu_ref[r0:r0 + sub, :FB].astype(F32)
            u = gu_ref[r0:r0 + sub, FB:].astype(F32)
            sg = _sig(gt)
            dgu_ref[r0:r0 + sub, :FB] = (da * u * (sg * (1.0 + gt * (1.0 - sg)))).astype(dgu_ref.dtype)
            dgu_ref[r0:r0 + sub, FB:] = (da * (gt * sg)).astype(dgu_ref.dtype)

    row1 = pl.BlockSpec((tm, N), lambda i, s: (i, 0), pipeline_mode=pl.Buffered(1))
    row = pl.BlockSpec((tm, N), lambda i, s: (i, 0))
    vec = pl.BlockSpec((1, N), lambda i, s: (0, 0))
    return pl.pallas_call(
        body, name=name, grid=(T // tm, NCHIP),
        in_specs=[row1, row1, vec, pl.BlockSpec((None, FB, N), lambda i, s: (s, ob, 0)),
                  pl.BlockSpec((tm, 2 * FB), lambda i, s: (i, s))],
        out_specs=[row, vec, pl.BlockSpec((tm, 2 * FB), lambda i, s: (i, s))],
        out_shape=[jax.ShapeDtypeStruct((T, N), BF16), jax.ShapeDtypeStruct((1, N), F32),
                   jax.ShapeDtypeStruct((T, NCHIP * 2 * FB), BF16)],
        compiler_params=_cp(("arbitrary", "arbitrary")))(dh, f, g, GB, gu)


def mm_cs_t_rms(dy, G, off, nb, tn, h, g, skip, name):
    T = dy.shape[0]
    K = G.shape[1]
    tm = min(1024, T)
    sub = min(512, tm)
    nj, ob = nb // tn, off // tn
    nk = NCHIP * nj
    assert nb % tn == 0 and off % tn == 0

    def body(dy_ref, w_ref, h_ref, g_ref, s_ref, o_ref, dg_ref, acc):
        i, k = pl.program_id(0), pl.program_id(1)

        @pl.when(k == 0)
        def _():
            acc[...] = jnp.zeros_like(acc)

        acc[...] += _dot_nt(dy_ref[...], w_ref[...])

        @pl.when(k == nk - 1)
        def _():
            dg = jnp.zeros((1, K), F32)
            for r0 in range(0, tm, sub):
                d = acc[r0:r0 + sub, :]
                x = h_ref[r0:r0 + sub, :]
                r = lax.rsqrt(jnp.mean(x * x, axis=-1, keepdims=True) + EPS)
                xh = x * r
                t = d * g_ref[...]
                o_ref[r0:r0 + sub, :] = s_ref[r0:r0 + sub, :] + r * (t - xh * jnp.mean(t * xh, axis=-1, keepdims=True))
                dg = dg + jnp.sum(d * xh, axis=0, keepdims=True)
            _acc_rows(dg_ref, i, dg)

    row1 = pl.BlockSpec((tm, K), lambda i, k: (i, 0), pipeline_mode=pl.Buffered(1))
    row = pl.BlockSpec((tm, K), lambda i, k: (i, 0))
    vec = pl.BlockSpec((1, K), lambda i, k: (0, 0))
    return pl.pallas_call(
        body, name=name, grid=(T // tm, nk),
        in_specs=[pl.BlockSpec((tm, tn), lambda i, k: (i, k)),
                  pl.BlockSpec((None, K, tn), lambda i, k: (k // nj, 0, ob + k % nj)), row1, vec, row1],
        out_specs=[row, vec],
        out_shape=[jax.ShapeDtypeStruct((T, K), F32), jax.ShapeDtypeStruct((1, K), F32)],
        scratch_shapes=[pltpu.VMEM((tm, K), F32)],
        compiler_params=_cp(("arbitrary", "arbitrary")))(dy, G, h, g, skip)


def _rows(tm, C):
    return pl.BlockSpec((tm, C), lambda i: (i, 0))


def _vec(C):
    return pl.BlockSpec((1, C), lambda i: (0, 0))


def _acc_rows(ref, i, val):
    @pl.when(i == 0)
    def _():
        ref[...] = val

    @pl.when(i > 0)
    def _():
        ref[...] += val


def rms_fwd(h, g, out_dtype, name):
    T, C = h.shape
    tm = min(512, T)

    def body(h_ref, g_ref, o_ref):
        x = h_ref[...]
        r = lax.rsqrt(jnp.mean(x * x, axis=-1, keepdims=True) + EPS)
        o_ref[...] = (x * r * g_ref[...]).astype(o_ref.dtype)

    return pl.pallas_call(
        body, name=name, grid=(T // tm,), in_specs=[_rows(tm, C), _vec(C)], out_specs=_rows(tm, C),
        out_shape=jax.ShapeDtypeStruct((T, C), out_dtype), compiler_params=_cp(("parallel",)))(h, g)


def rms_bwd(dxn, h, g, dh_skip, name):
    T, C = h.shape
    tm = min(512, T)

    def body(d_ref, h_ref, g_ref, s_ref, o_ref, dg_ref):
        i = pl.program_id(0)
        x = h_ref[...]
        r = lax.rsqrt(jnp.mean(x * x, axis=-1, keepdims=True) + EPS)
        xh = x * r
        d = d_ref[...].astype(F32)
        t = d * g_ref[...]
        o_ref[...] = s_ref[...] + r * (t - xh * jnp.mean(t * xh, axis=-1, keepdims=True))
        _acc_rows(dg_ref, i, jnp.sum(d * xh, axis=0, keepdims=True))

    return pl.pallas_call(
        body, name=name, grid=(T // tm,),
        in_specs=[_rows(tm, C), _rows(tm, C), _vec(C), _rows(tm, C)],
        out_specs=[_rows(tm, C), _vec(C)],
        out_shape=[jax.ShapeDtypeStruct((T, C), F32), jax.ShapeDtypeStruct((1, C), F32)],
        compiler_params=_cp(("arbitrary",)))(dxn, h, g, dh_skip)


def post_res(h, f, g, scale, name):
    T, C = h.shape
    tm = min(512, T)

    def body(h_ref, f_ref, g_ref, o_ref):
        f = f_ref[...]
        r = lax.rsqrt(jnp.mean(f * f, axis=-1, keepdims=True) + EPS)
        o_ref[...] = h_ref[...] + scale * (f * r * g_ref[...])

    return pl.pallas_call(
        body, name=name, grid=(T // tm,), in_specs=[_rows(tm, C), _rows(tm, C), _vec(C)],
        out_specs=_rows(tm, C), out_shape=jax.ShapeDtypeStruct((T, C), F32),
        compiler_params=_cp(("parallel",)))(h, f, g)


def post_res_bwd(dh, f, g, scale, out_dtype, name):
    T, C = dh.shape
    tm = min(512, T)

    def body(d_ref, f_ref, g_ref, o_ref, dg_ref):
        i = pl.program_id(0)
        f = f_ref[...]
        r = lax.rsqrt(jnp.mean(f * f, axis=-1, keepdims=True) + EPS)
        d = scale * d_ref[...]
        t = d * g_ref[...]
        o_ref[...] = (r * t - f * (r * r * r * jnp.mean(t * f, axis=-1, keepdims=True))).astype(o_ref.dtype)
        _acc_rows(dg_ref, i, jnp.sum(d * f * r, axis=0, keepdims=True))

    return pl.pallas_call(
        body, name=name, grid=(T // tm,), in_specs=[_rows(tm, C), _rows(tm, C), _vec(C)],
        out_specs=[_rows(tm, C), _vec(C)],
        out_shape=[jax.ShapeDtypeStruct((T, C), out_dtype), jax.ShapeDtypeStruct((1, C), F32)],
        compiler_params=_cp(("arbitrary",)))(dh, f, g)


def ff_act(gu, name):
    T = gu.shape[0]
    tm = min(512, T)

    def body(gu_ref, o_ref):
        g = gu_ref[:, :FB].astype(F32)
        u = gu_ref[:, FB:].astype(F32)
        o_ref[...] = (g * _sig(g) * u).astype(o_ref.dtype)

    return pl.pallas_call(
        body, name=name, grid=(T // tm, NCHIP),
        in_specs=[pl.BlockSpec((tm, 2 * FB), lambda i, s: (i, s))],
        out_specs=pl.BlockSpec((tm, FB), lambda i, s: (i, s)),
        out_shape=jax.ShapeDtypeStruct((T, NCHIP * FB), BF16),
        compiler_params=_cp(("parallel", "parallel")))(gu)


def ff_act_bwd(da, gu, name):
    T = gu.shape[0]
    tm = min(512, T)

    def body(da_ref, gu_ref, o_ref):
        g = gu_ref[:, :FB].astype(F32)
        u = gu_ref[:, FB:].astype(F32)
        da = da_ref[...].astype(F32)
        s = _sig(g)
        o_ref[:, :FB] = (da * u * (s * (1.0 + g * (1.0 - s)))).astype(o_ref.dtype)
        o_ref[:, FB:] = (da * (g * s)).astype(o_ref.dtype)

    return pl.pallas_call(
        body, name=name, grid=(T // tm, NCHIP),
        in_specs=[pl.BlockSpec((tm, FB), lambda i, s: (i, s)), pl.BlockSpec((tm, 2 * FB), lambda i, s: (i, s))],
        out_specs=pl.BlockSpec((tm, 2 * FB), lambda i, s: (i, s)),
        out_shape=jax.ShapeDtypeStruct((T, NCHIP * 2 * FB), BF16),
        compiler_params=_cp(("parallel", "parallel")))(da, gu)


def ple_post(h, zg, pe, g, name):
    T, C = h.shape
    tm = min(512, T)

    def body(h_ref, z_ref, p_ref, g_ref, o_ref):
        e = p_ref[...] * _sig(z_ref[...])
        r = lax.rsqrt(jnp.mean(e * e, axis=-1, keepdims=True) + EPS)
        o_ref[...] = h_ref[...] + e * r * g_ref[...]

    return pl.pallas_call(
        body, name=name, grid=(T // tm,), in_specs=[_rows(tm, C), _rows(tm, C), _rows(tm, C), _vec(C)],
        out_specs=_rows(tm, C), out_shape=jax.ShapeDtypeStruct((T, C), F32),
        compiler_params=_cp(("parallel",)))(h, zg, pe, g)


def ple_post_bwd(dh, zg, pe, g, name):
    T, C = dh.shape
    tm = min(512, T)

    def body(d_ref, z_ref, p_ref, g_ref, dz_ref, dp_ref, dg_ref):
        i = pl.program_id(0)
        s = _sig(z_ref[...])
        pe_ = p_ref[...]
        e = pe_ * s
        r = lax.rsqrt(jnp.mean(e * e, axis=-1, keepdims=True) + EPS)
        d = d_ref[...]
        t = d * g_ref[...]
        de = r * t - e * (r * r * r * jnp.mean(t * e, axis=-1, keepdims=True))
        dp_ref[...] = (de * s).astype(dp_ref.dtype)
        dz_ref[...] = (de * pe_ * s * (1.0 - s)).astype(dz_ref.dtype)
        _acc_rows(dg_ref, i, jnp.sum(d * e * r, axis=0, keepdims=True))

    return pl.pallas_call(
        body, name=name, grid=(T // tm,), in_specs=[_rows(tm, C), _rows(tm, C), _rows(tm, C), _vec(C)],
        out_specs=[_rows(tm, C), _rows(tm, C), _vec(C)],
        out_shape=[jax.ShapeDtypeStruct((T, C), BF16), jax.ShapeDtypeStruct((T, C), BF16),
                   jax.ShapeDtypeStruct((1, C), F32)],
        compiler_params=_cp(("arbitrary",)))(dh, zg, pe, g)


def loss_head(h, tgt, name):
    T, C = h.shape
    tm = min(512, T)

    def body(h_ref, t_ref, d_ref, l_ref):
        i = pl.program_id(0)
        e = h_ref[...] - t_ref[...]
        d_ref[...] = e * (1.0 / C)
        _acc_rows(l_ref, i, jnp.sum(e * e, axis=0, keepdims=True))

    return pl.pallas_call(
        body, name=name, grid=(T // tm,), in_specs=[_rows(tm, C), _rows(tm, C)],
        out_specs=[_rows(tm, C), _vec(C)],
        out_shape=[jax.ShapeDtypeStruct((T, C), F32), jax.ShapeDtypeStruct((1, C), F32)],
        compiler_params=_cp(("arbitrary",)))(h, tgt)


AH, AG_N, AGW, CHUNK = 3072, 12, 256, 128


def _tril_bf16(w):
    r = lax.broadcasted_iota(jnp.int32, (CHUNK, CHUNK), 0)
    c = lax.broadcasted_iota(jnp.int32, (CHUNK, CHUNK), 1)
    return jnp.where(r >= c, w, 0.0).astype(BF16)


def _ln_stats(vs_ref, width):
    v = vs_ref[...]
    mu = jnp.sum(v, axis=-1, keepdims=True) * (1.0 / width)
    vc = v - mu
    var = jnp.sum(vc * vc, axis=-1, keepdims=True) * (1.0 / width)
    return mu, lax.rsqrt(var + EPS)


def gmlp_mid_fwd(zpre, vg, vb, ws, bsf, name):
    T = zpre.shape[0]

    def body(z_ref, vg_ref, vb_ref, ws_ref, bs_ref, y_ref, vs_ref):
        for g in range(AG_N):
            vs_ref[:, g * AGW:(g + 1) * AGW] = _gelu(z_ref[:, AH + g * AGW:AH + (g + 1) * AGW].astype(F32))
        mu, rstd = _ln_stats(vs_ref, AH)
        for g in range(AG_N):
            sl = slice(g * AGW, (g + 1) * AGW)
            vn = ((vs_ref[:, sl] - mu) * rstd * vg_ref[:, sl] + vb_ref[:, sl]).astype(BF16)
            sv = _dot_nn(_tril_bf16(ws_ref[g]), vn) + bs_ref[g]
            u = _gelu(z_ref[:, sl].astype(F32))
            y_ref[:, sl] = (u * sv).astype(y_ref.dtype)

    return pl.pallas_call(
        body, name=name, grid=(T // CHUNK,),
        in_specs=[_rows(CHUNK, 2 * AH), _vec(AH), _vec(AH),
                  pl.BlockSpec((AG_N, CHUNK, CHUNK), lambda i: (0, 0, 0)),
                  pl.BlockSpec((AG_N, CHUNK, AGW), lambda i: (0, 0, 0))],
        out_specs=_rows(CHUNK, AH), out_shape=jax.ShapeDtypeStruct((T, AH), BF16),
        scratch_shapes=[pltpu.VMEM((CHUNK, AH), F32)],
        compiler_params=_cp(("parallel",)))(zpre, vg, vb, ws, bsf)


def gmlp_mid_bwd(zpre, dy, vg, vb, ws, bsf, name):
    T = zpre.shape[0]

    def body(z_ref, dy_ref, vg_ref, vb_ref, ws_ref, bs_ref, dz_ref, dws_ref, dbs_ref, dvg_ref, dvb_ref,
             vs_ref, dvn_ref):
        i = pl.program_id(0)

        @pl.when(i == 0)
        def _():
            dws_ref[...] = jnp.zeros_like(dws_ref)
            dbs_ref[...] = jnp.zeros_like(dbs_ref)
            dvg_ref[...] = jnp.zeros_like(dvg_ref)
            dvb_ref[...] = jnp.zeros_like(dvb_ref)

        for g in range(AG_N):
            vs_ref[:, g * AGW:(g + 1) * AGW] = _gelu(z_ref[:, AH + g * AGW:AH + (g + 1) * AGW].astype(F32))
        mu, rstd = _ln_stats(vs_ref, AH)
        r_i = lax.broadcasted_iota(jnp.int32, (CHUNK, CHUNK), 0)
        c_i = lax.broadcasted_iota(jnp.int32, (CHUNK, CHUNK), 1)
        ones8 = jnp.ones((8, AGW), F32)
        m1 = jnp.zeros((CHUNK, 1), F32)
        m2 = jnp.zeros((CHUNK, 1), F32)
        for g in range(AG_N):
            sl = slice(g * AGW, (g + 1) * AGW)
            vh = (vs_ref[:, sl] - mu) * rstd
            vn = (vh * vg_ref[:, sl] + vb_ref[:, sl]).astype(BF16)
            wm = _tril_bf16(ws_ref[g])
            sv = _dot_nn(wm, vn) + bs_ref[g]
            zu = z_ref[:, sl].astype(F32)
            u = _gelu(zu)
            dyg = dy_ref[:, sl].astype(F32)
            dz_ref[:, sl] = (dyg * sv * _gelu_grad(zu)).astype(dz_ref.dtype)
            dsv = dyg * u
            dsv_b = dsv.astype(BF16)
            dws_ref[g] += jnp.where(r_i >= c_i, _dot_nt(dsv_b, vn), 0.0)
            dbs_ref[g] += _dot_nt(ones8, dsv)
            dvn = _dot_tn(wm, dsv_b)
            dvn_ref[:, sl] = dvn
            dvh = dvn * vg_ref[:, sl]
            m1 = m1 + jnp.sum(dvh, axis=-1, keepdims=True)
            m2 = m2 + jnp.sum(dvh * vh, axis=-1, keepdims=True)
            dvg_ref[:, sl] += jnp.sum(dvn * vh, axis=0, keepdims=True)
            dvb_ref[:, sl] += jnp.sum(dvn, axis=0, keepdims=True)
        m1 = m1 * (1.0 / AH)
        m2 = m2 * (1.0 / AH)
        for g in range(AG_N):
            sl = slice(g * AGW, (g + 1) * AGW)
            vh = (vs_ref[:, sl] - mu) * rstd
            dv = rstd * (dvn_ref[:, sl] * vg_ref[:, sl] - m1 - vh * m2)
            zv = z_ref[:, AH + g * AGW:AH + (g + 1) * AGW].astype(F32)
            dz_ref[:, AH + g * AGW:AH + (g + 1) * AGW] = (dv * _gelu_grad(zv)).astype(dz_ref.dtype)

    full3 = lambda a, b, c: pl.BlockSpec((a, b, c), lambda i: (0, 0, 0))
    return pl.pallas_call(
        body, name=name, grid=(T // CHUNK,),
        in_specs=[_rows(CHUNK, 2 * AH), _rows(CHUNK, AH), _vec(AH), _vec(AH),
                  full3(AG_N, CHUNK, CHUNK), full3(AG_N, CHUNK, AGW)],
        out_specs=[_rows(CHUNK, 2 * AH), full3(AG_N, CHUNK, CHUNK), full3(AG_N, 8, CHUNK), _vec(AH), _vec(AH)],
        out_shape=[jax.ShapeDtypeStruct((T, 2 * AH), BF16), jax.ShapeDtypeStruct((AG_N, CHUNK, CHUNK), F32),
                   jax.ShapeDtypeStruct((AG_N, 8, CHUNK), F32), jax.ShapeDtypeStruct((1, AH), F32),
                   jax.ShapeDtypeStruct((1, AH), F32)],
        scratch_shapes=[pltpu.VMEM((CHUNK, AH), F32), pltpu.VMEM((CHUNK, AH), F32)],
        compiler_params=_cp(("arbitrary",)))(zpre, dy, vg, vb, ws, bsf)


SLAB = 256
NSLAB = DM // SLAB
RC = 256
PAD = 32


def _col(T, j):
    return pl.BlockSpec((T, SLAB), lambda c: (0, j * NSLAB + c))


def _chunks(T, fn):
    def step(i, carry):
        fn(pl.multiple_of(i * RC, RC))
        return carry
    lax.fori_loop(0, T // RC, step, 0)


def _conv_taps(K):
    return [(r, [q for q in range(4) if 8 * q + r < K]) for r in range(min(8, K))]


def _causal_conv(zpad_ref, wrow, K, r0):
    acc = None
    for r, qs in _conv_taps(K):
        a = None
        for q in qs:
            term = wrow(8 * q + r) * zpad_ref[pl.ds(r0 + (PAD - 8 - 8 * q), RC + 8), :]
            a = term if a is None else a + term
        a = a if r == 0 else pltpu.roll(a, r, 0)
        acc = a if acc is None else acc + a
    return acc[8:, :]


def _anticausal_conv(gpad_ref, wrow, K, r0):
    acc = None
    for r, qs in _conv_taps(K):
        b = None
        for q in qs:
            term = wrow(8 * q + r) * gpad_ref[pl.ds(r0 + 8 * q, RC + 8), :]
            b = term if b is None else b + term
        b = b if r == 0 else pltpu.roll(b, RC + 8 - r, 0)
        acc = b if acc is None else acc + b
    return acc[:RC, :]


def _conv_dw(gpad_ref, zpad_ref, dw_ref, K, r0):
    for r, qs in _conv_taps(K):
        gw = gpad_ref[pl.ds(r0, RC + 8), :]
        p = (gw if r == 0 else pltpu.roll(gw, RC + 8 - r, 0))[:RC, :]
        for q in qs:
            z = zpad_ref[pl.ds(r0 + (PAD - 8 * q), RC), :]
            dw_ref[8 * q + r] += jnp.sum((p * z).reshape(RC // 8, 8, SLAB), axis=0)


def _zero_rows(ref, start, n):
    ref[pl.ds(start, n), :] = jnp.zeros((n, SLAB), F32)


def pool_fwd(hn, wg, sc, name):
    T = hn.shape[0]

    def body(h_ref, w_ref, s_ref, p_ref, yp_ref, y_ref, xpad):
        g = pl.program_id(0)
        wf = jnp.left_shift(2, g).astype(F32)
        _zero_rows(xpad, 0, PAD)

        def fill(r0):
            xpad[pl.ds(r0 + PAD, RC), :] = h_ref[pl.ds(r0, RC), :]
        _chunks(T, fill)

        def step(r0):
            w = xpad[pl.ds(r0 + (PAD - 16), RC + 16), :]
            s2 = w + pltpu.roll(w, 1, 0)
            s4 = s2 + pltpu.roll(s2, 2, 0)
            s8 = s4 + pltpu.roll(s4, 4, 0)
            s16 = s8 + pltpu.roll(s8, 8, 0)
            sel = jnp.where(g == 0, s2, jnp.where(g == 1, s4, jnp.where(g == 2, s8, s16)))[16:, :]
            t1 = (r0 + 1 + lax.broadcasted_iota(jnp.int32, (RC, SLAB), 0)).astype(F32)
            pooled = (sel / jnp.minimum(t1, wf) - w[16:, :]).astype(BF16)
            p_ref[pl.ds(r0, RC), :] = pooled
            yp = _dot_nn(pooled, w_ref[...])
            yp_ref[pl.ds(r0, RC), :] = yp
            y_ref[pl.ds(r0, RC), :] = yp * s_ref[...]
        _chunks(T, step)

    slab = pl.BlockSpec((T, SLAB), lambda c: (0, c))
    return pl.pallas_call(
        body, name=name, grid=(NSLAB,),
        in_specs=[slab, pl.BlockSpec((None, SLAB, SLAB), lambda c: (c, 0, 0)), pl.BlockSpec((1, SLAB), lambda c: (0, c))],
        out_specs=[slab, slab, slab],
        out_shape=[jax.ShapeDtypeStruct((T, DM), BF16), jax.ShapeDtypeStruct((T, DM), F32),
                   jax.ShapeDtypeStruct((T, DM), F32)],
        scratch_shapes=[pltpu.VMEM((T + PAD, SLAB), F32)],
        compiler_params=_cp(("parallel",)))(hn, wg, sc)


def pool_bwd(dy, ypre, pooled, wg, sc, name):
    T = dy.shape[0]

    def body(d_ref, yp_ref, p_ref, w_ref, s_ref, dh_ref, dw_ref, ds_ref, qpad, dwacc, dsacc):
        g = pl.program_id(0)
        wf = jnp.left_shift(2, g).astype(F32)
        dwacc[...] = jnp.zeros_like(dwacc)
        dsacc[...] = jnp.zeros_like(dsacc)
        _zero_rows(qpad, T, PAD)

        def first(r0):
            d = d_ref[pl.ds(r0, RC), :]
            dsacc[...] += jnp.sum((d * yp_ref[pl.ds(r0, RC), :]).reshape(RC // 8, 8, SLAB), axis=0)
            dyp = (d * s_ref[...]).astype(BF16)
            dpool = _dot_nt(dyp, w_ref[...])
            dwacc[...] += _dot_tn(p_ref[pl.ds(r0, RC), :], dyp)
            t1 = (r0 + 1 + lax.broadcasted_iota(jnp.int32, (RC, SLAB), 0)).astype(F32)
            qpad[pl.ds(r0, RC), :] = dpool / jnp.minimum(t1, wf)
            dh_ref[pl.ds(r0, RC), :] = dpool
        _chunks(T, first)

        def second(r0):
            w = qpad[pl.ds(r0, RC + 16), :]
            n = RC + 16
            a2 = w + pltpu.roll(w, n - 1, 0)
            a4 = a2 + pltpu.roll(a2, n - 2, 0)
            a8 = a4 + pltpu.roll(a4, n - 4, 0)
            a16 = a8 + pltpu.roll(a8, n - 8, 0)
            sel = jnp.where(g == 0, a2, jnp.where(g == 1, a4, jnp.where(g == 2, a8, a16)))[:RC, :]
            dh_ref[pl.ds(r0, RC), :] = sel - dh_ref[pl.ds(r0, RC), :]
        _chunks(T, second)
        dw_ref[...] = dwacc[...]
        ds_ref[...] = jnp.sum(dsacc[...], axis=0, keepdims=True)

    slab = pl.BlockSpec((T, SLAB), lambda c: (0, c))
    wspec = pl.BlockSpec((None, SLAB, SLAB), lambda c: (c, 0, 0))
    vec = pl.BlockSpec((1, SLAB), lambda c: (0, c))
    return pl.pallas_call(
        body, name=name, grid=(NSLAB,),
        in_specs=[slab, slab, slab, wspec, vec],
        out_specs=[slab, wspec, vec],
        out_shape=[jax.ShapeDtypeStruct((T, DM), F32), jax.ShapeDtypeStruct((NSLAB, SLAB, SLAB), F32),
                   jax.ShapeDtypeStruct((1, DM), F32)],
        scratch_shapes=[pltpu.VMEM((T + PAD, SLAB), F32), pltpu.VMEM((SLAB, SLAB), F32), pltpu.VMEM((8, SLAB), F32)],
        compiler_params=_cp(("parallel",)))(dy, ypre, pooled, wg, sc)


KC = 31
KD = 3


def conf_conv_fwd(ag, wdw, bdw, name):
    T = ag.shape[0]

    def body(a_ref, g_ref, w_ref, b_ref, o_ref, zpad):
        _zero_rows(zpad, 0, PAD)

        def fill(r0):
            a = a_ref[pl.ds(r0, RC), :].astype(F32)
            gt = g_ref[pl.ds(r0, RC), :].astype(F32)
            zpad[pl.ds(r0 + PAD, RC), :] = a * _sig(gt)
        _chunks(T, fill)
        wrow = lambda j: w_ref[KC - 1 - j:KC - j, :]

        def step(r0):
            o_ref[pl.ds(r0, RC), :] = _causal_conv(zpad, wrow, KC, r0) + b_ref[...]
        _chunks(T, step)

    vec = pl.BlockSpec((1, SLAB), lambda c: (0, c))
    return pl.pallas_call(
        body, name=name, grid=(NSLAB,),
        in_specs=[_col(T, 0), _col(T, 1), pl.BlockSpec((32, SLAB), lambda c: (0, c)), vec],
        out_specs=pl.BlockSpec((T, SLAB), lambda c: (0, c)),
        out_shape=jax.ShapeDtypeStruct((T, DM), F32),
        scratch_shapes=[pltpu.VMEM((T + PAD, SLAB), F32)],
        compiler_params=_cp(("parallel",)))(ag, ag, wdw, bdw)


def conf_conv_bwd(dzc, ag, wdw, name):
    T = ag.shape[0]

    def body(d_ref, a_ref, g_ref, w_ref, da_ref, dg_ref, dw_ref, db_ref, zpad, gpad, dwacc, dbacc):
        _zero_rows(zpad, 0, PAD)
        _zero_rows(gpad, T, PAD)
        dwacc[...] = jnp.zeros_like(dwacc)
        dbacc[...] = jnp.zeros_like(dbacc)

        def fill(r0):
            a = a_ref[pl.ds(r0, RC), :].astype(F32)
            gt = g_ref[pl.ds(r0, RC), :].astype(F32)
            zpad[pl.ds(r0 + PAD, RC), :] = a * _sig(gt)
            d = d_ref[pl.ds(r0, RC), :]
            gpad[pl.ds(r0, RC), :] = d
            dbacc[...] += jnp.sum(d.reshape(RC // 8, 8, SLAB), axis=0)
        _chunks(T, fill)
        wrow = lambda j: w_ref[KC - 1 - j:KC - j, :]

        def step(r0):
            dz = _anticausal_conv(gpad, wrow, KC, r0)
            a = a_ref[pl.ds(r0, RC), :].astype(F32)
            s = _sig(g_ref[pl.ds(r0, RC), :].astype(F32))
            da_ref[pl.ds(r0, RC), :] = (dz * s).astype(da_ref.dtype)
            dg_ref[pl.ds(r0, RC), :] = (dz * a * s * (1.0 - s)).astype(dg_ref.dtype)
            _conv_dw(gpad, zpad, dwacc, KC, r0)
        _chunks(T, step)
        dw_ref[...] = jnp.zeros_like(dw_ref)
        for k in range(KC):
            dw_ref[k:k + 1, :] = jnp.sum(dwacc[KC - 1 - k], axis=0, keepdims=True)
        db_ref[...] = jnp.sum(dbacc[...], axis=0, keepdims=True)

    vec = pl.BlockSpec((1, SLAB), lambda c: (0, c))
    w32 = pl.BlockSpec((32, SLAB), lambda c: (0, c))
    return pl.pallas_call(
        body, name=name, grid=(NSLAB,),
        in_specs=[pl.BlockSpec((T, SLAB), lambda c: (0, c)), _col(T, 0), _col(T, 1), w32],
        out_specs=[_col(T, 0), _col(T, 0), w32, vec],
        out_shape=[jax.ShapeDtypeStruct((T, DM), BF16), jax.ShapeDtypeStruct((T, DM), BF16),
                   jax.ShapeDtypeStruct((32, DM), F32), jax.ShapeDtypeStruct((1, DM), F32)],
        scratch_shapes=[pltpu.VMEM((T + PAD, SLAB), F32), pltpu.VMEM((T + PAD, SLAB), F32),
                        pltpu.VMEM((32, 8, SLAB), F32), pltpu.VMEM((8, SLAB), F32)],
        compiler_params=_cp(("parallel",)))(dzc, ag, ag, wdw)


def conf_ln_fwd(zc, g, b, name):
    T, C = zc.shape
    tm = min(512, T)

    def body(z_ref, g_ref, b_ref, o_ref):
        x = z_ref[...]
        xc = x - jnp.mean(x, axis=-1, keepdims=True)
        r = lax.rsqrt(jnp.mean(xc * xc, axis=-1, keepdims=True) + EPS)
        zl = xc * r * g_ref[...] + b_ref[...]
        o_ref[...] = (zl * _sig(zl)).astype(o_ref.dtype)

    return pl.pallas_call(
        body, name=name, grid=(T // tm,), in_specs=[_rows(tm, C), _vec(C), _vec(C)], out_specs=_rows(tm, C),
        out_shape=jax.ShapeDtypeStruct((T, C), BF16), compiler_params=_cp(("parallel",)))(zc, g, b)


def conf_ln_bwd(dzs, zc, g, b, name):
    T, C = zc.shape
    tm = min(512, T)

    def body(d_ref, z_ref, g_ref, b_ref, o_ref, dg_ref, db_ref):
        i = pl.program_id(0)
        x = z_ref[...]
        xc = x - jnp.mean(x, axis=-1, keepdims=True)
        r = lax.rsqrt(jnp.mean(xc * xc, axis=-1, keepdims=True) + EPS)
        xh = xc * r
        zl = xh * g_ref[...] + b_ref[...]
        s = _sig(zl)
        dzl = d_ref[...].astype(F32) * (s * (1.0 + zl * (1.0 - s)))
        t = dzl * g_ref[...]
        o_ref[...] = r * (t - jnp.mean(t, axis=-1, keepdims=True) - xh * jnp.mean(t * xh, axis=-1, keepdims=True))
        _acc_rows(dg_ref, i, jnp.sum(dzl * xh, axis=0, keepdims=True))
        _acc_rows(db_ref, i, jnp.sum(dzl, axis=0, keepdims=True))

    return pl.pallas_call(
        body, name=name, grid=(T // tm,), in_specs=[_rows(tm, C), _rows(tm, C), _vec(C), _vec(C)],
        out_specs=[_rows(tm, C), _vec(C), _vec(C)],
        out_shape=[jax.ShapeDtypeStruct((T, C), F32), jax.ShapeDtypeStruct((1, C), F32),
                   jax.ShapeDtypeStruct((1, C), F32)],
        compiler_params=_cp(("arbitrary",)))(dzs, zc, g, b)


def sconv_fwd(bgx, wc, name):
    T = bgx.shape[0]

    def body(b_ref, c_ref, x_ref, w_ref, o_ref, zpad):
        _zero_rows(zpad, 0, PAD)

        def fill(r0):
            zpad[pl.ds(r0 + PAD, RC), :] = c_ref[pl.ds(r0, RC), :].astype(F32) * x_ref[pl.ds(r0, RC), :].astype(F32)
        _chunks(T, fill)
        wrow = lambda j: w_ref[KD - 1 - j:KD - j, :]

        def step(r0):
            qc = _causal_conv(zpad, wrow, KD, r0)
            o_ref[pl.ds(r0, RC), :] = (b_ref[pl.ds(r0, RC), :].astype(F32) * qc).astype(o_ref.dtype)
        _chunks(T, step)

    return pl.pallas_call(
        body, name=name, grid=(NSLAB,),
        in_specs=[_col(T, 0), _col(T, 1), _col(T, 2), pl.BlockSpec((8, SLAB), lambda c: (0, c))],
        out_specs=pl.BlockSpec((T, SLAB), lambda c: (0, c)),
        out_shape=jax.ShapeDtypeStruct((T, DM), BF16),
        scratch_shapes=[pltpu.VMEM((T + PAD, SLAB), F32)],
        compiler_params=_cp(("parallel",)))(bgx, bgx, bgx, wc)


def sconv_bwd(dy, bgx, wc, name):
    T = bgx.shape[0]

    def body(d_ref, b_ref, c_ref, x_ref, w_ref, db_ref, dc_ref, dx_ref, dw_ref, zpad, gpad, dwacc):
        _zero_rows(zpad, 0, PAD)
        _zero_rows(gpad, T, PAD)
        dwacc[...] = jnp.zeros_like(dwacc)

        def fill(r0):
            zpad[pl.ds(r0 + PAD, RC), :] = c_ref[pl.ds(r0, RC), :].astype(F32) * x_ref[pl.ds(r0, RC), :].astype(F32)
            gpad[pl.ds(r0, RC), :] = d_ref[pl.ds(r0, RC), :].astype(F32) * b_ref[pl.ds(r0, RC), :].astype(F32)
        _chunks(T, fill)
        wrow = lambda j: w_ref[KD - 1 - j:KD - j, :]

        def step(r0):
            qc = _causal_conv(zpad, wrow, KD, r0)
            db_ref[pl.ds(r0, RC), :] = (d_ref[pl.ds(r0, RC), :].astype(F32) * qc).astype(db_ref.dtype)
            dq = _anticausal_conv(gpad, wrow, KD, r0)
            dc_ref[pl.ds(r0, RC), :] = (dq * x_ref[pl.ds(r0, RC), :].astype(F32)).astype(dc_ref.dtype)
            dx_ref[pl.ds(r0, RC), :] = (dq * c_ref[pl.ds(r0, RC), :].astype(F32)).astype(dx_ref.dtype)
            _conv_dw(gpad, zpad, dwacc, KD, r0)
        _chunks(T, step)
        dw_ref[...] = jnp.zeros_like(dw_ref)
        for k in range(KD):
            dw_ref[k:k + 1, :] = jnp.sum(dwacc[KD - 1 - k], axis=0, keepdims=True)

    w8 = pl.BlockSpec((8, SLAB), lambda c: (0, c))
    return pl.pallas_call(
        body, name=name, grid=(NSLAB,),
        in_specs=[pl.BlockSpec((T, SLAB), lambda c: (0, c)), _col(T, 0), _col(T, 1), _col(T, 2), w8],
        out_specs=[_col(T, 0), _col(T, 0), _col(T, 0), w8],
        out_shape=[jax.ShapeDtypeStruct((T, DM), BF16)] * 3 + [jax.ShapeDtypeStruct((8, DM), F32)],
        scratch_shapes=[pltpu.VMEM((T + PAD, SLAB), F32), pltpu.VMEM((T + PAD, SLAB), F32),
                        pltpu.VMEM((8, 8, SLAB), F32)],
        compiler_params=_cp(("parallel",)))(dy, bgx, bgx, bgx, wc)


def merge_cols(parts, name):
    T = parts[0].shape[0]
    n = len(parts)
    C = n * DM
    tm = min(512, T)

    def body(*refs):
        o_ref = refs[n]
        for j in range(n):
            o_ref[:, j * DM:(j + 1) * DM] = refs[j][...]

    return pl.pallas_call(
        body, name=name, grid=(T // tm,),
        in_specs=[_rows(tm, DM) for j in range(n)],
        out_specs=_rows(tm, C), out_shape=jax.ShapeDtypeStruct((T, C), parts[0].dtype),
        compiler_params=_cp(("parallel",)))(*parts)


def adamw(w, g, m, v, name):
    shape = w.shape
    C = shape[-1]
    R = w.size // C
    w2, g2, m2, v2 = (a.reshape(R, C) for a in (w, g, m, v))
    tr = R
    while tr * C > 512 * 1024 and tr % 16 == 0:
        tr //= 2
    bc1 = 1.0 - ADAM_B1 ** ADAM_STEP
    bc2 = 1.0 - ADAM_B2 ** ADAM_STEP

    def body(w_ref, g_ref, m_ref, v_ref, d_ref, nm_ref, nv_ref):
        gg = g_ref[...]
        nm = ADAM_B1 * m_ref[...] + (1.0 - ADAM_B1) * gg
        nv = ADAM_B2 * v_ref[...] + (1.0 - ADAM_B2) * (gg * gg)
        nm_ref[...] = nm
        nv_ref[...] = nv
        d_ref[...] = -ADAM_LR * ((nm / bc1) / (jnp.sqrt(nv / bc2) + ADAM_EPS) + ADAM_WD * w_ref[...])

    spec = pl.BlockSpec((tr, C), lambda i: (i, 0))
    outs = pl.pallas_call(
        body, name=name, grid=(R // tr,), in_specs=[spec] * 4, out_specs=[spec] * 3,
        out_shape=[jax.ShapeDtypeStruct((R, C), F32)] * 3, compiler_params=_cp(("parallel",)))(w2, g2, m2, v2)
    return tuple(o.reshape(shape) for o in outs)


def _place():
    x, y, c = lax.axis_index("x"), lax.axis_index("y"), lax.axis_index("c")
    return x, y, c


def all_gather_chips(bufs, name):
    n = len(bufs)
    me_ = 2 * lax.axis_index("x") + lax.axis_index("y")
    slots = [lax.dynamic_update_slice(lax.empty((NCHIP,) + b.shape, b.dtype), b[None], (me_, 0, 0)) for b in bufs]

    def body(*refs):
        dst = refs[n:2 * n]
        send, recv = refs[2 * n:]
        x, y, c = _place()
        me = 2 * x + y
        sib = (x, y, 1 - c)
        chips = [(1 - x, y), (x, 1 - y), (1 - x, 1 - y)]

        def half(b, slot, hc):
            rows = bufs[b].shape[0] // 2
            return dst[b].at[slot, pl.ds(hc * rows, rows), :]

        def remote(k, s, d, to):
            return pltpu.make_async_remote_copy(src_ref=s, dst_ref=d, send_sem=send.at[k], recv_sem=recv.at[k],
                                                device_id=to, device_id_type=MESH)

        first = []
        for b in range(n):
            for j, (cx, cy) in enumerate(chips):
                first.append(remote(b * 6 + j, half(b, me, c), half(b, me, c), (cx, cy, c)))
        for cp in first:
            cp.start()
        passed = []
        for b in range(n):
            for j, (cx, cy) in enumerate(chips):
                slot = 2 * cx + cy
                remote(b * 6 + j, half(b, slot, c), half(b, slot, c), (cx, cy, c)).wait_recv()
                fwd = remote(b * 6 + 3 + j, half(b, slot, c), half(b, slot, c), sib)
                fwd.start()
                passed.append(fwd)
        for b in range(n):
            for j, (cx, cy) in enumerate(chips):
                slot = 2 * cx + cy
                remote(b * 6 + 3 + j, half(b, slot, 1 - c), half(b, slot, 1 - c), sib).wait_recv()
        for cp in first + passed:
            cp.wait_send()

    return pl.pallas_call(
        body, name=name, in_specs=[ANY] * n, out_specs=[ANY] * n,
        out_shape=[jax.ShapeDtypeStruct(s.shape, s.dtype) for s in slots],
        input_output_aliases={b: b for b in range(n)},
        scratch_shapes=[pltpu.SemaphoreType.DMA((6 * n,)), pltpu.SemaphoreType.DMA((6 * n,))],
        compiler_params=pltpu.CompilerParams())(*slots)


def pair_exchange(bufs, name):
    n = len(bufs)

    def body(*refs):
        src, dst = refs[:n], refs[n:2 * n]
        send, recv = refs[2 * n:]
        x, y, c = _place()
        cps = []
        for b in range(n):
            rows = bufs[b].shape[1] // 2
            cp = pltpu.make_async_remote_copy(
                src_ref=src[b].at[:, pl.ds((1 - c) * rows, rows), :], dst_ref=dst[b],
                send_sem=send.at[b], recv_sem=recv.at[b], device_id=(x, y, 1 - c), device_id_type=MESH)
            cp.start()
            cps.append(cp)
        for cp in cps:
            cp.wait()

    return pl.pallas_call(
        body, name=name, in_specs=[ANY] * n, out_specs=[ANY] * n,
        out_shape=[jax.ShapeDtypeStruct((NCHIP, b.shape[1] // 2, b.shape[2]), b.dtype) for b in bufs],
        scratch_shapes=[pltpu.SemaphoreType.DMA((n,)), pltpu.SemaphoreType.DMA((n,))],
        compiler_params=pltpu.CompilerParams())(*bufs)


def add_half(full, got, tr, tc, name):
    _, R, C = full.shape
    rows = R // 2
    nr = rows // tr
    c_arr = lax.axis_index("c").astype(jnp.int32).reshape(1)

    def body(c_ref, a_ref, b_ref, o_ref):
        o_ref[...] = (a_ref[...].astype(F32) + b_ref[...].astype(F32)).astype(o_ref.dtype)

    return pl.pallas_call(
        body, name=name,
        grid_spec=pltpu.PrefetchScalarGridSpec(
            num_scalar_prefetch=1, grid=(NCHIP, nr, C // tc),
            in_specs=[pl.BlockSpec((None, tr, tc), lambda s, i, j, c_ref: (s, c_ref[0] * nr + i, j)),
                      pl.BlockSpec((None, tr, tc), lambda s, i, j, c_ref: (s, i, j))],
            out_specs=pl.BlockSpec((None, tr, tc), lambda s, i, j, c_ref: (s, i, j))),
        out_shape=jax.ShapeDtypeStruct((NCHIP, rows, C), full.dtype),
        compiler_params=_cp(("parallel", "parallel", "parallel")))(c_arr, full, got)


def chip_exchange(bufs, name):
    n = len(bufs)

    def body(*refs):
        src, dst = refs[:n], refs[n:2 * n]
        send, recv, lsem = refs[2 * n:]
        x, y, c = _place()
        me = 2 * x + y
        chips = [(1 - x, y), (x, 1 - y), (1 - x, 1 - y)]
        local = [pltpu.make_async_copy(src[b].at[me], dst[b].at[me], lsem.at[b]) for b in range(n)]
        for cp in local:
            cp.start()
        cps = []
        for b in range(n):
            for j, (cx, cy) in enumerate(chips):
                cp = pltpu.make_async_remote_copy(
                    src_ref=src[b].at[2 * cx + cy], dst_ref=dst[b].at[me],
                    send_sem=send.at[b * 3 + j], recv_sem=recv.at[b * 3 + j],
                    device_id=(cx, cy, c), device_id_type=MESH)
                cp.start()
                cps.append((cp, b, cx, cy, j))
        for cp, b, cx, cy, j in cps:
            cp.wait_send()
            pltpu.make_async_remote_copy(
                src_ref=src[b].at[me], dst_ref=dst[b].at[2 * cx + cy],
                send_sem=send.at[b * 3 + j], recv_sem=recv.at[b * 3 + j],
                device_id=(cx, cy, c), device_id_type=MESH).wait_recv()
        for cp in local:
            cp.wait()

    return pl.pallas_call(
        body, name=name, in_specs=[ANY] * n, out_specs=[ANY] * n,
        out_shape=[jax.ShapeDtypeStruct(b.shape, b.dtype) for b in bufs],
        scratch_shapes=[pltpu.SemaphoreType.DMA((3 * n,)), pltpu.SemaphoreType.DMA((3 * n,)),
                        pltpu.SemaphoreType.DMA((n,))],
        compiler_params=pltpu.CompilerParams())(*bufs)


def sum_slots(buf, tr, tc, name):
    _, r, C = buf.shape
    nr = r // tr
    c_arr = lax.axis_index("c").astype(jnp.int32).reshape(1)

    def body(c_ref, a_ref, o_ref):
        o_ref[...] = ((a_ref[0].astype(F32) + a_ref[1].astype(F32)) + a_ref[2].astype(F32)) + a_ref[3].astype(F32)

    return pl.pallas_call(
        body, name=name,
        grid_spec=pltpu.PrefetchScalarGridSpec(
            num_scalar_prefetch=1, grid=(nr, C // tc),
            in_specs=[pl.BlockSpec((NCHIP, tr, tc), lambda i, j, c_ref: (0, i, j))],
            out_specs=pl.BlockSpec((tr, tc), lambda i, j, c_ref: (c_ref[0] * nr + i, j))),
        out_shape=jax.ShapeDtypeStruct((2 * r, C), F32),
        compiler_params=_cp(("parallel", "parallel")))(c_arr, buf)


def pair_share(bufs, name):
    n = len(bufs)

    def body(*refs):
        dst = refs[n:2 * n]
        send, recv = refs[2 * n:]
        x, y, c = _place()
        cps = []
        for b in range(n):
            rows = bufs[b].shape[0] // 2
            here = dst[b].at[pl.ds(c * rows, rows), :]
            cp = pltpu.make_async_remote_copy(src_ref=here, dst_ref=here, send_sem=send.at[b], recv_sem=recv.at[b],
                                              device_id=(x, y, 1 - c), device_id_type=MESH)
            cp.start()
            cps.append((cp, b))
        for cp, b in cps:
            rows = bufs[b].shape[0] // 2
            there = dst[b].at[pl.ds((1 - c) * rows, rows), :]
            cp.wait_send()
            pltpu.make_async_remote_copy(src_ref=there, dst_ref=there, send_sem=send.at[b], recv_sem=recv.at[b],
                                         device_id=(x, y, 1 - c), device_id_type=MESH).wait_recv()

    return pl.pallas_call(
        body, name=name, in_specs=[ANY] * n, out_specs=[ANY] * n,
        out_shape=[jax.ShapeDtypeStruct(b.shape, b.dtype) for b in bufs],
        input_output_aliases={b: b for b in range(n)},
        scratch_shapes=[pltpu.SemaphoreType.DMA((n,)), pltpu.SemaphoreType.DMA((n,))],
        compiler_params=pltpu.CompilerParams())(*bufs)


def _tile(kind, buf):
    return {"A": (256, buf.shape[2]), "B": (buf.shape[1], 1024), "C": (128, 256), "V": (40, 256),
            "E": (40, 1024)}[kind]


def reduce_scatter(parts, tag):
    names = list(parts)
    got = pair_exchange([parts[k] for k in names], tag + "_pair_exchange")
    sums = [add_half(parts[k], got[i], *_tile(k[0], got[i]), name=tag + "_add_pair_" + k) for i, k in enumerate(names)]
    landed = chip_exchange(sums, tag + "_chip_exchange")
    halves = [sum_slots(landed[i], *_tile(k[0], landed[i]), name=tag + "_sum_chips_" + k) for i, k in enumerate(names)]
    full = pair_share(halves, tag + "_pair_share")
    return dict(zip(names, full))


HBM = pl.BlockSpec(memory_space=pltpu.HBM)
SEMS = pl.BlockSpec(memory_space=pltpu.SEMAPHORE)
FLOWS = pltpu.SideEffectType.DATAFLOW_SIDE_EFFECTING


def _in_hbm(a):
    return pltpu.with_memory_space_constraint(a, pltpu.HBM)


def _other_chips():
    x, y, c = _place()
    return 2 * x + y, c, [(1 - x, y), (x, 1 - y), (1 - x, 1 - y)]


def own_slots(bufs):
    me = 2 * lax.axis_index("x") + lax.axis_index("y")
    return [lax.dynamic_update_slice(lax.empty((NCHIP,) + b.shape, b.dtype), b[None], (me, 0, 0)) for b in bufs]


def gather_start(slots, name):
    n = len(slots)

    def body(*refs):
        ins = refs[:n]
        send, recv = refs[n], refs[n + 1]
        token = refs[2 * n + 2]
        me, c, chips = _other_chips()
        for b in range(n):
            rows = slots[b].shape[1] // 2
            own = ins[b].at[me, pl.ds(c * rows, rows), :]
            for j, (cx, cy) in enumerate(chips):
                pltpu.make_async_remote_copy(src_ref=own, dst_ref=own, send_sem=send.at[3 * b + j],
                                             recv_sem=recv.at[3 * b + j], device_id=(cx, cy, c),
                                             device_id_type=MESH).start()
        token[...] = jnp.zeros_like(token)

    out = pl.pallas_call(
        body, name=name, in_specs=[HBM] * n,
        out_specs=[SEMS, SEMS] + [HBM] * n + [pl.BlockSpec(memory_space=pltpu.VMEM)],
        out_shape=[pltpu.SemaphoreType.DMA((3 * n,)), pltpu.SemaphoreType.DMA((3 * n,))]
        + [pltpu.HBM(s.shape, s.dtype) for s in slots] + [jax.ShapeDtypeStruct((8, 128), F32)],
        input_output_aliases={b: b + 2 for b in range(n)},
        compiler_params=pltpu.CompilerParams(has_side_effects=FLOWS))(*[_in_hbm(s) for s in slots])
    return out[0], out[1], list(out[2:2 + n]), out[2 + n]


def gather_wait(send, recv, slots, picks, after, name):
    n = len(slots)

    def body(*refs):
        ins = refs[:n]
        send_, recv_ = refs[n], refs[n + 1]
        me, c, chips = _other_chips()
        for i, b in enumerate(picks):
            rows = slots[i].shape[1] // 2
            own = ins[i].at[me, pl.ds(c * rows, rows), :]
            for j, (cx, cy) in enumerate(chips):
                got = ins[i].at[2 * cx + cy, pl.ds(c * rows, rows), :]
                pltpu.make_async_remote_copy(src_ref=own, dst_ref=own, send_sem=send_.at[3 * b + j],
                                             recv_sem=recv_.at[3 * b + j], device_id=(cx, cy, c),
                                             device_id_type=MESH).wait_send()
                pltpu.make_async_remote_copy(src_ref=got, dst_ref=got, send_sem=send_.at[3 * b + j],
                                             recv_sem=recv_.at[3 * b + j], device_id=(cx, cy, c),
                                             device_id_type=MESH).wait_recv()

    return pl.pallas_call(
        body, name=name, in_specs=[HBM] * n + [SEMS, SEMS, ANY], out_specs=[HBM] * n,
        out_shape=[pltpu.HBM(s.shape, s.dtype) for s in slots],
        input_output_aliases={b: b for b in range(n)},
        compiler_params=pltpu.CompilerParams(has_side_effects=FLOWS))(*slots, send, recv, after)


def gather_pass(slots, name):
    n = len(slots)

    def body(*refs):
        dst = refs[n:2 * n]
        send, recv = refs[2 * n:]
        x, y, c = _place()
        sib = (x, y, 1 - c)
        chips = [(1 - x, y), (x, 1 - y), (1 - x, 1 - y)]

        def half(b, slot, hc):
            rows = slots[b].shape[1] // 2
            return dst[b].at[slot, pl.ds(hc * rows, rows), :]

        passed = []
        for b in range(n):
            for j, (cx, cy) in enumerate(chips):
                slot = 2 * cx + cy
                cp = pltpu.make_async_remote_copy(src_ref=half(b, slot, c), dst_ref=half(b, slot, c),
                                                  send_sem=send.at[3 * b + j], recv_sem=recv.at[3 * b + j],
                                                  device_id=sib, device_id_type=MESH)
                cp.start()
                passed.append(cp)
        for b in range(n):
            for j, (cx, cy) in enumerate(chips):
                slot = 2 * cx + cy
                pltpu.make_async_remote_copy(src_ref=half(b, slot, 1 - c), dst_ref=half(b, slot, 1 - c),
                                             send_sem=send.at[3 * b + j], recv_sem=recv.at[3 * b + j],
                                             device_id=sib, device_id_type=MESH).wait_recv()
        for cp in passed:
            cp.wait_send()

    return pl.pallas_call(
        body, name=name, in_specs=[ANY] * n, out_specs=[ANY] * n,
        out_shape=[jax.ShapeDtypeStruct(s.shape, s.dtype) for s in slots],
        input_output_aliases={b: b for b in range(n)},
        scratch_shapes=[pltpu.SemaphoreType.DMA((3 * n,)), pltpu.SemaphoreType.DMA((3 * n,))],
        compiler_params=pltpu.CompilerParams())(*slots)


def chip_exchange_start(sums, name):
    n = len(sums)
    me_ = 2 * lax.axis_index("x") + lax.axis_index("y")
    landing = [lax.dynamic_update_slice(lax.empty(s.shape, s.dtype),
                                        lax.dynamic_slice(s, (me_, 0, 0), (1,) + s.shape[1:]), (me_, 0, 0)) for s in sums]

    def body(*refs):
        src, land = refs[:n], refs[n:2 * n]
        send, recv = refs[2 * n], refs[2 * n + 1]
        token = refs[4 * n + 2]
        me, c, chips = _other_chips()
        for b in range(n):
            for j, (cx, cy) in enumerate(chips):
                pltpu.make_async_remote_copy(src_ref=src[b].at[2 * cx + cy], dst_ref=land[b].at[me],
                                             send_sem=send.at[3 * b + j], recv_sem=recv.at[3 * b + j],
                                             device_id=(cx, cy, c), device_id_type=MESH).start()
        token[...] = jnp.zeros_like(token)

    out = pl.pallas_call(
        body, name=name, in_specs=[HBM] * (2 * n),
        out_specs=[SEMS, SEMS] + [HBM] * (2 * n) + [pl.BlockSpec(memory_space=pltpu.VMEM)],
        out_shape=[pltpu.SemaphoreType.DMA((3 * n,)), pltpu.SemaphoreType.DMA((3 * n,))]
        + [pltpu.HBM(s.shape, s.dtype) for s in sums + landing] + [jax.ShapeDtypeStruct((8, 128), F32)],
        input_output_aliases={b: b + 2 for b in range(2 * n)},
        compiler_params=pltpu.CompilerParams(has_side_effects=FLOWS))(*[_in_hbm(s) for s in sums + landing])
    return out[0], out[1], list(out[2:2 + n]), list(out[2 + n:2 + 2 * n]), out[2 + 2 * n]


def chip_exchange_wait(send, recv, sums, landing, after, name):
    n = len(sums)

    def body(*refs):
        src, land = refs[:n], refs[n:2 * n]
        send_, recv_ = refs[2 * n], refs[2 * n + 1]
        me, c, chips = _other_chips()
        for b in range(n):
            for j, (cx, cy) in enumerate(chips):
                slot = 2 * cx + cy
                pltpu.make_async_remote_copy(src_ref=src[b].at[slot], dst_ref=land[b].at[me],
                                             send_sem=send_.at[3 * b + j], recv_sem=recv_.at[3 * b + j],
                                             device_id=(cx, cy, c), device_id_type=MESH).wait_send()
                pltpu.make_async_remote_copy(src_ref=src[b].at[me], dst_ref=land[b].at[slot],
                                             send_sem=send_.at[3 * b + j], recv_sem=recv_.at[3 * b + j],
                                             device_id=(cx, cy, c), device_id_type=MESH).wait_recv()

    out = pl.pallas_call(
        body, name=name, in_specs=[HBM] * (2 * n) + [SEMS, SEMS, ANY], out_specs=[HBM] * (2 * n),
        out_shape=[pltpu.HBM(s.shape, s.dtype) for s in sums + landing],
        input_output_aliases={b: b for b in range(2 * n)},
        compiler_params=pltpu.CompilerParams(has_side_effects=FLOWS))(*sums, *landing, send, recv, after)
    return list(out[n:])


def _row(a, l):
    return a[l:l + 1]


def local_step(x, p, tgt, small, weights_of, vecs, a_ws, a_bs, grads_ready):
    T = x.shape[0]
    bsf = jnp.broadcast_to(a_bs[:, :, None], (AG_N, CHUNK, AGW))
    vrow = lambda r: vecs[r:r + 1]
    saved = []
    W = []
    h = x
    GA1 = GB1 = GA = GB = GC = bgrp = None

    def ff_fwd(h, l, which, pre, post):
        wa, wb = (GA1, GB1) if which == 1 else (GA, GB)
        tag = "ff%d_l%d" % (which, l)
        xn = rms_fwd(h, _row(pre, l), BF16, tag + "_pre")
        gu, a = ff_gateup(xn, wa, 0, tag + "_gateup")
        f, hn = mm_rs_post(a, wb, 0, FB, FB, h, _row(post, l), 0.5, tag + "_down")
        return hn, (h, xn, gu, a, f)

    for l in range(4):
        rec = {}
        GA1, GB1 = weights_of(l, "a", h)
        h, rec["ff1"] = ff_fwd(h, l, 1, small["ff1_pre_g"], small["ff1_post_g"])
        GA, GB, GC = weights_of(l, "b", h)
        W.append((GA1, GB1, GA, GB, GC))
        if l == 1:
            bgrp = GC[:, C_BGRP:C_BGRP + 256, :].reshape(NCHIP, 4, 64, 256).transpose(1, 0, 2, 3).reshape(4, 256, 256)
        tag = "mix_l%d" % l
        h_in = h
        if l == 1:
            hn = rms_fwd(h, _row(small["mix_pre_g"], l), F32, tag + "_pre")
            pooled, ypre, f = pool_fwd(hn, bgrp, vrow(V_BSCALE), tag + "_pool")
            rec["mix"] = (h_in, pooled, ypre, f)
            h = post_res(h, f, _row(small["mix_post_g"], l), 1.0, tag + "_post")
        else:
            gpost = _row(small["mix_post_g"], l)
            hn = rms_fwd(h, _row(small["mix_pre_g"], l), BF16, tag + "_pre")
            if l == 0:
                zpre = mm_cs(hn, GA, A_AIN, 1536, 1536, BF16, tag + "_in")
                y = gmlp_mid_fwd(zpre, small["a_v_norm_g"], small["a_v_norm_b"], a_ws, bsf, tag + "_gate")
                f, h = mm_rs_post(y, GB, B_AOUT, 768, 768, h, gpost, 1.0, tag + "_out")
                rec["mix"] = (h_in, hn, zpre, y, f)
            elif l == 2:
                ag = mm_cs(hn, GA, A_CIN, 512, 512, BF16, tag + "_pw1")
                zc = conf_conv_fwd(ag, vecs[V_CDW:V_CDW + 32], vrow(V_CBDW), tag + "_conv")
                zs = conf_ln_fwd(zc, vrow(V_CNG), vrow(V_CNB), tag + "_ln")
                f, h = mm_rs_post(zs, GB, B_CPW2, 256, 256, h, gpost, 1.0, tag + "_pw2")
                rec["mix"] = (h_in, hn, ag, zc, zs, f)
            else:
                bgx = mm_cs(hn, GA, A_DIN, 768, 768, BF16, tag + "_in")
                y = sconv_fwd(bgx, vecs[V_DCONV:V_DCONV + 8], tag + "_conv")
                f, h = mm_rs_post(y, GB, B_DOUT, 256, 256, h, gpost, 1.0, tag + "_out")
                rec["mix"] = (h_in, hn, bgx, y, f)
        h, rec["ff2"] = ff_fwd(h, l, 2, small["ff2_pre_g"], small["ff2_post_g"])
        tag = "ple_l%d" % l
        xn = rms_fwd(h, _row(small["ple_gate_norm_g"], l), BF16, tag + "_pre")
        zg = mm_rs(xn, GB, B_PLEG(l), 256, 256, tag + "_gate")
        pb = p[l].astype(BF16)
        pe = mm_cs(pb, GC, 0, 256, 256, F32, tag + "_proj", roff=C_PROJ(l))
        rec["ple"] = (h, xn, zg, pe, pb)
        h = ple_post(h, zg, pe, _row(small["ple_post_g"], l), tag + "_post")
        saved.append(rec)

    dh, loss_cols = loss_head(h, tgt, "loss_head")

    dA2 = dB2 = None
    layer_grads = [None] * 4
    tok = None
    gV = {}
    gains = {k: [None] * 4 for k in ("ff1_pre_g", "ff1_post_g", "mix_pre_g", "mix_post_g", "ff2_pre_g", "ff2_post_g",
                                      "ple_gate_norm_g", "ple_post_g")}
    extra = {}

    def ff_bwd(dh, l, which, pre, post, rec, after=None):
        wa, wb = (GA1, GB1) if which == 1 else (GA, GB)
        tag = "ff%d_l%d_b" % (which, l)
        h_in, xn, gu, a, f = rec
        gp = _row(post, l) if after is None else _row(post, l) + after
        df, dpost, dgu = ff_bwd_down(dh, f, gp, wb, 0, gu, tag + "_down")
        if which == 1:
            db = dw_rs(a, df, FB, FB, tag + "_dwdown")
        else:
            db = dw_rs(a, df, FB, FB, tag + "_dwdown", height=B2_ROWS(l), off=B_FF2D(l), into=dB2)
        dh_in, dpre = mm_cs_t_rms(dgu, wa, 0, 2 * FB, 2 * FB, h_in, _row(pre, l), dh, tag + "_gateup")
        da = dw_cs(xn, dgu, 2 * FB, 2 * FB, tag + "_dwgateup", width=None if which == 1 else A2_COLS(l))
        return dh_in, dpre, dpost, (da, db)

    for l in reversed(range(4)):
        rec = saved[l]
        GA1, GB1, GA, GB, GC = W[l]
        gC = {}
        tag = "ple_l%d_b" % l
        h_in, xn, zg, pe, pb = rec["ple"]
        gpost = _row(small["ple_post_g"], l)
        if tok is not None:
            gpost = gpost + tok
        dzg, dpe, gains["ple_post_g"][l] = ple_post_bwd(dh, zg, pe, gpost, tag + "_post")
        gC[C_PROJ(l)] = dw_cs(pb, dpe, 256, 256, tag + "_dwproj")
        dxn = mm_rs_t(dzg, GB, B_PLEG(l), 256, 256, tag + "_gate")
        dB2 = dw_rs(xn, dzg, 256, 256, tag + "_dwgate", height=B2_ROWS(l), off=B_PLEG(l))
        dh, gains["ple_gate_norm_g"][l] = rms_bwd(dxn, h_in, _row(small["ple_gate_norm_g"], l), dh, tag + "_pre")

        dh, gains["ff2_pre_g"][l], gains["ff2_post_g"][l], (dA2, dB2) = ff_bwd(
            dh, l, 2, small["ff2_pre_g"], small["ff2_post_g"], rec["ff2"])

        tag = "mix_l%d_b" % l
        mix = rec["mix"]
        h_in, f = mix[0], mix[-1]
        if l == 1:
            _, pooled, ypre, _ = mix
            df, gains["mix_post_g"][l] = post_res_bwd(dh, f, _row(small["mix_post_g"], l), 1.0, F32, tag + "_post")
            dhn, dwg, dsc = pool_bwd(df, ypre, pooled, bgrp, vrow(V_BSCALE), tag + "_pool")
            gC[C_BGRP] = dwg.astype(BF16).reshape(4, NCHIP, 64, 256).transpose(1, 0, 2, 3).reshape(NCHIP, 256, 256)
            gV[V_BSCALE] = jnp.pad(dsc, ((0, 7), (0, 0)))
            dh, gains["mix_pre_g"][l] = rms_bwd(dhn, h_in, _row(small["mix_pre_g"], l), dh, tag + "_pre")
        else:
            gpre = _row(small["mix_pre_g"], l)
            df, gains["mix_post_g"][l] = post_res_bwd(dh, f, _row(small["mix_post_g"], l), 1.0, BF16, tag + "_post")
            if l == 0:
                _, hn, zpre, y, _ = mix
                dy = mm_rs_t(df, GB, B_AOUT, 768, 768, tag + "_out")
                dB2 = dw_rs(y, df, 768, 768, tag + "_dwout", height=B2_ROWS(l), off=B_AOUT, into=dB2)
                dz, dws, dbs, dvg, dvb = gmlp_mid_bwd(zpre, dy, small["a_v_norm_g"], small["a_v_norm_b"], a_ws, bsf,
                                                      tag + "_gate")
                extra.update(a_w_s=dws, a_b_s=dbs[:, 0, :], a_v_norm_g=dvg, a_v_norm_b=dvb)
                dA2 = dw_cs(hn, dz, 1536, 1536, tag + "_dwin", width=A2_COLS(l), off=A_AIN, into=dA2)
                dh, gains["mix_pre_g"][l] = mm_cs_t_rms(dz, GA, A_AIN, 1536, 1536, h_in, gpre, dh, tag + "_in")
            elif l == 2:
                _, hn, ag, zc, zs, _ = mix
                dzs = mm_rs_t(df, GB, B_CPW2, 256, 256, tag + "_pw2")
                dB2 = dw_rs(zs, df, 256, 256, tag + "_dwpw2", height=B2_ROWS(l), off=B_CPW2, into=dB2)
                dzc, dng, dnb = conf_ln_bwd(dzs, zc, vrow(V_CNG), vrow(V_CNB), tag + "_ln")
                da_, dg_, dwdw, dbdw = conf_conv_bwd(dzc, ag, vecs[V_CDW:V_CDW + 32], tag + "_conv")
                dag = merge_cols([da_, dg_], tag + "_merge")
                gV[V_CDW] = dwdw
                gV[V_CBDW] = jnp.pad(dbdw, ((0, 7), (0, 0)))
                gV[V_CNG] = jnp.pad(dng, ((0, 7), (0, 0)))
                gV[V_CNB] = jnp.pad(dnb, ((0, 7), (0, 0)))
                dA2 = dw_cs(hn, dag, 512, 512, tag + "_dwpw1", width=A2_COLS(l), off=A_CIN, into=dA2)
                dh, gains["mix_pre_g"][l] = mm_cs_t_rms(dag, GA, A_CIN, 512, 512, h_in, gpre, dh, tag + "_pw1")
            else:
                _, hn, bgx, y, _ = mix
                dy = mm_rs_t(df, GB, B_DOUT, 256, 256, tag + "_out")
                dB2 = dw_rs(y, df, 256, 256, tag + "_dwout", height=B2_ROWS(l), off=B_DOUT, into=dB2)
                db_, dc_, dx_, dwc = sconv_bwd(dy, bgx, vecs[V_DCONV:V_DCONV + 8], tag + "_conv")
                dbgx = merge_cols([db_, dc_, dx_], tag + "_merge")
                gV[V_DCONV] = dwc
                dA2 = dw_cs(hn, dbgx, 768, 768, tag + "_dwin", width=A2_COLS(l), off=A_DIN, into=dA2)
                dh, gains["mix_pre_g"][l] = mm_cs_t_rms(dbgx, GA, A_DIN, 768, 768, h_in, gpre, dh, tag + "_in")

        dC = jnp.concatenate([gC[C_PROJ(l)]] + ([gC[C_BGRP]] if l == 1 else []), axis=1)
        tok = grads_ready(l, "b", (dA2, dB2, dC), dh)
        dh, gains["ff1_pre_g"][l], gains["ff1_post_g"][l], (dA1, dB1) = ff_bwd(
            dh, l, 1, small["ff1_pre_g"], small["ff1_post_g"], rec["ff1"], after=tok)
        layer_grads[l] = (dA1, dB1, dA2, dB2, dC)
        tok = grads_ready(l, "a", (dA1, dB1), dh)

    return loss_cols, dh, layer_grads, gV, gains, extra


GAIN_NAMES = ("ff1_pre_g", "ff1_post_g", "mix_pre_g", "mix_post_g", "ff2_pre_g", "ff2_post_g", "ple_gate_norm_g",
              "ple_post_g")


def _pad_rows(a, rows):
    return jnp.pad(a, ((0, rows - a.shape[0]), (0, 0)))


def kernel(x, p, ff1_pre_g, ff1_w_gate, ff1_w_up, ff1_w_down, ff1_post_g, mix_pre_g, mix_post_g, ff2_pre_g, ff2_w_gate, ff2_w_up, ff2_w_down, ff2_post_g, ple_gate_norm_g, ple_w_gate, ple_w_proj, ple_post_g, a_w_in, a_v_norm_g, a_v_norm_b, a_w_s, a_b_s, a_w_out, b_w_grp, b_scale, c_w_pw1, c_w_dw, c_b_dw, c_norm_g, c_norm_b, c_w_pw2, d_w_in, d_w_conv, d_w_out, loss_target, m_ff1_pre_g, m_ff1_w_gate, m_ff1_w_up, m_ff1_w_down, m_ff1_post_g, m_mix_pre_g, m_mix_post_g, m_ff2_pre_g, m_ff2_w_gate, m_ff2_w_up, m_ff2_w_down, m_ff2_post_g, m_ple_gate_norm_g, m_ple_w_gate, m_ple_w_proj, m_ple_post_g, m_a_w_in, m_a_v_norm_g, m_a_v_norm_b, m_a_w_s, m_a_b_s, m_a_w_out, m_b_w_grp, m_b_scale, m_c_w_pw1, m_c_w_dw, m_c_b_dw, m_c_norm_g, m_c_norm_b, m_c_w_pw2, m_d_w_in, m_d_w_conv, m_d_w_out, v_ff1_pre_g, v_ff1_w_gate, v_ff1_w_up, v_ff1_w_down, v_ff1_post_g, v_mix_pre_g, v_mix_post_g, v_ff2_pre_g, v_ff2_w_gate, v_ff2_w_up, v_ff2_w_down, v_ff2_post_g, v_ple_gate_norm_g, v_ple_w_gate, v_ple_w_proj, v_ple_post_g, v_a_w_in, v_a_v_norm_g, v_a_v_norm_b, v_a_w_s, v_a_b_s, v_a_w_out, v_b_w_grp, v_b_scale, v_c_w_pw1, v_c_w_dw, v_c_b_dw, v_c_norm_g, v_c_norm_b, v_c_w_pw2, v_d_w_in, v_d_w_conv, v_d_w_out):
    args = dict(locals())
    wnames = ["ff1_pre_g", "ff1_w_gate", "ff1_w_up", "ff1_w_down", "ff1_post_g", "mix_pre_g", "mix_post_g",
              "ff2_pre_g", "ff2_w_gate", "ff2_w_up", "ff2_w_down", "ff2_post_g", "ple_gate_norm_g", "ple_w_gate",
              "ple_w_proj", "ple_post_g", "a_w_in", "a_v_norm_g", "a_v_norm_b", "a_w_s", "a_b_s", "a_w_out",
              "b_w_grp", "b_scale", "c_w_pw1", "c_w_dw", "c_b_dw", "c_norm_g", "c_norm_b", "c_w_pw2", "d_w_in",
              "d_w_conv", "d_w_out"]

    P = pack_weights(args)
    G0a = all_gather_chips([P[0][0], P[0][1], P[4]], "gather_l0a")
    vecs = G0a[2].transpose(1, 0, 2).reshape(V_ROWS, DM)
    flying = {}
    tok = 0.0
    for key, bufs in [("0b", P[0][2:]), ("1", P[1]), ("2", P[2]), ("3", P[3])]:
        send, recv, slots, token = gather_start(own_slots(list(bufs)), "gather_start_l" + key)
        flying[key] = (send, recv, slots)
        tok = tok + token[0, 0]
    arrived = {}

    def weights_of(l, part, h):
        if l == 0 and part == "a":
            return G0a[0], G0a[1]
        key = "0b" if l == 0 else str(l)
        if key not in arrived:
            send, recv, slots = flying[key]
            n = len(slots)
            landed = gather_wait(send, recv, slots, list(range(n)), h, "gather_wait_l" + key)
            arrived[key] = gather_pass(landed, "gather_pass_l" + key)
        got = arrived[key]
        if l == 0:
            return tuple(got)
        return tuple(got[:2]) if part == "a" else tuple(got[2:])

    pending = {}
    reduced = {}
    held = {}

    def finish(key, after):
        kinds, send, recv, sums, landing = pending.pop(key)
        landed = chip_exchange_wait(send, recv, sums, landing, after, "rs_wait_l" + key)
        halves = [sum_slots(landed[i], *_tile(k, landed[i]), name="rs_sum_chips_l%s_%d%s" % (key, i, k))
                  for i, k in enumerate(kinds)]
        reduced[key] = pair_share(halves, "rs_pair_share_l" + key)

    def grads_ready(l, part, bufs, dh):
        if part == "b" and l > 0:
            held[l] = list(bufs)
            return None
        if part == "a" and l > 0:
            key, kinds, parts = str(l), "ABABC", list(bufs) + held.pop(l)
        elif part == "b":
            key, kinds, parts = "0b", "ABC", list(bufs)
        else:
            finish("0b", dh)
            return None
        for other in list(pending):
            finish(other, dh)
        got = pair_exchange(parts, "rs_pair_exchange_l" + key)
        sums = [add_half(parts[i], got[i], *_tile(k, got[i]), name="rs_add_pair_l%s_%d%s" % (key, i, k))
                for i, k in enumerate(kinds)]
        send, recv, sums, landing, token = chip_exchange_start(sums, "rs_start_l" + key)
        pending[key] = (kinds, send, recv, sums, landing)
        return token[0, 0]

    small = {k: args[k] for k in GAIN_NAMES}
    small["ff1_pre_g"] = ff1_pre_g + tok
    small["a_v_norm_g"] = a_v_norm_g
    small["a_v_norm_b"] = a_v_norm_b
    loss_cols, grad_x, layer_grads, gV, gains, extra = local_step(
        x[0], p[:, 0], loss_target[0], small, weights_of, vecs, a_w_s[0], a_b_s[0], grads_ready)

    loss = lax.psum((0.5 / DM) * jnp.sum(loss_cols), ("x", "y", "c"))

    dV, dE = pack_small_grads(gV, gains, extra)
    red = reduce_scatter({"A": layer_grads[0][0], "B": layer_grads[0][1], "V": dV, "E": dE}, "rs_l0a")
    (gE,) = all_gather_chips([red["E"]], "gather_replicated_grads")
    per_layer = [[red["A"], red["B"]] + list(reduced["0b"])] + [list(reduced[str(l)]) for l in (1, 2, 3)]
    grads = unpack_grads(per_layer, red["V"], gE.reshape(E_ROWS, DM))

    deltas, new_m, new_v = {}, {}, {}
    for k in wnames:
        deltas[k], new_m[k], new_v[k] = adamw(args[k], grads[k], args["m_" + k], args["v_" + k], "adamw_" + k)
    return (loss, grad_x[None], *[grads[k] for k in wnames], *[deltas[k] for k in wnames],
            *[new_m[k] for k in wnames], *[new_v[k] for k in wnames])


def pack_weights(w):
    padc = lambda a: jnp.pad(a, ((0, 0), (0, FB - FW)))
    mix_in = [w["a_w_in"][0], None, w["c_w_pw1"][0], w["d_w_in"][0]]
    mix_out = [w["a_w_out"][0], None, w["c_w_pw2"][0], w["d_w_out"][0]]
    packed = []
    for l in range(4):
        a1 = jnp.concatenate([padc(w["ff1_w_gate"][l]), padc(w["ff1_w_up"][l])], axis=1).astype(BF16)
        b1 = _pad_rows(w["ff1_w_down"][l], FB).astype(BF16)
        cols = [padc(w["ff2_w_gate"][l]), padc(w["ff2_w_up"][l])]
        rows = [_pad_rows(w["ff2_w_down"][l], FB)]
        if l != 1:
            cols.append(mix_in[l])
            rows.append(mix_out[l])
        rows.append(w["ple_w_gate"][l])
        proj = [w["ple_w_proj"][l]] + ([w["b_w_grp"][0].reshape(256, 256)] if l == 1 else [])
        packed.append((a1, b1, jnp.concatenate(cols, axis=1).astype(BF16), jnp.concatenate(rows, axis=0).astype(BF16),
                       jnp.concatenate(proj, axis=0).astype(BF16)))
    PV = jnp.concatenate([_pad_rows(w["b_scale"], 8), _pad_rows(w["c_b_dw"], 8), _pad_rows(w["c_norm_g"], 8),
                          _pad_rows(w["c_norm_b"], 8), _pad_rows(w["d_w_conv"][0], 8), _pad_rows(w["c_w_dw"][0], 40)],
                         axis=0)
    return packed + [PV]


def pack_small_grads(gV, gains, extra):
    dVt = jnp.concatenate([gV[V_BSCALE], gV[V_CBDW], gV[V_CNG], gV[V_CNB], gV[V_DCONV], gV[V_CDW],
                           jnp.zeros((8, DM), F32)], axis=0)
    dV = dVt.reshape(V_ROWS, NCHIP, 256).transpose(1, 0, 2)
    rowsE = [_pad_rows(jnp.concatenate(gains[k], axis=0), 8) for k in GAIN_NAMES]
    rowsE += [_pad_rows(extra["a_v_norm_g"].reshape(3, DM), 8), _pad_rows(extra["a_v_norm_b"].reshape(3, DM), 8),
              jnp.pad(extra["a_b_s"].reshape(1536), (0, 8 * DM - 1536)).reshape(8, DM),
              extra["a_w_s"].reshape(192, DM)]
    dE = _pad_rows(jnp.concatenate(rowsE, axis=0), E_ROWS).reshape(NCHIP, E_ROWS // NCHIP, DM)
    return dV, dE


def unpack_grads(per_layer, RV, gE):
    grads = {}
    for i, k in enumerate(GAIN_NAMES):
        grads[k] = gE[8 * i:8 * i + 4]
    grads["a_v_norm_g"] = gE[64:67].reshape(1, 3072)
    grads["a_v_norm_b"] = gE[72:75].reshape(1, 3072)
    grads["a_b_s"] = gE[80:88].reshape(8 * DM)[:1536].reshape(1, 12, 128)
    grads["a_w_s"] = gE[88:280].reshape(1, 12, 128, 128)
    col1 = lambda l, off, n: per_layer[l][0][:, off:off + n]
    col2 = lambda l, off, n: per_layer[l][2][:, off:off + n]
    grads["ff1_w_gate"] = jnp.stack([col1(l, A_FF(l, 0), FW) for l in range(4)])
    grads["ff1_w_up"] = jnp.stack([col1(l, A_FF(l, 1), FW) for l in range(4)])
    grads["ff2_w_gate"] = jnp.stack([col2(l, A_FF(l, 2), FW) for l in range(4)])
    grads["ff2_w_up"] = jnp.stack([col2(l, A_FF(l, 3), FW) for l in range(4)])
    grads["a_w_in"] = col2(0, A_AIN, 1536)[None]
    grads["c_w_pw1"] = col2(2, A_CIN, 512)[None]
    grads["d_w_in"] = col2(3, A_DIN, 768)[None]
    row2 = lambda l, off, n: per_layer[l][3][off:off + n]
    grads["ff1_w_down"] = jnp.stack([per_layer[l][1][:FW] for l in range(4)])
    grads["ff2_w_down"] = jnp.stack([row2(l, B_FF2D(l), FW) for l in range(4)])
    grads["ple_w_gate"] = jnp.stack([row2(l, B_PLEG(l), 256) for l in range(4)])
    grads["a_w_out"] = row2(0, B_AOUT, 768)[None]
    grads["c_w_pw2"] = row2(2, B_CPW2, 256)[None]
    grads["d_w_out"] = row2(3, B_DOUT, 256)[None]
    grads["ple_w_proj"] = jnp.stack([per_layer[l][4][C_PROJ(l):C_PROJ(l) + 256] for l in range(4)])
    grads["b_w_grp"] = per_layer[1][4][C_BGRP:C_BGRP + 256].reshape(1, 4, 64, 256)
    grads["b_scale"] = RV[V_BSCALE:V_BSCALE + 1]
    grads["c_b_dw"] = RV[V_CBDW:V_CBDW + 1]
    grads["c_norm_g"] = RV[V_CNG:V_CNG + 1]
    grads["c_norm_b"] = RV[V_CNB:V_CNB + 1]
    grads["d_w_conv"] = RV[V_DCONV:V_DCONV + 3][None]
    grads["c_w_dw"] = RV[V_CDW:V_CDW + 31][None]
    return grads
```

```python
import functools
import math

import jax
import jax.numpy as jnp
from jax import lax
from jax.experimental import pallas as pl
from jax.experimental.pallas import tpu as pltpu

F32, BF16 = jnp.float32, jnp.bfloat16
EPS = 1e-6
DM = 1024
FW = 704
FB = 768
NCHIP = 4
VMEM_LIMIT = 56 * 1024 * 1024
ANY = pl.BlockSpec(memory_space=pl.ANY)
MESH = pl.DeviceIdType.MESH

A_FF = lambda l, j: (j % 2) * FB
A_AIN = A_CIN = A_DIN = 2 * FB
A2_COLS = lambda l: 2 * FB + (1536, 0, 512, 768)[l]
B_FF1D = lambda l: 0
B_FF2D = lambda l: 0
B_AOUT = B_CPW2 = B_DOUT = FB
B_PLEG = lambda l: FB + (768, 0, 256, 256)[l]
B2_ROWS = lambda l: B_PLEG(l) + 256
C_PROJ = lambda l: 0
C_BGRP = 256
V_BSCALE, V_CBDW, V_CNG, V_CNB, V_DCONV, V_CDW, V_ROWS = 0, 8, 16, 24, 32, 40, 80
E_ROWS = 320

ADAM_LR, ADAM_B1, ADAM_B2, ADAM_EPS, ADAM_WD, ADAM_STEP = 0.001, 0.9, 0.999, 1e-08, 0.01, 10


def _cp(sem):
    return pltpu.CompilerParams(dimension_semantics=sem, vmem_limit_bytes=VMEM_LIMIT)


def _sig(x):
    return 0.5 * jnp.tanh(0.5 * x) + 0.5


_GC = math.sqrt(2.0 / math.pi)


def _gelu(x):
    return 0.5 * x * (1.0 + jnp.tanh(_GC * (x + 0.044715 * x * x * x)))


def _gelu_grad(x):
    t = jnp.tanh(_GC * (x + 0.044715 * x * x * x))
    return 0.5 * (1.0 + t) + 0.5 * x * (1.0 - t * t) * _GC * (1.0 + 3.0 * 0.044715 * x * x)


def _dot_nn(a, b):
    return lax.dot_general(a, b, (((1,), (0,)), ((), ())), preferred_element_type=F32)


def _dot_nt(a, b):
    return lax.dot_general(a, b, (((1,), (1,)), ((), ())), preferred_element_type=F32)


def _dot_tn(a, b):
    return lax.dot_general(a, b, (((0,), (0,)), ((), ())), preferred_element_type=F32)


def mm_cs(x, G, off, nb, tn, out_dtype, name, roff=0):
    T, K = x.shape
    tm = min(1024, T)
    nj, ob, rb_ = nb // tn, off // tn, roff // K
    assert nb % tn == 0 and off % tn == 0 and roff % K == 0

    def body(x_ref, w_ref, o_ref):
        o_ref[...] = _dot_nn(x_ref[...], w_ref[...]).astype(o_ref.dtype)

    return pl.pallas_call(
        body, name=name, grid=(T // tm, NCHIP, nj),
        in_specs=[pl.BlockSpec((tm, K), lambda i, s, j: (i, 0)),
                  pl.BlockSpec((None, K, tn), lambda i, s, j: (s, rb_, ob + j))],
        out_specs=pl.BlockSpec((tm, tn), lambda i, s, j: (i, s * nj + j)),
        out_shape=jax.ShapeDtypeStruct((T, NCHIP * nb), out_dtype),
        compiler_params=_cp(("parallel", "arbitrary", "arbitrary")))(x, G)


def mm_cs_t(dy, G, off, nb, tn, name):
    T = dy.shape[0]
    K = G.shape[1]
    tm = min(1024, T)
    nj, ob = nb // tn, off // tn
    nk = NCHIP * nj

    def body(dy_ref, w_ref, o_ref, acc):
        k = pl.program_id(1)

        @pl.when(k == 0)
        def _():
            acc[...] = jnp.zeros_like(acc)

        acc[...] += _dot_nt(dy_ref[...], w_ref[...])

        @pl.when(k == nk - 1)
        def _():
            o_ref[...] = acc[...]

    return pl.pallas_call(
        body, name=name, grid=(T // tm, nk),
        in_specs=[pl.BlockSpec((tm, tn), lambda i, k: (i, k)),
                  pl.BlockSpec((None, K, tn), lambda i, k: (k // nj, 0, ob + k % nj))],
        out_specs=pl.BlockSpec((tm, K), lambda i, k: (i, 0)),
        out_shape=jax.ShapeDtypeStruct((T, K), F32),
        scratch_shapes=[pltpu.VMEM((tm, K), F32)],
        compiler_params=_cp(("parallel", "arbitrary")))(dy, G)


def mm_rs(a, G, off, rb, tk, name):
    T = a.shape[0]
    N = G.shape[2]
    tm = min(1024, T)
    nkk, ob = rb // tk, off // tk
    nk = NCHIP * nkk
    assert rb % tk == 0 and off % tk == 0

    def body(a_ref, w_ref, o_ref, acc):
        k = pl.program_id(1)

        @pl.when(k == 0)
        def _():
            acc[...] = jnp.zeros_like(acc)

        acc[...] += _dot_nn(a_ref[...], w_ref[...])

        @pl.when(k == nk - 1)
        def _():
            o_ref[...] = acc[...]

    return pl.pallas_call(
        body, name=name, grid=(T // tm, nk),
        in_specs=[pl.BlockSpec((tm, tk), lambda i, k: (i, k)),
                  pl.BlockSpec((None, tk, N), lambda i, k: (k // nkk, ob + k % nkk, 0))],
        out_specs=pl.BlockSpec((tm, N), lambda i, k: (i, 0)),
        out_shape=jax.ShapeDtypeStruct((T, N), F32),
        scratch_shapes=[pltpu.VMEM((tm, N), F32)],
        compiler_params=_cp(("parallel", "arbitrary")))(a, G)


def mm_rs_t(dy, G, off, rb, tk, name):
    T, N = dy.shape
    tm = min(1024, T)
    nkk, ob = rb // tk, off // tk
    nk = NCHIP * nkk

    def body(dy_ref, w_ref, o_ref):
        o_ref[...] = _dot_nt(dy_ref[...], w_ref[...]).astype(o_ref.dtype)

    return pl.pallas_call(
        body, name=name, grid=(T // tm, nk),
        in_specs=[pl.BlockSpec((tm, N), lambda i, k: (i, 0)),
                  pl.BlockSpec((None, tk, N), lambda i, k: (k // nkk, ob + k % nkk, 0))],
        out_specs=pl.BlockSpec((tm, tk), lambda i, k: (i, k)),
        out_shape=jax.ShapeDtypeStruct((T, NCHIP * rb), BF16),
        compiler_params=_cp(("parallel", "arbitrary")))(dy, G)


def mm_tn(a, b, tmm, tn, out_shape, out_map, name, into=None):
    T, M = a.shape
    N = b.shape[1]
    tt = min(2048, T)
    nt = T // tt

    def body(a_ref, b_ref, o_ref, acc):
        t = pl.program_id(2)

        @pl.when(t == 0)
        def _():
            acc[...] = jnp.zeros_like(acc)

        acc[...] += _dot_tn(a_ref[...], b_ref[...])

        @pl.when(t == nt - 1)
        def _():
            o_ref[...] = acc[...].astype(o_ref.dtype)

    in_specs = [pl.BlockSpec((tt, tmm), lambda i, j, t: (t, i)), pl.BlockSpec((tt, tn), lambda i, j, t: (t, j))]
    operands = (a, b)
    if into is None:
        def kern(a_ref, b_ref, o_ref, acc):
            body(a_ref, b_ref, o_ref, acc)
        aliases = {}
    else:
        def kern(a_ref, b_ref, into_ref, o_ref, acc):
            body(a_ref, b_ref, o_ref, acc)
        in_specs.append(ANY)
        operands = (a, b, into)
        aliases = {2: 0}
        out_shape = into.shape
    return pl.pallas_call(
        kern, name=name, grid=(M // tmm, N // tn, nt), in_specs=in_specs,
        out_specs=pl.BlockSpec((None, tmm, tn), lambda i, j, t: out_map(i, j)),
        out_shape=jax.ShapeDtypeStruct(out_shape, BF16), input_output_aliases=aliases,
        scratch_shapes=[pltpu.VMEM((tmm, tn), F32)],
        compiler_params=_cp(("parallel", "parallel", "arbitrary")))(*operands)


def dw_cs(x, dy, nb, tn, name, width=None, off=0, into=None):
    K = x.shape[1]
    nj, ob = nb // tn, off // tn
    assert off % tn == 0
    return mm_tn(x, dy, K, tn, (NCHIP, K, width or nb), lambda i, j: (j // nj, 0, ob + j % nj), name, into)


def dw_rs(a, dy, rb, tr, name, height=None, off=0, into=None):
    N = dy.shape[1]
    ni, ob = rb // tr, off // tr
    assert off % tr == 0
    return mm_tn(a, dy, tr, N, (NCHIP, height or rb, N), lambda i, j: (i // ni, ob + i % ni, 0), name, into)


def ff_gateup(xn, GA, off, name):
    T, K = xn.shape
    tm = min(1024, T)
    ob = off // (2 * FB)
    assert off % (2 * FB) == 0

    sub = min(512, tm)

    def body(x_ref, w_ref, gu_ref, a_ref):
        for r0 in range(0, tm, sub):
            r = _dot_nn(x_ref[r0:r0 + sub, :], w_ref[...])
            g, u = r[:, :FB], r[:, FB:]
            gu_ref[r0:r0 + sub, :] = r.astype(gu_ref.dtype)
            a_ref[r0:r0 + sub, :] = (g * _sig(g) * u).astype(a_ref.dtype)

    return pl.pallas_call(
        body, name=name, grid=(T // tm, NCHIP),
        in_specs=[pl.BlockSpec((tm, K), lambda i, s: (i, 0)),
                  pl.BlockSpec((None, K, 2 * FB), lambda i, s: (s, 0, ob))],
        out_specs=[pl.BlockSpec((tm, 2 * FB), lambda i, s: (i, s)), pl.BlockSpec((tm, FB), lambda i, s: (i, s))],
        out_shape=[jax.ShapeDtypeStruct((T, NCHIP * 2 * FB), BF16), jax.ShapeDtypeStruct((T, NCHIP * FB), BF16)],
        compiler_params=_cp(("parallel", "arbitrary")))(xn, GA)


def mm_rs_post(a, G, off, rb, tk, h, g, scale, name, g_next=None):
    T = a.shape[0]
    N = G.shape[2]
    tm = min(1024, T)
    sub = min(512, tm)
    nkk, ob = rb // tk, off // tk
    nk = NCHIP * nkk
    assert rb % tk == 0 and off % tk == 0
    more = g_next is not None

    def body(a_ref, w_ref, h_ref, g_ref, *rest):
        if more:
            gn_ref, f_ref, o_ref, xn_ref, acc = rest
        else:
            f_ref, o_ref, acc = rest
        k = pl.program_id(1)

        @pl.when(k == 0)
        def _():
            acc[...] = jnp.zeros_like(acc)

        acc[...] += _dot_nn(a_ref[...], w_ref[...])

        @pl.when(k == nk - 1)
        def _():
            for r0 in range(0, tm, sub):
                f = acc[r0:r0 + sub, :]
                f_ref[r0:r0 + sub, :] = f
                r = lax.rsqrt(jnp.mean(f * f, axis=-1, keepdims=True) + EPS)
                hn = h_ref[r0:r0 + sub, :] + scale * (f * r * g_ref[...])
                o_ref[r0:r0 + sub, :] = hn
                if more:
                    rn = lax.rsqrt(jnp.mean(hn * hn, axis=-1, keepdims=True) + EPS)
                    xn_ref[r0:r0 + sub, :] = (hn * rn * gn_ref[...]).astype(xn_ref.dtype)

    row = pl.BlockSpec((tm, N), lambda i, k: (i, 0))
    row1 = pl.BlockSpec((tm, N), lambda i, k: (i, 0), pipeline_mode=pl.Buffered(1))
    vec = pl.BlockSpec((1, N), lambda i, k: (0, 0))
    return pl.pallas_call(
        body, name=name, grid=(T // tm, nk),
        in_specs=[pl.BlockSpec((tm, tk), lambda i, k: (i, k)),
                  pl.BlockSpec((None, tk, N), lambda i, k: (k // nkk, ob + k % nkk, 0)),
                  row1, vec] + ([vec] if more else []),
        out_specs=[row, row] + ([row] if more else []),
        out_shape=[jax.ShapeDtypeStruct((T, N), F32), jax.ShapeDtypeStruct((T, N), F32)]
        + ([jax.ShapeDtypeStruct((T, N), BF16)] if more else []),
        scratch_shapes=[pltpu.VMEM((tm, N), F32)],
        compiler_params=_cp(("parallel", "arbitrary")))(*((a, G, h, g) + ((g_next,) if more else ())))


def ff_bwd_down(dh, f, g, GB, down, gu, name):
    T, N = dh.shape
    tm = min(1024, T)
    sub = min(512, tm)
    ob = down // FB

    def body(d_ref, f_ref, g_ref, w_ref, gu_ref, df_ref, dg_ref, dgu_ref):
        i, s = pl.program_id(0), pl.program_id(1)

        @pl.when(s == 0)
        def _():
            dg = jnp.zeros((1, N), F32)
            for r0 in range(0, tm, sub):
                f = f_ref[r0:r0 + sub, :]
                r = lax.rsqrt(jnp.mean(f * f, axis=-1, keepdims=True) + EPS)
                d = 0.5 * d_ref[r0:r0 + sub, :]
                t = d * g_ref[...]
                df_ref[r0:r0 + sub, :] = (
                    r * t - f * (r * r * r * jnp.mean(t * f, axis=-1, keepdims=True))).astype(df_ref.dtype)
                dg = dg + jnp.sum(d * f * r, axis=0, keepdims=True)
            _acc_rows(dg_ref, i, dg)

        for r0 in range(0, tm, sub):
            da = _dot_nt(df_ref[r0:r0 + sub, :], w_ref[...])
            gt = gu_ref[r0:r0 + sub, :FB].astype(F32)
            u = gu_ref[r0:r0 + sub, FB:].astype(F32)
            sg = _sig(gt)
            silu = gt * sg
            dgu_ref[r0:r0 + sub, :FB] = (da * u * (sg + silu - silu * sg)).astype(dgu_ref.dtype)
            dgu_ref[r0:r0 + sub, FB:] = (da * silu).astype(dgu_ref.dtype)

    row1 = pl.BlockSpec((tm, N), lambda i, s: (i, 0), pipeline_mode=pl.Buffered(1))
    row = pl.BlockSpec((tm, N), lambda i, s: (i, 0))
    vec = pl.BlockSpec((1, N), lambda i, s: (0, 0))
    return pl.pallas_call(
        body, name=name, grid=(T // tm, NCHIP),
        in_specs=[row1, row1, vec, pl.BlockSpec((None, FB, N), lambda i, s: (s, ob, 0)),
                  pl.BlockSpec((tm, 2 * FB), lambda i, s: (i, s))],
        out_specs=[row, vec, pl.BlockSpec((tm, 2 * FB), lambda i, s: (i, s))],
        out_shape=[jax.ShapeDtypeStruct((T, N), BF16), jax.ShapeDtypeStruct((1, N), F32),
                   jax.ShapeDtypeStruct((T, NCHIP * 2 * FB), BF16)],
        compiler_params=_cp(("arbitrary", "arbitrary")))(dh, f, g, GB, gu)


def mm_cs_t_rms(dy, G, off, nb, tn, h, g, skip, name):
    T = dy.shape[0]
    K = G.shape[1]
    tm = min(1024, T)
    sub = min(512, tm)
    nj, ob = nb // tn, off // tn
    nk = NCHIP * nj
    assert nb % tn == 0 and off % tn == 0

    def body(dy_ref, w_ref, h_ref, g_ref, s_ref, o_ref, dg_ref, acc):
        i, k = pl.program_id(0), pl.program_id(1)

        @pl.when(k == 0)
        def _():
            acc[...] = jnp.zeros_like(acc)

        acc[...] += _dot_nt(dy_ref[...], w_ref[...])

        @pl.when(k == nk - 1)
        def _():
            dg = jnp.zeros((1, K), F32)
            for r0 in range(0, tm, sub):
                d = acc[r0:r0 + sub, :]
                x = h_ref[r0:r0 + sub, :]
                r = lax.rsqrt(jnp.mean(x * x, axis=-1, keepdims=True) + EPS)
                xh = x * r
                t = d * g_ref[...]
                o_ref[r0:r0 + sub, :] = s_ref[r0:r0 + sub, :] + r * (t - xh * jnp.mean(t * xh, axis=-1, keepdims=True))
                dg = dg + jnp.sum(d * xh, axis=0, keepdims=True)
            _acc_rows(dg_ref, i, dg)

    row1 = pl.BlockSpec((tm, K), lambda i, k: (i, 0), pipeline_mode=pl.Buffered(1))
    row = pl.BlockSpec((tm, K), lambda i, k: (i, 0))
    vec = pl.BlockSpec((1, K), lambda i, k: (0, 0))
    return pl.pallas_call(
        body, name=name, grid=(T // tm, nk),
        in_specs=[pl.BlockSpec((tm, tn), lambda i, k: (i, k)),
                  pl.BlockSpec((None, K, tn), lambda i, k: (k // nj, 0, ob + k % nj)), row1, vec, row1],
        out_specs=[row, vec],
        out_shape=[jax.ShapeDtypeStruct((T, K), F32), jax.ShapeDtypeStruct((1, K), F32)],
        scratch_shapes=[pltpu.VMEM((tm, K), F32)],
        compiler_params=_cp(("arbitrary", "arbitrary")))(dy, G, h, g, skip)


def _rows(tm, C):
    return pl.BlockSpec((tm, C), lambda i: (i, 0))


def _vec(C):
    return pl.BlockSpec((1, C), lambda i: (0, 0))


def _acc_rows(ref, i, val):
    @pl.when(i == 0)
    def _():
        ref[...] = val

    @pl.when(i > 0)
    def _():
        ref[...] += val


def rms_fwd(h, g, out_dtype, name):
    T, C = h.shape
    tm = min(512, T)

    def body(h_ref, g_ref, o_ref):
        x = h_ref[...]
        r = lax.rsqrt(jnp.mean(x * x, axis=-1, keepdims=True) + EPS)
        o_ref[...] = (x * r * g_ref[...]).astype(o_ref.dtype)

    return pl.pallas_call(
        body, name=name, grid=(T // tm,), in_specs=[_rows(tm, C), _vec(C)], out_specs=_rows(tm, C),
        out_shape=jax.ShapeDtypeStruct((T, C), out_dtype), compiler_params=_cp(("parallel",)))(h, g)


def rms_bwd(dxn, h, g, dh_skip, name):
    T, C = h.shape
    tm = min(512, T)

    def body(d_ref, h_ref, g_ref, s_ref, o_ref, dg_ref):
        i = pl.program_id(0)
        x = h_ref[...]
        r = lax.rsqrt(jnp.mean(x * x, axis=-1, keepdims=True) + EPS)
        xh = x * r
        d = d_ref[...].astype(F32)
        t = d * g_ref[...]
        o_ref[...] = s_ref[...] + r * (t - xh * jnp.mean(t * xh, axis=-1, keepdims=True))
        _acc_rows(dg_ref, i, jnp.sum(d * xh, axis=0, keepdims=True))

    return pl.pallas_call(
        body, name=name, grid=(T // tm,),
        in_specs=[_rows(tm, C), _rows(tm, C), _vec(C), _rows(tm, C)],
        out_specs=[_rows(tm, C), _vec(C)],
        out_shape=[jax.ShapeDtypeStruct((T, C), F32), jax.ShapeDtypeStruct((1, C), F32)],
        compiler_params=_cp(("arbitrary",)))(dxn, h, g, dh_skip)


def post_res(h, f, g, scale, name):
    T, C = h.shape
    tm = min(512, T)

    def body(h_ref, f_ref, g_ref, o_ref):
        f = f_ref[...]
        r = lax.rsqrt(jnp.mean(f * f, axis=-1, keepdims=True) + EPS)
        o_ref[...] = h_ref[...] + scale * (f * r * g_ref[...])

    return pl.pallas_call(
        body, name=name, grid=(T // tm,), in_specs=[_rows(tm, C), _rows(tm, C), _vec(C)],
        out_specs=_rows(tm, C), out_shape=jax.ShapeDtypeStruct((T, C), F32),
        compiler_params=_cp(("parallel",)))(h, f, g)


def post_res_bwd(dh, f, g, scale, out_dtype, name):
    T, C = dh.shape
    tm = min(512, T)

    def body(d_ref, f_ref, g_ref, o_ref, dg_ref):
        i = pl.program_id(0)
        f = f_ref[...]
        r = lax.rsqrt(jnp.mean(f * f, axis=-1, keepdims=True) + EPS)
        d = scale * d_ref[...]
        t = d * g_ref[...]
        o_ref[...] = (r * t - f * (r * r * r * jnp.mean(t * f, axis=-1, keepdims=True))).astype(o_ref.dtype)
        _acc_rows(dg_ref, i, jnp.sum(d * f * r, axis=0, keepdims=True))

    return pl.pallas_call(
        body, name=name, grid=(T // tm,), in_specs=[_rows(tm, C), _rows(tm, C), _vec(C)],
        out_specs=[_rows(tm, C), _vec(C)],
        out_shape=[jax.ShapeDtypeStruct((T, C), out_dtype), jax.ShapeDtypeStruct((1, C), F32)],
        compiler_params=_cp(("arbitrary",)))(dh, f, g)


def ff_act(gu, name):
    T = gu.shape[0]
    tm = min(512, T)

    def body(gu_ref, o_ref):
        g = gu_ref[:, :FB].astype(F32)
        u = gu_ref[:, FB:].astype(F32)
        o_ref[...] = (g * _sig(g) * u).astype(o_ref.dtype)

    return pl.pallas_call(
        body, name=name, grid=(T // tm, NCHIP),
        in_specs=[pl.BlockSpec((tm, 2 * FB), lambda i, s: (i, s))],
        out_specs=pl.BlockSpec((tm, FB), lambda i, s: (i, s)),
        out_shape=jax.ShapeDtypeStruct((T, NCHIP * FB), BF16),
        compiler_params=_cp(("parallel", "parallel")))(gu)


def ff_act_bwd(da, gu, name):
    T = gu.shape[0]
    tm = min(512, T)

    def body(da_ref, gu_ref, o_ref):
        g = gu_ref[:, :FB].astype(F32)
        u = gu_ref[:, FB:].astype(F32)
        da = da_ref[...].astype(F32)
        s = _sig(g)
        o_ref[:, :FB] = (da * u * (s * (1.0 + g * (1.0 - s)))).astype(o_ref.dtype)
        o_ref[:, FB:] = (da * (g * s)).astype(o_ref.dtype)

    return pl.pallas_call(
        body, name=name, grid=(T // tm, NCHIP),
        in_specs=[pl.BlockSpec((tm, FB), lambda i, s: (i, s)), pl.BlockSpec((tm, 2 * FB), lambda i, s: (i, s))],
        out_specs=pl.BlockSpec((tm, 2 * FB), lambda i, s: (i, s)),
        out_shape=jax.ShapeDtypeStruct((T, NCHIP * 2 * FB), BF16),
        compiler_params=_cp(("parallel", "parallel")))(da, gu)


def ple_post(h, zg, pe, g, name, g_next=None):
    T, C = h.shape
    tm = min(512, T)
    more = g_next is not None

    def body(h_ref, z_ref, p_ref, g_ref, *rest):
        e = p_ref[...] * _sig(z_ref[...])
        r = lax.rsqrt(jnp.mean(e * e, axis=-1, keepdims=True) + EPS)
        hn = h_ref[...] + e * r * g_ref[...]
        if more:
            gn_ref, o_ref, xn_ref = rest
            rn = lax.rsqrt(jnp.mean(hn * hn, axis=-1, keepdims=True) + EPS)
            xn_ref[...] = (hn * rn * gn_ref[...]).astype(xn_ref.dtype)
        else:
            (o_ref,) = rest
        o_ref[...] = hn

    return pl.pallas_call(
        body, name=name, grid=(T // tm,),
        in_specs=[_rows(tm, C), _rows(tm, C), _rows(tm, C), _vec(C)] + ([_vec(C)] if more else []),
        out_specs=[_rows(tm, C)] + ([_rows(tm, C)] if more else []),
        out_shape=[jax.ShapeDtypeStruct((T, C), F32)] + ([jax.ShapeDtypeStruct((T, C), BF16)] if more else []),
        compiler_params=_cp(("parallel",)))(*((h, zg, pe, g) + ((g_next,) if more else ())))


def ple_post_bwd(dh, zg, pe, g, name):
    T, C = dh.shape
    tm = min(512, T)

    def body(d_ref, z_ref, p_ref, g_ref, dz_ref, dp_ref, dg_ref):
        i = pl.program_id(0)
        s = _sig(z_ref[...])
        pe_ = p_ref[...]
        e = pe_ * s
        r = lax.rsqrt(jnp.mean(e * e, axis=-1, keepdims=True) + EPS)
        d = d_ref[...]
        t = d * g_ref[...]
        de = r * t - e * (r * r * r * jnp.mean(t * e, axis=-1, keepdims=True))
        dp_ref[...] = (de * s).astype(dp_ref.dtype)
        dz_ref[...] = (de * pe_ * s * (1.0 - s)).astype(dz_ref.dtype)
        _acc_rows(dg_ref, i, jnp.sum(d * e * r, axis=0, keepdims=True))

    return pl.pallas_call(
        body, name=name, grid=(T // tm,), in_specs=[_rows(tm, C), _rows(tm, C), _rows(tm, C), _vec(C)],
        out_specs=[_rows(tm, C), _rows(tm, C), _vec(C)],
        out_shape=[jax.ShapeDtypeStruct((T, C), BF16), jax.ShapeDtypeStruct((T, C), BF16),
                   jax.ShapeDtypeStruct((1, C), F32)],
        compiler_params=_cp(("arbitrary",)))(dh, zg, pe, g)


def loss_head(h, tgt, name):
    T, C = h.shape
    tm = min(512, T)

    def body(h_ref, t_ref, d_ref, l_ref):
        i = pl.program_id(0)
        e = h_ref[...] - t_ref[...]
        d_ref[...] = e * (1.0 / C)
        _acc_rows(l_ref, i, jnp.sum(e * e, axis=0, keepdims=True))

    return pl.pallas_call(
        body, name=name, grid=(T // tm,), in_specs=[_rows(tm, C), _rows(tm, C)],
        out_specs=[_rows(tm, C), _vec(C)],
        out_shape=[jax.ShapeDtypeStruct((T, C), F32), jax.ShapeDtypeStruct((1, C), F32)],
        compiler_params=_cp(("arbitrary",)))(h, tgt)


AH, AG_N, AGW, CHUNK = 3072, 12, 256, 128


def _tril_bf16(w):
    r = lax.broadcasted_iota(jnp.int32, (CHUNK, CHUNK), 0)
    c = lax.broadcasted_iota(jnp.int32, (CHUNK, CHUNK), 1)
    return jnp.where(r >= c, w, 0.0).astype(BF16)


def _ln_stats(vs_ref, width):
    v = vs_ref[...]
    mu = jnp.sum(v, axis=-1, keepdims=True) * (1.0 / width)
    vc = v - mu
    var = jnp.sum(vc * vc, axis=-1, keepdims=True) * (1.0 / width)
    return mu, lax.rsqrt(var + EPS)


def gmlp_mid_fwd(zpre, vg, vb, ws, bsf, name):
    T = zpre.shape[0]

    def body(z_ref, vg_ref, vb_ref, ws_ref, bs_ref, y_ref, vs_ref):
        for g in range(AG_N):
            vs_ref[:, g * AGW:(g + 1) * AGW] = _gelu(z_ref[:, AH + g * AGW:AH + (g + 1) * AGW].astype(F32))
        mu, rstd = _ln_stats(vs_ref, AH)
        for g in range(AG_N):
            sl = slice(g * AGW, (g + 1) * AGW)
            vn = ((vs_ref[:, sl] - mu) * rstd * vg_ref[:, sl] + vb_ref[:, sl]).astype(BF16)
            sv = _dot_nn(_tril_bf16(ws_ref[g]), vn) + bs_ref[g]
            u = _gelu(z_ref[:, sl].astype(F32))
            y_ref[:, sl] = (u * sv).astype(y_ref.dtype)

    return pl.pallas_call(
        body, name=name, grid=(T // CHUNK,),
        in_specs=[_rows(CHUNK, 2 * AH), _vec(AH), _vec(AH),
                  pl.BlockSpec((AG_N, CHUNK, CHUNK), lambda i: (0, 0, 0)),
                  pl.BlockSpec((AG_N, CHUNK, AGW), lambda i: (0, 0, 0))],
        out_specs=_rows(CHUNK, AH), out_shape=jax.ShapeDtypeStruct((T, AH), BF16),
        scratch_shapes=[pltpu.VMEM((CHUNK, AH), F32)],
        compiler_params=_cp(("parallel",)))(zpre, vg, vb, ws, bsf)


def gmlp_mid_bwd(zpre, dy, vg, vb, ws, bsf, name):
    T = zpre.shape[0]

    def body(z_ref, dy_ref, vg_ref, vb_ref, ws_ref, bs_ref, dz_ref, dws_ref, dbs_ref, dvg_ref, dvb_ref,
             vs_ref, dvn_ref):
        i = pl.program_id(0)

        @pl.when(i == 0)
        def _():
            dws_ref[...] = jnp.zeros_like(dws_ref)
            dbs_ref[...] = jnp.zeros_like(dbs_ref)
            dvg_ref[...] = jnp.zeros_like(dvg_ref)
            dvb_ref[...] = jnp.zeros_like(dvb_ref)

        for g in range(AG_N):
            vs_ref[:, g * AGW:(g + 1) * AGW] = _gelu(z_ref[:, AH + g * AGW:AH + (g + 1) * AGW].astype(F32))
        mu, rstd = _ln_stats(vs_ref, AH)
        r_i = lax.broadcasted_iota(jnp.int32, (CHUNK, CHUNK), 0)
        c_i = lax.broadcasted_iota(jnp.int32, (CHUNK, CHUNK), 1)
        ones8 = jnp.ones((8, AGW), F32)
        m1 = jnp.zeros((CHUNK, 1), F32)
        m2 = jnp.zeros((CHUNK, 1), F32)
        for g in range(AG_N):
            sl = slice(g * AGW, (g + 1) * AGW)
            vh = (vs_ref[:, sl] - mu) * rstd
            vn = (vh * vg_ref[:, sl] + vb_ref[:, sl]).astype(BF16)
            wm = _tril_bf16(ws_ref[g])
            sv = _dot_nn(wm, vn) + bs_ref[g]
            zu = z_ref[:, sl].astype(F32)
            u = _gelu(zu)
            dyg = dy_ref[:, sl].astype(F32)
            dz_ref[:, sl] = (dyg * sv * _gelu_grad(zu)).astype(dz_ref.dtype)
            dsv = dyg * u
            dsv_b = dsv.astype(BF16)
            dws_ref[g] += jnp.where(r_i >= c_i, _dot_nt(dsv_b, vn), 0.0)
            dbs_ref[g] += _dot_nt(ones8, dsv)
            dvn = _dot_tn(wm, dsv_b)
            dvn_ref[:, sl] = dvn
            dvh = dvn * vg_ref[:, sl]
            m1 = m1 + jnp.sum(dvh, axis=-1, keepdims=True)
            m2 = m2 + jnp.sum(dvh * vh, axis=-1, keepdims=True)
            dvg_ref[:, sl] += jnp.sum(dvn * vh, axis=0, keepdims=True)
            dvb_ref[:, sl] += jnp.sum(dvn, axis=0, keepdims=True)
        m1 = m1 * (1.0 / AH)
        m2 = m2 * (1.0 / AH)
        for g in range(AG_N):
            sl = slice(g * AGW, (g + 1) * AGW)
            vh = (vs_ref[:, sl] - mu) * rstd
            dv = rstd * (dvn_ref[:, sl] * vg_ref[:, sl] - m1 - vh * m2)
            zv = z_ref[:, AH + g * AGW:AH + (g + 1) * AGW].astype(F32)
            dz_ref[:, AH + g * AGW:AH + (g + 1) * AGW] = (dv * _gelu_grad(zv)).astype(dz_ref.dtype)

    full3 = lambda a, b, c: pl.BlockSpec((a, b, c), lambda i: (0, 0, 0))
    return pl.pallas_call(
        body, name=name, grid=(T // CHUNK,),
        in_specs=[_rows(CHUNK, 2 * AH), _rows(CHUNK, AH), _vec(AH), _vec(AH),
                  full3(AG_N, CHUNK, CHUNK), full3(AG_N, CHUNK, AGW)],
        out_specs=[_rows(CHUNK, 2 * AH), full3(AG_N, CHUNK, CHUNK), full3(AG_N, 8, CHUNK), _vec(AH), _vec(AH)],
        out_shape=[jax.ShapeDtypeStruct((T, 2 * AH), BF16), jax.ShapeDtypeStruct((AG_N, CHUNK, CHUNK), F32),
                   jax.ShapeDtypeStruct((AG_N, 8, CHUNK), F32), jax.ShapeDtypeStruct((1, AH), F32),
                   jax.ShapeDtypeStruct((1, AH), F32)],
        scratch_shapes=[pltpu.VMEM((CHUNK, AH), F32), pltpu.VMEM((CHUNK, AH), F32)],
        compiler_params=_cp(("arbitrary",)))(zpre, dy, vg, vb, ws, bsf)


SLAB = 256
NSLAB = DM // SLAB
RC = 256
PAD = 32


def _col(T, j):
    return pl.BlockSpec((T, SLAB), lambda c: (0, j * NSLAB + c))


def _chunks(T, fn):
    def step(i, carry):
        fn(pl.multiple_of(i * RC, RC))
        return carry
    lax.fori_loop(0, T // RC, step, 0)


def _conv_taps(K):
    return [(r, [q for q in range(4) if 8 * q + r < K]) for r in range(min(8, K))]


def _causal_conv(zpad_ref, wrow, K, r0):
    acc = None
    for r, qs in _conv_taps(K):
        a = None
        for q in qs:
            term = wrow(8 * q + r) * zpad_ref[pl.ds(r0 + (PAD - 8 - 8 * q), RC + 8), :]
            a = term if a is None else a + term
        a = a if r == 0 else pltpu.roll(a, r, 0)
        acc = a if acc is None else acc + a
    return acc[8:, :]


def _anticausal_conv(gpad_ref, wrow, K, r0):
    acc = None
    for r, qs in _conv_taps(K):
        b = None
        for q in qs:
            term = wrow(8 * q + r) * gpad_ref[pl.ds(r0 + 8 * q, RC + 8), :]
            b = term if b is None else b + term
        b = b if r == 0 else pltpu.roll(b, RC + 8 - r, 0)
        acc = b if acc is None else acc + b
    return acc[:RC, :]


def _conv_dw(gpad_ref, zpad_ref, dw_ref, K, r0):
    for r, qs in _conv_taps(K):
        gw = gpad_ref[pl.ds(r0, RC + 8), :]
        p = (gw if r == 0 else pltpu.roll(gw, RC + 8 - r, 0))[:RC, :]
        for q in qs:
            z = zpad_ref[pl.ds(r0 + (PAD - 8 * q), RC), :]
            dw_ref[8 * q + r] += jnp.sum((p * z).reshape(RC // 8, 8, SLAB), axis=0)


def _zero_rows(ref, start, n):
    ref[pl.ds(start, n), :] = jnp.zeros((n, SLAB), F32)


def pool_fwd(hn, wg, sc, name):
    T = hn.shape[0]

    def body(h_ref, w_ref, s_ref, p_ref, yp_ref, y_ref, xpad):
        g = pl.program_id(0)
        wf = jnp.left_shift(2, g).astype(F32)
        _zero_rows(xpad, 0, PAD)

        def fill(r0):
            xpad[pl.ds(r0 + PAD, RC), :] = h_ref[pl.ds(r0, RC), :]
        _chunks(T, fill)

        def step(r0):
            w = xpad[pl.ds(r0 + (PAD - 16), RC + 16), :]
            s2 = w + pltpu.roll(w, 1, 0)
            s4 = s2 + pltpu.roll(s2, 2, 0)
            s8 = s4 + pltpu.roll(s4, 4, 0)
            s16 = s8 + pltpu.roll(s8, 8, 0)
            sel = jnp.where(g == 0, s2, jnp.where(g == 1, s4, jnp.where(g == 2, s8, s16)))[16:, :]
            t1 = (r0 + 1 + lax.broadcasted_iota(jnp.int32, (RC, SLAB), 0)).astype(F32)
            pooled = (sel / jnp.minimum(t1, wf) - w[16:, :]).astype(BF16)
            p_ref[pl.ds(r0, RC), :] = pooled
            yp = _dot_nn(pooled, w_ref[...])
            yp_ref[pl.ds(r0, RC), :] = yp
            y_ref[pl.ds(r0, RC), :] = yp * s_ref[...]
        _chunks(T, step)

    slab = pl.BlockSpec((T, SLAB), lambda c: (0, c))
    return pl.pallas_call(
        body, name=name, grid=(NSLAB,),
        in_specs=[slab, pl.BlockSpec((None, SLAB, SLAB), lambda c: (c, 0, 0)), pl.BlockSpec((1, SLAB), lambda c: (0, c))],
        out_specs=[slab, slab, slab],
        out_shape=[jax.ShapeDtypeStruct((T, DM), BF16), jax.ShapeDtypeStruct((T, DM), F32),
                   jax.ShapeDtypeStruct((T, DM), F32)],
        scratch_shapes=[pltpu.VMEM((T + PAD, SLAB), F32)],
        compiler_params=_cp(("parallel",)))(hn, wg, sc)


def pool_bwd(dy, ypre, pooled, wg, sc, name):
    T = dy.shape[0]

    def body(d_ref, yp_ref, p_ref, w_ref, s_ref, dh_ref, dw_ref, ds_ref, qpad, dwacc, dsacc):
        g = pl.program_id(0)
        wf = jnp.left_shift(2, g).astype(F32)
        dwacc[...] = jnp.zeros_like(dwacc)
        dsacc[...] = jnp.zeros_like(dsacc)
        _zero_rows(qpad, T, PAD)

        def first(r0):
            d = d_ref[pl.ds(r0, RC), :]
            dsacc[...] += jnp.sum((d * yp_ref[pl.ds(r0, RC), :]).reshape(RC // 8, 8, SLAB), axis=0)
            dyp = (d * s_ref[...]).astype(BF16)
            dpool = _dot_nt(dyp, w_ref[...])
            dwacc[...] += _dot_tn(p_ref[pl.ds(r0, RC), :], dyp)
            t1 = (r0 + 1 + lax.broadcasted_iota(jnp.int32, (RC, SLAB), 0)).astype(F32)
            qpad[pl.ds(r0, RC), :] = dpool / jnp.minimum(t1, wf)
            dh_ref[pl.ds(r0, RC), :] = dpool
        _chunks(T, first)

        def second(r0):
            w = qpad[pl.ds(r0, RC + 16), :]
            n = RC + 16
            a2 = w + pltpu.roll(w, n - 1, 0)
            a4 = a2 + pltpu.roll(a2, n - 2, 0)
            a8 = a4 + pltpu.roll(a4, n - 4, 0)
            a16 = a8 + pltpu.roll(a8, n - 8, 0)
            sel = jnp.where(g == 0, a2, jnp.where(g == 1, a4, jnp.where(g == 2, a8, a16)))[:RC, :]
            dh_ref[pl.ds(r0, RC), :] = sel - dh_ref[pl.ds(r0, RC), :]
        _chunks(T, second)
        dw_ref[...] = dwacc[...]
        ds_ref[...] = jnp.sum(dsacc[...], axis=0, keepdims=True)

    slab = pl.BlockSpec((T, SLAB), lambda c: (0, c))
    wspec = pl.BlockSpec((None, SLAB, SLAB), lambda c: (c, 0, 0))
    vec = pl.BlockSpec((1, SLAB), lambda c: (0, c))
    return pl.pallas_call(
        body, name=name, grid=(NSLAB,),
        in_specs=[slab, slab, slab, wspec, vec],
        out_specs=[slab, wspec, vec],
        out_shape=[jax.ShapeDtypeStruct((T, DM), F32), jax.ShapeDtypeStruct((NSLAB, SLAB, SLAB), F32),
                   jax.ShapeDtypeStruct((1, DM), F32)],
        scratch_shapes=[pltpu.VMEM((T + PAD, SLAB), F32), pltpu.VMEM((SLAB, SLAB), F32), pltpu.VMEM((8, SLAB), F32)],
        compiler_params=_cp(("parallel",)))(dy, ypre, pooled, wg, sc)


KC = 31
KD = 3


def conf_conv_fwd(ag, wdw, bdw, name):
    T = ag.shape[0]

    def body(a_ref, g_ref, w_ref, b_ref, o_ref, zpad):
        _zero_rows(zpad, 0, PAD)

        def fill(r0):
            a = a_ref[pl.ds(r0, RC), :].astype(F32)
            gt = g_ref[pl.ds(r0, RC), :].astype(F32)
            zpad[pl.ds(r0 + PAD, RC), :] = a * _sig(gt)
        _chunks(T, fill)
        wrow = lambda j: w_ref[KC - 1 - j:KC - j, :]

        def step(r0):
            o_ref[pl.ds(r0, RC), :] = _causal_conv(zpad, wrow, KC, r0) + b_ref[...]
        _chunks(T, step)

    vec = pl.BlockSpec((1, SLAB), lambda c: (0, c))
    return pl.pallas_call(
        body, name=name, grid=(NSLAB,),
        in_specs=[_col(T, 0), _col(T, 1), pl.BlockSpec((32, SLAB), lambda c: (0, c)), vec],
        out_specs=pl.BlockSpec((T, SLAB), lambda c: (0, c)),
        out_shape=jax.ShapeDtypeStruct((T, DM), F32),
        scratch_shapes=[pltpu.VMEM((T + PAD, SLAB), F32)],
        compiler_params=_cp(("parallel",)))(ag, ag, wdw, bdw)


def conf_conv_bwd(dzc, ag, wdw, name):
    T = ag.shape[0]

    def body(d_ref, a_ref, g_ref, w_ref, da_ref, dg_ref, dw_ref, db_ref, zpad, gpad, dwacc, dbacc):
        _zero_rows(zpad, 0, PAD)
        _zero_rows(gpad, T, PAD)
        dwacc[...] = jnp.zeros_like(dwacc)
        dbacc[...] = jnp.zeros_like(dbacc)

        def fill(r0):
            a = a_ref[pl.ds(r0, RC), :].astype(F32)
            gt = g_ref[pl.ds(r0, RC), :].astype(F32)
            zpad[pl.ds(r0 + PAD, RC), :] = a * _sig(gt)
            d = d_ref[pl.ds(r0, RC), :]
            gpad[pl.ds(r0, RC), :] = d
            dbacc[...] += jnp.sum(d.reshape(RC // 8, 8, SLAB), axis=0)
        _chunks(T, fill)
        wrow = lambda j: w_ref[KC - 1 - j:KC - j, :]

        def step(r0):
            dz = _anticausal_conv(gpad, wrow, KC, r0)
            a = a_ref[pl.ds(r0, RC), :].astype(F32)
            s = _sig(g_ref[pl.ds(r0, RC), :].astype(F32))
            da_ref[pl.ds(r0, RC), :] = (dz * s).astype(da_ref.dtype)
            dg_ref[pl.ds(r0, RC), :] = (dz * a * s * (1.0 - s)).astype(dg_ref.dtype)
            _conv_dw(gpad, zpad, dwacc, KC, r0)
        _chunks(T, step)
        dw_ref[...] = jnp.zeros_like(dw_ref)
        for k in range(KC):
            dw_ref[k:k + 1, :] = jnp.sum(dwacc[KC - 1 - k], axis=0, keepdims=True)
        db_ref[...] = jnp.sum(dbacc[...], axis=0, keepdims=True)

    vec = pl.BlockSpec((1, SLAB), lambda c: (0, c))
    w32 = pl.BlockSpec((32, SLAB), lambda c: (0, c))
    return pl.pallas_call(
        body, name=name, grid=(NSLAB,),
        in_specs=[pl.BlockSpec((T, SLAB), lambda c: (0, c)), _col(T, 0), _col(T, 1), w32],
        out_specs=[_col(T, 0), _col(T, 0), w32, vec],
        out_shape=[jax.ShapeDtypeStruct((T, DM), BF16), jax.ShapeDtypeStruct((T, DM), BF16),
                   jax.ShapeDtypeStruct((32, DM), F32), jax.ShapeDtypeStruct((1, DM), F32)],
        scratch_shapes=[pltpu.VMEM((T + PAD, SLAB), F32), pltpu.VMEM((T + PAD, SLAB), F32),
                        pltpu.VMEM((32, 8, SLAB), F32), pltpu.VMEM((8, SLAB), F32)],
        compiler_params=_cp(("parallel",)))(dzc, ag, ag, wdw)


def conf_ln_fwd(zc, g, b, name):
    T, C = zc.shape
    tm = min(512, T)

    def body(z_ref, g_ref, b_ref, o_ref):
        x = z_ref[...]
        xc = x - jnp.mean(x, axis=-1, keepdims=True)
        r = lax.rsqrt(jnp.mean(xc * xc, axis=-1, keepdims=True) + EPS)
        zl = xc * r * g_ref[...] + b_ref[...]
        o_ref[...] = (zl * _sig(zl)).astype(o_ref.dtype)

    return pl.pallas_call(
        body, name=name, grid=(T // tm,), in_specs=[_rows(tm, C), _vec(C), _vec(C)], out_specs=_rows(tm, C),
        out_shape=jax.ShapeDtypeStruct((T, C), BF16), compiler_params=_cp(("parallel",)))(zc, g, b)


def conf_ln_bwd(dzs, zc, g, b, name):
    T, C = zc.shape
    tm = min(512, T)

    def body(d_ref, z_ref, g_ref, b_ref, o_ref, dg_ref, db_ref):
        i = pl.program_id(0)
        x = z_ref[...]
        xc = x - jnp.mean(x, axis=-1, keepdims=True)
        r = lax.rsqrt(jnp.mean(xc * xc, axis=-1, keepdims=True) + EPS)
        xh = xc * r
        zl = xh * g_ref[...] + b_ref[...]
        s = _sig(zl)
        dzl = d_ref[...].astype(F32) * (s * (1.0 + zl * (1.0 - s)))
        t = dzl * g_ref[...]
        o_ref[...] = r * (t - jnp.mean(t, axis=-1, keepdims=True) - xh * jnp.mean(t * xh, axis=-1, keepdims=True))
        _acc_rows(dg_ref, i, jnp.sum(dzl * xh, axis=0, keepdims=True))
        _acc_rows(db_ref, i, jnp.sum(dzl, axis=0, keepdims=True))

    return pl.pallas_call(
        body, name=name, grid=(T // tm,), in_specs=[_rows(tm, C), _rows(tm, C), _vec(C), _vec(C)],
        out_specs=[_rows(tm, C), _vec(C), _vec(C)],
        out_shape=[jax.ShapeDtypeStruct((T, C), F32), jax.ShapeDtypeStruct((1, C), F32),
                   jax.ShapeDtypeStruct((1, C), F32)],
        compiler_params=_cp(("arbitrary",)))(dzs, zc, g, b)


def sconv_fwd(bgx, wc, name):
    T = bgx.shape[0]

    def body(b_ref, c_ref, x_ref, w_ref, o_ref, zpad):
        _zero_rows(zpad, 0, PAD)

        def fill(r0):
            zpad[pl.ds(r0 + PAD, RC), :] = c_ref[pl.ds(r0, RC), :].astype(F32) * x_ref[pl.ds(r0, RC), :].astype(F32)
        _chunks(T, fill)
        wrow = lambda j: w_ref[KD - 1 - j:KD - j, :]

        def step(r0):
            qc = _causal_conv(zpad, wrow, KD, r0)
            o_ref[pl.ds(r0, RC), :] = (b_ref[pl.ds(r0, RC), :].astype(F32) * qc).astype(o_ref.dtype)
        _chunks(T, step)

    return pl.pallas_call(
        body, name=name, grid=(NSLAB,),
        in_specs=[_col(T, 0), _col(T, 1), _col(T, 2), pl.BlockSpec((8, SLAB), lambda c: (0, c))],
        out_specs=pl.BlockSpec((T, SLAB), lambda c: (0, c)),
        out_shape=jax.ShapeDtypeStruct((T, DM), BF16),
        scratch_shapes=[pltpu.VMEM((T + PAD, SLAB), F32)],
        compiler_params=_cp(("parallel",)))(bgx, bgx, bgx, wc)


def sconv_bwd(dy, bgx, wc, name):
    T = bgx.shape[0]

    def body(d_ref, b_ref, c_ref, x_ref, w_ref, db_ref, dc_ref, dx_ref, dw_ref, zpad, gpad, dwacc):
        _zero_rows(zpad, 0, PAD)
        _zero_rows(gpad, T, PAD)
        dwacc[...] = jnp.zeros_like(dwacc)

        def fill(r0):
            zpad[pl.ds(r0 + PAD, RC), :] = c_ref[pl.ds(r0, RC), :].astype(F32) * x_ref[pl.ds(r0, RC), :].astype(F32)
            gpad[pl.ds(r0, RC), :] = d_ref[pl.ds(r0, RC), :].astype(F32) * b_ref[pl.ds(r0, RC), :].astype(F32)
        _chunks(T, fill)
        wrow = lambda j: w_ref[KD - 1 - j:KD - j, :]

        def step(r0):
            qc = _causal_conv(zpad, wrow, KD, r0)
            db_ref[pl.ds(r0, RC), :] = (d_ref[pl.ds(r0, RC), :].astype(F32) * qc).astype(db_ref.dtype)
            dq = _anticausal_conv(gpad, wrow, KD, r0)
            dc_ref[pl.ds(r0, RC), :] = (dq * x_ref[pl.ds(r0, RC), :].astype(F32)).astype(dc_ref.dtype)
            dx_ref[pl.ds(r0, RC), :] = (dq * c_ref[pl.ds(r0, RC), :].astype(F32)).astype(dx_ref.dtype)
            _conv_dw(gpad, zpad, dwacc, KD, r0)
        _chunks(T, step)
        dw_ref[...] = jnp.zeros_like(dw_ref)
        for k in range(KD):
            dw_ref[k:k + 1, :] = jnp.sum(dwacc[KD - 1 - k], axis=0, keepdims=True)

    w8 = pl.BlockSpec((8, SLAB), lambda c: (0, c))
    return pl.pallas_call(
        body, name=name, grid=(NSLAB,),
        in_specs=[pl.BlockSpec((T, SLAB), lambda c: (0, c)), _col(T, 0), _col(T, 1), _col(T, 2), w8],
        out_specs=[_col(T, 0), _col(T, 0), _col(T, 0), w8],
        out_shape=[jax.ShapeDtypeStruct((T, DM), BF16)] * 3 + [jax.ShapeDtypeStruct((8, DM), F32)],
        scratch_shapes=[pltpu.VMEM((T + PAD, SLAB), F32), pltpu.VMEM((T + PAD, SLAB), F32),
                        pltpu.VMEM((8, 8, SLAB), F32)],
        compiler_params=_cp(("parallel",)))(dy, bgx, bgx, bgx, wc)


def merge_cols(parts, name):
    T = parts[0].shape[0]
    n = len(parts)
    C = n * DM
    tm = min(512, T)

    def body(*refs):
        o_ref = refs[n]
        for j in range(n):
            o_ref[:, j * DM:(j + 1) * DM] = refs[j][...]

    return pl.pallas_call(
        body, name=name, grid=(T // tm,),
        in_specs=[_rows(tm, DM) for j in range(n)],
        out_specs=_rows(tm, C), out_shape=jax.ShapeDtypeStruct((T, C), parts[0].dtype),
        compiler_params=_cp(("parallel",)))(*parts)


def adamw(w, g, m, v, name):
    shape = w.shape
    R, C = shape[-2], shape[-1]
    L = w.size // (R * C)
    w2, g2, m2, v2 = (a.reshape(L, R, C) for a in (w, g, m, v))
    tr = R
    while tr * C > 512 * 1024 and tr % 16 == 0:
        tr //= 2
    bc1 = 1.0 - ADAM_B1 ** ADAM_STEP
    bc2 = 1.0 - ADAM_B2 ** ADAM_STEP

    def body(w_ref, g_ref, m_ref, v_ref, d_ref, nm_ref, nv_ref):
        gg = g_ref[...]
        nm = ADAM_B1 * m_ref[...] + (1.0 - ADAM_B1) * gg
        nv = ADAM_B2 * v_ref[...] + (1.0 - ADAM_B2) * (gg * gg)
        nm_ref[...] = nm
        nv_ref[...] = nv
        d_ref[...] = -ADAM_LR * ((nm / bc1) / (jnp.sqrt(nv / bc2) + ADAM_EPS) + ADAM_WD * w_ref[...])

    spec = pl.BlockSpec((None, tr, C), lambda l, i: (l, i, 0))
    outs = pl.pallas_call(
        body, name=name, grid=(L, R // tr), in_specs=[spec] * 4, out_specs=[spec] * 3,
        out_shape=[jax.ShapeDtypeStruct((L, R, C), F32)] * 3,
        compiler_params=_cp(("parallel", "parallel")))(w2, g2, m2, v2)
    return tuple(o.reshape(shape) for o in outs)


def _place():
    x, y, c = lax.axis_index("x"), lax.axis_index("y"), lax.axis_index("c")
    return x, y, c


def all_gather_chips(bufs, name):
    n = len(bufs)
    me_ = 2 * lax.axis_index("x") + lax.axis_index("y")
    slots = [lax.dynamic_update_slice(lax.empty((NCHIP,) + b.shape, b.dtype), b[None], (me_, 0, 0)) for b in bufs]

    def body(*refs):
        dst = refs[n:2 * n]
        send, recv = refs[2 * n:]
        x, y, c = _place()
        me = 2 * x + y
        sib = (x, y, 1 - c)
        chips = [(1 - x, y), (x, 1 - y), (1 - x, 1 - y)]

        def half(b, slot, hc):
            rows = bufs[b].shape[0] // 2
            return dst[b].at[slot, pl.ds(hc * rows, rows), :]

        def remote(k, s, d, to):
            return pltpu.make_async_remote_copy(src_ref=s, dst_ref=d, send_sem=send.at[k], recv_sem=recv.at[k],
                                                device_id=to, device_id_type=MESH)

        first = []
        for b in range(n):
            for j, (cx, cy) in enumerate(chips):
                first.append(remote(b * 6 + j, half(b, me, c), half(b, me, c), (cx, cy, c)))
        for cp in first:
            cp.start()
        passed = []
        for b in range(n):
            for j, (cx, cy) in enumerate(chips):
                slot = 2 * cx + cy
                remote(b * 6 + j, half(b, slot, c), half(b, slot, c), (cx, cy, c)).wait_recv()
                fwd = remote(b * 6 + 3 + j, half(b, slot, c), half(b, slot, c), sib)
                fwd.start()
                passed.append(fwd)
        for b in range(n):
            for j, (cx, cy) in enumerate(chips):
                slot = 2 * cx + cy
                remote(b * 6 + 3 + j, half(b, slot, 1 - c), half(b, slot, 1 - c), sib).wait_recv()
        for cp in first + passed:
            cp.wait_send()

    return pl.pallas_call(
        body, name=name, in_specs=[ANY] * n, out_specs=[ANY] * n,
        out_shape=[jax.ShapeDtypeStruct(s.shape, s.dtype) for s in slots],
        input_output_aliases={b: b for b in range(n)},
        scratch_shapes=[pltpu.SemaphoreType.DMA((6 * n,)), pltpu.SemaphoreType.DMA((6 * n,))],
        compiler_params=pltpu.CompilerParams())(*slots)


def pair_exchange(bufs, name):
    n = len(bufs)

    def body(*refs):
        src, dst = refs[:n], refs[n:2 * n]
        send, recv = refs[2 * n:]
        x, y, c = _place()
        cps = []
        for b in range(n):
            rows = bufs[b].shape[1] // 2
            cp = pltpu.make_async_remote_copy(
                src_ref=src[b].at[:, pl.ds((1 - c) * rows, rows), :], dst_ref=dst[b],
                send_sem=send.at[b], recv_sem=recv.at[b], device_id=(x, y, 1 - c), device_id_type=MESH)
            cp.start()
            cps.append(cp)
        for cp in cps:
            cp.wait()

    return pl.pallas_call(
        body, name=name, in_specs=[ANY] * n, out_specs=[ANY] * n,
        out_shape=[jax.ShapeDtypeStruct((NCHIP, b.shape[1] // 2, b.shape[2]), b.dtype) for b in bufs],
        scratch_shapes=[pltpu.SemaphoreType.DMA((n,)), pltpu.SemaphoreType.DMA((n,))],
        compiler_params=pltpu.CompilerParams())(*bufs)


def add_half(full, got, tr, tc, name):
    _, R, C = full.shape
    rows = R // 2
    nr = rows // tr
    c_arr = lax.axis_index("c").astype(jnp.int32).reshape(1)

    def body(c_ref, a_ref, b_ref, o_ref):
        o_ref[...] = (a_ref[...].astype(F32) + b_ref[...].astype(F32)).astype(o_ref.dtype)

    return pl.pallas_call(
        body, name=name,
        grid_spec=pltpu.PrefetchScalarGridSpec(
            num_scalar_prefetch=1, grid=(NCHIP, nr, C // tc),
            in_specs=[pl.BlockSpec((None, tr, tc), lambda s, i, j, c_ref: (s, c_ref[0] * nr + i, j)),
                      pl.BlockSpec((None, tr, tc), lambda s, i, j, c_ref: (s, i, j))],
            out_specs=pl.BlockSpec((None, tr, tc), lambda s, i, j, c_ref: (s, i, j))),
        out_shape=jax.ShapeDtypeStruct((NCHIP, rows, C), full.dtype),
        compiler_params=_cp(("parallel", "parallel", "parallel")))(c_arr, full, got)


def chip_exchange(bufs, name):
    n = len(bufs)

    def body(*refs):
        src, dst = refs[:n], refs[n:2 * n]
        send, recv, lsem = refs[2 * n:]
        x, y, c = _place()
        me = 2 * x + y
        chips = [(1 - x, y), (x, 1 - y), (1 - x, 1 - y)]
        local = [pltpu.make_async_copy(src[b].at[me], dst[b].at[me], lsem.at[b]) for b in range(n)]
        for cp in local:
            cp.start()
        cps = []
        for b in range(n):
            for j, (cx, cy) in enumerate(chips):
                cp = pltpu.make_async_remote_copy(
                    src_ref=src[b].at[2 * cx + cy], dst_ref=dst[b].at[me],
                    send_sem=send.at[b * 3 + j], recv_sem=recv.at[b * 3 + j],
                    device_id=(cx, cy, c), device_id_type=MESH)
                cp.start()
                cps.append((cp, b, cx, cy, j))
        for cp, b, cx, cy, j in cps:
            cp.wait_send()
            pltpu.make_async_remote_copy(
                src_ref=src[b].at[me], dst_ref=dst[b].at[2 * cx + cy],
                send_sem=send.at[b * 3 + j], recv_sem=recv.at[b * 3 + j],
                device_id=(cx, cy, c), device_id_type=MESH).wait_recv()
        for cp in local:
            cp.wait()

    return pl.pallas_call(
        body, name=name, in_specs=[ANY] * n, out_specs=[ANY] * n,
        out_shape=[jax.ShapeDtypeStruct(b.shape, b.dtype) for b in bufs],
        scratch_shapes=[pltpu.SemaphoreType.DMA((3 * n,)), pltpu.SemaphoreType.DMA((3 * n,)),
                        pltpu.SemaphoreType.DMA((n,))],
        compiler_params=pltpu.CompilerParams())(*bufs)


def sum_slots(buf, tr, tc, name):
    _, r, C = buf.shape
    nr = r // tr
    c_arr = lax.axis_index("c").astype(jnp.int32).reshape(1)

    def body(c_ref, a_ref, o_ref):
        o_ref[...] = ((a_ref[0].astype(F32) + a_ref[1].astype(F32)) + a_ref[2].astype(F32)) + a_ref[3].astype(F32)

    return pl.pallas_call(
        body, name=name,
        grid_spec=pltpu.PrefetchScalarGridSpec(
            num_scalar_prefetch=1, grid=(nr, C // tc),
            in_specs=[pl.BlockSpec((NCHIP, tr, tc), lambda i, j, c_ref: (0, i, j))],
            out_specs=pl.BlockSpec((tr, tc), lambda i, j, c_ref: (c_ref[0] * nr + i, j))),
        out_shape=jax.ShapeDtypeStruct((2 * r, C), F32),
        compiler_params=_cp(("parallel", "parallel")))(c_arr, buf)


def pair_share(bufs, name):
    n = len(bufs)

    def body(*refs):
        dst = refs[n:2 * n]
        send, recv = refs[2 * n:]
        x, y, c = _place()
        cps = []
        for b in range(n):
            rows = bufs[b].shape[0] // 2
            here = dst[b].at[pl.ds(c * rows, rows), :]
            cp = pltpu.make_async_remote_copy(src_ref=here, dst_ref=here, send_sem=send.at[b], recv_sem=recv.at[b],
                                              device_id=(x, y, 1 - c), device_id_type=MESH)
            cp.start()
            cps.append((cp, b))
        for cp, b in cps:
            rows = bufs[b].shape[0] // 2
            there = dst[b].at[pl.ds((1 - c) * rows, rows), :]
            cp.wait_send()
            pltpu.make_async_remote_copy(src_ref=there, dst_ref=there, send_sem=send.at[b], recv_sem=recv.at[b],
                                         device_id=(x, y, 1 - c), device_id_type=MESH).wait_recv()

    return pl.pallas_call(
        body, name=name, in_specs=[ANY] * n, out_specs=[ANY] * n,
        out_shape=[jax.ShapeDtypeStruct(b.shape, b.dtype) for b in bufs],
        input_output_aliases={b: b for b in range(n)},
        scratch_shapes=[pltpu.SemaphoreType.DMA((n,)), pltpu.SemaphoreType.DMA((n,))],
        compiler_params=pltpu.CompilerParams())(*bufs)


def _tile(kind, buf):
    return {"A": (256, buf.shape[2]), "B": (buf.shape[1], 1024), "C": (128, 256), "V": (40, 256),
            "E": (40, 1024)}[kind]


def reduce_scatter(parts, tag):
    names = list(parts)
    got = pair_exchange([parts[k] for k in names], tag + "_pair_exchange")
    sums = [add_half(parts[k], got[i], *_tile(k[0], got[i]), name=tag + "_add_pair_" + k) for i, k in enumerate(names)]
    landed = chip_exchange(sums, tag + "_chip_exchange")
    halves = [sum_slots(landed[i], *_tile(k[0], landed[i]), name=tag + "_sum_chips_" + k) for i, k in enumerate(names)]
    full = pair_share(halves, tag + "_pair_share")
    return dict(zip(names, full))


HBM = pl.BlockSpec(memory_space=pltpu.HBM)
SEMS = pl.BlockSpec(memory_space=pltpu.SEMAPHORE)
FLOWS = pltpu.SideEffectType.DATAFLOW_SIDE_EFFECTING


def _in_hbm(a):
    return pltpu.with_memory_space_constraint(a, pltpu.HBM)


def _other_chips():
    x, y, c = _place()
    return 2 * x + y, c, [(1 - x, y), (x, 1 - y), (1 - x, 1 - y)]


def own_slots(bufs):
    me = 2 * lax.axis_index("x") + lax.axis_index("y")
    return [lax.dynamic_update_slice(lax.empty((NCHIP,) + b.shape, b.dtype), b[None], (me, 0, 0)) for b in bufs]


def gather_start(slots, after, name):
    n = len(slots)

    def body(*refs):
        ins = refs[:n]
        send, recv = refs[n + 1], refs[n + 2]
        token = refs[2 * n + 3]
        me, c, chips = _other_chips()
        for b in range(n):
            rows = slots[b].shape[1] // 2
            own = ins[b].at[me, pl.ds(c * rows, rows), :]
            for j, (cx, cy) in enumerate(chips):
                pltpu.make_async_remote_copy(src_ref=own, dst_ref=own, send_sem=send.at[3 * b + j],
                                             recv_sem=recv.at[3 * b + j], device_id=(cx, cy, c),
                                             device_id_type=MESH).start()
        token[...] = jnp.zeros_like(token)

    out = pl.pallas_call(
        body, name=name, in_specs=[HBM] * n + [ANY],
        out_specs=[SEMS, SEMS] + [HBM] * n + [pl.BlockSpec(memory_space=pltpu.VMEM)],
        out_shape=[pltpu.SemaphoreType.DMA((3 * n,)), pltpu.SemaphoreType.DMA((3 * n,))]
        + [pltpu.HBM(s.shape, s.dtype) for s in slots] + [jax.ShapeDtypeStruct((8, 128), F32)],
        input_output_aliases={b: b + 2 for b in range(n)},
        compiler_params=pltpu.CompilerParams(has_side_effects=FLOWS))(*[_in_hbm(s) for s in slots], after)
    return out[0], out[1], list(out[2:2 + n]), out[2 + n]


def gather_wait(send, recv, slots, picks, after, name):
    n = len(slots)

    def body(*refs):
        ins = refs[:n]
        send_, recv_ = refs[n], refs[n + 1]
        me, c, chips = _other_chips()
        for i, b in enumerate(picks):
            rows = slots[i].shape[1] // 2
            own = ins[i].at[me, pl.ds(c * rows, rows), :]
            for j, (cx, cy) in enumerate(chips):
                got = ins[i].at[2 * cx + cy, pl.ds(c * rows, rows), :]
                pltpu.make_async_remote_copy(src_ref=own, dst_ref=own, send_sem=send_.at[3 * b + j],
                                             recv_sem=recv_.at[3 * b + j], device_id=(cx, cy, c),
                                             device_id_type=MESH).wait_send()
                pltpu.make_async_remote_copy(src_ref=got, dst_ref=got, send_sem=send_.at[3 * b + j],
                                             recv_sem=recv_.at[3 * b + j], device_id=(cx, cy, c),
                                             device_id_type=MESH).wait_recv()

    return pl.pallas_call(
        body, name=name, in_specs=[HBM] * n + [SEMS, SEMS, ANY], out_specs=[HBM] * n,
        out_shape=[pltpu.HBM(s.shape, s.dtype) for s in slots],
        input_output_aliases={b: b for b in range(n)},
        compiler_params=pltpu.CompilerParams(has_side_effects=FLOWS))(*slots, send, recv, after)


def gather_pass(slots, name):
    n = len(slots)

    def body(*refs):
        dst = refs[n:2 * n]
        send, recv = refs[2 * n:]
        x, y, c = _place()
        sib = (x, y, 1 - c)
        chips = [(1 - x, y), (x, 1 - y), (1 - x, 1 - y)]

        def half(b, slot, hc):
            rows = slots[b].shape[1] // 2
            return dst[b].at[slot, pl.ds(hc * rows, rows), :]

        passed = []
        for b in range(n):
            for j, (cx, cy) in enumerate(chips):
                slot = 2 * cx + cy
                cp = pltpu.make_async_remote_copy(src_ref=half(b, slot, c), dst_ref=half(b, slot, c),
                                                  send_sem=send.at[3 * b + j], recv_sem=recv.at[3 * b + j],
                                                  device_id=sib, device_id_type=MESH)
                cp.start()
                passed.append(cp)
        for b in range(n):
            for j, (cx, cy) in enumerate(chips):
                slot = 2 * cx + cy
                pltpu.make_async_remote_copy(src_ref=half(b, slot, 1 - c), dst_ref=half(b, slot, 1 - c),
                                             send_sem=send.at[3 * b + j], recv_sem=recv.at[3 * b + j],
                                             device_id=sib, device_id_type=MESH).wait_recv()
        for cp in passed:
            cp.wait_send()

    return pl.pallas_call(
        body, name=name, in_specs=[ANY] * n, out_specs=[ANY] * n,
        out_shape=[jax.ShapeDtypeStruct(s.shape, s.dtype) for s in slots],
        input_output_aliases={b: b for b in range(n)},
        scratch_shapes=[pltpu.SemaphoreType.DMA((3 * n,)), pltpu.SemaphoreType.DMA((3 * n,))],
        compiler_params=pltpu.CompilerParams())(*slots)


def chip_exchange_start(sums, name):
    n = len(sums)
    me_ = 2 * lax.axis_index("x") + lax.axis_index("y")
    landing = [lax.dynamic_update_slice(lax.empty(s.shape, s.dtype),
                                        lax.dynamic_slice(s, (me_, 0, 0), (1,) + s.shape[1:]), (me_, 0, 0)) for s in sums]

    def body(*refs):
        src, land = refs[:n], refs[n:2 * n]
        send, recv = refs[2 * n], refs[2 * n + 1]
        token = refs[4 * n + 2]
        me, c, chips = _other_chips()
        for b in range(n):
            for j, (cx, cy) in enumerate(chips):
                pltpu.make_async_remote_copy(src_ref=src[b].at[2 * cx + cy], dst_ref=land[b].at[me],
                                             send_sem=send.at[3 * b + j], recv_sem=recv.at[3 * b + j],
                                             device_id=(cx, cy, c), device_id_type=MESH).start()
        token[...] = jnp.zeros_like(token)

    out = pl.pallas_call(
        body, name=name, in_specs=[HBM] * (2 * n),
        out_specs=[SEMS, SEMS] + [HBM] * (2 * n) + [pl.BlockSpec(memory_space=pltpu.VMEM)],
        out_shape=[pltpu.SemaphoreType.DMA((3 * n,)), pltpu.SemaphoreType.DMA((3 * n,))]
        + [pltpu.HBM(s.shape, s.dtype) for s in sums + landing] + [jax.ShapeDtypeStruct((8, 128), F32)],
        input_output_aliases={b: b + 2 for b in range(2 * n)},
        compiler_params=pltpu.CompilerParams(has_side_effects=FLOWS))(*[_in_hbm(s) for s in sums + landing])
    return out[0], out[1], list(out[2:2 + n]), list(out[2 + n:2 + 2 * n]), out[2 + 2 * n]


def chip_exchange_wait(send, recv, sums, landing, after, name):
    n = len(sums)

    def body(*refs):
        src, land = refs[:n], refs[n:2 * n]
        send_, recv_ = refs[2 * n], refs[2 * n + 1]
        me, c, chips = _other_chips()
        for b in range(n):
            for j, (cx, cy) in enumerate(chips):
                slot = 2 * cx + cy
                pltpu.make_async_remote_copy(src_ref=src[b].at[slot], dst_ref=land[b].at[me],
                                             send_sem=send_.at[3 * b + j], recv_sem=recv_.at[3 * b + j],
                                             device_id=(cx, cy, c), device_id_type=MESH).wait_send()
                pltpu.make_async_remote_copy(src_ref=src[b].at[me], dst_ref=land[b].at[slot],
                                             send_sem=send_.at[3 * b + j], recv_sem=recv_.at[3 * b + j],
                                             device_id=(cx, cy, c), device_id_type=MESH).wait_recv()

    out = pl.pallas_call(
        body, name=name, in_specs=[HBM] * (2 * n) + [SEMS, SEMS, ANY], out_specs=[HBM] * (2 * n),
        out_shape=[pltpu.HBM(s.shape, s.dtype) for s in sums + landing],
        input_output_aliases={b: b for b in range(2 * n)},
        compiler_params=pltpu.CompilerParams(has_side_effects=FLOWS))(*sums, *landing, send, recv, after)
    return list(out[n:])


def _row(a, l):
    return a[l:l + 1]


def local_step(x, p, tgt, small, weights_of, vecs, a_ws, a_bs, grads_ready):
    T = x.shape[0]
    bsf = jnp.broadcast_to(a_bs[:, :, None], (AG_N, CHUNK, AGW))
    vrow = lambda r: vecs[r:r + 1]
    saved = []
    W = []
    h = x
    GA1 = GB1 = GA = GB = GC = bgrp = None

    def ff_fwd(h, xn, l, which, post, g_next):
        wa, wb = (GA1, GB1) if which == 1 else (GA, GB)
        tag = "ff%d_l%d" % (which, l)
        gu, a = ff_gateup(xn, wa, 0, tag + "_gateup")
        out = mm_rs_post(a, wb, 0, FB, FB, h, _row(post, l), 0.5, tag + "_down", g_next=g_next)
        return out[1], (out[2] if g_next is not None else None), (h, xn, gu, a, out[0])

    xn = rms_fwd(h, _row(small["ff1_pre_g"], 0), BF16, "ff1_l0_pre")
    for l in range(4):
        rec = {}
        GA1, GB1 = weights_of(l, "a", h)
        g_mix = _row(small["mix_pre_g"], l) if l >= 2 else None
        h, hn, rec["ff1"] = ff_fwd(h, xn, l, 1, small["ff1_post_g"], g_mix)
        GA, GB, GC, wtok = weights_of(l, "b", h)
        W.append((GA1, GB1, GA, GB, GC))
        if l == 1:
            bgrp = GC[:, C_BGRP:C_BGRP + 256, :].reshape(NCHIP, 4, 64, 256).transpose(1, 0, 2, 3).reshape(4, 256, 256)
        tag = "mix_l%d" % l
        h_in = h
        g_ff2 = _row(small["ff2_pre_g"], l)
        if l == 1:
            hn = rms_fwd(h, _row(small["mix_pre_g"], l), F32, tag + "_pre")
            pooled, ypre, f = pool_fwd(hn, bgrp, vrow(V_BSCALE), tag + "_pool")
            rec["mix"] = (h_in, pooled, ypre, f)
            h = post_res(h, f, _row(small["mix_post_g"], l), 1.0, tag + "_post")
            xn = rms_fwd(h, g_ff2, BF16, "ff2_l1_pre")
        else:
            gpost = _row(small["mix_post_g"], l)
            if l == 0:
                g0 = _row(small["mix_pre_g"], l)
                hn = rms_fwd(h, g0 if wtok is None else g0 + wtok, BF16, tag + "_pre")
                zpre = mm_cs(hn, GA, A_AIN, 1536, 1536, BF16, tag + "_in")
                y = gmlp_mid_fwd(zpre, small["a_v_norm_g"], small["a_v_norm_b"], a_ws, bsf, tag + "_gate")
                f, h, xn = mm_rs_post(y, GB, B_AOUT, 768, 768, h, gpost, 1.0, tag + "_out", g_next=g_ff2)
                rec["mix"] = (h_in, hn, zpre, y, f)
            elif l == 2:
                ag = mm_cs(hn, GA, A_CIN, 512, 512, BF16, tag + "_pw1")
                zc = conf_conv_fwd(ag, vecs[V_CDW:V_CDW + 32], vrow(V_CBDW), tag + "_conv")
                zs = conf_ln_fwd(zc, vrow(V_CNG), vrow(V_CNB), tag + "_ln")
                f, h, xn = mm_rs_post(zs, GB, B_CPW2, 256, 256, h, gpost, 1.0, tag + "_pw2", g_next=g_ff2)
                rec["mix"] = (h_in, hn, ag, zc, zs, f)
            else:
                bgx = mm_cs(hn, GA, A_DIN, 768, 768, BF16, tag + "_in")
                y = sconv_fwd(bgx, vecs[V_DCONV:V_DCONV + 8], tag + "_conv")
                f, h, xn = mm_rs_post(y, GB, B_DOUT, 256, 256, h, gpost, 1.0, tag + "_out", g_next=g_ff2)
                rec["mix"] = (h_in, hn, bgx, y, f)
        h, xn, rec["ff2"] = ff_fwd(h, xn, l, 2, small["ff2_post_g"], _row(small["ple_gate_norm_g"], l))
        tag = "ple_l%d" % l
        zg = mm_rs(xn, GB, B_PLEG(l), 256, 256, tag + "_gate")
        pb = p[l].astype(BF16)
        pe = mm_cs(pb, GC, 0, 256, 256, F32, tag + "_proj", roff=C_PROJ(l))
        rec["ple"] = (h, xn, zg, pe, pb)
        if l < 3:
            h, xn = ple_post(h, zg, pe, _row(small["ple_post_g"], l), tag + "_post",
                             g_next=_row(small["ff1_pre_g"], l + 1))
        else:
            (h,) = ple_post(h, zg, pe, _row(small["ple_post_g"], l), tag + "_post")
        saved.append(rec)

    dh, loss_cols = loss_head(h, tgt, "loss_head")

    dA2 = dB2 = None
    layer_grads = [None] * 4
    tok = None
    gV = {}
    gains = {k: [None] * 4 for k in ("ff1_pre_g", "ff1_post_g", "mix_pre_g", "mix_post_g", "ff2_pre_g", "ff2_post_g",
                                      "ple_gate_norm_g", "ple_post_g")}
    extra = {}

    def ff_bwd(dh, l, which, pre, post, rec, after=None):
        wa, wb = (GA1, GB1) if which == 1 else (GA, GB)
        tag = "ff%d_l%d_b" % (which, l)
        h_in, xn, gu, a, f = rec
        gp = _row(post, l) if after is None else _row(post, l) + after
        df, dpost, dgu = ff_bwd_down(dh, f, gp, wb, 0, gu, tag + "_down")
        if which == 1:
            db = dw_rs(a, df, FB, FB, tag + "_dwdown")
        else:
            db = dw_rs(a, df, FB, FB, tag + "_dwdown", height=B2_ROWS(l), off=B_FF2D(l), into=dB2)
        dh_in, dpre = mm_cs_t_rms(dgu, wa, 0, 2 * FB, 2 * FB, h_in, _row(pre, l), dh, tag + "_gateup")
        da = dw_cs(xn, dgu, 2 * FB, 2 * FB, tag + "_dwgateup", width=None if which == 1 else A2_COLS(l))
        return dh_in, dpre, dpost, (da, db)

    for l in reversed(range(4)):
        rec = saved[l]
        GA1, GB1, GA, GB, GC = W[l]
        gC = {}
        tag = "ple_l%d_b" % l
        h_in, xn, zg, pe, pb = rec["ple"]
        gpost = _row(small["ple_post_g"], l)
        if tok is not None:
            gpost = gpost + tok
        dzg, dpe, gains["ple_post_g"][l] = ple_post_bwd(dh, zg, pe, gpost, tag + "_post")
        gC[C_PROJ(l)] = dw_cs(pb, dpe, 256, 256, tag + "_dwproj")
        dxn = mm_rs_t(dzg, GB, B_PLEG(l), 256, 256, tag + "_gate")
        dB2 = dw_rs(xn, dzg, 256, 256, tag + "_dwgate", height=B2_ROWS(l), off=B_PLEG(l))
        dh, gains["ple_gate_norm_g"][l] = rms_bwd(dxn, h_in, _row(small["ple_gate_norm_g"], l), dh, tag + "_pre")

        dh, gains["ff2_pre_g"][l], gains["ff2_post_g"][l], (dA2, dB2) = ff_bwd(
            dh, l, 2, small["ff2_pre_g"], small["ff2_post_g"], rec["ff2"])

        tag = "mix_l%d_b" % l
        mix = rec["mix"]
        h_in, f = mix[0], mix[-1]
        if l == 1:
            _, pooled, ypre, _ = mix
            df, gains["mix_post_g"][l] = post_res_bwd(dh, f, _row(small["mix_post_g"], l), 1.0, F32, tag + "_post")
            dhn, dwg, dsc = pool_bwd(df, ypre, pooled, bgrp, vrow(V_BSCALE), tag + "_pool")
            gC[C_BGRP] = dwg.astype(BF16).reshape(4, NCHIP, 64, 256).transpose(1, 0, 2, 3).reshape(NCHIP, 256, 256)
            gV[V_BSCALE] = jnp.pad(dsc, ((0, 7), (0, 0)))
            dh, gains["mix_pre_g"][l] = rms_bwd(dhn, h_in, _row(small["mix_pre_g"], l), dh, tag + "_pre")
        else:
            gpre = _row(small["mix_pre_g"], l)
            df, gains["mix_post_g"][l] = post_res_bwd(dh, f, _row(small["mix_post_g"], l), 1.0, BF16, tag + "_post")
            if l == 0:
                _, hn, zpre, y, _ = mix
                dy = mm_rs_t(df, GB, B_AOUT, 768, 768, tag + "_out")
                dB2 = dw_rs(y, df, 768, 768, tag + "_dwout", height=B2_ROWS(l), off=B_AOUT, into=dB2)
                dz, dws, dbs, dvg, dvb = gmlp_mid_bwd(zpre, dy, small["a_v_norm_g"], small["a_v_norm_b"], a_ws, bsf,
                                                      tag + "_gate")
                extra.update(a_w_s=dws, a_b_s=dbs[:, 0, :], a_v_norm_g=dvg, a_v_norm_b=dvb)
                dA2 = dw_cs(hn, dz, 1536, 1536, tag + "_dwin", width=A2_COLS(l), off=A_AIN, into=dA2)
                dh, gains["mix_pre_g"][l] = mm_cs_t_rms(dz, GA, A_AIN, 1536, 1536, h_in, gpre, dh, tag + "_in")
            elif l == 2:
                _, hn, ag, zc, zs, _ = mix
                dzs = mm_rs_t(df, GB, B_CPW2, 256, 256, tag + "_pw2")
                dB2 = dw_rs(zs, df, 256, 256, tag + "_dwpw2", height=B2_ROWS(l), off=B_CPW2, into=dB2)
                dzc, dng, dnb = conf_ln_bwd(dzs, zc, vrow(V_CNG), vrow(V_CNB), tag + "_ln")
                da_, dg_, dwdw, dbdw = conf_conv_bwd(dzc, ag, vecs[V_CDW:V_CDW + 32], tag + "_conv")
                dag = merge_cols([da_, dg_], tag + "_merge")
                gV[V_CDW] = dwdw
                gV[V_CBDW] = jnp.pad(dbdw, ((0, 7), (0, 0)))
                gV[V_CNG] = jnp.pad(dng, ((0, 7), (0, 0)))
                gV[V_CNB] = jnp.pad(dnb, ((0, 7), (0, 0)))
                dA2 = dw_cs(hn, dag, 512, 512, tag + "_dwpw1", width=A2_COLS(l), off=A_CIN, into=dA2)
                dh, gains["mix_pre_g"][l] = mm_cs_t_rms(dag, GA, A_CIN, 512, 512, h_in, gpre, dh, tag + "_pw1")
            else:
                _, hn, bgx, y, _ = mix
                dy = mm_rs_t(df, GB, B_DOUT, 256, 256, tag + "_out")
                dB2 = dw_rs(y, df, 256, 256, tag + "_dwout", height=B2_ROWS(l), off=B_DOUT, into=dB2)
                db_, dc_, dx_, dwc = sconv_bwd(dy, bgx, vecs[V_DCONV:V_DCONV + 8], tag + "_conv")
                dbgx = merge_cols([db_, dc_, dx_], tag + "_merge")
                gV[V_DCONV] = dwc
                dA2 = dw_cs(hn, dbgx, 768, 768, tag + "_dwin", width=A2_COLS(l), off=A_DIN, into=dA2)
                dh, gains["mix_pre_g"][l] = mm_cs_t_rms(dbgx, GA, A_DIN, 768, 768, h_in, gpre, dh, tag + "_in")

        dC = jnp.concatenate([gC[C_PROJ(l)]] + ([gC[C_BGRP]] if l == 1 else []), axis=1)
        tok = grads_ready(l, "b", (dA2, dB2, dC), dh)
        dh, gains["ff1_pre_g"][l], gains["ff1_post_g"][l], (dA1, dB1) = ff_bwd(
            dh, l, 1, small["ff1_pre_g"], small["ff1_post_g"], rec["ff1"], after=tok)
        layer_grads[l] = (dA1, dB1, dA2, dB2, dC)
        tok = grads_ready(l, "a", (dA1, dB1), dh)

    return loss_cols, dh, layer_grads, gV, gains, extra


GAIN_NAMES = ("ff1_pre_g", "ff1_post_g", "mix_pre_g", "mix_post_g", "ff2_pre_g", "ff2_post_g", "ple_gate_norm_g",
              "ple_post_g")


def _pad_rows(a, rows):
    return jnp.pad(a, ((0, rows - a.shape[0]), (0, 0)))


def kernel(x, p, ff1_pre_g, ff1_w_gate, ff1_w_up, ff1_w_down, ff1_post_g, mix_pre_g, mix_post_g, ff2_pre_g, ff2_w_gate, ff2_w_up, ff2_w_down, ff2_post_g, ple_gate_norm_g, ple_w_gate, ple_w_proj, ple_post_g, a_w_in, a_v_norm_g, a_v_norm_b, a_w_s, a_b_s, a_w_out, b_w_grp, b_scale, c_w_pw1, c_w_dw, c_b_dw, c_norm_g, c_norm_b, c_w_pw2, d_w_in, d_w_conv, d_w_out, loss_target, m_ff1_pre_g, m_ff1_w_gate, m_ff1_w_up, m_ff1_w_down, m_ff1_post_g, m_mix_pre_g, m_mix_post_g, m_ff2_pre_g, m_ff2_w_gate, m_ff2_w_up, m_ff2_w_down, m_ff2_post_g, m_ple_gate_norm_g, m_ple_w_gate, m_ple_w_proj, m_ple_post_g, m_a_w_in, m_a_v_norm_g, m_a_v_norm_b, m_a_w_s, m_a_b_s, m_a_w_out, m_b_w_grp, m_b_scale, m_c_w_pw1, m_c_w_dw, m_c_b_dw, m_c_norm_g, m_c_norm_b, m_c_w_pw2, m_d_w_in, m_d_w_conv, m_d_w_out, v_ff1_pre_g, v_ff1_w_gate, v_ff1_w_up, v_ff1_w_down, v_ff1_post_g, v_mix_pre_g, v_mix_post_g, v_ff2_pre_g, v_ff2_w_gate, v_ff2_w_up, v_ff2_w_down, v_ff2_post_g, v_ple_gate_norm_g, v_ple_w_gate, v_ple_w_proj, v_ple_post_g, v_a_w_in, v_a_v_norm_g, v_a_v_norm_b, v_a_w_s, v_a_b_s, v_a_w_out, v_b_w_grp, v_b_scale, v_c_w_pw1, v_c_w_dw, v_c_b_dw, v_c_norm_g, v_c_norm_b, v_c_w_pw2, v_d_w_in, v_d_w_conv, v_d_w_out):
    args = dict(locals())
    wnames = ["ff1_pre_g", "ff1_w_gate", "ff1_w_up", "ff1_w_down", "ff1_post_g", "mix_pre_g", "mix_post_g",
              "ff2_pre_g", "ff2_w_gate", "ff2_w_up", "ff2_w_down", "ff2_post_g", "ple_gate_norm_g", "ple_w_gate",
              "ple_w_proj", "ple_post_g", "a_w_in", "a_v_norm_g", "a_v_norm_b", "a_w_s", "a_b_s", "a_w_out",
              "b_w_grp", "b_scale", "c_w_pw1", "c_w_dw", "c_b_dw", "c_norm_g", "c_norm_b", "c_w_pw2", "d_w_in",
              "d_w_conv", "d_w_out"]

    P = pack_weights(args)
    G0a = all_gather_chips([P[0][0], P[0][1], P[4]], "gather_l0a")
    vecs = G0a[2].transpose(1, 0, 2).reshape(V_ROWS, DM)
    flying = {}

    def start(key, bufs, after):
        send, recv, slots, token = gather_start(own_slots(list(bufs)), after, "gather_start_l" + key)
        flying[key] = (send, recv, slots)
        return token[0, 0]

    tok = start("0b", P[0][2:], G0a[0])
    arrived = {}

    def weights_of(l, part, h):
        if l == 0 and part == "a":
            return G0a[0], G0a[1]
        key = "0b" if l == 0 else str(l)
        wtok = None
        if key not in arrived:
            send, recv, slots = flying[key]
            n = len(slots)
            landed = gather_wait(send, recv, slots, list(range(n)), h, "gather_wait_l" + key)
            arrived[key] = gather_pass(landed, "gather_pass_l" + key)
            if l == 0:
                wtok = sum(start(str(k), P[k], arrived[key][0]) for k in (1, 2, 3))
        got = arrived[key]
        if l == 0:
            return tuple(got) + (wtok,)
        return tuple(got[:2]) if part == "a" else tuple(got[2:]) + (None,)

    pending = {}
    reduced = {}
    held = {}

    def finish(key, after):
        kinds, send, recv, sums, landing = pending.pop(key)
        landed = chip_exchange_wait(send, recv, sums, landing, after, "rs_wait_l" + key)
        halves = [sum_slots(landed[i], *_tile(k, landed[i]), name="rs_sum_chips_l%s_%d%s" % (key, i, k))
                  for i, k in enumerate(kinds)]
        reduced[key] = pair_share(halves, "rs_pair_share_l" + key)

    def grads_ready(l, part, bufs, dh):
        if part == "b" and l > 0:
            held[l] = list(bufs)
            return None
        if part == "a" and l > 0:
            key, kinds, parts = str(l), "ABABC", list(bufs) + held.pop(l)
        elif part == "b":
            key, kinds, parts = "0b", "ABC", list(bufs)
        else:
            finish("0b", dh)
            return None
        for other in list(pending):
            finish(other, dh)
        got = pair_exchange(parts, "rs_pair_exchange_l" + key)
        sums = [add_half(parts[i], got[i], *_tile(k, got[i]), name="rs_add_pair_l%s_%d%s" % (key, i, k))
                for i, k in enumerate(kinds)]
        send, recv, sums, landing, token = chip_exchange_start(sums, "rs_start_l" + key)
        pending[key] = (kinds, send, recv, sums, landing)
        return token[0, 0]

    small = {k: args[k] for k in GAIN_NAMES}
    small["ff1_pre_g"] = ff1_pre_g + tok
    small["a_v_norm_g"] = a_v_norm_g
    small["a_v_norm_b"] = a_v_norm_b
    loss_cols, grad_x, layer_grads, gV, gains, extra = local_step(
        x[0], p[:, 0], loss_target[0], small, weights_of, vecs, a_w_s[0], a_b_s[0], grads_ready)

    loss = lax.psum((0.5 / DM) * jnp.sum(loss_cols), ("x", "y", "c"))

    dV, dE = pack_small_grads(gV, gains, extra)
    red = reduce_scatter({"A": layer_grads[0][0], "B": layer_grads[0][1], "V": dV, "E": dE}, "rs_l0a")
    (gE,) = all_gather_chips([red["E"]], "gather_replicated_grads")
    per_layer = [[red["A"], red["B"]] + list(reduced["0b"])] + [list(reduced[str(l)]) for l in (1, 2, 3)]
    grads = unpack_grads(per_layer, red["V"], gE.reshape(E_ROWS, DM))

    deltas, new_m, new_v = {}, {}, {}
    for k in wnames:
        if args[k].shape[-1] == FW:
            t = lambda a: jnp.swapaxes(a, 1, 2)
            outs = adamw(t(args[k]), t(grads[k]), t(args["m_" + k]), t(args["v_" + k]), "adamw_" + k)
            deltas[k], new_m[k], new_v[k] = (t(o) for o in outs)
        else:
            deltas[k], new_m[k], new_v[k] = adamw(args[k], grads[k], args["m_" + k], args["v_" + k], "adamw_" + k)
    return (loss, grad_x[None], *[grads[k] for k in wnames], *[deltas[k] for k in wnames],
            *[new_m[k] for k in wnames], *[new_v[k] for k in wnames])


def pack_weights(w):
    padc = lambda a: jnp.pad(a, ((0, 0), (0, FB - FW)))
    mix_in = [w["a_w_in"][0], None, w["c_w_pw1"][0], w["d_w_in"][0]]
    mix_out = [w["a_w_out"][0], None, w["c_w_pw2"][0], w["d_w_out"][0]]
    packed = []
    for l in range(4):
        a1 = jnp.concatenate([padc(w["ff1_w_gate"][l]), padc(w["ff1_w_up"][l])], axis=1).astype(BF16)
        b1 = _pad_rows(w["ff1_w_down"][l], FB).astype(BF16)
        cols = [padc(w["ff2_w_gate"][l]), padc(w["ff2_w_up"][l])]
        rows = [_pad_rows(w["ff2_w_down"][l], FB)]
        if l != 1:
            cols.append(mix_in[l])
            rows.append(mix_out[l])
        rows.append(w["ple_w_gate"][l])
        proj = [w["ple_w_proj"][l]] + ([w["b_w_grp"][0].reshape(256, 256)] if l == 1 else [])
        packed.append((a1, b1, jnp.concatenate(cols, axis=1).astype(BF16), jnp.concatenate(rows, axis=0).astype(BF16),
                       jnp.concatenate(proj, axis=0).astype(BF16)))
    PV = jnp.concatenate([_pad_rows(w["b_scale"], 8), _pad_rows(w["c_b_dw"], 8), _pad_rows(w["c_norm_g"], 8),
                          _pad_rows(w["c_norm_b"], 8), _pad_rows(w["d_w_conv"][0], 8), _pad_rows(w["c_w_dw"][0], 40)],
                         axis=0)
    return packed + [PV]


def pack_small_grads(gV, gains, extra):
    dVt = jnp.concatenate([gV[V_BSCALE], gV[V_CBDW], gV[V_CNG], gV[V_CNB], gV[V_DCONV], gV[V_CDW],
                           jnp.zeros((8, DM), F32)], axis=0)
    dV = dVt.reshape(V_ROWS, NCHIP, 256).transpose(1, 0, 2)
    rowsE = [_pad_rows(jnp.concatenate(gains[k], axis=0), 8) for k in GAIN_NAMES]
    rowsE += [_pad_rows(extra["a_v_norm_g"].reshape(3, DM), 8), _pad_rows(extra["a_v_norm_b"].reshape(3, DM), 8),
              jnp.pad(extra["a_b_s"].reshape(1536), (0, 8 * DM - 1536)).reshape(8, DM),
              extra["a_w_s"].reshape(192, DM)]
    dE = _pad_rows(jnp.concatenate(rowsE, axis=0), E_ROWS).reshape(NCHIP, E_ROWS // NCHIP, DM)
    return dV, dE


def unpack_grads(per_layer, RV, gE):
    grads = {}
    for i, k in enumerate(GAIN_NAMES):
        grads[k] = gE[8 * i:8 * i + 4]
    grads["a_v_norm_g"] = gE[64:67].reshape(1, 3072)
    grads["a_v_norm_b"] = gE[72:75].reshape(1, 3072)
    grads["a_b_s"] = gE[80:88].reshape(8 * DM)[:1536].reshape(1, 12, 128)
    grads["a_w_s"] = gE[88:280].reshape(1, 12, 128, 128)
    col1 = lambda l, off, n: per_layer[l][0][:, off:off + n]
    col2 = lambda l, off, n: per_layer[l][2][:, off:off + n]
    grads["ff1_w_gate"] = jnp.stack([col1(l, A_FF(l, 0), FW) for l in range(4)])
    grads["ff1_w_up"] = jnp.stack([col1(l, A_FF(l, 1), FW) for l in range(4)])
    grads["ff2_w_gate"] = jnp.stack([col2(l, A_FF(l, 2), FW) for l in range(4)])
    grads["ff2_w_up"] = jnp.stack([col2(l, A_FF(l, 3), FW) for l in range(4)])
    grads["a_w_in"] = col2(0, A_AIN, 1536)[None]
    grads["c_w_pw1"] = col2(2, A_CIN, 512)[None]
    grads["d_w_in"] = col2(3, A_DIN, 768)[None]
    row2 = lambda l, off, n: per_layer[l][3][off:off + n]
    grads["ff1_w_down"] = jnp.stack([per_layer[l][1][:FW] for l in range(4)])
    grads["ff2_w_down"] = jnp.stack([row2(l, B_FF2D(l), FW) for l in range(4)])
    grads["ple_w_gate"] = jnp.stack([row2(l, B_PLEG(l), 256) for l in range(4)])
    grads["a_w_out"] = row2(0, B_AOUT, 768)[None]
    grads["c_w_pw2"] = row2(2, B_CPW2, 256)[None]
    grads["d_w_out"] = row2(3, B_DOUT, 256)[None]
    grads["ple_w_proj"] = jnp.stack([per_layer[l][4][C_PROJ(l):C_PROJ(l) + 256] for l in range(4)])
    grads["b_w_grp"] = per_layer[1][4][C_BGRP:C_BGRP + 256].reshape(1, 4, 64, 256)
    grads["b_scale"] = RV[V_BSCALE:V_BSCALE + 1]
    grads["c_b_dw"] = RV[V_CBDW:V_CBDW + 1]
    grads["c_norm_g"] = RV[V_CNG:V_CNG + 1]
    grads["c_norm_b"] = RV[V_CNB:V_CNB + 1]
    grads["d_w_conv"] = RV[V_DCONV:V_DCONV + 3][None]
    grads["c_w_dw"] = RV[V_CDW:V_CDW + 31][None]
    return grads
```

```python
import functools
import math

import jax
import jax.numpy as jnp
from jax import lax
from jax.experimental import pallas as pl
from jax.experimental.pallas import tpu as pltpu

F32, BF16 = jnp.float32, jnp.bfloat16
EPS = 1e-6
DM = 1024
FW = 704
FB = 768
NCHIP = 4
VMEM_LIMIT = 56 * 1024 * 1024
ANY = pl.BlockSpec(memory_space=pl.ANY)
MESH = pl.DeviceIdType.MESH

A_FF = lambda l, j: (j % 2) * FB
A_AIN = A_CIN = A_DIN = 2 * FB
A2_COLS = lambda l: 2 * FB + (1536, 0, 512, 768)[l]
B_FF1D = lambda l: 0
B_FF2D = lambda l: 0
B_AOUT = B_CPW2 = B_DOUT = FB
B_PLEG = lambda l: FB + (768, 0, 256, 256)[l]
B2_ROWS = lambda l: B_PLEG(l) + 256
C_PROJ = lambda l: 0
C_BGRP = 256
V_BSCALE, V_CBDW, V_CNG, V_CNB, V_DCONV, V_CDW, V_ROWS = 0, 8, 16, 24, 32, 40, 80
E_ROWS = 320

ADAM_LR, ADAM_B1, ADAM_B2, ADAM_EPS, ADAM_WD, ADAM_STEP = 0.001, 0.9, 0.999, 1e-08, 0.01, 10


def _cp(sem):
    return pltpu.CompilerParams(dimension_semantics=sem, vmem_limit_bytes=VMEM_LIMIT)


def _sig(x):
    return 0.5 * jnp.tanh(0.5 * x) + 0.5


_GC = math.sqrt(2.0 / math.pi)


def _gelu(x):
    return 0.5 * x * (1.0 + jnp.tanh(_GC * (x + 0.044715 * x * x * x)))


def _gelu_grad(x):
    t = jnp.tanh(_GC * (x + 0.044715 * x * x * x))
    return 0.5 * (1.0 + t) + 0.5 * x * (1.0 - t * t) * _GC * (1.0 + 3.0 * 0.044715 * x * x)


def _dot_nn(a, b):
    return lax.dot_general(a, b, (((1,), (0,)), ((), ())), preferred_element_type=F32)


def _dot_nt(a, b):
    return lax.dot_general(a, b, (((1,), (1,)), ((), ())), preferred_element_type=F32)


def _dot_tn(a, b):
    return lax.dot_general(a, b, (((0,), (0,)), ((), ())), preferred_element_type=F32)


def mm_cs(x, G, off, nb, tn, out_dtype, name, roff=0):
    T, K = x.shape
    tm = min(1024, T)
    nj, ob, rb_ = nb // tn, off // tn, roff // K
    assert nb % tn == 0 and off % tn == 0 and roff % K == 0

    def body(x_ref, w_ref, o_ref):
        o_ref[...] = _dot_nn(x_ref[...], w_ref[...]).astype(o_ref.dtype)

    return pl.pallas_call(
        body, name=name, grid=(T // tm, NCHIP, nj),
        in_specs=[pl.BlockSpec((tm, K), lambda i, s, j: (i, 0)),
                  pl.BlockSpec((None, K, tn), lambda i, s, j: (s, rb_, ob + j))],
        out_specs=pl.BlockSpec((tm, tn), lambda i, s, j: (i, s * nj + j)),
        out_shape=jax.ShapeDtypeStruct((T, NCHIP * nb), out_dtype),
        compiler_params=_cp(("parallel", "arbitrary", "arbitrary")))(x, G)


def mm_cs_t(dy, G, off, nb, tn, name):
    T = dy.shape[0]
    K = G.shape[1]
    tm = min(1024, T)
    nj, ob = nb // tn, off // tn
    nk = NCHIP * nj

    def body(dy_ref, w_ref, o_ref, acc):
        k = pl.program_id(1)

        @pl.when(k == 0)
        def _():
            acc[...] = jnp.zeros_like(acc)

        acc[...] += _dot_nt(dy_ref[...], w_ref[...])

        @pl.when(k == nk - 1)
        def _():
            o_ref[...] = acc[...]

    return pl.pallas_call(
        body, name=name, grid=(T // tm, nk),
        in_specs=[pl.BlockSpec((tm, tn), lambda i, k: (i, k)),
                  pl.BlockSpec((None, K, tn), lambda i, k: (k // nj, 0, ob + k % nj))],
        out_specs=pl.BlockSpec((tm, K), lambda i, k: (i, 0)),
        out_shape=jax.ShapeDtypeStruct((T, K), F32),
        scratch_shapes=[pltpu.VMEM((tm, K), F32)],
        compiler_params=_cp(("parallel", "arbitrary")))(dy, G)


def mm_rs(a, G, off, rb, tk, name):
    T = a.shape[0]
    N = G.shape[2]
    tm = min(1024, T)
    nkk, ob = rb // tk, off // tk
    nk = NCHIP * nkk
    assert rb % tk == 0 and off % tk == 0

    def body(a_ref, w_ref, o_ref, acc):
        k = pl.program_id(1)

        @pl.when(k == 0)
        def _():
            acc[...] = jnp.zeros_like(acc)

        acc[...] += _dot_nn(a_ref[...], w_ref[...])

        @pl.when(k == nk - 1)
        def _():
            o_ref[...] = acc[...]

    return pl.pallas_call(
        body, name=name, grid=(T // tm, nk),
        in_specs=[pl.BlockSpec((tm, tk), lambda i, k: (i, k)),
                  pl.BlockSpec((None, tk, N), lambda i, k: (k // nkk, ob + k % nkk, 0))],
        out_specs=pl.BlockSpec((tm, N), lambda i, k: (i, 0)),
        out_shape=jax.ShapeDtypeStruct((T, N), F32),
        scratch_shapes=[pltpu.VMEM((tm, N), F32)],
        compiler_params=_cp(("parallel", "arbitrary")))(a, G)


def mm_rs_t(dy, G, off, rb, tk, name):
    T, N = dy.shape
    tm = min(1024, T)
    nkk, ob = rb // tk, off // tk
    nk = NCHIP * nkk

    def body(dy_ref, w_ref, o_ref):
        o_ref[...] = _dot_nt(dy_ref[...], w_ref[...]).astype(o_ref.dtype)

    return pl.pallas_call(
        body, name=name, grid=(T // tm, nk),
        in_specs=[pl.BlockSpec((tm, N), lambda i, k: (i, 0)),
                  pl.BlockSpec((None, tk, N), lambda i, k: (k // nkk, ob + k % nkk, 0))],
        out_specs=pl.BlockSpec((tm, tk), lambda i, k: (i, k)),
        out_shape=jax.ShapeDtypeStruct((T, NCHIP * rb), BF16),
        compiler_params=_cp(("parallel", "arbitrary")))(dy, G)


def mm_tn(a, b, tmm, tn, out_shape, out_map, name, into=None):
    T, M = a.shape
    N = b.shape[1]
    tt = min(2048, T)
    nt = T // tt

    def body(a_ref, b_ref, o_ref, acc):
        t = pl.program_id(2)

        @pl.when(t == 0)
        def _():
            acc[...] = jnp.zeros_like(acc)

        acc[...] += _dot_tn(a_ref[...], b_ref[...])

        @pl.when(t == nt - 1)
        def _():
            o_ref[...] = acc[...].astype(o_ref.dtype)

    in_specs = [pl.BlockSpec((tt, tmm), lambda i, j, t: (t, i)), pl.BlockSpec((tt, tn), lambda i, j, t: (t, j))]
    operands = (a, b)
    if into is None:
        def kern(a_ref, b_ref, o_ref, acc):
            body(a_ref, b_ref, o_ref, acc)
        aliases = {}
    else:
        def kern(a_ref, b_ref, into_ref, o_ref, acc):
            body(a_ref, b_ref, o_ref, acc)
        in_specs.append(ANY)
        operands = (a, b, into)
        aliases = {2: 0}
        out_shape = into.shape
    return pl.pallas_call(
        kern, name=name, grid=(M // tmm, N // tn, nt), in_specs=in_specs,
        out_specs=pl.BlockSpec((None, tmm, tn), lambda i, j, t: out_map(i, j)),
        out_shape=jax.ShapeDtypeStruct(out_shape, BF16), input_output_aliases=aliases,
        scratch_shapes=[pltpu.VMEM((tmm, tn), F32)],
        compiler_params=_cp(("parallel", "parallel", "arbitrary")))(*operands)


def dw_cs(x, dy, nb, tn, name, width=None, off=0, into=None):
    K = x.shape[1]
    nj, ob = nb // tn, off // tn
    assert off % tn == 0
    return mm_tn(x, dy, K, tn, (NCHIP, K, width or nb), lambda i, j: (j // nj, 0, ob + j % nj), name, into)


def dw_rs(a, dy, rb, tr, name, height=None, off=0, into=None):
    N = dy.shape[1]
    ni, ob = rb // tr, off // tr
    assert off % tr == 0
    return mm_tn(a, dy, tr, N, (NCHIP, height or rb, N), lambda i, j: (i // ni, ob + i % ni, 0), name, into)


def ff_gateup(xn, GA, off, name):
    T, K = xn.shape
    tm = min(1024, T)
    ob = off // (2 * FB)
    assert off % (2 * FB) == 0

    sub = min(512, tm)

    def body(x_ref, w_ref, gu_ref, a_ref):
        for r0 in range(0, tm, sub):
            r = _dot_nn(x_ref[r0:r0 + sub, :], w_ref[...])
            g, u = r[:, :FB], r[:, FB:]
            gu_ref[r0:r0 + sub, :] = r.astype(gu_ref.dtype)
            a_ref[r0:r0 + sub, :] = (g * _sig(g) * u).astype(a_ref.dtype)

    return pl.pallas_call(
        body, name=name, grid=(T // tm, NCHIP),
        in_specs=[pl.BlockSpec((tm, K), lambda i, s: (i, 0)),
                  pl.BlockSpec((None, K, 2 * FB), lambda i, s: (s, 0, ob))],
        out_specs=[pl.BlockSpec((tm, 2 * FB), lambda i, s: (i, s)), pl.BlockSpec((tm, FB), lambda i, s: (i, s))],
        out_shape=[jax.ShapeDtypeStruct((T, NCHIP * 2 * FB), BF16), jax.ShapeDtypeStruct((T, NCHIP * FB), BF16)],
        compiler_params=_cp(("parallel", "arbitrary")))(xn, GA)


def mm_rs_post(a, G, off, rb, tk, h, g, scale, name, g_next=None):
    T = a.shape[0]
    N = G.shape[2]
    tm = min(1024, T)
    sub = min(512, tm)
    nkk, ob = rb // tk, off // tk
    nk = NCHIP * nkk
    assert rb % tk == 0 and off % tk == 0
    more = g_next is not None

    def body(a_ref, w_ref, h_ref, g_ref, *rest):
        if more:
            gn_ref, f_ref, o_ref, xn_ref, acc = rest
        else:
            f_ref, o_ref, acc = rest
        k = pl.program_id(1)

        @pl.when(k == 0)
        def _():
            acc[...] = jnp.zeros_like(acc)

        acc[...] += _dot_nn(a_ref[...], w_ref[...])

        @pl.when(k == nk - 1)
        def _():
            for r0 in range(0, tm, sub):
                f = acc[r0:r0 + sub, :]
                f_ref[r0:r0 + sub, :] = f
                r = lax.rsqrt(jnp.mean(f * f, axis=-1, keepdims=True) + EPS)
                hn = h_ref[r0:r0 + sub, :] + scale * (f * r * g_ref[...])
                o_ref[r0:r0 + sub, :] = hn
                if more:
                    rn = lax.rsqrt(jnp.mean(hn * hn, axis=-1, keepdims=True) + EPS)
                    xn_ref[r0:r0 + sub, :] = (hn * rn * gn_ref[...]).astype(xn_ref.dtype)

    row = pl.BlockSpec((tm, N), lambda i, k: (i, 0))
    row1 = pl.BlockSpec((tm, N), lambda i, k: (i, 0), pipeline_mode=pl.Buffered(1))
    vec = pl.BlockSpec((1, N), lambda i, k: (0, 0))
    return pl.pallas_call(
        body, name=name, grid=(T // tm, nk),
        in_specs=[pl.BlockSpec((tm, tk), lambda i, k: (i, k)),
                  pl.BlockSpec((None, tk, N), lambda i, k: (k // nkk, ob + k % nkk, 0)),
                  row1, vec] + ([vec] if more else []),
        out_specs=[row, row] + ([row] if more else []),
        out_shape=[jax.ShapeDtypeStruct((T, N), F32), jax.ShapeDtypeStruct((T, N), F32)]
        + ([jax.ShapeDtypeStruct((T, N), BF16)] if more else []),
        scratch_shapes=[pltpu.VMEM((tm, N), F32)],
        compiler_params=_cp(("parallel", "arbitrary")))(*((a, G, h, g) + ((g_next,) if more else ())))


def ff_bwd_down(dh, f, g, GB, down, gu, name):
    T, N = dh.shape
    tm = min(1024, T)
    sub = min(512, tm)
    ob = down // FB

    def body(d_ref, f_ref, g_ref, w_ref, gu_ref, df_ref, dg_ref, dgu_ref):
        i, s = pl.program_id(0), pl.program_id(1)

        @pl.when(s == 0)
        def _():
            dg = jnp.zeros((1, N), F32)
            for r0 in range(0, tm, sub):
                f = f_ref[r0:r0 + sub, :]
                r = lax.rsqrt(jnp.mean(f * f, axis=-1, keepdims=True) + EPS)
                d = 0.5 * d_ref[r0:r0 + sub, :]
                t = d * g_ref[...]
                df_ref[r0:r0 + sub, :] = (
                    r * t - f * (r * r * r * jnp.mean(t * f, axis=-1, keepdims=True))).astype(df_ref.dtype)
                dg = dg + jnp.sum(d * f * r, axis=0, keepdims=True)
            _acc_rows(dg_ref, i, dg)

        for r0 in range(0, tm, sub):
            da = _dot_nt(df_ref[r0:r0 + sub, :], w_ref[...])
            gt = gu_ref[r0:r0 + sub, :FB].astype(F32)
            u = gu_ref[r0:r0 + sub, FB:].astype(F32)
            sg = _sig(gt)
            silu = gt * sg
            dgu_ref[r0:r0 + sub, :FB] = (da * u * (sg + silu - silu * sg)).astype(dgu_ref.dtype)
            dgu_ref[r0:r0 + sub, FB:] = (da * silu).astype(dgu_ref.dtype)

    row1 = pl.BlockSpec((tm, N), lambda i, s: (i, 0), pipeline_mode=pl.Buffered(1))
    row = pl.BlockSpec((tm, N), lambda i, s: (i, 0))
    vec = pl.BlockSpec((1, N), lambda i, s: (0, 0))
    return pl.pallas_call(
        body, name=name, grid=(T // tm, NCHIP),
        in_specs=[row1, row1, vec, pl.BlockSpec((None, FB, N), lambda i, s: (s, ob, 0)),
                  pl.BlockSpec((tm, 2 * FB), lambda i, s: (i, s))],
        out_specs=[row, vec, pl.BlockSpec((tm, 2 * FB), lambda i, s: (i, s))],
        out_shape=[jax.ShapeDtypeStruct((T, N), BF16), jax.ShapeDtypeStruct((1, N), F32),
                   jax.ShapeDtypeStruct((T, NCHIP * 2 * FB), BF16)],
        compiler_params=_cp(("arbitrary", "arbitrary")))(dh, f, g, GB, gu)


def mm_cs_t_rms(dy, G, off, nb, tn, h, g, skip, name):
    T = dy.shape[0]
    K = G.shape[1]
    tm = min(1024, T)
    sub = min(512, tm)
    nj, ob = nb // tn, off // tn
    nk = NCHIP * nj
    assert nb % tn == 0 and off % tn == 0

    def body(dy_ref, w_ref, h_ref, g_ref, s_ref, o_ref, dg_ref, acc):
        i, k = pl.program_id(0), pl.program_id(1)

        @pl.when(k == 0)
        def _():
            acc[...] = jnp.zeros_like(acc)

        acc[...] += _dot_nt(dy_ref[...], w_ref[...])

        @pl.when(k == nk - 1)
        def _():
            dg = jnp.zeros((1, K), F32)
            for r0 in range(0, tm, sub):
                d = acc[r0:r0 + sub, :]
                x = h_ref[r0:r0 + sub, :]
                r = lax.rsqrt(jnp.mean(x * x, axis=-1, keepdims=True) + EPS)
                xh = x * r
                t = d * g_ref[...]
                o_ref[r0:r0 + sub, :] = s_ref[r0:r0 + sub, :] + r * (t - xh * jnp.mean(t * xh, axis=-1, keepdims=True))
                dg = dg + jnp.sum(d * xh, axis=0, keepdims=True)
            _acc_rows(dg_ref, i, dg)

    row1 = pl.BlockSpec((tm, K), lambda i, k: (i, 0), pipeline_mode=pl.Buffered(1))
    row = pl.BlockSpec((tm, K), lambda i, k: (i, 0))
    vec = pl.BlockSpec((1, K), lambda i, k: (0, 0))
    return pl.pallas_call(
        body, name=name, grid=(T // tm, nk),
        in_specs=[pl.BlockSpec((tm, tn), lambda i, k: (i, k)),
                  pl.BlockSpec((None, K, tn), lambda i, k: (k // nj, 0, ob + k % nj)), row1, vec, row1],
        out_specs=[row, vec],
        out_shape=[jax.ShapeDtypeStruct((T, K), F32), jax.ShapeDtypeStruct((1, K), F32)],
        scratch_shapes=[pltpu.VMEM((tm, K), F32)],
        compiler_params=_cp(("arbitrary", "arbitrary")))(dy, G, h, g, skip)


def _rows(tm, C):
    return pl.BlockSpec((tm, C), lambda i: (i, 0))


def _vec(C):
    return pl.BlockSpec((1, C), lambda i: (0, 0))


def _acc_rows(ref, i, val):
    @pl.when(i == 0)
    def _():
        ref[...] = val

    @pl.when(i > 0)
    def _():
        ref[...] += val


def rms_fwd(h, g, out_dtype, name):
    T, C = h.shape
    tm = min(512, T)

    def body(h_ref, g_ref, o_ref):
        x = h_ref[...]
        r = lax.rsqrt(jnp.mean(x * x, axis=-1, keepdims=True) + EPS)
        o_ref[...] = (x * r * g_ref[...]).astype(o_ref.dtype)

    return pl.pallas_call(
        body, name=name, grid=(T // tm,), in_specs=[_rows(tm, C), _vec(C)], out_specs=_rows(tm, C),
        out_shape=jax.ShapeDtypeStruct((T, C), out_dtype), compiler_params=_cp(("parallel",)))(h, g)


def rms_bwd(dxn, h, g, dh_skip, name):
    T, C = h.shape
    tm = min(512, T)

    def body(d_ref, h_ref, g_ref, s_ref, o_ref, dg_ref):
        i = pl.program_id(0)
        x = h_ref[...]
        r = lax.rsqrt(jnp.mean(x * x, axis=-1, keepdims=True) + EPS)
        xh = x * r
        d = d_ref[...].astype(F32)
        t = d * g_ref[...]
        o_ref[...] = s_ref[...] + r * (t - xh * jnp.mean(t * xh, axis=-1, keepdims=True))
        _acc_rows(dg_ref, i, jnp.sum(d * xh, axis=0, keepdims=True))

    return pl.pallas_call(
        body, name=name, grid=(T // tm,),
        in_specs=[_rows(tm, C), _rows(tm, C), _vec(C), _rows(tm, C)],
        out_specs=[_rows(tm, C), _vec(C)],
        out_shape=[jax.ShapeDtypeStruct((T, C), F32), jax.ShapeDtypeStruct((1, C), F32)],
        compiler_params=_cp(("arbitrary",)))(dxn, h, g, dh_skip)


def post_res(h, f, g, scale, name):
    T, C = h.shape
    tm = min(512, T)

    def body(h_ref, f_ref, g_ref, o_ref):
        f = f_ref[...]
        r = lax.rsqrt(jnp.mean(f * f, axis=-1, keepdims=True) + EPS)
        o_ref[...] = h_ref[...] + scale * (f * r * g_ref[...])

    return pl.pallas_call(
        body, name=name, grid=(T // tm,), in_specs=[_rows(tm, C), _rows(tm, C), _vec(C)],
        out_specs=_rows(tm, C), out_shape=jax.ShapeDtypeStruct((T, C), F32),
        compiler_params=_cp(("parallel",)))(h, f, g)


def post_res_bwd(dh, f, g, scale, out_dtype, name):
    T, C = dh.shape
    tm = min(512, T)

    def body(d_ref, f_ref, g_ref, o_ref, dg_ref):
        i = pl.program_id(0)
        f = f_ref[...]
        r = lax.rsqrt(jnp.mean(f * f, axis=-1, keepdims=True) + EPS)
        d = scale * d_ref[...]
        t = d * g_ref[...]
        o_ref[...] = (r * t - f * (r * r * r * jnp.mean(t * f, axis=-1, keepdims=True))).astype(o_ref.dtype)
        _acc_rows(dg_ref, i, jnp.sum(d * f * r, axis=0, keepdims=True))

    return pl.pallas_call(
        body, name=name, grid=(T // tm,), in_specs=[_rows(tm, C), _rows(tm, C), _vec(C)],
        out_specs=[_rows(tm, C), _vec(C)],
        out_shape=[jax.ShapeDtypeStruct((T, C), out_dtype), jax.ShapeDtypeStruct((1, C), F32)],
        compiler_params=_cp(("arbitrary",)))(dh, f, g)


def ff_act(gu, name):
    T = gu.shape[0]
    tm = min(512, T)

    def body(gu_ref, o_ref):
        g = gu_ref[:, :FB].astype(F32)
        u = gu_ref[:, FB:].astype(F32)
        o_ref[...] = (g * _sig(g) * u).astype(o_ref.dtype)

    return pl.pallas_call(
        body, name=name, grid=(T // tm, NCHIP),
        in_specs=[pl.BlockSpec((tm, 2 * FB), lambda i, s: (i, s))],
        out_specs=pl.BlockSpec((tm, FB), lambda i, s: (i, s)),
        out_shape=jax.ShapeDtypeStruct((T, NCHIP * FB), BF16),
        compiler_params=_cp(("parallel", "parallel")))(gu)


def ff_act_bwd(da, gu, name):
    T = gu.shape[0]
    tm = min(512, T)

    def body(da_ref, gu_ref, o_ref):
        g = gu_ref[:, :FB].astype(F32)
        u = gu_ref[:, FB:].astype(F32)
        da = da_ref[...].astype(F32)
        s = _sig(g)
        o_ref[:, :FB] = (da * u * (s * (1.0 + g * (1.0 - s)))).astype(o_ref.dtype)
        o_ref[:, FB:] = (da * (g * s)).astype(o_ref.dtype)

    return pl.pallas_call(
        body, name=name, grid=(T // tm, NCHIP),
        in_specs=[pl.BlockSpec((tm, FB), lambda i, s: (i, s)), pl.BlockSpec((tm, 2 * FB), lambda i, s: (i, s))],
        out_specs=pl.BlockSpec((tm, 2 * FB), lambda i, s: (i, s)),
        out_shape=jax.ShapeDtypeStruct((T, NCHIP * 2 * FB), BF16),
        compiler_params=_cp(("parallel", "parallel")))(da, gu)


def ple_post(h, zg, pe, g, name, g_next=None):
    T, C = h.shape
    tm = min(512, T)
    more = g_next is not None

    def body(h_ref, z_ref, p_ref, g_ref, *rest):
        e = p_ref[...] * _sig(z_ref[...])
        r = lax.rsqrt(jnp.mean(e * e, axis=-1, keepdims=True) + EPS)
        hn = h_ref[...] + e * r * g_ref[...]
        if more:
            gn_ref, o_ref, xn_ref = rest
            rn = lax.rsqrt(jnp.mean(hn * hn, axis=-1, keepdims=True) + EPS)
            xn_ref[...] = (hn * rn * gn_ref[...]).astype(xn_ref.dtype)
        else:
            (o_ref,) = rest
        o_ref[...] = hn

    return pl.pallas_call(
        body, name=name, grid=(T // tm,),
        in_specs=[_rows(tm, C), _rows(tm, C), _rows(tm, C), _vec(C)] + ([_vec(C)] if more else []),
        out_specs=[_rows(tm, C)] + ([_rows(tm, C)] if more else []),
        out_shape=[jax.ShapeDtypeStruct((T, C), F32)] + ([jax.ShapeDtypeStruct((T, C), BF16)] if more else []),
        compiler_params=_cp(("parallel",)))(*((h, zg, pe, g) + ((g_next,) if more else ())))


def ple_post_bwd(dh, zg, pe, g, name):
    T, C = dh.shape
    tm = min(512, T)

    def body(d_ref, z_ref, p_ref, g_ref, dz_ref, dp_ref, dg_ref):
        i = pl.program_id(0)
        s = _sig(z_ref[...])
        pe_ = p_ref[...]
        e = pe_ * s
        r = lax.rsqrt(jnp.mean(e * e, axis=-1, keepdims=True) + EPS)
        d = d_ref[...]
        t = d * g_ref[...]
        de = r * t - e * (r * r * r * jnp.mean(t * e, axis=-1, keepdims=True))
        dp_ref[...] = (de * s).astype(dp_ref.dtype)
        dz_ref[...] = (de * pe_ * s * (1.0 - s)).astype(dz_ref.dtype)
        _acc_rows(dg_ref, i, jnp.sum(d * e * r, axis=0, keepdims=True))

    return pl.pallas_call(
        body, name=name, grid=(T // tm,), in_specs=[_rows(tm, C), _rows(tm, C), _rows(tm, C), _vec(C)],
        out_specs=[_rows(tm, C), _rows(tm, C), _vec(C)],
        out_shape=[jax.ShapeDtypeStruct((T, C), BF16), jax.ShapeDtypeStruct((T, C), BF16),
                   jax.ShapeDtypeStruct((1, C), F32)],
        compiler_params=_cp(("arbitrary",)))(dh, zg, pe, g)


PLE_W = 256


def ple_fwd(h, xn, pb, GB, goff, GC, g, name, g_next=None):
    T, C = h.shape
    tm = min(512, T)
    more = g_next is not None

    def body(h_ref, x_ref, p_ref, wg_ref, wp_ref, g_ref, *rest):
        if more:
            gn_ref, z_ref, pe_ref, o_ref, xn_ref = rest
        else:
            z_ref, pe_ref, o_ref = rest
        zg = _dot_nn(x_ref[:, :PLE_W], wg_ref[0])
        for s in range(1, NCHIP):
            zg = zg + _dot_nn(x_ref[:, s * PLE_W:(s + 1) * PLE_W], wg_ref[s])
        z_ref[...] = zg
        for s in range(NCHIP):
            pe_ref[:, s * PLE_W:(s + 1) * PLE_W] = _dot_nn(p_ref[...], wp_ref[s])
        e = pe_ref[...] * _sig(zg)
        r = lax.rsqrt(jnp.mean(e * e, axis=-1, keepdims=True) + EPS)
        hn = h_ref[...] + e * r * g_ref[...]
        o_ref[...] = hn
        if more:
            rn = lax.rsqrt(jnp.mean(hn * hn, axis=-1, keepdims=True) + EPS)
            xn_ref[...] = (hn * rn * gn_ref[...]).astype(xn_ref.dtype)

    return pl.pallas_call(
        body, name=name, grid=(T // tm,),
        in_specs=[_rows(tm, C), _rows(tm, C), _rows(tm, PLE_W),
                  pl.BlockSpec((NCHIP, PLE_W, C), lambda i: (0, goff // PLE_W, 0)),
                  pl.BlockSpec((NCHIP, PLE_W, PLE_W), lambda i: (0, 0, 0)), _vec(C)] + ([_vec(C)] if more else []),
        out_specs=[_rows(tm, C)] * (4 if more else 3),
        out_shape=[jax.ShapeDtypeStruct((T, C), F32)] * 3 + ([jax.ShapeDtypeStruct((T, C), BF16)] if more else []),
        compiler_params=_cp(("parallel",)))(*((h, xn, pb, GB, GC, g) + ((g_next,) if more else ())))


def ple_bwd(dh, zg, pe, h_in, GB, goff, g, g_pre, name):
    T, C = dh.shape
    tm = min(512, T)

    def body(d_ref, z_ref, p_ref, h_ref, wg_ref, g_ref, gp_ref, dz_ref, dp_ref, o_ref, dg_ref, dgp_ref, dxn):
        i = pl.program_id(0)
        s = _sig(z_ref[...])
        pe_ = p_ref[...]
        e = pe_ * s
        r = lax.rsqrt(jnp.mean(e * e, axis=-1, keepdims=True) + EPS)
        d = d_ref[...]
        t = d * g_ref[...]
        de = r * t - e * (r * r * r * jnp.mean(t * e, axis=-1, keepdims=True))
        dp_ref[...] = (de * s).astype(dp_ref.dtype)
        dz = (de * pe_ * s * (1.0 - s)).astype(dz_ref.dtype)
        dz_ref[...] = dz
        _acc_rows(dg_ref, i, jnp.sum(d * e * r, axis=0, keepdims=True))
        for k in range(NCHIP):
            dxn[:, k * PLE_W:(k + 1) * PLE_W] = _dot_nt(dz, wg_ref[k])
        x = h_ref[...]
        rx = lax.rsqrt(jnp.mean(x * x, axis=-1, keepdims=True) + EPS)
        xh = x * rx
        dx = dxn[...]
        tx = dx * gp_ref[...]
        o_ref[...] = d + rx * (tx - xh * jnp.mean(tx * xh, axis=-1, keepdims=True))
        _acc_rows(dgp_ref, i, jnp.sum(dx * xh, axis=0, keepdims=True))

    return pl.pallas_call(
        body, name=name, grid=(T // tm,),
        in_specs=[_rows(tm, C), _rows(tm, C), _rows(tm, C), _rows(tm, C),
                  pl.BlockSpec((NCHIP, PLE_W, C), lambda i: (0, goff // PLE_W, 0)), _vec(C), _vec(C)],
        out_specs=[_rows(tm, C), _rows(tm, C), _rows(tm, C), _vec(C), _vec(C)],
        out_shape=[jax.ShapeDtypeStruct((T, C), BF16), jax.ShapeDtypeStruct((T, C), BF16),
                   jax.ShapeDtypeStruct((T, C), F32), jax.ShapeDtypeStruct((1, C), F32),
                   jax.ShapeDtypeStruct((1, C), F32)],
        scratch_shapes=[pltpu.VMEM((tm, C), F32)],
        compiler_params=_cp(("arbitrary",)))(dh, zg, pe, h_in, GB, g, g_pre)


def loss_head(h, tgt, name):
    T, C = h.shape
    tm = min(512, T)

    def body(h_ref, t_ref, d_ref, l_ref):
        i = pl.program_id(0)
        e = h_ref[...] - t_ref[...]
        d_ref[...] = e * (1.0 / C)
        _acc_rows(l_ref, i, jnp.sum(e * e, axis=0, keepdims=True))

    return pl.pallas_call(
        body, name=name, grid=(T // tm,), in_specs=[_rows(tm, C), _rows(tm, C)],
        out_specs=[_rows(tm, C), _vec(C)],
        out_shape=[jax.ShapeDtypeStruct((T, C), F32), jax.ShapeDtypeStruct((1, C), F32)],
        compiler_params=_cp(("arbitrary",)))(h, tgt)


AH, AG_N, AGW, CHUNK = 3072, 12, 256, 128


def _tril_bf16(w):
    r = lax.broadcasted_iota(jnp.int32, (CHUNK, CHUNK), 0)
    c = lax.broadcasted_iota(jnp.int32, (CHUNK, CHUNK), 1)
    return jnp.where(r >= c, w, 0.0).astype(BF16)


def _ln_stats(vs_ref, width):
    v = vs_ref[...]
    mu = jnp.sum(v, axis=-1, keepdims=True) * (1.0 / width)
    vc = v - mu
    var = jnp.sum(vc * vc, axis=-1, keepdims=True) * (1.0 / width)
    return mu, lax.rsqrt(var + EPS)


def gmlp_mid_fwd(zpre, vg, vb, ws, bsf, name):
    T = zpre.shape[0]

    def body(z_ref, vg_ref, vb_ref, ws_ref, bs_ref, y_ref, vs_ref):
        for g in range(AG_N):
            vs_ref[:, g * AGW:(g + 1) * AGW] = _gelu(z_ref[:, AH + g * AGW:AH + (g + 1) * AGW].astype(F32))
        mu, rstd = _ln_stats(vs_ref, AH)
        for g in range(AG_N):
            sl = slice(g * AGW, (g + 1) * AGW)
            vn = ((vs_ref[:, sl] - mu) * rstd * vg_ref[:, sl] + vb_ref[:, sl]).astype(BF16)
            sv = _dot_nn(_tril_bf16(ws_ref[g]), vn) + bs_ref[g]
            u = _gelu(z_ref[:, sl].astype(F32))
            y_ref[:, sl] = (u * sv).astype(y_ref.dtype)

    return pl.pallas_call(
        body, name=name, grid=(T // CHUNK,),
        in_specs=[_rows(CHUNK, 2 * AH), _vec(AH), _vec(AH),
                  pl.BlockSpec((AG_N, CHUNK, CHUNK), lambda i: (0, 0, 0)),
                  pl.BlockSpec((AG_N, CHUNK, AGW), lambda i: (0, 0, 0))],
        out_specs=_rows(CHUNK, AH), out_shape=jax.ShapeDtypeStruct((T, AH), BF16),
        scratch_shapes=[pltpu.VMEM((CHUNK, AH), F32)],
        compiler_params=_cp(("parallel",)))(zpre, vg, vb, ws, bsf)


def gmlp_mid_bwd(zpre, dy, vg, vb, ws, bsf, name):
    T = zpre.shape[0]

    def body(z_ref, dy_ref, vg_ref, vb_ref, ws_ref, bs_ref, dz_ref, dws_ref, dbs_ref, dvg_ref, dvb_ref,
             vs_ref, dvn_ref):
        i = pl.program_id(0)

        @pl.when(i == 0)
        def _():
            dws_ref[...] = jnp.zeros_like(dws_ref)
            dbs_ref[...] = jnp.zeros_like(dbs_ref)
            dvg_ref[...] = jnp.zeros_like(dvg_ref)
            dvb_ref[...] = jnp.zeros_like(dvb_ref)

        for g in range(AG_N):
            vs_ref[:, g * AGW:(g + 1) * AGW] = _gelu(z_ref[:, AH + g * AGW:AH + (g + 1) * AGW].astype(F32))
        mu, rstd = _ln_stats(vs_ref, AH)
        r_i = lax.broadcasted_iota(jnp.int32, (CHUNK, CHUNK), 0)
        c_i = lax.broadcasted_iota(jnp.int32, (CHUNK, CHUNK), 1)
        ones8 = jnp.ones((8, AGW), F32)
        m1 = jnp.zeros((CHUNK, 1), F32)
        m2 = jnp.zeros((CHUNK, 1), F32)
        for g in range(AG_N):
            sl = slice(g * AGW, (g + 1) * AGW)
            vh = (vs_ref[:, sl] - mu) * rstd
            vn = (vh * vg_ref[:, sl] + vb_ref[:, sl]).astype(BF16)
            wm = _tril_bf16(ws_ref[g])
            sv = _dot_nn(wm, vn) + bs_ref[g]
            zu = z_ref[:, sl].astype(F32)
            u = _gelu(zu)
            dyg = dy_ref[:, sl].astype(F32)
            dz_ref[:, sl] = (dyg * sv * _gelu_grad(zu)).astype(dz_ref.dtype)
            dsv = dyg * u
            dsv_b = dsv.astype(BF16)
            dws_ref[g] += jnp.where(r_i >= c_i, _dot_nt(dsv_b, vn), 0.0)
            dbs_ref[g] += _dot_nt(ones8, dsv)
            dvn = _dot_tn(wm, dsv_b)
            dvn_ref[:, sl] = dvn
            dvh = dvn * vg_ref[:, sl]
            m1 = m1 + jnp.sum(dvh, axis=-1, keepdims=True)
            m2 = m2 + jnp.sum(dvh * vh, axis=-1, keepdims=True)
            dvg_ref[:, sl] += jnp.sum(dvn * vh, axis=0, keepdims=True)
            dvb_ref[:, sl] += jnp.sum(dvn, axis=0, keepdims=True)
        m1 = m1 * (1.0 / AH)
        m2 = m2 * (1.0 / AH)
        for g in range(AG_N):
            sl = slice(g * AGW, (g + 1) * AGW)
            vh = (vs_ref[:, sl] - mu) * rstd
            dv = rstd * (dvn_ref[:, sl] * vg_ref[:, sl] - m1 - vh * m2)
            zv = z_ref[:, AH + g * AGW:AH + (g + 1) * AGW].astype(F32)
            dz_ref[:, AH + g * AGW:AH + (g + 1) * AGW] = (dv * _gelu_grad(zv)).astype(dz_ref.dtype)

    full3 = lambda a, b, c: pl.BlockSpec((a, b, c), lambda i: (0, 0, 0))
    return pl.pallas_call(
        body, name=name, grid=(T // CHUNK,),
        in_specs=[_rows(CHUNK, 2 * AH), _rows(CHUNK, AH), _vec(AH), _vec(AH),
                  full3(AG_N, CHUNK, CHUNK), full3(AG_N, CHUNK, AGW)],
        out_specs=[_rows(CHUNK, 2 * AH), full3(AG_N, CHUNK, CHUNK), full3(AG_N, 8, CHUNK), _vec(AH), _vec(AH)],
        out_shape=[jax.ShapeDtypeStruct((T, 2 * AH), BF16), jax.ShapeDtypeStruct((AG_N, CHUNK, CHUNK), F32),
                   jax.ShapeDtypeStruct((AG_N, 8, CHUNK), F32), jax.ShapeDtypeStruct((1, AH), F32),
                   jax.ShapeDtypeStruct((1, AH), F32)],
        scratch_shapes=[pltpu.VMEM((CHUNK, AH), F32), pltpu.VMEM((CHUNK, AH), F32)],
        compiler_params=_cp(("arbitrary",)))(zpre, dy, vg, vb, ws, bsf)


SLAB = 256
NSLAB = DM // SLAB
RC = 256
PAD = 32


def _col(T, j):
    return pl.BlockSpec((T, SLAB), lambda c: (0, j * NSLAB + c))


def _chunks(T, fn):
    def step(i, carry):
        fn(pl.multiple_of(i * RC, RC))
        return carry
    lax.fori_loop(0, T // RC, step, 0)


def _conv_taps(K):
    return [(r, [q for q in range(4) if 8 * q + r < K]) for r in range(min(8, K))]


def _causal_conv(zpad_ref, wrow, K, r0):
    acc = None
    for r, qs in _conv_taps(K):
        a = None
        for q in qs:
            term = wrow(8 * q + r) * zpad_ref[pl.ds(r0 + (PAD - 8 - 8 * q), RC + 8), :]
            a = term if a is None else a + term
        a = a if r == 0 else pltpu.roll(a, r, 0)
        acc = a if acc is None else acc + a
    return acc[8:, :]


def _anticausal_conv(gpad_ref, wrow, K, r0):
    acc = None
    for r, qs in _conv_taps(K):
        b = None
        for q in qs:
            term = wrow(8 * q + r) * gpad_ref[pl.ds(r0 + 8 * q, RC + 8), :]
            b = term if b is None else b + term
        b = b if r == 0 else pltpu.roll(b, RC + 8 - r, 0)
        acc = b if acc is None else acc + b
    return acc[:RC, :]


def _conv_dw(gpad_ref, zpad_ref, dw_ref, K, r0):
    for r, qs in _conv_taps(K):
        gw = gpad_ref[pl.ds(r0, RC + 8), :]
        p = (gw if r == 0 else pltpu.roll(gw, RC + 8 - r, 0))[:RC, :]
        for q in qs:
            z = zpad_ref[pl.ds(r0 + (PAD - 8 * q), RC), :]
            dw_ref[8 * q + r] += jnp.sum((p * z).reshape(RC // 8, 8, SLAB), axis=0)


def _zero_rows(ref, start, n):
    ref[pl.ds(start, n), :] = jnp.zeros((n, SLAB), F32)


def pool_fwd(hn, wg, sc, name):
    T = hn.shape[0]

    def body(h_ref, w_ref, s_ref, p_ref, yp_ref, y_ref, xpad):
        g = pl.program_id(0)
        wf = jnp.left_shift(2, g).astype(F32)
        _zero_rows(xpad, 0, PAD)

        def fill(r0):
            xpad[pl.ds(r0 + PAD, RC), :] = h_ref[pl.ds(r0, RC), :]
        _chunks(T, fill)

        def step(r0):
            w = xpad[pl.ds(r0 + (PAD - 16), RC + 16), :]
            s2 = w + pltpu.roll(w, 1, 0)
            s4 = s2 + pltpu.roll(s2, 2, 0)
            s8 = s4 + pltpu.roll(s4, 4, 0)
            s16 = s8 + pltpu.roll(s8, 8, 0)
            sel = jnp.where(g == 0, s2, jnp.where(g == 1, s4, jnp.where(g == 2, s8, s16)))[16:, :]
            t1 = (r0 + 1 + lax.broadcasted_iota(jnp.int32, (RC, SLAB), 0)).astype(F32)
            pooled = (sel / jnp.minimum(t1, wf) - w[16:, :]).astype(BF16)
            p_ref[pl.ds(r0, RC), :] = pooled
            yp = _dot_nn(pooled, w_ref[...])
            yp_ref[pl.ds(r0, RC), :] = yp
            y_ref[pl.ds(r0, RC), :] = yp * s_ref[...]
        _chunks(T, step)

    slab = pl.BlockSpec((T, SLAB), lambda c: (0, c))
    return pl.pallas_call(
        body, name=name, grid=(NSLAB,),
        in_specs=[slab, pl.BlockSpec((None, SLAB, SLAB), lambda c: (c, 0, 0)), pl.BlockSpec((1, SLAB), lambda c: (0, c))],
        out_specs=[slab, slab, slab],
        out_shape=[jax.ShapeDtypeStruct((T, DM), BF16), jax.ShapeDtypeStruct((T, DM), F32),
                   jax.ShapeDtypeStruct((T, DM), F32)],
        scratch_shapes=[pltpu.VMEM((T + PAD, SLAB), F32)],
        compiler_params=_cp(("parallel",)))(hn, wg, sc)


def pool_bwd(dy, ypre, pooled, wg, sc, name):
    T = dy.shape[0]

    def body(d_ref, yp_ref, p_ref, w_ref, s_ref, dh_ref, dw_ref, ds_ref, qpad, dwacc, dsacc):
        g = pl.program_id(0)
        wf = jnp.left_shift(2, g).astype(F32)
        dwacc[...] = jnp.zeros_like(dwacc)
        dsacc[...] = jnp.zeros_like(dsacc)
        _zero_rows(qpad, T, PAD)

        def first(r0):
            d = d_ref[pl.ds(r0, RC), :]
            dsacc[...] += jnp.sum((d * yp_ref[pl.ds(r0, RC), :]).reshape(RC // 8, 8, SLAB), axis=0)
            dyp = (d * s_ref[...]).astype(BF16)
            dpool = _dot_nt(dyp, w_ref[...])
            dwacc[...] += _dot_tn(p_ref[pl.ds(r0, RC), :], dyp)
            t1 = (r0 + 1 + lax.broadcasted_iota(jnp.int32, (RC, SLAB), 0)).astype(F32)
            qpad[pl.ds(r0, RC), :] = dpool / jnp.minimum(t1, wf)
            dh_ref[pl.ds(r0, RC), :] = dpool
        _chunks(T, first)

        def second(r0):
            w = qpad[pl.ds(r0, RC + 16), :]
            n = RC + 16
            a2 = w + pltpu.roll(w, n - 1, 0)
            a4 = a2 + pltpu.roll(a2, n - 2, 0)
            a8 = a4 + pltpu.roll(a4, n - 4, 0)
            a16 = a8 + pltpu.roll(a8, n - 8, 0)
            sel = jnp.where(g == 0, a2, jnp.where(g == 1, a4, jnp.where(g == 2, a8, a16)))[:RC, :]
            dh_ref[pl.ds(r0, RC), :] = sel - dh_ref[pl.ds(r0, RC), :]
        _chunks(T, second)
        dw_ref[...] = dwacc[...]
        ds_ref[...] = jnp.sum(dsacc[...], axis=0, keepdims=True)

    slab = pl.BlockSpec((T, SLAB), lambda c: (0, c))
    wspec = pl.BlockSpec((None, SLAB, SLAB), lambda c: (c, 0, 0))
    vec = pl.BlockSpec((1, SLAB), lambda c: (0, c))
    return pl.pallas_call(
        body, name=name, grid=(NSLAB,),
        in_specs=[slab, slab, slab, wspec, vec],
        out_specs=[slab, wspec, vec],
        out_shape=[jax.ShapeDtypeStruct((T, DM), F32), jax.ShapeDtypeStruct((NSLAB, SLAB, SLAB), F32),
                   jax.ShapeDtypeStruct((1, DM), F32)],
        scratch_shapes=[pltpu.VMEM((T + PAD, SLAB), F32), pltpu.VMEM((SLAB, SLAB), F32), pltpu.VMEM((8, SLAB), F32)],
        compiler_params=_cp(("parallel",)))(dy, ypre, pooled, wg, sc)


KC = 31
KD = 3


def conf_conv_fwd(ag, wdw, bdw, name):
    T = ag.shape[0]

    def body(a_ref, g_ref, w_ref, b_ref, o_ref, zpad):
        _zero_rows(zpad, 0, PAD)

        def fill(r0):
            a = a_ref[pl.ds(r0, RC), :].astype(F32)
            gt = g_ref[pl.ds(r0, RC), :].astype(F32)
            zpad[pl.ds(r0 + PAD, RC), :] = a * _sig(gt)
        _chunks(T, fill)
        wrow = lambda j: w_ref[KC - 1 - j:KC - j, :]

        def step(r0):
            o_ref[pl.ds(r0, RC), :] = _causal_conv(zpad, wrow, KC, r0) + b_ref[...]
        _chunks(T, step)

    vec = pl.BlockSpec((1, SLAB), lambda c: (0, c))
    return pl.pallas_call(
        body, name=name, grid=(NSLAB,),
        in_specs=[_col(T, 0), _col(T, 1), pl.BlockSpec((32, SLAB), lambda c: (0, c)), vec],
        out_specs=pl.BlockSpec((T, SLAB), lambda c: (0, c)),
        out_shape=jax.ShapeDtypeStruct((T, DM), F32),
        scratch_shapes=[pltpu.VMEM((T + PAD, SLAB), F32)],
        compiler_params=_cp(("parallel",)))(ag, ag, wdw, bdw)


def conf_conv_bwd(dzc, ag, wdw, name):
    T = ag.shape[0]

    def body(d_ref, a_ref, g_ref, w_ref, da_ref, dg_ref, dw_ref, db_ref, zpad, gpad, dwacc, dbacc):
        _zero_rows(zpad, 0, PAD)
        _zero_rows(gpad, T, PAD)
        dwacc[...] = jnp.zeros_like(dwacc)
        dbacc[...] = jnp.zeros_like(dbacc)

        def fill(r0):
            a = a_ref[pl.ds(r0, RC), :].astype(F32)
            gt = g_ref[pl.ds(r0, RC), :].astype(F32)
            zpad[pl.ds(r0 + PAD, RC), :] = a * _sig(gt)
            d = d_ref[pl.ds(r0, RC), :]
            gpad[pl.ds(r0, RC), :] = d
            dbacc[...] += jnp.sum(d.reshape(RC // 8, 8, SLAB), axis=0)
        _chunks(T, fill)
        wrow = lambda j: w_ref[KC - 1 - j:KC - j, :]

        def step(r0):
            dz = _anticausal_conv(gpad, wrow, KC, r0)
            a = a_ref[pl.ds(r0, RC), :].astype(F32)
            s = _sig(g_ref[pl.ds(r0, RC), :].astype(F32))
            da_ref[pl.ds(r0, RC), :] = (dz * s).astype(da_ref.dtype)
            dg_ref[pl.ds(r0, RC), :] = (dz * a * s * (1.0 - s)).astype(dg_ref.dtype)
            _conv_dw(gpad, zpad, dwacc, KC, r0)
        _chunks(T, step)
        dw_ref[...] = jnp.zeros_like(dw_ref)
        for k in range(KC):
            dw_ref[k:k + 1, :] = jnp.sum(dwacc[KC - 1 - k], axis=0, keepdims=True)
        db_ref[...] = jnp.sum(dbacc[...], axis=0, keepdims=True)

    vec = pl.BlockSpec((1, SLAB), lambda c: (0, c))
    w32 = pl.BlockSpec((32, SLAB), lambda c: (0, c))
    return pl.pallas_call(
        body, name=name, grid=(NSLAB,),
        in_specs=[pl.BlockSpec((T, SLAB), lambda c: (0, c)), _col(T, 0), _col(T, 1), w32],
        out_specs=[_col(T, 0), _col(T, 0), w32, vec],
        out_shape=[jax.ShapeDtypeStruct((T, DM), BF16), jax.ShapeDtypeStruct((T, DM), BF16),
                   jax.ShapeDtypeStruct((32, DM), F32), jax.ShapeDtypeStruct((1, DM), F32)],
        scratch_shapes=[pltpu.VMEM((T + PAD, SLAB), F32), pltpu.VMEM((T + PAD, SLAB), F32),
                        pltpu.VMEM((32, 8, SLAB), F32), pltpu.VMEM((8, SLAB), F32)],
        compiler_params=_cp(("parallel",)))(dzc, ag, ag, wdw)


def conf_ln_fwd(zc, g, b, name):
    T, C = zc.shape
    tm = min(512, T)

    def body(z_ref, g_ref, b_ref, o_ref):
        x = z_ref[...]
        xc = x - jnp.mean(x, axis=-1, keepdims=True)
        r = lax.rsqrt(jnp.mean(xc * xc, axis=-1, keepdims=True) + EPS)
        zl = xc * r * g_ref[...] + b_ref[...]
        o_ref[...] = (zl * _sig(zl)).astype(o_ref.dtype)

    return pl.pallas_call(
        body, name=name, grid=(T // tm,), in_specs=[_rows(tm, C), _vec(C), _vec(C)], out_specs=_rows(tm, C),
        out_shape=jax.ShapeDtypeStruct((T, C), BF16), compiler_params=_cp(("parallel",)))(zc, g, b)


def conf_ln_bwd(dzs, zc, g, b, name):
    T, C = zc.shape
    tm = min(512, T)

    def body(d_ref, z_ref, g_ref, b_ref, o_ref, dg_ref, db_ref):
        i = pl.program_id(0)
        x = z_ref[...]
        xc = x - jnp.mean(x, axis=-1, keepdims=True)
        r = lax.rsqrt(jnp.mean(xc * xc, axis=-1, keepdims=True) + EPS)
        xh = xc * r
        zl = xh * g_ref[...] + b_ref[...]
        s = _sig(zl)
        dzl = d_ref[...].astype(F32) * (s * (1.0 + zl * (1.0 - s)))
        t = dzl * g_ref[...]
        o_ref[...] = r * (t - jnp.mean(t, axis=-1, keepdims=True) - xh * jnp.mean(t * xh, axis=-1, keepdims=True))
        _acc_rows(dg_ref, i, jnp.sum(dzl * xh, axis=0, keepdims=True))
        _acc_rows(db_ref, i, jnp.sum(dzl, axis=0, keepdims=True))

    return pl.pallas_call(
        body, name=name, grid=(T // tm,), in_specs=[_rows(tm, C), _rows(tm, C), _vec(C), _vec(C)],
        out_specs=[_rows(tm, C), _vec(C), _vec(C)],
        out_shape=[jax.ShapeDtypeStruct((T, C), F32), jax.ShapeDtypeStruct((1, C), F32),
                   jax.ShapeDtypeStruct((1, C), F32)],
        compiler_params=_cp(("arbitrary",)))(dzs, zc, g, b)


def sconv_fwd(bgx, wc, name):
    T = bgx.shape[0]

    def body(b_ref, c_ref, x_ref, w_ref, o_ref, zpad):
        _zero_rows(zpad, 0, PAD)

        def fill(r0):
            zpad[pl.ds(r0 + PAD, RC), :] = c_ref[pl.ds(r0, RC), :].astype(F32) * x_ref[pl.ds(r0, RC), :].astype(F32)
        _chunks(T, fill)
        wrow = lambda j: w_ref[KD - 1 - j:KD - j, :]

        def step(r0):
            qc = _causal_conv(zpad, wrow, KD, r0)
            o_ref[pl.ds(r0, RC), :] = (b_ref[pl.ds(r0, RC), :].astype(F32) * qc).astype(o_ref.dtype)
        _chunks(T, step)

    return pl.pallas_call(
        body, name=name, grid=(NSLAB,),
        in_specs=[_col(T, 0), _col(T, 1), _col(T, 2), pl.BlockSpec((8, SLAB), lambda c: (0, c))],
        out_specs=pl.BlockSpec((T, SLAB), lambda c: (0, c)),
        out_shape=jax.ShapeDtypeStruct((T, DM), BF16),
        scratch_shapes=[pltpu.VMEM((T + PAD, SLAB), F32)],
        compiler_params=_cp(("parallel",)))(bgx, bgx, bgx, wc)


def sconv_bwd(dy, bgx, wc, name):
    T = bgx.shape[0]

    def body(d_ref, b_ref, c_ref, x_ref, w_ref, db_ref, dc_ref, dx_ref, dw_ref, zpad, gpad, dwacc):
        _zero_rows(zpad, 0, PAD)
        _zero_rows(gpad, T, PAD)
        dwacc[...] = jnp.zeros_like(dwacc)

        def fill(r0):
            zpad[pl.ds(r0 + PAD, RC), :] = c_ref[pl.ds(r0, RC), :].astype(F32) * x_ref[pl.ds(r0, RC), :].astype(F32)
            gpad[pl.ds(r0, RC), :] = d_ref[pl.ds(r0, RC), :].astype(F32) * b_ref[pl.ds(r0, RC), :].astype(F32)
        _chunks(T, fill)
        wrow = lambda j: w_ref[KD - 1 - j:KD - j, :]

        def step(r0):
            qc = _causal_conv(zpad, wrow, KD, r0)
            db_ref[pl.ds(r0, RC), :] = (d_ref[pl.ds(r0, RC), :].astype(F32) * qc).astype(db_ref.dtype)
            dq = _anticausal_conv(gpad, wrow, KD, r0)
            dc_ref[pl.ds(r0, RC), :] = (dq * x_ref[pl.ds(r0, RC), :].astype(F32)).astype(dc_ref.dtype)
            dx_ref[pl.ds(r0, RC), :] = (dq * c_ref[pl.ds(r0, RC), :].astype(F32)).astype(dx_ref.dtype)
            _conv_dw(gpad, zpad, dwacc, KD, r0)
        _chunks(T, step)
        dw_ref[...] = jnp.zeros_like(dw_ref)
        for k in range(KD):
            dw_ref[k:k + 1, :] = jnp.sum(dwacc[KD - 1 - k], axis=0, keepdims=True)

    w8 = pl.BlockSpec((8, SLAB), lambda c: (0, c))
    return pl.pallas_call(
        body, name=name, grid=(NSLAB,),
        in_specs=[pl.BlockSpec((T, SLAB), lambda c: (0, c)), _col(T, 0), _col(T, 1), _col(T, 2), w8],
        out_specs=[_col(T, 0), _col(T, 0), _col(T, 0), w8],
        out_shape=[jax.ShapeDtypeStruct((T, DM), BF16)] * 3 + [jax.ShapeDtypeStruct((8, DM), F32)],
        scratch_shapes=[pltpu.VMEM((T + PAD, SLAB), F32), pltpu.VMEM((T + PAD, SLAB), F32),
                        pltpu.VMEM((8, 8, SLAB), F32)],
        compiler_params=_cp(("parallel",)))(dy, bgx, bgx, bgx, wc)


def merge_cols(parts, name):
    T = parts[0].shape[0]
    n = len(parts)
    C = n * DM
    tm = min(512, T)

    def body(*refs):
        o_ref = refs[n]
        for j in range(n):
            o_ref[:, j * DM:(j + 1) * DM] = refs[j][...]

    return pl.pallas_call(
        body, name=name, grid=(T // tm,),
        in_specs=[_rows(tm, DM) for j in range(n)],
        out_specs=_rows(tm, C), out_shape=jax.ShapeDtypeStruct((T, C), parts[0].dtype),
        compiler_params=_cp(("parallel",)))(*parts)


def adamw(w, g, m, v, name):
    shape = w.shape
    R, C = shape[-2], shape[-1]
    L = w.size // (R * C)
    w2, g2, m2, v2 = (a.reshape(L, R, C) for a in (w, g, m, v))
    tr = R
    while tr * C > 512 * 1024 and tr % 16 == 0:
        tr //= 2
    bc1 = 1.0 - ADAM_B1 ** ADAM_STEP
    bc2 = 1.0 - ADAM_B2 ** ADAM_STEP

    def body(w_ref, g_ref, m_ref, v_ref, d_ref, nm_ref, nv_ref):
        gg = g_ref[...]
        nm = ADAM_B1 * m_ref[...] + (1.0 - ADAM_B1) * gg
        nv = ADAM_B2 * v_ref[...] + (1.0 - ADAM_B2) * (gg * gg)
        nm_ref[...] = nm
        nv_ref[...] = nv
        d_ref[...] = -ADAM_LR * ((nm / bc1) / (jnp.sqrt(nv / bc2) + ADAM_EPS) + ADAM_WD * w_ref[...])

    spec = pl.BlockSpec((None, tr, C), lambda l, i: (l, i, 0))
    outs = pl.pallas_call(
        body, name=name, grid=(L, R // tr), in_specs=[spec] * 4, out_specs=[spec] * 3,
        out_shape=[jax.ShapeDtypeStruct((L, R, C), F32)] * 3,
        compiler_params=_cp(("parallel", "parallel")))(w2, g2, m2, v2)
    return tuple(o.reshape(shape) for o in outs)


def _place():
    x, y, c = lax.axis_index("x"), lax.axis_index("y"), lax.axis_index("c")
    return x, y, c


def all_gather_chips(bufs, name):
    n = len(bufs)
    me_ = 2 * lax.axis_index("x") + lax.axis_index("y")
    slots = [lax.dynamic_update_slice(lax.empty((NCHIP,) + b.shape, b.dtype), b[None], (me_, 0, 0)) for b in bufs]

    def body(*refs):
        dst = refs[n:2 * n]
        send, recv = refs[2 * n:]
        x, y, c = _place()
        me = 2 * x + y
        sib = (x, y, 1 - c)
        chips = [(1 - x, y), (x, 1 - y), (1 - x, 1 - y)]

        def half(b, slot, hc):
            rows = bufs[b].shape[0] // 2
            return dst[b].at[slot, pl.ds(hc * rows, rows), :]

        def remote(k, s, d, to):
            return pltpu.make_async_remote_copy(src_ref=s, dst_ref=d, send_sem=send.at[k], recv_sem=recv.at[k],
                                                device_id=to, device_id_type=MESH)

        first = []
        for b in range(n):
            for j, (cx, cy) in enumerate(chips):
                first.append(remote(b * 6 + j, half(b, me, c), half(b, me, c), (cx, cy, c)))
        for cp in first:
            cp.start()
        passed = []
        for b in range(n):
            for j, (cx, cy) in enumerate(chips):
                slot = 2 * cx + cy
                remote(b * 6 + j, half(b, slot, c), half(b, slot, c), (cx, cy, c)).wait_recv()
                fwd = remote(b * 6 + 3 + j, half(b, slot, c), half(b, slot, c), sib)
                fwd.start()
                passed.append(fwd)
        for b in range(n):
            for j, (cx, cy) in enumerate(chips):
                slot = 2 * cx + cy
                remote(b * 6 + 3 + j, half(b, slot, 1 - c), half(b, slot, 1 - c), sib).wait_recv()
        for cp in first + passed:
            cp.wait_send()

    return pl.pallas_call(
        body, name=name, in_specs=[ANY] * n, out_specs=[ANY] * n,
        out_shape=[jax.ShapeDtypeStruct(s.shape, s.dtype) for s in slots],
        input_output_aliases={b: b for b in range(n)},
        scratch_shapes=[pltpu.SemaphoreType.DMA((6 * n,)), pltpu.SemaphoreType.DMA((6 * n,))],
        compiler_params=pltpu.CompilerParams())(*slots)


def pair_exchange(bufs, name):
    n = len(bufs)

    def body(*refs):
        src, dst = refs[:n], refs[n:2 * n]
        send, recv = refs[2 * n:]
        x, y, c = _place()
        cps = []
        for b in range(n):
            rows = bufs[b].shape[1] // 2
            cp = pltpu.make_async_remote_copy(
                src_ref=src[b].at[:, pl.ds((1 - c) * rows, rows), :], dst_ref=dst[b],
                send_sem=send.at[b], recv_sem=recv.at[b], device_id=(x, y, 1 - c), device_id_type=MESH)
            cp.start()
            cps.append(cp)
        for cp in cps:
            cp.wait()

    return pl.pallas_call(
        body, name=name, in_specs=[ANY] * n, out_specs=[ANY] * n,
        out_shape=[jax.ShapeDtypeStruct((NCHIP, b.shape[1] // 2, b.shape[2]), b.dtype) for b in bufs],
        scratch_shapes=[pltpu.SemaphoreType.DMA((n,)), pltpu.SemaphoreType.DMA((n,))],
        compiler_params=pltpu.CompilerParams())(*bufs)


def add_half(full, got, tr, tc, name):
    _, R, C = full.shape
    rows = R // 2
    nr = rows // tr
    c_arr = lax.axis_index("c").astype(jnp.int32).reshape(1)

    def body(c_ref, a_ref, b_ref, o_ref):
        o_ref[...] = (a_ref[...].astype(F32) + b_ref[...].astype(F32)).astype(o_ref.dtype)

    return pl.pallas_call(
        body, name=name,
        grid_spec=pltpu.PrefetchScalarGridSpec(
            num_scalar_prefetch=1, grid=(NCHIP, nr, C // tc),
            in_specs=[pl.BlockSpec((None, tr, tc), lambda s, i, j, c_ref: (s, c_ref[0] * nr + i, j)),
                      pl.BlockSpec((None, tr, tc), lambda s, i, j, c_ref: (s, i, j))],
            out_specs=pl.BlockSpec((None, tr, tc), lambda s, i, j, c_ref: (s, i, j))),
        out_shape=jax.ShapeDtypeStruct((NCHIP, rows, C), full.dtype),
        compiler_params=_cp(("parallel", "parallel", "parallel")))(c_arr, full, got)


def chip_exchange(bufs, name):
    n = len(bufs)

    def body(*refs):
        src, dst = refs[:n], refs[n:2 * n]
        send, recv, lsem = refs[2 * n:]
        x, y, c = _place()
        me = 2 * x + y
        chips = [(1 - x, y), (x, 1 - y), (1 - x, 1 - y)]
        local = [pltpu.make_async_copy(src[b].at[me], dst[b].at[me], lsem.at[b]) for b in range(n)]
        for cp in local:
            cp.start()
        cps = []
        for b in range(n):
            for j, (cx, cy) in enumerate(chips):
                cp = pltpu.make_async_remote_copy(
                    src_ref=src[b].at[2 * cx + cy], dst_ref=dst[b].at[me],
                    send_sem=send.at[b * 3 + j], recv_sem=recv.at[b * 3 + j],
                    device_id=(cx, cy, c), device_id_type=MESH)
                cp.start()
                cps.append((cp, b, cx, cy, j))
        for cp, b, cx, cy, j in cps:
            cp.wait_send()
            pltpu.make_async_remote_copy(
                src_ref=src[b].at[me], dst_ref=dst[b].at[2 * cx + cy],
                send_sem=send.at[b * 3 + j], recv_sem=recv.at[b * 3 + j],
                device_id=(cx, cy, c), device_id_type=MESH).wait_recv()
        for cp in local:
            cp.wait()

    return pl.pallas_call(
        body, name=name, in_specs=[ANY] * n, out_specs=[ANY] * n,
        out_shape=[jax.ShapeDtypeStruct(b.shape, b.dtype) for b in bufs],
        scratch_shapes=[pltpu.SemaphoreType.DMA((3 * n,)), pltpu.SemaphoreType.DMA((3 * n,)),
                        pltpu.SemaphoreType.DMA((n,))],
        compiler_params=pltpu.CompilerParams())(*bufs)


def sum_slots(buf, tr, tc, name):
    _, r, C = buf.shape
    nr = r // tr
    c_arr = lax.axis_index("c").astype(jnp.int32).reshape(1)

    def body(c_ref, a_ref, o_ref):
        o_ref[...] = ((a_ref[0].astype(F32) + a_ref[1].astype(F32)) + a_ref[2].astype(F32)) + a_ref[3].astype(F32)

    return pl.pallas_call(
        body, name=name,
        grid_spec=pltpu.PrefetchScalarGridSpec(
            num_scalar_prefetch=1, grid=(nr, C // tc),
            in_specs=[pl.BlockSpec((NCHIP, tr, tc), lambda i, j, c_ref: (0, i, j))],
            out_specs=pl.BlockSpec((tr, tc), lambda i, j, c_ref: (c_ref[0] * nr + i, j))),
        out_shape=jax.ShapeDtypeStruct((2 * r, C), F32),
        compiler_params=_cp(("parallel", "parallel")))(c_arr, buf)


def pair_share(bufs, name):
    n = len(bufs)

    def body(*refs):
        dst = refs[n:2 * n]
        send, recv = refs[2 * n:]
        x, y, c = _place()
        cps = []
        for b in range(n):
            rows = bufs[b].shape[0] // 2
            here = dst[b].at[pl.ds(c * rows, rows), :]
            cp = pltpu.make_async_remote_copy(src_ref=here, dst_ref=here, send_sem=send.at[b], recv_sem=recv.at[b],
                                              device_id=(x, y, 1 - c), device_id_type=MESH)
            cp.start()
            cps.append((cp, b))
        for cp, b in cps:
            rows = bufs[b].shape[0] // 2
            there = dst[b].at[pl.ds((1 - c) * rows, rows), :]
            cp.wait_send()
            pltpu.make_async_remote_copy(src_ref=there, dst_ref=there, send_sem=send.at[b], recv_sem=recv.at[b],
                                         device_id=(x, y, 1 - c), device_id_type=MESH).wait_recv()

    return pl.pallas_call(
        body, name=name, in_specs=[ANY] * n, out_specs=[ANY] * n,
        out_shape=[jax.ShapeDtypeStruct(b.shape, b.dtype) for b in bufs],
        input_output_aliases={b: b for b in range(n)},
        scratch_shapes=[pltpu.SemaphoreType.DMA((n,)), pltpu.SemaphoreType.DMA((n,))],
        compiler_params=pltpu.CompilerParams())(*bufs)


def _tile(kind, buf):
    return {"A": (256, buf.shape[2]), "B": (buf.shape[1], 1024), "C": (128, 256), "V": (40, 256),
            "E": (40, 1024)}[kind]


def reduce_scatter(parts, tag):
    names = list(parts)
    got = pair_exchange([parts[k] for k in names], tag + "_pair_exchange")
    sums = [add_half(parts[k], got[i], *_tile(k[0], got[i]), name=tag + "_add_pair_" + k) for i, k in enumerate(names)]
    landed = chip_exchange(sums, tag + "_chip_exchange")
    halves = [sum_slots(landed[i], *_tile(k[0], landed[i]), name=tag + "_sum_chips_" + k) for i, k in enumerate(names)]
    full = pair_share(halves, tag + "_pair_share")
    return dict(zip(names, full))


HBM = pl.BlockSpec(memory_space=pltpu.HBM)
SEMS = pl.BlockSpec(memory_space=pltpu.SEMAPHORE)
FLOWS = pltpu.SideEffectType.DATAFLOW_SIDE_EFFECTING


def _in_hbm(a):
    return pltpu.with_memory_space_constraint(a, pltpu.HBM)


def _other_chips():
    x, y, c = _place()
    return 2 * x + y, c, [(1 - x, y), (x, 1 - y), (1 - x, 1 - y)]


def own_slots(bufs):
    me = 2 * lax.axis_index("x") + lax.axis_index("y")
    return [lax.dynamic_update_slice(lax.empty((NCHIP,) + b.shape, b.dtype), b[None], (me, 0, 0)) for b in bufs]


def gather_start(slots, after, name):
    n = len(slots)

    def body(*refs):
        ins = refs[:n]
        send, recv = refs[n + 1], refs[n + 2]
        token = refs[2 * n + 3]
        me, c, chips = _other_chips()
        for b in range(n):
            rows = slots[b].shape[1] // 2
            own = ins[b].at[me, pl.ds(c * rows, rows), :]
            for j, (cx, cy) in enumerate(chips):
                pltpu.make_async_remote_copy(src_ref=own, dst_ref=own, send_sem=send.at[3 * b + j],
                                             recv_sem=recv.at[3 * b + j], device_id=(cx, cy, c),
                                             device_id_type=MESH).start()
        token[...] = jnp.zeros_like(token)

    out = pl.pallas_call(
        body, name=name, in_specs=[HBM] * n + [ANY],
        out_specs=[SEMS, SEMS] + [HBM] * n + [pl.BlockSpec(memory_space=pltpu.VMEM)],
        out_shape=[pltpu.SemaphoreType.DMA((3 * n,)), pltpu.SemaphoreType.DMA((3 * n,))]
        + [pltpu.HBM(s.shape, s.dtype) for s in slots] + [jax.ShapeDtypeStruct((8, 128), F32)],
        input_output_aliases={b: b + 2 for b in range(n)},
        compiler_params=pltpu.CompilerParams(has_side_effects=FLOWS))(*[_in_hbm(s) for s in slots], after)
    return out[0], out[1], list(out[2:2 + n]), out[2 + n]


def gather_wait(send, recv, slots, picks, after, name):
    n = len(slots)

    def body(*refs):
        ins = refs[:n]
        send_, recv_ = refs[n], refs[n + 1]
        me, c, chips = _other_chips()
        for i, b in enumerate(picks):
            rows = slots[i].shape[1] // 2
            own = ins[i].at[me, pl.ds(c * rows, rows), :]
            for j, (cx, cy) in enumerate(chips):
                got = ins[i].at[2 * cx + cy, pl.ds(c * rows, rows), :]
                pltpu.make_async_remote_copy(src_ref=own, dst_ref=own, send_sem=send_.at[3 * b + j],
                                             recv_sem=recv_.at[3 * b + j], device_id=(cx, cy, c),
                                             device_id_type=MESH).wait_send()
                pltpu.make_async_remote_copy(src_ref=got, dst_ref=got, send_sem=send_.at[3 * b + j],
                                             recv_sem=recv_.at[3 * b + j], device_id=(cx, cy, c),
                                             device_id_type=MESH).wait_recv()

    return pl.pallas_call(
        body, name=name, in_specs=[HBM] * n + [SEMS, SEMS, ANY], out_specs=[HBM] * n,
        out_shape=[pltpu.HBM(s.shape, s.dtype) for s in slots],
        input_output_aliases={b: b for b in range(n)},
        compiler_params=pltpu.CompilerParams(has_side_effects=FLOWS))(*slots, send, recv, after)


def gather_pass(slots, name):
    n = len(slots)

    def body(*refs):
        dst = refs[n:2 * n]
        send, recv = refs[2 * n:]
        x, y, c = _place()
        sib = (x, y, 1 - c)
        chips = [(1 - x, y), (x, 1 - y), (1 - x, 1 - y)]

        def half(b, slot, hc):
            rows = slots[b].shape[1] // 2
            return dst[b].at[slot, pl.ds(hc * rows, rows), :]

        passed = []
        for b in range(n):
            for j, (cx, cy) in enumerate(chips):
                slot = 2 * cx + cy
                cp = pltpu.make_async_remote_copy(src_ref=half(b, slot, c), dst_ref=half(b, slot, c),
                                                  send_sem=send.at[3 * b + j], recv_sem=recv.at[3 * b + j],
                                                  device_id=sib, device_id_type=MESH)
                cp.start()
                passed.append(cp)
        for b in range(n):
            for j, (cx, cy) in enumerate(chips):
                slot = 2 * cx + cy
                pltpu.make_async_remote_copy(src_ref=half(b, slot, 1 - c), dst_ref=half(b, slot, 1 - c),
                                             send_sem=send.at[3 * b + j], recv_sem=recv.at[3 * b + j],
                                             device_id=sib, device_id_type=MESH).wait_recv()
        for cp in passed:
            cp.wait_send()

    return pl.pallas_call(
        body, name=name, in_specs=[ANY] * n, out_specs=[ANY] * n,
        out_shape=[jax.ShapeDtypeStruct(s.shape, s.dtype) for s in slots],
        input_output_aliases={b: b for b in range(n)},
        scratch_shapes=[pltpu.SemaphoreType.DMA((3 * n,)), pltpu.SemaphoreType.DMA((3 * n,))],
        compiler_params=pltpu.CompilerParams())(*slots)


def chip_exchange_start(sums, name):
    n = len(sums)
    me_ = 2 * lax.axis_index("x") + lax.axis_index("y")
    landing = [lax.dynamic_update_slice(lax.empty(s.shape, s.dtype),
                                        lax.dynamic_slice(s, (me_, 0, 0), (1,) + s.shape[1:]), (me_, 0, 0)) for s in sums]

    def body(*refs):
        src, land = refs[:n], refs[n:2 * n]
        send, recv = refs[2 * n], refs[2 * n + 1]
        token = refs[4 * n + 2]
        me, c, chips = _other_chips()
        for b in range(n):
            for j, (cx, cy) in enumerate(chips):
                pltpu.make_async_remote_copy(src_ref=src[b].at[2 * cx + cy], dst_ref=land[b].at[me],
                                             send_sem=send.at[3 * b + j], recv_sem=recv.at[3 * b + j],
                                             device_id=(cx, cy, c), device_id_type=MESH).start()
        token[...] = jnp.zeros_like(token)

    out = pl.pallas_call(
        body, name=name, in_specs=[HBM] * (2 * n),
        out_specs=[SEMS, SEMS] + [HBM] * (2 * n) + [pl.BlockSpec(memory_space=pltpu.VMEM)],
        out_shape=[pltpu.SemaphoreType.DMA((3 * n,)), pltpu.SemaphoreType.DMA((3 * n,))]
        + [pltpu.HBM(s.shape, s.dtype) for s in sums + landing] + [jax.ShapeDtypeStruct((8, 128), F32)],
        input_output_aliases={b: b + 2 for b in range(2 * n)},
        compiler_params=pltpu.CompilerParams(has_side_effects=FLOWS))(*[_in_hbm(s) for s in sums + landing])
    return out[0], out[1], list(out[2:2 + n]), list(out[2 + n:2 + 2 * n]), out[2 + 2 * n]


def chip_exchange_wait(send, recv, sums, landing, after, name):
    n = len(sums)

    def body(*refs):
        src, land = refs[:n], refs[n:2 * n]
        send_, recv_ = refs[2 * n], refs[2 * n + 1]
        me, c, chips = _other_chips()
        for b in range(n):
            for j, (cx, cy) in enumerate(chips):
                slot = 2 * cx + cy
                pltpu.make_async_remote_copy(src_ref=src[b].at[slot], dst_ref=land[b].at[me],
                                             send_sem=send_.at[3 * b + j], recv_sem=recv_.at[3 * b + j],
                                             device_id=(cx, cy, c), device_id_type=MESH).wait_send()
                pltpu.make_async_remote_copy(src_ref=src[b].at[me], dst_ref=land[b].at[slot],
                                             send_sem=send_.at[3 * b + j], recv_sem=recv_.at[3 * b + j],
                                             device_id=(cx, cy, c), device_id_type=MESH).wait_recv()

    out = pl.pallas_call(
        body, name=name, in_specs=[HBM] * (2 * n) + [SEMS, SEMS, ANY], out_specs=[HBM] * (2 * n),
        out_shape=[pltpu.HBM(s.shape, s.dtype) for s in sums + landing],
        input_output_aliases={b: b for b in range(2 * n)},
        compiler_params=pltpu.CompilerParams(has_side_effects=FLOWS))(*sums, *landing, send, recv, after)
    return list(out[n:])


def _row(a, l):
    return a[l:l + 1]


def local_step(x, p, tgt, small, weights_of, vecs, a_ws, a_bs, grads_ready):
    T = x.shape[0]
    bsf = jnp.broadcast_to(a_bs[:, :, None], (AG_N, CHUNK, AGW))
    vrow = lambda r: vecs[r:r + 1]
    saved = []
    W = []
    h = x
    GA1 = GB1 = GA = GB = GC = bgrp = None

    def ff_fwd(h, xn, l, which, post, g_next, tok=None):
        wa, wb = (GA1, GB1) if which == 1 else (GA, GB)
        tag = "ff%d_l%d" % (which, l)
        gu, a = ff_gateup(xn, wa, 0, tag + "_gateup")
        gp = _row(post, l) if tok is None else _row(post, l) + tok
        out = mm_rs_post(a, wb, 0, FB, FB, h, gp, 0.5, tag + "_down", g_next=g_next)
        return out[1], (out[2] if g_next is not None else None), (h, xn, gu, a, out[0])

    xn = rms_fwd(h, _row(small["ff1_pre_g"], 0), BF16, "ff1_l0_pre")
    for l in range(4):
        rec = {}
        GA1, GB1, atok = weights_of(l, "a", h)
        g_mix = _row(small["mix_pre_g"], l) if l >= 2 else None
        h, hn, rec["ff1"] = ff_fwd(h, xn, l, 1, small["ff1_post_g"], g_mix, atok)
        GA, GB, GC, wtok = weights_of(l, "b", h)
        W.append((GA1, GB1, GA, GB, GC))
        if l == 1:
            bgrp = GC[:, C_BGRP:C_BGRP + 256, :].reshape(NCHIP, 4, 64, 256).transpose(1, 0, 2, 3).reshape(4, 256, 256)
        tag = "mix_l%d" % l
        h_in = h
        g_ff2 = _row(small["ff2_pre_g"], l)
        if l == 1:
            hn = rms_fwd(h, _row(small["mix_pre_g"], l), F32, tag + "_pre")
            pooled, ypre, f = pool_fwd(hn, bgrp, vrow(V_BSCALE), tag + "_pool")
            rec["mix"] = (h_in, pooled, ypre, f)
            h = post_res(h, f, _row(small["mix_post_g"], l), 1.0, tag + "_post")
            xn = rms_fwd(h, g_ff2, BF16, "ff2_l1_pre")
        else:
            gpost = _row(small["mix_post_g"], l)
            if l == 0:
                g0 = _row(small["mix_pre_g"], l)
                hn = rms_fwd(h, g0 if wtok is None else g0 + wtok, BF16, tag + "_pre")
                zpre = mm_cs(hn, GA, A_AIN, 1536, 1536, BF16, tag + "_in")
                y = gmlp_mid_fwd(zpre, small["a_v_norm_g"], small["a_v_norm_b"], a_ws, bsf, tag + "_gate")
                f, h, xn = mm_rs_post(y, GB, B_AOUT, 768, 768, h, gpost, 1.0, tag + "_out", g_next=g_ff2)
                rec["mix"] = (h_in, hn, zpre, y, f)
            elif l == 2:
                ag = mm_cs(hn, GA, A_CIN, 512, 512, BF16, tag + "_pw1")
                zc = conf_conv_fwd(ag, vecs[V_CDW:V_CDW + 32], vrow(V_CBDW), tag + "_conv")
                zs = conf_ln_fwd(zc, vrow(V_CNG), vrow(V_CNB), tag + "_ln")
                f, h, xn = mm_rs_post(zs, GB, B_CPW2, 256, 256, h, gpost, 1.0, tag + "_pw2", g_next=g_ff2)
                rec["mix"] = (h_in, hn, ag, zc, zs, f)
            else:
                bgx = mm_cs(hn, GA, A_DIN, 768, 768, BF16, tag + "_in")
                y = sconv_fwd(bgx, vecs[V_DCONV:V_DCONV + 8], tag + "_conv")
                f, h, xn = mm_rs_post(y, GB, B_DOUT, 256, 256, h, gpost, 1.0, tag + "_out", g_next=g_ff2)
                rec["mix"] = (h_in, hn, bgx, y, f)
        h, xn, rec["ff2"] = ff_fwd(h, xn, l, 2, small["ff2_post_g"], _row(small["ple_gate_norm_g"], l))
        tag = "ple_l%d" % l
        pb = p[l].astype(BF16)
        h_in = h
        g_next = _row(small["ff1_pre_g"], l + 1) if l < 3 else None
        out = ple_fwd(h, xn, pb, GB, B_PLEG(l), GC, _row(small["ple_post_g"], l), tag, g_next=g_next)
        rec["ple"] = (h_in, xn, out[0], out[1], pb)
        h = out[2]
        xn = out[3] if l < 3 else None
        saved.append(rec)

    dh, loss_cols = loss_head(h, tgt, "loss_head")

    dA2 = dB2 = None
    layer_grads = [None] * 4
    tok = None
    gV = {}
    gains = {k: [None] * 4 for k in ("ff1_pre_g", "ff1_post_g", "mix_pre_g", "mix_post_g", "ff2_pre_g", "ff2_post_g",
                                      "ple_gate_norm_g", "ple_post_g")}
    extra = {}

    def ff_bwd(dh, l, which, pre, post, rec, after=None):
        wa, wb = (GA1, GB1) if which == 1 else (GA, GB)
        tag = "ff%d_l%d_b" % (which, l)
        h_in, xn, gu, a, f = rec
        gp = _row(post, l) if after is None else _row(post, l) + after
        df, dpost, dgu = ff_bwd_down(dh, f, gp, wb, 0, gu, tag + "_down")
        if which == 1:
            db = dw_rs(a, df, FB, FB, tag + "_dwdown")
        else:
            db = dw_rs(a, df, FB, FB, tag + "_dwdown", height=B2_ROWS(l), off=B_FF2D(l), into=dB2)
        dh_in, dpre = mm_cs_t_rms(dgu, wa, 0, 2 * FB, 2 * FB, h_in, _row(pre, l), dh, tag + "_gateup")
        da = dw_cs(xn, dgu, 2 * FB, 2 * FB, tag + "_dwgateup", width=None if which == 1 else A2_COLS(l))
        return dh_in, dpre, dpost, (da, db)

    for l in reversed(range(4)):
        rec = saved[l]
        GA1, GB1, GA, GB, GC = W[l]
        gC = {}
        tag = "ple_l%d_b" % l
        h_in, xn, zg, pe, pb = rec["ple"]
        gpost = _row(small["ple_post_g"], l)
        if tok is not None:
            gpost = gpost + tok
        dzg, dpe, dh, gains["ple_post_g"][l], gains["ple_gate_norm_g"][l] = ple_bwd(
            dh, zg, pe, h_in, GB, B_PLEG(l), gpost, _row(small["ple_gate_norm_g"], l), tag)
        gC[C_PROJ(l)] = dw_cs(pb, dpe, 256, 256, tag + "_dwproj")
        dB2 = dw_rs(xn, dzg, 256, 256, tag + "_dwgate", height=B2_ROWS(l), off=B_PLEG(l))

        dh, gains["ff2_pre_g"][l], gains["ff2_post_g"][l], (dA2, dB2) = ff_bwd(
            dh, l, 2, small["ff2_pre_g"], small["ff2_post_g"], rec["ff2"])

        tag = "mix_l%d_b" % l
        mix = rec["mix"]
        h_in, f = mix[0], mix[-1]
        if l == 1:
            _, pooled, ypre, _ = mix
            df, gains["mix_post_g"][l] = post_res_bwd(dh, f, _row(small["mix_post_g"], l), 1.0, F32, tag + "_post")
            dhn, dwg, dsc = pool_bwd(df, ypre, pooled, bgrp, vrow(V_BSCALE), tag + "_pool")
            gC[C_BGRP] = dwg.astype(BF16).reshape(4, NCHIP, 64, 256).transpose(1, 0, 2, 3).reshape(NCHIP, 256, 256)
            gV[V_BSCALE] = jnp.pad(dsc, ((0, 7), (0, 0)))
            dh, gains["mix_pre_g"][l] = rms_bwd(dhn, h_in, _row(small["mix_pre_g"], l), dh, tag + "_pre")
        else:
            gpre = _row(small["mix_pre_g"], l)
            df, gains["mix_post_g"][l] = post_res_bwd(dh, f, _row(small["mix_post_g"], l), 1.0, BF16, tag + "_post")
            if l == 0:
                _, hn, zpre, y, _ = mix
                dy = mm_rs_t(df, GB, B_AOUT, 768, 768, tag + "_out")
                dB2 = dw_rs(y, df, 768, 768, tag + "_dwout", height=B2_ROWS(l), off=B_AOUT, into=dB2)
                dz, dws, dbs, dvg, dvb = gmlp_mid_bwd(zpre, dy, small["a_v_norm_g"], small["a_v_norm_b"], a_ws, bsf,
                                                      tag + "_gate")
                extra.update(a_w_s=dws, a_b_s=dbs[:, 0, :], a_v_norm_g=dvg, a_v_norm_b=dvb)
                dA2 = dw_cs(hn, dz, 1536, 1536, tag + "_dwin", width=A2_COLS(l), off=A_AIN, into=dA2)
                dh, gains["mix_pre_g"][l] = mm_cs_t_rms(dz, GA, A_AIN, 1536, 1536, h_in, gpre, dh, tag + "_in")
            elif l == 2:
                _, hn, ag, zc, zs, _ = mix
                dzs = mm_rs_t(df, GB, B_CPW2, 256, 256, tag + "_pw2")
                dB2 = dw_rs(zs, df, 256, 256, tag + "_dwpw2", height=B2_ROWS(l), off=B_CPW2, into=dB2)
                dzc, dng, dnb = conf_ln_bwd(dzs, zc, vrow(V_CNG), vrow(V_CNB), tag + "_ln")
                da_, dg_, dwdw, dbdw = conf_conv_bwd(dzc, ag, vecs[V_CDW:V_CDW + 32], tag + "_conv")
                dag = merge_cols([da_, dg_], tag + "_merge")
                gV[V_CDW] = dwdw
                gV[V_CBDW] = jnp.pad(dbdw, ((0, 7), (0, 0)))
                gV[V_CNG] = jnp.pad(dng, ((0, 7), (0, 0)))
                gV[V_CNB] = jnp.pad(dnb, ((0, 7), (0, 0)))
                dA2 = dw_cs(hn, dag, 512, 512, tag + "_dwpw1", width=A2_COLS(l), off=A_CIN, into=dA2)
                dh, gains["mix_pre_g"][l] = mm_cs_t_rms(dag, GA, A_CIN, 512, 512, h_in, gpre, dh, tag + "_pw1")
            else:
                _, hn, bgx, y, _ = mix
                dy = mm_rs_t(df, GB, B_DOUT, 256, 256, tag + "_out")
                dB2 = dw_rs(y, df, 256, 256, tag + "_dwout", height=B2_ROWS(l), off=B_DOUT, into=dB2)
                db_, dc_, dx_, dwc = sconv_bwd(dy, bgx, vecs[V_DCONV:V_DCONV + 8], tag + "_conv")
                dbgx = merge_cols([db_, dc_, dx_], tag + "_merge")
                gV[V_DCONV] = dwc
                dA2 = dw_cs(hn, dbgx, 768, 768, tag + "_dwin", width=A2_COLS(l), off=A_DIN, into=dA2)
                dh, gains["mix_pre_g"][l] = mm_cs_t_rms(dbgx, GA, A_DIN, 768, 768, h_in, gpre, dh, tag + "_in")

        dC = jnp.concatenate([gC[C_PROJ(l)]] + ([gC[C_BGRP]] if l == 1 else []), axis=1)
        tok = grads_ready(l, "b", (dA2, dB2, dC), dh)
        dh, gains["ff1_pre_g"][l], gains["ff1_post_g"][l], (dA1, dB1) = ff_bwd(
            dh, l, 1, small["ff1_pre_g"], small["ff1_post_g"], rec["ff1"], after=tok)
        layer_grads[l] = (dA1, dB1, dA2, dB2, dC)
        tok = grads_ready(l, "a", (dA1, dB1), dh)

    return loss_cols, dh, layer_grads, gV, gains, extra


GAIN_NAMES = ("ff1_pre_g", "ff1_post_g", "mix_pre_g", "mix_post_g", "ff2_pre_g", "ff2_post_g", "ple_gate_norm_g",
              "ple_post_g")


def _pad_rows(a, rows):
    return jnp.pad(a, ((0, rows - a.shape[0]), (0, 0)))


def kernel(x, p, ff1_pre_g, ff1_w_gate, ff1_w_up, ff1_w_down, ff1_post_g, mix_pre_g, mix_post_g, ff2_pre_g, ff2_w_gate, ff2_w_up, ff2_w_down, ff2_post_g, ple_gate_norm_g, ple_w_gate, ple_w_proj, ple_post_g, a_w_in, a_v_norm_g, a_v_norm_b, a_w_s, a_b_s, a_w_out, b_w_grp, b_scale, c_w_pw1, c_w_dw, c_b_dw, c_norm_g, c_norm_b, c_w_pw2, d_w_in, d_w_conv, d_w_out, loss_target, m_ff1_pre_g, m_ff1_w_gate, m_ff1_w_up, m_ff1_w_down, m_ff1_post_g, m_mix_pre_g, m_mix_post_g, m_ff2_pre_g, m_ff2_w_gate, m_ff2_w_up, m_ff2_w_down, m_ff2_post_g, m_ple_gate_norm_g, m_ple_w_gate, m_ple_w_proj, m_ple_post_g, m_a_w_in, m_a_v_norm_g, m_a_v_norm_b, m_a_w_s, m_a_b_s, m_a_w_out, m_b_w_grp, m_b_scale, m_c_w_pw1, m_c_w_dw, m_c_b_dw, m_c_norm_g, m_c_norm_b, m_c_w_pw2, m_d_w_in, m_d_w_conv, m_d_w_out, v_ff1_pre_g, v_ff1_w_gate, v_ff1_w_up, v_ff1_w_down, v_ff1_post_g, v_mix_pre_g, v_mix_post_g, v_ff2_pre_g, v_ff2_w_gate, v_ff2_w_up, v_ff2_w_down, v_ff2_post_g, v_ple_gate_norm_g, v_ple_w_gate, v_ple_w_proj, v_ple_post_g, v_a_w_in, v_a_v_norm_g, v_a_v_norm_b, v_a_w_s, v_a_b_s, v_a_w_out, v_b_w_grp, v_b_scale, v_c_w_pw1, v_c_w_dw, v_c_b_dw, v_c_norm_g, v_c_norm_b, v_c_w_pw2, v_d_w_in, v_d_w_conv, v_d_w_out):
    args = dict(locals())
    wnames = ["ff1_pre_g", "ff1_w_gate", "ff1_w_up", "ff1_w_down", "ff1_post_g", "mix_pre_g", "mix_post_g",
              "ff2_pre_g", "ff2_w_gate", "ff2_w_up", "ff2_w_down", "ff2_post_g", "ple_gate_norm_g", "ple_w_gate",
              "ple_w_proj", "ple_post_g", "a_w_in", "a_v_norm_g", "a_v_norm_b", "a_w_s", "a_b_s", "a_w_out",
              "b_w_grp", "b_scale", "c_w_pw1", "c_w_dw", "c_b_dw", "c_norm_g", "c_norm_b", "c_w_pw2", "d_w_in",
              "d_w_conv", "d_w_out"]

    P = pack_weights(args)
    G0a = all_gather_chips([P[0][0], P[0][1], P[4]], "gather_l0a")
    vecs = G0a[2].transpose(1, 0, 2).reshape(V_ROWS, DM)
    flying = {}

    def start(key, bufs, after):
        send, recv, slots, token = gather_start(own_slots(list(bufs)), after, "gather_start_l" + key)
        flying[key] = (send, recv, slots)
        return token[0, 0]

    tok = start("0b", P[0][2:], G0a[0])
    arrived = {}

    def weights_of(l, part, h):
        if l == 0 and part == "a":
            return G0a[0], G0a[1], None
        key = "0b" if l == 0 else str(l)
        wtok = None
        if key not in arrived:
            send, recv, slots = flying[key]
            n = len(slots)
            landed = gather_wait(send, recv, slots, list(range(n)), h, "gather_wait_l" + key)
            arrived[key] = gather_pass(landed, "gather_pass_l" + key)
            if l < 3:
                wtok = start(str(l + 1), P[l + 1], arrived[key][0])
        got = arrived[key]
        if l == 0:
            return tuple(got) + (wtok,)
        return tuple(got[:2]) + (wtok,) if part == "a" else tuple(got[2:]) + (None,)

    pending = {}
    reduced = {}
    held = {}

    def finish(key, after):
        kinds, send, recv, sums, landing = pending.pop(key)
        landed = chip_exchange_wait(send, recv, sums, landing, after, "rs_wait_l" + key)
        halves = [sum_slots(landed[i], *_tile(k, landed[i]), name="rs_sum_chips_l%s_%d%s" % (key, i, k))
                  for i, k in enumerate(kinds)]
        reduced[key] = pair_share(halves, "rs_pair_share_l" + key)

    def grads_ready(l, part, bufs, dh):
        if part == "b" and l > 0:
            held[l] = list(bufs)
            return None
        if part == "a" and l > 0:
            key, kinds, parts = str(l), "ABABC", list(bufs) + held.pop(l)
        elif part == "b":
            key, kinds, parts = "0b", "ABC", list(bufs)
        else:
            finish("0b", dh)
            return None
        for other in list(pending):
            finish(other, dh)
        got = pair_exchange(parts, "rs_pair_exchange_l" + key)
        sums = [add_half(parts[i], got[i], *_tile(k, got[i]), name="rs_add_pair_l%s_%d%s" % (key, i, k))
                for i, k in enumerate(kinds)]
        send, recv, sums, landing, token = chip_exchange_start(sums, "rs_start_l" + key)
        pending[key] = (kinds, send, recv, sums, landing)
        return token[0, 0]

    small = {k: args[k] for k in GAIN_NAMES}
    small["ff1_pre_g"] = ff1_pre_g + tok
    small["a_v_norm_g"] = a_v_norm_g
    small["a_v_norm_b"] = a_v_norm_b
    loss_cols, grad_x, layer_grads, gV, gains, extra = local_step(
        x[0], p[:, 0], loss_target[0], small, weights_of, vecs, a_w_s[0], a_b_s[0], grads_ready)

    loss = lax.psum((0.5 / DM) * jnp.sum(loss_cols), ("x", "y", "c"))

    dV, dE = pack_small_grads(gV, gains, extra)
    red = reduce_scatter({"A": layer_grads[0][0], "B": layer_grads[0][1], "V": dV, "E": dE}, "rs_l0a")
    (gE,) = all_gather_chips([red["E"]], "gather_replicated_grads")
    per_layer = [[red["A"], red["B"]] + list(reduced["0b"])] + [list(reduced[str(l)]) for l in (1, 2, 3)]
    grads = unpack_grads(per_layer, red["V"], gE.reshape(E_ROWS, DM))

    deltas, new_m, new_v = {}, {}, {}
    for k in wnames:
        if args[k].shape[-1] == FW:
            t = lambda a: jnp.swapaxes(a, 1, 2)
            outs = adamw(t(args[k]), t(grads[k]), t(args["m_" + k]), t(args["v_" + k]), "adamw_" + k)
            deltas[k], new_m[k], new_v[k] = (t(o) for o in outs)
        else:
            deltas[k], new_m[k], new_v[k] = adamw(args[k], grads[k], args["m_" + k], args["v_" + k], "adamw_" + k)
    return (loss, grad_x[None], *[grads[k] for k in wnames], *[deltas[k] for k in wnames],
            *[new_m[k] for k in wnames], *[new_v[k] for k in wnames])


def pack_weights(w):
    padc = lambda a: jnp.pad(a, ((0, 0), (0, FB - FW)))
    mix_in = [w["a_w_in"][0], None, w["c_w_pw1"][0], w["d_w_in"][0]]
    mix_out = [w["a_w_out"][0], None, w["c_w_pw2"][0], w["d_w_out"][0]]
    packed = []
    for l in range(4):
        a1 = jnp.concatenate([padc(w["ff1_w_gate"][l]), padc(w["ff1_w_up"][l])], axis=1).astype(BF16)
        b1 = _pad_rows(w["ff1_w_down"][l], FB).astype(BF16)
        cols = [padc(w["ff2_w_gate"][l]), padc(w["ff2_w_up"][l])]
        rows = [_pad_rows(w["ff2_w_down"][l], FB)]
        if l != 1:
            cols.append(mix_in[l])
            rows.append(mix_out[l])
        rows.append(w["ple_w_gate"][l])
        proj = [w["ple_w_proj"][l]] + ([w["b_w_grp"][0].reshape(256, 256)] if l == 1 else [])
        packed.append((a1, b1, jnp.concatenate(cols, axis=1).astype(BF16), jnp.concatenate(rows, axis=0).astype(BF16),
                       jnp.concatenate(proj, axis=0).astype(BF16)))
    PV = jnp.concatenate([_pad_rows(w["b_scale"], 8), _pad_rows(w["c_b_dw"], 8), _pad_rows(w["c_norm_g"], 8),
                          _pad_rows(w["c_norm_b"], 8), _pad_rows(w["d_w_conv"][0], 8), _pad_rows(w["c_w_dw"][0], 40)],
                         axis=0)
    return packed + [PV]


def pack_small_grads(gV, gains, extra):
    dVt = jnp.concatenate([gV[V_BSCALE], gV[V_CBDW], gV[V_CNG], gV[V_CNB], gV[V_DCONV], gV[V_CDW],
                           jnp.zeros((8, DM), F32)], axis=0)
    dV = dVt.reshape(V_ROWS, NCHIP, 256).transpose(1, 0, 2)
    rowsE = [_pad_rows(jnp.concatenate(gains[k], axis=0), 8) for k in GAIN_NAMES]
    rowsE += [_pad_rows(extra["a_v_norm_g"].reshape(3, DM), 8), _pad_rows(extra["a_v_norm_b"].reshape(3, DM), 8),
              jnp.pad(extra["a_b_s"].reshape(1536), (0, 8 * DM - 1536)).reshape(8, DM),
              extra["a_w_s"].reshape(192, DM)]
    dE = _pad_rows(jnp.concatenate(rowsE, axis=0), E_ROWS).reshape(NCHIP, E_ROWS // NCHIP, DM)
    return dV, dE


def unpack_grads(per_layer, RV, gE):
    grads = {}
    for i, k in enumerate(GAIN_NAMES):
        grads[k] = gE[8 * i:8 * i + 4]
    grads["a_v_norm_g"] = gE[64:67].reshape(1, 3072)
    grads["a_v_norm_b"] = gE[72:75].reshape(1, 3072)
    grads["a_b_s"] = gE[80:88].reshape(8 * DM)[:1536].reshape(1, 12, 128)
    grads["a_w_s"] = gE[88:280].reshape(1, 12, 128, 128)
    col1 = lambda l, off, n: per_layer[l][0][:, off:off + n]
    col2 = lambda l, off, n: per_layer[l][2][:, off:off + n]
    grads["ff1_w_gate"] = jnp.stack([col1(l, A_FF(l, 0), FW) for l in range(4)])
    grads["ff1_w_up"] = jnp.stack([col1(l, A_FF(l, 1), FW) for l in range(4)])
    grads["ff2_w_gate"] = jnp.stack([col2(l, A_FF(l, 2), FW) for l in range(4)])
    grads["ff2_w_up"] = jnp.stack([col2(l, A_FF(l, 3), FW) for l in range(4)])
    grads["a_w_in"] = col2(0, A_AIN, 1536)[None]
    grads["c_w_pw1"] = col2(2, A_CIN, 512)[None]
    grads["d_w_in"] = col2(3, A_DIN, 768)[None]
    row2 = lambda l, off, n: per_layer[l][3][off:off + n]
    grads["ff1_w_down"] = jnp.stack([per_layer[l][1][:FW] for l in range(4)])
    grads["ff2_w_down"] = jnp.stack([row2(l, B_FF2D(l), FW) for l in range(4)])
    grads["ple_w_gate"] = jnp.stack([row2(l, B_PLEG(l), 256) for l in range(4)])
    grads["a_w_out"] = row2(0, B_AOUT, 768)[None]
    grads["c_w_pw2"] = row2(2, B_CPW2, 256)[None]
    grads["d_w_out"] = row2(3, B_DOUT, 256)[None]
    grads["ple_w_proj"] = jnp.stack([per_layer[l][4][C_PROJ(l):C_PROJ(l) + 256] for l in range(4)])
    grads["b_w_grp"] = per_layer[1][4][C_BGRP:C_BGRP + 256].reshape(1, 4, 64, 256)
    grads["b_scale"] = RV[V_BSCALE:V_BSCALE + 1]
    grads["c_b_dw"] = RV[V_CBDW:V_CBDW + 1]
    grads["c_norm_g"] = RV[V_CNG:V_CNG + 1]
    grads["c_norm_b"] = RV[V_CNB:V_CNB + 1]
    grads["d_w_conv"] = RV[V_DCONV:V_DCONV + 3][None]
    grads["c_w_dw"] = RV[V_CDW:V_CDW + 31][None]
    return grads
```

```python
import functools
import math

import jax
import jax.numpy as jnp
from jax import lax
from jax.experimental import pallas as pl
from jax.experimental.pallas import tpu as pltpu

F32, BF16 = jnp.float32, jnp.bfloat16
EPS = 1e-6
DM = 1024
FW = 704
FB = 768
NCHIP = 4
VMEM_LIMIT = 56 * 1024 * 1024
ANY = pl.BlockSpec(memory_space=pl.ANY)
MESH = pl.DeviceIdType.MESH

A_FF = lambda l, j: (j % 2) * FB
A_AIN = A_CIN = A_DIN = 2 * FB
A2_COLS = lambda l: 2 * FB + (1536, 0, 512, 768)[l]
B_FF1D = lambda l: 0
B_FF2D = lambda l: 0
B_AOUT = B_CPW2 = B_DOUT = FB
B_PLEG = lambda l: FB + (768, 0, 256, 256)[l]
B2_ROWS = lambda l: B_PLEG(l) + 256
C_PROJ = lambda l: 0
C_BGRP = 256
V_BSCALE, V_CBDW, V_CNG, V_CNB, V_DCONV, V_CDW, V_ROWS = 0, 8, 16, 24, 32, 40, 80
E_ROWS = 320

ADAM_LR, ADAM_B1, ADAM_B2, ADAM_EPS, ADAM_WD, ADAM_STEP = 0.001, 0.9, 0.999, 1e-08, 0.01, 10


def _cp(sem):
    return pltpu.CompilerParams(dimension_semantics=sem, vmem_limit_bytes=VMEM_LIMIT)


def _sig(x):
    return 0.5 * jnp.tanh(0.5 * x) + 0.5


_GC = math.sqrt(2.0 / math.pi)


def _gelu(x):
    return 0.5 * x * (1.0 + jnp.tanh(_GC * (x + 0.044715 * x * x * x)))


def _gelu_grad(x):
    t = jnp.tanh(_GC * (x + 0.044715 * x * x * x))
    return 0.5 * (1.0 + t) + 0.5 * x * (1.0 - t * t) * _GC * (1.0 + 3.0 * 0.044715 * x * x)


def _dot_nn(a, b):
    return lax.dot_general(a, b, (((1,), (0,)), ((), ())), preferred_element_type=F32)


def _dot_nt(a, b):
    return lax.dot_general(a, b, (((1,), (1,)), ((), ())), preferred_element_type=F32)


def _dot_tn(a, b):
    return lax.dot_general(a, b, (((0,), (0,)), ((), ())), preferred_element_type=F32)


def mm_cs(x, G, off, nb, tn, out_dtype, name, roff=0):
    T, K = x.shape
    tm = min(1024, T)
    nj, ob, rb_ = nb // tn, off // tn, roff // K
    assert nb % tn == 0 and off % tn == 0 and roff % K == 0

    def body(x_ref, w_ref, o_ref):
        o_ref[...] = _dot_nn(x_ref[...], w_ref[...]).astype(o_ref.dtype)

    return pl.pallas_call(
        body, name=name, grid=(T // tm, NCHIP, nj),
        in_specs=[pl.BlockSpec((tm, K), lambda i, s, j: (i, 0)),
                  pl.BlockSpec((None, K, tn), lambda i, s, j: (s, rb_, ob + j))],
        out_specs=pl.BlockSpec((tm, tn), lambda i, s, j: (i, s * nj + j)),
        out_shape=jax.ShapeDtypeStruct((T, NCHIP * nb), out_dtype),
        compiler_params=_cp(("parallel", "arbitrary", "arbitrary")))(x, G)


def mm_cs_t(dy, G, off, nb, tn, name):
    T = dy.shape[0]
    K = G.shape[1]
    tm = min(1024, T)
    nj, ob = nb // tn, off // tn
    nk = NCHIP * nj

    def body(dy_ref, w_ref, o_ref, acc):
        k = pl.program_id(1)

        @pl.when(k == 0)
        def _():
            acc[...] = jnp.zeros_like(acc)

        acc[...] += _dot_nt(dy_ref[...], w_ref[...])

        @pl.when(k == nk - 1)
        def _():
            o_ref[...] = acc[...]

    return pl.pallas_call(
        body, name=name, grid=(T // tm, nk),
        in_specs=[pl.BlockSpec((tm, tn), lambda i, k: (i, k)),
                  pl.BlockSpec((None, K, tn), lambda i, k: (k // nj, 0, ob + k % nj))],
        out_specs=pl.BlockSpec((tm, K), lambda i, k: (i, 0)),
        out_shape=jax.ShapeDtypeStruct((T, K), F32),
        scratch_shapes=[pltpu.VMEM((tm, K), F32)],
        compiler_params=_cp(("parallel", "arbitrary")))(dy, G)


def mm_rs(a, G, off, rb, tk, name):
    T = a.shape[0]
    N = G.shape[2]
    tm = min(1024, T)
    nkk, ob = rb // tk, off // tk
    nk = NCHIP * nkk
    assert rb % tk == 0 and off % tk == 0

    def body(a_ref, w_ref, o_ref, acc):
        k = pl.program_id(1)

        @pl.when(k == 0)
        def _():
            acc[...] = jnp.zeros_like(acc)

        acc[...] += _dot_nn(a_ref[...], w_ref[...])

        @pl.when(k == nk - 1)
        def _():
            o_ref[...] = acc[...]

    return pl.pallas_call(
        body, name=name, grid=(T // tm, nk),
        in_specs=[pl.BlockSpec((tm, tk), lambda i, k: (i, k)),
                  pl.BlockSpec((None, tk, N), lambda i, k: (k // nkk, ob + k % nkk, 0))],
        out_specs=pl.BlockSpec((tm, N), lambda i, k: (i, 0)),
        out_shape=jax.ShapeDtypeStruct((T, N), F32),
        scratch_shapes=[pltpu.VMEM((tm, N), F32)],
        compiler_params=_cp(("parallel", "arbitrary")))(a, G)


def mm_rs_t(dy, G, off, rb, tk, name):
    T, N = dy.shape
    tm = min(1024, T)
    nkk, ob = rb // tk, off // tk
    nk = NCHIP * nkk

    def body(dy_ref, w_ref, o_ref):
        o_ref[...] = _dot_nt(dy_ref[...], w_ref[...]).astype(o_ref.dtype)

    return pl.pallas_call(
        body, name=name, grid=(T // tm, nk),
        in_specs=[pl.BlockSpec((tm, N), lambda i, k: (i, 0)),
                  pl.BlockSpec((None, tk, N), lambda i, k: (k // nkk, ob + k % nkk, 0))],
        out_specs=pl.BlockSpec((tm, tk), lambda i, k: (i, k)),
        out_shape=jax.ShapeDtypeStruct((T, NCHIP * rb), BF16),
        compiler_params=_cp(("parallel", "arbitrary")))(dy, G)


def mm_tn(a, b, tmm, tn, out_shape, out_map, name, into=None):
    T, M = a.shape
    N = b.shape[1]
    tt = min(2048, T)
    nt = T // tt

    def body(a_ref, b_ref, o_ref, acc):
        t = pl.program_id(2)

        @pl.when(t == 0)
        def _():
            acc[...] = jnp.zeros_like(acc)

        acc[...] += _dot_tn(a_ref[...], b_ref[...])

        @pl.when(t == nt - 1)
        def _():
            o_ref[...] = acc[...].astype(o_ref.dtype)

    in_specs = [pl.BlockSpec((tt, tmm), lambda i, j, t: (t, i)), pl.BlockSpec((tt, tn), lambda i, j, t: (t, j))]
    operands = (a, b)
    if into is None:
        def kern(a_ref, b_ref, o_ref, acc):
            body(a_ref, b_ref, o_ref, acc)
        aliases = {}
    else:
        def kern(a_ref, b_ref, into_ref, o_ref, acc):
            body(a_ref, b_ref, o_ref, acc)
        in_specs.append(ANY)
        operands = (a, b, into)
        aliases = {2: 0}
        out_shape = into.shape
    return pl.pallas_call(
        kern, name=name, grid=(M // tmm, N // tn, nt), in_specs=in_specs,
        out_specs=pl.BlockSpec((None, tmm, tn), lambda i, j, t: out_map(i, j)),
        out_shape=jax.ShapeDtypeStruct(out_shape, BF16), input_output_aliases=aliases,
        scratch_shapes=[pltpu.VMEM((tmm, tn), F32)],
        compiler_params=_cp(("parallel", "parallel", "arbitrary")))(*operands)


def dw_cs(x, dy, nb, tn, name, width=None, off=0, into=None):
    K = x.shape[1]
    nj, ob = nb // tn, off // tn
    assert off % tn == 0
    return mm_tn(x, dy, K, tn, (NCHIP, K, width or nb), lambda i, j: (j // nj, 0, ob + j % nj), name, into)


def dw_rs(a, dy, rb, tr, name, height=None, off=0, into=None):
    N = dy.shape[1]
    ni, ob = rb // tr, off // tr
    assert off % tr == 0
    return mm_tn(a, dy, tr, N, (NCHIP, height or rb, N), lambda i, j: (i // ni, ob + i % ni, 0), name, into)


def ff_gateup(xn, GA, off, name):
    T, K = xn.shape
    tm = min(1024, T)
    ob = off // (2 * FB)
    assert off % (2 * FB) == 0

    sub = min(512, tm)

    def body(x_ref, w_ref, gu_ref, a_ref):
        for r0 in range(0, tm, sub):
            r = _dot_nn(x_ref[r0:r0 + sub, :], w_ref[...])
            g, u = r[:, :FB], r[:, FB:]
            gu_ref[r0:r0 + sub, :] = r.astype(gu_ref.dtype)
            a_ref[r0:r0 + sub, :] = (g * _sig(g) * u).astype(a_ref.dtype)

    return pl.pallas_call(
        body, name=name, grid=(T // tm, NCHIP),
        in_specs=[pl.BlockSpec((tm, K), lambda i, s: (i, 0)),
                  pl.BlockSpec((None, K, 2 * FB), lambda i, s: (s, 0, ob))],
        out_specs=[pl.BlockSpec((tm, 2 * FB), lambda i, s: (i, s)), pl.BlockSpec((tm, FB), lambda i, s: (i, s))],
        out_shape=[jax.ShapeDtypeStruct((T, NCHIP * 2 * FB), BF16), jax.ShapeDtypeStruct((T, NCHIP * FB), BF16)],
        compiler_params=_cp(("parallel", "arbitrary")))(xn, GA)


def mm_rs_post(a, G, off, rb, tk, h, g, scale, name, g_next=None):
    T = a.shape[0]
    N = G.shape[2]
    tm = min(1024, T)
    sub = min(512, tm)
    nkk, ob = rb // tk, off // tk
    nk = NCHIP * nkk
    assert rb % tk == 0 and off % tk == 0
    more = g_next is not None

    def body(a_ref, w_ref, h_ref, g_ref, *rest):
        if more:
            gn_ref, f_ref, o_ref, xn_ref, acc = rest
        else:
            f_ref, o_ref, acc = rest
        k = pl.program_id(1)

        @pl.when(k == 0)
        def _():
            acc[...] = jnp.zeros_like(acc)

        acc[...] += _dot_nn(a_ref[...], w_ref[...])

        @pl.when(k == nk - 1)
        def _():
            for r0 in range(0, tm, sub):
                f = acc[r0:r0 + sub, :]
                f_ref[r0:r0 + sub, :] = f
                r = lax.rsqrt(jnp.mean(f * f, axis=-1, keepdims=True) + EPS)
                hn = h_ref[r0:r0 + sub, :] + scale * (f * r * g_ref[...])
                o_ref[r0:r0 + sub, :] = hn
                if more:
                    rn = lax.rsqrt(jnp.mean(hn * hn, axis=-1, keepdims=True) + EPS)
                    xn_ref[r0:r0 + sub, :] = (hn * rn * gn_ref[...]).astype(xn_ref.dtype)

    row = pl.BlockSpec((tm, N), lambda i, k: (i, 0))
    row1 = pl.BlockSpec((tm, N), lambda i, k: (i, 0), pipeline_mode=pl.Buffered(1))
    vec = pl.BlockSpec((1, N), lambda i, k: (0, 0))
    return pl.pallas_call(
        body, name=name, grid=(T // tm, nk),
        in_specs=[pl.BlockSpec((tm, tk), lambda i, k: (i, k)),
                  pl.BlockSpec((None, tk, N), lambda i, k: (k // nkk, ob + k % nkk, 0)),
                  row1, vec] + ([vec] if more else []),
        out_specs=[row, row] + ([row] if more else []),
        out_shape=[jax.ShapeDtypeStruct((T, N), F32), jax.ShapeDtypeStruct((T, N), F32)]
        + ([jax.ShapeDtypeStruct((T, N), BF16)] if more else []),
        scratch_shapes=[pltpu.VMEM((tm, N), F32)],
        compiler_params=_cp(("parallel", "arbitrary")))(*((a, G, h, g) + ((g_next,) if more else ())))


def ff_bwd_down(dh, f, g, GB, down, gu, name):
    T, N = dh.shape
    tm = min(1024, T)
    sub = min(512, tm)
    ob = down // FB

    def body(d_ref, f_ref, g_ref, w_ref, gu_ref, df_ref, dg_ref, dgu_ref):
        i, s = pl.program_id(0), pl.program_id(1)

        @pl.when(s == 0)
        def _():
            dg = jnp.zeros((1, N), F32)
            for r0 in range(0, tm, sub):
                f = f_ref[r0:r0 + sub, :]
                r = lax.rsqrt(jnp.mean(f * f, axis=-1, keepdims=True) + EPS)
                d = 0.5 * d_ref[r0:r0 + sub, :]
                t = d * g_ref[...]
                df_ref[r0:r0 + sub, :] = (
                    r * t - f * (r * r * r * jnp.mean(t * f, axis=-1, keepdims=True))).astype(df_ref.dtype)
                dg = dg + jnp.sum(d * f * r, axis=0, keepdims=True)
            _acc_rows(dg_ref, i, dg)

        for r0 in range(0, tm, sub):
            da = _dot_nt(df_ref[r0:r0 + sub, :], w_ref[...])
            gt = gu_ref[r0:r0 + sub, :FB].astype(F32)
            u = gu_ref[r0:r0 + sub, FB:].astype(F32)
            sg = _sig(gt)
            silu = gt * sg
            dgu_ref[r0:r0 + sub, :FB] = (da * u * (sg + silu - silu * sg)).astype(dgu_ref.dtype)
            dgu_ref[r0:r0 + sub, FB:] = (da * silu).astype(dgu_ref.dtype)

    row1 = pl.BlockSpec((tm, N), lambda i, s: (i, 0), pipeline_mode=pl.Buffered(1))
    row = pl.BlockSpec((tm, N), lambda i, s: (i, 0))
    vec = pl.BlockSpec((1, N), lambda i, s: (0, 0))
    return pl.pallas_call(
        body, name=name, grid=(T // tm, NCHIP),
        in_specs=[row1, row1, vec, pl.BlockSpec((None, FB, N), lambda i, s: (s, ob, 0)),
                  pl.BlockSpec((tm, 2 * FB), lambda i, s: (i, s))],
        out_specs=[row, vec, pl.BlockSpec((tm, 2 * FB), lambda i, s: (i, s))],
        out_shape=[jax.ShapeDtypeStruct((T, N), BF16), jax.ShapeDtypeStruct((1, N), F32),
                   jax.ShapeDtypeStruct((T, NCHIP * 2 * FB), BF16)],
        compiler_params=_cp(("arbitrary", "arbitrary")))(dh, f, g, GB, gu)


def mm_cs_t_rms(dy, G, off, nb, tn, h, g, skip, name):
    T = dy.shape[0]
    K = G.shape[1]
    tm = min(1024, T)
    sub = min(512, tm)
    nj, ob = nb // tn, off // tn
    nk = NCHIP * nj
    assert nb % tn == 0 and off % tn == 0

    def body(dy_ref, w_ref, h_ref, g_ref, s_ref, o_ref, dg_ref, acc):
        i, k = pl.program_id(0), pl.program_id(1)

        @pl.when(k == 0)
        def _():
            acc[...] = jnp.zeros_like(acc)

        acc[...] += _dot_nt(dy_ref[...], w_ref[...])

        @pl.when(k == nk - 1)
        def _():
            dg = jnp.zeros((1, K), F32)
            for r0 in range(0, tm, sub):
                d = acc[r0:r0 + sub, :]
                x = h_ref[r0:r0 + sub, :]
                r = lax.rsqrt(jnp.mean(x * x, axis=-1, keepdims=True) + EPS)
                xh = x * r
                t = d * g_ref[...]
                o_ref[r0:r0 + sub, :] = s_ref[r0:r0 + sub, :] + r * (t - xh * jnp.mean(t * xh, axis=-1, keepdims=True))
                dg = dg + jnp.sum(d * xh, axis=0, keepdims=True)
            _acc_rows(dg_ref, i, dg)

    row1 = pl.BlockSpec((tm, K), lambda i, k: (i, 0), pipeline_mode=pl.Buffered(1))
    row = pl.BlockSpec((tm, K), lambda i, k: (i, 0))
    vec = pl.BlockSpec((1, K), lambda i, k: (0, 0))
    return pl.pallas_call(
        body, name=name, grid=(T // tm, nk),
        in_specs=[pl.BlockSpec((tm, tn), lambda i, k: (i, k)),
                  pl.BlockSpec((None, K, tn), lambda i, k: (k // nj, 0, ob + k % nj)), row1, vec, row1],
        out_specs=[row, vec],
        out_shape=[jax.ShapeDtypeStruct((T, K), F32), jax.ShapeDtypeStruct((1, K), F32)],
        scratch_shapes=[pltpu.VMEM((tm, K), F32)],
        compiler_params=_cp(("arbitrary", "arbitrary")))(dy, G, h, g, skip)


def _rows(tm, C):
    return pl.BlockSpec((tm, C), lambda i: (i, 0))


def _vec(C):
    return pl.BlockSpec((1, C), lambda i: (0, 0))


def _acc_rows(ref, i, val):
    @pl.when(i == 0)
    def _():
        ref[...] = val

    @pl.when(i > 0)
    def _():
        ref[...] += val


def rms_fwd(h, g, out_dtype, name):
    T, C = h.shape
    tm = min(512, T)

    def body(h_ref, g_ref, o_ref):
        x = h_ref[...]
        r = lax.rsqrt(jnp.mean(x * x, axis=-1, keepdims=True) + EPS)
        o_ref[...] = (x * r * g_ref[...]).astype(o_ref.dtype)

    return pl.pallas_call(
        body, name=name, grid=(T // tm,), in_specs=[_rows(tm, C), _vec(C)], out_specs=_rows(tm, C),
        out_shape=jax.ShapeDtypeStruct((T, C), out_dtype), compiler_params=_cp(("parallel",)))(h, g)


def rms_bwd(dxn, h, g, dh_skip, name):
    T, C = h.shape
    tm = min(512, T)

    def body(d_ref, h_ref, g_ref, s_ref, o_ref, dg_ref):
        i = pl.program_id(0)
        x = h_ref[...]
        r = lax.rsqrt(jnp.mean(x * x, axis=-1, keepdims=True) + EPS)
        xh = x * r
        d = d_ref[...].astype(F32)
        t = d * g_ref[...]
        o_ref[...] = s_ref[...] + r * (t - xh * jnp.mean(t * xh, axis=-1, keepdims=True))
        _acc_rows(dg_ref, i, jnp.sum(d * xh, axis=0, keepdims=True))

    return pl.pallas_call(
        body, name=name, grid=(T // tm,),
        in_specs=[_rows(tm, C), _rows(tm, C), _vec(C), _rows(tm, C)],
        out_specs=[_rows(tm, C), _vec(C)],
        out_shape=[jax.ShapeDtypeStruct((T, C), F32), jax.ShapeDtypeStruct((1, C), F32)],
        compiler_params=_cp(("arbitrary",)))(dxn, h, g, dh_skip)


def post_res(h, f, g, scale, name):
    T, C = h.shape
    tm = min(512, T)

    def body(h_ref, f_ref, g_ref, o_ref):
        f = f_ref[...]
        r = lax.rsqrt(jnp.mean(f * f, axis=-1, keepdims=True) + EPS)
        o_ref[...] = h_ref[...] + scale * (f * r * g_ref[...])

    return pl.pallas_call(
        body, name=name, grid=(T // tm,), in_specs=[_rows(tm, C), _rows(tm, C), _vec(C)],
        out_specs=_rows(tm, C), out_shape=jax.ShapeDtypeStruct((T, C), F32),
        compiler_params=_cp(("parallel",)))(h, f, g)


def post_res_bwd(dh, f, g, scale, out_dtype, name):
    T, C = dh.shape
    tm = min(512, T)

    def body(d_ref, f_ref, g_ref, o_ref, dg_ref):
        i = pl.program_id(0)
        f = f_ref[...]
        r = lax.rsqrt(jnp.mean(f * f, axis=-1, keepdims=True) + EPS)
        d = scale * d_ref[...]
        t = d * g_ref[...]
        o_ref[...] = (r * t - f * (r * r * r * jnp.mean(t * f, axis=-1, keepdims=True))).astype(o_ref.dtype)
        _acc_rows(dg_ref, i, jnp.sum(d * f * r, axis=0, keepdims=True))

    return pl.pallas_call(
        body, name=name, grid=(T // tm,), in_specs=[_rows(tm, C), _rows(tm, C), _vec(C)],
        out_specs=[_rows(tm, C), _vec(C)],
        out_shape=[jax.ShapeDtypeStruct((T, C), out_dtype), jax.ShapeDtypeStruct((1, C), F32)],
        compiler_params=_cp(("arbitrary",)))(dh, f, g)


def ff_act(gu, name):
    T = gu.shape[0]
    tm = min(512, T)

    def body(gu_ref, o_ref):
        g = gu_ref[:, :FB].astype(F32)
        u = gu_ref[:, FB:].astype(F32)
        o_ref[...] = (g * _sig(g) * u).astype(o_ref.dtype)

    return pl.pallas_call(
        body, name=name, grid=(T // tm, NCHIP),
        in_specs=[pl.BlockSpec((tm, 2 * FB), lambda i, s: (i, s))],
        out_specs=pl.BlockSpec((tm, FB), lambda i, s: (i, s)),
        out_shape=jax.ShapeDtypeStruct((T, NCHIP * FB), BF16),
        compiler_params=_cp(("parallel", "parallel")))(gu)


def ff_act_bwd(da, gu, name):
    T = gu.shape[0]
    tm = min(512, T)

    def body(da_ref, gu_ref, o_ref):
        g = gu_ref[:, :FB].astype(F32)
        u = gu_ref[:, FB:].astype(F32)
        da = da_ref[...].astype(F32)
        s = _sig(g)
        o_ref[:, :FB] = (da * u * (s * (1.0 + g * (1.0 - s)))).astype(o_ref.dtype)
        o_ref[:, FB:] = (da * (g * s)).astype(o_ref.dtype)

    return pl.pallas_call(
        body, name=name, grid=(T // tm, NCHIP),
        in_specs=[pl.BlockSpec((tm, FB), lambda i, s: (i, s)), pl.BlockSpec((tm, 2 * FB), lambda i, s: (i, s))],
        out_specs=pl.BlockSpec((tm, 2 * FB), lambda i, s: (i, s)),
        out_shape=jax.ShapeDtypeStruct((T, NCHIP * 2 * FB), BF16),
        compiler_params=_cp(("parallel", "parallel")))(da, gu)


def ple_post(h, zg, pe, g, name, g_next=None):
    T, C = h.shape
    tm = min(512, T)
    more = g_next is not None

    def body(h_ref, z_ref, p_ref, g_ref, *rest):
        e = p_ref[...] * _sig(z_ref[...])
        r = lax.rsqrt(jnp.mean(e * e, axis=-1, keepdims=True) + EPS)
        hn = h_ref[...] + e * r * g_ref[...]
        if more:
            gn_ref, o_ref, xn_ref = rest
            rn = lax.rsqrt(jnp.mean(hn * hn, axis=-1, keepdims=True) + EPS)
            xn_ref[...] = (hn * rn * gn_ref[...]).astype(xn_ref.dtype)
        else:
            (o_ref,) = rest
        o_ref[...] = hn

    return pl.pallas_call(
        body, name=name, grid=(T // tm,),
        in_specs=[_rows(tm, C), _rows(tm, C), _rows(tm, C), _vec(C)] + ([_vec(C)] if more else []),
        out_specs=[_rows(tm, C)] + ([_rows(tm, C)] if more else []),
        out_shape=[jax.ShapeDtypeStruct((T, C), F32)] + ([jax.ShapeDtypeStruct((T, C), BF16)] if more else []),
        compiler_params=_cp(("parallel",)))(*((h, zg, pe, g) + ((g_next,) if more else ())))


def ple_post_bwd(dh, zg, pe, g, name):
    T, C = dh.shape
    tm = min(512, T)

    def body(d_ref, z_ref, p_ref, g_ref, dz_ref, dp_ref, dg_ref):
        i = pl.program_id(0)
        s = _sig(z_ref[...])
        pe_ = p_ref[...]
        e = pe_ * s
        r = lax.rsqrt(jnp.mean(e * e, axis=-1, keepdims=True) + EPS)
        d = d_ref[...]
        t = d * g_ref[...]
        de = r * t - e * (r * r * r * jnp.mean(t * e, axis=-1, keepdims=True))
        dp_ref[...] = (de * s).astype(dp_ref.dtype)
        dz_ref[...] = (de * pe_ * s * (1.0 - s)).astype(dz_ref.dtype)
        _acc_rows(dg_ref, i, jnp.sum(d * e * r, axis=0, keepdims=True))

    return pl.pallas_call(
        body, name=name, grid=(T // tm,), in_specs=[_rows(tm, C), _rows(tm, C), _rows(tm, C), _vec(C)],
        out_specs=[_rows(tm, C), _rows(tm, C), _vec(C)],
        out_shape=[jax.ShapeDtypeStruct((T, C), BF16), jax.ShapeDtypeStruct((T, C), BF16),
                   jax.ShapeDtypeStruct((1, C), F32)],
        compiler_params=_cp(("arbitrary",)))(dh, zg, pe, g)


PLE_W = 256


def ple_fwd(h, xn, pb, GB, goff, GC, g, name, g_next=None):
    T, C = h.shape
    tm = min(512, T)
    more = g_next is not None

    def body(h_ref, x_ref, p_ref, wg_ref, wp_ref, g_ref, *rest):
        if more:
            gn_ref, z_ref, pe_ref, o_ref, xn_ref = rest
        else:
            z_ref, pe_ref, o_ref = rest
        zg = _dot_nn(x_ref[:, :PLE_W], wg_ref[0])
        for s in range(1, NCHIP):
            zg = zg + _dot_nn(x_ref[:, s * PLE_W:(s + 1) * PLE_W], wg_ref[s])
        z_ref[...] = zg
        for s in range(NCHIP):
            pe_ref[:, s * PLE_W:(s + 1) * PLE_W] = _dot_nn(p_ref[...], wp_ref[s])
        e = pe_ref[...] * _sig(zg)
        r = lax.rsqrt(jnp.mean(e * e, axis=-1, keepdims=True) + EPS)
        hn = h_ref[...] + e * r * g_ref[...]
        o_ref[...] = hn
        if more:
            rn = lax.rsqrt(jnp.mean(hn * hn, axis=-1, keepdims=True) + EPS)
            xn_ref[...] = (hn * rn * gn_ref[...]).astype(xn_ref.dtype)

    return pl.pallas_call(
        body, name=name, grid=(T // tm,),
        in_specs=[_rows(tm, C), _rows(tm, C), _rows(tm, PLE_W),
                  pl.BlockSpec((NCHIP, PLE_W, C), lambda i: (0, goff // PLE_W, 0)),
                  pl.BlockSpec((NCHIP, PLE_W, PLE_W), lambda i: (0, 0, 0)), _vec(C)] + ([_vec(C)] if more else []),
        out_specs=[_rows(tm, C)] * (4 if more else 3),
        out_shape=[jax.ShapeDtypeStruct((T, C), F32)] * 3 + ([jax.ShapeDtypeStruct((T, C), BF16)] if more else []),
        compiler_params=_cp(("parallel",)))(*((h, xn, pb, GB, GC, g) + ((g_next,) if more else ())))


def ple_bwd(dh, zg, pe, h_in, GB, goff, g, g_pre, name):
    T, C = dh.shape
    tm = min(512, T)

    def body(d_ref, z_ref, p_ref, h_ref, wg_ref, g_ref, gp_ref, dz_ref, dp_ref, o_ref, dg_ref, dgp_ref, dxn):
        i = pl.program_id(0)
        s = _sig(z_ref[...])
        pe_ = p_ref[...]
        e = pe_ * s
        r = lax.rsqrt(jnp.mean(e * e, axis=-1, keepdims=True) + EPS)
        d = d_ref[...]
        t = d * g_ref[...]
        de = r * t - e * (r * r * r * jnp.mean(t * e, axis=-1, keepdims=True))
        dp_ref[...] = (de * s).astype(dp_ref.dtype)
        dz = (de * pe_ * s * (1.0 - s)).astype(dz_ref.dtype)
        dz_ref[...] = dz
        _acc_rows(dg_ref, i, jnp.sum(d * e * r, axis=0, keepdims=True))
        for k in range(NCHIP):
            dxn[:, k * PLE_W:(k + 1) * PLE_W] = _dot_nt(dz, wg_ref[k])
        x = h_ref[...]
        rx = lax.rsqrt(jnp.mean(x * x, axis=-1, keepdims=True) + EPS)
        xh = x * rx
        dx = dxn[...]
        tx = dx * gp_ref[...]
        o_ref[...] = d + rx * (tx - xh * jnp.mean(tx * xh, axis=-1, keepdims=True))
        _acc_rows(dgp_ref, i, jnp.sum(dx * xh, axis=0, keepdims=True))

    return pl.pallas_call(
        body, name=name, grid=(T // tm,),
        in_specs=[_rows(tm, C), _rows(tm, C), _rows(tm, C), _rows(tm, C),
                  pl.BlockSpec((NCHIP, PLE_W, C), lambda i: (0, goff // PLE_W, 0)), _vec(C), _vec(C)],
        out_specs=[_rows(tm, C), _rows(tm, C), _rows(tm, C), _vec(C), _vec(C)],
        out_shape=[jax.ShapeDtypeStruct((T, C), BF16), jax.ShapeDtypeStruct((T, C), BF16),
                   jax.ShapeDtypeStruct((T, C), F32), jax.ShapeDtypeStruct((1, C), F32),
                   jax.ShapeDtypeStruct((1, C), F32)],
        scratch_shapes=[pltpu.VMEM((tm, C), F32)],
        compiler_params=_cp(("arbitrary",)))(dh, zg, pe, h_in, GB, g, g_pre)


def loss_head(h, tgt, name):
    T, C = h.shape
    tm = min(512, T)

    def body(h_ref, t_ref, d_ref, l_ref):
        i = pl.program_id(0)
        e = h_ref[...] - t_ref[...]
        d_ref[...] = e * (1.0 / C)
        _acc_rows(l_ref, i, jnp.sum(e * e, axis=0, keepdims=True))

    return pl.pallas_call(
        body, name=name, grid=(T // tm,), in_specs=[_rows(tm, C), _rows(tm, C)],
        out_specs=[_rows(tm, C), _vec(C)],
        out_shape=[jax.ShapeDtypeStruct((T, C), F32), jax.ShapeDtypeStruct((1, C), F32)],
        compiler_params=_cp(("arbitrary",)))(h, tgt)


AH, AG_N, AGW, CHUNK = 3072, 12, 256, 128


def _tril_bf16(w):
    r = lax.broadcasted_iota(jnp.int32, (CHUNK, CHUNK), 0)
    c = lax.broadcasted_iota(jnp.int32, (CHUNK, CHUNK), 1)
    return jnp.where(r >= c, w, 0.0).astype(BF16)


def _ln_stats(vs_ref, width):
    v = vs_ref[...]
    mu = jnp.sum(v, axis=-1, keepdims=True) * (1.0 / width)
    vc = v - mu
    var = jnp.sum(vc * vc, axis=-1, keepdims=True) * (1.0 / width)
    return mu, lax.rsqrt(var + EPS)


def gmlp_mid_fwd(zpre, vg, vb, ws, bsf, name):
    T = zpre.shape[0]

    def body(z_ref, vg_ref, vb_ref, ws_ref, bs_ref, y_ref, vs_ref):
        for g in range(AG_N):
            vs_ref[:, g * AGW:(g + 1) * AGW] = _gelu(z_ref[:, AH + g * AGW:AH + (g + 1) * AGW].astype(F32))
        mu, rstd = _ln_stats(vs_ref, AH)
        for g in range(AG_N):
            sl = slice(g * AGW, (g + 1) * AGW)
            vn = ((vs_ref[:, sl] - mu) * rstd * vg_ref[:, sl] + vb_ref[:, sl]).astype(BF16)
            sv = _dot_nn(_tril_bf16(ws_ref[g]), vn) + bs_ref[g]
            u = _gelu(z_ref[:, sl].astype(F32))
            y_ref[:, sl] = (u * sv).astype(y_ref.dtype)

    return pl.pallas_call(
        body, name=name, grid=(T // CHUNK,),
        in_specs=[_rows(CHUNK, 2 * AH), _vec(AH), _vec(AH),
                  pl.BlockSpec((AG_N, CHUNK, CHUNK), lambda i: (0, 0, 0)),
                  pl.BlockSpec((AG_N, CHUNK, AGW), lambda i: (0, 0, 0))],
        out_specs=_rows(CHUNK, AH), out_shape=jax.ShapeDtypeStruct((T, AH), BF16),
        scratch_shapes=[pltpu.VMEM((CHUNK, AH), F32)],
        compiler_params=_cp(("parallel",)))(zpre, vg, vb, ws, bsf)


def gmlp_mid_bwd(zpre, dy, vg, vb, ws, bsf, name):
    T = zpre.shape[0]

    def body(z_ref, dy_ref, vg_ref, vb_ref, ws_ref, bs_ref, dz_ref, dws_ref, dbs_ref, dvg_ref, dvb_ref,
             vs_ref, dvn_ref):
        i = pl.program_id(0)

        @pl.when(i == 0)
        def _():
            dws_ref[...] = jnp.zeros_like(dws_ref)
            dbs_ref[...] = jnp.zeros_like(dbs_ref)
            dvg_ref[...] = jnp.zeros_like(dvg_ref)
            dvb_ref[...] = jnp.zeros_like(dvb_ref)

        for g in range(AG_N):
            vs_ref[:, g * AGW:(g + 1) * AGW] = _gelu(z_ref[:, AH + g * AGW:AH + (g + 1) * AGW].astype(F32))
        mu, rstd = _ln_stats(vs_ref, AH)
        r_i = lax.broadcasted_iota(jnp.int32, (CHUNK, CHUNK), 0)
        c_i = lax.broadcasted_iota(jnp.int32, (CHUNK, CHUNK), 1)
        ones8 = jnp.ones((8, AGW), F32)
        m1 = jnp.zeros((CHUNK, 1), F32)
        m2 = jnp.zeros((CHUNK, 1), F32)
        for g in range(AG_N):
            sl = slice(g * AGW, (g + 1) * AGW)
            vh = (vs_ref[:, sl] - mu) * rstd
            vn = (vh * vg_ref[:, sl] + vb_ref[:, sl]).astype(BF16)
            wm = _tril_bf16(ws_ref[g])
            sv = _dot_nn(wm, vn) + bs_ref[g]
            zu = z_ref[:, sl].astype(F32)
            u = _gelu(zu)
            dyg = dy_ref[:, sl].astype(F32)
            dz_ref[:, sl] = (dyg * sv * _gelu_grad(zu)).astype(dz_ref.dtype)
            dsv = dyg * u
            dsv_b = dsv.astype(BF16)
            dws_ref[g] += jnp.where(r_i >= c_i, _dot_nt(dsv_b, vn), 0.0)
            dbs_ref[g] += _dot_nt(ones8, dsv)
            dvn = _dot_tn(wm, dsv_b)
            dvn_ref[:, sl] = dvn
            dvh = dvn * vg_ref[:, sl]
            m1 = m1 + jnp.sum(dvh, axis=-1, keepdims=True)
            m2 = m2 + jnp.sum(dvh * vh, axis=-1, keepdims=True)
            dvg_ref[:, sl] += jnp.sum(dvn * vh, axis=0, keepdims=True)
            dvb_ref[:, sl] += jnp.sum(dvn, axis=0, keepdims=True)
        m1 = m1 * (1.0 / AH)
        m2 = m2 * (1.0 / AH)
        for g in range(AG_N):
            sl = slice(g * AGW, (g + 1) * AGW)
            vh = (vs_ref[:, sl] - mu) * rstd
            dv = rstd * (dvn_ref[:, sl] * vg_ref[:, sl] - m1 - vh * m2)
            zv = z_ref[:, AH + g * AGW:AH + (g + 1) * AGW].astype(F32)
            dz_ref[:, AH + g * AGW:AH + (g + 1) * AGW] = (dv * _gelu_grad(zv)).astype(dz_ref.dtype)

    full3 = lambda a, b, c: pl.BlockSpec((a, b, c), lambda i: (0, 0, 0))
    return pl.pallas_call(
        body, name=name, grid=(T // CHUNK,),
        in_specs=[_rows(CHUNK, 2 * AH), _rows(CHUNK, AH), _vec(AH), _vec(AH),
                  full3(AG_N, CHUNK, CHUNK), full3(AG_N, CHUNK, AGW)],
        out_specs=[_rows(CHUNK, 2 * AH), full3(AG_N, CHUNK, CHUNK), full3(AG_N, 8, CHUNK), _vec(AH), _vec(AH)],
        out_shape=[jax.ShapeDtypeStruct((T, 2 * AH), BF16), jax.ShapeDtypeStruct((AG_N, CHUNK, CHUNK), F32),
                   jax.ShapeDtypeStruct((AG_N, 8, CHUNK), F32), jax.ShapeDtypeStruct((1, AH), F32),
                   jax.ShapeDtypeStruct((1, AH), F32)],
        scratch_shapes=[pltpu.VMEM((CHUNK, AH), F32), pltpu.VMEM((CHUNK, AH), F32)],
        compiler_params=_cp(("arbitrary",)))(zpre, dy, vg, vb, ws, bsf)


SLAB = 256
NSLAB = DM // SLAB
RC = 256
PAD = 32


def _col(T, j):
    return pl.BlockSpec((T, SLAB), lambda c: (0, j * NSLAB + c))


def _chunks(T, fn):
    def step(i, carry):
        fn(pl.multiple_of(i * RC, RC))
        return carry
    lax.fori_loop(0, T // RC, step, 0)


def _conv_taps(K):
    return [(r, [q for q in range(4) if 8 * q + r < K]) for r in range(min(8, K))]


def _causal_conv(zpad_ref, wrow, K, r0):
    acc = None
    for r, qs in _conv_taps(K):
        a = None
        for q in qs:
            term = wrow(8 * q + r) * zpad_ref[pl.ds(r0 + (PAD - 8 - 8 * q), RC + 8), :]
            a = term if a is None else a + term
        a = a if r == 0 else pltpu.roll(a, r, 0)
        acc = a if acc is None else acc + a
    return acc[8:, :]


def _anticausal_conv(gpad_ref, wrow, K, r0):
    acc = None
    for r, qs in _conv_taps(K):
        b = None
        for q in qs:
            term = wrow(8 * q + r) * gpad_ref[pl.ds(r0 + 8 * q, RC + 8), :]
            b = term if b is None else b + term
        b = b if r == 0 else pltpu.roll(b, RC + 8 - r, 0)
        acc = b if acc is None else acc + b
    return acc[:RC, :]


def _conv_dw(gpad_ref, zpad_ref, dw_ref, K, r0):
    for r, qs in _conv_taps(K):
        gw = gpad_ref[pl.ds(r0, RC + 8), :]
        p = (gw if r == 0 else pltpu.roll(gw, RC + 8 - r, 0))[:RC, :]
        for q in qs:
            z = zpad_ref[pl.ds(r0 + (PAD - 8 * q), RC), :]
            dw_ref[8 * q + r] += jnp.sum((p * z).reshape(RC // 8, 8, SLAB), axis=0)


def _zero_rows(ref, start, n):
    ref[pl.ds(start, n), :] = jnp.zeros((n, SLAB), F32)


def pool_fwd(hn, wg, sc, name):
    T = hn.shape[0]

    def body(h_ref, w_ref, s_ref, p_ref, yp_ref, y_ref, xpad):
        g = pl.program_id(0)
        wf = jnp.left_shift(2, g).astype(F32)
        _zero_rows(xpad, 0, PAD)

        def fill(r0):
            xpad[pl.ds(r0 + PAD, RC), :] = h_ref[pl.ds(r0, RC), :]
        _chunks(T, fill)

        def step(r0):
            w = xpad[pl.ds(r0 + (PAD - 16), RC + 16), :]
            s2 = w + pltpu.roll(w, 1, 0)
            s4 = s2 + pltpu.roll(s2, 2, 0)
            s8 = s4 + pltpu.roll(s4, 4, 0)
            s16 = s8 + pltpu.roll(s8, 8, 0)
            sel = jnp.where(g == 0, s2, jnp.where(g == 1, s4, jnp.where(g == 2, s8, s16)))[16:, :]
            t1 = (r0 + 1 + lax.broadcasted_iota(jnp.int32, (RC, SLAB), 0)).astype(F32)
            pooled = (sel / jnp.minimum(t1, wf) - w[16:, :]).astype(BF16)
            p_ref[pl.ds(r0, RC), :] = pooled
            yp = _dot_nn(pooled, w_ref[...])
            yp_ref[pl.ds(r0, RC), :] = yp
            y_ref[pl.ds(r0, RC), :] = yp * s_ref[...]
        _chunks(T, step)

    slab = pl.BlockSpec((T, SLAB), lambda c: (0, c))
    return pl.pallas_call(
        body, name=name, grid=(NSLAB,),
        in_specs=[slab, pl.BlockSpec((None, SLAB, SLAB), lambda c: (c, 0, 0)), pl.BlockSpec((1, SLAB), lambda c: (0, c))],
        out_specs=[slab, slab, slab],
        out_shape=[jax.ShapeDtypeStruct((T, DM), BF16), jax.ShapeDtypeStruct((T, DM), F32),
                   jax.ShapeDtypeStruct((T, DM), F32)],
        scratch_shapes=[pltpu.VMEM((T + PAD, SLAB), F32)],
        compiler_params=_cp(("parallel",)))(hn, wg, sc)


def pool_bwd(dy, ypre, pooled, wg, sc, name):
    T = dy.shape[0]

    def body(d_ref, yp_ref, p_ref, w_ref, s_ref, dh_ref, dw_ref, ds_ref, qpad, dwacc, dsacc):
        g = pl.program_id(0)
        wf = jnp.left_shift(2, g).astype(F32)
        dwacc[...] = jnp.zeros_like(dwacc)
        dsacc[...] = jnp.zeros_like(dsacc)
        _zero_rows(qpad, T, PAD)

        def first(r0):
            d = d_ref[pl.ds(r0, RC), :]
            dsacc[...] += jnp.sum((d * yp_ref[pl.ds(r0, RC), :]).reshape(RC // 8, 8, SLAB), axis=0)
            dyp = (d * s_ref[...]).astype(BF16)
            dpool = _dot_nt(dyp, w_ref[...])
            dwacc[...] += _dot_tn(p_ref[pl.ds(r0, RC), :], dyp)
            t1 = (r0 + 1 + lax.broadcasted_iota(jnp.int32, (RC, SLAB), 0)).astype(F32)
            qpad[pl.ds(r0, RC), :] = dpool / jnp.minimum(t1, wf)
            dh_ref[pl.ds(r0, RC), :] = dpool
        _chunks(T, first)

        def second(r0):
            w = qpad[pl.ds(r0, RC + 16), :]
            n = RC + 16
            a2 = w + pltpu.roll(w, n - 1, 0)
            a4 = a2 + pltpu.roll(a2, n - 2, 0)
            a8 = a4 + pltpu.roll(a4, n - 4, 0)
            a16 = a8 + pltpu.roll(a8, n - 8, 0)
            sel = jnp.where(g == 0, a2, jnp.where(g == 1, a4, jnp.where(g == 2, a8, a16)))[:RC, :]
            dh_ref[pl.ds(r0, RC), :] = sel - dh_ref[pl.ds(r0, RC), :]
        _chunks(T, second)
        dw_ref[...] = dwacc[...]
        ds_ref[...] = jnp.sum(dsacc[...], axis=0, keepdims=True)

    slab = pl.BlockSpec((T, SLAB), lambda c: (0, c))
    wspec = pl.BlockSpec((None, SLAB, SLAB), lambda c: (c, 0, 0))
    vec = pl.BlockSpec((1, SLAB), lambda c: (0, c))
    return pl.pallas_call(
        body, name=name, grid=(NSLAB,),
        in_specs=[slab, slab, slab, wspec, vec],
        out_specs=[slab, wspec, vec],
        out_shape=[jax.ShapeDtypeStruct((T, DM), F32), jax.ShapeDtypeStruct((NSLAB, SLAB, SLAB), F32),
                   jax.ShapeDtypeStruct((1, DM), F32)],
        scratch_shapes=[pltpu.VMEM((T + PAD, SLAB), F32), pltpu.VMEM((SLAB, SLAB), F32), pltpu.VMEM((8, SLAB), F32)],
        compiler_params=_cp(("parallel",)))(dy, ypre, pooled, wg, sc)


KC = 31
KD = 3


def conf_conv_fwd(ag, wdw, bdw, name):
    T = ag.shape[0]

    def body(a_ref, g_ref, w_ref, b_ref, o_ref, zpad):
        _zero_rows(zpad, 0, PAD)

        def fill(r0):
            a = a_ref[pl.ds(r0, RC), :].astype(F32)
            gt = g_ref[pl.ds(r0, RC), :].astype(F32)
            zpad[pl.ds(r0 + PAD, RC), :] = a * _sig(gt)
        _chunks(T, fill)
        wrow = lambda j: w_ref[KC - 1 - j:KC - j, :]

        def step(r0):
            o_ref[pl.ds(r0, RC), :] = _causal_conv(zpad, wrow, KC, r0) + b_ref[...]
        _chunks(T, step)

    vec = pl.BlockSpec((1, SLAB), lambda c: (0, c))
    return pl.pallas_call(
        body, name=name, grid=(NSLAB,),
        in_specs=[_col(T, 0), _col(T, 1), pl.BlockSpec((32, SLAB), lambda c: (0, c)), vec],
        out_specs=pl.BlockSpec((T, SLAB), lambda c: (0, c)),
        out_shape=jax.ShapeDtypeStruct((T, DM), F32),
        scratch_shapes=[pltpu.VMEM((T + PAD, SLAB), F32)],
        compiler_params=_cp(("parallel",)))(ag, ag, wdw, bdw)


def conf_conv_bwd(dzc, ag, wdw, name):
    T = ag.shape[0]

    def body(d_ref, a_ref, g_ref, w_ref, da_ref, dg_ref, dw_ref, db_ref, zpad, gpad, dwacc, dbacc):
        _zero_rows(zpad, 0, PAD)
        _zero_rows(gpad, T, PAD)
        dwacc[...] = jnp.zeros_like(dwacc)
        dbacc[...] = jnp.zeros_like(dbacc)

        def fill(r0):
            a = a_ref[pl.ds(r0, RC), :].astype(F32)
            gt = g_ref[pl.ds(r0, RC), :].astype(F32)
            zpad[pl.ds(r0 + PAD, RC), :] = a * _sig(gt)
            d = d_ref[pl.ds(r0, RC), :]
            gpad[pl.ds(r0, RC), :] = d
            dbacc[...] += jnp.sum(d.reshape(RC // 8, 8, SLAB), axis=0)
        _chunks(T, fill)
        wrow = lambda j: w_ref[KC - 1 - j:KC - j, :]

        def step(r0):
            dz = _anticausal_conv(gpad, wrow, KC, r0)
            a = a_ref[pl.ds(r0, RC), :].astype(F32)
            s = _sig(g_ref[pl.ds(r0, RC), :].astype(F32))
            da_ref[pl.ds(r0, RC), :] = (dz * s).astype(da_ref.dtype)
            dg_ref[pl.ds(r0, RC), :] = (dz * a * s * (1.0 - s)).astype(dg_ref.dtype)
            _conv_dw(gpad, zpad, dwacc, KC, r0)
        _chunks(T, step)
        dw_ref[...] = jnp.zeros_like(dw_ref)
        for k in range(KC):
            dw_ref[k:k + 1, :] = jnp.sum(dwacc[KC - 1 - k], axis=0, keepdims=True)
        db_ref[...] = jnp.sum(dbacc[...], axis=0, keepdims=True)

    vec = pl.BlockSpec((1, SLAB), lambda c: (0, c))
    w32 = pl.BlockSpec((32, SLAB), lambda c: (0, c))
    return pl.pallas_call(
        body, name=name, grid=(NSLAB,),
        in_specs=[pl.BlockSpec((T, SLAB), lambda c: (0, c)), _col(T, 0), _col(T, 1), w32],
        out_specs=[_col(T, 0), _col(T, 0), w32, vec],
        out_shape=[jax.ShapeDtypeStruct((T, DM), BF16), jax.ShapeDtypeStruct((T, DM), BF16),
                   jax.ShapeDtypeStruct((32, DM), F32), jax.ShapeDtypeStruct((1, DM), F32)],
        scratch_shapes=[pltpu.VMEM((T + PAD, SLAB), F32), pltpu.VMEM((T + PAD, SLAB), F32),
                        pltpu.VMEM((32, 8, SLAB), F32), pltpu.VMEM((8, SLAB), F32)],
        compiler_params=_cp(("parallel",)))(dzc, ag, ag, wdw)


def conf_ln_fwd(zc, g, b, name):
    T, C = zc.shape
    tm = min(512, T)

    def body(z_ref, g_ref, b_ref, o_ref):
        x = z_ref[...]
        xc = x - jnp.mean(x, axis=-1, keepdims=True)
        r = lax.rsqrt(jnp.mean(xc * xc, axis=-1, keepdims=True) + EPS)
        zl = xc * r * g_ref[...] + b_ref[...]
        o_ref[...] = (zl * _sig(zl)).astype(o_ref.dtype)

    return pl.pallas_call(
        body, name=name, grid=(T // tm,), in_specs=[_rows(tm, C), _vec(C), _vec(C)], out_specs=_rows(tm, C),
        out_shape=jax.ShapeDtypeStruct((T, C), BF16), compiler_params=_cp(("parallel",)))(zc, g, b)


def conf_ln_bwd(dzs, zc, g, b, name):
    T, C = zc.shape
    tm = min(512, T)

    def body(d_ref, z_ref, g_ref, b_ref, o_ref, dg_ref, db_ref):
        i = pl.program_id(0)
        x = z_ref[...]
        xc = x - jnp.mean(x, axis=-1, keepdims=True)
        r = lax.rsqrt(jnp.mean(xc * xc, axis=-1, keepdims=True) + EPS)
        xh = xc * r
        zl = xh * g_ref[...] + b_ref[...]
        s = _sig(zl)
        dzl = d_ref[...].astype(F32) * (s * (1.0 + zl * (1.0 - s)))
        t = dzl * g_ref[...]
        o_ref[...] = r * (t - jnp.mean(t, axis=-1, keepdims=True) - xh * jnp.mean(t * xh, axis=-1, keepdims=True))
        _acc_rows(dg_ref, i, jnp.sum(dzl * xh, axis=0, keepdims=True))
        _acc_rows(db_ref, i, jnp.sum(dzl, axis=0, keepdims=True))

    return pl.pallas_call(
        body, name=name, grid=(T // tm,), in_specs=[_rows(tm, C), _rows(tm, C), _vec(C), _vec(C)],
        out_specs=[_rows(tm, C), _vec(C), _vec(C)],
        out_shape=[jax.ShapeDtypeStruct((T, C), F32), jax.ShapeDtypeStruct((1, C), F32),
                   jax.ShapeDtypeStruct((1, C), F32)],
        compiler_params=_cp(("arbitrary",)))(dzs, zc, g, b)


def sconv_fwd(bgx, wc, name):
    T = bgx.shape[0]

    def body(b_ref, c_ref, x_ref, w_ref, o_ref, zpad):
        _zero_rows(zpad, 0, PAD)

        def fill(r0):
            zpad[pl.ds(r0 + PAD, RC), :] = c_ref[pl.ds(r0, RC), :].astype(F32) * x_ref[pl.ds(r0, RC), :].astype(F32)
        _chunks(T, fill)
        wrow = lambda j: w_ref[KD - 1 - j:KD - j, :]

        def step(r0):
            qc = _causal_conv(zpad, wrow, KD, r0)
            o_ref[pl.ds(r0, RC), :] = (b_ref[pl.ds(r0, RC), :].astype(F32) * qc).astype(o_ref.dtype)
        _chunks(T, step)

    return pl.pallas_call(
        body, name=name, grid=(NSLAB,),
        in_specs=[_col(T, 0), _col(T, 1), _col(T, 2), pl.BlockSpec((8, SLAB), lambda c: (0, c))],
        out_specs=pl.BlockSpec((T, SLAB), lambda c: (0, c)),
        out_shape=jax.ShapeDtypeStruct((T, DM), BF16),
        scratch_shapes=[pltpu.VMEM((T + PAD, SLAB), F32)],
        compiler_params=_cp(("parallel",)))(bgx, bgx, bgx, wc)


def sconv_bwd(dy, bgx, wc, name):
    T = bgx.shape[0]

    def body(d_ref, b_ref, c_ref, x_ref, w_ref, db_ref, dc_ref, dx_ref, dw_ref, zpad, gpad, dwacc):
        _zero_rows(zpad, 0, PAD)
        _zero_rows(gpad, T, PAD)
        dwacc[...] = jnp.zeros_like(dwacc)

        def fill(r0):
            zpad[pl.ds(r0 + PAD, RC), :] = c_ref[pl.ds(r0, RC), :].astype(F32) * x_ref[pl.ds(r0, RC), :].astype(F32)
            gpad[pl.ds(r0, RC), :] = d_ref[pl.ds(r0, RC), :].astype(F32) * b_ref[pl.ds(r0, RC), :].astype(F32)
        _chunks(T, fill)
        wrow = lambda j: w_ref[KD - 1 - j:KD - j, :]

        def step(r0):
            qc = _causal_conv(zpad, wrow, KD, r0)
            db_ref[pl.ds(r0, RC), :] = (d_ref[pl.ds(r0, RC), :].astype(F32) * qc).astype(db_ref.dtype)
            dq = _anticausal_conv(gpad, wrow, KD, r0)
            dc_ref[pl.ds(r0, RC), :] = (dq * x_ref[pl.ds(r0, RC), :].astype(F32)).astype(dc_ref.dtype)
            dx_ref[pl.ds(r0, RC), :] = (dq * c_ref[pl.ds(r0, RC), :].astype(F32)).astype(dx_ref.dtype)
            _conv_dw(gpad, zpad, dwacc, KD, r0)
        _chunks(T, step)
        dw_ref[...] = jnp.zeros_like(dw_ref)
        for k in range(KD):
            dw_ref[k:k + 1, :] = jnp.sum(dwacc[KD - 1 - k], axis=0, keepdims=True)

    w8 = pl.BlockSpec((8, SLAB), lambda c: (0, c))
    return pl.pallas_call(
        body, name=name, grid=(NSLAB,),
        in_specs=[pl.BlockSpec((T, SLAB), lambda c: (0, c)), _col(T, 0), _col(T, 1), _col(T, 2), w8],
        out_specs=[_col(T, 0), _col(T, 0), _col(T, 0), w8],
        out_shape=[jax.ShapeDtypeStruct((T, DM), BF16)] * 3 + [jax.ShapeDtypeStruct((8, DM), F32)],
        scratch_shapes=[pltpu.VMEM((T + PAD, SLAB), F32), pltpu.VMEM((T + PAD, SLAB), F32),
                        pltpu.VMEM((8, 8, SLAB), F32)],
        compiler_params=_cp(("parallel",)))(dy, bgx, bgx, bgx, wc)


def merge_cols(parts, name):
    T = parts[0].shape[0]
    n = len(parts)
    C = n * DM
    tm = min(512, T)

    def body(*refs):
        o_ref = refs[n]
        for j in range(n):
            o_ref[:, j * DM:(j + 1) * DM] = refs[j][...]

    return pl.pallas_call(
        body, name=name, grid=(T // tm,),
        in_specs=[_rows(tm, DM) for j in range(n)],
        out_specs=_rows(tm, C), out_shape=jax.ShapeDtypeStruct((T, C), parts[0].dtype),
        compiler_params=_cp(("parallel",)))(*parts)


def adamw(w, g, m, v, name):
    shape = w.shape
    R, C = shape[-2], shape[-1]
    L = w.size // (R * C)
    w2, g2, m2, v2 = (a.reshape(L, R, C) for a in (w, g, m, v))
    tr = R
    while tr * C > 512 * 1024 and tr % 16 == 0:
        tr //= 2
    bc1 = 1.0 - ADAM_B1 ** ADAM_STEP
    bc2 = 1.0 - ADAM_B2 ** ADAM_STEP

    def body(w_ref, g_ref, m_ref, v_ref, d_ref, nm_ref, nv_ref):
        gg = g_ref[...]
        nm = ADAM_B1 * m_ref[...] + (1.0 - ADAM_B1) * gg
        nv = ADAM_B2 * v_ref[...] + (1.0 - ADAM_B2) * (gg * gg)
        nm_ref[...] = nm
        nv_ref[...] = nv
        d_ref[...] = -ADAM_LR * ((nm / bc1) / (jnp.sqrt(nv / bc2) + ADAM_EPS) + ADAM_WD * w_ref[...])

    spec = pl.BlockSpec((None, tr, C), lambda l, i: (l, i, 0))
    outs = pl.pallas_call(
        body, name=name, grid=(L, R // tr), in_specs=[spec] * 4, out_specs=[spec] * 3,
        out_shape=[jax.ShapeDtypeStruct((L, R, C), F32)] * 3,
        compiler_params=_cp(("parallel", "parallel")))(w2, g2, m2, v2)
    return tuple(o.reshape(shape) for o in outs)


def _place():
    x, y, c = lax.axis_index("x"), lax.axis_index("y"), lax.axis_index("c")
    return x, y, c


def all_gather_chips(bufs, name):
    n = len(bufs)
    me_ = 2 * lax.axis_index("x") + lax.axis_index("y")
    slots = [lax.dynamic_update_slice(lax.empty((NCHIP,) + b.shape, b.dtype), b[None], (me_, 0, 0)) for b in bufs]

    def body(*refs):
        dst = refs[n:2 * n]
        send, recv = refs[2 * n:]
        x, y, c = _place()
        me = 2 * x + y
        sib = (x, y, 1 - c)
        chips = [(1 - x, y), (x, 1 - y), (1 - x, 1 - y)]

        def half(b, slot, hc):
            rows = bufs[b].shape[0] // 2
            return dst[b].at[slot, pl.ds(hc * rows, rows), :]

        def remote(k, s, d, to):
            return pltpu.make_async_remote_copy(src_ref=s, dst_ref=d, send_sem=send.at[k], recv_sem=recv.at[k],
                                                device_id=to, device_id_type=MESH)

        first = []
        for b in range(n):
            for j, (cx, cy) in enumerate(chips):
                first.append(remote(b * 6 + j, half(b, me, c), half(b, me, c), (cx, cy, c)))
        for cp in first:
            cp.start()
        passed = []
        for b in range(n):
            for j, (cx, cy) in enumerate(chips):
                slot = 2 * cx + cy
                remote(b * 6 + j, half(b, slot, c), half(b, slot, c), (cx, cy, c)).wait_recv()
                fwd = remote(b * 6 + 3 + j, half(b, slot, c), half(b, slot, c), sib)
                fwd.start()
                passed.append(fwd)
        for b in range(n):
            for j, (cx, cy) in enumerate(chips):
                slot = 2 * cx + cy
                remote(b * 6 + 3 + j, half(b, slot, 1 - c), half(b, slot, 1 - c), sib).wait_recv()
        for cp in first + passed:
            cp.wait_send()

    return pl.pallas_call(
        body, name=name, in_specs=[ANY] * n, out_specs=[ANY] * n,
        out_shape=[jax.ShapeDtypeStruct(s.shape, s.dtype) for s in slots],
        input_output_aliases={b: b for b in range(n)},
        scratch_shapes=[pltpu.SemaphoreType.DMA((6 * n,)), pltpu.SemaphoreType.DMA((6 * n,))],
        compiler_params=pltpu.CompilerParams())(*slots)


def pair_exchange(bufs, name):
    n = len(bufs)

    def body(*refs):
        src, dst = refs[:n], refs[n:2 * n]
        send, recv = refs[2 * n:]
        x, y, c = _place()
        cps = []
        for b in range(n):
            rows = bufs[b].shape[1] // 2
            cp = pltpu.make_async_remote_copy(
                src_ref=src[b].at[:, pl.ds((1 - c) * rows, rows), :], dst_ref=dst[b],
                send_sem=send.at[b], recv_sem=recv.at[b], device_id=(x, y, 1 - c), device_id_type=MESH)
            cp.start()
            cps.append(cp)
        for cp in cps:
            cp.wait()

    return pl.pallas_call(
        body, name=name, in_specs=[ANY] * n, out_specs=[ANY] * n,
        out_shape=[jax.ShapeDtypeStruct((NCHIP, b.shape[1] // 2, b.shape[2]), b.dtype) for b in bufs],
        scratch_shapes=[pltpu.SemaphoreType.DMA((n,)), pltpu.SemaphoreType.DMA((n,))],
        compiler_params=pltpu.CompilerParams())(*bufs)


def add_half(full, got, tr, tc, name):
    _, R, C = full.shape
    rows = R // 2
    nr = rows // tr
    c_arr = lax.axis_index("c").astype(jnp.int32).reshape(1)

    def body(c_ref, a_ref, b_ref, o_ref):
        o_ref[...] = (a_ref[...].astype(F32) + b_ref[...].astype(F32)).astype(o_ref.dtype)

    return pl.pallas_call(
        body, name=name,
        grid_spec=pltpu.PrefetchScalarGridSpec(
            num_scalar_prefetch=1, grid=(NCHIP, nr, C // tc),
            in_specs=[pl.BlockSpec((None, tr, tc), lambda s, i, j, c_ref: (s, c_ref[0] * nr + i, j)),
                      pl.BlockSpec((None, tr, tc), lambda s, i, j, c_ref: (s, i, j))],
            out_specs=pl.BlockSpec((None, tr, tc), lambda s, i, j, c_ref: (s, i, j))),
        out_shape=jax.ShapeDtypeStruct((NCHIP, rows, C), full.dtype),
        compiler_params=_cp(("parallel", "parallel", "parallel")))(c_arr, full, got)


def chip_exchange(bufs, name):
    n = len(bufs)

    def body(*refs):
        src, dst = refs[:n], refs[n:2 * n]
        send, recv, lsem = refs[2 * n:]
        x, y, c = _place()
        me = 2 * x + y
        chips = [(1 - x, y), (x, 1 - y), (1 - x, 1 - y)]
        local = [pltpu.make_async_copy(src[b].at[me], dst[b].at[me], lsem.at[b]) for b in range(n)]
        for cp in local:
            cp.start()
        cps = []
        for b in range(n):
            for j, (cx, cy) in enumerate(chips):
                cp = pltpu.make_async_remote_copy(
                    src_ref=src[b].at[2 * cx + cy], dst_ref=dst[b].at[me],
                    send_sem=send.at[b * 3 + j], recv_sem=recv.at[b * 3 + j],
                    device_id=(cx, cy, c), device_id_type=MESH)
                cp.start()
                cps.append((cp, b, cx, cy, j))
        for cp, b, cx, cy, j in cps:
            cp.wait_send()
            pltpu.make_async_remote_copy(
                src_ref=src[b].at[me], dst_ref=dst[b].at[2 * cx + cy],
                send_sem=send.at[b * 3 + j], recv_sem=recv.at[b * 3 + j],
                device_id=(cx, cy, c), device_id_type=MESH).wait_recv()
        for cp in local:
            cp.wait()

    return pl.pallas_call(
        body, name=name, in_specs=[ANY] * n, out_specs=[ANY] * n,
        out_shape=[jax.ShapeDtypeStruct(b.shape, b.dtype) for b in bufs],
        scratch_shapes=[pltpu.SemaphoreType.DMA((3 * n,)), pltpu.SemaphoreType.DMA((3 * n,)),
                        pltpu.SemaphoreType.DMA((n,))],
        compiler_params=pltpu.CompilerParams())(*bufs)


def sum_slots(buf, tr, tc, name):
    _, r, C = buf.shape
    nr = r // tr
    c_arr = lax.axis_index("c").astype(jnp.int32).reshape(1)

    def body(c_ref, a_ref, o_ref):
        o_ref[...] = ((a_ref[0].astype(F32) + a_ref[1].astype(F32)) + a_ref[2].astype(F32)) + a_ref[3].astype(F32)

    return pl.pallas_call(
        body, name=name,
        grid_spec=pltpu.PrefetchScalarGridSpec(
            num_scalar_prefetch=1, grid=(nr, C // tc),
            in_specs=[pl.BlockSpec((NCHIP, tr, tc), lambda i, j, c_ref: (0, i, j))],
            out_specs=pl.BlockSpec((tr, tc), lambda i, j, c_ref: (c_ref[0] * nr + i, j))),
        out_shape=jax.ShapeDtypeStruct((2 * r, C), F32),
        compiler_params=_cp(("parallel", "parallel")))(c_arr, buf)


def pair_share(bufs, name):
    n = len(bufs)

    def body(*refs):
        dst = refs[n:2 * n]
        send, recv = refs[2 * n:]
        x, y, c = _place()
        cps = []
        for b in range(n):
            rows = bufs[b].shape[0] // 2
            here = dst[b].at[pl.ds(c * rows, rows), :]
            cp = pltpu.make_async_remote_copy(src_ref=here, dst_ref=here, send_sem=send.at[b], recv_sem=recv.at[b],
                                              device_id=(x, y, 1 - c), device_id_type=MESH)
            cp.start()
            cps.append((cp, b))
        for cp, b in cps:
            rows = bufs[b].shape[0] // 2
            there = dst[b].at[pl.ds((1 - c) * rows, rows), :]
            cp.wait_send()
            pltpu.make_async_remote_copy(src_ref=there, dst_ref=there, send_sem=send.at[b], recv_sem=recv.at[b],
                                         device_id=(x, y, 1 - c), device_id_type=MESH).wait_recv()

    return pl.pallas_call(
        body, name=name, in_specs=[ANY] * n, out_specs=[ANY] * n,
        out_shape=[jax.ShapeDtypeStruct(b.shape, b.dtype) for b in bufs],
        input_output_aliases={b: b for b in range(n)},
        scratch_shapes=[pltpu.SemaphoreType.DMA((n,)), pltpu.SemaphoreType.DMA((n,))],
        compiler_params=pltpu.CompilerParams())(*bufs)


def _tile(kind, buf):
    return {"A": (256, buf.shape[2]), "B": (buf.shape[1], 1024), "C": (128, 256), "V": (40, 256),
            "E": (40, 1024)}[kind]


def reduce_scatter(parts, tag):
    names = list(parts)
    got = pair_exchange([parts[k] for k in names], tag + "_pair_exchange")
    sums = [add_half(parts[k], got[i], *_tile(k[0], got[i]), name=tag + "_add_pair_" + k) for i, k in enumerate(names)]
    landed = chip_exchange(sums, tag + "_chip_exchange")
    halves = [sum_slots(landed[i], *_tile(k[0], landed[i]), name=tag + "_sum_chips_" + k) for i, k in enumerate(names)]
    full = pair_share(halves, tag + "_pair_share")
    return dict(zip(names, full))


HBM = pl.BlockSpec(memory_space=pltpu.HBM)
SEMS = pl.BlockSpec(memory_space=pltpu.SEMAPHORE)
FLOWS = pltpu.SideEffectType.DATAFLOW_SIDE_EFFECTING


def _in_hbm(a):
    return pltpu.with_memory_space_constraint(a, pltpu.HBM)


def _other_chips():
    x, y, c = _place()
    return 2 * x + y, c, [(1 - x, y), (x, 1 - y), (1 - x, 1 - y)]


def own_slots(bufs):
    me = 2 * lax.axis_index("x") + lax.axis_index("y")
    return [lax.dynamic_update_slice(lax.empty((NCHIP,) + b.shape, b.dtype), b[None], (me, 0, 0)) for b in bufs]


def gather_start(slots, after, name):
    n = len(slots)

    def body(*refs):
        ins = refs[:n]
        send, recv = refs[n + 1], refs[n + 2]
        token = refs[2 * n + 3]
        me, c, chips = _other_chips()
        for b in range(n):
            rows = slots[b].shape[1] // 2
            own = ins[b].at[me, pl.ds(c * rows, rows), :]
            for j, (cx, cy) in enumerate(chips):
                pltpu.make_async_remote_copy(src_ref=own, dst_ref=own, send_sem=send.at[3 * b + j],
                                             recv_sem=recv.at[3 * b + j], device_id=(cx, cy, c),
                                             device_id_type=MESH).start()
        token[...] = jnp.zeros_like(token)

    out = pl.pallas_call(
        body, name=name, in_specs=[HBM] * n + [ANY],
        out_specs=[SEMS, SEMS] + [HBM] * n + [pl.BlockSpec(memory_space=pltpu.VMEM)],
        out_shape=[pltpu.SemaphoreType.DMA((3 * n,)), pltpu.SemaphoreType.DMA((3 * n,))]
        + [pltpu.HBM(s.shape, s.dtype) for s in slots] + [jax.ShapeDtypeStruct((8, 128), F32)],
        input_output_aliases={b: b + 2 for b in range(n)},
        compiler_params=pltpu.CompilerParams(has_side_effects=FLOWS))(*[_in_hbm(s) for s in slots], after)
    return out[0], out[1], list(out[2:2 + n]), out[2 + n]


def gather_wait(send, recv, slots, picks, after, name):
    n = len(slots)

    def body(*refs):
        ins = refs[:n]
        send_, recv_ = refs[n], refs[n + 1]
        me, c, chips = _other_chips()
        for i, b in enumerate(picks):
            rows = slots[i].shape[1] // 2
            own = ins[i].at[me, pl.ds(c * rows, rows), :]
            for j, (cx, cy) in enumerate(chips):
                got = ins[i].at[2 * cx + cy, pl.ds(c * rows, rows), :]
                pltpu.make_async_remote_copy(src_ref=own, dst_ref=own, send_sem=send_.at[3 * b + j],
                                             recv_sem=recv_.at[3 * b + j], device_id=(cx, cy, c),
                                             device_id_type=MESH).wait_send()
                pltpu.make_async_remote_copy(src_ref=got, dst_ref=got, send_sem=send_.at[3 * b + j],
                                             recv_sem=recv_.at[3 * b + j], device_id=(cx, cy, c),
                                             device_id_type=MESH).wait_recv()

    return pl.pallas_call(
        body, name=name, in_specs=[HBM] * n + [SEMS, SEMS, ANY], out_specs=[HBM] * n,
        out_shape=[pltpu.HBM(s.shape, s.dtype) for s in slots],
        input_output_aliases={b: b for b in range(n)},
        compiler_params=pltpu.CompilerParams(has_side_effects=FLOWS))(*slots, send, recv, after)


def gather_pass(slots, name):
    n = len(slots)

    def body(*refs):
        dst = refs[n:2 * n]
        send, recv = refs[2 * n:]
        x, y, c = _place()
        sib = (x, y, 1 - c)
        chips = [(1 - x, y), (x, 1 - y), (1 - x, 1 - y)]

        def half(b, slot, hc):
            rows = slots[b].shape[1] // 2
            return dst[b].at[slot, pl.ds(hc * rows, rows), :]

        passed = []
        for b in range(n):
            for j, (cx, cy) in enumerate(chips):
                slot = 2 * cx + cy
                cp = pltpu.make_async_remote_copy(src_ref=half(b, slot, c), dst_ref=half(b, slot, c),
                                                  send_sem=send.at[3 * b + j], recv_sem=recv.at[3 * b + j],
                                                  device_id=sib, device_id_type=MESH)
                cp.start()
                passed.append(cp)
        for b in range(n):
            for j, (cx, cy) in enumerate(chips):
                slot = 2 * cx + cy
                pltpu.make_async_remote_copy(src_ref=half(b, slot, 1 - c), dst_ref=half(b, slot, 1 - c),
                                             send_sem=send.at[3 * b + j], recv_sem=recv.at[3 * b + j],
                                             device_id=sib, device_id_type=MESH).wait_recv()
        for cp in passed:
            cp.wait_send()

    return pl.pallas_call(
        body, name=name, in_specs=[ANY] * n, out_specs=[ANY] * n,
        out_shape=[jax.ShapeDtypeStruct(s.shape, s.dtype) for s in slots],
        input_output_aliases={b: b for b in range(n)},
        scratch_shapes=[pltpu.SemaphoreType.DMA((3 * n,)), pltpu.SemaphoreType.DMA((3 * n,))],
        compiler_params=pltpu.CompilerParams())(*slots)


def chip_exchange_start(sums, name):
    n = len(sums)
    me_ = 2 * lax.axis_index("x") + lax.axis_index("y")
    landing = [lax.dynamic_update_slice(lax.empty(s.shape, s.dtype),
                                        lax.dynamic_slice(s, (me_, 0, 0), (1,) + s.shape[1:]), (me_, 0, 0)) for s in sums]

    def body(*refs):
        src, land = refs[:n], refs[n:2 * n]
        send, recv = refs[2 * n], refs[2 * n + 1]
        token = refs[4 * n + 2]
        me, c, chips = _other_chips()
        for b in range(n):
            for j, (cx, cy) in enumerate(chips):
                pltpu.make_async_remote_copy(src_ref=src[b].at[2 * cx + cy], dst_ref=land[b].at[me],
                                             send_sem=send.at[3 * b + j], recv_sem=recv.at[3 * b + j],
                                             device_id=(cx, cy, c), device_id_type=MESH).start()
        token[...] = jnp.zeros_like(token)

    out = pl.pallas_call(
        body, name=name, in_specs=[HBM] * (2 * n),
        out_specs=[SEMS, SEMS] + [HBM] * (2 * n) + [pl.BlockSpec(memory_space=pltpu.VMEM)],
        out_shape=[pltpu.SemaphoreType.DMA((3 * n,)), pltpu.SemaphoreType.DMA((3 * n,))]
        + [pltpu.HBM(s.shape, s.dtype) for s in sums + landing] + [jax.ShapeDtypeStruct((8, 128), F32)],
        input_output_aliases={b: b + 2 for b in range(2 * n)},
        compiler_params=pltpu.CompilerParams(has_side_effects=FLOWS))(*[_in_hbm(s) for s in sums + landing])
    return out[0], out[1], list(out[2:2 + n]), list(out[2 + n:2 + 2 * n]), out[2 + 2 * n]


def chip_exchange_wait(send, recv, sums, landing, after, name):
    n = len(sums)

    def body(*refs):
        src, land = refs[:n], refs[n:2 * n]
        send_, recv_ = refs[2 * n], refs[2 * n + 1]
        me, c, chips = _other_chips()
        for b in range(n):
            for j, (cx, cy) in enumerate(chips):
                slot = 2 * cx + cy
                pltpu.make_async_remote_copy(src_ref=src[b].at[slot], dst_ref=land[b].at[me],
                                             send_sem=send_.at[3 * b + j], recv_sem=recv_.at[3 * b + j],
                                             device_id=(cx, cy, c), device_id_type=MESH).wait_send()
                pltpu.make_async_remote_copy(src_ref=src[b].at[me], dst_ref=land[b].at[slot],
                                             send_sem=send_.at[3 * b + j], recv_sem=recv_.at[3 * b + j],
                                             device_id=(cx, cy, c), device_id_type=MESH).wait_recv()

    out = pl.pallas_call(
        body, name=name, in_specs=[HBM] * (2 * n) + [SEMS, SEMS, ANY], out_specs=[HBM] * (2 * n),
        out_shape=[pltpu.HBM(s.shape, s.dtype) for s in sums + landing],
        input_output_aliases={b: b for b in range(2 * n)},
        compiler_params=pltpu.CompilerParams(has_side_effects=FLOWS))(*sums, *landing, send, recv, after)
    return list(out[n:])


def _row(a, l):
    return a[l:l + 1]


def local_step(x, p, tgt, small, weights_of, vecs, a_ws, a_bs, grads_ready):
    T = x.shape[0]
    bsf = jnp.broadcast_to(a_bs[:, :, None], (AG_N, CHUNK, AGW))
    vrow = lambda r: vecs[r:r + 1]
    saved = []
    W = []
    h = x
    GA1 = GB1 = GA = GB = GC = bgrp = None

    def ff_fwd(h, xn, l, which, post, g_next, tok=None):
        wa, wb = (GA1, GB1) if which == 1 else (GA, GB)
        tag = "ff%d_l%d" % (which, l)
        gu, a = ff_gateup(xn, wa, 0, tag + "_gateup")
        gp = _row(post, l) if tok is None else _row(post, l) + tok
        out = mm_rs_post(a, wb, 0, FB, FB, h, gp, 0.5, tag + "_down", g_next=g_next)
        return out[1], (out[2] if g_next is not None else None), (h, xn, gu, a, out[0])

    xn = rms_fwd(h, _row(small["ff1_pre_g"], 0), BF16, "ff1_l0_pre")
    for l in range(4):
        rec = {}
        GA1, GB1, atok = weights_of(l, "a", h)
        g_mix = _row(small["mix_pre_g"], l) if l >= 2 else None
        h, hn, rec["ff1"] = ff_fwd(h, xn, l, 1, small["ff1_post_g"], g_mix, atok)
        GA, GB, GC, wtok = weights_of(l, "b", h)
        W.append((GA1, GB1, GA, GB, GC))
        if l == 1:
            bgrp = GC[:, C_BGRP:C_BGRP + 256, :].reshape(NCHIP, 4, 64, 256).transpose(1, 0, 2, 3).reshape(4, 256, 256)
        tag = "mix_l%d" % l
        h_in = h
        g_ff2 = _row(small["ff2_pre_g"], l)
        if l == 1:
            hn = rms_fwd(h, _row(small["mix_pre_g"], l), F32, tag + "_pre")
            pooled, ypre, f = pool_fwd(hn, bgrp, vrow(V_BSCALE), tag + "_pool")
            rec["mix"] = (h_in, pooled, ypre, f)
            h = post_res(h, f, _row(small["mix_post_g"], l), 1.0, tag + "_post")
            xn = rms_fwd(h, g_ff2, BF16, "ff2_l1_pre")
        else:
            gpost = _row(small["mix_post_g"], l)
            if l == 0:
                g0 = _row(small["mix_pre_g"], l)
                hn = rms_fwd(h, g0 if wtok is None else g0 + wtok, BF16, tag + "_pre")
                zpre = mm_cs(hn, GA, A_AIN, 1536, 1536, BF16, tag + "_in")
                y = gmlp_mid_fwd(zpre, small["a_v_norm_g"], small["a_v_norm_b"], a_ws, bsf, tag + "_gate")
                f, h, xn = mm_rs_post(y, GB, B_AOUT, 768, 768, h, gpost, 1.0, tag + "_out", g_next=g_ff2)
                rec["mix"] = (h_in, hn, zpre, y, f)
            elif l == 2:
                ag = mm_cs(hn, GA, A_CIN, 512, 512, BF16, tag + "_pw1")
                zc = conf_conv_fwd(ag, vecs[V_CDW:V_CDW + 32], vrow(V_CBDW), tag + "_conv")
                zs = conf_ln_fwd(zc, vrow(V_CNG), vrow(V_CNB), tag + "_ln")
                f, h, xn = mm_rs_post(zs, GB, B_CPW2, 256, 256, h, gpost, 1.0, tag + "_pw2", g_next=g_ff2)
                rec["mix"] = (h_in, hn, ag, zc, zs, f)
            else:
                bgx = mm_cs(hn, GA, A_DIN, 768, 768, BF16, tag + "_in")
                y = sconv_fwd(bgx, vecs[V_DCONV:V_DCONV + 8], tag + "_conv")
                f, h, xn = mm_rs_post(y, GB, B_DOUT, 256, 256, h, gpost, 1.0, tag + "_out", g_next=g_ff2)
                rec["mix"] = (h_in, hn, bgx, y, f)
        h, xn, rec["ff2"] = ff_fwd(h, xn, l, 2, small["ff2_post_g"], _row(small["ple_gate_norm_g"], l))
        tag = "ple_l%d" % l
        pb = p[l].astype(BF16)
        h_in = h
        g_next = _row(small["ff1_pre_g"], l + 1) if l < 3 else None
        out = ple_fwd(h, xn, pb, GB, B_PLEG(l), GC, _row(small["ple_post_g"], l), tag, g_next=g_next)
        rec["ple"] = (h_in, xn, out[0], out[1], pb)
        h = out[2]
        xn = out[3] if l < 3 else None
        saved.append(rec)

    dh, loss_cols = loss_head(h, tgt, "loss_head")

    dA2 = dB2 = None
    layer_grads = [None] * 4
    tok = None
    gV = {}
    gains = {k: [None] * 4 for k in ("ff1_pre_g", "ff1_post_g", "mix_pre_g", "mix_post_g", "ff2_pre_g", "ff2_post_g",
                                      "ple_gate_norm_g", "ple_post_g")}
    extra = {}

    def ff_bwd(dh, l, which, pre, post, rec, after=None):
        wa, wb = (GA1, GB1) if which == 1 else (GA, GB)
        tag = "ff%d_l%d_b" % (which, l)
        h_in, xn, gu, a, f = rec
        gp = _row(post, l) if after is None else _row(post, l) + after
        df, dpost, dgu = ff_bwd_down(dh, f, gp, wb, 0, gu, tag + "_down")
        if which == 1:
            db = dw_rs(a, df, FB, FB, tag + "_dwdown")
        else:
            db = dw_rs(a, df, FB, FB, tag + "_dwdown", height=B2_ROWS(l), off=B_FF2D(l), into=dB2)
        dh_in, dpre = mm_cs_t_rms(dgu, wa, 0, 2 * FB, 2 * FB, h_in, _row(pre, l), dh, tag + "_gateup")
        da = dw_cs(xn, dgu, 2 * FB, 2 * FB, tag + "_dwgateup", width=None if which == 1 else A2_COLS(l))
        return dh_in, dpre, dpost, (da, db)

    for l in reversed(range(4)):
        rec = saved[l]
        GA1, GB1, GA, GB, GC = W[l]
        gC = {}
        tag = "ple_l%d_b" % l
        h_in, xn, zg, pe, pb = rec["ple"]
        gpost = _row(small["ple_post_g"], l)
        if tok is not None:
            gpost = gpost + tok
        dzg, dpe, dh, gains["ple_post_g"][l], gains["ple_gate_norm_g"][l] = ple_bwd(
            dh, zg, pe, h_in, GB, B_PLEG(l), gpost, _row(small["ple_gate_norm_g"], l), tag)
        gC[C_PROJ(l)] = dw_cs(pb, dpe, 256, 256, tag + "_dwproj")
        dB2 = dw_rs(xn, dzg, 256, 256, tag + "_dwgate", height=B2_ROWS(l), off=B_PLEG(l))

        dh, gains["ff2_pre_g"][l], gains["ff2_post_g"][l], (dA2, dB2) = ff_bwd(
            dh, l, 2, small["ff2_pre_g"], small["ff2_post_g"], rec["ff2"])

        tag = "mix_l%d_b" % l
        mix = rec["mix"]
        h_in, f = mix[0], mix[-1]
        if l == 1:
            _, pooled, ypre, _ = mix
            df, gains["mix_post_g"][l] = post_res_bwd(dh, f, _row(small["mix_post_g"], l), 1.0, F32, tag + "_post")
            dhn, dwg, dsc = pool_bwd(df, ypre, pooled, bgrp, vrow(V_BSCALE), tag + "_pool")
            gC[C_BGRP] = dwg.astype(BF16).reshape(4, NCHIP, 64, 256).transpose(1, 0, 2, 3).reshape(NCHIP, 256, 256)
            gV[V_BSCALE] = jnp.pad(dsc, ((0, 7), (0, 0)))
            dh, gains["mix_pre_g"][l] = rms_bwd(dhn, h_in, _row(small["mix_pre_g"], l), dh, tag + "_pre")
        else:
            gpre = _row(small["mix_pre_g"], l)
            df, gains["mix_post_g"][l] = post_res_bwd(dh, f, _row(small["mix_post_g"], l), 1.0, BF16, tag + "_post")
            if l == 0:
                _, hn, zpre, y, _ = mix
                dy = mm_rs_t(df, GB, B_AOUT, 768, 768, tag + "_out")
                dB2 = dw_rs(y, df, 768, 768, tag + "_dwout", height=B2_ROWS(l), off=B_AOUT, into=dB2)
                dz, dws, dbs, dvg, dvb = gmlp_mid_bwd(zpre, dy, small["a_v_norm_g"], small["a_v_norm_b"], a_ws, bsf,
                                                      tag + "_gate")
                extra.update(a_w_s=dws, a_b_s=dbs[:, 0, :], a_v_norm_g=dvg, a_v_norm_b=dvb)
                dA2 = dw_cs(hn, dz, 1536, 1536, tag + "_dwin", width=A2_COLS(l), off=A_AIN, into=dA2)
                dh, gains["mix_pre_g"][l] = mm_cs_t_rms(dz, GA, A_AIN, 1536, 1536, h_in, gpre, dh, tag + "_in")
            elif l == 2:
                _, hn, ag, zc, zs, _ = mix
                dzs = mm_rs_t(df, GB, B_CPW2, 256, 256, tag + "_pw2")
                dB2 = dw_rs(zs, df, 256, 256, tag + "_dwpw2", height=B2_ROWS(l), off=B_CPW2, into=dB2)
                dzc, dng, dnb = conf_ln_bwd(dzs, zc, vrow(V_CNG), vrow(V_CNB), tag + "_ln")
                da_, dg_, dwdw, dbdw = conf_conv_bwd(dzc, ag, vecs[V_CDW:V_CDW + 32], tag + "_conv")
                dag = merge_cols([da_, dg_], tag + "_merge")
                gV[V_CDW] = dwdw
                gV[V_CBDW] = jnp.pad(dbdw, ((0, 7), (0, 0)))
                gV[V_CNG] = jnp.pad(dng, ((0, 7), (0, 0)))
                gV[V_CNB] = jnp.pad(dnb, ((0, 7), (0, 0)))
                dA2 = dw_cs(hn, dag, 512, 512, tag + "_dwpw1", width=A2_COLS(l), off=A_CIN, into=dA2)
                dh, gains["mix_pre_g"][l] = mm_cs_t_rms(dag, GA, A_CIN, 512, 512, h_in, gpre, dh, tag + "_pw1")
            else:
                _, hn, bgx, y, _ = mix
                dy = mm_rs_t(df, GB, B_DOUT, 256, 256, tag + "_out")
                dB2 = dw_rs(y, df, 256, 256, tag + "_dwout", height=B2_ROWS(l), off=B_DOUT, into=dB2)
                db_, dc_, dx_, dwc = sconv_bwd(dy, bgx, vecs[V_DCONV:V_DCONV + 8], tag + "_conv")
                dbgx = merge_cols([db_, dc_, dx_], tag + "_merge")
                gV[V_DCONV] = dwc
                dA2 = dw_cs(hn, dbgx, 768, 768, tag + "_dwin", width=A2_COLS(l), off=A_DIN, into=dA2)
                dh, gains["mix_pre_g"][l] = mm_cs_t_rms(dbgx, GA, A_DIN, 768, 768, h_in, gpre, dh, tag + "_in")

        dC = jnp.concatenate([gC[C_PROJ(l)]] + ([gC[C_BGRP]] if l == 1 else []), axis=1)
        tok = grads_ready(l, "b", (dA2, dB2, dC), dh)
        dh, gains["ff1_pre_g"][l], gains["ff1_post_g"][l], (dA1, dB1) = ff_bwd(
            dh, l, 1, small["ff1_pre_g"], small["ff1_post_g"], rec["ff1"], after=tok)
        layer_grads[l] = (dA1, dB1, dA2, dB2, dC)
        tok = grads_ready(l, "a", (dA1, dB1), dh)

    return loss_cols, dh, layer_grads, gV, gains, extra


GAIN_NAMES = ("ff1_pre_g", "ff1_post_g", "mix_pre_g", "mix_post_g", "ff2_pre_g", "ff2_post_g", "ple_gate_norm_g",
              "ple_post_g")


def _pad_rows(a, rows):
    return jnp.pad(a, ((0, rows - a.shape[0]), (0, 0)))


def kernel(x, p, ff1_pre_g, ff1_w_gate, ff1_w_up, ff1_w_down, ff1_post_g, mix_pre_g, mix_post_g, ff2_pre_g, ff2_w_gate, ff2_w_up, ff2_w_down, ff2_post_g, ple_gate_norm_g, ple_w_gate, ple_w_proj, ple_post_g, a_w_in, a_v_norm_g, a_v_norm_b, a_w_s, a_b_s, a_w_out, b_w_grp, b_scale, c_w_pw1, c_w_dw, c_b_dw, c_norm_g, c_norm_b, c_w_pw2, d_w_in, d_w_conv, d_w_out, loss_target, m_ff1_pre_g, m_ff1_w_gate, m_ff1_w_up, m_ff1_w_down, m_ff1_post_g, m_mix_pre_g, m_mix_post_g, m_ff2_pre_g, m_ff2_w_gate, m_ff2_w_up, m_ff2_w_down, m_ff2_post_g, m_ple_gate_norm_g, m_ple_w_gate, m_ple_w_proj, m_ple_post_g, m_a_w_in, m_a_v_norm_g, m_a_v_norm_b, m_a_w_s, m_a_b_s, m_a_w_out, m_b_w_grp, m_b_scale, m_c_w_pw1, m_c_w_dw, m_c_b_dw, m_c_norm_g, m_c_norm_b, m_c_w_pw2, m_d_w_in, m_d_w_conv, m_d_w_out, v_ff1_pre_g, v_ff1_w_gate, v_ff1_w_up, v_ff1_w_down, v_ff1_post_g, v_mix_pre_g, v_mix_post_g, v_ff2_pre_g, v_ff2_w_gate, v_ff2_w_up, v_ff2_w_down, v_ff2_post_g, v_ple_gate_norm_g, v_ple_w_gate, v_ple_w_proj, v_ple_post_g, v_a_w_in, v_a_v_norm_g, v_a_v_norm_b, v_a_w_s, v_a_b_s, v_a_w_out, v_b_w_grp, v_b_scale, v_c_w_pw1, v_c_w_dw, v_c_b_dw, v_c_norm_g, v_c_norm_b, v_c_w_pw2, v_d_w_in, v_d_w_conv, v_d_w_out):
    args = dict(locals())
    wnames = ["ff1_pre_g", "ff1_w_gate", "ff1_w_up", "ff1_w_down", "ff1_post_g", "mix_pre_g", "mix_post_g",
              "ff2_pre_g", "ff2_w_gate", "ff2_w_up", "ff2_w_down", "ff2_post_g", "ple_gate_norm_g", "ple_w_gate",
              "ple_w_proj", "ple_post_g", "a_w_in", "a_v_norm_g", "a_v_norm_b", "a_w_s", "a_b_s", "a_w_out",
              "b_w_grp", "b_scale", "c_w_pw1", "c_w_dw", "c_b_dw", "c_norm_g", "c_norm_b", "c_w_pw2", "d_w_in",
              "d_w_conv", "d_w_out"]

    P = pack_weights(args)
    G0a = all_gather_chips([P[0][0], P[0][1], P[4]], "gather_l0a")
    vecs = G0a[2].transpose(1, 0, 2).reshape(V_ROWS, DM)
    flying = {}

    def start(key, bufs, after):
        send, recv, slots, token = gather_start(own_slots(list(bufs)), after, "gather_start_l" + key)
        flying[key] = (send, recv, slots)
        return token[0, 0]

    tok = start("0b", P[0][2:], G0a[0])
    arrived = {}

    def weights_of(l, part, h):
        if l == 0 and part == "a":
            return G0a[0], G0a[1], None
        key = "0b" if l == 0 else str(l)
        wtok = None
        if key not in arrived:
            send, recv, slots = flying[key]
            n = len(slots)
            landed = gather_wait(send, recv, slots, list(range(n)), h, "gather_wait_l" + key)
            arrived[key] = gather_pass(landed, "gather_pass_l" + key)
            if l < 3:
                wtok = start(str(l + 1), P[l + 1], arrived[key][0])
        got = arrived[key]
        if l == 0:
            return tuple(got) + (wtok,)
        return tuple(got[:2]) + (wtok,) if part == "a" else tuple(got[2:]) + (None,)

    pending = {}
    reduced = {}
    held = {}

    def finish(key, after):
        kinds, send, recv, sums, landing = pending.pop(key)
        landed = chip_exchange_wait(send, recv, sums, landing, after, "rs_wait_l" + key)
        halves = [sum_slots(landed[i], *_tile(k, landed[i]), name="rs_sum_chips_l%s_%d%s" % (key, i, k))
                  for i, k in enumerate(kinds)]
        reduced[key] = pair_share(halves, "rs_pair_share_l" + key)

    def grads_ready(l, part, bufs, dh):
        if part == "b" and l > 0:
            held[l] = list(bufs)
            return None
        if part == "a" and l > 0:
            key, kinds, parts = str(l), "ABABC", list(bufs) + held.pop(l)
        elif part == "b":
            key, kinds, parts = "0b", "ABC", list(bufs)
        else:
            finish("0b", dh)
            return None
        for other in list(pending):
            finish(other, dh)
        got = pair_exchange(parts, "rs_pair_exchange_l" + key)
        sums = [add_half(parts[i], got[i], *_tile(k, got[i]), name="rs_add_pair_l%s_%d%s" % (key, i, k))
                for i, k in enumerate(kinds)]
        send, recv, sums, landing, token = chip_exchange_start(sums, "rs_start_l" + key)
        pending[key] = (kinds, send, recv, sums, landing)
        return token[0, 0]

    small = {k: args[k] for k in GAIN_NAMES}
    small["ff1_pre_g"] = ff1_pre_g + tok
    small["a_v_norm_g"] = a_v_norm_g
    small["a_v_norm_b"] = a_v_norm_b
    loss_cols, grad_x, layer_grads, gV, gains, extra = local_step(
        x[0], p[:, 0], loss_target[0], small, weights_of, vecs, a_w_s[0], a_b_s[0], grads_ready)

    loss = lax.psum((0.5 / DM) * jnp.sum(loss_cols), ("x", "y", "c"))

    deltas, new_m, new_v = {}, {}, {}

    def update(k, g):
        if args[k].shape[-1] == FW:
            t = lambda a: jnp.swapaxes(a, 1, 2)
            outs = adamw(t(args[k]), t(g), t(args["m_" + k]), t(args["v_" + k]), "adamw_" + k)
            deltas[k], new_m[k], new_v[k] = (t(o) for o in outs)
        else:
            deltas[k], new_m[k], new_v[k] = adamw(args[k], g, args["m_" + k], args["v_" + k], "adamw_" + k)

    dV, dE = pack_small_grads(gV, gains, extra)
    last = {"A": layer_grads[0][0], "B": layer_grads[0][1], "V": dV, "E": dE}
    kinds = list(last)
    parts = [last[k] for k in kinds]
    got = pair_exchange(parts, "rs_pair_exchange_l0a")
    sums = [add_half(parts[i], got[i], *_tile(k, got[i]), name="rs_add_pair_l0a_" + k) for i, k in enumerate(kinds)]
    send, recv, sums, landing, token = chip_exchange_start(sums, "rs_start_l0a")
    rest = [[None, None] + list(reduced["0b"])] + [list(reduced[str(l)]) for l in (1, 2, 3)]
    early = ("b_w_grp", "ff2_w_gate", "ff2_w_up", "ff2_w_down", "ple_w_gate", "ple_w_proj", "a_w_in", "a_w_out",
             "c_w_pw1", "c_w_pw2", "d_w_in", "d_w_out")
    grads = unpack_grads(rest, None, None, only=early)
    for i, k in enumerate(early):
        update(k, grads[k] + token[0, 0] if i == 0 else grads[k])
    landed = chip_exchange_wait(send, recv, sums, landing, deltas[early[-1]], "rs_wait_l0a")
    halves = [sum_slots(landed[i], *_tile(k, landed[i]), name="rs_sum_chips_l0a_" + k) for i, k in enumerate(kinds)]
    red = dict(zip(kinds, pair_share(halves, "rs_pair_share_l0a")))
    (gE,) = all_gather_chips([red["E"]], "gather_replicated_grads")
    per_layer = [[red["A"], red["B"]] + list(reduced["0b"])] + rest[1:]
    grads.update(unpack_grads(per_layer, red["V"], gE.reshape(E_ROWS, DM), only=[k for k in wnames if k not in early]))
    for k in wnames:
        if k not in early:
            update(k, grads[k])
    return (loss, grad_x[None], *[grads[k] for k in wnames], *[deltas[k] for k in wnames],
            *[new_m[k] for k in wnames], *[new_v[k] for k in wnames])


def pack_weights(w):
    padc = lambda a: jnp.pad(a, ((0, 0), (0, FB - FW)))
    mix_in = [w["a_w_in"][0], None, w["c_w_pw1"][0], w["d_w_in"][0]]
    mix_out = [w["a_w_out"][0], None, w["c_w_pw2"][0], w["d_w_out"][0]]
    packed = []
    for l in range(4):
        a1 = jnp.concatenate([padc(w["ff1_w_gate"][l]), padc(w["ff1_w_up"][l])], axis=1).astype(BF16)
        b1 = _pad_rows(w["ff1_w_down"][l], FB).astype(BF16)
        cols = [padc(w["ff2_w_gate"][l]), padc(w["ff2_w_up"][l])]
        rows = [_pad_rows(w["ff2_w_down"][l], FB)]
        if l != 1:
            cols.append(mix_in[l])
            rows.append(mix_out[l])
        rows.append(w["ple_w_gate"][l])
        proj = [w["ple_w_proj"][l]] + ([w["b_w_grp"][0].reshape(256, 256)] if l == 1 else [])
        packed.append((a1, b1, jnp.concatenate(cols, axis=1).astype(BF16), jnp.concatenate(rows, axis=0).astype(BF16),
                       jnp.concatenate(proj, axis=0).astype(BF16)))
    PV = jnp.concatenate([_pad_rows(w["b_scale"], 8), _pad_rows(w["c_b_dw"], 8), _pad_rows(w["c_norm_g"], 8),
                          _pad_rows(w["c_norm_b"], 8), _pad_rows(w["d_w_conv"][0], 8), _pad_rows(w["c_w_dw"][0], 40)],
                         axis=0)
    return packed + [PV]


def pack_small_grads(gV, gains, extra):
    dVt = jnp.concatenate([gV[V_BSCALE], gV[V_CBDW], gV[V_CNG], gV[V_CNB], gV[V_DCONV], gV[V_CDW],
                           jnp.zeros((8, DM), F32)], axis=0)
    dV = dVt.reshape(V_ROWS, NCHIP, 256).transpose(1, 0, 2)
    rowsE = [_pad_rows(jnp.concatenate(gains[k], axis=0), 8) for k in GAIN_NAMES]
    rowsE += [_pad_rows(extra["a_v_norm_g"].reshape(3, DM), 8), _pad_rows(extra["a_v_norm_b"].reshape(3, DM), 8),
              jnp.pad(extra["a_b_s"].reshape(1536), (0, 8 * DM - 1536)).reshape(8, DM),
              extra["a_w_s"].reshape(192, DM)]
    dE = _pad_rows(jnp.concatenate(rowsE, axis=0), E_ROWS).reshape(NCHIP, E_ROWS // NCHIP, DM)
    return dV, dE


def unpack_grads(per_layer, RV, gE, only=None):
    grads = {}
    for i, k in enumerate(GAIN_NAMES):
        grads[k] = lambda i=i: gE[8 * i:8 * i + 4]
    grads["a_v_norm_g"] = lambda: gE[64:67].reshape(1, 3072)
    grads["a_v_norm_b"] = lambda: gE[72:75].reshape(1, 3072)
    grads["a_b_s"] = lambda: gE[80:88].reshape(8 * DM)[:1536].reshape(1, 12, 128)
    grads["a_w_s"] = lambda: gE[88:280].reshape(1, 12, 128, 128)
    col1 = lambda l, off, n: per_layer[l][0][:, off:off + n]
    col2 = lambda l, off, n: per_layer[l][2][:, off:off + n]
    grads["ff1_w_gate"] = lambda: jnp.stack([col1(l, A_FF(l, 0), FW) for l in range(4)])
    grads["ff1_w_up"] = lambda: jnp.stack([col1(l, A_FF(l, 1), FW) for l in range(4)])
    grads["ff2_w_gate"] = lambda: jnp.stack([col2(l, A_FF(l, 2), FW) for l in range(4)])
    grads["ff2_w_up"] = lambda: jnp.stack([col2(l, A_FF(l, 3), FW) for l in range(4)])
    grads["a_w_in"] = lambda: col2(0, A_AIN, 1536)[None]
    grads["c_w_pw1"] = lambda: col2(2, A_CIN, 512)[None]
    grads["d_w_in"] = lambda: col2(3, A_DIN, 768)[None]
    row2 = lambda l, off, n: per_layer[l][3][off:off + n]
    grads["ff1_w_down"] = lambda: jnp.stack([per_layer[l][1][:FW] for l in range(4)])
    grads["ff2_w_down"] = lambda: jnp.stack([row2(l, B_FF2D(l), FW) for l in range(4)])
    grads["ple_w_gate"] = lambda: jnp.stack([row2(l, B_PLEG(l), 256) for l in range(4)])
    grads["a_w_out"] = lambda: row2(0, B_AOUT, 768)[None]
    grads["c_w_pw2"] = lambda: row2(2, B_CPW2, 256)[None]
    grads["d_w_out"] = lambda: row2(3, B_DOUT, 256)[None]
    grads["ple_w_proj"] = lambda: jnp.stack([per_layer[l][4][C_PROJ(l):C_PROJ(l) + 256] for l in range(4)])
    grads["b_w_grp"] = lambda: per_layer[1][4][C_BGRP:C_BGRP + 256].reshape(1, 4, 64, 256)
    grads["b_scale"] = lambda: RV[V_BSCALE:V_BSCALE + 1]
    grads["c_b_dw"] = lambda: RV[V_CBDW:V_CBDW + 1]
    grads["c_norm_g"] = lambda: RV[V_CNG:V_CNG + 1]
    grads["c_norm_b"] = lambda: RV[V_CNB:V_CNB + 1]
    grads["d_w_conv"] = lambda: RV[V_DCONV:V_DCONV + 3][None]
    grads["c_w_dw"] = lambda: RV[V_CDW:V_CDW + 31][None]
    return {k: f() for k, f in grads.items() if only is None or k in only}
```

```python
import functools
import math

import jax
import jax.numpy as jnp
from jax import lax
from jax.experimental import pallas as pl
from jax.experimental.pallas import tpu as pltpu

F32, BF16 = jnp.float32, jnp.bfloat16
EPS = 1e-6
DM = 1024
FW = 704
FB = 768
NCHIP = 4
VMEM_LIMIT = 56 * 1024 * 1024
ANY = pl.BlockSpec(memory_space=pl.ANY)
MESH = pl.DeviceIdType.MESH

A_FF = lambda l, j: (j % 2) * FB
A_AIN = A_CIN = A_DIN = 2 * FB
A2_COLS = lambda l: 2 * FB + (1536, 0, 512, 768)[l]
B_FF1D = lambda l: 0
B_FF2D = lambda l: 0
B_AOUT = B_CPW2 = B_DOUT = FB
B_PLEG = lambda l: FB + (768, 0, 256, 256)[l]
B2_ROWS = lambda l: B_PLEG(l) + 256
C_PROJ = lambda l: 0
C_BGRP = 256
V_BSCALE, V_CBDW, V_CNG, V_CNB, V_DCONV, V_CDW, V_ROWS = 0, 8, 16, 24, 32, 40, 80
E_ROWS = 320

ADAM_LR, ADAM_B1, ADAM_B2, ADAM_EPS, ADAM_WD, ADAM_STEP = 0.001, 0.9, 0.999, 1e-08, 0.01, 10


def _cp(sem):
    return pltpu.CompilerParams(dimension_semantics=sem, vmem_limit_bytes=VMEM_LIMIT)


def _sig(x):
    return 0.5 * jnp.tanh(0.5 * x) + 0.5


_GC = math.sqrt(2.0 / math.pi)


def _gelu(x):
    return 0.5 * x * (1.0 + jnp.tanh(_GC * (x + 0.044715 * x * x * x)))


def _gelu_grad(x):
    t = jnp.tanh(_GC * (x + 0.044715 * x * x * x))
    return 0.5 * (1.0 + t) + 0.5 * x * (1.0 - t * t) * _GC * (1.0 + 3.0 * 0.044715 * x * x)


def _dot_nn(a, b):
    return lax.dot_general(a, b, (((1,), (0,)), ((), ())), preferred_element_type=F32)


def _dot_nt(a, b):
    return lax.dot_general(a, b, (((1,), (1,)), ((), ())), preferred_element_type=F32)


def _dot_tn(a, b):
    return lax.dot_general(a, b, (((0,), (0,)), ((), ())), preferred_element_type=F32)


def mm_cs(x, G, off, nb, tn, out_dtype, name, roff=0):
    T, K = x.shape
    tm = min(1024, T)
    nj, ob, rb_ = nb // tn, off // tn, roff // K
    assert nb % tn == 0 and off % tn == 0 and roff % K == 0

    def body(x_ref, w_ref, o_ref):
        o_ref[...] = _dot_nn(x_ref[...], w_ref[...]).astype(o_ref.dtype)

    return pl.pallas_call(
        body, name=name, grid=(T // tm, NCHIP, nj),
        in_specs=[pl.BlockSpec((tm, K), lambda i, s, j: (i, 0)),
                  pl.BlockSpec((None, K, tn), lambda i, s, j: (s, rb_, ob + j))],
        out_specs=pl.BlockSpec((tm, tn), lambda i, s, j: (i, s * nj + j)),
        out_shape=jax.ShapeDtypeStruct((T, NCHIP * nb), out_dtype),
        compiler_params=_cp(("parallel", "arbitrary", "arbitrary")))(x, G)


def mm_rs_t(dy, G, off, rb, tk, name):
    T, N = dy.shape
    tm = min(1024, T)
    nkk, ob = rb // tk, off // tk
    nk = NCHIP * nkk

    def body(dy_ref, w_ref, o_ref):
        o_ref[...] = _dot_nt(dy_ref[...], w_ref[...]).astype(o_ref.dtype)

    return pl.pallas_call(
        body, name=name, grid=(T // tm, nk),
        in_specs=[pl.BlockSpec((tm, N), lambda i, k: (i, 0)),
                  pl.BlockSpec((None, tk, N), lambda i, k: (k // nkk, ob + k % nkk, 0))],
        out_specs=pl.BlockSpec((tm, tk), lambda i, k: (i, k)),
        out_shape=jax.ShapeDtypeStruct((T, NCHIP * rb), BF16),
        compiler_params=_cp(("parallel", "arbitrary")))(dy, G)


def mm_tn(a, b, tmm, tn, out_shape, out_map, name, into=None):
    T, M = a.shape
    N = b.shape[1]
    tt = min(2048, T)
    nt = T // tt

    def body(a_ref, b_ref, o_ref, acc):
        t = pl.program_id(2)

        @pl.when(t == 0)
        def _():
            acc[...] = jnp.zeros_like(acc)

        acc[...] += _dot_tn(a_ref[...], b_ref[...])

        @pl.when(t == nt - 1)
        def _():
            o_ref[...] = acc[...].astype(o_ref.dtype)

    in_specs = [pl.BlockSpec((tt, tmm), lambda i, j, t: (t, i)), pl.BlockSpec((tt, tn), lambda i, j, t: (t, j))]
    operands = (a, b)
    if into is None:
        def kern(a_ref, b_ref, o_ref, acc):
            body(a_ref, b_ref, o_ref, acc)
        aliases = {}
    else:
        def kern(a_ref, b_ref, into_ref, o_ref, acc):
            body(a_ref, b_ref, o_ref, acc)
        in_specs.append(ANY)
        operands = (a, b, into)
        aliases = {2: 0}
        out_shape = into.shape
    return pl.pallas_call(
        kern, name=name, grid=(M // tmm, N // tn, nt), in_specs=in_specs,
        out_specs=pl.BlockSpec((None, tmm, tn), lambda i, j, t: out_map(i, j)),
        out_shape=jax.ShapeDtypeStruct(out_shape, BF16), input_output_aliases=aliases,
        scratch_shapes=[pltpu.VMEM((tmm, tn), F32)],
        compiler_params=_cp(("parallel", "parallel", "arbitrary")))(*operands)


def dw_cs(x, dy, nb, tn, name, width=None, off=0, into=None):
    K = x.shape[1]
    nj, ob = nb // tn, off // tn
    assert off % tn == 0
    return mm_tn(x, dy, K, tn, (NCHIP, K, width or nb), lambda i, j: (j // nj, 0, ob + j % nj), name, into)


def dw_rs(a, dy, rb, tr, name, height=None, off=0, into=None):
    N = dy.shape[1]
    ni, ob = rb // tr, off // tr
    assert off % tr == 0
    return mm_tn(a, dy, tr, N, (NCHIP, height or rb, N), lambda i, j: (i // ni, ob + i % ni, 0), name, into)


def ff_gateup(xn, GA, off, name):
    T, K = xn.shape
    tm = min(2048, T)
    ob = off // (2 * FB)
    assert off % (2 * FB) == 0

    sub = min(512, tm)

    def body(x_ref, w_ref, gu_ref, a_ref):
        for r0 in range(0, tm, sub):
            r = _dot_nn(x_ref[r0:r0 + sub, :], w_ref[...])
            g, u = r[:, :FB], r[:, FB:]
            gu_ref[r0:r0 + sub, :] = r.astype(gu_ref.dtype)
            a_ref[r0:r0 + sub, :] = (g * _sig(g) * u).astype(a_ref.dtype)

    return pl.pallas_call(
        body, name=name, grid=(T // tm, NCHIP),
        in_specs=[pl.BlockSpec((tm, K), lambda i, s: (i, 0), pipeline_mode=pl.Buffered(1)),
                  pl.BlockSpec((None, K, 2 * FB), lambda i, s: (s, 0, ob))],
        out_specs=[pl.BlockSpec((tm, 2 * FB), lambda i, s: (i, s)), pl.BlockSpec((tm, FB), lambda i, s: (i, s))],
        out_shape=[jax.ShapeDtypeStruct((T, NCHIP * 2 * FB), BF16), jax.ShapeDtypeStruct((T, NCHIP * FB), BF16)],
        compiler_params=_cp(("parallel", "arbitrary")))(xn, GA)


def mm_rs_post(a, G, off, rb, tk, h, g, scale, name, g_next=None):
    T = a.shape[0]
    N = G.shape[2]
    tm = min(1024, T)
    sub = min(512, tm)
    nkk, ob = rb // tk, off // tk
    nk = NCHIP * nkk
    assert rb % tk == 0 and off % tk == 0
    more = g_next is not None

    def body(a_ref, w_ref, h_ref, g_ref, *rest):
        if more:
            gn_ref, f_ref, o_ref, xn_ref, acc = rest
        else:
            f_ref, o_ref, acc = rest
        k = pl.program_id(1)

        @pl.when(k == 0)
        def _():
            acc[...] = jnp.zeros_like(acc)

        acc[...] += _dot_nn(a_ref[...], w_ref[...])

        @pl.when(k == nk - 1)
        def _():
            for r0 in range(0, tm, sub):
                f = acc[r0:r0 + sub, :]
                f_ref[r0:r0 + sub, :] = f
                r = lax.rsqrt(jnp.mean(f * f, axis=-1, keepdims=True) + EPS)
                hn = h_ref[r0:r0 + sub, :] + scale * (f * r * g_ref[...])
                o_ref[r0:r0 + sub, :] = hn
                if more:
                    rn = lax.rsqrt(jnp.mean(hn * hn, axis=-1, keepdims=True) + EPS)
                    xn_ref[r0:r0 + sub, :] = (hn * rn * gn_ref[...]).astype(xn_ref.dtype)

    row = pl.BlockSpec((tm, N), lambda i, k: (i, 0))
    row1 = pl.BlockSpec((tm, N), lambda i, k: (i, 0), pipeline_mode=pl.Buffered(1))
    vec = pl.BlockSpec((1, N), lambda i, k: (0, 0))
    return pl.pallas_call(
        body, name=name, grid=(T // tm, nk),
        in_specs=[pl.BlockSpec((tm, tk), lambda i, k: (i, k)),
                  pl.BlockSpec((None, tk, N), lambda i, k: (k // nkk, ob + k % nkk, 0)),
                  row1, vec] + ([vec] if more else []),
        out_specs=[row, row] + ([row] if more else []),
        out_shape=[jax.ShapeDtypeStruct((T, N), F32), jax.ShapeDtypeStruct((T, N), F32)]
        + ([jax.ShapeDtypeStruct((T, N), BF16)] if more else []),
        scratch_shapes=[pltpu.VMEM((tm, N), F32)],
        compiler_params=_cp(("parallel", "arbitrary")))(*((a, G, h, g) + ((g_next,) if more else ())))


def ff_bwd_down(dh, f, g, GB, down, gu, name):
    T, N = dh.shape
    tm = min(1024, T)
    sub = min(512, tm)
    ob = down // FB

    def body(d_ref, f_ref, g_ref, w_ref, gu_ref, df_ref, dg_ref, dgu_ref):
        i, s = pl.program_id(0), pl.program_id(1)

        @pl.when(s == 0)
        def _():
            dg = jnp.zeros((1, N), F32)
            for r0 in range(0, tm, sub):
                f = f_ref[r0:r0 + sub, :]
                r = lax.rsqrt(jnp.mean(f * f, axis=-1, keepdims=True) + EPS)
                d = 0.5 * d_ref[r0:r0 + sub, :]
                t = d * g_ref[...]
                df_ref[r0:r0 + sub, :] = (
                    r * t - f * (r * r * r * jnp.mean(t * f, axis=-1, keepdims=True))).astype(df_ref.dtype)
                dg = dg + jnp.sum(d * f * r, axis=0, keepdims=True)
            _acc_rows(dg_ref, i, dg)

        for r0 in range(0, tm, sub):
            da = _dot_nt(df_ref[r0:r0 + sub, :], w_ref[...])
            gt = gu_ref[r0:r0 + sub, :FB].astype(F32)
            u = gu_ref[r0:r0 + sub, FB:].astype(F32)
            sg = _sig(gt)
            silu = gt * sg
            dgu_ref[r0:r0 + sub, :FB] = (da * u * (sg + silu - silu * sg)).astype(dgu_ref.dtype)
            dgu_ref[r0:r0 + sub, FB:] = (da * silu).astype(dgu_ref.dtype)

    row1 = pl.BlockSpec((tm, N), lambda i, s: (i, 0), pipeline_mode=pl.Buffered(1))
    row = pl.BlockSpec((tm, N), lambda i, s: (i, 0))
    vec = pl.BlockSpec((1, N), lambda i, s: (0, 0))
    return pl.pallas_call(
        body, name=name, grid=(T // tm, NCHIP),
        in_specs=[row1, row1, vec, pl.BlockSpec((None, FB, N), lambda i, s: (s, ob, 0)),
                  pl.BlockSpec((tm, 2 * FB), lambda i, s: (i, s))],
        out_specs=[row, vec, pl.BlockSpec((tm, 2 * FB), lambda i, s: (i, s))],
        out_shape=[jax.ShapeDtypeStruct((T, N), BF16), jax.ShapeDtypeStruct((1, N), F32),
                   jax.ShapeDtypeStruct((T, NCHIP * 2 * FB), BF16)],
        compiler_params=_cp(("arbitrary", "arbitrary")))(dh, f, g, GB, gu)


def mm_cs_t_rms(dy, G, off, nb, tn, h, g, skip, name):
    T = dy.shape[0]
    K = G.shape[1]
    tm = min(1024, T)
    sub = min(512, tm)
    nj, ob = nb // tn, off // tn
    nk = NCHIP * nj
    assert nb % tn == 0 and off % tn == 0

    def body(dy_ref, w_ref, h_ref, g_ref, s_ref, o_ref, dg_ref, acc):
        i, k = pl.program_id(0), pl.program_id(1)

        @pl.when(k == 0)
        def _():
            acc[...] = jnp.zeros_like(acc)

        acc[...] += _dot_nt(dy_ref[...], w_ref[...])

        @pl.when(k == nk - 1)
        def _():
            dg = jnp.zeros((1, K), F32)
            for r0 in range(0, tm, sub):
                d = acc[r0:r0 + sub, :]
                x = h_ref[r0:r0 + sub, :]
                r = lax.rsqrt(jnp.mean(x * x, axis=-1, keepdims=True) + EPS)
                xh = x * r
                t = d * g_ref[...]
                o_ref[r0:r0 + sub, :] = s_ref[r0:r0 + sub, :] + r * (t - xh * jnp.mean(t * xh, axis=-1, keepdims=True))
                dg = dg + jnp.sum(d * xh, axis=0, keepdims=True)
            _acc_rows(dg_ref, i, dg)

    row1 = pl.BlockSpec((tm, K), lambda i, k: (i, 0), pipeline_mode=pl.Buffered(1))
    row = pl.BlockSpec((tm, K), lambda i, k: (i, 0))
    vec = pl.BlockSpec((1, K), lambda i, k: (0, 0))
    return pl.pallas_call(
        body, name=name, grid=(T // tm, nk),
        in_specs=[pl.BlockSpec((tm, tn), lambda i, k: (i, k)),
                  pl.BlockSpec((None, K, tn), lambda i, k: (k // nj, 0, ob + k % nj)), row1, vec, row1],
        out_specs=[row, vec],
        out_shape=[jax.ShapeDtypeStruct((T, K), F32), jax.ShapeDtypeStruct((1, K), F32)],
        scratch_shapes=[pltpu.VMEM((tm, K), F32)],
        compiler_params=_cp(("arbitrary", "arbitrary")))(dy, G, h, g, skip)


def _rows(tm, C):
    return pl.BlockSpec((tm, C), lambda i: (i, 0))


def _vec(C):
    return pl.BlockSpec((1, C), lambda i: (0, 0))


def _acc_rows(ref, i, val):
    @pl.when(i == 0)
    def _():
        ref[...] = val

    @pl.when(i > 0)
    def _():
        ref[...] += val


def rms_fwd(h, g, out_dtype, name):
    T, C = h.shape
    tm = min(512, T)

    def body(h_ref, g_ref, o_ref):
        x = h_ref[...]
        r = lax.rsqrt(jnp.mean(x * x, axis=-1, keepdims=True) + EPS)
        o_ref[...] = (x * r * g_ref[...]).astype(o_ref.dtype)

    return pl.pallas_call(
        body, name=name, grid=(T // tm,), in_specs=[_rows(tm, C), _vec(C)], out_specs=_rows(tm, C),
        out_shape=jax.ShapeDtypeStruct((T, C), out_dtype), compiler_params=_cp(("parallel",)))(h, g)


def rms_bwd(dxn, h, g, dh_skip, name):
    T, C = h.shape
    tm = min(512, T)

    def body(d_ref, h_ref, g_ref, s_ref, o_ref, dg_ref):
        i = pl.program_id(0)
        x = h_ref[...]
        r = lax.rsqrt(jnp.mean(x * x, axis=-1, keepdims=True) + EPS)
        xh = x * r
        d = d_ref[...].astype(F32)
        t = d * g_ref[...]
        o_ref[...] = s_ref[...] + r * (t - xh * jnp.mean(t * xh, axis=-1, keepdims=True))
        _acc_rows(dg_ref, i, jnp.sum(d * xh, axis=0, keepdims=True))

    return pl.pallas_call(
        body, name=name, grid=(T // tm,),
        in_specs=[_rows(tm, C), _rows(tm, C), _vec(C), _rows(tm, C)],
        out_specs=[_rows(tm, C), _vec(C)],
        out_shape=[jax.ShapeDtypeStruct((T, C), F32), jax.ShapeDtypeStruct((1, C), F32)],
        compiler_params=_cp(("arbitrary",)))(dxn, h, g, dh_skip)


def post_res(h, f, g, scale, name):
    T, C = h.shape
    tm = min(512, T)

    def body(h_ref, f_ref, g_ref, o_ref):
        f = f_ref[...]
        r = lax.rsqrt(jnp.mean(f * f, axis=-1, keepdims=True) + EPS)
        o_ref[...] = h_ref[...] + scale * (f * r * g_ref[...])

    return pl.pallas_call(
        body, name=name, grid=(T // tm,), in_specs=[_rows(tm, C), _rows(tm, C), _vec(C)],
        out_specs=_rows(tm, C), out_shape=jax.ShapeDtypeStruct((T, C), F32),
        compiler_params=_cp(("parallel",)))(h, f, g)


def post_res_bwd(dh, f, g, scale, out_dtype, name):
    T, C = dh.shape
    tm = min(512, T)

    def body(d_ref, f_ref, g_ref, o_ref, dg_ref):
        i = pl.program_id(0)
        f = f_ref[...]
        r = lax.rsqrt(jnp.mean(f * f, axis=-1, keepdims=True) + EPS)
        d = scale * d_ref[...]
        t = d * g_ref[...]
        o_ref[...] = (r * t - f * (r * r * r * jnp.mean(t * f, axis=-1, keepdims=True))).astype(o_ref.dtype)
        _acc_rows(dg_ref, i, jnp.sum(d * f * r, axis=0, keepdims=True))

    return pl.pallas_call(
        body, name=name, grid=(T // tm,), in_specs=[_rows(tm, C), _rows(tm, C), _vec(C)],
        out_specs=[_rows(tm, C), _vec(C)],
        out_shape=[jax.ShapeDtypeStruct((T, C), out_dtype), jax.ShapeDtypeStruct((1, C), F32)],
        compiler_params=_cp(("arbitrary",)))(dh, f, g)


PLE_W = 256


def ple_fwd(h, xn, pb, GB, goff, GC, g, name, g_next=None):
    T, C = h.shape
    tm = min(512, T)
    more = g_next is not None

    def body(h_ref, x_ref, p_ref, wg_ref, wp_ref, g_ref, *rest):
        if more:
            gn_ref, z_ref, pe_ref, o_ref, xn_ref = rest
        else:
            z_ref, pe_ref, o_ref = rest
        zg = _dot_nn(x_ref[:, :PLE_W], wg_ref[0])
        for s in range(1, NCHIP):
            zg = zg + _dot_nn(x_ref[:, s * PLE_W:(s + 1) * PLE_W], wg_ref[s])
        z_ref[...] = zg
        for s in range(NCHIP):
            pe_ref[:, s * PLE_W:(s + 1) * PLE_W] = _dot_nn(p_ref[...], wp_ref[s])
        e = pe_ref[...] * _sig(zg)
        r = lax.rsqrt(jnp.mean(e * e, axis=-1, keepdims=True) + EPS)
        hn = h_ref[...] + e * r * g_ref[...]
        o_ref[...] = hn
        if more:
            rn = lax.rsqrt(jnp.mean(hn * hn, axis=-1, keepdims=True) + EPS)
            xn_ref[...] = (hn * rn * gn_ref[...]).astype(xn_ref.dtype)

    return pl.pallas_call(
        body, name=name, grid=(T // tm,),
        in_specs=[_rows(tm, C), _rows(tm, C), _rows(tm, PLE_W),
                  pl.BlockSpec((NCHIP, PLE_W, C), lambda i: (0, goff // PLE_W, 0)),
                  pl.BlockSpec((NCHIP, PLE_W, PLE_W), lambda i: (0, 0, 0)), _vec(C)] + ([_vec(C)] if more else []),
        out_specs=[_rows(tm, C)] * (4 if more else 3),
        out_shape=[jax.ShapeDtypeStruct((T, C), F32)] * 3 + ([jax.ShapeDtypeStruct((T, C), BF16)] if more else []),
        compiler_params=_cp(("parallel",)))(*((h, xn, pb, GB, GC, g) + ((g_next,) if more else ())))


def ple_bwd(dh, zg, pe, h_in, GB, goff, g, g_pre, name):
    T, C = dh.shape
    tm = min(512, T)

    def body(d_ref, z_ref, p_ref, h_ref, wg_ref, g_ref, gp_ref, dz_ref, dp_ref, o_ref, dg_ref, dgp_ref, dxn):
        i = pl.program_id(0)
        s = _sig(z_ref[...])
        pe_ = p_ref[...]
        e = pe_ * s
        r = lax.rsqrt(jnp.mean(e * e, axis=-1, keepdims=True) + EPS)
        d = d_ref[...]
        t = d * g_ref[...]
        de = r * t - e * (r * r * r * jnp.mean(t * e, axis=-1, keepdims=True))
        dp_ref[...] = (de * s).astype(dp_ref.dtype)
        dz = (de * pe_ * s * (1.0 - s)).astype(dz_ref.dtype)
        dz_ref[...] = dz
        _acc_rows(dg_ref, i, jnp.sum(d * e * r, axis=0, keepdims=True))
        for k in range(NCHIP):
            dxn[:, k * PLE_W:(k + 1) * PLE_W] = _dot_nt(dz, wg_ref[k])
        x = h_ref[...]
        rx = lax.rsqrt(jnp.mean(x * x, axis=-1, keepdims=True) + EPS)
        xh = x * rx
        dx = dxn[...]
        tx = dx * gp_ref[...]
        o_ref[...] = d + rx * (tx - xh * jnp.mean(tx * xh, axis=-1, keepdims=True))
        _acc_rows(dgp_ref, i, jnp.sum(dx * xh, axis=0, keepdims=True))

    return pl.pallas_call(
        body, name=name, grid=(T // tm,),
        in_specs=[_rows(tm, C), _rows(tm, C), _rows(tm, C), _rows(tm, C),
                  pl.BlockSpec((NCHIP, PLE_W, C), lambda i: (0, goff // PLE_W, 0)), _vec(C), _vec(C)],
        out_specs=[_rows(tm, C), _rows(tm, C), _rows(tm, C), _vec(C), _vec(C)],
        out_shape=[jax.ShapeDtypeStruct((T, C), BF16), jax.ShapeDtypeStruct((T, C), BF16),
                   jax.ShapeDtypeStruct((T, C), F32), jax.ShapeDtypeStruct((1, C), F32),
                   jax.ShapeDtypeStruct((1, C), F32)],
        scratch_shapes=[pltpu.VMEM((tm, C), F32)],
        compiler_params=_cp(("arbitrary",)))(dh, zg, pe, h_in, GB, g, g_pre)


def loss_head(h, tgt, name):
    T, C = h.shape
    tm = min(512, T)

    def body(h_ref, t_ref, d_ref, l_ref):
        i = pl.program_id(0)
        e = h_ref[...] - t_ref[...]
        d_ref[...] = e * (1.0 / C)
        _acc_rows(l_ref, i, jnp.sum(e * e, axis=0, keepdims=True))

    return pl.pallas_call(
        body, name=name, grid=(T // tm,), in_specs=[_rows(tm, C), _rows(tm, C)],
        out_specs=[_rows(tm, C), _vec(C)],
        out_shape=[jax.ShapeDtypeStruct((T, C), F32), jax.ShapeDtypeStruct((1, C), F32)],
        compiler_params=_cp(("arbitrary",)))(h, tgt)


AH, AG_N, AGW, CHUNK = 3072, 12, 256, 128


def _tril_bf16(w):
    r = lax.broadcasted_iota(jnp.int32, (CHUNK, CHUNK), 0)
    c = lax.broadcasted_iota(jnp.int32, (CHUNK, CHUNK), 1)
    return jnp.where(r >= c, w, 0.0).astype(BF16)


def _ln_stats(vs_ref, width):
    v = vs_ref[...]
    mu = jnp.sum(v, axis=-1, keepdims=True) * (1.0 / width)
    vc = v - mu
    var = jnp.sum(vc * vc, axis=-1, keepdims=True) * (1.0 / width)
    return mu, lax.rsqrt(var + EPS)


def gmlp_mid_fwd(zpre, vg, vb, ws, bsf, name):
    T = zpre.shape[0]

    def body(z_ref, vg_ref, vb_ref, ws_ref, bs_ref, y_ref, vs_ref):
        for g in range(AG_N):
            vs_ref[:, g * AGW:(g + 1) * AGW] = _gelu(z_ref[:, AH + g * AGW:AH + (g + 1) * AGW].astype(F32))
        mu, rstd = _ln_stats(vs_ref, AH)
        for g in range(AG_N):
            sl = slice(g * AGW, (g + 1) * AGW)
            vn = ((vs_ref[:, sl] - mu) * rstd * vg_ref[:, sl] + vb_ref[:, sl]).astype(BF16)
            sv = _dot_nn(_tril_bf16(ws_ref[g]), vn) + bs_ref[g]
            u = _gelu(z_ref[:, sl].astype(F32))
            y_ref[:, sl] = (u * sv).astype(y_ref.dtype)

    return pl.pallas_call(
        body, name=name, grid=(T // CHUNK,),
        in_specs=[_rows(CHUNK, 2 * AH), _vec(AH), _vec(AH),
                  pl.BlockSpec((AG_N, CHUNK, CHUNK), lambda i: (0, 0, 0)),
                  pl.BlockSpec((AG_N, CHUNK, AGW), lambda i: (0, 0, 0))],
        out_specs=_rows(CHUNK, AH), out_shape=jax.ShapeDtypeStruct((T, AH), BF16),
        scratch_shapes=[pltpu.VMEM((CHUNK, AH), F32)],
        compiler_params=_cp(("parallel",)))(zpre, vg, vb, ws, bsf)


def gmlp_mid_bwd(zpre, dy, vg, vb, ws, bsf, name):
    T = zpre.shape[0]

    def body(z_ref, dy_ref, vg_ref, vb_ref, ws_ref, bs_ref, dz_ref, dws_ref, dbs_ref, dvg_ref, dvb_ref,
             vs_ref, dvn_ref):
        i = pl.program_id(0)

        @pl.when(i == 0)
        def _():
            dws_ref[...] = jnp.zeros_like(dws_ref)
            dbs_ref[...] = jnp.zeros_like(dbs_ref)
            dvg_ref[...] = jnp.zeros_like(dvg_ref)
            dvb_ref[...] = jnp.zeros_like(dvb_ref)

        for g in range(AG_N):
            vs_ref[:, g * AGW:(g + 1) * AGW] = _gelu(z_ref[:, AH + g * AGW:AH + (g + 1) * AGW].astype(F32))
        mu, rstd = _ln_stats(vs_ref, AH)
        r_i = lax.broadcasted_iota(jnp.int32, (CHUNK, CHUNK), 0)
        c_i = lax.broadcasted_iota(jnp.int32, (CHUNK, CHUNK), 1)
        ones8 = jnp.ones((8, AGW), F32)
        m1 = jnp.zeros((CHUNK, 1), F32)
        m2 = jnp.zeros((CHUNK, 1), F32)
        for g in range(AG_N):
            sl = slice(g * AGW, (g + 1) * AGW)
            vh = (vs_ref[:, sl] - mu) * rstd
            vn = (vh * vg_ref[:, sl] + vb_ref[:, sl]).astype(BF16)
            wm = _tril_bf16(ws_ref[g])
            sv = _dot_nn(wm, vn) + bs_ref[g]
            zu = z_ref[:, sl].astype(F32)
            u = _gelu(zu)
            dyg = dy_ref[:, sl].astype(F32)
            dz_ref[:, sl] = (dyg * sv * _gelu_grad(zu)).astype(dz_ref.dtype)
            dsv = dyg * u
            dsv_b = dsv.astype(BF16)
            dws_ref[g] += jnp.where(r_i >= c_i, _dot_nt(dsv_b, vn), 0.0)
            dbs_ref[g] += _dot_nt(ones8, dsv)
            dvn = _dot_tn(wm, dsv_b)
            dvn_ref[:, sl] = dvn
            dvh = dvn * vg_ref[:, sl]
            m1 = m1 + jnp.sum(dvh, axis=-1, keepdims=True)
            m2 = m2 + jnp.sum(dvh * vh, axis=-1, keepdims=True)
            dvg_ref[:, sl] += jnp.sum(dvn * vh, axis=0, keepdims=True)
            dvb_ref[:, sl] += jnp.sum(dvn, axis=0, keepdims=True)
        m1 = m1 * (1.0 / AH)
        m2 = m2 * (1.0 / AH)
        for g in range(AG_N):
            sl = slice(g * AGW, (g + 1) * AGW)
            vh = (vs_ref[:, sl] - mu) * rstd
            dv = rstd * (dvn_ref[:, sl] * vg_ref[:, sl] - m1 - vh * m2)
            zv = z_ref[:, AH + g * AGW:AH + (g + 1) * AGW].astype(F32)
            dz_ref[:, AH + g * AGW:AH + (g + 1) * AGW] = (dv * _gelu_grad(zv)).astype(dz_ref.dtype)

    full3 = lambda a, b, c: pl.BlockSpec((a, b, c), lambda i: (0, 0, 0))
    return pl.pallas_call(
        body, name=name, grid=(T // CHUNK,),
        in_specs=[_rows(CHUNK, 2 * AH), _rows(CHUNK, AH), _vec(AH), _vec(AH),
                  full3(AG_N, CHUNK, CHUNK), full3(AG_N, CHUNK, AGW)],
        out_specs=[_rows(CHUNK, 2 * AH), full3(AG_N, CHUNK, CHUNK), full3(AG_N, 8, CHUNK), _vec(AH), _vec(AH)],
        out_shape=[jax.ShapeDtypeStruct((T, 2 * AH), BF16), jax.ShapeDtypeStruct((AG_N, CHUNK, CHUNK), F32),
                   jax.ShapeDtypeStruct((AG_N, 8, CHUNK), F32), jax.ShapeDtypeStruct((1, AH), F32),
                   jax.ShapeDtypeStruct((1, AH), F32)],
        scratch_shapes=[pltpu.VMEM((CHUNK, AH), F32), pltpu.VMEM((CHUNK, AH), F32)],
        compiler_params=_cp(("arbitrary",)))(zpre, dy, vg, vb, ws, bsf)


SLAB = 256
NSLAB = DM // SLAB
RC = 256
PAD = 32


def _col(T, j):
    return pl.BlockSpec((T, SLAB), lambda c: (0, j * NSLAB + c))


def _chunks(T, fn):
    def step(i, carry):
        fn(pl.multiple_of(i * RC, RC))
        return carry
    lax.fori_loop(0, T // RC, step, 0)


def _conv_taps(K):
    return [(r, [q for q in range(4) if 8 * q + r < K]) for r in range(min(8, K))]


def _causal_conv(zpad_ref, wrow, K, r0):
    acc = None
    for r, qs in _conv_taps(K):
        a = None
        for q in qs:
            term = wrow(8 * q + r) * zpad_ref[pl.ds(r0 + (PAD - 8 - 8 * q), RC + 8), :]
            a = term if a is None else a + term
        a = a if r == 0 else pltpu.roll(a, r, 0)
        acc = a if acc is None else acc + a
    return acc[8:, :]


def _anticausal_conv(gpad_ref, wrow, K, r0):
    acc = None
    for r, qs in _conv_taps(K):
        b = None
        for q in qs:
            term = wrow(8 * q + r) * gpad_ref[pl.ds(r0 + 8 * q, RC + 8), :]
            b = term if b is None else b + term
        b = b if r == 0 else pltpu.roll(b, RC + 8 - r, 0)
        acc = b if acc is None else acc + b
    return acc[:RC, :]


def _conv_dw(gpad_ref, zpad_ref, dw_ref, K, r0):
    for r, qs in _conv_taps(K):
        gw = gpad_ref[pl.ds(r0, RC + 8), :]
        p = (gw if r == 0 else pltpu.roll(gw, RC + 8 - r, 0))[:RC, :]
        for q in qs:
            z = zpad_ref[pl.ds(r0 + (PAD - 8 * q), RC), :]
            dw_ref[8 * q + r] += jnp.sum((p * z).reshape(RC // 8, 8, SLAB), axis=0)


def _zero_rows(ref, start, n):
    ref[pl.ds(start, n), :] = jnp.zeros((n, SLAB), F32)


def pool_fwd(hn, wg, sc, name):
    T = hn.shape[0]

    def body(h_ref, w_ref, s_ref, p_ref, yp_ref, y_ref, xpad):
        g = pl.program_id(0)
        wf = jnp.left_shift(2, g).astype(F32)
        _zero_rows(xpad, 0, PAD)

        def fill(r0):
            xpad[pl.ds(r0 + PAD, RC), :] = h_ref[pl.ds(r0, RC), :]
        _chunks(T, fill)

        def step(r0):
            w = xpad[pl.ds(r0 + (PAD - 16), RC + 16), :]
            s2 = w + pltpu.roll(w, 1, 0)
            s4 = s2 + pltpu.roll(s2, 2, 0)
            s8 = s4 + pltpu.roll(s4, 4, 0)
            s16 = s8 + pltpu.roll(s8, 8, 0)
            sel = jnp.where(g == 0, s2, jnp.where(g == 1, s4, jnp.where(g == 2, s8, s16)))[16:, :]
            t1 = (r0 + 1 + lax.broadcasted_iota(jnp.int32, (RC, SLAB), 0)).astype(F32)
            pooled = (sel / jnp.minimum(t1, wf) - w[16:, :]).astype(BF16)
            p_ref[pl.ds(r0, RC), :] = pooled
            yp = _dot_nn(pooled, w_ref[...])
            yp_ref[pl.ds(r0, RC), :] = yp
            y_ref[pl.ds(r0, RC), :] = yp * s_ref[...]
        _chunks(T, step)

    slab = pl.BlockSpec((T, SLAB), lambda c: (0, c))
    return pl.pallas_call(
        body, name=name, grid=(NSLAB,),
        in_specs=[slab, pl.BlockSpec((None, SLAB, SLAB), lambda c: (c, 0, 0)), pl.BlockSpec((1, SLAB), lambda c: (0, c))],
        out_specs=[slab, slab, slab],
        out_shape=[jax.ShapeDtypeStruct((T, DM), BF16), jax.ShapeDtypeStruct((T, DM), F32),
                   jax.ShapeDtypeStruct((T, DM), F32)],
        scratch_shapes=[pltpu.VMEM((T + PAD, SLAB), F32)],
        compiler_params=_cp(("parallel",)))(hn, wg, sc)


def pool_bwd(dy, ypre, pooled, wg, sc, name):
    T = dy.shape[0]

    def body(d_ref, yp_ref, p_ref, w_ref, s_ref, dh_ref, dw_ref, ds_ref, qpad, dwacc, dsacc):
        g = pl.program_id(0)
        wf = jnp.left_shift(2, g).astype(F32)
        dwacc[...] = jnp.zeros_like(dwacc)
        dsacc[...] = jnp.zeros_like(dsacc)
        _zero_rows(qpad, T, PAD)

        def first(r0):
            d = d_ref[pl.ds(r0, RC), :]
            dsacc[...] += jnp.sum((d * yp_ref[pl.ds(r0, RC), :]).reshape(RC // 8, 8, SLAB), axis=0)
            dyp = (d * s_ref[...]).astype(BF16)
            dpool = _dot_nt(dyp, w_ref[...])
            dwacc[...] += _dot_tn(p_ref[pl.ds(r0, RC), :], dyp)
            t1 = (r0 + 1 + lax.broadcasted_iota(jnp.int32, (RC, SLAB), 0)).astype(F32)
            qpad[pl.ds(r0, RC), :] = dpool / jnp.minimum(t1, wf)
            dh_ref[pl.ds(r0, RC), :] = dpool
        _chunks(T, first)

        def second(r0):
            w = qpad[pl.ds(r0, RC + 16), :]
            n = RC + 16
            a2 = w + pltpu.roll(w, n - 1, 0)
            a4 = a2 + pltpu.roll(a2, n - 2, 0)
            a8 = a4 + pltpu.roll(a4, n - 4, 0)
            a16 = a8 + pltpu.roll(a8, n - 8, 0)
            sel = jnp.where(g == 0, a2, jnp.where(g == 1, a4, jnp.where(g == 2, a8, a16)))[:RC, :]
            dh_ref[pl.ds(r0, RC), :] = sel - dh_ref[pl.ds(r0, RC), :]
        _chunks(T, second)
        dw_ref[...] = dwacc[...]
        ds_ref[...] = jnp.sum(dsacc[...], axis=0, keepdims=True)

    slab = pl.BlockSpec((T, SLAB), lambda c: (0, c))
    wspec = pl.BlockSpec((None, SLAB, SLAB), lambda c: (c, 0, 0))
    vec = pl.BlockSpec((1, SLAB), lambda c: (0, c))
    return pl.pallas_call(
        body, name=name, grid=(NSLAB,),
        in_specs=[slab, slab, slab, wspec, vec],
        out_specs=[slab, wspec, vec],
        out_shape=[jax.ShapeDtypeStruct((T, DM), F32), jax.ShapeDtypeStruct((NSLAB, SLAB, SLAB), F32),
                   jax.ShapeDtypeStruct((1, DM), F32)],
        scratch_shapes=[pltpu.VMEM((T + PAD, SLAB), F32), pltpu.VMEM((SLAB, SLAB), F32), pltpu.VMEM((8, SLAB), F32)],
        compiler_params=_cp(("parallel",)))(dy, ypre, pooled, wg, sc)


KC = 31
KD = 3


def conf_conv_fwd(ag, wdw, bdw, name):
    T = ag.shape[0]

    def body(a_ref, g_ref, w_ref, b_ref, o_ref, zpad):
        _zero_rows(zpad, 0, PAD)

        def fill(r0):
            a = a_ref[pl.ds(r0, RC), :].astype(F32)
            gt = g_ref[pl.ds(r0, RC), :].astype(F32)
            zpad[pl.ds(r0 + PAD, RC), :] = a * _sig(gt)
        _chunks(T, fill)
        wrow = lambda j: w_ref[KC - 1 - j:KC - j, :]

        def step(r0):
            o_ref[pl.ds(r0, RC), :] = _causal_conv(zpad, wrow, KC, r0) + b_ref[...]
        _chunks(T, step)

    vec = pl.BlockSpec((1, SLAB), lambda c: (0, c))
    return pl.pallas_call(
        body, name=name, grid=(NSLAB,),
        in_specs=[_col(T, 0), _col(T, 1), pl.BlockSpec((32, SLAB), lambda c: (0, c)), vec],
        out_specs=pl.BlockSpec((T, SLAB), lambda c: (0, c)),
        out_shape=jax.ShapeDtypeStruct((T, DM), F32),
        scratch_shapes=[pltpu.VMEM((T + PAD, SLAB), F32)],
        compiler_params=_cp(("parallel",)))(ag, ag, wdw, bdw)


def conf_conv_bwd(dzc, ag, wdw, name):
    T = ag.shape[0]

    def body(d_ref, a_ref, g_ref, w_ref, da_ref, dg_ref, dw_ref, db_ref, zpad, gpad, dwacc, dbacc):
        _zero_rows(zpad, 0, PAD)
        _zero_rows(gpad, T, PAD)
        dwacc[...] = jnp.zeros_like(dwacc)
        dbacc[...] = jnp.zeros_like(dbacc)

        def fill(r0):
            a = a_ref[pl.ds(r0, RC), :].astype(F32)
            gt = g_ref[pl.ds(r0, RC), :].astype(F32)
            zpad[pl.ds(r0 + PAD, RC), :] = a * _sig(gt)
            d = d_ref[pl.ds(r0, RC), :]
            gpad[pl.ds(r0, RC), :] = d
            dbacc[...] += jnp.sum(d.reshape(RC // 8, 8, SLAB), axis=0)
        _chunks(T, fill)
        wrow = lambda j: w_ref[KC - 1 - j:KC - j, :]

        def step(r0):
            dz = _anticausal_conv(gpad, wrow, KC, r0)
            a = a_ref[pl.ds(r0, RC), :].astype(F32)
            s = _sig(g_ref[pl.ds(r0, RC), :].astype(F32))
            da_ref[pl.ds(r0, RC), :] = (dz * s).astype(da_ref.dtype)
            dg_ref[pl.ds(r0, RC), :] = (dz * a * s * (1.0 - s)).astype(dg_ref.dtype)
            _conv_dw(gpad, zpad, dwacc, KC, r0)
        _chunks(T, step)
        dw_ref[...] = jnp.zeros_like(dw_ref)
        for k in range(KC):
            dw_ref[k:k + 1, :] = jnp.sum(dwacc[KC - 1 - k], axis=0, keepdims=True)
        db_ref[...] = jnp.sum(dbacc[...], axis=0, keepdims=True)

    vec = pl.BlockSpec((1, SLAB), lambda c: (0, c))
    w32 = pl.BlockSpec((32, SLAB), lambda c: (0, c))
    return pl.pallas_call(
        body, name=name, grid=(NSLAB,),
        in_specs=[pl.BlockSpec((T, SLAB), lambda c: (0, c)), _col(T, 0), _col(T, 1), w32],
        out_specs=[_col(T, 0), _col(T, 0), w32, vec],
        out_shape=[jax.ShapeDtypeStruct((T, DM), BF16), jax.ShapeDtypeStruct((T, DM), BF16),
                   jax.ShapeDtypeStruct((32, DM), F32), jax.ShapeDtypeStruct((1, DM), F32)],
        scratch_shapes=[pltpu.VMEM((T + PAD, SLAB), F32), pltpu.VMEM((T + PAD, SLAB), F32),
                        pltpu.VMEM((32, 8, SLAB), F32), pltpu.VMEM((8, SLAB), F32)],
        compiler_params=_cp(("parallel",)))(dzc, ag, ag, wdw)


def conf_ln_fwd(zc, g, b, name):
    T, C = zc.shape
    tm = min(512, T)

    def body(z_ref, g_ref, b_ref, o_ref):
        x = z_ref[...]
        xc = x - jnp.mean(x, axis=-1, keepdims=True)
        r = lax.rsqrt(jnp.mean(xc * xc, axis=-1, keepdims=True) + EPS)
        zl = xc * r * g_ref[...] + b_ref[...]
        o_ref[...] = (zl * _sig(zl)).astype(o_ref.dtype)

    return pl.pallas_call(
        body, name=name, grid=(T // tm,), in_specs=[_rows(tm, C), _vec(C), _vec(C)], out_specs=_rows(tm, C),
        out_shape=jax.ShapeDtypeStruct((T, C), BF16), compiler_params=_cp(("parallel",)))(zc, g, b)


def conf_ln_bwd(dzs, zc, g, b, name):
    T, C = zc.shape
    tm = min(512, T)

    def body(d_ref, z_ref, g_ref, b_ref, o_ref, dg_ref, db_ref):
        i = pl.program_id(0)
        x = z_ref[...]
        xc = x - jnp.mean(x, axis=-1, keepdims=True)
        r = lax.rsqrt(jnp.mean(xc * xc, axis=-1, keepdims=True) + EPS)
        xh = xc * r
        zl = xh * g_ref[...] + b_ref[...]
        s = _sig(zl)
        dzl = d_ref[...].astype(F32) * (s * (1.0 + zl * (1.0 - s)))
        t = dzl * g_ref[...]
        o_ref[...] = r * (t - jnp.mean(t, axis=-1, keepdims=True) - xh * jnp.mean(t * xh, axis=-1, keepdims=True))
        _acc_rows(dg_ref, i, jnp.sum(dzl * xh, axis=0, keepdims=True))
        _acc_rows(db_ref, i, jnp.sum(dzl, axis=0, keepdims=True))

    return pl.pallas_call(
        body, name=name, grid=(T // tm,), in_specs=[_rows(tm, C), _rows(tm, C), _vec(C), _vec(C)],
        out_specs=[_rows(tm, C), _vec(C), _vec(C)],
        out_shape=[jax.ShapeDtypeStruct((T, C), F32), jax.ShapeDtypeStruct((1, C), F32),
                   jax.ShapeDtypeStruct((1, C), F32)],
        compiler_params=_cp(("arbitrary",)))(dzs, zc, g, b)


def sconv_fwd(bgx, wc, name):
    T = bgx.shape[0]

    def body(b_ref, c_ref, x_ref, w_ref, o_ref, zpad):
        _zero_rows(zpad, 0, PAD)

        def fill(r0):
            zpad[pl.ds(r0 + PAD, RC), :] = c_ref[pl.ds(r0, RC), :].astype(F32) * x_ref[pl.ds(r0, RC), :].astype(F32)
        _chunks(T, fill)
        wrow = lambda j: w_ref[KD - 1 - j:KD - j, :]

        def step(r0):
            qc = _causal_conv(zpad, wrow, KD, r0)
            o_ref[pl.ds(r0, RC), :] = (b_ref[pl.ds(r0, RC), :].astype(F32) * qc).astype(o_ref.dtype)
        _chunks(T, step)

    return pl.pallas_call(
        body, name=name, grid=(NSLAB,),
        in_specs=[_col(T, 0), _col(T, 1), _col(T, 2), pl.BlockSpec((8, SLAB), lambda c: (0, c))],
        out_specs=pl.BlockSpec((T, SLAB), lambda c: (0, c)),
        out_shape=jax.ShapeDtypeStruct((T, DM), BF16),
        scratch_shapes=[pltpu.VMEM((T + PAD, SLAB), F32)],
        compiler_params=_cp(("parallel",)))(bgx, bgx, bgx, wc)


def sconv_bwd(dy, bgx, wc, name):
    T = bgx.shape[0]

    def body(d_ref, b_ref, c_ref, x_ref, w_ref, db_ref, dc_ref, dx_ref, dw_ref, zpad, gpad, dwacc):
        _zero_rows(zpad, 0, PAD)
        _zero_rows(gpad, T, PAD)
        dwacc[...] = jnp.zeros_like(dwacc)

        def fill(r0):
            zpad[pl.ds(r0 + PAD, RC), :] = c_ref[pl.ds(r0, RC), :].astype(F32) * x_ref[pl.ds(r0, RC), :].astype(F32)
            gpad[pl.ds(r0, RC), :] = d_ref[pl.ds(r0, RC), :].astype(F32) * b_ref[pl.ds(r0, RC), :].astype(F32)
        _chunks(T, fill)
        wrow = lambda j: w_ref[KD - 1 - j:KD - j, :]

        def step(r0):
            qc = _causal_conv(zpad, wrow, KD, r0)
            db_ref[pl.ds(r0, RC), :] = (d_ref[pl.ds(r0, RC), :].astype(F32) * qc).astype(db_ref.dtype)
            dq = _anticausal_conv(gpad, wrow, KD, r0)
            dc_ref[pl.ds(r0, RC), :] = (dq * x_ref[pl.ds(r0, RC), :].astype(F32)).astype(dc_ref.dtype)
            dx_ref[pl.ds(r0, RC), :] = (dq * c_ref[pl.ds(r0, RC), :].astype(F32)).astype(dx_ref.dtype)
            _conv_dw(gpad, zpad, dwacc, KD, r0)
        _chunks(T, step)
        dw_ref[...] = jnp.zeros_like(dw_ref)
        for k in range(KD):
            dw_ref[k:k + 1, :] = jnp.sum(dwacc[KD - 1 - k], axis=0, keepdims=True)

    w8 = pl.BlockSpec((8, SLAB), lambda c: (0, c))
    return pl.pallas_call(
        body, name=name, grid=(NSLAB,),
        in_specs=[pl.BlockSpec((T, SLAB), lambda c: (0, c)), _col(T, 0), _col(T, 1), _col(T, 2), w8],
        out_specs=[_col(T, 0), _col(T, 0), _col(T, 0), w8],
        out_shape=[jax.ShapeDtypeStruct((T, DM), BF16)] * 3 + [jax.ShapeDtypeStruct((8, DM), F32)],
        scratch_shapes=[pltpu.VMEM((T + PAD, SLAB), F32), pltpu.VMEM((T + PAD, SLAB), F32),
                        pltpu.VMEM((8, 8, SLAB), F32)],
        compiler_params=_cp(("parallel",)))(dy, bgx, bgx, bgx, wc)


def merge_cols(parts, name):
    T = parts[0].shape[0]
    n = len(parts)
    C = n * DM
    tm = min(512, T)

    def body(*refs):
        o_ref = refs[n]
        for j in range(n):
            o_ref[:, j * DM:(j + 1) * DM] = refs[j][...]

    return pl.pallas_call(
        body, name=name, grid=(T // tm,),
        in_specs=[_rows(tm, DM) for j in range(n)],
        out_specs=_rows(tm, C), out_shape=jax.ShapeDtypeStruct((T, C), parts[0].dtype),
        compiler_params=_cp(("parallel",)))(*parts)


def adamw(w, g, m, v, name):
    shape = w.shape
    R, C = shape[-2], shape[-1]
    L = w.size // (R * C)
    w2, g2, m2, v2 = (a.reshape(L, R, C) for a in (w, g, m, v))
    tr = R
    while tr * C > 512 * 1024 and tr % 16 == 0:
        tr //= 2
    bc1 = 1.0 - ADAM_B1 ** ADAM_STEP
    bc2 = 1.0 - ADAM_B2 ** ADAM_STEP

    def body(w_ref, g_ref, m_ref, v_ref, d_ref, nm_ref, nv_ref):
        gg = g_ref[...]
        nm = ADAM_B1 * m_ref[...] + (1.0 - ADAM_B1) * gg
        nv = ADAM_B2 * v_ref[...] + (1.0 - ADAM_B2) * (gg * gg)
        nm_ref[...] = nm
        nv_ref[...] = nv
        d_ref[...] = -ADAM_LR * ((nm / bc1) / (jnp.sqrt(nv / bc2) + ADAM_EPS) + ADAM_WD * w_ref[...])

    spec = pl.BlockSpec((None, tr, C), lambda l, i: (l, i, 0))
    outs = pl.pallas_call(
        body, name=name, grid=(L, R // tr), in_specs=[spec] * 4, out_specs=[spec] * 3,
        out_shape=[jax.ShapeDtypeStruct((L, R, C), F32)] * 3,
        compiler_params=_cp(("parallel", "parallel")))(w2, g2, m2, v2)
    return tuple(o.reshape(shape) for o in outs)


def _place():
    x, y, c = lax.axis_index("x"), lax.axis_index("y"), lax.axis_index("c")
    return x, y, c


def all_gather_chips(bufs, name):
    n = len(bufs)
    me_ = 2 * lax.axis_index("x") + lax.axis_index("y")
    slots = [lax.dynamic_update_slice(lax.empty((NCHIP,) + b.shape, b.dtype), b[None], (me_, 0, 0)) for b in bufs]

    def body(*refs):
        dst = refs[n:2 * n]
        send, recv = refs[2 * n:]
        x, y, c = _place()
        me = 2 * x + y
        sib = (x, y, 1 - c)
        chips = [(1 - x, y), (x, 1 - y), (1 - x, 1 - y)]

        def half(b, slot, hc):
            rows = bufs[b].shape[0] // 2
            return dst[b].at[slot, pl.ds(hc * rows, rows), :]

        def remote(k, s, d, to):
            return pltpu.make_async_remote_copy(src_ref=s, dst_ref=d, send_sem=send.at[k], recv_sem=recv.at[k],
                                                device_id=to, device_id_type=MESH)

        first = []
        for b in range(n):
            for j, (cx, cy) in enumerate(chips):
                first.append(remote(b * 6 + j, half(b, me, c), half(b, me, c), (cx, cy, c)))
        for cp in first:
            cp.start()
        passed = []
        for b in range(n):
            for j, (cx, cy) in enumerate(chips):
                slot = 2 * cx + cy
                remote(b * 6 + j, half(b, slot, c), half(b, slot, c), (cx, cy, c)).wait_recv()
                fwd = remote(b * 6 + 3 + j, half(b, slot, c), half(b, slot, c), sib)
                fwd.start()
                passed.append(fwd)
        for b in range(n):
            for j, (cx, cy) in enumerate(chips):
                slot = 2 * cx + cy
                remote(b * 6 + 3 + j, half(b, slot, 1 - c), half(b, slot, 1 - c), sib).wait_recv()
        for cp in first + passed:
            cp.wait_send()

    return pl.pallas_call(
        body, name=name, in_specs=[ANY] * n, out_specs=[ANY] * n,
        out_shape=[jax.ShapeDtypeStruct(s.shape, s.dtype) for s in slots],
        input_output_aliases={b: b for b in range(n)},
        scratch_shapes=[pltpu.SemaphoreType.DMA((6 * n,)), pltpu.SemaphoreType.DMA((6 * n,))],
        compiler_params=pltpu.CompilerParams())(*slots)


def pair_exchange(bufs, name):
    n = len(bufs)

    def body(*refs):
        src, dst = refs[:n], refs[n:2 * n]
        send, recv = refs[2 * n:]
        x, y, c = _place()
        cps = []
        for b in range(n):
            rows = bufs[b].shape[1] // 2
            cp = pltpu.make_async_remote_copy(
                src_ref=src[b].at[:, pl.ds((1 - c) * rows, rows), :], dst_ref=dst[b],
                send_sem=send.at[b], recv_sem=recv.at[b], device_id=(x, y, 1 - c), device_id_type=MESH)
            cp.start()
            cps.append(cp)
        for cp in cps:
            cp.wait()

    return pl.pallas_call(
        body, name=name, in_specs=[ANY] * n, out_specs=[ANY] * n,
        out_shape=[jax.ShapeDtypeStruct((NCHIP, b.shape[1] // 2, b.shape[2]), b.dtype) for b in bufs],
        scratch_shapes=[pltpu.SemaphoreType.DMA((n,)), pltpu.SemaphoreType.DMA((n,))],
        compiler_params=pltpu.CompilerParams())(*bufs)


def add_half(full, got, tr, tc, name):
    _, R, C = full.shape
    rows = R // 2
    nr = rows // tr
    c_arr = lax.axis_index("c").astype(jnp.int32).reshape(1)

    def body(c_ref, a_ref, b_ref, o_ref):
        o_ref[...] = (a_ref[...].astype(F32) + b_ref[...].astype(F32)).astype(o_ref.dtype)

    return pl.pallas_call(
        body, name=name,
        grid_spec=pltpu.PrefetchScalarGridSpec(
            num_scalar_prefetch=1, grid=(NCHIP, nr, C // tc),
            in_specs=[pl.BlockSpec((None, tr, tc), lambda s, i, j, c_ref: (s, c_ref[0] * nr + i, j)),
                      pl.BlockSpec((None, tr, tc), lambda s, i, j, c_ref: (s, i, j))],
            out_specs=pl.BlockSpec((None, tr, tc), lambda s, i, j, c_ref: (s, i, j))),
        out_shape=jax.ShapeDtypeStruct((NCHIP, rows, C), full.dtype),
        compiler_params=_cp(("parallel", "parallel", "parallel")))(c_arr, full, got)


def chip_exchange(bufs, name):
    n = len(bufs)

    def body(*refs):
        src, dst = refs[:n], refs[n:2 * n]
        send, recv, lsem = refs[2 * n:]
        x, y, c = _place()
        me = 2 * x + y
        chips = [(1 - x, y), (x, 1 - y), (1 - x, 1 - y)]
        local = [pltpu.make_async_copy(src[b].at[me], dst[b].at[me], lsem.at[b]) for b in range(n)]
        for cp in local:
            cp.start()
        cps = []
        for b in range(n):
            for j, (cx, cy) in enumerate(chips):
                cp = pltpu.make_async_remote_copy(
                    src_ref=src[b].at[2 * cx + cy], dst_ref=dst[b].at[me],
                    send_sem=send.at[b * 3 + j], recv_sem=recv.at[b * 3 + j],
                    device_id=(cx, cy, c), device_id_type=MESH)
                cp.start()
                cps.append((cp, b, cx, cy, j))
        for cp, b, cx, cy, j in cps:
            cp.wait_send()
            pltpu.make_async_remote_copy(
                src_ref=src[b].at[me], dst_ref=dst[b].at[2 * cx + cy],
                send_sem=send.at[b * 3 + j], recv_sem=recv.at[b * 3 + j],
                device_id=(cx, cy, c), device_id_type=MESH).wait_recv()
        for cp in local:
            cp.wait()

    return pl.pallas_call(
        body, name=name, in_specs=[ANY] * n, out_specs=[ANY] * n,
        out_shape=[jax.ShapeDtypeStruct(b.shape, b.dtype) for b in bufs],
        scratch_shapes=[pltpu.SemaphoreType.DMA((3 * n,)), pltpu.SemaphoreType.DMA((3 * n,)),
                        pltpu.SemaphoreType.DMA((n,))],
        compiler_params=pltpu.CompilerParams())(*bufs)


def sum_slots(buf, tr, tc, name):
    _, r, C = buf.shape
    nr = r // tr
    c_arr = lax.axis_index("c").astype(jnp.int32).reshape(1)

    def body(c_ref, a_ref, o_ref):
        o_ref[...] = ((a_ref[0].astype(F32) + a_ref[1].astype(F32)) + a_ref[2].astype(F32)) + a_ref[3].astype(F32)

    return pl.pallas_call(
        body, name=name,
        grid_spec=pltpu.PrefetchScalarGridSpec(
            num_scalar_prefetch=1, grid=(nr, C // tc),
            in_specs=[pl.BlockSpec((NCHIP, tr, tc), lambda i, j, c_ref: (0, i, j))],
            out_specs=pl.BlockSpec((tr, tc), lambda i, j, c_ref: (c_ref[0] * nr + i, j))),
        out_shape=jax.ShapeDtypeStruct((2 * r, C), F32),
        compiler_params=_cp(("parallel", "parallel")))(c_arr, buf)


def pair_share(bufs, name):
    n = len(bufs)

    def body(*refs):
        dst = refs[n:2 * n]
        send, recv = refs[2 * n:]
        x, y, c = _place()
        cps = []
        for b in range(n):
            rows = bufs[b].shape[0] // 2
            here = dst[b].at[pl.ds(c * rows, rows), :]
            cp = pltpu.make_async_remote_copy(src_ref=here, dst_ref=here, send_sem=send.at[b], recv_sem=recv.at[b],
                                              device_id=(x, y, 1 - c), device_id_type=MESH)
            cp.start()
            cps.append((cp, b))
        for cp, b in cps:
            rows = bufs[b].shape[0] // 2
            there = dst[b].at[pl.ds((1 - c) * rows, rows), :]
            cp.wait_send()
            pltpu.make_async_remote_copy(src_ref=there, dst_ref=there, send_sem=send.at[b], recv_sem=recv.at[b],
                                         device_id=(x, y, 1 - c), device_id_type=MESH).wait_recv()

    return pl.pallas_call(
        body, name=name, in_specs=[ANY] * n, out_specs=[ANY] * n,
        out_shape=[jax.ShapeDtypeStruct(b.shape, b.dtype) for b in bufs],
        input_output_aliases={b: b for b in range(n)},
        scratch_shapes=[pltpu.SemaphoreType.DMA((n,)), pltpu.SemaphoreType.DMA((n,))],
        compiler_params=pltpu.CompilerParams())(*bufs)


def _tile(kind, buf):
    return {"A": (256, buf.shape[2]), "B": (buf.shape[1], 1024), "C": (128, 256), "V": (40, 256),
            "E": (40, 1024)}[kind]


def reduce_scatter(parts, tag):
    names = list(parts)
    got = pair_exchange([parts[k] for k in names], tag + "_pair_exchange")
    sums = [add_half(parts[k], got[i], *_tile(k[0], got[i]), name=tag + "_add_pair_" + k) for i, k in enumerate(names)]
    landed = chip_exchange(sums, tag + "_chip_exchange")
    halves = [sum_slots(landed[i], *_tile(k[0], landed[i]), name=tag + "_sum_chips_" + k) for i, k in enumerate(names)]
    full = pair_share(halves, tag + "_pair_share")
    return dict(zip(names, full))


HBM = pl.BlockSpec(memory_space=pltpu.HBM)
SEMS = pl.BlockSpec(memory_space=pltpu.SEMAPHORE)
FLOWS = pltpu.SideEffectType.DATAFLOW_SIDE_EFFECTING


def _in_hbm(a):
    return pltpu.with_memory_space_constraint(a, pltpu.HBM)


def _other_chips():
    x, y, c = _place()
    return 2 * x + y, c, [(1 - x, y), (x, 1 - y), (1 - x, 1 - y)]


def own_slots(bufs):
    me = 2 * lax.axis_index("x") + lax.axis_index("y")
    return [lax.dynamic_update_slice(lax.empty((NCHIP,) + b.shape, b.dtype), b[None], (me, 0, 0)) for b in bufs]


def gather_start(slots, after, name):
    n = len(slots)

    def body(*refs):
        ins = refs[:n]
        send, recv = refs[n + 1], refs[n + 2]
        token = refs[2 * n + 3]
        me, c, chips = _other_chips()
        for b in range(n):
            rows = slots[b].shape[1] // 2
            own = ins[b].at[me, pl.ds(c * rows, rows), :]
            for j, (cx, cy) in enumerate(chips):
                pltpu.make_async_remote_copy(src_ref=own, dst_ref=own, send_sem=send.at[3 * b + j],
                                             recv_sem=recv.at[3 * b + j], device_id=(cx, cy, c),
                                             device_id_type=MESH).start()
        token[...] = jnp.zeros_like(token)

    out = pl.pallas_call(
        body, name=name, in_specs=[HBM] * n + [ANY],
        out_specs=[SEMS, SEMS] + [HBM] * n + [pl.BlockSpec(memory_space=pltpu.VMEM)],
        out_shape=[pltpu.SemaphoreType.DMA((3 * n,)), pltpu.SemaphoreType.DMA((3 * n,))]
        + [pltpu.HBM(s.shape, s.dtype) for s in slots] + [jax.ShapeDtypeStruct((8, 128), F32)],
        input_output_aliases={b: b + 2 for b in range(n)},
        compiler_params=pltpu.CompilerParams(has_side_effects=FLOWS))(*[_in_hbm(s) for s in slots], after)
    return out[0], out[1], list(out[2:2 + n]), out[2 + n]


def gather_wait(send, recv, slots, picks, after, name):
    n = len(slots)

    def body(*refs):
        ins = refs[:n]
        send_, recv_ = refs[n], refs[n + 1]
        me, c, chips = _other_chips()
        for i, b in enumerate(picks):
            rows = slots[i].shape[1] // 2
            own = ins[i].at[me, pl.ds(c * rows, rows), :]
            for j, (cx, cy) in enumerate(chips):
                got = ins[i].at[2 * cx + cy, pl.ds(c * rows, rows), :]
                pltpu.make_async_remote_copy(src_ref=own, dst_ref=own, send_sem=send_.at[3 * b + j],
                                             recv_sem=recv_.at[3 * b + j], device_id=(cx, cy, c),
                                             device_id_type=MESH).wait_send()
                pltpu.make_async_remote_copy(src_ref=got, dst_ref=got, send_sem=send_.at[3 * b + j],
                                             recv_sem=recv_.at[3 * b + j], device_id=(cx, cy, c),
                                             device_id_type=MESH).wait_recv()

    return pl.pallas_call(
        body, name=name, in_specs=[HBM] * n + [SEMS, SEMS, ANY], out_specs=[HBM] * n,
        out_shape=[pltpu.HBM(s.shape, s.dtype) for s in slots],
        input_output_aliases={b: b for b in range(n)},
        compiler_params=pltpu.CompilerParams(has_side_effects=FLOWS))(*slots, send, recv, after)


def gather_pass(slots, name):
    n = len(slots)

    def body(*refs):
        dst = refs[n:2 * n]
        send, recv = refs[2 * n:]
        x, y, c = _place()
        sib = (x, y, 1 - c)
        chips = [(1 - x, y), (x, 1 - y), (1 - x, 1 - y)]

        def half(b, slot, hc):
            rows = slots[b].shape[1] // 2
            return dst[b].at[slot, pl.ds(hc * rows, rows), :]

        passed = []
        for b in range(n):
            for j, (cx, cy) in enumerate(chips):
                slot = 2 * cx + cy
                cp = pltpu.make_async_remote_copy(src_ref=half(b, slot, c), dst_ref=half(b, slot, c),
                                                  send_sem=send.at[3 * b + j], recv_sem=recv.at[3 * b + j],
                                                  device_id=sib, device_id_type=MESH)
                cp.start()
                passed.append(cp)
        for b in range(n):
            for j, (cx, cy) in enumerate(chips):
                slot = 2 * cx + cy
                pltpu.make_async_remote_copy(src_ref=half(b, slot, 1 - c), dst_ref=half(b, slot, 1 - c),
                                             send_sem=send.at[3 * b + j], recv_sem=recv.at[3 * b + j],
                                             device_id=sib, device_id_type=MESH).wait_recv()
        for cp in passed:
            cp.wait_send()

    return pl.pallas_call(
        body, name=name, in_specs=[ANY] * n, out_specs=[ANY] * n,
        out_shape=[jax.ShapeDtypeStruct(s.shape, s.dtype) for s in slots],
        input_output_aliases={b: b for b in range(n)},
        scratch_shapes=[pltpu.SemaphoreType.DMA((3 * n,)), pltpu.SemaphoreType.DMA((3 * n,))],
        compiler_params=pltpu.CompilerParams())(*slots)


def chip_exchange_start(sums, name):
    n = len(sums)
    me_ = 2 * lax.axis_index("x") + lax.axis_index("y")
    landing = [lax.dynamic_update_slice(lax.empty(s.shape, s.dtype),
                                        lax.dynamic_slice(s, (me_, 0, 0), (1,) + s.shape[1:]), (me_, 0, 0)) for s in sums]

    def body(*refs):
        src, land = refs[:n], refs[n:2 * n]
        send, recv = refs[2 * n], refs[2 * n + 1]
        token = refs[4 * n + 2]
        me, c, chips = _other_chips()
        for b in range(n):
            for j, (cx, cy) in enumerate(chips):
                pltpu.make_async_remote_copy(src_ref=src[b].at[2 * cx + cy], dst_ref=land[b].at[me],
                                             send_sem=send.at[3 * b + j], recv_sem=recv.at[3 * b + j],
                                             device_id=(cx, cy, c), device_id_type=MESH).start()
        token[...] = jnp.zeros_like(token)

    out = pl.pallas_call(
        body, name=name, in_specs=[HBM] * (2 * n),
        out_specs=[SEMS, SEMS] + [HBM] * (2 * n) + [pl.BlockSpec(memory_space=pltpu.VMEM)],
        out_shape=[pltpu.SemaphoreType.DMA((3 * n,)), pltpu.SemaphoreType.DMA((3 * n,))]
        + [pltpu.HBM(s.shape, s.dtype) for s in sums + landing] + [jax.ShapeDtypeStruct((8, 128), F32)],
        input_output_aliases={b: b + 2 for b in range(2 * n)},
        compiler_params=pltpu.CompilerParams(has_side_effects=FLOWS))(*[_in_hbm(s) for s in sums + landing])
    return out[0], out[1], list(out[2:2 + n]), list(out[2 + n:2 + 2 * n]), out[2 + 2 * n]


def chip_exchange_wait(send, recv, sums, landing, after, name):
    n = len(sums)

    def body(*refs):
        src, land = refs[:n], refs[n:2 * n]
        send_, recv_ = refs[2 * n], refs[2 * n + 1]
        me, c, chips = _other_chips()
        for b in range(n):
            for j, (cx, cy) in enumerate(chips):
                slot = 2 * cx + cy
                pltpu.make_async_remote_copy(src_ref=src[b].at[slot], dst_ref=land[b].at[me],
                                             send_sem=send_.at[3 * b + j], recv_sem=recv_.at[3 * b + j],
                                             device_id=(cx, cy, c), device_id_type=MESH).wait_send()
                pltpu.make_async_remote_copy(src_ref=src[b].at[me], dst_ref=land[b].at[slot],
                                             send_sem=send_.at[3 * b + j], recv_sem=recv_.at[3 * b + j],
                                             device_id=(cx, cy, c), device_id_type=MESH).wait_recv()

    out = pl.pallas_call(
        body, name=name, in_specs=[HBM] * (2 * n) + [SEMS, SEMS, ANY], out_specs=[HBM] * (2 * n),
        out_shape=[pltpu.HBM(s.shape, s.dtype) for s in sums + landing],
        input_output_aliases={b: b for b in range(2 * n)},
        compiler_params=pltpu.CompilerParams(has_side_effects=FLOWS))(*sums, *landing, send, recv, after)
    return list(out[n:])


def _row(a, l):
    return a[l:l + 1]


def local_step(x, p, tgt, small, weights_of, vecs, a_ws, a_bs, grads_ready):
    T = x.shape[0]
    bsf = jnp.broadcast_to(a_bs[:, :, None], (AG_N, CHUNK, AGW))
    vrow = lambda r: vecs[r:r + 1]
    saved = []
    W = []
    h = x
    GA1 = GB1 = GA = GB = GC = bgrp = None

    def ff_fwd(h, xn, l, which, post, g_next, tok=None):
        wa, wb = (GA1, GB1) if which == 1 else (GA, GB)
        tag = "ff%d_l%d" % (which, l)
        gu, a = ff_gateup(xn, wa, 0, tag + "_gateup")
        gp = _row(post, l) if tok is None else _row(post, l) + tok
        out = mm_rs_post(a, wb, 0, FB, FB, h, gp, 0.5, tag + "_down", g_next=g_next)
        return out[1], (out[2] if g_next is not None else None), (h, xn, gu, a, out[0])

    xn = rms_fwd(h, _row(small["ff1_pre_g"], 0), BF16, "ff1_l0_pre")
    for l in range(4):
        rec = {}
        GA1, GB1, atok = weights_of(l, "a", h)
        g_mix = _row(small["mix_pre_g"], l) if l >= 2 else None
        h, hn, rec["ff1"] = ff_fwd(h, xn, l, 1, small["ff1_post_g"], g_mix, atok)
        GA, GB, GC, wtok = weights_of(l, "b", h)
        W.append((GA1, GB1, GA, GB, GC))
        if l == 1:
            bgrp = GC[:, C_BGRP:C_BGRP + 256, :].reshape(NCHIP, 4, 64, 256).transpose(1, 0, 2, 3).reshape(4, 256, 256)
        tag = "mix_l%d" % l
        h_in = h
        g_ff2 = _row(small["ff2_pre_g"], l)
        if l == 1:
            hn = rms_fwd(h, _row(small["mix_pre_g"], l), F32, tag + "_pre")
            pooled, ypre, f = pool_fwd(hn, bgrp, vrow(V_BSCALE), tag + "_pool")
            rec["mix"] = (h_in, pooled, ypre, f)
            h = post_res(h, f, _row(small["mix_post_g"], l), 1.0, tag + "_post")
            xn = rms_fwd(h, g_ff2, BF16, "ff2_l1_pre")
        else:
            gpost = _row(small["mix_post_g"], l)
            if l == 0:
                g0 = _row(small["mix_pre_g"], l)
                hn = rms_fwd(h, g0 if wtok is None else g0 + wtok, BF16, tag + "_pre")
                zpre = mm_cs(hn, GA, A_AIN, 1536, 1536, BF16, tag + "_in")
                y = gmlp_mid_fwd(zpre, small["a_v_norm_g"], small["a_v_norm_b"], a_ws, bsf, tag + "_gate")
                f, h, xn = mm_rs_post(y, GB, B_AOUT, 768, 768, h, gpost, 1.0, tag + "_out", g_next=g_ff2)
                rec["mix"] = (h_in, hn, zpre, y, f)
            elif l == 2:
                ag = mm_cs(hn, GA, A_CIN, 512, 512, BF16, tag + "_pw1")
                zc = conf_conv_fwd(ag, vecs[V_CDW:V_CDW + 32], vrow(V_CBDW), tag + "_conv")
                zs = conf_ln_fwd(zc, vrow(V_CNG), vrow(V_CNB), tag + "_ln")
                f, h, xn = mm_rs_post(zs, GB, B_CPW2, 256, 256, h, gpost, 1.0, tag + "_pw2", g_next=g_ff2)
                rec["mix"] = (h_in, hn, ag, zc, zs, f)
            else:
                bgx = mm_cs(hn, GA, A_DIN, 768, 768, BF16, tag + "_in")
                y = sconv_fwd(bgx, vecs[V_DCONV:V_DCONV + 8], tag + "_conv")
                f, h, xn = mm_rs_post(y, GB, B_DOUT, 256, 256, h, gpost, 1.0, tag + "_out", g_next=g_ff2)
                rec["mix"] = (h_in, hn, bgx, y, f)
        h, xn, rec["ff2"] = ff_fwd(h, xn, l, 2, small["ff2_post_g"], _row(small["ple_gate_norm_g"], l))
        tag = "ple_l%d" % l
        pb = p[l].astype(BF16)
        h_in = h
        g_next = _row(small["ff1_pre_g"], l + 1) if l < 3 else None
        out = ple_fwd(h, xn, pb, GB, B_PLEG(l), GC, _row(small["ple_post_g"], l), tag, g_next=g_next)
        rec["ple"] = (h_in, xn, out[0], out[1], pb)
        h = out[2]
        xn = out[3] if l < 3 else None
        saved.append(rec)

    dh, loss_cols = loss_head(h, tgt, "loss_head")

    dA2 = dB2 = None
    layer_grads = [None] * 4
    tok = None
    gV = {}
    gains = {k: [None] * 4 for k in ("ff1_pre_g", "ff1_post_g", "mix_pre_g", "mix_post_g", "ff2_pre_g", "ff2_post_g",
                                      "ple_gate_norm_g", "ple_post_g")}
    extra = {}

    def ff_bwd(dh, l, which, pre, post, rec, after=None):
        wa, wb = (GA1, GB1) if which == 1 else (GA, GB)
        tag = "ff%d_l%d_b" % (which, l)
        h_in, xn, gu, a, f = rec
        gp = _row(post, l) if after is None else _row(post, l) + after
        df, dpost, dgu = ff_bwd_down(dh, f, gp, wb, 0, gu, tag + "_down")
        if which == 1:
            db = dw_rs(a, df, FB, FB, tag + "_dwdown")
        else:
            db = dw_rs(a, df, FB, FB, tag + "_dwdown", height=B2_ROWS(l), off=B_FF2D(l), into=dB2)
        dh_in, dpre = mm_cs_t_rms(dgu, wa, 0, 2 * FB, 2 * FB, h_in, _row(pre, l), dh, tag + "_gateup")
        da = dw_cs(xn, dgu, 2 * FB, 2 * FB, tag + "_dwgateup", width=None if which == 1 else A2_COLS(l))
        return dh_in, dpre, dpost, (da, db)

    for l in reversed(range(4)):
        rec = saved[l]
        GA1, GB1, GA, GB, GC = W[l]
        gC = {}
        tag = "ple_l%d_b" % l
        h_in, xn, zg, pe, pb = rec["ple"]
        gpost = _row(small["ple_post_g"], l)
        if tok is not None:
            gpost = gpost + tok
        dzg, dpe, dh, gains["ple_post_g"][l], gains["ple_gate_norm_g"][l] = ple_bwd(
            dh, zg, pe, h_in, GB, B_PLEG(l), gpost, _row(small["ple_gate_norm_g"], l), tag)
        gC[C_PROJ(l)] = dw_cs(pb, dpe, 256, 256, tag + "_dwproj")
        dB2 = dw_rs(xn, dzg, 256, 256, tag + "_dwgate", height=B2_ROWS(l), off=B_PLEG(l))

        dh, gains["ff2_pre_g"][l], gains["ff2_post_g"][l], (dA2, dB2) = ff_bwd(
            dh, l, 2, small["ff2_pre_g"], small["ff2_post_g"], rec["ff2"])

        tag = "mix_l%d_b" % l
        mix = rec["mix"]
        h_in, f = mix[0], mix[-1]
        if l == 1:
            _, pooled, ypre, _ = mix
            df, gains["mix_post_g"][l] = post_res_bwd(dh, f, _row(small["mix_post_g"], l), 1.0, F32, tag + "_post")
            dhn, dwg, dsc = pool_bwd(df, ypre, pooled, bgrp, vrow(V_BSCALE), tag + "_pool")
            gC[C_BGRP] = dwg.astype(BF16).reshape(4, NCHIP, 64, 256).transpose(1, 0, 2, 3).reshape(NCHIP, 256, 256)
            gV[V_BSCALE] = jnp.pad(dsc, ((0, 7), (0, 0)))
            dh, gains["mix_pre_g"][l] = rms_bwd(dhn, h_in, _row(small["mix_pre_g"], l), dh, tag + "_pre")
        else:
            gpre = _row(small["mix_pre_g"], l)
            df, gains["mix_post_g"][l] = post_res_bwd(dh, f, _row(small["mix_post_g"], l), 1.0, BF16, tag + "_post")
            if l == 0:
                _, hn, zpre, y, _ = mix
                dy = mm_rs_t(df, GB, B_AOUT, 768, 768, tag + "_out")
                dB2 = dw_rs(y, df, 768, 768, tag + "_dwout", height=B2_ROWS(l), off=B_AOUT, into=dB2)
                dz, dws, dbs, dvg, dvb = gmlp_mid_bwd(zpre, dy, small["a_v_norm_g"], small["a_v_norm_b"], a_ws, bsf,
                                                      tag + "_gate")
                extra.update(a_w_s=dws, a_b_s=dbs[:, 0, :], a_v_norm_g=dvg, a_v_norm_b=dvb)
                dA2 = dw_cs(hn, dz, 1536, 1536, tag + "_dwin", width=A2_COLS(l), off=A_AIN, into=dA2)
                dh, gains["mix_pre_g"][l] = mm_cs_t_rms(dz, GA, A_AIN, 1536, 1536, h_in, gpre, dh, tag + "_in")
            elif l == 2:
                _, hn, ag, zc, zs, _ = mix
                dzs = mm_rs_t(df, GB, B_CPW2, 256, 256, tag + "_pw2")
                dB2 = dw_rs(zs, df, 256, 256, tag + "_dwpw2", height=B2_ROWS(l), off=B_CPW2, into=dB2)
                dzc, dng, dnb = conf_ln_bwd(dzs, zc, vrow(V_CNG), vrow(V_CNB), tag + "_ln")
                da_, dg_, dwdw, dbdw = conf_conv_bwd(dzc, ag, vecs[V_CDW:V_CDW + 32], tag + "_conv")
                dag = merge_cols([da_, dg_], tag + "_merge")
                gV[V_CDW] = dwdw
                gV[V_CBDW] = jnp.pad(dbdw, ((0, 7), (0, 0)))
                gV[V_CNG] = jnp.pad(dng, ((0, 7), (0, 0)))
                gV[V_CNB] = jnp.pad(dnb, ((0, 7), (0, 0)))
                dA2 = dw_cs(hn, dag, 512, 512, tag + "_dwpw1", width=A2_COLS(l), off=A_CIN, into=dA2)
                dh, gains["mix_pre_g"][l] = mm_cs_t_rms(dag, GA, A_CIN, 512, 512, h_in, gpre, dh, tag + "_pw1")
            else:
                _, hn, bgx, y, _ = mix
                dy = mm_rs_t(df, GB, B_DOUT, 256, 256, tag + "_out")
                dB2 = dw_rs(y, df, 256, 256, tag + "_dwout", height=B2_ROWS(l), off=B_DOUT, into=dB2)
                db_, dc_, dx_, dwc = sconv_bwd(dy, bgx, vecs[V_DCONV:V_DCONV + 8], tag + "_conv")
                dbgx = merge_cols([db_, dc_, dx_], tag + "_merge")
                gV[V_DCONV] = dwc
                dA2 = dw_cs(hn, dbgx, 768, 768, tag + "_dwin", width=A2_COLS(l), off=A_DIN, into=dA2)
                dh, gains["mix_pre_g"][l] = mm_cs_t_rms(dbgx, GA, A_DIN, 768, 768, h_in, gpre, dh, tag + "_in")

        dC = jnp.concatenate([gC[C_PROJ(l)]] + ([gC[C_BGRP]] if l == 1 else []), axis=1)
        tok = grads_ready(l, "b", (dA2, dB2, dC), dh)
        dh, gains["ff1_pre_g"][l], gains["ff1_post_g"][l], (dA1, dB1) = ff_bwd(
            dh, l, 1, small["ff1_pre_g"], small["ff1_post_g"], rec["ff1"], after=tok)
        layer_grads[l] = (dA1, dB1, dA2, dB2, dC)
        tok = grads_ready(l, "a", (dA1, dB1), dh)

    return loss_cols, dh, layer_grads, gV, gains, extra


GAIN_NAMES = ("ff1_pre_g", "ff1_post_g", "mix_pre_g", "mix_post_g", "ff2_pre_g", "ff2_post_g", "ple_gate_norm_g",
              "ple_post_g")


def _pad_rows(a, rows):
    return jnp.pad(a, ((0, rows - a.shape[0]), (0, 0)))


def kernel(x, p, ff1_pre_g, ff1_w_gate, ff1_w_up, ff1_w_down, ff1_post_g, mix_pre_g, mix_post_g, ff2_pre_g, ff2_w_gate, ff2_w_up, ff2_w_down, ff2_post_g, ple_gate_norm_g, ple_w_gate, ple_w_proj, ple_post_g, a_w_in, a_v_norm_g, a_v_norm_b, a_w_s, a_b_s, a_w_out, b_w_grp, b_scale, c_w_pw1, c_w_dw, c_b_dw, c_norm_g, c_norm_b, c_w_pw2, d_w_in, d_w_conv, d_w_out, loss_target, m_ff1_pre_g, m_ff1_w_gate, m_ff1_w_up, m_ff1_w_down, m_ff1_post_g, m_mix_pre_g, m_mix_post_g, m_ff2_pre_g, m_ff2_w_gate, m_ff2_w_up, m_ff2_w_down, m_ff2_post_g, m_ple_gate_norm_g, m_ple_w_gate, m_ple_w_proj, m_ple_post_g, m_a_w_in, m_a_v_norm_g, m_a_v_norm_b, m_a_w_s, m_a_b_s, m_a_w_out, m_b_w_grp, m_b_scale, m_c_w_pw1, m_c_w_dw, m_c_b_dw, m_c_norm_g, m_c_norm_b, m_c_w_pw2, m_d_w_in, m_d_w_conv, m_d_w_out, v_ff1_pre_g, v_ff1_w_gate, v_ff1_w_up, v_ff1_w_down, v_ff1_post_g, v_mix_pre_g, v_mix_post_g, v_ff2_pre_g, v_ff2_w_gate, v_ff2_w_up, v_ff2_w_down, v_ff2_post_g, v_ple_gate_norm_g, v_ple_w_gate, v_ple_w_proj, v_ple_post_g, v_a_w_in, v_a_v_norm_g, v_a_v_norm_b, v_a_w_s, v_a_b_s, v_a_w_out, v_b_w_grp, v_b_scale, v_c_w_pw1, v_c_w_dw, v_c_b_dw, v_c_norm_g, v_c_norm_b, v_c_w_pw2, v_d_w_in, v_d_w_conv, v_d_w_out):
    args = dict(locals())
    wnames = ["ff1_pre_g", "ff1_w_gate", "ff1_w_up", "ff1_w_down", "ff1_post_g", "mix_pre_g", "mix_post_g",
              "ff2_pre_g", "ff2_w_gate", "ff2_w_up", "ff2_w_down", "ff2_post_g", "ple_gate_norm_g", "ple_w_gate",
              "ple_w_proj", "ple_post_g", "a_w_in", "a_v_norm_g", "a_v_norm_b", "a_w_s", "a_b_s", "a_w_out",
              "b_w_grp", "b_scale", "c_w_pw1", "c_w_dw", "c_b_dw", "c_norm_g", "c_norm_b", "c_w_pw2", "d_w_in",
              "d_w_conv", "d_w_out"]

    P = pack_weights(args)
    G0a = all_gather_chips([P[0][0], P[0][1], P[4]], "gather_l0a")
    vecs = G0a[2].transpose(1, 0, 2).reshape(V_ROWS, DM)
    flying = {}

    def start(key, bufs, after):
        send, recv, slots, token = gather_start(own_slots(list(bufs)), after, "gather_start_l" + key)
        flying[key] = (send, recv, slots)
        return token[0, 0]

    tok = start("0b", P[0][2:], G0a[0])
    arrived = {}

    def weights_of(l, part, h):
        if l == 0 and part == "a":
            return G0a[0], G0a[1], None
        key = "0b" if l == 0 else str(l)
        wtok = None
        if key not in arrived:
            send, recv, slots = flying[key]
            n = len(slots)
            landed = gather_wait(send, recv, slots, list(range(n)), h, "gather_wait_l" + key)
            arrived[key] = gather_pass(landed, "gather_pass_l" + key)
            if l < 3:
                wtok = start(str(l + 1), P[l + 1], arrived[key][0])
        got = arrived[key]
        if l == 0:
            return tuple(got) + (wtok,)
        return tuple(got[:2]) + (wtok,) if part == "a" else tuple(got[2:]) + (None,)

    pending = {}
    reduced = {}
    held = {}

    def finish(key, after):
        kinds, send, recv, sums, landing = pending.pop(key)
        landed = chip_exchange_wait(send, recv, sums, landing, after, "rs_wait_l" + key)
        halves = [sum_slots(landed[i], *_tile(k, landed[i]), name="rs_sum_chips_l%s_%d%s" % (key, i, k))
                  for i, k in enumerate(kinds)]
        reduced[key] = pair_share(halves, "rs_pair_share_l" + key)

    def grads_ready(l, part, bufs, dh):
        if part == "b" and l > 0:
            held[l] = list(bufs)
            return None
        if part == "a" and l > 0:
            key, kinds, parts = str(l), "ABABC", list(bufs) + held.pop(l)
        elif part == "b":
            key, kinds, parts = "0b", "ABC", list(bufs)
        else:
            finish("0b", dh)
            return None
        for other in list(pending):
            finish(other, dh)
        got = pair_exchange(parts, "rs_pair_exchange_l" + key)
        sums = [add_half(parts[i], got[i], *_tile(k, got[i]), name="rs_add_pair_l%s_%d%s" % (key, i, k))
                for i, k in enumerate(kinds)]
        send, recv, sums, landing, token = chip_exchange_start(sums, "rs_start_l" + key)
        pending[key] = (kinds, send, recv, sums, landing)
        return token[0, 0]

    small = {k: args[k] for k in GAIN_NAMES}
    small["ff1_pre_g"] = ff1_pre_g + tok
    small["a_v_norm_g"] = a_v_norm_g
    small["a_v_norm_b"] = a_v_norm_b
    loss_cols, grad_x, layer_grads, gV, gains, extra = local_step(
        x[0], p[:, 0], loss_target[0], small, weights_of, vecs, a_w_s[0], a_b_s[0], grads_ready)

    loss = lax.psum((0.5 / DM) * jnp.sum(loss_cols), ("x", "y", "c"))

    deltas, new_m, new_v = {}, {}, {}

    def update(k, g):
        if args[k].shape[-1] == FW:
            t = lambda a: jnp.swapaxes(a, 1, 2)
            outs = adamw(t(args[k]), t(g), t(args["m_" + k]), t(args["v_" + k]), "adamw_" + k)
            deltas[k], new_m[k], new_v[k] = (t(o) for o in outs)
        else:
            deltas[k], new_m[k], new_v[k] = adamw(args[k], g, args["m_" + k], args["v_" + k], "adamw_" + k)

    dV, dE = pack_small_grads(gV, gains, extra)
    red = reduce_scatter({"A": layer_grads[0][0], "B": layer_grads[0][1], "V": dV, "E": dE}, "rs_l0a")
    (gE,) = all_gather_chips([red["E"]], "gather_replicated_grads")
    per_layer = [[red["A"], red["B"]] + list(reduced["0b"])] + [list(reduced[str(l)]) for l in (1, 2, 3)]
    grads = unpack_grads(per_layer, red["V"], gE.reshape(E_ROWS, DM))
    for k in wnames:
        update(k, grads[k])
    return (loss, grad_x[None], *[grads[k] for k in wnames], *[deltas[k] for k in wnames],
            *[new_m[k] for k in wnames], *[new_v[k] for k in wnames])


def pack_weights(w):
    padc = lambda a: jnp.pad(a, ((0, 0), (0, FB - FW)))
    mix_in = [w["a_w_in"][0], None, w["c_w_pw1"][0], w["d_w_in"][0]]
    mix_out = [w["a_w_out"][0], None, w["c_w_pw2"][0], w["d_w_out"][0]]
    packed = []
    for l in range(4):
        a1 = jnp.concatenate([padc(w["ff1_w_gate"][l]), padc(w["ff1_w_up"][l])], axis=1).astype(BF16)
        b1 = _pad_rows(w["ff1_w_down"][l], FB).astype(BF16)
        cols = [padc(w["ff2_w_gate"][l]), padc(w["ff2_w_up"][l])]
        rows = [_pad_rows(w["ff2_w_down"][l], FB)]
        if l != 1:
            cols.append(mix_in[l])
            rows.append(mix_out[l])
        rows.append(w["ple_w_gate"][l])
        proj = [w["ple_w_proj"][l]] + ([w["b_w_grp"][0].reshape(256, 256)] if l == 1 else [])
        packed.append((a1, b1, jnp.concatenate(cols, axis=1).astype(BF16), jnp.concatenate(rows, axis=0).astype(BF16),
                       jnp.concatenate(proj, axis=0).astype(BF16)))
    PV = jnp.concatenate([_pad_rows(w["b_scale"], 8), _pad_rows(w["c_b_dw"], 8), _pad_rows(w["c_norm_g"], 8),
                          _pad_rows(w["c_norm_b"], 8), _pad_rows(w["d_w_conv"][0], 8), _pad_rows(w["c_w_dw"][0], 40)],
                         axis=0)
    return packed + [PV]


def pack_small_grads(gV, gains, extra):
    dVt = jnp.concatenate([gV[V_BSCALE], gV[V_CBDW], gV[V_CNG], gV[V_CNB], gV[V_DCONV], gV[V_CDW],
                           jnp.zeros((8, DM), F32)], axis=0)
    dV = dVt.reshape(V_ROWS, NCHIP, 256).transpose(1, 0, 2)
    rowsE = [_pad_rows(jnp.concatenate(gains[k], axis=0), 8) for k in GAIN_NAMES]
    rowsE += [_pad_rows(extra["a_v_norm_g"].reshape(3, DM), 8), _pad_rows(extra["a_v_norm_b"].reshape(3, DM), 8),
              jnp.pad(extra["a_b_s"].reshape(1536), (0, 8 * DM - 1536)).reshape(8, DM),
              extra["a_w_s"].reshape(192, DM)]
    dE = _pad_rows(jnp.concatenate(rowsE, axis=0), E_ROWS).reshape(NCHIP, E_ROWS // NCHIP, DM)
    return dV, dE


def unpack_grads(per_layer, RV, gE):
    grads = {}
    for i, k in enumerate(GAIN_NAMES):
        grads[k] = lambda i=i: gE[8 * i:8 * i + 4]
    grads["a_v_norm_g"] = lambda: gE[64:67].reshape(1, 3072)
    grads["a_v_norm_b"] = lambda: gE[72:75].reshape(1, 3072)
    grads["a_b_s"] = lambda: gE[80:88].reshape(8 * DM)[:1536].reshape(1, 12, 128)
    grads["a_w_s"] = lambda: gE[88:280].reshape(1, 12, 128, 128)
    col1 = lambda l, off, n: per_layer[l][0][:, off:off + n]
    col2 = lambda l, off, n: per_layer[l][2][:, off:off + n]
    grads["ff1_w_gate"] = lambda: jnp.stack([col1(l, A_FF(l, 0), FW) for l in range(4)])
    grads["ff1_w_up"] = lambda: jnp.stack([col1(l, A_FF(l, 1), FW) for l in range(4)])
    grads["ff2_w_gate"] = lambda: jnp.stack([col2(l, A_FF(l, 2), FW) for l in range(4)])
    grads["ff2_w_up"] = lambda: jnp.stack([col2(l, A_FF(l, 3), FW) for l in range(4)])
    grads["a_w_in"] = lambda: col2(0, A_AIN, 1536)[None]
    grads["c_w_pw1"] = lambda: col2(2, A_CIN, 512)[None]
    grads["d_w_in"] = lambda: col2(3, A_DIN, 768)[None]
    row2 = lambda l, off, n: per_layer[l][3][off:off + n]
    grads["ff1_w_down"] = lambda: jnp.stack([per_layer[l][1][:FW] for l in range(4)])
    grads["ff2_w_down"] = lambda: jnp.stack([row2(l, B_FF2D(l), FW) for l in range(4)])
    grads["ple_w_gate"] = lambda: jnp.stack([row2(l, B_PLEG(l), 256) for l in range(4)])
    grads["a_w_out"] = lambda: row2(0, B_AOUT, 768)[None]
    grads["c_w_pw2"] = lambda: row2(2, B_CPW2, 256)[None]
    grads["d_w_out"] = lambda: row2(3, B_DOUT, 256)[None]
    grads["ple_w_proj"] = lambda: jnp.stack([per_layer[l][4][C_PROJ(l):C_PROJ(l) + 256] for l in range(4)])
    grads["b_w_grp"] = lambda: per_layer[1][4][C_BGRP:C_BGRP + 256].reshape(1, 4, 64, 256)
    grads["b_scale"] = lambda: RV[V_BSCALE:V_BSCALE + 1]
    grads["c_b_dw"] = lambda: RV[V_CBDW:V_CBDW + 1]
    grads["c_norm_g"] = lambda: RV[V_CNG:V_CNG + 1]
    grads["c_norm_b"] = lambda: RV[V_CNB:V_CNB + 1]
    grads["d_w_conv"] = lambda: RV[V_DCONV:V_DCONV + 3][None]
    grads["c_w_dw"] = lambda: RV[V_CDW:V_CDW + 31][None]
    return {k: f() for k, f in grads.items()}
```

```python
import functools
import math

import jax
import jax.numpy as jnp
from jax import lax
from jax.experimental import pallas as pl
from jax.experimental.pallas import tpu as pltpu

F32, BF16 = jnp.float32, jnp.bfloat16
EPS = 1e-6
DM = 1024
FW = 704
FB = 768
NCHIP = 4
VMEM_LIMIT = 56 * 1024 * 1024
ANY = pl.BlockSpec(memory_space=pl.ANY)
MESH = pl.DeviceIdType.MESH

A_FF = lambda l, j: (j % 2) * FB
A_AIN = A_CIN = A_DIN = 2 * FB
A2_COLS = lambda l: 2 * FB + (1536, 0, 512, 768)[l]
B_FF1D = lambda l: 0
B_FF2D = lambda l: 0
B_AOUT = B_CPW2 = B_DOUT = FB
B_PLEG = lambda l: FB + (768, 0, 256, 256)[l]
B2_ROWS = lambda l: B_PLEG(l) + 256
C_PROJ = lambda l: 0
C_BGRP = 256
V_BSCALE, V_CBDW, V_CNG, V_CNB, V_DCONV, V_CDW, V_ROWS = 0, 8, 16, 24, 32, 40, 80
E_ROWS = 320

ADAM_LR, ADAM_B1, ADAM_B2, ADAM_EPS, ADAM_WD, ADAM_STEP = 0.001, 0.9, 0.999, 1e-08, 0.01, 10


def _cp(sem):
    return pltpu.CompilerParams(dimension_semantics=sem, vmem_limit_bytes=VMEM_LIMIT)


def _sig(x):
    return 0.5 * jnp.tanh(0.5 * x) + 0.5


_GC = math.sqrt(2.0 / math.pi)


def _gelu(x):
    return 0.5 * x * (1.0 + jnp.tanh(_GC * (x + 0.044715 * x * x * x)))


def _gelu_grad(x):
    t = jnp.tanh(_GC * (x + 0.044715 * x * x * x))
    return 0.5 * (1.0 + t) + 0.5 * x * (1.0 - t * t) * _GC * (1.0 + 3.0 * 0.044715 * x * x)


def _dot_nn(a, b):
    return lax.dot_general(a, b, (((1,), (0,)), ((), ())), preferred_element_type=F32)


def _dot_nt(a, b):
    return lax.dot_general(a, b, (((1,), (1,)), ((), ())), preferred_element_type=F32)


def _dot_tn(a, b):
    return lax.dot_general(a, b, (((0,), (0,)), ((), ())), preferred_element_type=F32)


def mm_cs(x, G, off, nb, tn, out_dtype, name, roff=0):
    T, K = x.shape
    tm = min(1024, T)
    nj, ob, rb_ = nb // tn, off // tn, roff // K
    assert nb % tn == 0 and off % tn == 0 and roff % K == 0

    def body(x_ref, w_ref, o_ref):
        o_ref[...] = _dot_nn(x_ref[...], w_ref[...]).astype(o_ref.dtype)

    return pl.pallas_call(
        body, name=name, grid=(T // tm, NCHIP, nj),
        in_specs=[pl.BlockSpec((tm, K), lambda i, s, j: (i, 0)),
                  pl.BlockSpec((None, K, tn), lambda i, s, j: (s, rb_, ob + j))],
        out_specs=pl.BlockSpec((tm, tn), lambda i, s, j: (i, s * nj + j)),
        out_shape=jax.ShapeDtypeStruct((T, NCHIP * nb), out_dtype),
        compiler_params=_cp(("parallel", "arbitrary", "arbitrary")))(x, G)


def mm_rs_t(dy, G, off, rb, tk, name):
    T, N = dy.shape
    tm = min(1024, T)
    nkk, ob = rb // tk, off // tk
    nk = NCHIP * nkk

    def body(dy_ref, w_ref, o_ref):
        o_ref[...] = _dot_nt(dy_ref[...], w_ref[...]).astype(o_ref.dtype)

    return pl.pallas_call(
        body, name=name, grid=(T // tm, nk),
        in_specs=[pl.BlockSpec((tm, N), lambda i, k: (i, 0)),
                  pl.BlockSpec((None, tk, N), lambda i, k: (k // nkk, ob + k % nkk, 0))],
        out_specs=pl.BlockSpec((tm, tk), lambda i, k: (i, k)),
        out_shape=jax.ShapeDtypeStruct((T, NCHIP * rb), BF16),
        compiler_params=_cp(("parallel", "arbitrary")))(dy, G)


def mm_tn(a, b, tmm, tn, out_shape, out_map, name, into=None):
    T, M = a.shape
    N = b.shape[1]
    tt = min(2048, T)
    nt = T // tt

    def body(a_ref, b_ref, o_ref, acc):
        t = pl.program_id(2)

        @pl.when(t == 0)
        def _():
            acc[...] = jnp.zeros_like(acc)

        acc[...] += _dot_tn(a_ref[...], b_ref[...])

        @pl.when(t == nt - 1)
        def _():
            o_ref[...] = acc[...].astype(o_ref.dtype)

    in_specs = [pl.BlockSpec((tt, tmm), lambda i, j, t: (t, i)), pl.BlockSpec((tt, tn), lambda i, j, t: (t, j))]
    operands = (a, b)
    if into is None:
        def kern(a_ref, b_ref, o_ref, acc):
            body(a_ref, b_ref, o_ref, acc)
        aliases = {}
    else:
        def kern(a_ref, b_ref, into_ref, o_ref, acc):
            body(a_ref, b_ref, o_ref, acc)
        in_specs.append(ANY)
        operands = (a, b, into)
        aliases = {2: 0}
        out_shape = into.shape
    return pl.pallas_call(
        kern, name=name, grid=(M // tmm, N // tn, nt), in_specs=in_specs,
        out_specs=pl.BlockSpec((None, tmm, tn), lambda i, j, t: out_map(i, j)),
        out_shape=jax.ShapeDtypeStruct(out_shape, BF16), input_output_aliases=aliases,
        scratch_shapes=[pltpu.VMEM((tmm, tn), F32)],
        compiler_params=_cp(("parallel", "parallel", "arbitrary")))(*operands)


def dw_cs(x, dy, nb, tn, name, width=None, off=0, into=None):
    K = x.shape[1]
    nj, ob = nb // tn, off // tn
    assert off % tn == 0
    return mm_tn(x, dy, K, tn, (NCHIP, K, width or nb), lambda i, j: (j // nj, 0, ob + j % nj), name, into)


def dw_rs(a, dy, rb, tr, name, height=None, off=0, into=None):
    N = dy.shape[1]
    ni, ob = rb // tr, off // tr
    assert off % tr == 0
    return mm_tn(a, dy, tr, N, (NCHIP, height or rb, N), lambda i, j: (i // ni, ob + i % ni, 0), name, into)


def ff_gateup(xn, GA, off, name):
    T, K = xn.shape
    tm = min(1024, T)
    ob = off // (2 * FB)
    assert off % (2 * FB) == 0

    sub = min(512, tm)

    def body(x_ref, w_ref, gu_ref, a_ref):
        for r0 in range(0, tm, sub):
            r = _dot_nn(x_ref[r0:r0 + sub, :], w_ref[...])
            g, u = r[:, :FB], r[:, FB:]
            gu_ref[r0:r0 + sub, :] = r.astype(gu_ref.dtype)
            a_ref[r0:r0 + sub, :] = (g * _sig(g) * u).astype(a_ref.dtype)

    return pl.pallas_call(
        body, name=name, grid=(T // tm, NCHIP),
        in_specs=[pl.BlockSpec((tm, K), lambda i, s: (i, 0)),
                  pl.BlockSpec((None, K, 2 * FB), lambda i, s: (s, 0, ob))],
        out_specs=[pl.BlockSpec((tm, 2 * FB), lambda i, s: (i, s)), pl.BlockSpec((tm, FB), lambda i, s: (i, s))],
        out_shape=[jax.ShapeDtypeStruct((T, NCHIP * 2 * FB), BF16), jax.ShapeDtypeStruct((T, NCHIP * FB), BF16)],
        compiler_params=_cp(("parallel", "arbitrary")))(xn, GA)


def mm_rs_post(a, G, off, rb, tk, h, g, scale, name, g_next=None):
    T = a.shape[0]
    N = G.shape[2]
    tm = min(1024, T)
    sub = min(512, tm)
    nkk, ob = rb // tk, off // tk
    nk = NCHIP * nkk
    assert rb % tk == 0 and off % tk == 0
    more = g_next is not None

    def body(a_ref, w_ref, h_ref, g_ref, *rest):
        if more:
            gn_ref, f_ref, o_ref, xn_ref, acc = rest
        else:
            f_ref, o_ref, acc = rest
        k = pl.program_id(1)

        @pl.when(k == 0)
        def _():
            acc[...] = jnp.zeros_like(acc)

        acc[...] += _dot_nn(a_ref[...], w_ref[...])

        @pl.when(k == nk - 1)
        def _():
            for r0 in range(0, tm, sub):
                f = acc[r0:r0 + sub, :]
                f_ref[r0:r0 + sub, :] = f
                r = lax.rsqrt(jnp.mean(f * f, axis=-1, keepdims=True) + EPS)
                hn = h_ref[r0:r0 + sub, :] + scale * (f * r * g_ref[...])
                o_ref[r0:r0 + sub, :] = hn
                if more:
                    rn = lax.rsqrt(jnp.mean(hn * hn, axis=-1, keepdims=True) + EPS)
                    xn_ref[r0:r0 + sub, :] = (hn * rn * gn_ref[...]).astype(xn_ref.dtype)

    row = pl.BlockSpec((tm, N), lambda i, k: (i, 0))
    row1 = pl.BlockSpec((tm, N), lambda i, k: (i, 0), pipeline_mode=pl.Buffered(1))
    vec = pl.BlockSpec((1, N), lambda i, k: (0, 0))
    return pl.pallas_call(
        body, name=name, grid=(T // tm, nk),
        in_specs=[pl.BlockSpec((tm, tk), lambda i, k: (i, k)),
                  pl.BlockSpec((None, tk, N), lambda i, k: (k // nkk, ob + k % nkk, 0)),
                  row1, vec] + ([vec] if more else []),
        out_specs=[row, row] + ([row] if more else []),
        out_shape=[jax.ShapeDtypeStruct((T, N), F32), jax.ShapeDtypeStruct((T, N), F32)]
        + ([jax.ShapeDtypeStruct((T, N), BF16)] if more else []),
        scratch_shapes=[pltpu.VMEM((tm, N), F32)],
        compiler_params=_cp(("parallel", "arbitrary")))(*((a, G, h, g) + ((g_next,) if more else ())))


def ff_bwd_down(dh, f, g, GB, down, gu, name):
    T, N = dh.shape
    tm = min(1024, T)
    sub = min(256, tm)
    ob = down // FB

    def body(d_ref, f_ref, g_ref, w_ref, gu_ref, df_ref, dg_ref, dgu_ref):
        i, s = pl.program_id(0), pl.program_id(1)

        @pl.when(s == 0)
        def _():
            dg = jnp.zeros((1, N), F32)
            for r0 in range(0, tm, sub):
                f = f_ref[r0:r0 + sub, :]
                r = lax.rsqrt(jnp.mean(f * f, axis=-1, keepdims=True) + EPS)
                d = 0.5 * d_ref[r0:r0 + sub, :]
                t = d * g_ref[...]
                df_ref[r0:r0 + sub, :] = (
                    r * t - f * (r * r * r * jnp.mean(t * f, axis=-1, keepdims=True))).astype(df_ref.dtype)
                dg = dg + jnp.sum(d * f * r, axis=0, keepdims=True)
            _acc_rows(dg_ref, i, dg)

        for r0 in range(0, tm, sub):
            da = _dot_nt(df_ref[r0:r0 + sub, :], w_ref[...])
            gt = gu_ref[r0:r0 + sub, :FB].astype(F32)
            u = gu_ref[r0:r0 + sub, FB:].astype(F32)
            sg = _sig(gt)
            silu = gt * sg
            dgu_ref[r0:r0 + sub, :FB] = (da * u * (sg + silu - silu * sg)).astype(dgu_ref.dtype)
            dgu_ref[r0:r0 + sub, FB:] = (da * silu).astype(dgu_ref.dtype)

    row1 = pl.BlockSpec((tm, N), lambda i, s: (i, 0), pipeline_mode=pl.Buffered(1))
    row = pl.BlockSpec((tm, N), lambda i, s: (i, 0))
    vec = pl.BlockSpec((1, N), lambda i, s: (0, 0))
    return pl.pallas_call(
        body, name=name, grid=(T // tm, NCHIP),
        in_specs=[row1, row1, vec, pl.BlockSpec((None, FB, N), lambda i, s: (s, ob, 0)),
                  pl.BlockSpec((tm, 2 * FB), lambda i, s: (i, s))],
        out_specs=[row, vec, pl.BlockSpec((tm, 2 * FB), lambda i, s: (i, s))],
        out_shape=[jax.ShapeDtypeStruct((T, N), BF16), jax.ShapeDtypeStruct((1, N), F32),
                   jax.ShapeDtypeStruct((T, NCHIP * 2 * FB), BF16)],
        compiler_params=_cp(("arbitrary", "arbitrary")))(dh, f, g, GB, gu)


def mm_cs_t_rms(dy, G, off, nb, tn, h, g, skip, name):
    T = dy.shape[0]
    K = G.shape[1]
    tm = min(1024, T)
    sub = min(512, tm)
    nj, ob = nb // tn, off // tn
    nk = NCHIP * nj
    assert nb % tn == 0 and off % tn == 0

    def body(dy_ref, w_ref, h_ref, g_ref, s_ref, o_ref, dg_ref, acc):
        i, k = pl.program_id(0), pl.program_id(1)

        @pl.when(k == 0)
        def _():
            acc[...] = jnp.zeros_like(acc)

        acc[...] += _dot_nt(dy_ref[...], w_ref[...])

        @pl.when(k == nk - 1)
        def _():
            dg = jnp.zeros((1, K), F32)
            for r0 in range(0, tm, sub):
                d = acc[r0:r0 + sub, :]
                x = h_ref[r0:r0 + sub, :]
                r = lax.rsqrt(jnp.mean(x * x, axis=-1, keepdims=True) + EPS)
                xh = x * r
                t = d * g_ref[...]
                o_ref[r0:r0 + sub, :] = s_ref[r0:r0 + sub, :] + r * (t - xh * jnp.mean(t * xh, axis=-1, keepdims=True))
                dg = dg + jnp.sum(d * xh, axis=0, keepdims=True)
            _acc_rows(dg_ref, i, dg)

    row1 = pl.BlockSpec((tm, K), lambda i, k: (i, 0), pipeline_mode=pl.Buffered(1))
    row = pl.BlockSpec((tm, K), lambda i, k: (i, 0))
    vec = pl.BlockSpec((1, K), lambda i, k: (0, 0))
    return pl.pallas_call(
        body, name=name, grid=(T // tm, nk),
        in_specs=[pl.BlockSpec((tm, tn), lambda i, k: (i, k)),
                  pl.BlockSpec((None, K, tn), lambda i, k: (k // nj, 0, ob + k % nj)), row1, vec, row1],
        out_specs=[row, vec],
        out_shape=[jax.ShapeDtypeStruct((T, K), F32), jax.ShapeDtypeStruct((1, K), F32)],
        scratch_shapes=[pltpu.VMEM((tm, K), F32)],
        compiler_params=_cp(("arbitrary", "arbitrary")))(dy, G, h, g, skip)


def _rows(tm, C):
    return pl.BlockSpec((tm, C), lambda i: (i, 0))


def _vec(C):
    return pl.BlockSpec((1, C), lambda i: (0, 0))


def _acc_rows(ref, i, val):
    @pl.when(i == 0)
    def _():
        ref[...] = val

    @pl.when(i > 0)
    def _():
        ref[...] += val


def rms_fwd(h, g, out_dtype, name):
    T, C = h.shape
    tm = min(512, T)

    def body(h_ref, g_ref, o_ref):
        x = h_ref[...]
        r = lax.rsqrt(jnp.mean(x * x, axis=-1, keepdims=True) + EPS)
        o_ref[...] = (x * r * g_ref[...]).astype(o_ref.dtype)

    return pl.pallas_call(
        body, name=name, grid=(T // tm,), in_specs=[_rows(tm, C), _vec(C)], out_specs=_rows(tm, C),
        out_shape=jax.ShapeDtypeStruct((T, C), out_dtype), compiler_params=_cp(("parallel",)))(h, g)


def rms_bwd(dxn, h, g, dh_skip, name):
    T, C = h.shape
    tm = min(512, T)

    def body(d_ref, h_ref, g_ref, s_ref, o_ref, dg_ref):
        i = pl.program_id(0)
        x = h_ref[...]
        r = lax.rsqrt(jnp.mean(x * x, axis=-1, keepdims=True) + EPS)
        xh = x * r
        d = d_ref[...].astype(F32)
        t = d * g_ref[...]
        o_ref[...] = s_ref[...] + r * (t - xh * jnp.mean(t * xh, axis=-1, keepdims=True))
        _acc_rows(dg_ref, i, jnp.sum(d * xh, axis=0, keepdims=True))

    return pl.pallas_call(
        body, name=name, grid=(T // tm,),
        in_specs=[_rows(tm, C), _rows(tm, C), _vec(C), _rows(tm, C)],
        out_specs=[_rows(tm, C), _vec(C)],
        out_shape=[jax.ShapeDtypeStruct((T, C), F32), jax.ShapeDtypeStruct((1, C), F32)],
        compiler_params=_cp(("arbitrary",)))(dxn, h, g, dh_skip)


def post_res(h, f, g, scale, name):
    T, C = h.shape
    tm = min(512, T)

    def body(h_ref, f_ref, g_ref, o_ref):
        f = f_ref[...]
        r = lax.rsqrt(jnp.mean(f * f, axis=-1, keepdims=True) + EPS)
        o_ref[...] = h_ref[...] + scale * (f * r * g_ref[...])

    return pl.pallas_call(
        body, name=name, grid=(T // tm,), in_specs=[_rows(tm, C), _rows(tm, C), _vec(C)],
        out_specs=_rows(tm, C), out_shape=jax.ShapeDtypeStruct((T, C), F32),
        compiler_params=_cp(("parallel",)))(h, f, g)


def post_res_bwd(dh, f, g, scale, out_dtype, name):
    T, C = dh.shape
    tm = min(512, T)

    def body(d_ref, f_ref, g_ref, o_ref, dg_ref):
        i = pl.program_id(0)
        f = f_ref[...]
        r = lax.rsqrt(jnp.mean(f * f, axis=-1, keepdims=True) + EPS)
        d = scale * d_ref[...]
        t = d * g_ref[...]
        o_ref[...] = (r * t - f * (r * r * r * jnp.mean(t * f, axis=-1, keepdims=True))).astype(o_ref.dtype)
        _acc_rows(dg_ref, i, jnp.sum(d * f * r, axis=0, keepdims=True))

    return pl.pallas_call(
        body, name=name, grid=(T // tm,), in_specs=[_rows(tm, C), _rows(tm, C), _vec(C)],
        out_specs=[_rows(tm, C), _vec(C)],
        out_shape=[jax.ShapeDtypeStruct((T, C), out_dtype), jax.ShapeDtypeStruct((1, C), F32)],
        compiler_params=_cp(("arbitrary",)))(dh, f, g)


PLE_W = 256


def ple_fwd(h, xn, pb, GB, goff, GC, g, name, g_next=None):
    T, C = h.shape
    tm = min(512, T)
    more = g_next is not None

    def body(h_ref, x_ref, p_ref, wg_ref, wp_ref, g_ref, *rest):
        if more:
            gn_ref, z_ref, pe_ref, o_ref, xn_ref = rest
        else:
            z_ref, pe_ref, o_ref = rest
        zg = _dot_nn(x_ref[:, :PLE_W], wg_ref[0])
        for s in range(1, NCHIP):
            zg = zg + _dot_nn(x_ref[:, s * PLE_W:(s + 1) * PLE_W], wg_ref[s])
        z_ref[...] = zg
        for s in range(NCHIP):
            pe_ref[:, s * PLE_W:(s + 1) * PLE_W] = _dot_nn(p_ref[...], wp_ref[s])
        e = pe_ref[...] * _sig(zg)
        r = lax.rsqrt(jnp.mean(e * e, axis=-1, keepdims=True) + EPS)
        hn = h_ref[...] + e * r * g_ref[...]
        o_ref[...] = hn
        if more:
            rn = lax.rsqrt(jnp.mean(hn * hn, axis=-1, keepdims=True) + EPS)
            xn_ref[...] = (hn * rn * gn_ref[...]).astype(xn_ref.dtype)

    return pl.pallas_call(
        body, name=name, grid=(T // tm,),
        in_specs=[_rows(tm, C), _rows(tm, C), _rows(tm, PLE_W),
                  pl.BlockSpec((NCHIP, PLE_W, C), lambda i: (0, goff // PLE_W, 0)),
                  pl.BlockSpec((NCHIP, PLE_W, PLE_W), lambda i: (0, 0, 0)), _vec(C)] + ([_vec(C)] if more else []),
        out_specs=[_rows(tm, C)] * (4 if more else 3),
        out_shape=[jax.ShapeDtypeStruct((T, C), F32)] * 3 + ([jax.ShapeDtypeStruct((T, C), BF16)] if more else []),
        compiler_params=_cp(("parallel",)))(*((h, xn, pb, GB, GC, g) + ((g_next,) if more else ())))


def ple_bwd(dh, zg, pe, h_in, GB, goff, g, g_pre, name):
    T, C = dh.shape
    tm = min(512, T)

    def body(d_ref, z_ref, p_ref, h_ref, wg_ref, g_ref, gp_ref, dz_ref, dp_ref, o_ref, dg_ref, dgp_ref, dxn):
        i = pl.program_id(0)
        s = _sig(z_ref[...])
        pe_ = p_ref[...]
        e = pe_ * s
        r = lax.rsqrt(jnp.mean(e * e, axis=-1, keepdims=True) + EPS)
        d = d_ref[...]
        t = d * g_ref[...]
        de = r * t - e * (r * r * r * jnp.mean(t * e, axis=-1, keepdims=True))
        dp_ref[...] = (de * s).astype(dp_ref.dtype)
        dz = (de * pe_ * s * (1.0 - s)).astype(dz_ref.dtype)
        dz_ref[...] = dz
        _acc_rows(dg_ref, i, jnp.sum(d * e * r, axis=0, keepdims=True))
        for k in range(NCHIP):
            dxn[:, k * PLE_W:(k + 1) * PLE_W] = _dot_nt(dz, wg_ref[k])
        x = h_ref[...]
        rx = lax.rsqrt(jnp.mean(x * x, axis=-1, keepdims=True) + EPS)
        xh = x * rx
        dx = dxn[...]
        tx = dx * gp_ref[...]
        o_ref[...] = d + rx * (tx - xh * jnp.mean(tx * xh, axis=-1, keepdims=True))
        _acc_rows(dgp_ref, i, jnp.sum(dx * xh, axis=0, keepdims=True))

    return pl.pallas_call(
        body, name=name, grid=(T // tm,),
        in_specs=[_rows(tm, C), _rows(tm, C), _rows(tm, C), _rows(tm, C),
                  pl.BlockSpec((NCHIP, PLE_W, C), lambda i: (0, goff // PLE_W, 0)), _vec(C), _vec(C)],
        out_specs=[_rows(tm, C), _rows(tm, C), _rows(tm, C), _vec(C), _vec(C)],
        out_shape=[jax.ShapeDtypeStruct((T, C), BF16), jax.ShapeDtypeStruct((T, C), BF16),
                   jax.ShapeDtypeStruct((T, C), F32), jax.ShapeDtypeStruct((1, C), F32),
                   jax.ShapeDtypeStruct((1, C), F32)],
        scratch_shapes=[pltpu.VMEM((tm, C), F32)],
        compiler_params=_cp(("arbitrary",)))(dh, zg, pe, h_in, GB, g, g_pre)


def loss_head(h, tgt, name):
    T, C = h.shape
    tm = min(512, T)

    def body(h_ref, t_ref, d_ref, l_ref):
        i = pl.program_id(0)
        e = h_ref[...] - t_ref[...]
        d_ref[...] = e * (1.0 / C)
        _acc_rows(l_ref, i, jnp.sum(e * e, axis=0, keepdims=True))

    return pl.pallas_call(
        body, name=name, grid=(T // tm,), in_specs=[_rows(tm, C), _rows(tm, C)],
        out_specs=[_rows(tm, C), _vec(C)],
        out_shape=[jax.ShapeDtypeStruct((T, C), F32), jax.ShapeDtypeStruct((1, C), F32)],
        compiler_params=_cp(("arbitrary",)))(h, tgt)


AH, AG_N, AGW, CHUNK = 3072, 12, 256, 128


def _tril_bf16(w):
    r = lax.broadcasted_iota(jnp.int32, (CHUNK, CHUNK), 0)
    c = lax.broadcasted_iota(jnp.int32, (CHUNK, CHUNK), 1)
    return jnp.where(r >= c, w, 0.0).astype(BF16)


def _ln_stats(vs_ref, width):
    v = vs_ref[...]
    mu = jnp.sum(v, axis=-1, keepdims=True) * (1.0 / width)
    vc = v - mu
    var = jnp.sum(vc * vc, axis=-1, keepdims=True) * (1.0 / width)
    return mu, lax.rsqrt(var + EPS)


def gmlp_mid_fwd(zpre, vg, vb, ws, bsf, name):
    T = zpre.shape[0]

    def body(z_ref, vg_ref, vb_ref, ws_ref, bs_ref, y_ref, vs_ref):
        for g in range(AG_N):
            vs_ref[:, g * AGW:(g + 1) * AGW] = _gelu(z_ref[:, AH + g * AGW:AH + (g + 1) * AGW].astype(F32))
        mu, rstd = _ln_stats(vs_ref, AH)
        for g in range(AG_N):
            sl = slice(g * AGW, (g + 1) * AGW)
            vn = ((vs_ref[:, sl] - mu) * rstd * vg_ref[:, sl] + vb_ref[:, sl]).astype(BF16)
            sv = _dot_nn(_tril_bf16(ws_ref[g]), vn) + bs_ref[g]
            u = _gelu(z_ref[:, sl].astype(F32))
            y_ref[:, sl] = (u * sv).astype(y_ref.dtype)

    return pl.pallas_call(
        body, name=name, grid=(T // CHUNK,),
        in_specs=[_rows(CHUNK, 2 * AH), _vec(AH), _vec(AH),
                  pl.BlockSpec((AG_N, CHUNK, CHUNK), lambda i: (0, 0, 0)),
                  pl.BlockSpec((AG_N, CHUNK, AGW), lambda i: (0, 0, 0))],
        out_specs=_rows(CHUNK, AH), out_shape=jax.ShapeDtypeStruct((T, AH), BF16),
        scratch_shapes=[pltpu.VMEM((CHUNK, AH), F32)],
        compiler_params=_cp(("parallel",)))(zpre, vg, vb, ws, bsf)


def gmlp_mid_bwd(zpre, dy, vg, vb, ws, bsf, name):
    T = zpre.shape[0]

    def body(z_ref, dy_ref, vg_ref, vb_ref, ws_ref, bs_ref, dz_ref, dws_ref, dbs_ref, dvg_ref, dvb_ref,
             vs_ref, dvn_ref):
        i = pl.program_id(0)

        @pl.when(i == 0)
        def _():
            dws_ref[...] = jnp.zeros_like(dws_ref)
            dbs_ref[...] = jnp.zeros_like(dbs_ref)
            dvg_ref[...] = jnp.zeros_like(dvg_ref)
            dvb_ref[...] = jnp.zeros_like(dvb_ref)

        for g in range(AG_N):
            vs_ref[:, g * AGW:(g + 1) * AGW] = _gelu(z_ref[:, AH + g * AGW:AH + (g + 1) * AGW].astype(F32))
        mu, rstd = _ln_stats(vs_ref, AH)
        r_i = lax.broadcasted_iota(jnp.int32, (CHUNK, CHUNK), 0)
        c_i = lax.broadcasted_iota(jnp.int32, (CHUNK, CHUNK), 1)
        ones8 = jnp.ones((8, AGW), F32)
        m1 = jnp.zeros((CHUNK, 1), F32)
        m2 = jnp.zeros((CHUNK, 1), F32)
        for g in range(AG_N):
            sl = slice(g * AGW, (g + 1) * AGW)
            vh = (vs_ref[:, sl] - mu) * rstd
            vn = (vh * vg_ref[:, sl] + vb_ref[:, sl]).astype(BF16)
            wm = _tril_bf16(ws_ref[g])
            sv = _dot_nn(wm, vn) + bs_ref[g]
            zu = z_ref[:, sl].astype(F32)
            u = _gelu(zu)
            dyg = dy_ref[:, sl].astype(F32)
            dz_ref[:, sl] = (dyg * sv * _gelu_grad(zu)).astype(dz_ref.dtype)
            dsv = dyg * u
            dsv_b = dsv.astype(BF16)
            dws_ref[g] += jnp.where(r_i >= c_i, _dot_nt(dsv_b, vn), 0.0)
            dbs_ref[g] += _dot_nt(ones8, dsv)
            dvn = _dot_tn(wm, dsv_b)
            dvn_ref[:, sl] = dvn
            dvh = dvn * vg_ref[:, sl]
            m1 = m1 + jnp.sum(dvh, axis=-1, keepdims=True)
            m2 = m2 + jnp.sum(dvh * vh, axis=-1, keepdims=True)
            dvg_ref[:, sl] += jnp.sum(dvn * vh, axis=0, keepdims=True)
            dvb_ref[:, sl] += jnp.sum(dvn, axis=0, keepdims=True)
        m1 = m1 * (1.0 / AH)
        m2 = m2 * (1.0 / AH)
        for g in range(AG_N):
            sl = slice(g * AGW, (g + 1) * AGW)
            vh = (vs_ref[:, sl] - mu) * rstd
            dv = rstd * (dvn_ref[:, sl] * vg_ref[:, sl] - m1 - vh * m2)
            zv = z_ref[:, AH + g * AGW:AH + (g + 1) * AGW].astype(F32)
            dz_ref[:, AH + g * AGW:AH + (g + 1) * AGW] = (dv * _gelu_grad(zv)).astype(dz_ref.dtype)

    full3 = lambda a, b, c: pl.BlockSpec((a, b, c), lambda i: (0, 0, 0))
    return pl.pallas_call(
        body, name=name, grid=(T // CHUNK,),
        in_specs=[_rows(CHUNK, 2 * AH), _rows(CHUNK, AH), _vec(AH), _vec(AH),
                  full3(AG_N, CHUNK, CHUNK), full3(AG_N, CHUNK, AGW)],
        out_specs=[_rows(CHUNK, 2 * AH), full3(AG_N, CHUNK, CHUNK), full3(AG_N, 8, CHUNK), _vec(AH), _vec(AH)],
        out_shape=[jax.ShapeDtypeStruct((T, 2 * AH), BF16), jax.ShapeDtypeStruct((AG_N, CHUNK, CHUNK), F32),
                   jax.ShapeDtypeStruct((AG_N, 8, CHUNK), F32), jax.ShapeDtypeStruct((1, AH), F32),
                   jax.ShapeDtypeStruct((1, AH), F32)],
        scratch_shapes=[pltpu.VMEM((CHUNK, AH), F32), pltpu.VMEM((CHUNK, AH), F32)],
        compiler_params=_cp(("arbitrary",)))(zpre, dy, vg, vb, ws, bsf)


SLAB = 256
NSLAB = DM // SLAB
RC = 256
PAD = 32


def _col(T, j):
    return pl.BlockSpec((T, SLAB), lambda c: (0, j * NSLAB + c))


def _chunks(T, fn):
    def step(i, carry):
        fn(pl.multiple_of(i * RC, RC))
        return carry
    lax.fori_loop(0, T // RC, step, 0)


def _conv_taps(K):
    return [(r, [q for q in range(4) if 8 * q + r < K]) for r in range(min(8, K))]


def _causal_conv(zpad_ref, wrow, K, r0):
    acc = None
    for r, qs in _conv_taps(K):
        a = None
        for q in qs:
            term = wrow(8 * q + r) * zpad_ref[pl.ds(r0 + (PAD - 8 - 8 * q), RC + 8), :]
            a = term if a is None else a + term
        a = a if r == 0 else pltpu.roll(a, r, 0)
        acc = a if acc is None else acc + a
    return acc[8:, :]


def _anticausal_conv(gpad_ref, wrow, K, r0):
    acc = None
    for r, qs in _conv_taps(K):
        b = None
        for q in qs:
            term = wrow(8 * q + r) * gpad_ref[pl.ds(r0 + 8 * q, RC + 8), :]
            b = term if b is None else b + term
        b = b if r == 0 else pltpu.roll(b, RC + 8 - r, 0)
        acc = b if acc is None else acc + b
    return acc[:RC, :]


def _conv_dw(gpad_ref, zpad_ref, dw_ref, K, r0):
    for r, qs in _conv_taps(K):
        gw = gpad_ref[pl.ds(r0, RC + 8), :]
        p = (gw if r == 0 else pltpu.roll(gw, RC + 8 - r, 0))[:RC, :]
        for q in qs:
            z = zpad_ref[pl.ds(r0 + (PAD - 8 * q), RC), :]
            dw_ref[8 * q + r] += jnp.sum((p * z).reshape(RC // 8, 8, SLAB), axis=0)


def _zero_rows(ref, start, n):
    ref[pl.ds(start, n), :] = jnp.zeros((n, SLAB), F32)


def pool_fwd(hn, wg, sc, name):
    T = hn.shape[0]

    def body(h_ref, w_ref, s_ref, p_ref, yp_ref, y_ref, xpad):
        g = pl.program_id(0)
        wf = jnp.left_shift(2, g).astype(F32)
        _zero_rows(xpad, 0, PAD)

        def fill(r0):
            xpad[pl.ds(r0 + PAD, RC), :] = h_ref[pl.ds(r0, RC), :]
        _chunks(T, fill)

        def step(r0):
            w = xpad[pl.ds(r0 + (PAD - 16), RC + 16), :]
            s2 = w + pltpu.roll(w, 1, 0)
            s4 = s2 + pltpu.roll(s2, 2, 0)
            s8 = s4 + pltpu.roll(s4, 4, 0)
            s16 = s8 + pltpu.roll(s8, 8, 0)
            sel = jnp.where(g == 0, s2, jnp.where(g == 1, s4, jnp.where(g == 2, s8, s16)))[16:, :]
            t1 = (r0 + 1 + lax.broadcasted_iota(jnp.int32, (RC, SLAB), 0)).astype(F32)
            pooled = (sel / jnp.minimum(t1, wf) - w[16:, :]).astype(BF16)
            p_ref[pl.ds(r0, RC), :] = pooled
            yp = _dot_nn(pooled, w_ref[...])
            yp_ref[pl.ds(r0, RC), :] = yp
            y_ref[pl.ds(r0, RC), :] = yp * s_ref[...]
        _chunks(T, step)

    slab = pl.BlockSpec((T, SLAB), lambda c: (0, c))
    return pl.pallas_call(
        body, name=name, grid=(NSLAB,),
        in_specs=[slab, pl.BlockSpec((None, SLAB, SLAB), lambda c: (c, 0, 0)), pl.BlockSpec((1, SLAB), lambda c: (0, c))],
        out_specs=[slab, slab, slab],
        out_shape=[jax.ShapeDtypeStruct((T, DM), BF16), jax.ShapeDtypeStruct((T, DM), F32),
                   jax.ShapeDtypeStruct((T, DM), F32)],
        scratch_shapes=[pltpu.VMEM((T + PAD, SLAB), F32)],
        compiler_params=_cp(("parallel",)))(hn, wg, sc)


def pool_bwd(dy, ypre, pooled, wg, sc, name):
    T = dy.shape[0]

    def body(d_ref, yp_ref, p_ref, w_ref, s_ref, dh_ref, dw_ref, ds_ref, qpad, dwacc, dsacc):
        g = pl.program_id(0)
        wf = jnp.left_shift(2, g).astype(F32)
        dwacc[...] = jnp.zeros_like(dwacc)
        dsacc[...] = jnp.zeros_like(dsacc)
        _zero_rows(qpad, T, PAD)

        def first(r0):
            d = d_ref[pl.ds(r0, RC), :]
            dsacc[...] += jnp.sum((d * yp_ref[pl.ds(r0, RC), :]).reshape(RC // 8, 8, SLAB), axis=0)
            dyp = (d * s_ref[...]).astype(BF16)
            dpool = _dot_nt(dyp, w_ref[...])
            dwacc[...] += _dot_tn(p_ref[pl.ds(r0, RC), :], dyp)
            t1 = (r0 + 1 + lax.broadcasted_iota(jnp.int32, (RC, SLAB), 0)).astype(F32)
            qpad[pl.ds(r0, RC), :] = dpool / jnp.minimum(t1, wf)
            dh_ref[pl.ds(r0, RC), :] = dpool
        _chunks(T, first)

        def second(r0):
            w = qpad[pl.ds(r0, RC + 16), :]
            n = RC + 16
            a2 = w + pltpu.roll(w, n - 1, 0)
            a4 = a2 + pltpu.roll(a2, n - 2, 0)
            a8 = a4 + pltpu.roll(a4, n - 4, 0)
            a16 = a8 + pltpu.roll(a8, n - 8, 0)
            sel = jnp.where(g == 0, a2, jnp.where(g == 1, a4, jnp.where(g == 2, a8, a16)))[:RC, :]
            dh_ref[pl.ds(r0, RC), :] = sel - dh_ref[pl.ds(r0, RC), :]
        _chunks(T, second)
        dw_ref[...] = dwacc[...]
        ds_ref[...] = jnp.sum(dsacc[...], axis=0, keepdims=True)

    slab = pl.BlockSpec((T, SLAB), lambda c: (0, c))
    wspec = pl.BlockSpec((None, SLAB, SLAB), lambda c: (c, 0, 0))
    vec = pl.BlockSpec((1, SLAB), lambda c: (0, c))
    return pl.pallas_call(
        body, name=name, grid=(NSLAB,),
        in_specs=[slab, slab, slab, wspec, vec],
        out_specs=[slab, wspec, vec],
        out_shape=[jax.ShapeDtypeStruct((T, DM), F32), jax.ShapeDtypeStruct((NSLAB, SLAB, SLAB), F32),
                   jax.ShapeDtypeStruct((1, DM), F32)],
        scratch_shapes=[pltpu.VMEM((T + PAD, SLAB), F32), pltpu.VMEM((SLAB, SLAB), F32), pltpu.VMEM((8, SLAB), F32)],
        compiler_params=_cp(("parallel",)))(dy, ypre, pooled, wg, sc)


KC = 31
KD = 3


def conf_conv_fwd(ag, wdw, bdw, name):
    T = ag.shape[0]

    def body(a_ref, g_ref, w_ref, b_ref, o_ref, zpad):
        _zero_rows(zpad, 0, PAD)

        def fill(r0):
            a = a_ref[pl.ds(r0, RC), :].astype(F32)
            gt = g_ref[pl.ds(r0, RC), :].astype(F32)
            zpad[pl.ds(r0 + PAD, RC), :] = a * _sig(gt)
        _chunks(T, fill)
        wrow = lambda j: w_ref[KC - 1 - j:KC - j, :]

        def step(r0):
            o_ref[pl.ds(r0, RC), :] = _causal_conv(zpad, wrow, KC, r0) + b_ref[...]
        _chunks(T, step)

    vec = pl.BlockSpec((1, SLAB), lambda c: (0, c))
    return pl.pallas_call(
        body, name=name, grid=(NSLAB,),
        in_specs=[_col(T, 0), _col(T, 1), pl.BlockSpec((32, SLAB), lambda c: (0, c)), vec],
        out_specs=pl.BlockSpec((T, SLAB), lambda c: (0, c)),
        out_shape=jax.ShapeDtypeStruct((T, DM), F32),
        scratch_shapes=[pltpu.VMEM((T + PAD, SLAB), F32)],
        compiler_params=_cp(("parallel",)))(ag, ag, wdw, bdw)


def conf_conv_bwd(dzc, ag, wdw, name):
    T = ag.shape[0]

    def body(d_ref, a_ref, g_ref, w_ref, da_ref, dg_ref, dw_ref, db_ref, zpad, gpad, dwacc, dbacc):
        _zero_rows(zpad, 0, PAD)
        _zero_rows(gpad, T, PAD)
        dwacc[...] = jnp.zeros_like(dwacc)
        dbacc[...] = jnp.zeros_like(dbacc)

        def fill(r0):
            a = a_ref[pl.ds(r0, RC), :].astype(F32)
            gt = g_ref[pl.ds(r0, RC), :].astype(F32)
            zpad[pl.ds(r0 + PAD, RC), :] = a * _sig(gt)
            d = d_ref[pl.ds(r0, RC), :]
            gpad[pl.ds(r0, RC), :] = d
            dbacc[...] += jnp.sum(d.reshape(RC // 8, 8, SLAB), axis=0)
        _chunks(T, fill)
        wrow = lambda j: w_ref[KC - 1 - j:KC - j, :]

        def step(r0):
            dz = _anticausal_conv(gpad, wrow, KC, r0)
            a = a_ref[pl.ds(r0, RC), :].astype(F32)
            s = _sig(g_ref[pl.ds(r0, RC), :].astype(F32))
            da_ref[pl.ds(r0, RC), :] = (dz * s).astype(da_ref.dtype)
            dg_ref[pl.ds(r0, RC), :] = (dz * a * s * (1.0 - s)).astype(dg_ref.dtype)
            _conv_dw(gpad, zpad, dwacc, KC, r0)
        _chunks(T, step)
        dw_ref[...] = jnp.zeros_like(dw_ref)
        for k in range(KC):
            dw_ref[k:k + 1, :] = jnp.sum(dwacc[KC - 1 - k], axis=0, keepdims=True)
        db_ref[...] = jnp.sum(dbacc[...], axis=0, keepdims=True)

    vec = pl.BlockSpec((1, SLAB), lambda c: (0, c))
    w32 = pl.BlockSpec((32, SLAB), lambda c: (0, c))
    return pl.pallas_call(
        body, name=name, grid=(NSLAB,),
        in_specs=[pl.BlockSpec((T, SLAB), lambda c: (0, c)), _col(T, 0), _col(T, 1), w32],
        out_specs=[_col(T, 0), _col(T, 0), w32, vec],
        out_shape=[jax.ShapeDtypeStruct((T, DM), BF16), jax.ShapeDtypeStruct((T, DM), BF16),
                   jax.ShapeDtypeStruct((32, DM), F32), jax.ShapeDtypeStruct((1, DM), F32)],
        scratch_shapes=[pltpu.VMEM((T + PAD, SLAB), F32), pltpu.VMEM((T + PAD, SLAB), F32),
                        pltpu.VMEM((32, 8, SLAB), F32), pltpu.VMEM((8, SLAB), F32)],
        compiler_params=_cp(("parallel",)))(dzc, ag, ag, wdw)


def conf_ln_fwd(zc, g, b, name):
    T, C = zc.shape
    tm = min(512, T)

    def body(z_ref, g_ref, b_ref, o_ref):
        x = z_ref[...]
        xc = x - jnp.mean(x, axis=-1, keepdims=True)
        r = lax.rsqrt(jnp.mean(xc * xc, axis=-1, keepdims=True) + EPS)
        zl = xc * r * g_ref[...] + b_ref[...]
        o_ref[...] = (zl * _sig(zl)).astype(o_ref.dtype)

    return pl.pallas_call(
        body, name=name, grid=(T // tm,), in_specs=[_rows(tm, C), _vec(C), _vec(C)], out_specs=_rows(tm, C),
        out_shape=jax.ShapeDtypeStruct((T, C), BF16), compiler_params=_cp(("parallel",)))(zc, g, b)


def conf_ln_bwd(dzs, zc, g, b, name):
    T, C = zc.shape
    tm = min(512, T)

    def body(d_ref, z_ref, g_ref, b_ref, o_ref, dg_ref, db_ref):
        i = pl.program_id(0)
        x = z_ref[...]
        xc = x - jnp.mean(x, axis=-1, keepdims=True)
        r = lax.rsqrt(jnp.mean(xc * xc, axis=-1, keepdims=True) + EPS)
        xh = xc * r
        zl = xh * g_ref[...] + b_ref[...]
        s = _sig(zl)
        dzl = d_ref[...].astype(F32) * (s * (1.0 + zl * (1.0 - s)))
        t = dzl * g_ref[...]
        o_ref[...] = r * (t - jnp.mean(t, axis=-1, keepdims=True) - xh * jnp.mean(t * xh, axis=-1, keepdims=True))
        _acc_rows(dg_ref, i, jnp.sum(dzl * xh, axis=0, keepdims=True))
        _acc_rows(db_ref, i, jnp.sum(dzl, axis=0, keepdims=True))

    return pl.pallas_call(
        body, name=name, grid=(T // tm,), in_specs=[_rows(tm, C), _rows(tm, C), _vec(C), _vec(C)],
        out_specs=[_rows(tm, C), _vec(C), _vec(C)],
        out_shape=[jax.ShapeDtypeStruct((T, C), F32), jax.ShapeDtypeStruct((1, C), F32),
                   jax.ShapeDtypeStruct((1, C), F32)],
        compiler_params=_cp(("arbitrary",)))(dzs, zc, g, b)


def sconv_fwd(bgx, wc, name):
    T = bgx.shape[0]

    def body(b_ref, c_ref, x_ref, w_ref, o_ref, zpad):
        _zero_rows(zpad, 0, PAD)

        def fill(r0):
            zpad[pl.ds(r0 + PAD, RC), :] = c_ref[pl.ds(r0, RC), :].astype(F32) * x_ref[pl.ds(r0, RC), :].astype(F32)
        _chunks(T, fill)
        wrow = lambda j: w_ref[KD - 1 - j:KD - j, :]

        def step(r0):
            qc = _causal_conv(zpad, wrow, KD, r0)
            o_ref[pl.ds(r0, RC), :] = (b_ref[pl.ds(r0, RC), :].astype(F32) * qc).astype(o_ref.dtype)
        _chunks(T, step)

    return pl.pallas_call(
        body, name=name, grid=(NSLAB,),
        in_specs=[_col(T, 0), _col(T, 1), _col(T, 2), pl.BlockSpec((8, SLAB), lambda c: (0, c))],
        out_specs=pl.BlockSpec((T, SLAB), lambda c: (0, c)),
        out_shape=jax.ShapeDtypeStruct((T, DM), BF16),
        scratch_shapes=[pltpu.VMEM((T + PAD, SLAB), F32)],
        compiler_params=_cp(("parallel",)))(bgx, bgx, bgx, wc)


def sconv_bwd(dy, bgx, wc, name):
    T = bgx.shape[0]

    def body(d_ref, b_ref, c_ref, x_ref, w_ref, db_ref, dc_ref, dx_ref, dw_ref, zpad, gpad, dwacc):
        _zero_rows(zpad, 0, PAD)
        _zero_rows(gpad, T, PAD)
        dwacc[...] = jnp.zeros_like(dwacc)

        def fill(r0):
            zpad[pl.ds(r0 + PAD, RC), :] = c_ref[pl.ds(r0, RC), :].astype(F32) * x_ref[pl.ds(r0, RC), :].astype(F32)
            gpad[pl.ds(r0, RC), :] = d_ref[pl.ds(r0, RC), :].astype(F32) * b_ref[pl.ds(r0, RC), :].astype(F32)
        _chunks(T, fill)
        wrow = lambda j: w_ref[KD - 1 - j:KD - j, :]

        def step(r0):
            qc = _causal_conv(zpad, wrow, KD, r0)
            db_ref[pl.ds(r0, RC), :] = (d_ref[pl.ds(r0, RC), :].astype(F32) * qc).astype(db_ref.dtype)
            dq = _anticausal_conv(gpad, wrow, KD, r0)
            dc_ref[pl.ds(r0, RC), :] = (dq * x_ref[pl.ds(r0, RC), :].astype(F32)).astype(dc_ref.dtype)
            dx_ref[pl.ds(r0, RC), :] = (dq * c_ref[pl.ds(r0, RC), :].astype(F32)).astype(dx_ref.dtype)
            _conv_dw(gpad, zpad, dwacc, KD, r0)
        _chunks(T, step)
        dw_ref[...] = jnp.zeros_like(dw_ref)
        for k in range(KD):
            dw_ref[k:k + 1, :] = jnp.sum(dwacc[KD - 1 - k], axis=0, keepdims=True)

    w8 = pl.BlockSpec((8, SLAB), lambda c: (0, c))
    return pl.pallas_call(
        body, name=name, grid=(NSLAB,),
        in_specs=[pl.BlockSpec((T, SLAB), lambda c: (0, c)), _col(T, 0), _col(T, 1), _col(T, 2), w8],
        out_specs=[_col(T, 0), _col(T, 0), _col(T, 0), w8],
        out_shape=[jax.ShapeDtypeStruct((T, DM), BF16)] * 3 + [jax.ShapeDtypeStruct((8, DM), F32)],
        scratch_shapes=[pltpu.VMEM((T + PAD, SLAB), F32), pltpu.VMEM((T + PAD, SLAB), F32),
                        pltpu.VMEM((8, 8, SLAB), F32)],
        compiler_params=_cp(("parallel",)))(dy, bgx, bgx, bgx, wc)


def merge_cols(parts, name):
    T = parts[0].shape[0]
    n = len(parts)
    C = n * DM
    tm = min(512, T)

    def body(*refs):
        o_ref = refs[n]
        for j in range(n):
            o_ref[:, j * DM:(j + 1) * DM] = refs[j][...]

    return pl.pallas_call(
        body, name=name, grid=(T // tm,),
        in_specs=[_rows(tm, DM) for j in range(n)],
        out_specs=_rows(tm, C), out_shape=jax.ShapeDtypeStruct((T, C), parts[0].dtype),
        compiler_params=_cp(("parallel",)))(*parts)


def adamw(w, g, m, v, name):
    shape = w.shape
    R, C = shape[-2], shape[-1]
    L = w.size // (R * C)
    w2, g2, m2, v2 = (a.reshape(L, R, C) for a in (w, g, m, v))
    tr = R
    while tr * C > 512 * 1024 and tr % 16 == 0:
        tr //= 2
    bc1 = 1.0 - ADAM_B1 ** ADAM_STEP
    bc2 = 1.0 - ADAM_B2 ** ADAM_STEP

    def body(w_ref, g_ref, m_ref, v_ref, d_ref, nm_ref, nv_ref):
        gg = g_ref[...]
        nm = ADAM_B1 * m_ref[...] + (1.0 - ADAM_B1) * gg
        nv = ADAM_B2 * v_ref[...] + (1.0 - ADAM_B2) * (gg * gg)
        nm_ref[...] = nm
        nv_ref[...] = nv
        d_ref[...] = -ADAM_LR * ((nm / bc1) / (jnp.sqrt(nv / bc2) + ADAM_EPS) + ADAM_WD * w_ref[...])

    spec = pl.BlockSpec((None, tr, C), lambda l, i: (l, i, 0))
    outs = pl.pallas_call(
        body, name=name, grid=(L, R // tr), in_specs=[spec] * 4, out_specs=[spec] * 3,
        out_shape=[jax.ShapeDtypeStruct((L, R, C), F32)] * 3,
        compiler_params=_cp(("parallel", "parallel")))(w2, g2, m2, v2)
    return tuple(o.reshape(shape) for o in outs)


def _place():
    x, y, c = lax.axis_index("x"), lax.axis_index("y"), lax.axis_index("c")
    return x, y, c


def all_gather_chips(bufs, name):
    n = len(bufs)
    me_ = 2 * lax.axis_index("x") + lax.axis_index("y")
    slots = [lax.dynamic_update_slice(lax.empty((NCHIP,) + b.shape, b.dtype), b[None], (me_, 0, 0)) for b in bufs]

    def body(*refs):
        dst = refs[n:2 * n]
        send, recv = refs[2 * n:]
        x, y, c = _place()
        me = 2 * x + y
        sib = (x, y, 1 - c)
        chips = [(1 - x, y), (x, 1 - y), (1 - x, 1 - y)]

        def half(b, slot, hc):
            rows = bufs[b].shape[0] // 2
            return dst[b].at[slot, pl.ds(hc * rows, rows), :]

        def remote(k, s, d, to):
            return pltpu.make_async_remote_copy(src_ref=s, dst_ref=d, send_sem=send.at[k], recv_sem=recv.at[k],
                                                device_id=to, device_id_type=MESH)

        first = []
        for b in range(n):
            for j, (cx, cy) in enumerate(chips):
                first.append(remote(b * 6 + j, half(b, me, c), half(b, me, c), (cx, cy, c)))
        for cp in first:
            cp.start()
        passed = []
        for b in range(n):
            for j, (cx, cy) in enumerate(chips):
                slot = 2 * cx + cy
                remote(b * 6 + j, half(b, slot, c), half(b, slot, c), (cx, cy, c)).wait_recv()
                fwd = remote(b * 6 + 3 + j, half(b, slot, c), half(b, slot, c), sib)
                fwd.start()
                passed.append(fwd)
        for b in range(n):
            for j, (cx, cy) in enumerate(chips):
                slot = 2 * cx + cy
                remote(b * 6 + 3 + j, half(b, slot, 1 - c), half(b, slot, 1 - c), sib).wait_recv()
        for cp in first + passed:
            cp.wait_send()

    return pl.pallas_call(
        body, name=name, in_specs=[ANY] * n, out_specs=[ANY] * n,
        out_shape=[jax.ShapeDtypeStruct(s.shape, s.dtype) for s in slots],
        input_output_aliases={b: b for b in range(n)},
        scratch_shapes=[pltpu.SemaphoreType.DMA((6 * n,)), pltpu.SemaphoreType.DMA((6 * n,))],
        compiler_params=pltpu.CompilerParams())(*slots)


def pair_exchange(bufs, name):
    n = len(bufs)

    def body(*refs):
        src, dst = refs[:n], refs[n:2 * n]
        send, recv = refs[2 * n:]
        x, y, c = _place()
        cps = []
        for b in range(n):
            rows = bufs[b].shape[1] // 2
            cp = pltpu.make_async_remote_copy(
                src_ref=src[b].at[:, pl.ds((1 - c) * rows, rows), :], dst_ref=dst[b],
                send_sem=send.at[b], recv_sem=recv.at[b], device_id=(x, y, 1 - c), device_id_type=MESH)
            cp.start()
            cps.append(cp)
        for cp in cps:
            cp.wait()

    return pl.pallas_call(
        body, name=name, in_specs=[ANY] * n, out_specs=[ANY] * n,
        out_shape=[jax.ShapeDtypeStruct((NCHIP, b.shape[1] // 2, b.shape[2]), b.dtype) for b in bufs],
        scratch_shapes=[pltpu.SemaphoreType.DMA((n,)), pltpu.SemaphoreType.DMA((n,))],
        compiler_params=pltpu.CompilerParams())(*bufs)


def add_half(full, got, tr, tc, name):
    _, R, C = full.shape
    rows = R // 2
    nr = rows // tr
    c_arr = lax.axis_index("c").astype(jnp.int32).reshape(1)

    def body(c_ref, a_ref, b_ref, o_ref):
        o_ref[...] = (a_ref[...].astype(F32) + b_ref[...].astype(F32)).astype(o_ref.dtype)

    return pl.pallas_call(
        body, name=name,
        grid_spec=pltpu.PrefetchScalarGridSpec(
            num_scalar_prefetch=1, grid=(NCHIP, nr, C // tc),
            in_specs=[pl.BlockSpec((None, tr, tc), lambda s, i, j, c_ref: (s, c_ref[0] * nr + i, j)),
                      pl.BlockSpec((None, tr, tc), lambda s, i, j, c_ref: (s, i, j))],
            out_specs=pl.BlockSpec((None, tr, tc), lambda s, i, j, c_ref: (s, i, j))),
        out_shape=jax.ShapeDtypeStruct((NCHIP, rows, C), full.dtype),
        compiler_params=_cp(("parallel", "parallel", "parallel")))(c_arr, full, got)


def chip_exchange(bufs, name):
    n = len(bufs)

    def body(*refs):
        src, dst = refs[:n], refs[n:2 * n]
        send, recv, lsem = refs[2 * n:]
        x, y, c = _place()
        me = 2 * x + y
        chips = [(1 - x, y), (x, 1 - y), (1 - x, 1 - y)]
        local = [pltpu.make_async_copy(src[b].at[me], dst[b].at[me], lsem.at[b]) for b in range(n)]
        for cp in local:
            cp.start()
        cps = []
        for b in range(n):
            for j, (cx, cy) in enumerate(chips):
                cp = pltpu.make_async_remote_copy(
                    src_ref=src[b].at[2 * cx + cy], dst_ref=dst[b].at[me],
                    send_sem=send.at[b * 3 + j], recv_sem=recv.at[b * 3 + j],
                    device_id=(cx, cy, c), device_id_type=MESH)
                cp.start()
                cps.append((cp, b, cx, cy, j))
        for cp, b, cx, cy, j in cps:
            cp.wait_send()
            pltpu.make_async_remote_copy(
                src_ref=src[b].at[me], dst_ref=dst[b].at[2 * cx + cy],
                send_sem=send.at[b * 3 + j], recv_sem=recv.at[b * 3 + j],
                device_id=(cx, cy, c), device_id_type=MESH).wait_recv()
        for cp in local:
            cp.wait()

    return pl.pallas_call(
        body, name=name, in_specs=[ANY] * n, out_specs=[ANY] * n,
        out_shape=[jax.ShapeDtypeStruct(b.shape, b.dtype) for b in bufs],
        scratch_shapes=[pltpu.SemaphoreType.DMA((3 * n,)), pltpu.SemaphoreType.DMA((3 * n,)),
                        pltpu.SemaphoreType.DMA((n,))],
        compiler_params=pltpu.CompilerParams())(*bufs)


def sum_slots(buf, tr, tc, name):
    _, r, C = buf.shape
    nr = r // tr
    c_arr = lax.axis_index("c").astype(jnp.int32).reshape(1)

    def body(c_ref, a_ref, o_ref):
        o_ref[...] = ((a_ref[0].astype(F32) + a_ref[1].astype(F32)) + a_ref[2].astype(F32)) + a_ref[3].astype(F32)

    return pl.pallas_call(
        body, name=name,
        grid_spec=pltpu.PrefetchScalarGridSpec(
            num_scalar_prefetch=1, grid=(nr, C // tc),
            in_specs=[pl.BlockSpec((NCHIP, tr, tc), lambda i, j, c_ref: (0, i, j))],
            out_specs=pl.BlockSpec((tr, tc), lambda i, j, c_ref: (c_ref[0] * nr + i, j))),
        out_shape=jax.ShapeDtypeStruct((2 * r, C), F32),
        compiler_params=_cp(("parallel", "parallel")))(c_arr, buf)


def pair_share(bufs, name):
    n = len(bufs)

    def body(*refs):
        dst = refs[n:2 * n]
        send, recv = refs[2 * n:]
        x, y, c = _place()
        cps = []
        for b in range(n):
            rows = bufs[b].shape[0] // 2
            here = dst[b].at[pl.ds(c * rows, rows), :]
            cp = pltpu.make_async_remote_copy(src_ref=here, dst_ref=here, send_sem=send.at[b], recv_sem=recv.at[b],
                                              device_id=(x, y, 1 - c), device_id_type=MESH)
            cp.start()
            cps.append((cp, b))
        for cp, b in cps:
            rows = bufs[b].shape[0] // 2
            there = dst[b].at[pl.ds((1 - c) * rows, rows), :]
            cp.wait_send()
            pltpu.make_async_remote_copy(src_ref=there, dst_ref=there, send_sem=send.at[b], recv_sem=recv.at[b],
                                         device_id=(x, y, 1 - c), device_id_type=MESH).wait_recv()

    return pl.pallas_call(
        body, name=name, in_specs=[ANY] * n, out_specs=[ANY] * n,
        out_shape=[jax.ShapeDtypeStruct(b.shape, b.dtype) for b in bufs],
        input_output_aliases={b: b for b in range(n)},
        scratch_shapes=[pltpu.SemaphoreType.DMA((n,)), pltpu.SemaphoreType.DMA((n,))],
        compiler_params=pltpu.CompilerParams())(*bufs)


def _tile(kind, buf):
    return {"A": (256, buf.shape[2]), "B": (buf.shape[1], 1024), "C": (128, 256), "V": (40, 256),
            "E": (40, 1024)}[kind]


def reduce_scatter(parts, tag):
    names = list(parts)
    got = pair_exchange([parts[k] for k in names], tag + "_pair_exchange")
    sums = [add_half(parts[k], got[i], *_tile(k[0], got[i]), name=tag + "_add_pair_" + k) for i, k in enumerate(names)]
    landed = chip_exchange(sums, tag + "_chip_exchange")
    halves = [sum_slots(landed[i], *_tile(k[0], landed[i]), name=tag + "_sum_chips_" + k) for i, k in enumerate(names)]
    full = pair_share(halves, tag + "_pair_share")
    return dict(zip(names, full))


HBM = pl.BlockSpec(memory_space=pltpu.HBM)
SEMS = pl.BlockSpec(memory_space=pltpu.SEMAPHORE)
FLOWS = pltpu.SideEffectType.DATAFLOW_SIDE_EFFECTING


def _in_hbm(a):
    return pltpu.with_memory_space_constraint(a, pltpu.HBM)


def _other_chips():
    x, y, c = _place()
    return 2 * x + y, c, [(1 - x, y), (x, 1 - y), (1 - x, 1 - y)]


def own_slots(bufs):
    me = 2 * lax.axis_index("x") + lax.axis_index("y")
    return [lax.dynamic_update_slice(lax.empty((NCHIP,) + b.shape, b.dtype), b[None], (me, 0, 0)) for b in bufs]


def gather_start(slots, after, name):
    n = len(slots)

    def body(*refs):
        ins = refs[:n]
        send, recv = refs[n + 1], refs[n + 2]
        token = refs[2 * n + 3]
        me, c, chips = _other_chips()
        for b in range(n):
            rows = slots[b].shape[1] // 2
            own = ins[b].at[me, pl.ds(c * rows, rows), :]
            for j, (cx, cy) in enumerate(chips):
                pltpu.make_async_remote_copy(src_ref=own, dst_ref=own, send_sem=send.at[3 * b + j],
                                             recv_sem=recv.at[3 * b + j], device_id=(cx, cy, c),
                                             device_id_type=MESH).start()
        token[...] = jnp.zeros_like(token)

    out = pl.pallas_call(
        body, name=name, in_specs=[HBM] * n + [ANY],
        out_specs=[SEMS, SEMS] + [HBM] * n + [pl.BlockSpec(memory_space=pltpu.VMEM)],
        out_shape=[pltpu.SemaphoreType.DMA((3 * n,)), pltpu.SemaphoreType.DMA((3 * n,))]
        + [pltpu.HBM(s.shape, s.dtype) for s in slots] + [jax.ShapeDtypeStruct((8, 128), F32)],
        input_output_aliases={b: b + 2 for b in range(n)},
        compiler_params=pltpu.CompilerParams(has_side_effects=FLOWS))(*[_in_hbm(s) for s in slots], after)
    return out[0], out[1], list(out[2:2 + n]), out[2 + n]


def gather_wait(send, recv, slots, picks, after, name):
    n = len(slots)

    def body(*refs):
        ins = refs[:n]
        send_, recv_ = refs[n], refs[n + 1]
        me, c, chips = _other_chips()
        for i, b in enumerate(picks):
            rows = slots[i].shape[1] // 2
            own = ins[i].at[me, pl.ds(c * rows, rows), :]
            for j, (cx, cy) in enumerate(chips):
                got = ins[i].at[2 * cx + cy, pl.ds(c * rows, rows), :]
                pltpu.make_async_remote_copy(src_ref=own, dst_ref=own, send_sem=send_.at[3 * b + j],
                                             recv_sem=recv_.at[3 * b + j], device_id=(cx, cy, c),
                                             device_id_type=MESH).wait_send()
                pltpu.make_async_remote_copy(src_ref=got, dst_ref=got, send_sem=send_.at[3 * b + j],
                                             recv_sem=recv_.at[3 * b + j], device_id=(cx, cy, c),
                                             device_id_type=MESH).wait_recv()

    return pl.pallas_call(
        body, name=name, in_specs=[HBM] * n + [SEMS, SEMS, ANY], out_specs=[HBM] * n,
        out_shape=[pltpu.HBM(s.shape, s.dtype) for s in slots],
        input_output_aliases={b: b for b in range(n)},
        compiler_params=pltpu.CompilerParams(has_side_effects=FLOWS))(*slots, send, recv, after)


def gather_pass(slots, name):
    n = len(slots)

    def body(*refs):
        dst = refs[n:2 * n]
        send, recv = refs[2 * n:]
        x, y, c = _place()
        sib = (x, y, 1 - c)
        chips = [(1 - x, y), (x, 1 - y), (1 - x, 1 - y)]

        def half(b, slot, hc):
            rows = slots[b].shape[1] // 2
            return dst[b].at[slot, pl.ds(hc * rows, rows), :]

        passed = []
        for b in range(n):
            for j, (cx, cy) in enumerate(chips):
                slot = 2 * cx + cy
                cp = pltpu.make_async_remote_copy(src_ref=half(b, slot, c), dst_ref=half(b, slot, c),
                                                  send_sem=send.at[3 * b + j], recv_sem=recv.at[3 * b + j],
                                                  device_id=sib, device_id_type=MESH)
                cp.start()
                passed.append(cp)
        for b in range(n):
            for j, (cx, cy) in enumerate(chips):
                slot = 2 * cx + cy
                pltpu.make_async_remote_copy(src_ref=half(b, slot, 1 - c), dst_ref=half(b, slot, 1 - c),
                                             send_sem=send.at[3 * b + j], recv_sem=recv.at[3 * b + j],
                                             device_id=sib, device_id_type=MESH).wait_recv()
        for cp in passed:
            cp.wait_send()

    return pl.pallas_call(
        body, name=name, in_specs=[ANY] * n, out_specs=[ANY] * n,
        out_shape=[jax.ShapeDtypeStruct(s.shape, s.dtype) for s in slots],
        input_output_aliases={b: b for b in range(n)},
        scratch_shapes=[pltpu.SemaphoreType.DMA((3 * n,)), pltpu.SemaphoreType.DMA((3 * n,))],
        compiler_params=pltpu.CompilerParams())(*slots)


def chip_exchange_start(sums, name):
    n = len(sums)
    me_ = 2 * lax.axis_index("x") + lax.axis_index("y")
    landing = [lax.dynamic_update_slice(lax.empty(s.shape, s.dtype),
                                        lax.dynamic_slice(s, (me_, 0, 0), (1,) + s.shape[1:]), (me_, 0, 0)) for s in sums]

    def body(*refs):
        src, land = refs[:n], refs[n:2 * n]
        send, recv = refs[2 * n], refs[2 * n + 1]
        token = refs[4 * n + 2]
        me, c, chips = _other_chips()
        for b in range(n):
            for j, (cx, cy) in enumerate(chips):
                pltpu.make_async_remote_copy(src_ref=src[b].at[2 * cx + cy], dst_ref=land[b].at[me],
                                             send_sem=send.at[3 * b + j], recv_sem=recv.at[3 * b + j],
                                             device_id=(cx, cy, c), device_id_type=MESH).start()
        token[...] = jnp.zeros_like(token)

    out = pl.pallas_call(
        body, name=name, in_specs=[HBM] * (2 * n),
        out_specs=[SEMS, SEMS] + [HBM] * (2 * n) + [pl.BlockSpec(memory_space=pltpu.VMEM)],
        out_shape=[pltpu.SemaphoreType.DMA((3 * n,)), pltpu.SemaphoreType.DMA((3 * n,))]
        + [pltpu.HBM(s.shape, s.dtype) for s in sums + landing] + [jax.ShapeDtypeStruct((8, 128), F32)],
        input_output_aliases={b: b + 2 for b in range(2 * n)},
        compiler_params=pltpu.CompilerParams(has_side_effects=FLOWS))(*[_in_hbm(s) for s in sums + landing])
    return out[0], out[1], list(out[2:2 + n]), list(out[2 + n:2 + 2 * n]), out[2 + 2 * n]


def chip_exchange_wait(send, recv, sums, landing, after, name):
    n = len(sums)

    def body(*refs):
        src, land = refs[:n], refs[n:2 * n]
        send_, recv_ = refs[2 * n], refs[2 * n + 1]
        me, c, chips = _other_chips()
        for b in range(n):
            for j, (cx, cy) in enumerate(chips):
                slot = 2 * cx + cy
                pltpu.make_async_remote_copy(src_ref=src[b].at[slot], dst_ref=land[b].at[me],
                                             send_sem=send_.at[3 * b + j], recv_sem=recv_.at[3 * b + j],
                                             device_id=(cx, cy, c), device_id_type=MESH).wait_send()
                pltpu.make_async_remote_copy(src_ref=src[b].at[me], dst_ref=land[b].at[slot],
                                             send_sem=send_.at[3 * b + j], recv_sem=recv_.at[3 * b + j],
                                             device_id=(cx, cy, c), device_id_type=MESH).wait_recv()

    out = pl.pallas_call(
        body, name=name, in_specs=[HBM] * (2 * n) + [SEMS, SEMS, ANY], out_specs=[HBM] * (2 * n),
        out_shape=[pltpu.HBM(s.shape, s.dtype) for s in sums + landing],
        input_output_aliases={b: b for b in range(2 * n)},
        compiler_params=pltpu.CompilerParams(has_side_effects=FLOWS))(*sums, *landing, send, recv, after)
    return list(out[n:])


def _row(a, l):
    return a[l:l + 1]


def local_step(x, p, tgt, small, weights_of, vecs, a_ws, a_bs, grads_ready):
    T = x.shape[0]
    bsf = jnp.broadcast_to(a_bs[:, :, None], (AG_N, CHUNK, AGW))
    vrow = lambda r: vecs[r:r + 1]
    saved = []
    W = []
    h = x
    GA1 = GB1 = GA = GB = GC = bgrp = None

    def ff_fwd(h, xn, l, which, post, g_next, tok=None):
        wa, wb = (GA1, GB1) if which == 1 else (GA, GB)
        tag = "ff%d_l%d" % (which, l)
        gu, a = ff_gateup(xn, wa, 0, tag + "_gateup")
        gp = _row(post, l) if tok is None else _row(post, l) + tok
        out = mm_rs_post(a, wb, 0, FB, FB, h, gp, 0.5, tag + "_down", g_next=g_next)
        return out[1], (out[2] if g_next is not None else None), (h, xn, gu, a, out[0])

    xn = rms_fwd(h, _row(small["ff1_pre_g"], 0), BF16, "ff1_l0_pre")
    for l in range(4):
        rec = {}
        GA1, GB1, atok = weights_of(l, "a", h)
        g_mix = _row(small["mix_pre_g"], l) if l >= 2 else None
        h, hn, rec["ff1"] = ff_fwd(h, xn, l, 1, small["ff1_post_g"], g_mix, atok)
        GA, GB, GC, wtok = weights_of(l, "b", h)
        W.append((GA1, GB1, GA, GB, GC))
        if l == 1:
            bgrp = GC[:, C_BGRP:C_BGRP + 256, :].reshape(NCHIP, 4, 64, 256).transpose(1, 0, 2, 3).reshape(4, 256, 256)
        tag = "mix_l%d" % l
        h_in = h
        g_ff2 = _row(small["ff2_pre_g"], l)
        if l == 1:
            hn = rms_fwd(h, _row(small["mix_pre_g"], l), F32, tag + "_pre")
            pooled, ypre, f = pool_fwd(hn, bgrp, vrow(V_BSCALE), tag + "_pool")
            rec["mix"] = (h_in, pooled, ypre, f)
            h = post_res(h, f, _row(small["mix_post_g"], l), 1.0, tag + "_post")
            xn = rms_fwd(h, g_ff2, BF16, "ff2_l1_pre")
        else:
            gpost = _row(small["mix_post_g"], l)
            if l == 0:
                g0 = _row(small["mix_pre_g"], l)
                hn = rms_fwd(h, g0 if wtok is None else g0 + wtok, BF16, tag + "_pre")
                zpre = mm_cs(hn, GA, A_AIN, 1536, 1536, BF16, tag + "_in")
                y = gmlp_mid_fwd(zpre, small["a_v_norm_g"], small["a_v_norm_b"], a_ws, bsf, tag + "_gate")
                f, h, xn = mm_rs_post(y, GB, B_AOUT, 768, 768, h, gpost, 1.0, tag + "_out", g_next=g_ff2)
                rec["mix"] = (h_in, hn, zpre, y, f)
            elif l == 2:
                ag = mm_cs(hn, GA, A_CIN, 512, 512, BF16, tag + "_pw1")
                zc = conf_conv_fwd(ag, vecs[V_CDW:V_CDW + 32], vrow(V_CBDW), tag + "_conv")
                zs = conf_ln_fwd(zc, vrow(V_CNG), vrow(V_CNB), tag + "_ln")
                f, h, xn = mm_rs_post(zs, GB, B_CPW2, 256, 256, h, gpost, 1.0, tag + "_pw2", g_next=g_ff2)
                rec["mix"] = (h_in, hn, ag, zc, zs, f)
            else:
                bgx = mm_cs(hn, GA, A_DIN, 768, 768, BF16, tag + "_in")
                y = sconv_fwd(bgx, vecs[V_DCONV:V_DCONV + 8], tag + "_conv")
                f, h, xn = mm_rs_post(y, GB, B_DOUT, 256, 256, h, gpost, 1.0, tag + "_out", g_next=g_ff2)
                rec["mix"] = (h_in, hn, bgx, y, f)
        h, xn, rec["ff2"] = ff_fwd(h, xn, l, 2, small["ff2_post_g"], _row(small["ple_gate_norm_g"], l))
        tag = "ple_l%d" % l
        pb = p[l].astype(BF16)
        h_in = h
        g_next = _row(small["ff1_pre_g"], l + 1) if l < 3 else None
        out = ple_fwd(h, xn, pb, GB, B_PLEG(l), GC, _row(small["ple_post_g"], l), tag, g_next=g_next)
        rec["ple"] = (h_in, xn, out[0], out[1], pb)
        h = out[2]
        xn = out[3] if l < 3 else None
        saved.append(rec)

    dh, loss_cols = loss_head(h, tgt, "loss_head")

    dA2 = dB2 = None
    layer_grads = [None] * 4
    tok = None
    gV = {}
    gains = {k: [None] * 4 for k in ("ff1_pre_g", "ff1_post_g", "mix_pre_g", "mix_post_g", "ff2_pre_g", "ff2_post_g",
                                      "ple_gate_norm_g", "ple_post_g")}
    extra = {}

    def ff_bwd(dh, l, which, pre, post, rec, after=None):
        wa, wb = (GA1, GB1) if which == 1 else (GA, GB)
        tag = "ff%d_l%d_b" % (which, l)
        h_in, xn, gu, a, f = rec
        gp = _row(post, l) if after is None else _row(post, l) + after
        df, dpost, dgu = ff_bwd_down(dh, f, gp, wb, 0, gu, tag + "_down")
        if which == 1:
            db = dw_rs(a, df, FB, FB, tag + "_dwdown")
        else:
            db = dw_rs(a, df, FB, FB, tag + "_dwdown", height=B2_ROWS(l), off=B_FF2D(l), into=dB2)
        dh_in, dpre = mm_cs_t_rms(dgu, wa, 0, 2 * FB, 2 * FB, h_in, _row(pre, l), dh, tag + "_gateup")
        da = dw_cs(xn, dgu, 2 * FB, 2 * FB, tag + "_dwgateup", width=None if which == 1 else A2_COLS(l))
        return dh_in, dpre, dpost, (da, db)

    for l in reversed(range(4)):
        rec = saved[l]
        GA1, GB1, GA, GB, GC = W[l]
        gC = {}
        tag = "ple_l%d_b" % l
        h_in, xn, zg, pe, pb = rec["ple"]
        gpost = _row(small["ple_post_g"], l)
        if tok is not None:
            gpost = gpost + tok
        dzg, dpe, dh, gains["ple_post_g"][l], gains["ple_gate_norm_g"][l] = ple_bwd(
            dh, zg, pe, h_in, GB, B_PLEG(l), gpost, _row(small["ple_gate_norm_g"], l), tag)
        gC[C_PROJ(l)] = dw_cs(pb, dpe, 256, 256, tag + "_dwproj")
        dB2 = dw_rs(xn, dzg, 256, 256, tag + "_dwgate", height=B2_ROWS(l), off=B_PLEG(l))

        dh, gains["ff2_pre_g"][l], gains["ff2_post_g"][l], (dA2, dB2) = ff_bwd(
            dh, l, 2, small["ff2_pre_g"], small["ff2_post_g"], rec["ff2"])

        tag = "mix_l%d_b" % l
        mix = rec["mix"]
        h_in, f = mix[0], mix[-1]
        if l == 1:
            _, pooled, ypre, _ = mix
            df, gains["mix_post_g"][l] = post_res_bwd(dh, f, _row(small["mix_post_g"], l), 1.0, F32, tag + "_post")
            dhn, dwg, dsc = pool_bwd(df, ypre, pooled, bgrp, vrow(V_BSCALE), tag + "_pool")
            gC[C_BGRP] = dwg.astype(BF16).reshape(4, NCHIP, 64, 256).transpose(1, 0, 2, 3).reshape(NCHIP, 256, 256)
            gV[V_BSCALE] = jnp.pad(dsc, ((0, 7), (0, 0)))
            dh, gains["mix_pre_g"][l] = rms_bwd(dhn, h_in, _row(small["mix_pre_g"], l), dh, tag + "_pre")
        else:
            gpre = _row(small["mix_pre_g"], l)
            df, gains["mix_post_g"][l] = post_res_bwd(dh, f, _row(small["mix_post_g"], l), 1.0, BF16, tag + "_post")
            if l == 0:
                _, hn, zpre, y, _ = mix
                dy = mm_rs_t(df, GB, B_AOUT, 768, 768, tag + "_out")
                dB2 = dw_rs(y, df, 768, 768, tag + "_dwout", height=B2_ROWS(l), off=B_AOUT, into=dB2)
                dz, dws, dbs, dvg, dvb = gmlp_mid_bwd(zpre, dy, small["a_v_norm_g"], small["a_v_norm_b"], a_ws, bsf,
                                                      tag + "_gate")
                extra.update(a_w_s=dws, a_b_s=dbs[:, 0, :], a_v_norm_g=dvg, a_v_norm_b=dvb)
                dA2 = dw_cs(hn, dz, 1536, 1536, tag + "_dwin", width=A2_COLS(l), off=A_AIN, into=dA2)
                dh, gains["mix_pre_g"][l] = mm_cs_t_rms(dz, GA, A_AIN, 1536, 1536, h_in, gpre, dh, tag + "_in")
            elif l == 2:
                _, hn, ag, zc, zs, _ = mix
                dzs = mm_rs_t(df, GB, B_CPW2, 256, 256, tag + "_pw2")
                dB2 = dw_rs(zs, df, 256, 256, tag + "_dwpw2", height=B2_ROWS(l), off=B_CPW2, into=dB2)
                dzc, dng, dnb = conf_ln_bwd(dzs, zc, vrow(V_CNG), vrow(V_CNB), tag + "_ln")
                da_, dg_, dwdw, dbdw = conf_conv_bwd(dzc, ag, vecs[V_CDW:V_CDW + 32], tag + "_conv")
                dag = merge_cols([da_, dg_], tag + "_merge")
                gV[V_CDW] = dwdw
                gV[V_CBDW] = jnp.pad(dbdw, ((0, 7), (0, 0)))
                gV[V_CNG] = jnp.pad(dng, ((0, 7), (0, 0)))
                gV[V_CNB] = jnp.pad(dnb, ((0, 7), (0, 0)))
                dA2 = dw_cs(hn, dag, 512, 512, tag + "_dwpw1", width=A2_COLS(l), off=A_CIN, into=dA2)
                dh, gains["mix_pre_g"][l] = mm_cs_t_rms(dag, GA, A_CIN, 512, 512, h_in, gpre, dh, tag + "_pw1")
            else:
                _, hn, bgx, y, _ = mix
                dy = mm_rs_t(df, GB, B_DOUT, 256, 256, tag + "_out")
                dB2 = dw_rs(y, df, 256, 256, tag + "_dwout", height=B2_ROWS(l), off=B_DOUT, into=dB2)
                db_, dc_, dx_, dwc = sconv_bwd(dy, bgx, vecs[V_DCONV:V_DCONV + 8], tag + "_conv")
                dbgx = merge_cols([db_, dc_, dx_], tag + "_merge")
                gV[V_DCONV] = dwc
                dA2 = dw_cs(hn, dbgx, 768, 768, tag + "_dwin", width=A2_COLS(l), off=A_DIN, into=dA2)
                dh, gains["mix_pre_g"][l] = mm_cs_t_rms(dbgx, GA, A_DIN, 768, 768, h_in, gpre, dh, tag + "_in")

        dC = jnp.concatenate([gC[C_PROJ(l)]] + ([gC[C_BGRP]] if l == 1 else []), axis=1)
        tok = grads_ready(l, "b", (dA2, dB2, dC), dh)
        dh, gains["ff1_pre_g"][l], gains["ff1_post_g"][l], (dA1, dB1) = ff_bwd(
            dh, l, 1, small["ff1_pre_g"], small["ff1_post_g"], rec["ff1"], after=tok)
        layer_grads[l] = (dA1, dB1, dA2, dB2, dC)
        tok = grads_ready(l, "a", (dA1, dB1), dh)

    return loss_cols, dh, layer_grads, gV, gains, extra


GAIN_NAMES = ("ff1_pre_g", "ff1_post_g", "mix_pre_g", "mix_post_g", "ff2_pre_g", "ff2_post_g", "ple_gate_norm_g",
              "ple_post_g")


def _pad_rows(a, rows):
    return jnp.pad(a, ((0, rows - a.shape[0]), (0, 0)))


def kernel(x, p, ff1_pre_g, ff1_w_gate, ff1_w_up, ff1_w_down, ff1_post_g, mix_pre_g, mix_post_g, ff2_pre_g, ff2_w_gate, ff2_w_up, ff2_w_down, ff2_post_g, ple_gate_norm_g, ple_w_gate, ple_w_proj, ple_post_g, a_w_in, a_v_norm_g, a_v_norm_b, a_w_s, a_b_s, a_w_out, b_w_grp, b_scale, c_w_pw1, c_w_dw, c_b_dw, c_norm_g, c_norm_b, c_w_pw2, d_w_in, d_w_conv, d_w_out, loss_target, m_ff1_pre_g, m_ff1_w_gate, m_ff1_w_up, m_ff1_w_down, m_ff1_post_g, m_mix_pre_g, m_mix_post_g, m_ff2_pre_g, m_ff2_w_gate, m_ff2_w_up, m_ff2_w_down, m_ff2_post_g, m_ple_gate_norm_g, m_ple_w_gate, m_ple_w_proj, m_ple_post_g, m_a_w_in, m_a_v_norm_g, m_a_v_norm_b, m_a_w_s, m_a_b_s, m_a_w_out, m_b_w_grp, m_b_scale, m_c_w_pw1, m_c_w_dw, m_c_b_dw, m_c_norm_g, m_c_norm_b, m_c_w_pw2, m_d_w_in, m_d_w_conv, m_d_w_out, v_ff1_pre_g, v_ff1_w_gate, v_ff1_w_up, v_ff1_w_down, v_ff1_post_g, v_mix_pre_g, v_mix_post_g, v_ff2_pre_g, v_ff2_w_gate, v_ff2_w_up, v_ff2_w_down, v_ff2_post_g, v_ple_gate_norm_g, v_ple_w_gate, v_ple_w_proj, v_ple_post_g, v_a_w_in, v_a_v_norm_g, v_a_v_norm_b, v_a_w_s, v_a_b_s, v_a_w_out, v_b_w_grp, v_b_scale, v_c_w_pw1, v_c_w_dw, v_c_b_dw, v_c_norm_g, v_c_norm_b, v_c_w_pw2, v_d_w_in, v_d_w_conv, v_d_w_out):
    args = dict(locals())
    wnames = ["ff1_pre_g", "ff1_w_gate", "ff1_w_up", "ff1_w_down", "ff1_post_g", "mix_pre_g", "mix_post_g",
              "ff2_pre_g", "ff2_w_gate", "ff2_w_up", "ff2_w_down", "ff2_post_g", "ple_gate_norm_g", "ple_w_gate",
              "ple_w_proj", "ple_post_g", "a_w_in", "a_v_norm_g", "a_v_norm_b", "a_w_s", "a_b_s", "a_w_out",
              "b_w_grp", "b_scale", "c_w_pw1", "c_w_dw", "c_b_dw", "c_norm_g", "c_norm_b", "c_w_pw2", "d_w_in",
              "d_w_conv", "d_w_out"]

    P = pack_weights(args)
    G0a = all_gather_chips([P[0][0], P[0][1], P[4]], "gather_l0a")
    vecs = G0a[2].transpose(1, 0, 2).reshape(V_ROWS, DM)
    flying = {}

    def start(key, bufs, after):
        send, recv, slots, token = gather_start(own_slots(list(bufs)), after, "gather_start_l" + key)
        flying[key] = (send, recv, slots)
        return token[0, 0]

    tok = start("0b", P[0][2:], G0a[0])
    arrived = {}

    def weights_of(l, part, h):
        if l == 0 and part == "a":
            return G0a[0], G0a[1], None
        key = "0b" if l == 0 else str(l)
        wtok = None
        if key not in arrived:
            send, recv, slots = flying[key]
            n = len(slots)
            landed = gather_wait(send, recv, slots, list(range(n)), h, "gather_wait_l" + key)
            arrived[key] = gather_pass(landed, "gather_pass_l" + key)
            if l < 3:
                wtok = start(str(l + 1), P[l + 1], arrived[key][0])
        got = arrived[key]
        if l == 0:
            return tuple(got) + (wtok,)
        return tuple(got[:2]) + (wtok,) if part == "a" else tuple(got[2:]) + (None,)

    pending = {}
    reduced = {}
    held = {}

    def finish(key, after):
        kinds, send, recv, sums, landing = pending.pop(key)
        landed = chip_exchange_wait(send, recv, sums, landing, after, "rs_wait_l" + key)
        halves = [sum_slots(landed[i], *_tile(k, landed[i]), name="rs_sum_chips_l%s_%d%s" % (key, i, k))
                  for i, k in enumerate(kinds)]
        reduced[key] = pair_share(halves, "rs_pair_share_l" + key)

    def grads_ready(l, part, bufs, dh):
        if part == "b" and l > 0:
            held[l] = list(bufs)
            return None
        if part == "a" and l > 0:
            key, kinds, parts = str(l), "ABABC", list(bufs) + held.pop(l)
        elif part == "b":
            key, kinds, parts = "0b", "ABC", list(bufs)
        else:
            finish("0b", dh)
            return None
        for other in list(pending):
            finish(other, dh)
        got = pair_exchange(parts, "rs_pair_exchange_l" + key)
        sums = [add_half(parts[i], got[i], *_tile(k, got[i]), name="rs_add_pair_l%s_%d%s" % (key, i, k))
                for i, k in enumerate(kinds)]
        send, recv, sums, landing, token = chip_exchange_start(sums, "rs_start_l" + key)
        pending[key] = (kinds, send, recv, sums, landing)
        return token[0, 0]

    small = {k: args[k] for k in GAIN_NAMES}
    small["ff1_pre_g"] = ff1_pre_g + tok
    small["a_v_norm_g"] = a_v_norm_g
    small["a_v_norm_b"] = a_v_norm_b
    loss_cols, grad_x, layer_grads, gV, gains, extra = local_step(
        x[0], p[:, 0], loss_target[0], small, weights_of, vecs, a_w_s[0], a_b_s[0], grads_ready)

    loss = lax.psum((0.5 / DM) * jnp.sum(loss_cols), ("x", "y", "c"))

    deltas, new_m, new_v = {}, {}, {}

    def update(k, g):
        if args[k].shape[-1] == FW:
            t = lambda a: jnp.swapaxes(a, 1, 2)
            outs = adamw(t(args[k]), t(g), t(args["m_" + k]), t(args["v_" + k]), "adamw_" + k)
            deltas[k], new_m[k], new_v[k] = (t(o) for o in outs)
        else:
            deltas[k], new_m[k], new_v[k] = adamw(args[k], g, args["m_" + k], args["v_" + k], "adamw_" + k)

    dV, dE = pack_small_grads(gV, gains, extra)
    red = reduce_scatter({"A": layer_grads[0][0], "B": layer_grads[0][1], "V": dV, "E": dE}, "rs_l0a")
    (gE,) = all_gather_chips([red["E"]], "gather_replicated_grads")
    per_layer = [[red["A"], red["B"]] + list(reduced["0b"])] + [list(reduced[str(l)]) for l in (1, 2, 3)]
    grads = unpack_grads(per_layer, red["V"], gE.reshape(E_ROWS, DM))
    for k in wnames:
        update(k, grads[k])
    return (loss, grad_x[None], *[grads[k] for k in wnames], *[deltas[k] for k in wnames],
            *[new_m[k] for k in wnames], *[new_v[k] for k in wnames])


def pack_weights(w):
    padc = lambda a: jnp.pad(a, ((0, 0), (0, FB - FW)))
    mix_in = [w["a_w_in"][0], None, w["c_w_pw1"][0], w["d_w_in"][0]]
    mix_out = [w["a_w_out"][0], None, w["c_w_pw2"][0], w["d_w_out"][0]]
    packed = []
    for l in range(4):
        a1 = jnp.concatenate([padc(w["ff1_w_gate"][l]), padc(w["ff1_w_up"][l])], axis=1).astype(BF16)
        b1 = _pad_rows(w["ff1_w_down"][l], FB).astype(BF16)
        cols = [padc(w["ff2_w_gate"][l]), padc(w["ff2_w_up"][l])]
        rows = [_pad_rows(w["ff2_w_down"][l], FB)]
        if l != 1:
            cols.append(mix_in[l])
            rows.append(mix_out[l])
        rows.append(w["ple_w_gate"][l])
        proj = [w["ple_w_proj"][l]] + ([w["b_w_grp"][0].reshape(256, 256)] if l == 1 else [])
        packed.append((a1, b1, jnp.concatenate(cols, axis=1).astype(BF16), jnp.concatenate(rows, axis=0).astype(BF16),
                       jnp.concatenate(proj, axis=0).astype(BF16)))
    PV = jnp.concatenate([_pad_rows(w["b_scale"], 8), _pad_rows(w["c_b_dw"], 8), _pad_rows(w["c_norm_g"], 8),
                          _pad_rows(w["c_norm_b"], 8), _pad_rows(w["d_w_conv"][0], 8), _pad_rows(w["c_w_dw"][0], 40)],
                         axis=0)
    return packed + [PV]


def pack_small_grads(gV, gains, extra):
    dVt = jnp.concatenate([gV[V_BSCALE], gV[V_CBDW], gV[V_CNG], gV[V_CNB], gV[V_DCONV], gV[V_CDW],
                           jnp.zeros((8, DM), F32)], axis=0)
    dV = dVt.reshape(V_ROWS, NCHIP, 256).transpose(1, 0, 2)
    rowsE = [_pad_rows(jnp.concatenate(gains[k], axis=0), 8) for k in GAIN_NAMES]
    rowsE += [_pad_rows(extra["a_v_norm_g"].reshape(3, DM), 8), _pad_rows(extra["a_v_norm_b"].reshape(3, DM), 8),
              jnp.pad(extra["a_b_s"].reshape(1536), (0, 8 * DM - 1536)).reshape(8, DM),
              extra["a_w_s"].reshape(192, DM)]
    dE = _pad_rows(jnp.concatenate(rowsE, axis=0), E_ROWS).reshape(NCHIP, E_ROWS // NCHIP, DM)
    return dV, dE


def unpack_grads(per_layer, RV, gE):
    grads = {}
    for i, k in enumerate(GAIN_NAMES):
        grads[k] = lambda i=i: gE[8 * i:8 * i + 4]
    grads["a_v_norm_g"] = lambda: gE[64:67].reshape(1, 3072)
    grads["a_v_norm_b"] = lambda: gE[72:75].reshape(1, 3072)
    grads["a_b_s"] = lambda: gE[80:88].reshape(8 * DM)[:1536].reshape(1, 12, 128)
    grads["a_w_s"] = lambda: gE[88:280].reshape(1, 12, 128, 128)
    col1 = lambda l, off, n: per_layer[l][0][:, off:off + n]
    col2 = lambda l, off, n: per_layer[l][2][:, off:off + n]
    grads["ff1_w_gate"] = lambda: jnp.stack([col1(l, A_FF(l, 0), FW) for l in range(4)])
    grads["ff1_w_up"] = lambda: jnp.stack([col1(l, A_FF(l, 1), FW) for l in range(4)])
    grads["ff2_w_gate"] = lambda: jnp.stack([col2(l, A_FF(l, 2), FW) for l in range(4)])
    grads["ff2_w_up"] = lambda: jnp.stack([col2(l, A_FF(l, 3), FW) for l in range(4)])
    grads["a_w_in"] = lambda: col2(0, A_AIN, 1536)[None]
    grads["c_w_pw1"] = lambda: col2(2, A_CIN, 512)[None]
    grads["d_w_in"] = lambda: col2(3, A_DIN, 768)[None]
    row2 = lambda l, off, n: per_layer[l][3][off:off + n]
    grads["ff1_w_down"] = lambda: jnp.stack([per_layer[l][1][:FW] for l in range(4)])
    grads["ff2_w_down"] = lambda: jnp.stack([row2(l, B_FF2D(l), FW) for l in range(4)])
    grads["ple_w_gate"] = lambda: jnp.stack([row2(l, B_PLEG(l), 256) for l in range(4)])
    grads["a_w_out"] = lambda: row2(0, B_AOUT, 768)[None]
    grads["c_w_pw2"] = lambda: row2(2, B_CPW2, 256)[None]
    grads["d_w_out"] = lambda: row2(3, B_DOUT, 256)[None]
    grads["ple_w_proj"] = lambda: jnp.stack([per_layer[l][4][C_PROJ(l):C_PROJ(l) + 256] for l in range(4)])
    grads["b_w_grp"] = lambda: per_layer[1][4][C_BGRP:C_BGRP + 256].reshape(1, 4, 64, 256)
    grads["b_scale"] = lambda: RV[V_BSCALE:V_BSCALE + 1]
    grads["c_b_dw"] = lambda: RV[V_CBDW:V_CBDW + 1]
    grads["c_norm_g"] = lambda: RV[V_CNG:V_CNG + 1]
    grads["c_norm_b"] = lambda: RV[V_CNB:V_CNB + 1]
    grads["d_w_conv"] = lambda: RV[V_DCONV:V_DCONV + 3][None]
    grads["c_w_dw"] = lambda: RV[V_CDW:V_CDW + 31][None]
    return {k: f() for k, f in grads.items()}
```

```python
import functools
import math

import jax
import jax.numpy as jnp
from jax import lax
from jax.experimental import pallas as pl
from jax.experimental.pallas import tpu as pltpu

F32, BF16 = jnp.float32, jnp.bfloat16
EPS = 1e-6
DM = 1024
FW = 704
FB = 768
NCHIP = 4
VMEM_LIMIT = 56 * 1024 * 1024
ANY = pl.BlockSpec(memory_space=pl.ANY)
MESH = pl.DeviceIdType.MESH

A_FF = lambda l, j: (j % 2) * FB
A_AIN = A_CIN = A_DIN = 2 * FB
A2_COLS = lambda l: 2 * FB + (1536, 0, 512, 768)[l]
B_FF1D = lambda l: 0
B_FF2D = lambda l: 0
B_AOUT = B_CPW2 = B_DOUT = FB
B_PLEG = lambda l: FB + (768, 0, 256, 256)[l]
B2_ROWS = lambda l: B_PLEG(l) + 256
C_PROJ = lambda l: 0
C_BGRP = 256
V_BSCALE, V_CBDW, V_CNG, V_CNB, V_DCONV, V_CDW, V_ROWS = 0, 8, 16, 24, 32, 40, 80
E_ROWS = 320

ADAM_LR, ADAM_B1, ADAM_B2, ADAM_EPS, ADAM_WD, ADAM_STEP = 0.001, 0.9, 0.999, 1e-08, 0.01, 10


def _cp(sem):
    return pltpu.CompilerParams(dimension_semantics=sem, vmem_limit_bytes=VMEM_LIMIT)


def _sig(x):
    return 0.5 * jnp.tanh(0.5 * x) + 0.5


_GC = math.sqrt(2.0 / math.pi)


def _gelu(x):
    return 0.5 * x * (1.0 + jnp.tanh(_GC * (x + 0.044715 * x * x * x)))


def _gelu_grad(x):
    t = jnp.tanh(_GC * (x + 0.044715 * x * x * x))
    return 0.5 * (1.0 + t) + 0.5 * x * (1.0 - t * t) * _GC * (1.0 + 3.0 * 0.044715 * x * x)


def _dot_nn(a, b):
    return lax.dot_general(a, b, (((1,), (0,)), ((), ())), preferred_element_type=F32)


def _dot_nt(a, b):
    return lax.dot_general(a, b, (((1,), (1,)), ((), ())), preferred_element_type=F32)


def _dot_tn(a, b):
    return lax.dot_general(a, b, (((0,), (0,)), ((), ())), preferred_element_type=F32)


def mm_cs(x, G, off, nb, tn, out_dtype, name, roff=0):
    T, K = x.shape
    tm = min(1024, T)
    nj, ob, rb_ = nb // tn, off // tn, roff // K
    assert nb % tn == 0 and off % tn == 0 and roff % K == 0

    def body(x_ref, w_ref, o_ref):
        o_ref[...] = _dot_nn(x_ref[...], w_ref[...]).astype(o_ref.dtype)

    return pl.pallas_call(
        body, name=name, grid=(T // tm, NCHIP, nj),
        in_specs=[pl.BlockSpec((tm, K), lambda i, s, j: (i, 0)),
                  pl.BlockSpec((None, K, tn), lambda i, s, j: (s, rb_, ob + j))],
        out_specs=pl.BlockSpec((tm, tn), lambda i, s, j: (i, s * nj + j)),
        out_shape=jax.ShapeDtypeStruct((T, NCHIP * nb), out_dtype),
        compiler_params=_cp(("parallel", "arbitrary", "arbitrary")))(x, G)


def mm_rs_t(dy, G, off, rb, tk, name):
    T, N = dy.shape
    tm = min(1024, T)
    nkk, ob = rb // tk, off // tk
    nk = NCHIP * nkk

    def body(dy_ref, w_ref, o_ref):
        o_ref[...] = _dot_nt(dy_ref[...], w_ref[...]).astype(o_ref.dtype)

    return pl.pallas_call(
        body, name=name, grid=(T // tm, nk),
        in_specs=[pl.BlockSpec((tm, N), lambda i, k: (i, 0)),
                  pl.BlockSpec((None, tk, N), lambda i, k: (k // nkk, ob + k % nkk, 0))],
        out_specs=pl.BlockSpec((tm, tk), lambda i, k: (i, k)),
        out_shape=jax.ShapeDtypeStruct((T, NCHIP * rb), BF16),
        compiler_params=_cp(("parallel", "arbitrary")))(dy, G)


def mm_tn(a, b, tmm, tn, out_shape, out_map, name, into=None):
    T, M = a.shape
    N = b.shape[1]
    tt = min(2048, T)
    nt = T // tt

    def body(a_ref, b_ref, o_ref, acc):
        t = pl.program_id(2)

        @pl.when(t == 0)
        def _():
            acc[...] = jnp.zeros_like(acc)

        acc[...] += _dot_tn(a_ref[...], b_ref[...])

        @pl.when(t == nt - 1)
        def _():
            o_ref[...] = acc[...].astype(o_ref.dtype)

    in_specs = [pl.BlockSpec((tt, tmm), lambda i, j, t: (t, i)), pl.BlockSpec((tt, tn), lambda i, j, t: (t, j))]
    operands = (a, b)
    if into is None:
        def kern(a_ref, b_ref, o_ref, acc):
            body(a_ref, b_ref, o_ref, acc)
        aliases = {}
    else:
        def kern(a_ref, b_ref, into_ref, o_ref, acc):
            body(a_ref, b_ref, o_ref, acc)
        in_specs.append(ANY)
        operands = (a, b, into)
        aliases = {2: 0}
        out_shape = into.shape
    return pl.pallas_call(
        kern, name=name, grid=(M // tmm, N // tn, nt), in_specs=in_specs,
        out_specs=pl.BlockSpec((None, tmm, tn), lambda i, j, t: out_map(i, j)),
        out_shape=jax.ShapeDtypeStruct(out_shape, BF16), input_output_aliases=aliases,
        scratch_shapes=[pltpu.VMEM((tmm, tn), F32)],
        compiler_params=_cp(("parallel", "parallel", "arbitrary")))(*operands)


def dw_cs(x, dy, nb, tn, name, width=None, off=0, into=None):
    K = x.shape[1]
    nj, ob = nb // tn, off // tn
    assert off % tn == 0
    return mm_tn(x, dy, K, tn, (NCHIP, K, width or nb), lambda i, j: (j // nj, 0, ob + j % nj), name, into)


def dw_rs(a, dy, rb, tr, name, height=None, off=0, into=None):
    N = dy.shape[1]
    ni, ob = rb // tr, off // tr
    assert off % tr == 0
    return mm_tn(a, dy, tr, N, (NCHIP, height or rb, N), lambda i, j: (i // ni, ob + i % ni, 0), name, into)


def ff_gateup(xn, GA, off, name):
    T, K = xn.shape
    tm = min(1024, T)
    ob = off // (2 * FB)
    assert off % (2 * FB) == 0

    sub = min(512, tm)

    def body(x_ref, w_ref, gu_ref, a_ref):
        for r0 in range(0, tm, sub):
            r = _dot_nn(x_ref[r0:r0 + sub, :], w_ref[...])
            g, u = r[:, :FB], r[:, FB:]
            gu_ref[r0:r0 + sub, :] = r.astype(gu_ref.dtype)
            a_ref[r0:r0 + sub, :] = (g * _sig(g) * u).astype(a_ref.dtype)

    return pl.pallas_call(
        body, name=name, grid=(T // tm, NCHIP),
        in_specs=[pl.BlockSpec((tm, K), lambda i, s: (i, 0)),
                  pl.BlockSpec((None, K, 2 * FB), lambda i, s: (s, 0, ob))],
        out_specs=[pl.BlockSpec((tm, 2 * FB), lambda i, s: (i, s)), pl.BlockSpec((tm, FB), lambda i, s: (i, s))],
        out_shape=[jax.ShapeDtypeStruct((T, NCHIP * 2 * FB), BF16), jax.ShapeDtypeStruct((T, NCHIP * FB), BF16)],
        compiler_params=_cp(("parallel", "arbitrary")))(xn, GA)


def mm_rs_post(a, G, off, rb, tk, h, g, scale, name, g_next=None):
    T = a.shape[0]
    N = G.shape[2]
    tm = min(1024, T)
    sub = min(512, tm)
    nkk, ob = rb // tk, off // tk
    nk = NCHIP * nkk
    assert rb % tk == 0 and off % tk == 0
    more = g_next is not None

    def body(a_ref, w_ref, h_ref, g_ref, *rest):
        if more:
            gn_ref, f_ref, o_ref, xn_ref, acc = rest
        else:
            f_ref, o_ref, acc = rest
        k = pl.program_id(1)

        @pl.when(k == 0)
        def _():
            acc[...] = jnp.zeros_like(acc)

        acc[...] += _dot_nn(a_ref[...], w_ref[...])

        @pl.when(k == nk - 1)
        def _():
            for r0 in range(0, tm, sub):
                f = acc[r0:r0 + sub, :]
                f_ref[r0:r0 + sub, :] = f
                r = lax.rsqrt(jnp.mean(f * f, axis=-1, keepdims=True) + EPS)
                hn = h_ref[r0:r0 + sub, :] + scale * (f * r * g_ref[...])
                o_ref[r0:r0 + sub, :] = hn
                if more:
                    rn = lax.rsqrt(jnp.mean(hn * hn, axis=-1, keepdims=True) + EPS)
                    xn_ref[r0:r0 + sub, :] = (hn * rn * gn_ref[...]).astype(xn_ref.dtype)

    row = pl.BlockSpec((tm, N), lambda i, k: (i, 0))
    row1 = pl.BlockSpec((tm, N), lambda i, k: (i, 0), pipeline_mode=pl.Buffered(1))
    vec = pl.BlockSpec((1, N), lambda i, k: (0, 0))
    return pl.pallas_call(
        body, name=name, grid=(T // tm, nk),
        in_specs=[pl.BlockSpec((tm, tk), lambda i, k: (i, k)),
                  pl.BlockSpec((None, tk, N), lambda i, k: (k // nkk, ob + k % nkk, 0)),
                  row1, vec] + ([vec] if more else []),
        out_specs=[row, row] + ([row] if more else []),
        out_shape=[jax.ShapeDtypeStruct((T, N), F32), jax.ShapeDtypeStruct((T, N), F32)]
        + ([jax.ShapeDtypeStruct((T, N), BF16)] if more else []),
        scratch_shapes=[pltpu.VMEM((tm, N), F32)],
        compiler_params=_cp(("parallel", "arbitrary")))(*((a, G, h, g) + ((g_next,) if more else ())))


def ff_bwd_down(dh, f, g, GB, down, gu, name):
    T, N = dh.shape
    tm = min(1024, T)
    sub = min(256, tm)
    ob = down // FB

    def body(d_ref, f_ref, g_ref, w_ref, gu_ref, df_ref, dg_ref, dgu_ref):
        i, s = pl.program_id(0), pl.program_id(1)

        @pl.when(s == 0)
        def _():
            dg = jnp.zeros((1, N), F32)
            for r0 in range(0, tm, sub):
                f = f_ref[r0:r0 + sub, :]
                r = lax.rsqrt(jnp.mean(f * f, axis=-1, keepdims=True) + EPS)
                d = 0.5 * d_ref[r0:r0 + sub, :]
                t = d * g_ref[...]
                df_ref[r0:r0 + sub, :] = (
                    r * t - f * (r * r * r * jnp.mean(t * f, axis=-1, keepdims=True))).astype(df_ref.dtype)
                dg = dg + jnp.sum(d * f * r, axis=0, keepdims=True)
            _acc_rows(dg_ref, i, dg)

        for r0 in range(0, tm, sub):
            da = _dot_nt(df_ref[r0:r0 + sub, :], w_ref[...])
            gt = gu_ref[r0:r0 + sub, :FB].astype(F32)
            u = gu_ref[r0:r0 + sub, FB:].astype(F32)
            sg = _sig(gt)
            silu = gt * sg
            dgu_ref[r0:r0 + sub, :FB] = (da * u * (sg + silu - silu * sg)).astype(dgu_ref.dtype)
            dgu_ref[r0:r0 + sub, FB:] = (da * silu).astype(dgu_ref.dtype)

    row1 = pl.BlockSpec((tm, N), lambda i, s: (i, 0), pipeline_mode=pl.Buffered(1))
    row = pl.BlockSpec((tm, N), lambda i, s: (i, 0))
    vec = pl.BlockSpec((1, N), lambda i, s: (0, 0))
    return pl.pallas_call(
        body, name=name, grid=(T // tm, NCHIP),
        in_specs=[row1, row1, vec, pl.BlockSpec((None, FB, N), lambda i, s: (s, ob, 0)),
                  pl.BlockSpec((tm, 2 * FB), lambda i, s: (i, s))],
        out_specs=[row, vec, pl.BlockSpec((tm, 2 * FB), lambda i, s: (i, s))],
        out_shape=[jax.ShapeDtypeStruct((T, N), BF16), jax.ShapeDtypeStruct((1, N), F32),
                   jax.ShapeDtypeStruct((T, NCHIP * 2 * FB), BF16)],
        compiler_params=_cp(("arbitrary", "arbitrary")))(dh, f, g, GB, gu)


def mm_cs_t_rms(dy, G, off, nb, tn, h, g, skip, name):
    T = dy.shape[0]
    K = G.shape[1]
    tm = min(512, T)
    sub = min(256, tm)
    assert off % nb == 0 and nb % tn == 0

    def body(dy_ref, w_ref, h_ref, g_ref, s_ref, o_ref, dg_ref):
        i = pl.program_id(0)
        dg = jnp.zeros((1, K), F32)
        for r0 in range(0, tm, sub):
            d = _dot_nt(dy_ref[r0:r0 + sub, :nb], w_ref[0])
            for s in range(1, NCHIP):
                d = d + _dot_nt(dy_ref[r0:r0 + sub, s * nb:(s + 1) * nb], w_ref[s])
            x = h_ref[r0:r0 + sub, :]
            r = lax.rsqrt(jnp.mean(x * x, axis=-1, keepdims=True) + EPS)
            xh = x * r
            t = d * g_ref[...]
            o_ref[r0:r0 + sub, :] = s_ref[r0:r0 + sub, :] + r * (t - xh * jnp.mean(t * xh, axis=-1, keepdims=True))
            dg = dg + jnp.sum(d * xh, axis=0, keepdims=True)
        _acc_rows(dg_ref, i, dg)

    row = pl.BlockSpec((tm, K), lambda i: (i, 0))
    vec = pl.BlockSpec((1, K), lambda i: (0, 0))
    return pl.pallas_call(
        body, name=name, grid=(T // tm,),
        in_specs=[pl.BlockSpec((tm, NCHIP * nb), lambda i: (i, 0)),
                  pl.BlockSpec((NCHIP, K, nb), lambda i: (0, 0, off // nb), pipeline_mode=pl.Buffered(1)),
                  row, vec, row],
        out_specs=[row, vec],
        out_shape=[jax.ShapeDtypeStruct((T, K), F32), jax.ShapeDtypeStruct((1, K), F32)],
        compiler_params=_cp(("arbitrary",)))(dy, G, h, g, skip)


def _rows(tm, C):
    return pl.BlockSpec((tm, C), lambda i: (i, 0))


def _vec(C):
    return pl.BlockSpec((1, C), lambda i: (0, 0))


def _acc_rows(ref, i, val):
    @pl.when(i == 0)
    def _():
        ref[...] = val

    @pl.when(i > 0)
    def _():
        ref[...] += val


def rms_fwd(h, g, out_dtype, name):
    T, C = h.shape
    tm = min(512, T)

    def body(h_ref, g_ref, o_ref):
        x = h_ref[...]
        r = lax.rsqrt(jnp.mean(x * x, axis=-1, keepdims=True) + EPS)
        o_ref[...] = (x * r * g_ref[...]).astype(o_ref.dtype)

    return pl.pallas_call(
        body, name=name, grid=(T // tm,), in_specs=[_rows(tm, C), _vec(C)], out_specs=_rows(tm, C),
        out_shape=jax.ShapeDtypeStruct((T, C), out_dtype), compiler_params=_cp(("parallel",)))(h, g)


def rms_bwd(dxn, h, g, dh_skip, name):
    T, C = h.shape
    tm = min(512, T)

    def body(d_ref, h_ref, g_ref, s_ref, o_ref, dg_ref):
        i = pl.program_id(0)
        x = h_ref[...]
        r = lax.rsqrt(jnp.mean(x * x, axis=-1, keepdims=True) + EPS)
        xh = x * r
        d = d_ref[...].astype(F32)
        t = d * g_ref[...]
        o_ref[...] = s_ref[...] + r * (t - xh * jnp.mean(t * xh, axis=-1, keepdims=True))
        _acc_rows(dg_ref, i, jnp.sum(d * xh, axis=0, keepdims=True))

    return pl.pallas_call(
        body, name=name, grid=(T // tm,),
        in_specs=[_rows(tm, C), _rows(tm, C), _vec(C), _rows(tm, C)],
        out_specs=[_rows(tm, C), _vec(C)],
        out_shape=[jax.ShapeDtypeStruct((T, C), F32), jax.ShapeDtypeStruct((1, C), F32)],
        compiler_params=_cp(("arbitrary",)))(dxn, h, g, dh_skip)


def post_res(h, f, g, scale, name):
    T, C = h.shape
    tm = min(512, T)

    def body(h_ref, f_ref, g_ref, o_ref):
        f = f_ref[...]
        r = lax.rsqrt(jnp.mean(f * f, axis=-1, keepdims=True) + EPS)
        o_ref[...] = h_ref[...] + scale * (f * r * g_ref[...])

    return pl.pallas_call(
        body, name=name, grid=(T // tm,), in_specs=[_rows(tm, C), _rows(tm, C), _vec(C)],
        out_specs=_rows(tm, C), out_shape=jax.ShapeDtypeStruct((T, C), F32),
        compiler_params=_cp(("parallel",)))(h, f, g)


def post_res_bwd(dh, f, g, scale, out_dtype, name):
    T, C = dh.shape
    tm = min(512, T)

    def body(d_ref, f_ref, g_ref, o_ref, dg_ref):
        i = pl.program_id(0)
        f = f_ref[...]
        r = lax.rsqrt(jnp.mean(f * f, axis=-1, keepdims=True) + EPS)
        d = scale * d_ref[...]
        t = d * g_ref[...]
        o_ref[...] = (r * t - f * (r * r * r * jnp.mean(t * f, axis=-1, keepdims=True))).astype(o_ref.dtype)
        _acc_rows(dg_ref, i, jnp.sum(d * f * r, axis=0, keepdims=True))

    return pl.pallas_call(
        body, name=name, grid=(T // tm,), in_specs=[_rows(tm, C), _rows(tm, C), _vec(C)],
        out_specs=[_rows(tm, C), _vec(C)],
        out_shape=[jax.ShapeDtypeStruct((T, C), out_dtype), jax.ShapeDtypeStruct((1, C), F32)],
        compiler_params=_cp(("arbitrary",)))(dh, f, g)


PLE_W = 256


def ple_fwd(h, xn, pb, GB, goff, GC, g, name, g_next=None):
    T, C = h.shape
    tm = min(512, T)
    more = g_next is not None

    def body(h_ref, x_ref, p_ref, wg_ref, wp_ref, g_ref, *rest):
        if more:
            gn_ref, z_ref, pe_ref, o_ref, xn_ref = rest
        else:
            z_ref, pe_ref, o_ref = rest
        zg = _dot_nn(x_ref[:, :PLE_W], wg_ref[0])
        for s in range(1, NCHIP):
            zg = zg + _dot_nn(x_ref[:, s * PLE_W:(s + 1) * PLE_W], wg_ref[s])
        z_ref[...] = zg
        for s in range(NCHIP):
            pe_ref[:, s * PLE_W:(s + 1) * PLE_W] = _dot_nn(p_ref[...], wp_ref[s])
        e = pe_ref[...] * _sig(zg)
        r = lax.rsqrt(jnp.mean(e * e, axis=-1, keepdims=True) + EPS)
        hn = h_ref[...] + e * r * g_ref[...]
        o_ref[...] = hn
        if more:
            rn = lax.rsqrt(jnp.mean(hn * hn, axis=-1, keepdims=True) + EPS)
            xn_ref[...] = (hn * rn * gn_ref[...]).astype(xn_ref.dtype)

    return pl.pallas_call(
        body, name=name, grid=(T // tm,),
        in_specs=[_rows(tm, C), _rows(tm, C), _rows(tm, PLE_W),
                  pl.BlockSpec((NCHIP, PLE_W, C), lambda i: (0, goff // PLE_W, 0)),
                  pl.BlockSpec((NCHIP, PLE_W, PLE_W), lambda i: (0, 0, 0)), _vec(C)] + ([_vec(C)] if more else []),
        out_specs=[_rows(tm, C)] * (4 if more else 3),
        out_shape=[jax.ShapeDtypeStruct((T, C), F32)] * 3 + ([jax.ShapeDtypeStruct((T, C), BF16)] if more else []),
        compiler_params=_cp(("parallel",)))(*((h, xn, pb, GB, GC, g) + ((g_next,) if more else ())))


def ple_bwd(dh, zg, pe, h_in, GB, goff, g, g_pre, name):
    T, C = dh.shape
    tm = min(512, T)

    def body(d_ref, z_ref, p_ref, h_ref, wg_ref, g_ref, gp_ref, dz_ref, dp_ref, o_ref, dg_ref, dgp_ref, dxn):
        i = pl.program_id(0)
        s = _sig(z_ref[...])
        pe_ = p_ref[...]
        e = pe_ * s
        r = lax.rsqrt(jnp.mean(e * e, axis=-1, keepdims=True) + EPS)
        d = d_ref[...]
        t = d * g_ref[...]
        de = r * t - e * (r * r * r * jnp.mean(t * e, axis=-1, keepdims=True))
        dp_ref[...] = (de * s).astype(dp_ref.dtype)
        dz = (de * pe_ * s * (1.0 - s)).astype(dz_ref.dtype)
        dz_ref[...] = dz
        _acc_rows(dg_ref, i, jnp.sum(d * e * r, axis=0, keepdims=True))
        for k in range(NCHIP):
            dxn[:, k * PLE_W:(k + 1) * PLE_W] = _dot_nt(dz, wg_ref[k])
        x = h_ref[...]
        rx = lax.rsqrt(jnp.mean(x * x, axis=-1, keepdims=True) + EPS)
        xh = x * rx
        dx = dxn[...]
        tx = dx * gp_ref[...]
        o_ref[...] = d + rx * (tx - xh * jnp.mean(tx * xh, axis=-1, keepdims=True))
        _acc_rows(dgp_ref, i, jnp.sum(dx * xh, axis=0, keepdims=True))

    return pl.pallas_call(
        body, name=name, grid=(T // tm,),
        in_specs=[_rows(tm, C), _rows(tm, C), _rows(tm, C), _rows(tm, C),
                  pl.BlockSpec((NCHIP, PLE_W, C), lambda i: (0, goff // PLE_W, 0)), _vec(C), _vec(C)],
        out_specs=[_rows(tm, C), _rows(tm, C), _rows(tm, C), _vec(C), _vec(C)],
        out_shape=[jax.ShapeDtypeStruct((T, C), BF16), jax.ShapeDtypeStruct((T, C), BF16),
                   jax.ShapeDtypeStruct((T, C), F32), jax.ShapeDtypeStruct((1, C), F32),
                   jax.ShapeDtypeStruct((1, C), F32)],
        scratch_shapes=[pltpu.VMEM((tm, C), F32)],
        compiler_params=_cp(("arbitrary",)))(dh, zg, pe, h_in, GB, g, g_pre)


def loss_head(h, tgt, name):
    T, C = h.shape
    tm = min(512, T)

    def body(h_ref, t_ref, d_ref, l_ref):
        i = pl.program_id(0)
        e = h_ref[...] - t_ref[...]
        d_ref[...] = e * (1.0 / C)
        _acc_rows(l_ref, i, jnp.sum(e * e, axis=0, keepdims=True))

    return pl.pallas_call(
        body, name=name, grid=(T // tm,), in_specs=[_rows(tm, C), _rows(tm, C)],
        out_specs=[_rows(tm, C), _vec(C)],
        out_shape=[jax.ShapeDtypeStruct((T, C), F32), jax.ShapeDtypeStruct((1, C), F32)],
        compiler_params=_cp(("arbitrary",)))(h, tgt)


AH, AG_N, AGW, CHUNK = 3072, 12, 256, 128


def _tril_bf16(w):
    r = lax.broadcasted_iota(jnp.int32, (CHUNK, CHUNK), 0)
    c = lax.broadcasted_iota(jnp.int32, (CHUNK, CHUNK), 1)
    return jnp.where(r >= c, w, 0.0).astype(BF16)


def _ln_stats(vs_ref, width):
    v = vs_ref[...]
    mu = jnp.sum(v, axis=-1, keepdims=True) * (1.0 / width)
    vc = v - mu
    var = jnp.sum(vc * vc, axis=-1, keepdims=True) * (1.0 / width)
    return mu, lax.rsqrt(var + EPS)


def gmlp_mid_fwd(zpre, vg, vb, ws, bsf, name):
    T = zpre.shape[0]

    def body(z_ref, vg_ref, vb_ref, ws_ref, bs_ref, y_ref, vs_ref):
        for g in range(AG_N):
            vs_ref[:, g * AGW:(g + 1) * AGW] = _gelu(z_ref[:, AH + g * AGW:AH + (g + 1) * AGW].astype(F32))
        mu, rstd = _ln_stats(vs_ref, AH)
        for g in range(AG_N):
            sl = slice(g * AGW, (g + 1) * AGW)
            vn = ((vs_ref[:, sl] - mu) * rstd * vg_ref[:, sl] + vb_ref[:, sl]).astype(BF16)
            sv = _dot_nn(_tril_bf16(ws_ref[g]), vn) + bs_ref[g]
            u = _gelu(z_ref[:, sl].astype(F32))
            y_ref[:, sl] = (u * sv).astype(y_ref.dtype)

    return pl.pallas_call(
        body, name=name, grid=(T // CHUNK,),
        in_specs=[_rows(CHUNK, 2 * AH), _vec(AH), _vec(AH),
                  pl.BlockSpec((AG_N, CHUNK, CHUNK), lambda i: (0, 0, 0)),
                  pl.BlockSpec((AG_N, CHUNK, AGW), lambda i: (0, 0, 0))],
        out_specs=_rows(CHUNK, AH), out_shape=jax.ShapeDtypeStruct((T, AH), BF16),
        scratch_shapes=[pltpu.VMEM((CHUNK, AH), F32)],
        compiler_params=_cp(("parallel",)))(zpre, vg, vb, ws, bsf)


def gmlp_mid_bwd(zpre, dy, vg, vb, ws, bsf, name):
    T = zpre.shape[0]

    def body(z_ref, dy_ref, vg_ref, vb_ref, ws_ref, bs_ref, dz_ref, dws_ref, dbs_ref, dvg_ref, dvb_ref,
             vs_ref, dvn_ref):
        i = pl.program_id(0)

        @pl.when(i == 0)
        def _():
            dws_ref[...] = jnp.zeros_like(dws_ref)
            dbs_ref[...] = jnp.zeros_like(dbs_ref)
            dvg_ref[...] = jnp.zeros_like(dvg_ref)
            dvb_ref[...] = jnp.zeros_like(dvb_ref)

        for g in range(AG_N):
            vs_ref[:, g * AGW:(g + 1) * AGW] = _gelu(z_ref[:, AH + g * AGW:AH + (g + 1) * AGW].astype(F32))
        mu, rstd = _ln_stats(vs_ref, AH)
        r_i = lax.broadcasted_iota(jnp.int32, (CHUNK, CHUNK), 0)
        c_i = lax.broadcasted_iota(jnp.int32, (CHUNK, CHUNK), 1)
        ones8 = jnp.ones((8, AGW), F32)
        m1 = jnp.zeros((CHUNK, 1), F32)
        m2 = jnp.zeros((CHUNK, 1), F32)
        for g in range(AG_N):
            sl = slice(g * AGW, (g + 1) * AGW)
            vh = (vs_ref[:, sl] - mu) * rstd
            vn = (vh * vg_ref[:, sl] + vb_ref[:, sl]).astype(BF16)
            wm = _tril_bf16(ws_ref[g])
            sv = _dot_nn(wm, vn) + bs_ref[g]
            zu = z_ref[:, sl].astype(F32)
            u = _gelu(zu)
            dyg = dy_ref[:, sl].astype(F32)
            dz_ref[:, sl] = (dyg * sv * _gelu_grad(zu)).astype(dz_ref.dtype)
            dsv = dyg * u
            dsv_b = dsv.astype(BF16)
            dws_ref[g] += jnp.where(r_i >= c_i, _dot_nt(dsv_b, vn), 0.0)
            dbs_ref[g] += _dot_nt(ones8, dsv)
            dvn = _dot_tn(wm, dsv_b)
            dvn_ref[:, sl] = dvn
            dvh = dvn * vg_ref[:, sl]
            m1 = m1 + jnp.sum(dvh, axis=-1, keepdims=True)
            m2 = m2 + jnp.sum(dvh * vh, axis=-1, keepdims=True)
            dvg_ref[:, sl] += jnp.sum(dvn * vh, axis=0, keepdims=True)
            dvb_ref[:, sl] += jnp.sum(dvn, axis=0, keepdims=True)
        m1 = m1 * (1.0 / AH)
        m2 = m2 * (1.0 / AH)
        for g in range(AG_N):
            sl = slice(g * AGW, (g + 1) * AGW)
            vh = (vs_ref[:, sl] - mu) * rstd
            dv = rstd * (dvn_ref[:, sl] * vg_ref[:, sl] - m1 - vh * m2)
            zv = z_ref[:, AH + g * AGW:AH + (g + 1) * AGW].astype(F32)
            dz_ref[:, AH + g * AGW:AH + (g + 1) * AGW] = (dv * _gelu_grad(zv)).astype(dz_ref.dtype)

    full3 = lambda a, b, c: pl.BlockSpec((a, b, c), lambda i: (0, 0, 0))
    return pl.pallas_call(
        body, name=name, grid=(T // CHUNK,),
        in_specs=[_rows(CHUNK, 2 * AH), _rows(CHUNK, AH), _vec(AH), _vec(AH),
                  full3(AG_N, CHUNK, CHUNK), full3(AG_N, CHUNK, AGW)],
        out_specs=[_rows(CHUNK, 2 * AH), full3(AG_N, CHUNK, CHUNK), full3(AG_N, 8, CHUNK), _vec(AH), _vec(AH)],
        out_shape=[jax.ShapeDtypeStruct((T, 2 * AH), BF16), jax.ShapeDtypeStruct((AG_N, CHUNK, CHUNK), F32),
                   jax.ShapeDtypeStruct((AG_N, 8, CHUNK), F32), jax.ShapeDtypeStruct((1, AH), F32),
                   jax.ShapeDtypeStruct((1, AH), F32)],
        scratch_shapes=[pltpu.VMEM((CHUNK, AH), F32), pltpu.VMEM((CHUNK, AH), F32)],
        compiler_params=_cp(("arbitrary",)))(zpre, dy, vg, vb, ws, bsf)


SLAB = 256
NSLAB = DM // SLAB
RC = 256
PAD = 32


def _col(T, j):
    return pl.BlockSpec((T, SLAB), lambda c: (0, j * NSLAB + c))


def _chunks(T, fn):
    def step(i, carry):
        fn(pl.multiple_of(i * RC, RC))
        return carry
    lax.fori_loop(0, T // RC, step, 0)


def _conv_taps(K):
    return [(r, [q for q in range(4) if 8 * q + r < K]) for r in range(min(8, K))]


def _causal_conv(zpad_ref, wrow, K, r0):
    acc = None
    for r, qs in _conv_taps(K):
        a = None
        for q in qs:
            term = wrow(8 * q + r) * zpad_ref[pl.ds(r0 + (PAD - 8 - 8 * q), RC + 8), :]
            a = term if a is None else a + term
        a = a if r == 0 else pltpu.roll(a, r, 0)
        acc = a if acc is None else acc + a
    return acc[8:, :]


def _anticausal_conv(gpad_ref, wrow, K, r0):
    acc = None
    for r, qs in _conv_taps(K):
        b = None
        for q in qs:
            term = wrow(8 * q + r) * gpad_ref[pl.ds(r0 + 8 * q, RC + 8), :]
            b = term if b is None else b + term
        b = b if r == 0 else pltpu.roll(b, RC + 8 - r, 0)
        acc = b if acc is None else acc + b
    return acc[:RC, :]


def _conv_dw(gpad_ref, zpad_ref, dw_ref, K, r0):
    for r, qs in _conv_taps(K):
        gw = gpad_ref[pl.ds(r0, RC + 8), :]
        p = (gw if r == 0 else pltpu.roll(gw, RC + 8 - r, 0))[:RC, :]
        for q in qs:
            z = zpad_ref[pl.ds(r0 + (PAD - 8 * q), RC), :]
            dw_ref[8 * q + r] += jnp.sum((p * z).reshape(RC // 8, 8, SLAB), axis=0)


def _zero_rows(ref, start, n):
    ref[pl.ds(start, n), :] = jnp.zeros((n, SLAB), F32)


def pool_fwd(hn, wg, sc, name):
    T = hn.shape[0]

    def body(h_ref, w_ref, s_ref, p_ref, yp_ref, y_ref, xpad):
        g = pl.program_id(0)
        wf = jnp.left_shift(2, g).astype(F32)
        _zero_rows(xpad, 0, PAD)

        def fill(r0):
            xpad[pl.ds(r0 + PAD, RC), :] = h_ref[pl.ds(r0, RC), :]
        _chunks(T, fill)

        def step(r0):
            w = xpad[pl.ds(r0 + (PAD - 16), RC + 16), :]
            s2 = w + pltpu.roll(w, 1, 0)
            s4 = s2 + pltpu.roll(s2, 2, 0)
            s8 = s4 + pltpu.roll(s4, 4, 0)
            s16 = s8 + pltpu.roll(s8, 8, 0)
            sel = jnp.where(g == 0, s2, jnp.where(g == 1, s4, jnp.where(g == 2, s8, s16)))[16:, :]
            t1 = (r0 + 1 + lax.broadcasted_iota(jnp.int32, (RC, SLAB), 0)).astype(F32)
            pooled = (sel / jnp.minimum(t1, wf) - w[16:, :]).astype(BF16)
            p_ref[pl.ds(r0, RC), :] = pooled
            yp = _dot_nn(pooled, w_ref[...])
            yp_ref[pl.ds(r0, RC), :] = yp
            y_ref[pl.ds(r0, RC), :] = yp * s_ref[...]
        _chunks(T, step)

    slab = pl.BlockSpec((T, SLAB), lambda c: (0, c))
    return pl.pallas_call(
        body, name=name, grid=(NSLAB,),
        in_specs=[slab, pl.BlockSpec((None, SLAB, SLAB), lambda c: (c, 0, 0)), pl.BlockSpec((1, SLAB), lambda c: (0, c))],
        out_specs=[slab, slab, slab],
        out_shape=[jax.ShapeDtypeStruct((T, DM), BF16), jax.ShapeDtypeStruct((T, DM), F32),
                   jax.ShapeDtypeStruct((T, DM), F32)],
        scratch_shapes=[pltpu.VMEM((T + PAD, SLAB), F32)],
        compiler_params=_cp(("parallel",)))(hn, wg, sc)


def pool_bwd(dy, ypre, pooled, wg, sc, name):
    T = dy.shape[0]

    def body(d_ref, yp_ref, p_ref, w_ref, s_ref, dh_ref, dw_ref, ds_ref, qpad, dwacc, dsacc):
        g = pl.program_id(0)
        wf = jnp.left_shift(2, g).astype(F32)
        dwacc[...] = jnp.zeros_like(dwacc)
        dsacc[...] = jnp.zeros_like(dsacc)
        _zero_rows(qpad, T, PAD)

        def first(r0):
            d = d_ref[pl.ds(r0, RC), :]
            dsacc[...] += jnp.sum((d * yp_ref[pl.ds(r0, RC), :]).reshape(RC // 8, 8, SLAB), axis=0)
            dyp = (d * s_ref[...]).astype(BF16)
            dpool = _dot_nt(dyp, w_ref[...])
            dwacc[...] += _dot_tn(p_ref[pl.ds(r0, RC), :], dyp)
            t1 = (r0 + 1 + lax.broadcasted_iota(jnp.int32, (RC, SLAB), 0)).astype(F32)
            qpad[pl.ds(r0, RC), :] = dpool / jnp.minimum(t1, wf)
            dh_ref[pl.ds(r0, RC), :] = dpool
        _chunks(T, first)

        def second(r0):
            w = qpad[pl.ds(r0, RC + 16), :]
            n = RC + 16
            a2 = w + pltpu.roll(w, n - 1, 0)
            a4 = a2 + pltpu.roll(a2, n - 2, 0)
            a8 = a4 + pltpu.roll(a4, n - 4, 0)
            a16 = a8 + pltpu.roll(a8, n - 8, 0)
            sel = jnp.where(g == 0, a2, jnp.where(g == 1, a4, jnp.where(g == 2, a8, a16)))[:RC, :]
            dh_ref[pl.ds(r0, RC), :] = sel - dh_ref[pl.ds(r0, RC), :]
        _chunks(T, second)
        dw_ref[...] = dwacc[...]
        ds_ref[...] = jnp.sum(dsacc[...], axis=0, keepdims=True)

    slab = pl.BlockSpec((T, SLAB), lambda c: (0, c))
    wspec = pl.BlockSpec((None, SLAB, SLAB), lambda c: (c, 0, 0))
    vec = pl.BlockSpec((1, SLAB), lambda c: (0, c))
    return pl.pallas_call(
        body, name=name, grid=(NSLAB,),
        in_specs=[slab, slab, slab, wspec, vec],
        out_specs=[slab, wspec, vec],
        out_shape=[jax.ShapeDtypeStruct((T, DM), F32), jax.ShapeDtypeStruct((NSLAB, SLAB, SLAB), F32),
                   jax.ShapeDtypeStruct((1, DM), F32)],
        scratch_shapes=[pltpu.VMEM((T + PAD, SLAB), F32), pltpu.VMEM((SLAB, SLAB), F32), pltpu.VMEM((8, SLAB), F32)],
        compiler_params=_cp(("parallel",)))(dy, ypre, pooled, wg, sc)


KC = 31
KD = 3


def conf_conv_fwd(ag, wdw, bdw, name):
    T = ag.shape[0]

    def body(a_ref, g_ref, w_ref, b_ref, o_ref, zpad):
        _zero_rows(zpad, 0, PAD)

        def fill(r0):
            a = a_ref[pl.ds(r0, RC), :].astype(F32)
            gt = g_ref[pl.ds(r0, RC), :].astype(F32)
            zpad[pl.ds(r0 + PAD, RC), :] = a * _sig(gt)
        _chunks(T, fill)
        wrow = lambda j: w_ref[KC - 1 - j:KC - j, :]

        def step(r0):
            o_ref[pl.ds(r0, RC), :] = _causal_conv(zpad, wrow, KC, r0) + b_ref[...]
        _chunks(T, step)

    vec = pl.BlockSpec((1, SLAB), lambda c: (0, c))
    return pl.pallas_call(
        body, name=name, grid=(NSLAB,),
        in_specs=[_col(T, 0), _col(T, 1), pl.BlockSpec((32, SLAB), lambda c: (0, c)), vec],
        out_specs=pl.BlockSpec((T, SLAB), lambda c: (0, c)),
        out_shape=jax.ShapeDtypeStruct((T, DM), F32),
        scratch_shapes=[pltpu.VMEM((T + PAD, SLAB), F32)],
        compiler_params=_cp(("parallel",)))(ag, ag, wdw, bdw)


def conf_conv_bwd(dzc, ag, wdw, name):
    T = ag.shape[0]

    def body(d_ref, a_ref, g_ref, w_ref, da_ref, dg_ref, dw_ref, db_ref, zpad, gpad, dwacc, dbacc):
        _zero_rows(zpad, 0, PAD)
        _zero_rows(gpad, T, PAD)
        dwacc[...] = jnp.zeros_like(dwacc)
        dbacc[...] = jnp.zeros_like(dbacc)

        def fill(r0):
            a = a_ref[pl.ds(r0, RC), :].astype(F32)
            gt = g_ref[pl.ds(r0, RC), :].astype(F32)
            zpad[pl.ds(r0 + PAD, RC), :] = a * _sig(gt)
            d = d_ref[pl.ds(r0, RC), :]
            gpad[pl.ds(r0, RC), :] = d
            dbacc[...] += jnp.sum(d.reshape(RC // 8, 8, SLAB), axis=0)
        _chunks(T, fill)
        wrow = lambda j: w_ref[KC - 1 - j:KC - j, :]

        def step(r0):
            dz = _anticausal_conv(gpad, wrow, KC, r0)
            a = a_ref[pl.ds(r0, RC), :].astype(F32)
            s = _sig(g_ref[pl.ds(r0, RC), :].astype(F32))
            da_ref[pl.ds(r0, RC), :] = (dz * s).astype(da_ref.dtype)
            dg_ref[pl.ds(r0, RC), :] = (dz * a * s * (1.0 - s)).astype(dg_ref.dtype)
            _conv_dw(gpad, zpad, dwacc, KC, r0)
        _chunks(T, step)
        dw_ref[...] = jnp.zeros_like(dw_ref)
        for k in range(KC):
            dw_ref[k:k + 1, :] = jnp.sum(dwacc[KC - 1 - k], axis=0, keepdims=True)
        db_ref[...] = jnp.sum(dbacc[...], axis=0, keepdims=True)

    vec = pl.BlockSpec((1, SLAB), lambda c: (0, c))
    w32 = pl.BlockSpec((32, SLAB), lambda c: (0, c))
    return pl.pallas_call(
        body, name=name, grid=(NSLAB,),
        in_specs=[pl.BlockSpec((T, SLAB), lambda c: (0, c)), _col(T, 0), _col(T, 1), w32],
        out_specs=[_col(T, 0), _col(T, 0), w32, vec],
        out_shape=[jax.ShapeDtypeStruct((T, DM), BF16), jax.ShapeDtypeStruct((T, DM), BF16),
                   jax.ShapeDtypeStruct((32, DM), F32), jax.ShapeDtypeStruct((1, DM), F32)],
        scratch_shapes=[pltpu.VMEM((T + PAD, SLAB), F32), pltpu.VMEM((T + PAD, SLAB), F32),
                        pltpu.VMEM((32, 8, SLAB), F32), pltpu.VMEM((8, SLAB), F32)],
        compiler_params=_cp(("parallel",)))(dzc, ag, ag, wdw)


def conf_ln_fwd(zc, g, b, name):
    T, C = zc.shape
    tm = min(512, T)

    def body(z_ref, g_ref, b_ref, o_ref):
        x = z_ref[...]
        xc = x - jnp.mean(x, axis=-1, keepdims=True)
        r = lax.rsqrt(jnp.mean(xc * xc, axis=-1, keepdims=True) + EPS)
        zl = xc * r * g_ref[...] + b_ref[...]
        o_ref[...] = (zl * _sig(zl)).astype(o_ref.dtype)

    return pl.pallas_call(
        body, name=name, grid=(T // tm,), in_specs=[_rows(tm, C), _vec(C), _vec(C)], out_specs=_rows(tm, C),
        out_shape=jax.ShapeDtypeStruct((T, C), BF16), compiler_params=_cp(("parallel",)))(zc, g, b)


def conf_ln_bwd(dzs, zc, g, b, name):
    T, C = zc.shape
    tm = min(512, T)

    def body(d_ref, z_ref, g_ref, b_ref, o_ref, dg_ref, db_ref):
        i = pl.program_id(0)
        x = z_ref[...]
        xc = x - jnp.mean(x, axis=-1, keepdims=True)
        r = lax.rsqrt(jnp.mean(xc * xc, axis=-1, keepdims=True) + EPS)
        xh = xc * r
        zl = xh * g_ref[...] + b_ref[...]
        s = _sig(zl)
        dzl = d_ref[...].astype(F32) * (s * (1.0 + zl * (1.0 - s)))
        t = dzl * g_ref[...]
        o_ref[...] = r * (t - jnp.mean(t, axis=-1, keepdims=True) - xh * jnp.mean(t * xh, axis=-1, keepdims=True))
        _acc_rows(dg_ref, i, jnp.sum(dzl * xh, axis=0, keepdims=True))
        _acc_rows(db_ref, i, jnp.sum(dzl, axis=0, keepdims=True))

    return pl.pallas_call(
        body, name=name, grid=(T // tm,), in_specs=[_rows(tm, C), _rows(tm, C), _vec(C), _vec(C)],
        out_specs=[_rows(tm, C), _vec(C), _vec(C)],
        out_shape=[jax.ShapeDtypeStruct((T, C), F32), jax.ShapeDtypeStruct((1, C), F32),
                   jax.ShapeDtypeStruct((1, C), F32)],
        compiler_params=_cp(("arbitrary",)))(dzs, zc, g, b)


def sconv_fwd(bgx, wc, name):
    T = bgx.shape[0]

    def body(b_ref, c_ref, x_ref, w_ref, o_ref, zpad):
        _zero_rows(zpad, 0, PAD)

        def fill(r0):
            zpad[pl.ds(r0 + PAD, RC), :] = c_ref[pl.ds(r0, RC), :].astype(F32) * x_ref[pl.ds(r0, RC), :].astype(F32)
        _chunks(T, fill)
        wrow = lambda j: w_ref[KD - 1 - j:KD - j, :]

        def step(r0):
            qc = _causal_conv(zpad, wrow, KD, r0)
            o_ref[pl.ds(r0, RC), :] = (b_ref[pl.ds(r0, RC), :].astype(F32) * qc).astype(o_ref.dtype)
        _chunks(T, step)

    return pl.pallas_call(
        body, name=name, grid=(NSLAB,),
        in_specs=[_col(T, 0), _col(T, 1), _col(T, 2), pl.BlockSpec((8, SLAB), lambda c: (0, c))],
        out_specs=pl.BlockSpec((T, SLAB), lambda c: (0, c)),
        out_shape=jax.ShapeDtypeStruct((T, DM), BF16),
        scratch_shapes=[pltpu.VMEM((T + PAD, SLAB), F32)],
        compiler_params=_cp(("parallel",)))(bgx, bgx, bgx, wc)


def sconv_bwd(dy, bgx, wc, name):
    T = bgx.shape[0]

    def body(d_ref, b_ref, c_ref, x_ref, w_ref, db_ref, dc_ref, dx_ref, dw_ref, zpad, gpad, dwacc):
        _zero_rows(zpad, 0, PAD)
        _zero_rows(gpad, T, PAD)
        dwacc[...] = jnp.zeros_like(dwacc)

        def fill(r0):
            zpad[pl.ds(r0 + PAD, RC), :] = c_ref[pl.ds(r0, RC), :].astype(F32) * x_ref[pl.ds(r0, RC), :].astype(F32)
            gpad[pl.ds(r0, RC), :] = d_ref[pl.ds(r0, RC), :].astype(F32) * b_ref[pl.ds(r0, RC), :].astype(F32)
        _chunks(T, fill)
        wrow = lambda j: w_ref[KD - 1 - j:KD - j, :]

        def step(r0):
            qc = _causal_conv(zpad, wrow, KD, r0)
            db_ref[pl.ds(r0, RC), :] = (d_ref[pl.ds(r0, RC), :].astype(F32) * qc).astype(db_ref.dtype)
            dq = _anticausal_conv(gpad, wrow, KD, r0)
            dc_ref[pl.ds(r0, RC), :] = (dq * x_ref[pl.ds(r0, RC), :].astype(F32)).astype(dc_ref.dtype)
            dx_ref[pl.ds(r0, RC), :] = (dq * c_ref[pl.ds(r0, RC), :].astype(F32)).astype(dx_ref.dtype)
            _conv_dw(gpad, zpad, dwacc, KD, r0)
        _chunks(T, step)
        dw_ref[...] = jnp.zeros_like(dw_ref)
        for k in range(KD):
            dw_ref[k:k + 1, :] = jnp.sum(dwacc[KD - 1 - k], axis=0, keepdims=True)

    w8 = pl.BlockSpec((8, SLAB), lambda c: (0, c))
    return pl.pallas_call(
        body, name=name, grid=(NSLAB,),
        in_specs=[pl.BlockSpec((T, SLAB), lambda c: (0, c)), _col(T, 0), _col(T, 1), _col(T, 2), w8],
        out_specs=[_col(T, 0), _col(T, 0), _col(T, 0), w8],
        out_shape=[jax.ShapeDtypeStruct((T, DM), BF16)] * 3 + [jax.ShapeDtypeStruct((8, DM), F32)],
        scratch_shapes=[pltpu.VMEM((T + PAD, SLAB), F32), pltpu.VMEM((T + PAD, SLAB), F32),
                        pltpu.VMEM((8, 8, SLAB), F32)],
        compiler_params=_cp(("parallel",)))(dy, bgx, bgx, bgx, wc)


def merge_cols(parts, name):
    T = parts[0].shape[0]
    n = len(parts)
    C = n * DM
    tm = min(512, T)

    def body(*refs):
        o_ref = refs[n]
        for j in range(n):
            o_ref[:, j * DM:(j + 1) * DM] = refs[j][...]

    return pl.pallas_call(
        body, name=name, grid=(T // tm,),
        in_specs=[_rows(tm, DM) for j in range(n)],
        out_specs=_rows(tm, C), out_shape=jax.ShapeDtypeStruct((T, C), parts[0].dtype),
        compiler_params=_cp(("parallel",)))(*parts)


def adamw(w, g, m, v, name):
    shape = w.shape
    R, C = shape[-2], shape[-1]
    L = w.size // (R * C)
    w2, g2, m2, v2 = (a.reshape(L, R, C) for a in (w, g, m, v))
    tr = R
    while tr * C > 512 * 1024 and tr % 16 == 0:
        tr //= 2
    bc1 = 1.0 - ADAM_B1 ** ADAM_STEP
    bc2 = 1.0 - ADAM_B2 ** ADAM_STEP

    def body(w_ref, g_ref, m_ref, v_ref, d_ref, nm_ref, nv_ref):
        gg = g_ref[...]
        nm = ADAM_B1 * m_ref[...] + (1.0 - ADAM_B1) * gg
        nv = ADAM_B2 * v_ref[...] + (1.0 - ADAM_B2) * (gg * gg)
        nm_ref[...] = nm
        nv_ref[...] = nv
        d_ref[...] = -ADAM_LR * ((nm / bc1) / (jnp.sqrt(nv / bc2) + ADAM_EPS) + ADAM_WD * w_ref[...])

    spec = pl.BlockSpec((None, tr, C), lambda l, i: (l, i, 0))
    outs = pl.pallas_call(
        body, name=name, grid=(L, R // tr), in_specs=[spec] * 4, out_specs=[spec] * 3,
        out_shape=[jax.ShapeDtypeStruct((L, R, C), F32)] * 3,
        compiler_params=_cp(("parallel", "parallel")))(w2, g2, m2, v2)
    return tuple(o.reshape(shape) for o in outs)


def _place():
    x, y, c = lax.axis_index("x"), lax.axis_index("y"), lax.axis_index("c")
    return x, y, c


def all_gather_chips(bufs, name):
    n = len(bufs)
    me_ = 2 * lax.axis_index("x") + lax.axis_index("y")
    slots = [lax.dynamic_update_slice(lax.empty((NCHIP,) + b.shape, b.dtype), b[None], (me_, 0, 0)) for b in bufs]

    def body(*refs):
        dst = refs[n:2 * n]
        send, recv = refs[2 * n:]
        x, y, c = _place()
        me = 2 * x + y
        sib = (x, y, 1 - c)
        chips = [(1 - x, y), (x, 1 - y), (1 - x, 1 - y)]

        def half(b, slot, hc):
            rows = bufs[b].shape[0] // 2
            return dst[b].at[slot, pl.ds(hc * rows, rows), :]

        def remote(k, s, d, to):
            return pltpu.make_async_remote_copy(src_ref=s, dst_ref=d, send_sem=send.at[k], recv_sem=recv.at[k],
                                                device_id=to, device_id_type=MESH)

        first = []
        for b in range(n):
            for j, (cx, cy) in enumerate(chips):
                first.append(remote(b * 6 + j, half(b, me, c), half(b, me, c), (cx, cy, c)))
        for cp in first:
            cp.start()
        passed = []
        for b in range(n):
            for j, (cx, cy) in enumerate(chips):
                slot = 2 * cx + cy
                remote(b * 6 + j, half(b, slot, c), half(b, slot, c), (cx, cy, c)).wait_recv()
                fwd = remote(b * 6 + 3 + j, half(b, slot, c), half(b, slot, c), sib)
                fwd.start()
                passed.append(fwd)
        for b in range(n):
            for j, (cx, cy) in enumerate(chips):
                slot = 2 * cx + cy
                remote(b * 6 + 3 + j, half(b, slot, 1 - c), half(b, slot, 1 - c), sib).wait_recv()
        for cp in first + passed:
            cp.wait_send()

    return pl.pallas_call(
        body, name=name, in_specs=[ANY] * n, out_specs=[ANY] * n,
        out_shape=[jax.ShapeDtypeStruct(s.shape, s.dtype) for s in slots],
        input_output_aliases={b: b for b in range(n)},
        scratch_shapes=[pltpu.SemaphoreType.DMA((6 * n,)), pltpu.SemaphoreType.DMA((6 * n,))],
        compiler_params=pltpu.CompilerParams())(*slots)


def pair_exchange(bufs, name):
    n = len(bufs)

    def body(*refs):
        src, dst = refs[:n], refs[n:2 * n]
        send, recv = refs[2 * n:]
        x, y, c = _place()
        cps = []
        for b in range(n):
            rows = bufs[b].shape[1] // 2
            cp = pltpu.make_async_remote_copy(
                src_ref=src[b].at[:, pl.ds((1 - c) * rows, rows), :], dst_ref=dst[b],
                send_sem=send.at[b], recv_sem=recv.at[b], device_id=(x, y, 1 - c), device_id_type=MESH)
            cp.start()
            cps.append(cp)
        for cp in cps:
            cp.wait()

    return pl.pallas_call(
        body, name=name, in_specs=[ANY] * n, out_specs=[ANY] * n,
        out_shape=[jax.ShapeDtypeStruct((NCHIP, b.shape[1] // 2, b.shape[2]), b.dtype) for b in bufs],
        scratch_shapes=[pltpu.SemaphoreType.DMA((n,)), pltpu.SemaphoreType.DMA((n,))],
        compiler_params=pltpu.CompilerParams())(*bufs)


def add_half(full, got, tr, tc, name):
    _, R, C = full.shape
    rows = R // 2
    nr = rows // tr
    c_arr = lax.axis_index("c").astype(jnp.int32).reshape(1)

    def body(c_ref, a_ref, b_ref, o_ref):
        o_ref[...] = (a_ref[...].astype(F32) + b_ref[...].astype(F32)).astype(o_ref.dtype)

    return pl.pallas_call(
        body, name=name,
        grid_spec=pltpu.PrefetchScalarGridSpec(
            num_scalar_prefetch=1, grid=(NCHIP, nr, C // tc),
            in_specs=[pl.BlockSpec((None, tr, tc), lambda s, i, j, c_ref: (s, c_ref[0] * nr + i, j)),
                      pl.BlockSpec((None, tr, tc), lambda s, i, j, c_ref: (s, i, j))],
            out_specs=pl.BlockSpec((None, tr, tc), lambda s, i, j, c_ref: (s, i, j))),
        out_shape=jax.ShapeDtypeStruct((NCHIP, rows, C), full.dtype),
        compiler_params=_cp(("parallel", "parallel", "parallel")))(c_arr, full, got)


def chip_exchange(bufs, name):
    n = len(bufs)

    def body(*refs):
        src, dst = refs[:n], refs[n:2 * n]
        send, recv, lsem = refs[2 * n:]
        x, y, c = _place()
        me = 2 * x + y
        chips = [(1 - x, y), (x, 1 - y), (1 - x, 1 - y)]
        local = [pltpu.make_async_copy(src[b].at[me], dst[b].at[me], lsem.at[b]) for b in range(n)]
        for cp in local:
            cp.start()
        cps = []
        for b in range(n):
            for j, (cx, cy) in enumerate(chips):
                cp = pltpu.make_async_remote_copy(
                    src_ref=src[b].at[2 * cx + cy], dst_ref=dst[b].at[me],
                    send_sem=send.at[b * 3 + j], recv_sem=recv.at[b * 3 + j],
                    device_id=(cx, cy, c), device_id_type=MESH)
                cp.start()
                cps.append((cp, b, cx, cy, j))
        for cp, b, cx, cy, j in cps:
            cp.wait_send()
            pltpu.make_async_remote_copy(
                src_ref=src[b].at[me], dst_ref=dst[b].at[2 * cx + cy],
                send_sem=send.at[b * 3 + j], recv_sem=recv.at[b * 3 + j],
                device_id=(cx, cy, c), device_id_type=MESH).wait_recv()
        for cp in local:
            cp.wait()

    return pl.pallas_call(
        body, name=name, in_specs=[ANY] * n, out_specs=[ANY] * n,
        out_shape=[jax.ShapeDtypeStruct(b.shape, b.dtype) for b in bufs],
        scratch_shapes=[pltpu.SemaphoreType.DMA((3 * n,)), pltpu.SemaphoreType.DMA((3 * n,)),
                        pltpu.SemaphoreType.DMA((n,))],
        compiler_params=pltpu.CompilerParams())(*bufs)


def sum_slots(buf, tr, tc, name):
    _, r, C = buf.shape
    nr = r // tr
    c_arr = lax.axis_index("c").astype(jnp.int32).reshape(1)

    def body(c_ref, a_ref, o_ref):
        o_ref[...] = ((a_ref[0].astype(F32) + a_ref[1].astype(F32)) + a_ref[2].astype(F32)) + a_ref[3].astype(F32)

    return pl.pallas_call(
        body, name=name,
        grid_spec=pltpu.PrefetchScalarGridSpec(
            num_scalar_prefetch=1, grid=(nr, C // tc),
            in_specs=[pl.BlockSpec((NCHIP, tr, tc), lambda i, j, c_ref: (0, i, j))],
            out_specs=pl.BlockSpec((tr, tc), lambda i, j, c_ref: (c_ref[0] * nr + i, j))),
        out_shape=jax.ShapeDtypeStruct((2 * r, C), F32),
        compiler_params=_cp(("parallel", "parallel")))(c_arr, buf)


def pair_share(bufs, name):
    n = len(bufs)

    def body(*refs):
        dst = refs[n:2 * n]
        send, recv = refs[2 * n:]
        x, y, c = _place()
        cps = []
        for b in range(n):
            rows = bufs[b].shape[0] // 2
            here = dst[b].at[pl.ds(c * rows, rows), :]
            cp = pltpu.make_async_remote_copy(src_ref=here, dst_ref=here, send_sem=send.at[b], recv_sem=recv.at[b],
                                              device_id=(x, y, 1 - c), device_id_type=MESH)
            cp.start()
            cps.append((cp, b))
        for cp, b in cps:
            rows = bufs[b].shape[0] // 2
            there = dst[b].at[pl.ds((1 - c) * rows, rows), :]
            cp.wait_send()
            pltpu.make_async_remote_copy(src_ref=there, dst_ref=there, send_sem=send.at[b], recv_sem=recv.at[b],
                                         device_id=(x, y, 1 - c), device_id_type=MESH).wait_recv()

    return pl.pallas_call(
        body, name=name, in_specs=[ANY] * n, out_specs=[ANY] * n,
        out_shape=[jax.ShapeDtypeStruct(b.shape, b.dtype) for b in bufs],
        input_output_aliases={b: b for b in range(n)},
        scratch_shapes=[pltpu.SemaphoreType.DMA((n,)), pltpu.SemaphoreType.DMA((n,))],
        compiler_params=pltpu.CompilerParams())(*bufs)


def _tile(kind, buf):
    return {"A": (256, buf.shape[2]), "B": (buf.shape[1], 1024), "C": (128, 256), "V": (40, 256),
            "E": (40, 1024)}[kind]


def reduce_scatter(parts, tag):
    names = list(parts)
    got = pair_exchange([parts[k] for k in names], tag + "_pair_exchange")
    sums = [add_half(parts[k], got[i], *_tile(k[0], got[i]), name=tag + "_add_pair_" + k) for i, k in enumerate(names)]
    landed = chip_exchange(sums, tag + "_chip_exchange")
    halves = [sum_slots(landed[i], *_tile(k[0], landed[i]), name=tag + "_sum_chips_" + k) for i, k in enumerate(names)]
    full = pair_share(halves, tag + "_pair_share")
    return dict(zip(names, full))


HBM = pl.BlockSpec(memory_space=pltpu.HBM)
SEMS = pl.BlockSpec(memory_space=pltpu.SEMAPHORE)
FLOWS = pltpu.SideEffectType.DATAFLOW_SIDE_EFFECTING


def _in_hbm(a):
    return pltpu.with_memory_space_constraint(a, pltpu.HBM)


def _other_chips():
    x, y, c = _place()
    return 2 * x + y, c, [(1 - x, y), (x, 1 - y), (1 - x, 1 - y)]


def own_slots(bufs):
    me = 2 * lax.axis_index("x") + lax.axis_index("y")
    return [lax.dynamic_update_slice(lax.empty((NCHIP,) + b.shape, b.dtype), b[None], (me, 0, 0)) for b in bufs]


def gather_start(slots, after, name):
    n = len(slots)

    def body(*refs):
        ins = refs[:n]
        send, recv = refs[n + 1], refs[n + 2]
        token = refs[2 * n + 3]
        me, c, chips = _other_chips()
        for b in range(n):
            rows = slots[b].shape[1] // 2
            own = ins[b].at[me, pl.ds(c * rows, rows), :]
            for j, (cx, cy) in enumerate(chips):
                pltpu.make_async_remote_copy(src_ref=own, dst_ref=own, send_sem=send.at[3 * b + j],
                                             recv_sem=recv.at[3 * b + j], device_id=(cx, cy, c),
                                             device_id_type=MESH).start()
        token[...] = jnp.zeros_like(token)

    out = pl.pallas_call(
        body, name=name, in_specs=[HBM] * n + [ANY],
        out_specs=[SEMS, SEMS] + [HBM] * n + [pl.BlockSpec(memory_space=pltpu.VMEM)],
        out_shape=[pltpu.SemaphoreType.DMA((3 * n,)), pltpu.SemaphoreType.DMA((3 * n,))]
        + [pltpu.HBM(s.shape, s.dtype) for s in slots] + [jax.ShapeDtypeStruct((8, 128), F32)],
        input_output_aliases={b: b + 2 for b in range(n)},
        compiler_params=pltpu.CompilerParams(has_side_effects=FLOWS))(*[_in_hbm(s) for s in slots], after)
    return out[0], out[1], list(out[2:2 + n]), out[2 + n]


def gather_wait(send, recv, slots, picks, after, name):
    n = len(slots)

    def body(*refs):
        ins = refs[:n]
        send_, recv_ = refs[n], refs[n + 1]
        me, c, chips = _other_chips()
        for i, b in enumerate(picks):
            rows = slots[i].shape[1] // 2
            own = ins[i].at[me, pl.ds(c * rows, rows), :]
            for j, (cx, cy) in enumerate(chips):
                got = ins[i].at[2 * cx + cy, pl.ds(c * rows, rows), :]
                pltpu.make_async_remote_copy(src_ref=own, dst_ref=own, send_sem=send_.at[3 * b + j],
                                             recv_sem=recv_.at[3 * b + j], device_id=(cx, cy, c),
                                             device_id_type=MESH).wait_send()
                pltpu.make_async_remote_copy(src_ref=got, dst_ref=got, send_sem=send_.at[3 * b + j],
                                             recv_sem=recv_.at[3 * b + j], device_id=(cx, cy, c),
                                             device_id_type=MESH).wait_recv()

    return pl.pallas_call(
        body, name=name, in_specs=[HBM] * n + [SEMS, SEMS, ANY], out_specs=[HBM] * n,
        out_shape=[pltpu.HBM(s.shape, s.dtype) for s in slots],
        input_output_aliases={b: b for b in range(n)},
        compiler_params=pltpu.CompilerParams(has_side_effects=FLOWS))(*slots, send, recv, after)


def gather_pass(slots, name):
    n = len(slots)

    def body(*refs):
        dst = refs[n:2 * n]
        send, recv = refs[2 * n:]
        x, y, c = _place()
        sib = (x, y, 1 - c)
        chips = [(1 - x, y), (x, 1 - y), (1 - x, 1 - y)]

        def half(b, slot, hc):
            rows = slots[b].shape[1] // 2
            return dst[b].at[slot, pl.ds(hc * rows, rows), :]

        passed = []
        for b in range(n):
            for j, (cx, cy) in enumerate(chips):
                slot = 2 * cx + cy
                cp = pltpu.make_async_remote_copy(src_ref=half(b, slot, c), dst_ref=half(b, slot, c),
                                                  send_sem=send.at[3 * b + j], recv_sem=recv.at[3 * b + j],
                                                  device_id=sib, device_id_type=MESH)
                cp.start()
                passed.append(cp)
        for b in range(n):
            for j, (cx, cy) in enumerate(chips):
                slot = 2 * cx + cy
                pltpu.make_async_remote_copy(src_ref=half(b, slot, 1 - c), dst_ref=half(b, slot, 1 - c),
                                             send_sem=send.at[3 * b + j], recv_sem=recv.at[3 * b + j],
                                             device_id=sib, device_id_type=MESH).wait_recv()
        for cp in passed:
            cp.wait_send()

    return pl.pallas_call(
        body, name=name, in_specs=[ANY] * n, out_specs=[ANY] * n,
        out_shape=[jax.ShapeDtypeStruct(s.shape, s.dtype) for s in slots],
        input_output_aliases={b: b for b in range(n)},
        scratch_shapes=[pltpu.SemaphoreType.DMA((3 * n,)), pltpu.SemaphoreType.DMA((3 * n,))],
        compiler_params=pltpu.CompilerParams())(*slots)


def chip_exchange_start(sums, name):
    n = len(sums)
    me_ = 2 * lax.axis_index("x") + lax.axis_index("y")
    landing = [lax.dynamic_update_slice(lax.empty(s.shape, s.dtype),
                                        lax.dynamic_slice(s, (me_, 0, 0), (1,) + s.shape[1:]), (me_, 0, 0)) for s in sums]

    def body(*refs):
        src, land = refs[:n], refs[n:2 * n]
        send, recv = refs[2 * n], refs[2 * n + 1]
        token = refs[4 * n + 2]
        me, c, chips = _other_chips()
        for b in range(n):
            for j, (cx, cy) in enumerate(chips):
                pltpu.make_async_remote_copy(src_ref=src[b].at[2 * cx + cy], dst_ref=land[b].at[me],
                                             send_sem=send.at[3 * b + j], recv_sem=recv.at[3 * b + j],
                                             device_id=(cx, cy, c), device_id_type=MESH).start()
        token[...] = jnp.zeros_like(token)

    out = pl.pallas_call(
        body, name=name, in_specs=[HBM] * (2 * n),
        out_specs=[SEMS, SEMS] + [HBM] * (2 * n) + [pl.BlockSpec(memory_space=pltpu.VMEM)],
        out_shape=[pltpu.SemaphoreType.DMA((3 * n,)), pltpu.SemaphoreType.DMA((3 * n,))]
        + [pltpu.HBM(s.shape, s.dtype) for s in sums + landing] + [jax.ShapeDtypeStruct((8, 128), F32)],
        input_output_aliases={b: b + 2 for b in range(2 * n)},
        compiler_params=pltpu.CompilerParams(has_side_effects=FLOWS))(*[_in_hbm(s) for s in sums + landing])
    return out[0], out[1], list(out[2:2 + n]), list(out[2 + n:2 + 2 * n]), out[2 + 2 * n]


def chip_exchange_wait(send, recv, sums, landing, after, name):
    n = len(sums)

    def body(*refs):
        src, land = refs[:n], refs[n:2 * n]
        send_, recv_ = refs[2 * n], refs[2 * n + 1]
        me, c, chips = _other_chips()
        for b in range(n):
            for j, (cx, cy) in enumerate(chips):
                slot = 2 * cx + cy
                pltpu.make_async_remote_copy(src_ref=src[b].at[slot], dst_ref=land[b].at[me],
                                             send_sem=send_.at[3 * b + j], recv_sem=recv_.at[3 * b + j],
                                             device_id=(cx, cy, c), device_id_type=MESH).wait_send()
                pltpu.make_async_remote_copy(src_ref=src[b].at[me], dst_ref=land[b].at[slot],
                                             send_sem=send_.at[3 * b + j], recv_sem=recv_.at[3 * b + j],
                                             device_id=(cx, cy, c), device_id_type=MESH).wait_recv()

    out = pl.pallas_call(
        body, name=name, in_specs=[HBM] * (2 * n) + [SEMS, SEMS, ANY], out_specs=[HBM] * (2 * n),
        out_shape=[pltpu.HBM(s.shape, s.dtype) for s in sums + landing],
        input_output_aliases={b: b for b in range(2 * n)},
        compiler_params=pltpu.CompilerParams(has_side_effects=FLOWS))(*sums, *landing, send, recv, after)
    return list(out[n:])


def _row(a, l):
    return a[l:l + 1]


def local_step(x, p, tgt, small, weights_of, vecs, a_ws, a_bs, grads_ready):
    T = x.shape[0]
    bsf = jnp.broadcast_to(a_bs[:, :, None], (AG_N, CHUNK, AGW))
    vrow = lambda r: vecs[r:r + 1]
    saved = []
    W = []
    h = x
    GA1 = GB1 = GA = GB = GC = bgrp = None

    def ff_fwd(h, xn, l, which, post, g_next, tok=None):
        wa, wb = (GA1, GB1) if which == 1 else (GA, GB)
        tag = "ff%d_l%d" % (which, l)
        gu, a = ff_gateup(xn, wa, 0, tag + "_gateup")
        gp = _row(post, l) if tok is None else _row(post, l) + tok
        out = mm_rs_post(a, wb, 0, FB, FB, h, gp, 0.5, tag + "_down", g_next=g_next)
        return out[1], (out[2] if g_next is not None else None), (h, xn, gu, a, out[0])

    xn = rms_fwd(h, _row(small["ff1_pre_g"], 0), BF16, "ff1_l0_pre")
    for l in range(4):
        rec = {}
        GA1, GB1, atok = weights_of(l, "a", h)
        g_mix = _row(small["mix_pre_g"], l) if l >= 2 else None
        h, hn, rec["ff1"] = ff_fwd(h, xn, l, 1, small["ff1_post_g"], g_mix, atok)
        GA, GB, GC, wtok = weights_of(l, "b", h)
        W.append((GA1, GB1, GA, GB, GC))
        if l == 1:
            bgrp = GC[:, C_BGRP:C_BGRP + 256, :].reshape(NCHIP, 4, 64, 256).transpose(1, 0, 2, 3).reshape(4, 256, 256)
        tag = "mix_l%d" % l
        h_in = h
        g_ff2 = _row(small["ff2_pre_g"], l)
        if l == 1:
            hn = rms_fwd(h, _row(small["mix_pre_g"], l), F32, tag + "_pre")
            pooled, ypre, f = pool_fwd(hn, bgrp, vrow(V_BSCALE), tag + "_pool")
            rec["mix"] = (h_in, pooled, ypre, f)
            h = post_res(h, f, _row(small["mix_post_g"], l), 1.0, tag + "_post")
            xn = rms_fwd(h, g_ff2, BF16, "ff2_l1_pre")
        else:
            gpost = _row(small["mix_post_g"], l)
            if l == 0:
                g0 = _row(small["mix_pre_g"], l)
                hn = rms_fwd(h, g0 if wtok is None else g0 + wtok, BF16, tag + "_pre")
                zpre = mm_cs(hn, GA, A_AIN, 1536, 1536, BF16, tag + "_in")
                y = gmlp_mid_fwd(zpre, small["a_v_norm_g"], small["a_v_norm_b"], a_ws, bsf, tag + "_gate")
                f, h, xn = mm_rs_post(y, GB, B_AOUT, 768, 768, h, gpost, 1.0, tag + "_out", g_next=g_ff2)
                rec["mix"] = (h_in, hn, zpre, y, f)
            elif l == 2:
                ag = mm_cs(hn, GA, A_CIN, 512, 512, BF16, tag + "_pw1")
                zc = conf_conv_fwd(ag, vecs[V_CDW:V_CDW + 32], vrow(V_CBDW), tag + "_conv")
                zs = conf_ln_fwd(zc, vrow(V_CNG), vrow(V_CNB), tag + "_ln")
                f, h, xn = mm_rs_post(zs, GB, B_CPW2, 256, 256, h, gpost, 1.0, tag + "_pw2", g_next=g_ff2)
                rec["mix"] = (h_in, hn, ag, zc, zs, f)
            else:
                bgx = mm_cs(hn, GA, A_DIN, 768, 768, BF16, tag + "_in")
                y = sconv_fwd(bgx, vecs[V_DCONV:V_DCONV + 8], tag + "_conv")
                f, h, xn = mm_rs_post(y, GB, B_DOUT, 256, 256, h, gpost, 1.0, tag + "_out", g_next=g_ff2)
                rec["mix"] = (h_in, hn, bgx, y, f)
        h, xn, rec["ff2"] = ff_fwd(h, xn, l, 2, small["ff2_post_g"], _row(small["ple_gate_norm_g"], l))
        tag = "ple_l%d" % l
        pb = p[l].astype(BF16)
        h_in = h
        g_next = _row(small["ff1_pre_g"], l + 1) if l < 3 else None
        out = ple_fwd(h, xn, pb, GB, B_PLEG(l), GC, _row(small["ple_post_g"], l), tag, g_next=g_next)
        rec["ple"] = (h_in, xn, out[0], out[1], pb)
        h = out[2]
        xn = out[3] if l < 3 else None
        saved.append(rec)

    dh, loss_cols = loss_head(h, tgt, "loss_head")

    dA2 = dB2 = None
    layer_grads = [None] * 4
    tok = None
    gV = {}
    gains = {k: [None] * 4 for k in ("ff1_pre_g", "ff1_post_g", "mix_pre_g", "mix_post_g", "ff2_pre_g", "ff2_post_g",
                                      "ple_gate_norm_g", "ple_post_g")}
    extra = {}

    def ff_bwd(dh, l, which, pre, post, rec, after=None):
        wa, wb = (GA1, GB1) if which == 1 else (GA, GB)
        tag = "ff%d_l%d_b" % (which, l)
        h_in, xn, gu, a, f = rec
        gp = _row(post, l) if after is None else _row(post, l) + after
        df, dpost, dgu = ff_bwd_down(dh, f, gp, wb, 0, gu, tag + "_down")
        if which == 1:
            db = dw_rs(a, df, FB, FB, tag + "_dwdown")
        else:
            db = dw_rs(a, df, FB, FB, tag + "_dwdown", height=B2_ROWS(l), off=B_FF2D(l), into=dB2)
        dh_in, dpre = mm_cs_t_rms(dgu, wa, 0, 2 * FB, 2 * FB, h_in, _row(pre, l), dh, tag + "_gateup")
        da = dw_cs(xn, dgu, 2 * FB, 2 * FB, tag + "_dwgateup", width=None if which == 1 else A2_COLS(l))
        return dh_in, dpre, dpost, (da, db)

    for l in reversed(range(4)):
        rec = saved[l]
        GA1, GB1, GA, GB, GC = W[l]
        gC = {}
        tag = "ple_l%d_b" % l
        h_in, xn, zg, pe, pb = rec["ple"]
        gpost = _row(small["ple_post_g"], l)
        if tok is not None:
            gpost = gpost + tok
        dzg, dpe, dh, gains["ple_post_g"][l], gains["ple_gate_norm_g"][l] = ple_bwd(
            dh, zg, pe, h_in, GB, B_PLEG(l), gpost, _row(small["ple_gate_norm_g"], l), tag)
        gC[C_PROJ(l)] = dw_cs(pb, dpe, 256, 256, tag + "_dwproj")
        dB2 = dw_rs(xn, dzg, 256, 256, tag + "_dwgate", height=B2_ROWS(l), off=B_PLEG(l))

        dh, gains["ff2_pre_g"][l], gains["ff2_post_g"][l], (dA2, dB2) = ff_bwd(
            dh, l, 2, small["ff2_pre_g"], small["ff2_post_g"], rec["ff2"])

        tag = "mix_l%d_b" % l
        mix = rec["mix"]
        h_in, f = mix[0], mix[-1]
        if l == 1:
            _, pooled, ypre, _ = mix
            df, gains["mix_post_g"][l] = post_res_bwd(dh, f, _row(small["mix_post_g"], l), 1.0, F32, tag + "_post")
            dhn, dwg, dsc = pool_bwd(df, ypre, pooled, bgrp, vrow(V_BSCALE), tag + "_pool")
            gC[C_BGRP] = dwg.astype(BF16).reshape(4, NCHIP, 64, 256).transpose(1, 0, 2, 3).reshape(NCHIP, 256, 256)
            gV[V_BSCALE] = jnp.pad(dsc, ((0, 7), (0, 0)))
            dh, gains["mix_pre_g"][l] = rms_bwd(dhn, h_in, _row(small["mix_pre_g"], l), dh, tag + "_pre")
        else:
            gpre = _row(small["mix_pre_g"], l)
            df, gains["mix_post_g"][l] = post_res_bwd(dh, f, _row(small["mix_post_g"], l), 1.0, BF16, tag + "_post")
            if l == 0:
                _, hn, zpre, y, _ = mix
                dy = mm_rs_t(df, GB, B_AOUT, 768, 768, tag + "_out")
                dB2 = dw_rs(y, df, 768, 768, tag + "_dwout", height=B2_ROWS(l), off=B_AOUT, into=dB2)
                dz, dws, dbs, dvg, dvb = gmlp_mid_bwd(zpre, dy, small["a_v_norm_g"], small["a_v_norm_b"], a_ws, bsf,
                                                      tag + "_gate")
                extra.update(a_w_s=dws, a_b_s=dbs[:, 0, :], a_v_norm_g=dvg, a_v_norm_b=dvb)
                dA2 = dw_cs(hn, dz, 1536, 1536, tag + "_dwin", width=A2_COLS(l), off=A_AIN, into=dA2)
                dh, gains["mix_pre_g"][l] = mm_cs_t_rms(dz, GA, A_AIN, 1536, 1536, h_in, gpre, dh, tag + "_in")
            elif l == 2:
                _, hn, ag, zc, zs, _ = mix
                dzs = mm_rs_t(df, GB, B_CPW2, 256, 256, tag + "_pw2")
                dB2 = dw_rs(zs, df, 256, 256, tag + "_dwpw2", height=B2_ROWS(l), off=B_CPW2, into=dB2)
                dzc, dng, dnb = conf_ln_bwd(dzs, zc, vrow(V_CNG), vrow(V_CNB), tag + "_ln")
                da_, dg_, dwdw, dbdw = conf_conv_bwd(dzc, ag, vecs[V_CDW:V_CDW + 32], tag + "_conv")
                dag = merge_cols([da_, dg_], tag + "_merge")
                gV[V_CDW] = dwdw
                gV[V_CBDW] = jnp.pad(dbdw, ((0, 7), (0, 0)))
                gV[V_CNG] = jnp.pad(dng, ((0, 7), (0, 0)))
                gV[V_CNB] = jnp.pad(dnb, ((0, 7), (0, 0)))
                dA2 = dw_cs(hn, dag, 512, 512, tag + "_dwpw1", width=A2_COLS(l), off=A_CIN, into=dA2)
                dh, gains["mix_pre_g"][l] = mm_cs_t_rms(dag, GA, A_CIN, 512, 512, h_in, gpre, dh, tag + "_pw1")
            else:
                _, hn, bgx, y, _ = mix
                dy = mm_rs_t(df, GB, B_DOUT, 256, 256, tag + "_out")
                dB2 = dw_rs(y, df, 256, 256, tag + "_dwout", height=B2_ROWS(l), off=B_DOUT, into=dB2)
                db_, dc_, dx_, dwc = sconv_bwd(dy, bgx, vecs[V_DCONV:V_DCONV + 8], tag + "_conv")
                dbgx = merge_cols([db_, dc_, dx_], tag + "_merge")
                gV[V_DCONV] = dwc
                dA2 = dw_cs(hn, dbgx, 768, 768, tag + "_dwin", width=A2_COLS(l), off=A_DIN, into=dA2)
                dh, gains["mix_pre_g"][l] = mm_cs_t_rms(dbgx, GA, A_DIN, 768, 768, h_in, gpre, dh, tag + "_in")

        dC = jnp.concatenate([gC[C_PROJ(l)]] + ([gC[C_BGRP]] if l == 1 else []), axis=1)
        tok = grads_ready(l, "b", (dA2, dB2, dC), dh)
        dh, gains["ff1_pre_g"][l], gains["ff1_post_g"][l], (dA1, dB1) = ff_bwd(
            dh, l, 1, small["ff1_pre_g"], small["ff1_post_g"], rec["ff1"], after=tok)
        layer_grads[l] = (dA1, dB1, dA2, dB2, dC)
        tok = grads_ready(l, "a", (dA1, dB1), dh)

    return loss_cols, dh, layer_grads, gV, gains, extra


GAIN_NAMES = ("ff1_pre_g", "ff1_post_g", "mix_pre_g", "mix_post_g", "ff2_pre_g", "ff2_post_g", "ple_gate_norm_g",
              "ple_post_g")


def _pad_rows(a, rows):
    return jnp.pad(a, ((0, rows - a.shape[0]), (0, 0)))


def kernel(x, p, ff1_pre_g, ff1_w_gate, ff1_w_up, ff1_w_down, ff1_post_g, mix_pre_g, mix_post_g, ff2_pre_g, ff2_w_gate, ff2_w_up, ff2_w_down, ff2_post_g, ple_gate_norm_g, ple_w_gate, ple_w_proj, ple_post_g, a_w_in, a_v_norm_g, a_v_norm_b, a_w_s, a_b_s, a_w_out, b_w_grp, b_scale, c_w_pw1, c_w_dw, c_b_dw, c_norm_g, c_norm_b, c_w_pw2, d_w_in, d_w_conv, d_w_out, loss_target, m_ff1_pre_g, m_ff1_w_gate, m_ff1_w_up, m_ff1_w_down, m_ff1_post_g, m_mix_pre_g, m_mix_post_g, m_ff2_pre_g, m_ff2_w_gate, m_ff2_w_up, m_ff2_w_down, m_ff2_post_g, m_ple_gate_norm_g, m_ple_w_gate, m_ple_w_proj, m_ple_post_g, m_a_w_in, m_a_v_norm_g, m_a_v_norm_b, m_a_w_s, m_a_b_s, m_a_w_out, m_b_w_grp, m_b_scale, m_c_w_pw1, m_c_w_dw, m_c_b_dw, m_c_norm_g, m_c_norm_b, m_c_w_pw2, m_d_w_in, m_d_w_conv, m_d_w_out, v_ff1_pre_g, v_ff1_w_gate, v_ff1_w_up, v_ff1_w_down, v_ff1_post_g, v_mix_pre_g, v_mix_post_g, v_ff2_pre_g, v_ff2_w_gate, v_ff2_w_up, v_ff2_w_down, v_ff2_post_g, v_ple_gate_norm_g, v_ple_w_gate, v_ple_w_proj, v_ple_post_g, v_a_w_in, v_a_v_norm_g, v_a_v_norm_b, v_a_w_s, v_a_b_s, v_a_w_out, v_b_w_grp, v_b_scale, v_c_w_pw1, v_c_w_dw, v_c_b_dw, v_c_norm_g, v_c_norm_b, v_c_w_pw2, v_d_w_in, v_d_w_conv, v_d_w_out):
    args = dict(locals())
    wnames = ["ff1_pre_g", "ff1_w_gate", "ff1_w_up", "ff1_w_down", "ff1_post_g", "mix_pre_g", "mix_post_g",
              "ff2_pre_g", "ff2_w_gate", "ff2_w_up", "ff2_w_down", "ff2_post_g", "ple_gate_norm_g", "ple_w_gate",
              "ple_w_proj", "ple_post_g", "a_w_in", "a_v_norm_g", "a_v_norm_b", "a_w_s", "a_b_s", "a_w_out",
              "b_w_grp", "b_scale", "c_w_pw1", "c_w_dw", "c_b_dw", "c_norm_g", "c_norm_b", "c_w_pw2", "d_w_in",
              "d_w_conv", "d_w_out"]

    P = pack_weights(args)
    G0a = all_gather_chips([P[0][0], P[0][1], P[4]], "gather_l0a")
    vecs = G0a[2].transpose(1, 0, 2).reshape(V_ROWS, DM)
    flying = {}

    def start(key, bufs, after):
        send, recv, slots, token = gather_start(own_slots(list(bufs)), after, "gather_start_l" + key)
        flying[key] = (send, recv, slots)
        return token[0, 0]

    tok = start("0b", P[0][2:], G0a[0])
    arrived = {}

    def weights_of(l, part, h):
        if l == 0 and part == "a":
            return G0a[0], G0a[1], None
        key = "0b" if l == 0 else str(l)
        wtok = None
        if key not in arrived:
            send, recv, slots = flying[key]
            n = len(slots)
            landed = gather_wait(send, recv, slots, list(range(n)), h, "gather_wait_l" + key)
            arrived[key] = gather_pass(landed, "gather_pass_l" + key)
            if l < 3:
                wtok = start(str(l + 1), P[l + 1], arrived[key][0])
        got = arrived[key]
        if l == 0:
            return tuple(got) + (wtok,)
        return tuple(got[:2]) + (wtok,) if part == "a" else tuple(got[2:]) + (None,)

    pending = {}
    reduced = {}
    held = {}

    def finish(key, after):
        kinds, send, recv, sums, landing = pending.pop(key)
        landed = chip_exchange_wait(send, recv, sums, landing, after, "rs_wait_l" + key)
        halves = [sum_slots(landed[i], *_tile(k, landed[i]), name="rs_sum_chips_l%s_%d%s" % (key, i, k))
                  for i, k in enumerate(kinds)]
        reduced[key] = pair_share(halves, "rs_pair_share_l" + key)

    def grads_ready(l, part, bufs, dh):
        if part == "b" and l > 0:
            held[l] = list(bufs)
            return None
        if part == "a" and l > 0:
            key, kinds, parts = str(l), "ABABC", list(bufs) + held.pop(l)
        elif part == "b":
            key, kinds, parts = "0b", "ABC", list(bufs)
        else:
            finish("0b", dh)
            return None
        for other in list(pending):
            finish(other, dh)
        got = pair_exchange(parts, "rs_pair_exchange_l" + key)
        sums = [add_half(parts[i], got[i], *_tile(k, got[i]), name="rs_add_pair_l%s_%d%s" % (key, i, k))
                for i, k in enumerate(kinds)]
        send, recv, sums, landing, token = chip_exchange_start(sums, "rs_start_l" + key)
        pending[key] = (kinds, send, recv, sums, landing)
        return token[0, 0]

    small = {k: args[k] for k in GAIN_NAMES}
    small["ff1_pre_g"] = ff1_pre_g + tok
    small["a_v_norm_g"] = a_v_norm_g
    small["a_v_norm_b"] = a_v_norm_b
    loss_cols, grad_x, layer_grads, gV, gains, extra = local_step(
        x[0], p[:, 0], loss_target[0], small, weights_of, vecs, a_w_s[0], a_b_s[0], grads_ready)

    loss = lax.psum((0.5 / DM) * jnp.sum(loss_cols), ("x", "y", "c"))

    deltas, new_m, new_v = {}, {}, {}

    def update(k, g):
        if args[k].shape[-1] == FW:
            t = lambda a: jnp.swapaxes(a, 1, 2)
            outs = adamw(t(args[k]), t(g), t(args["m_" + k]), t(args["v_" + k]), "adamw_" + k)
            deltas[k], new_m[k], new_v[k] = (t(o) for o in outs)
        else:
            deltas[k], new_m[k], new_v[k] = adamw(args[k], g, args["m_" + k], args["v_" + k], "adamw_" + k)

    dV, dE = pack_small_grads(gV, gains, extra)
    red = reduce_scatter({"A": layer_grads[0][0], "B": layer_grads[0][1], "V": dV, "E": dE}, "rs_l0a")
    (gE,) = all_gather_chips([red["E"]], "gather_replicated_grads")
    per_layer = [[red["A"], red["B"]] + list(reduced["0b"])] + [list(reduced[str(l)]) for l in (1, 2, 3)]
    grads = unpack_grads(per_layer, red["V"], gE.reshape(E_ROWS, DM))
    for k in wnames:
        update(k, grads[k])
    return (loss, grad_x[None], *[grads[k] for k in wnames], *[deltas[k] for k in wnames],
            *[new_m[k] for k in wnames], *[new_v[k] for k in wnames])


def pack_weights(w):
    padc = lambda a: jnp.pad(a, ((0, 0), (0, FB - FW)))
    mix_in = [w["a_w_in"][0], None, w["c_w_pw1"][0], w["d_w_in"][0]]
    mix_out = [w["a_w_out"][0], None, w["c_w_pw2"][0], w["d_w_out"][0]]
    packed = []
    for l in range(4):
        a1 = jnp.concatenate([padc(w["ff1_w_gate"][l]), padc(w["ff1_w_up"][l])], axis=1).astype(BF16)
        b1 = _pad_rows(w["ff1_w_down"][l], FB).astype(BF16)
        cols = [padc(w["ff2_w_gate"][l]), padc(w["ff2_w_up"][l])]
        rows = [_pad_rows(w["ff2_w_down"][l], FB)]
        if l != 1:
            cols.append(mix_in[l])
            rows.append(mix_out[l])
        rows.append(w["ple_w_gate"][l])
        proj = [w["ple_w_proj"][l]] + ([w["b_w_grp"][0].reshape(256, 256)] if l == 1 else [])
        packed.append((a1, b1, jnp.concatenate(cols, axis=1).astype(BF16), jnp.concatenate(rows, axis=0).astype(BF16),
                       jnp.concatenate(proj, axis=0).astype(BF16)))
    PV = jnp.concatenate([_pad_rows(w["b_scale"], 8), _pad_rows(w["c_b_dw"], 8), _pad_rows(w["c_norm_g"], 8),
                          _pad_rows(w["c_norm_b"], 8), _pad_rows(w["d_w_conv"][0], 8), _pad_rows(w["c_w_dw"][0], 40)],
                         axis=0)
    return packed + [PV]


def pack_small_grads(gV, gains, extra):
    dVt = jnp.concatenate([gV[V_BSCALE], gV[V_CBDW], gV[V_CNG], gV[V_CNB], gV[V_DCONV], gV[V_CDW],
                           jnp.zeros((8, DM), F32)], axis=0)
    dV = dVt.reshape(V_ROWS, NCHIP, 256).transpose(1, 0, 2)
    rowsE = [_pad_rows(jnp.concatenate(gains[k], axis=0), 8) for k in GAIN_NAMES]
    rowsE += [_pad_rows(extra["a_v_norm_g"].reshape(3, DM), 8), _pad_rows(extra["a_v_norm_b"].reshape(3, DM), 8),
              jnp.pad(extra["a_b_s"].reshape(1536), (0, 8 * DM - 1536)).reshape(8, DM),
              extra["a_w_s"].reshape(192, DM)]
    dE = _pad_rows(jnp.concatenate(rowsE, axis=0), E_ROWS).reshape(NCHIP, E_ROWS // NCHIP, DM)
    return dV, dE


def unpack_grads(per_layer, RV, gE):
    grads = {}
    for i, k in enumerate(GAIN_NAMES):
        grads[k] = lambda i=i: gE[8 * i:8 * i + 4]
    grads["a_v_norm_g"] = lambda: gE[64:67].reshape(1, 3072)
    grads["a_v_norm_b"] = lambda: gE[72:75].reshape(1, 3072)
    grads["a_b_s"] = lambda: gE[80:88].reshape(8 * DM)[:1536].reshape(1, 12, 128)
    grads["a_w_s"] = lambda: gE[88:280].reshape(1, 12, 128, 128)
    col1 = lambda l, off, n: per_layer[l][0][:, off:off + n]
    col2 = lambda l, off, n: per_layer[l][2][:, off:off + n]
    grads["ff1_w_gate"] = lambda: jnp.stack([col1(l, A_FF(l, 0), FW) for l in range(4)])
    grads["ff1_w_up"] = lambda: jnp.stack([col1(l, A_FF(l, 1), FW) for l in range(4)])
    grads["ff2_w_gate"] = lambda: jnp.stack([col2(l, A_FF(l, 2), FW) for l in range(4)])
    grads["ff2_w_up"] = lambda: jnp.stack([col2(l, A_FF(l, 3), FW) for l in range(4)])
    grads["a_w_in"] = lambda: col2(0, A_AIN, 1536)[None]
    grads["c_w_pw1"] = lambda: col2(2, A_CIN, 512)[None]
    grads["d_w_in"] = lambda: col2(3, A_DIN, 768)[None]
    row2 = lambda l, off, n: per_layer[l][3][off:off + n]
    grads["ff1_w_down"] = lambda: jnp.stack([per_layer[l][1][:FW] for l in range(4)])
    grads["ff2_w_down"] = lambda: jnp.stack([row2(l, B_FF2D(l), FW) for l in range(4)])
    grads["ple_w_gate"] = lambda: jnp.stack([row2(l, B_PLEG(l), 256) for l in range(4)])
    grads["a_w_out"] = lambda: row2(0, B_AOUT, 768)[None]
    grads["c_w_pw2"] = lambda: row2(2, B_CPW2, 256)[None]
    grads["d_w_out"] = lambda: row2(3, B_DOUT, 256)[None]
    grads["ple_w_proj"] = lambda: jnp.stack([per_layer[l][4][C_PROJ(l):C_PROJ(l) + 256] for l in range(4)])
    grads["b_w_grp"] = lambda: per_layer[1][4][C_BGRP:C_BGRP + 256].reshape(1, 4, 64, 256)
    grads["b_scale"] = lambda: RV[V_BSCALE:V_BSCALE + 1]
    grads["c_b_dw"] = lambda: RV[V_CBDW:V_CBDW + 1]
    grads["c_norm_g"] = lambda: RV[V_CNG:V_CNG + 1]
    grads["c_norm_b"] = lambda: RV[V_CNB:V_CNB + 1]
    grads["d_w_conv"] = lambda: RV[V_DCONV:V_DCONV + 3][None]
    grads["c_w_dw"] = lambda: RV[V_CDW:V_CDW + 31][None]
    return {k: f() for k, f in grads.items()}
```

```python
import functools
import math

import jax
import jax.numpy as jnp
from jax import lax
from jax.experimental import pallas as pl
from jax.experimental.pallas import tpu as pltpu

F32, BF16 = jnp.float32, jnp.bfloat16
EPS = 1e-6
DM = 1024
FW = 704
FB = 768
NCHIP = 4
VMEM_LIMIT = 56 * 1024 * 1024
ANY = pl.BlockSpec(memory_space=pl.ANY)
MESH = pl.DeviceIdType.MESH

A_FF = lambda l, j: (j % 2) * FB
A_AIN = A_CIN = A_DIN = 2 * FB
A2_COLS = lambda l: 2 * FB + (1536, 0, 512, 768)[l]
B_FF1D = lambda l: 0
B_FF2D = lambda l: 0
B_AOUT = B_CPW2 = B_DOUT = FB
B_PLEG = lambda l: FB + (768, 0, 256, 256)[l]
B2_ROWS = lambda l: B_PLEG(l) + 256
C_PROJ = lambda l: 0
C_BGRP = 256
V_BSCALE, V_CBDW, V_CNG, V_CNB, V_DCONV, V_CDW, V_ROWS = 0, 8, 16, 24, 32, 40, 80
E_ROWS = 320

ADAM_LR, ADAM_B1, ADAM_B2, ADAM_EPS, ADAM_WD, ADAM_STEP = 0.001, 0.9, 0.999, 1e-08, 0.01, 10


def _cp(sem):
    return pltpu.CompilerParams(dimension_semantics=sem, vmem_limit_bytes=VMEM_LIMIT)


def _sig(x):
    return 0.5 * jnp.tanh(0.5 * x) + 0.5


_GC = math.sqrt(2.0 / math.pi)


def _gelu(x):
    return 0.5 * x * (1.0 + jnp.tanh(_GC * (x + 0.044715 * x * x * x)))


def _gelu_grad(x):
    t = jnp.tanh(_GC * (x + 0.044715 * x * x * x))
    return 0.5 * (1.0 + t) + 0.5 * x * (1.0 - t * t) * _GC * (1.0 + 3.0 * 0.044715 * x * x)


def _dot_nn(a, b):
    return lax.dot_general(a, b, (((1,), (0,)), ((), ())), preferred_element_type=F32)


def _dot_nt(a, b):
    return lax.dot_general(a, b, (((1,), (1,)), ((), ())), preferred_element_type=F32)


def _dot_tn(a, b):
    return lax.dot_general(a, b, (((0,), (0,)), ((), ())), preferred_element_type=F32)


def mm_cs(x, G, off, nb, tn, out_dtype, name, roff=0):
    T, K = x.shape
    tm = min(1024, T)
    nj, ob, rb_ = nb // tn, off // tn, roff // K
    assert nb % tn == 0 and off % tn == 0 and roff % K == 0

    def body(x_ref, w_ref, o_ref):
        o_ref[...] = _dot_nn(x_ref[...], w_ref[...]).astype(o_ref.dtype)

    return pl.pallas_call(
        body, name=name, grid=(T // tm, NCHIP, nj),
        in_specs=[pl.BlockSpec((tm, K), lambda i, s, j: (i, 0)),
                  pl.BlockSpec((None, K, tn), lambda i, s, j: (s, rb_, ob + j))],
        out_specs=pl.BlockSpec((tm, tn), lambda i, s, j: (i, s * nj + j)),
        out_shape=jax.ShapeDtypeStruct((T, NCHIP * nb), out_dtype),
        compiler_params=_cp(("parallel", "arbitrary", "arbitrary")))(x, G)


def mm_rs_t(dy, G, off, rb, tk, name):
    T, N = dy.shape
    tm = min(1024, T)
    nkk, ob = rb // tk, off // tk
    nk = NCHIP * nkk

    def body(dy_ref, w_ref, o_ref):
        o_ref[...] = _dot_nt(dy_ref[...], w_ref[...]).astype(o_ref.dtype)

    return pl.pallas_call(
        body, name=name, grid=(T // tm, nk),
        in_specs=[pl.BlockSpec((tm, N), lambda i, k: (i, 0)),
                  pl.BlockSpec((None, tk, N), lambda i, k: (k // nkk, ob + k % nkk, 0))],
        out_specs=pl.BlockSpec((tm, tk), lambda i, k: (i, k)),
        out_shape=jax.ShapeDtypeStruct((T, NCHIP * rb), BF16),
        compiler_params=_cp(("parallel", "arbitrary")))(dy, G)


def mm_tn(a, b, tmm, tn, out_shape, out_map, name, into=None):
    T, M = a.shape
    N = b.shape[1]
    tt = min(2048, T)
    nt = T // tt

    def body(a_ref, b_ref, o_ref, acc):
        t = pl.program_id(2)

        @pl.when(t == 0)
        def _():
            acc[...] = jnp.zeros_like(acc)

        acc[...] += _dot_tn(a_ref[...], b_ref[...])

        @pl.when(t == nt - 1)
        def _():
            o_ref[...] = acc[...].astype(o_ref.dtype)

    in_specs = [pl.BlockSpec((tt, tmm), lambda i, j, t: (t, i)), pl.BlockSpec((tt, tn), lambda i, j, t: (t, j))]
    operands = (a, b)
    if into is None:
        def kern(a_ref, b_ref, o_ref, acc):
            body(a_ref, b_ref, o_ref, acc)
        aliases = {}
    else:
        def kern(a_ref, b_ref, into_ref, o_ref, acc):
            body(a_ref, b_ref, o_ref, acc)
        in_specs.append(ANY)
        operands = (a, b, into)
        aliases = {2: 0}
        out_shape = into.shape
    return pl.pallas_call(
        kern, name=name, grid=(M // tmm, N // tn, nt), in_specs=in_specs,
        out_specs=pl.BlockSpec((None, tmm, tn), lambda i, j, t: out_map(i, j)),
        out_shape=jax.ShapeDtypeStruct(out_shape, BF16), input_output_aliases=aliases,
        scratch_shapes=[pltpu.VMEM((tmm, tn), F32)],
        compiler_params=_cp(("parallel", "parallel", "arbitrary")))(*operands)


def dw_cs(x, dy, nb, tn, name, width=None, off=0, into=None):
    K = x.shape[1]
    nj, ob = nb // tn, off // tn
    assert off % tn == 0
    return mm_tn(x, dy, K, tn, (NCHIP, K, width or nb), lambda i, j: (j // nj, 0, ob + j % nj), name, into)


def dw_rs(a, dy, rb, tr, name, height=None, off=0, into=None):
    N = dy.shape[1]
    ni, ob = rb // tr, off // tr
    assert off % tr == 0
    return mm_tn(a, dy, tr, N, (NCHIP, height or rb, N), lambda i, j: (i // ni, ob + i % ni, 0), name, into)


def ff_gateup(xn, GA, off, name):
    T, K = xn.shape
    tm = min(1024, T)
    ob = off // (2 * FB)
    assert off % (2 * FB) == 0

    sub = min(512, tm)

    def body(x_ref, w_ref, gu_ref, a_ref):
        for r0 in range(0, tm, sub):
            r = _dot_nn(x_ref[r0:r0 + sub, :], w_ref[...])
            g, u = r[:, :FB], r[:, FB:]
            gu_ref[r0:r0 + sub, :] = r.astype(gu_ref.dtype)
            a_ref[r0:r0 + sub, :] = (g * _sig(g) * u).astype(a_ref.dtype)

    return pl.pallas_call(
        body, name=name, grid=(T // tm, NCHIP),
        in_specs=[pl.BlockSpec((tm, K), lambda i, s: (i, 0)),
                  pl.BlockSpec((None, K, 2 * FB), lambda i, s: (s, 0, ob))],
        out_specs=[pl.BlockSpec((tm, 2 * FB), lambda i, s: (i, s)), pl.BlockSpec((tm, FB), lambda i, s: (i, s))],
        out_shape=[jax.ShapeDtypeStruct((T, NCHIP * 2 * FB), BF16), jax.ShapeDtypeStruct((T, NCHIP * FB), BF16)],
        compiler_params=_cp(("parallel", "arbitrary")))(xn, GA)


def mm_rs_post(a, G, off, rb, tk, h, g, scale, name, g_next=None):
    T = a.shape[0]
    N = G.shape[2]
    tm = min(1024, T)
    sub = min(512, tm)
    nkk, ob = rb // tk, off // tk
    nk = NCHIP * nkk
    assert rb % tk == 0 and off % tk == 0
    more = g_next is not None

    def body(a_ref, w_ref, h_ref, g_ref, *rest):
        if more:
            gn_ref, f_ref, o_ref, xn_ref, acc = rest
        else:
            f_ref, o_ref, acc = rest
        k = pl.program_id(1)

        @pl.when(k == 0)
        def _():
            acc[...] = jnp.zeros_like(acc)

        acc[...] += _dot_nn(a_ref[...], w_ref[...])

        @pl.when(k == nk - 1)
        def _():
            for r0 in range(0, tm, sub):
                f = acc[r0:r0 + sub, :]
                f_ref[r0:r0 + sub, :] = f
                r = lax.rsqrt(jnp.mean(f * f, axis=-1, keepdims=True) + EPS)
                hn = h_ref[r0:r0 + sub, :] + scale * (f * r * g_ref[...])
                o_ref[r0:r0 + sub, :] = hn
                if more:
                    rn = lax.rsqrt(jnp.mean(hn * hn, axis=-1, keepdims=True) + EPS)
                    xn_ref[r0:r0 + sub, :] = (hn * rn * gn_ref[...]).astype(xn_ref.dtype)

    row = pl.BlockSpec((tm, N), lambda i, k: (i, 0))
    row1 = pl.BlockSpec((tm, N), lambda i, k: (i, 0), pipeline_mode=pl.Buffered(1))
    vec = pl.BlockSpec((1, N), lambda i, k: (0, 0))
    return pl.pallas_call(
        body, name=name, grid=(T // tm, nk),
        in_specs=[pl.BlockSpec((tm, tk), lambda i, k: (i, k)),
                  pl.BlockSpec((None, tk, N), lambda i, k: (k // nkk, ob + k % nkk, 0)),
                  row1, vec] + ([vec] if more else []),
        out_specs=[row, row] + ([row] if more else []),
        out_shape=[jax.ShapeDtypeStruct((T, N), F32), jax.ShapeDtypeStruct((T, N), F32)]
        + ([jax.ShapeDtypeStruct((T, N), BF16)] if more else []),
        scratch_shapes=[pltpu.VMEM((tm, N), F32)],
        compiler_params=_cp(("parallel", "arbitrary")))(*((a, G, h, g) + ((g_next,) if more else ())))


def ff_bwd_down(dh, f, g, GB, down, gu, name):
    T, N = dh.shape
    tm = min(512, T)
    sub = min(256, tm)
    ob = down // FB

    def body(d_ref, f_ref, g_ref, w_ref, gu_ref, df_ref, dg_ref, dgu_ref):
        i = pl.program_id(0)
        dg = jnp.zeros((1, N), F32)
        for r0 in range(0, tm, sub):
            f = f_ref[r0:r0 + sub, :]
            r = lax.rsqrt(jnp.mean(f * f, axis=-1, keepdims=True) + EPS)
            d = 0.5 * d_ref[r0:r0 + sub, :]
            t = d * g_ref[...]
            df = (r * t - f * (r * r * r * jnp.mean(t * f, axis=-1, keepdims=True))).astype(df_ref.dtype)
            df_ref[r0:r0 + sub, :] = df
            dg = dg + jnp.sum(d * f * r, axis=0, keepdims=True)
            for s in range(NCHIP):
                c0 = s * 2 * FB
                da = _dot_nt(df, w_ref[s])
                gt = gu_ref[r0:r0 + sub, c0:c0 + FB].astype(F32)
                u = gu_ref[r0:r0 + sub, c0 + FB:c0 + 2 * FB].astype(F32)
                sg = _sig(gt)
                silu = gt * sg
                dgu_ref[r0:r0 + sub, c0:c0 + FB] = (da * u * (sg + silu - silu * sg)).astype(dgu_ref.dtype)
                dgu_ref[r0:r0 + sub, c0 + FB:c0 + 2 * FB] = (da * silu).astype(dgu_ref.dtype)
        _acc_rows(dg_ref, i, dg)

    row = pl.BlockSpec((tm, N), lambda i: (i, 0))
    vec = pl.BlockSpec((1, N), lambda i: (0, 0))
    wide = pl.BlockSpec((tm, NCHIP * 2 * FB), lambda i: (i, 0))
    return pl.pallas_call(
        body, name=name, grid=(T // tm,),
        in_specs=[row, row, vec,
                  pl.BlockSpec((NCHIP, FB, N), lambda i: (0, ob, 0), pipeline_mode=pl.Buffered(1)), wide],
        out_specs=[row, vec, wide],
        out_shape=[jax.ShapeDtypeStruct((T, N), BF16), jax.ShapeDtypeStruct((1, N), F32),
                   jax.ShapeDtypeStruct((T, NCHIP * 2 * FB), BF16)],
        compiler_params=_cp(("arbitrary",)))(dh, f, g, GB, gu)


def mm_cs_t_rms(dy, G, off, nb, tn, h, g, skip, name):
    T = dy.shape[0]
    K = G.shape[1]
    tm = min(512, T)
    sub = min(256, tm)
    assert off % nb == 0 and nb % tn == 0

    def body(dy_ref, w_ref, h_ref, g_ref, s_ref, o_ref, dg_ref):
        i = pl.program_id(0)
        dg = jnp.zeros((1, K), F32)
        for r0 in range(0, tm, sub):
            d = _dot_nt(dy_ref[r0:r0 + sub, :nb], w_ref[0])
            for s in range(1, NCHIP):
                d = d + _dot_nt(dy_ref[r0:r0 + sub, s * nb:(s + 1) * nb], w_ref[s])
            x = h_ref[r0:r0 + sub, :]
            r = lax.rsqrt(jnp.mean(x * x, axis=-1, keepdims=True) + EPS)
            xh = x * r
            t = d * g_ref[...]
            o_ref[r0:r0 + sub, :] = s_ref[r0:r0 + sub, :] + r * (t - xh * jnp.mean(t * xh, axis=-1, keepdims=True))
            dg = dg + jnp.sum(d * xh, axis=0, keepdims=True)
        _acc_rows(dg_ref, i, dg)

    row = pl.BlockSpec((tm, K), lambda i: (i, 0))
    vec = pl.BlockSpec((1, K), lambda i: (0, 0))
    return pl.pallas_call(
        body, name=name, grid=(T // tm,),
        in_specs=[pl.BlockSpec((tm, NCHIP * nb), lambda i: (i, 0)),
                  pl.BlockSpec((NCHIP, K, nb), lambda i: (0, 0, off // nb), pipeline_mode=pl.Buffered(1)),
                  row, vec, row],
        out_specs=[row, vec],
        out_shape=[jax.ShapeDtypeStruct((T, K), F32), jax.ShapeDtypeStruct((1, K), F32)],
        compiler_params=_cp(("arbitrary",)))(dy, G, h, g, skip)


def _rows(tm, C):
    return pl.BlockSpec((tm, C), lambda i: (i, 0))


def _vec(C):
    return pl.BlockSpec((1, C), lambda i: (0, 0))


def _acc_rows(ref, i, val):
    @pl.when(i == 0)
    def _():
        ref[...] = val

    @pl.when(i > 0)
    def _():
        ref[...] += val


def rms_fwd(h, g, out_dtype, name):
    T, C = h.shape
    tm = min(512, T)

    def body(h_ref, g_ref, o_ref):
        x = h_ref[...]
        r = lax.rsqrt(jnp.mean(x * x, axis=-1, keepdims=True) + EPS)
        o_ref[...] = (x * r * g_ref[...]).astype(o_ref.dtype)

    return pl.pallas_call(
        body, name=name, grid=(T // tm,), in_specs=[_rows(tm, C), _vec(C)], out_specs=_rows(tm, C),
        out_shape=jax.ShapeDtypeStruct((T, C), out_dtype), compiler_params=_cp(("parallel",)))(h, g)


def rms_bwd(dxn, h, g, dh_skip, name):
    T, C = h.shape
    tm = min(512, T)

    def body(d_ref, h_ref, g_ref, s_ref, o_ref, dg_ref):
        i = pl.program_id(0)
        x = h_ref[...]
        r = lax.rsqrt(jnp.mean(x * x, axis=-1, keepdims=True) + EPS)
        xh = x * r
        d = d_ref[...].astype(F32)
        t = d * g_ref[...]
        o_ref[...] = s_ref[...] + r * (t - xh * jnp.mean(t * xh, axis=-1, keepdims=True))
        _acc_rows(dg_ref, i, jnp.sum(d * xh, axis=0, keepdims=True))

    return pl.pallas_call(
        body, name=name, grid=(T // tm,),
        in_specs=[_rows(tm, C), _rows(tm, C), _vec(C), _rows(tm, C)],
        out_specs=[_rows(tm, C), _vec(C)],
        out_shape=[jax.ShapeDtypeStruct((T, C), F32), jax.ShapeDtypeStruct((1, C), F32)],
        compiler_params=_cp(("arbitrary",)))(dxn, h, g, dh_skip)


def post_res(h, f, g, scale, name):
    T, C = h.shape
    tm = min(512, T)

    def body(h_ref, f_ref, g_ref, o_ref):
        f = f_ref[...]
        r = lax.rsqrt(jnp.mean(f * f, axis=-1, keepdims=True) + EPS)
        o_ref[...] = h_ref[...] + scale * (f * r * g_ref[...])

    return pl.pallas_call(
        body, name=name, grid=(T // tm,), in_specs=[_rows(tm, C), _rows(tm, C), _vec(C)],
        out_specs=_rows(tm, C), out_shape=jax.ShapeDtypeStruct((T, C), F32),
        compiler_params=_cp(("parallel",)))(h, f, g)


def post_res_bwd(dh, f, g, scale, out_dtype, name):
    T, C = dh.shape
    tm = min(512, T)

    def body(d_ref, f_ref, g_ref, o_ref, dg_ref):
        i = pl.program_id(0)
        f = f_ref[...]
        r = lax.rsqrt(jnp.mean(f * f, axis=-1, keepdims=True) + EPS)
        d = scale * d_ref[...]
        t = d * g_ref[...]
        o_ref[...] = (r * t - f * (r * r * r * jnp.mean(t * f, axis=-1, keepdims=True))).astype(o_ref.dtype)
        _acc_rows(dg_ref, i, jnp.sum(d * f * r, axis=0, keepdims=True))

    return pl.pallas_call(
        body, name=name, grid=(T // tm,), in_specs=[_rows(tm, C), _rows(tm, C), _vec(C)],
        out_specs=[_rows(tm, C), _vec(C)],
        out_shape=[jax.ShapeDtypeStruct((T, C), out_dtype), jax.ShapeDtypeStruct((1, C), F32)],
        compiler_params=_cp(("arbitrary",)))(dh, f, g)


PLE_W = 256


def ple_fwd(h, xn, pb, GB, goff, GC, g, name, g_next=None):
    T, C = h.shape
    tm = min(512, T)
    more = g_next is not None

    def body(h_ref, x_ref, p_ref, wg_ref, wp_ref, g_ref, *rest):
        if more:
            gn_ref, z_ref, pe_ref, o_ref, xn_ref = rest
        else:
            z_ref, pe_ref, o_ref = rest
        zg = _dot_nn(x_ref[:, :PLE_W], wg_ref[0])
        for s in range(1, NCHIP):
            zg = zg + _dot_nn(x_ref[:, s * PLE_W:(s + 1) * PLE_W], wg_ref[s])
        z_ref[...] = zg
        for s in range(NCHIP):
            pe_ref[:, s * PLE_W:(s + 1) * PLE_W] = _dot_nn(p_ref[...], wp_ref[s])
        e = pe_ref[...] * _sig(zg)
        r = lax.rsqrt(jnp.mean(e * e, axis=-1, keepdims=True) + EPS)
        hn = h_ref[...] + e * r * g_ref[...]
        o_ref[...] = hn
        if more:
            rn = lax.rsqrt(jnp.mean(hn * hn, axis=-1, keepdims=True) + EPS)
            xn_ref[...] = (hn * rn * gn_ref[...]).astype(xn_ref.dtype)

    return pl.pallas_call(
        body, name=name, grid=(T // tm,),
        in_specs=[_rows(tm, C), _rows(tm, C), _rows(tm, PLE_W),
                  pl.BlockSpec((NCHIP, PLE_W, C), lambda i: (0, goff // PLE_W, 0)),
                  pl.BlockSpec((NCHIP, PLE_W, PLE_W), lambda i: (0, 0, 0)), _vec(C)] + ([_vec(C)] if more else []),
        out_specs=[_rows(tm, C)] * (4 if more else 3),
        out_shape=[jax.ShapeDtypeStruct((T, C), F32)] * 3 + ([jax.ShapeDtypeStruct((T, C), BF16)] if more else []),
        compiler_params=_cp(("parallel",)))(*((h, xn, pb, GB, GC, g) + ((g_next,) if more else ())))


def ple_bwd(dh, zg, pe, h_in, GB, goff, g, g_pre, name):
    T, C = dh.shape
    tm = min(512, T)

    def body(d_ref, z_ref, p_ref, h_ref, wg_ref, g_ref, gp_ref, dz_ref, dp_ref, o_ref, dg_ref, dgp_ref, dxn):
        i = pl.program_id(0)
        s = _sig(z_ref[...])
        pe_ = p_ref[...]
        e = pe_ * s
        r = lax.rsqrt(jnp.mean(e * e, axis=-1, keepdims=True) + EPS)
        d = d_ref[...]
        t = d * g_ref[...]
        de = r * t - e * (r * r * r * jnp.mean(t * e, axis=-1, keepdims=True))
        dp_ref[...] = (de * s).astype(dp_ref.dtype)
        dz = (de * pe_ * s * (1.0 - s)).astype(dz_ref.dtype)
        dz_ref[...] = dz
        _acc_rows(dg_ref, i, jnp.sum(d * e * r, axis=0, keepdims=True))
        for k in range(NCHIP):
            dxn[:, k * PLE_W:(k + 1) * PLE_W] = _dot_nt(dz, wg_ref[k])
        x = h_ref[...]
        rx = lax.rsqrt(jnp.mean(x * x, axis=-1, keepdims=True) + EPS)
        xh = x * rx
        dx = dxn[...]
        tx = dx * gp_ref[...]
        o_ref[...] = d + rx * (tx - xh * jnp.mean(tx * xh, axis=-1, keepdims=True))
        _acc_rows(dgp_ref, i, jnp.sum(dx * xh, axis=0, keepdims=True))

    return pl.pallas_call(
        body, name=name, grid=(T // tm,),
        in_specs=[_rows(tm, C), _rows(tm, C), _rows(tm, C), _rows(tm, C),
                  pl.BlockSpec((NCHIP, PLE_W, C), lambda i: (0, goff // PLE_W, 0)), _vec(C), _vec(C)],
        out_specs=[_rows(tm, C), _rows(tm, C), _rows(tm, C), _vec(C), _vec(C)],
        out_shape=[jax.ShapeDtypeStruct((T, C), BF16), jax.ShapeDtypeStruct((T, C), BF16),
                   jax.ShapeDtypeStruct((T, C), F32), jax.ShapeDtypeStruct((1, C), F32),
                   jax.ShapeDtypeStruct((1, C), F32)],
        scratch_shapes=[pltpu.VMEM((tm, C), F32)],
        compiler_params=_cp(("arbitrary",)))(dh, zg, pe, h_in, GB, g, g_pre)


def loss_head(h, tgt, name):
    T, C = h.shape
    tm = min(512, T)

    def body(h_ref, t_ref, d_ref, l_ref):
        i = pl.program_id(0)
        e = h_ref[...] - t_ref[...]
        d_ref[...] = e * (1.0 / C)
        _acc_rows(l_ref, i, jnp.sum(e * e, axis=0, keepdims=True))

    return pl.pallas_call(
        body, name=name, grid=(T // tm,), in_specs=[_rows(tm, C), _rows(tm, C)],
        out_specs=[_rows(tm, C), _vec(C)],
        out_shape=[jax.ShapeDtypeStruct((T, C), F32), jax.ShapeDtypeStruct((1, C), F32)],
        compiler_params=_cp(("arbitrary",)))(h, tgt)


AH, AG_N, AGW, CHUNK = 3072, 12, 256, 128


def _tril_bf16(w):
    r = lax.broadcasted_iota(jnp.int32, (CHUNK, CHUNK), 0)
    c = lax.broadcasted_iota(jnp.int32, (CHUNK, CHUNK), 1)
    return jnp.where(r >= c, w, 0.0).astype(BF16)


def _ln_stats(vs_ref, width):
    v = vs_ref[...]
    mu = jnp.sum(v, axis=-1, keepdims=True) * (1.0 / width)
    vc = v - mu
    var = jnp.sum(vc * vc, axis=-1, keepdims=True) * (1.0 / width)
    return mu, lax.rsqrt(var + EPS)


def gmlp_mid_fwd(zpre, vg, vb, ws, bsf, name):
    T = zpre.shape[0]

    def body(z_ref, vg_ref, vb_ref, ws_ref, bs_ref, y_ref, vs_ref):
        for g in range(AG_N):
            vs_ref[:, g * AGW:(g + 1) * AGW] = _gelu(z_ref[:, AH + g * AGW:AH + (g + 1) * AGW].astype(F32))
        mu, rstd = _ln_stats(vs_ref, AH)
        for g in range(AG_N):
            sl = slice(g * AGW, (g + 1) * AGW)
            vn = ((vs_ref[:, sl] - mu) * rstd * vg_ref[:, sl] + vb_ref[:, sl]).astype(BF16)
            sv = _dot_nn(_tril_bf16(ws_ref[g]), vn) + bs_ref[g]
            u = _gelu(z_ref[:, sl].astype(F32))
            y_ref[:, sl] = (u * sv).astype(y_ref.dtype)

    return pl.pallas_call(
        body, name=name, grid=(T // CHUNK,),
        in_specs=[_rows(CHUNK, 2 * AH), _vec(AH), _vec(AH),
                  pl.BlockSpec((AG_N, CHUNK, CHUNK), lambda i: (0, 0, 0)),
                  pl.BlockSpec((AG_N, CHUNK, AGW), lambda i: (0, 0, 0))],
        out_specs=_rows(CHUNK, AH), out_shape=jax.ShapeDtypeStruct((T, AH), BF16),
        scratch_shapes=[pltpu.VMEM((CHUNK, AH), F32)],
        compiler_params=_cp(("parallel",)))(zpre, vg, vb, ws, bsf)


def gmlp_mid_bwd(zpre, dy, vg, vb, ws, bsf, name):
    T = zpre.shape[0]

    def body(z_ref, dy_ref, vg_ref, vb_ref, ws_ref, bs_ref, dz_ref, dws_ref, dbs_ref, dvg_ref, dvb_ref,
             vs_ref, dvn_ref):
        i = pl.program_id(0)

        @pl.when(i == 0)
        def _():
            dws_ref[...] = jnp.zeros_like(dws_ref)
            dbs_ref[...] = jnp.zeros_like(dbs_ref)
            dvg_ref[...] = jnp.zeros_like(dvg_ref)
            dvb_ref[...] = jnp.zeros_like(dvb_ref)

        for g in range(AG_N):
            vs_ref[:, g * AGW:(g + 1) * AGW] = _gelu(z_ref[:, AH + g * AGW:AH + (g + 1) * AGW].astype(F32))
        mu, rstd = _ln_stats(vs_ref, AH)
        r_i = lax.broadcasted_iota(jnp.int32, (CHUNK, CHUNK), 0)
        c_i = lax.broadcasted_iota(jnp.int32, (CHUNK, CHUNK), 1)
        ones8 = jnp.ones((8, AGW), F32)
        m1 = jnp.zeros((CHUNK, 1), F32)
        m2 = jnp.zeros((CHUNK, 1), F32)
        for g in range(AG_N):
            sl = slice(g * AGW, (g + 1) * AGW)
            vh = (vs_ref[:, sl] - mu) * rstd
            vn = (vh * vg_ref[:, sl] + vb_ref[:, sl]).astype(BF16)
            wm = _tril_bf16(ws_ref[g])
            sv = _dot_nn(wm, vn) + bs_ref[g]
            zu = z_ref[:, sl].astype(F32)
            u = _gelu(zu)
            dyg = dy_ref[:, sl].astype(F32)
            dz_ref[:, sl] = (dyg * sv * _gelu_grad(zu)).astype(dz_ref.dtype)
            dsv = dyg * u
            dsv_b = dsv.astype(BF16)
            dws_ref[g] += jnp.where(r_i >= c_i, _dot_nt(dsv_b, vn), 0.0)
            dbs_ref[g] += _dot_nt(ones8, dsv)
            dvn = _dot_tn(wm, dsv_b)
            dvn_ref[:, sl] = dvn
            dvh = dvn * vg_ref[:, sl]
            m1 = m1 + jnp.sum(dvh, axis=-1, keepdims=True)
            m2 = m2 + jnp.sum(dvh * vh, axis=-1, keepdims=True)
            dvg_ref[:, sl] += jnp.sum(dvn * vh, axis=0, keepdims=True)
            dvb_ref[:, sl] += jnp.sum(dvn, axis=0, keepdims=True)
        m1 = m1 * (1.0 / AH)
        m2 = m2 * (1.0 / AH)
        for g in range(AG_N):
            sl = slice(g * AGW, (g + 1) * AGW)
            vh = (vs_ref[:, sl] - mu) * rstd
            dv = rstd * (dvn_ref[:, sl] * vg_ref[:, sl] - m1 - vh * m2)
            zv = z_ref[:, AH + g * AGW:AH + (g + 1) * AGW].astype(F32)
            dz_ref[:, AH + g * AGW:AH + (g + 1) * AGW] = (dv * _gelu_grad(zv)).astype(dz_ref.dtype)

    full3 = lambda a, b, c: pl.BlockSpec((a, b, c), lambda i: (0, 0, 0))
    return pl.pallas_call(
        body, name=name, grid=(T // CHUNK,),
        in_specs=[_rows(CHUNK, 2 * AH), _rows(CHUNK, AH), _vec(AH), _vec(AH),
                  full3(AG_N, CHUNK, CHUNK), full3(AG_N, CHUNK, AGW)],
        out_specs=[_rows(CHUNK, 2 * AH), full3(AG_N, CHUNK, CHUNK), full3(AG_N, 8, CHUNK), _vec(AH), _vec(AH)],
        out_shape=[jax.ShapeDtypeStruct((T, 2 * AH), BF16), jax.ShapeDtypeStruct((AG_N, CHUNK, CHUNK), F32),
                   jax.ShapeDtypeStruct((AG_N, 8, CHUNK), F32), jax.ShapeDtypeStruct((1, AH), F32),
                   jax.ShapeDtypeStruct((1, AH), F32)],
        scratch_shapes=[pltpu.VMEM((CHUNK, AH), F32), pltpu.VMEM((CHUNK, AH), F32)],
        compiler_params=_cp(("arbitrary",)))(zpre, dy, vg, vb, ws, bsf)


SLAB = 256
NSLAB = DM // SLAB
RC = 256
PAD = 32


def _col(T, j):
    return pl.BlockSpec((T, SLAB), lambda c: (0, j * NSLAB + c))


def _chunks(T, fn):
    def step(i, carry):
        fn(pl.multiple_of(i * RC, RC))
        return carry
    lax.fori_loop(0, T // RC, step, 0)


def _conv_taps(K):
    return [(r, [q for q in range(4) if 8 * q + r < K]) for r in range(min(8, K))]


def _causal_conv(zpad_ref, wrow, K, r0):
    acc = None
    for r, qs in _conv_taps(K):
        a = None
        for q in qs:
            term = wrow(8 * q + r) * zpad_ref[pl.ds(r0 + (PAD - 8 - 8 * q), RC + 8), :]
            a = term if a is None else a + term
        a = a if r == 0 else pltpu.roll(a, r, 0)
        acc = a if acc is None else acc + a
    return acc[8:, :]


def _anticausal_conv(gpad_ref, wrow, K, r0):
    acc = None
    for r, qs in _conv_taps(K):
        b = None
        for q in qs:
            term = wrow(8 * q + r) * gpad_ref[pl.ds(r0 + 8 * q, RC + 8), :]
            b = term if b is None else b + term
        b = b if r == 0 else pltpu.roll(b, RC + 8 - r, 0)
        acc = b if acc is None else acc + b
    return acc[:RC, :]


def _conv_dw(gpad_ref, zpad_ref, dw_ref, K, r0):
    for r, qs in _conv_taps(K):
        gw = gpad_ref[pl.ds(r0, RC + 8), :]
        p = (gw if r == 0 else pltpu.roll(gw, RC + 8 - r, 0))[:RC, :]
        for q in qs:
            z = zpad_ref[pl.ds(r0 + (PAD - 8 * q), RC), :]
            dw_ref[8 * q + r] += jnp.sum((p * z).reshape(RC // 8, 8, SLAB), axis=0)


def _zero_rows(ref, start, n):
    ref[pl.ds(start, n), :] = jnp.zeros((n, SLAB), F32)


def pool_fwd(hn, wg, sc, name):
    T = hn.shape[0]

    def body(h_ref, w_ref, s_ref, p_ref, yp_ref, y_ref, xpad):
        g = pl.program_id(0)
        wf = jnp.left_shift(2, g).astype(F32)
        _zero_rows(xpad, 0, PAD)

        def fill(r0):
            xpad[pl.ds(r0 + PAD, RC), :] = h_ref[pl.ds(r0, RC), :]
        _chunks(T, fill)

        def step(r0):
            w = xpad[pl.ds(r0 + (PAD - 16), RC + 16), :]
            s2 = w + pltpu.roll(w, 1, 0)
            s4 = s2 + pltpu.roll(s2, 2, 0)
            s8 = s4 + pltpu.roll(s4, 4, 0)
            s16 = s8 + pltpu.roll(s8, 8, 0)
            sel = jnp.where(g == 0, s2, jnp.where(g == 1, s4, jnp.where(g == 2, s8, s16)))[16:, :]
            t1 = (r0 + 1 + lax.broadcasted_iota(jnp.int32, (RC, SLAB), 0)).astype(F32)
            pooled = (sel / jnp.minimum(t1, wf) - w[16:, :]).astype(BF16)
            p_ref[pl.ds(r0, RC), :] = pooled
            yp = _dot_nn(pooled, w_ref[...])
            yp_ref[pl.ds(r0, RC), :] = yp
            y_ref[pl.ds(r0, RC), :] = yp * s_ref[...]
        _chunks(T, step)

    slab = pl.BlockSpec((T, SLAB), lambda c: (0, c))
    return pl.pallas_call(
        body, name=name, grid=(NSLAB,),
        in_specs=[slab, pl.BlockSpec((None, SLAB, SLAB), lambda c: (c, 0, 0)), pl.BlockSpec((1, SLAB), lambda c: (0, c))],
        out_specs=[slab, slab, slab],
        out_shape=[jax.ShapeDtypeStruct((T, DM), BF16), jax.ShapeDtypeStruct((T, DM), F32),
                   jax.ShapeDtypeStruct((T, DM), F32)],
        scratch_shapes=[pltpu.VMEM((T + PAD, SLAB), F32)],
        compiler_params=_cp(("parallel",)))(hn, wg, sc)


def pool_bwd(dy, ypre, pooled, wg, sc, name):
    T = dy.shape[0]

    def body(d_ref, yp_ref, p_ref, w_ref, s_ref, dh_ref, dw_ref, ds_ref, qpad, dwacc, dsacc):
        g = pl.program_id(0)
        wf = jnp.left_shift(2, g).astype(F32)
        dwacc[...] = jnp.zeros_like(dwacc)
        dsacc[...] = jnp.zeros_like(dsacc)
        _zero_rows(qpad, T, PAD)

        def first(r0):
            d = d_ref[pl.ds(r0, RC), :]
            dsacc[...] += jnp.sum((d * yp_ref[pl.ds(r0, RC), :]).reshape(RC // 8, 8, SLAB), axis=0)
            dyp = (d * s_ref[...]).astype(BF16)
            dpool = _dot_nt(dyp, w_ref[...])
            dwacc[...] += _dot_tn(p_ref[pl.ds(r0, RC), :], dyp)
            t1 = (r0 + 1 + lax.broadcasted_iota(jnp.int32, (RC, SLAB), 0)).astype(F32)
            qpad[pl.ds(r0, RC), :] = dpool / jnp.minimum(t1, wf)
            dh_ref[pl.ds(r0, RC), :] = dpool
        _chunks(T, first)

        def second(r0):
            w = qpad[pl.ds(r0, RC + 16), :]
            n = RC + 16
            a2 = w + pltpu.roll(w, n - 1, 0)
            a4 = a2 + pltpu.roll(a2, n - 2, 0)
            a8 = a4 + pltpu.roll(a4, n - 4, 0)
            a16 = a8 + pltpu.roll(a8, n - 8, 0)
            sel = jnp.where(g == 0, a2, jnp.where(g == 1, a4, jnp.where(g == 2, a8, a16)))[:RC, :]
            dh_ref[pl.ds(r0, RC), :] = sel - dh_ref[pl.ds(r0, RC), :]
        _chunks(T, second)
        dw_ref[...] = dwacc[...]
        ds_ref[...] = jnp.sum(dsacc[...], axis=0, keepdims=True)

    slab = pl.BlockSpec((T, SLAB), lambda c: (0, c))
    wspec = pl.BlockSpec((None, SLAB, SLAB), lambda c: (c, 0, 0))
    vec = pl.BlockSpec((1, SLAB), lambda c: (0, c))
    return pl.pallas_call(
        body, name=name, grid=(NSLAB,),
        in_specs=[slab, slab, slab, wspec, vec],
        out_specs=[slab, wspec, vec],
        out_shape=[jax.ShapeDtypeStruct((T, DM), F32), jax.ShapeDtypeStruct((NSLAB, SLAB, SLAB), F32),
                   jax.ShapeDtypeStruct((1, DM), F32)],
        scratch_shapes=[pltpu.VMEM((T + PAD, SLAB), F32), pltpu.VMEM((SLAB, SLAB), F32), pltpu.VMEM((8, SLAB), F32)],
        compiler_params=_cp(("parallel",)))(dy, ypre, pooled, wg, sc)


KC = 31
KD = 3


def conf_conv_fwd(ag, wdw, bdw, name):
    T = ag.shape[0]

    def body(a_ref, g_ref, w_ref, b_ref, o_ref, zpad):
        _zero_rows(zpad, 0, PAD)

        def fill(r0):
            a = a_ref[pl.ds(r0, RC), :].astype(F32)
            gt = g_ref[pl.ds(r0, RC), :].astype(F32)
            zpad[pl.ds(r0 + PAD, RC), :] = a * _sig(gt)
        _chunks(T, fill)
        wrow = lambda j: w_ref[KC - 1 - j:KC - j, :]

        def step(r0):
            o_ref[pl.ds(r0, RC), :] = _causal_conv(zpad, wrow, KC, r0) + b_ref[...]
        _chunks(T, step)

    vec = pl.BlockSpec((1, SLAB), lambda c: (0, c))
    return pl.pallas_call(
        body, name=name, grid=(NSLAB,),
        in_specs=[_col(T, 0), _col(T, 1), pl.BlockSpec((32, SLAB), lambda c: (0, c)), vec],
        out_specs=pl.BlockSpec((T, SLAB), lambda c: (0, c)),
        out_shape=jax.ShapeDtypeStruct((T, DM), F32),
        scratch_shapes=[pltpu.VMEM((T + PAD, SLAB), F32)],
        compiler_params=_cp(("parallel",)))(ag, ag, wdw, bdw)


def conf_conv_bwd(dzc, ag, wdw, name):
    T = ag.shape[0]

    def body(d_ref, a_ref, g_ref, w_ref, da_ref, dg_ref, dw_ref, db_ref, zpad, gpad, dwacc, dbacc):
        _zero_rows(zpad, 0, PAD)
        _zero_rows(gpad, T, PAD)
        dwacc[...] = jnp.zeros_like(dwacc)
        dbacc[...] = jnp.zeros_like(dbacc)

        def fill(r0):
            a = a_ref[pl.ds(r0, RC), :].astype(F32)
            gt = g_ref[pl.ds(r0, RC), :].astype(F32)
            zpad[pl.ds(r0 + PAD, RC), :] = a * _sig(gt)
            d = d_ref[pl.ds(r0, RC), :]
            gpad[pl.ds(r0, RC), :] = d
            dbacc[...] += jnp.sum(d.reshape(RC // 8, 8, SLAB), axis=0)
        _chunks(T, fill)
        wrow = lambda j: w_ref[KC - 1 - j:KC - j, :]

        def step(r0):
            dz = _anticausal_conv(gpad, wrow, KC, r0)
            a = a_ref[pl.ds(r0, RC), :].astype(F32)
            s = _sig(g_ref[pl.ds(r0, RC), :].astype(F32))
            da_ref[pl.ds(r0, RC), :] = (dz * s).astype(da_ref.dtype)
            dg_ref[pl.ds(r0, RC), :] = (dz * a * s * (1.0 - s)).astype(dg_ref.dtype)
            _conv_dw(gpad, zpad, dwacc, KC, r0)
        _chunks(T, step)
        dw_ref[...] = jnp.zeros_like(dw_ref)
        for k in range(KC):
            dw_ref[k:k + 1, :] = jnp.sum(dwacc[KC - 1 - k], axis=0, keepdims=True)
        db_ref[...] = jnp.sum(dbacc[...], axis=0, keepdims=True)

    vec = pl.BlockSpec((1, SLAB), lambda c: (0, c))
    w32 = pl.BlockSpec((32, SLAB), lambda c: (0, c))
    return pl.pallas_call(
        body, name=name, grid=(NSLAB,),
        in_specs=[pl.BlockSpec((T, SLAB), lambda c: (0, c)), _col(T, 0), _col(T, 1), w32],
        out_specs=[_col(T, 0), _col(T, 0), w32, vec],
        out_shape=[jax.ShapeDtypeStruct((T, DM), BF16), jax.ShapeDtypeStruct((T, DM), BF16),
                   jax.ShapeDtypeStruct((32, DM), F32), jax.ShapeDtypeStruct((1, DM), F32)],
        scratch_shapes=[pltpu.VMEM((T + PAD, SLAB), F32), pltpu.VMEM((T + PAD, SLAB), F32),
                        pltpu.VMEM((32, 8, SLAB), F32), pltpu.VMEM((8, SLAB), F32)],
        compiler_params=_cp(("parallel",)))(dzc, ag, ag, wdw)


def conf_ln_fwd(zc, g, b, name):
    T, C = zc.shape
    tm = min(512, T)

    def body(z_ref, g_ref, b_ref, o_ref):
        x = z_ref[...]
        xc = x - jnp.mean(x, axis=-1, keepdims=True)
        r = lax.rsqrt(jnp.mean(xc * xc, axis=-1, keepdims=True) + EPS)
        zl = xc * r * g_ref[...] + b_ref[...]
        o_ref[...] = (zl * _sig(zl)).astype(o_ref.dtype)

    return pl.pallas_call(
        body, name=name, grid=(T // tm,), in_specs=[_rows(tm, C), _vec(C), _vec(C)], out_specs=_rows(tm, C),
        out_shape=jax.ShapeDtypeStruct((T, C), BF16), compiler_params=_cp(("parallel",)))(zc, g, b)


def conf_ln_bwd(dzs, zc, g, b, name):
    T, C = zc.shape
    tm = min(512, T)

    def body(d_ref, z_ref, g_ref, b_ref, o_ref, dg_ref, db_ref):
        i = pl.program_id(0)
        x = z_ref[...]
        xc = x - jnp.mean(x, axis=-1, keepdims=True)
        r = lax.rsqrt(jnp.mean(xc * xc, axis=-1, keepdims=True) + EPS)
        xh = xc * r
        zl = xh * g_ref[...] + b_ref[...]
        s = _sig(zl)
        dzl = d_ref[...].astype(F32) * (s * (1.0 + zl * (1.0 - s)))
        t = dzl * g_ref[...]
        o_ref[...] = r * (t - jnp.mean(t, axis=-1, keepdims=True) - xh * jnp.mean(t * xh, axis=-1, keepdims=True))
        _acc_rows(dg_ref, i, jnp.sum(dzl * xh, axis=0, keepdims=True))
        _acc_rows(db_ref, i, jnp.sum(dzl, axis=0, keepdims=True))

    return pl.pallas_call(
        body, name=name, grid=(T // tm,), in_specs=[_rows(tm, C), _rows(tm, C), _vec(C), _vec(C)],
        out_specs=[_rows(tm, C), _vec(C), _vec(C)],
        out_shape=[jax.ShapeDtypeStruct((T, C), F32), jax.ShapeDtypeStruct((1, C), F32),
                   jax.ShapeDtypeStruct((1, C), F32)],
        compiler_params=_cp(("arbitrary",)))(dzs, zc, g, b)


def sconv_fwd(bgx, wc, name):
    T = bgx.shape[0]

    def body(b_ref, c_ref, x_ref, w_ref, o_ref, zpad):
        _zero_rows(zpad, 0, PAD)

        def fill(r0):
            zpad[pl.ds(r0 + PAD, RC), :] = c_ref[pl.ds(r0, RC), :].astype(F32) * x_ref[pl.ds(r0, RC), :].astype(F32)
        _chunks(T, fill)
        wrow = lambda j: w_ref[KD - 1 - j:KD - j, :]

        def step(r0):
            qc = _causal_conv(zpad, wrow, KD, r0)
            o_ref[pl.ds(r0, RC), :] = (b_ref[pl.ds(r0, RC), :].astype(F32) * qc).astype(o_ref.dtype)
        _chunks(T, step)

    return pl.pallas_call(
        body, name=name, grid=(NSLAB,),
        in_specs=[_col(T, 0), _col(T, 1), _col(T, 2), pl.BlockSpec((8, SLAB), lambda c: (0, c))],
        out_specs=pl.BlockSpec((T, SLAB), lambda c: (0, c)),
        out_shape=jax.ShapeDtypeStruct((T, DM), BF16),
        scratch_shapes=[pltpu.VMEM((T + PAD, SLAB), F32)],
        compiler_params=_cp(("parallel",)))(bgx, bgx, bgx, wc)


def sconv_bwd(dy, bgx, wc, name):
    T = bgx.shape[0]

    def body(d_ref, b_ref, c_ref, x_ref, w_ref, db_ref, dc_ref, dx_ref, dw_ref, zpad, gpad, dwacc):
        _zero_rows(zpad, 0, PAD)
        _zero_rows(gpad, T, PAD)
        dwacc[...] = jnp.zeros_like(dwacc)

        def fill(r0):
            zpad[pl.ds(r0 + PAD, RC), :] = c_ref[pl.ds(r0, RC), :].astype(F32) * x_ref[pl.ds(r0, RC), :].astype(F32)
            gpad[pl.ds(r0, RC), :] = d_ref[pl.ds(r0, RC), :].astype(F32) * b_ref[pl.ds(r0, RC), :].astype(F32)
        _chunks(T, fill)
        wrow = lambda j: w_ref[KD - 1 - j:KD - j, :]

        def step(r0):
            qc = _causal_conv(zpad, wrow, KD, r0)
            db_ref[pl.ds(r0, RC), :] = (d_ref[pl.ds(r0, RC), :].astype(F32) * qc).astype(db_ref.dtype)
            dq = _anticausal_conv(gpad, wrow, KD, r0)
            dc_ref[pl.ds(r0, RC), :] = (dq * x_ref[pl.ds(r0, RC), :].astype(F32)).astype(dc_ref.dtype)
            dx_ref[pl.ds(r0, RC), :] = (dq * c_ref[pl.ds(r0, RC), :].astype(F32)).astype(dx_ref.dtype)
            _conv_dw(gpad, zpad, dwacc, KD, r0)
        _chunks(T, step)
        dw_ref[...] = jnp.zeros_like(dw_ref)
        for k in range(KD):
            dw_ref[k:k + 1, :] = jnp.sum(dwacc[KD - 1 - k], axis=0, keepdims=True)

    w8 = pl.BlockSpec((8, SLAB), lambda c: (0, c))
    return pl.pallas_call(
        body, name=name, grid=(NSLAB,),
        in_specs=[pl.BlockSpec((T, SLAB), lambda c: (0, c)), _col(T, 0), _col(T, 1), _col(T, 2), w8],
        out_specs=[_col(T, 0), _col(T, 0), _col(T, 0), w8],
        out_shape=[jax.ShapeDtypeStruct((T, DM), BF16)] * 3 + [jax.ShapeDtypeStruct((8, DM), F32)],
        scratch_shapes=[pltpu.VMEM((T + PAD, SLAB), F32), pltpu.VMEM((T + PAD, SLAB), F32),
                        pltpu.VMEM((8, 8, SLAB), F32)],
        compiler_params=_cp(("parallel",)))(dy, bgx, bgx, bgx, wc)


def merge_cols(parts, name):
    T = parts[0].shape[0]
    n = len(parts)
    C = n * DM
    tm = min(512, T)

    def body(*refs):
        o_ref = refs[n]
        for j in range(n):
            o_ref[:, j * DM:(j + 1) * DM] = refs[j][...]

    return pl.pallas_call(
        body, name=name, grid=(T // tm,),
        in_specs=[_rows(tm, DM) for j in range(n)],
        out_specs=_rows(tm, C), out_shape=jax.ShapeDtypeStruct((T, C), parts[0].dtype),
        compiler_params=_cp(("parallel",)))(*parts)


def adamw(w, g, m, v, name):
    shape = w.shape
    R, C = shape[-2], shape[-1]
    L = w.size // (R * C)
    w2, g2, m2, v2 = (a.reshape(L, R, C) for a in (w, g, m, v))
    tr = R
    while tr * C > 512 * 1024 and tr % 16 == 0:
        tr //= 2
    bc1 = 1.0 - ADAM_B1 ** ADAM_STEP
    bc2 = 1.0 - ADAM_B2 ** ADAM_STEP

    def body(w_ref, g_ref, m_ref, v_ref, d_ref, nm_ref, nv_ref):
        gg = g_ref[...]
        nm = ADAM_B1 * m_ref[...] + (1.0 - ADAM_B1) * gg
        nv = ADAM_B2 * v_ref[...] + (1.0 - ADAM_B2) * (gg * gg)
        nm_ref[...] = nm
        nv_ref[...] = nv
        d_ref[...] = -ADAM_LR * ((nm / bc1) / (jnp.sqrt(nv / bc2) + ADAM_EPS) + ADAM_WD * w_ref[...])

    spec = pl.BlockSpec((None, tr, C), lambda l, i: (l, i, 0))
    outs = pl.pallas_call(
        body, name=name, grid=(L, R // tr), in_specs=[spec] * 4, out_specs=[spec] * 3,
        out_shape=[jax.ShapeDtypeStruct((L, R, C), F32)] * 3,
        compiler_params=_cp(("parallel", "parallel")))(w2, g2, m2, v2)
    return tuple(o.reshape(shape) for o in outs)


def _place():
    x, y, c = lax.axis_index("x"), lax.axis_index("y"), lax.axis_index("c")
    return x, y, c


def all_gather_chips(bufs, name):
    n = len(bufs)
    me_ = 2 * lax.axis_index("x") + lax.axis_index("y")
    slots = [lax.dynamic_update_slice(lax.empty((NCHIP,) + b.shape, b.dtype), b[None], (me_, 0, 0)) for b in bufs]

    def body(*refs):
        dst = refs[n:2 * n]
        send, recv = refs[2 * n:]
        x, y, c = _place()
        me = 2 * x + y
        sib = (x, y, 1 - c)
        chips = [(1 - x, y), (x, 1 - y), (1 - x, 1 - y)]

        def half(b, slot, hc):
            rows = bufs[b].shape[0] // 2
            return dst[b].at[slot, pl.ds(hc * rows, rows), :]

        def remote(k, s, d, to):
            return pltpu.make_async_remote_copy(src_ref=s, dst_ref=d, send_sem=send.at[k], recv_sem=recv.at[k],
                                                device_id=to, device_id_type=MESH)

        first = []
        for b in range(n):
            for j, (cx, cy) in enumerate(chips):
                first.append(remote(b * 6 + j, half(b, me, c), half(b, me, c), (cx, cy, c)))
        for cp in first:
            cp.start()
        passed = []
        for b in range(n):
            for j, (cx, cy) in enumerate(chips):
                slot = 2 * cx + cy
                remote(b * 6 + j, half(b, slot, c), half(b, slot, c), (cx, cy, c)).wait_recv()
                fwd = remote(b * 6 + 3 + j, half(b, slot, c), half(b, slot, c), sib)
                fwd.start()
                passed.append(fwd)
        for b in range(n):
            for j, (cx, cy) in enumerate(chips):
                slot = 2 * cx + cy
                remote(b * 6 + 3 + j, half(b, slot, 1 - c), half(b, slot, 1 - c), sib).wait_recv()
        for cp in first + passed:
            cp.wait_send()

    return pl.pallas_call(
        body, name=name, in_specs=[ANY] * n, out_specs=[ANY] * n,
        out_shape=[jax.ShapeDtypeStruct(s.shape, s.dtype) for s in slots],
        input_output_aliases={b: b for b in range(n)},
        scratch_shapes=[pltpu.SemaphoreType.DMA((6 * n,)), pltpu.SemaphoreType.DMA((6 * n,))],
        compiler_params=pltpu.CompilerParams())(*slots)


def pair_exchange(bufs, name):
    n = len(bufs)

    def body(*refs):
        src, dst = refs[:n], refs[n:2 * n]
        send, recv = refs[2 * n:]
        x, y, c = _place()
        cps = []
        for b in range(n):
            rows = bufs[b].shape[1] // 2
            cp = pltpu.make_async_remote_copy(
                src_ref=src[b].at[:, pl.ds((1 - c) * rows, rows), :], dst_ref=dst[b],
                send_sem=send.at[b], recv_sem=recv.at[b], device_id=(x, y, 1 - c), device_id_type=MESH)
            cp.start()
            cps.append(cp)
        for cp in cps:
            cp.wait()

    return pl.pallas_call(
        body, name=name, in_specs=[ANY] * n, out_specs=[ANY] * n,
        out_shape=[jax.ShapeDtypeStruct((NCHIP, b.shape[1] // 2, b.shape[2]), b.dtype) for b in bufs],
        scratch_shapes=[pltpu.SemaphoreType.DMA((n,)), pltpu.SemaphoreType.DMA((n,))],
        compiler_params=pltpu.CompilerParams())(*bufs)


def add_half(full, got, tr, tc, name):
    _, R, C = full.shape
    rows = R // 2
    nr = rows // tr
    c_arr = lax.axis_index("c").astype(jnp.int32).reshape(1)

    def body(c_ref, a_ref, b_ref, o_ref):
        o_ref[...] = (a_ref[...].astype(F32) + b_ref[...].astype(F32)).astype(o_ref.dtype)

    return pl.pallas_call(
        body, name=name,
        grid_spec=pltpu.PrefetchScalarGridSpec(
            num_scalar_prefetch=1, grid=(NCHIP, nr, C // tc),
            in_specs=[pl.BlockSpec((None, tr, tc), lambda s, i, j, c_ref: (s, c_ref[0] * nr + i, j)),
                      pl.BlockSpec((None, tr, tc), lambda s, i, j, c_ref: (s, i, j))],
            out_specs=pl.BlockSpec((None, tr, tc), lambda s, i, j, c_ref: (s, i, j))),
        out_shape=jax.ShapeDtypeStruct((NCHIP, rows, C), full.dtype),
        compiler_params=_cp(("parallel", "parallel", "parallel")))(c_arr, full, got)


def chip_exchange(bufs, name):
    n = len(bufs)

    def body(*refs):
        src, dst = refs[:n], refs[n:2 * n]
        send, recv, lsem = refs[2 * n:]
        x, y, c = _place()
        me = 2 * x + y
        chips = [(1 - x, y), (x, 1 - y), (1 - x, 1 - y)]
        local = [pltpu.make_async_copy(src[b].at[me], dst[b].at[me], lsem.at[b]) for b in range(n)]
        for cp in local:
            cp.start()
        cps = []
        for b in range(n):
            for j, (cx, cy) in enumerate(chips):
                cp = pltpu.make_async_remote_copy(
                    src_ref=src[b].at[2 * cx + cy], dst_ref=dst[b].at[me],
                    send_sem=send.at[b * 3 + j], recv_sem=recv.at[b * 3 + j],
                    device_id=(cx, cy, c), device_id_type=MESH)
                cp.start()
                cps.append((cp, b, cx, cy, j))
        for cp, b, cx, cy, j in cps:
            cp.wait_send()
            pltpu.make_async_remote_copy(
                src_ref=src[b].at[me], dst_ref=dst[b].at[2 * cx + cy],
                send_sem=send.at[b * 3 + j], recv_sem=recv.at[b * 3 + j],
                device_id=(cx, cy, c), device_id_type=MESH).wait_recv()
        for cp in local:
            cp.wait()

    return pl.pallas_call(
        body, name=name, in_specs=[ANY] * n, out_specs=[ANY] * n,
        out_shape=[jax.ShapeDtypeStruct(b.shape, b.dtype) for b in bufs],
        scratch_shapes=[pltpu.SemaphoreType.DMA((3 * n,)), pltpu.SemaphoreType.DMA((3 * n,)),
                        pltpu.SemaphoreType.DMA((n,))],
        compiler_params=pltpu.CompilerParams())(*bufs)


def sum_slots(buf, tr, tc, name):
    _, r, C = buf.shape
    nr = r // tr
    c_arr = lax.axis_index("c").astype(jnp.int32).reshape(1)

    def body(c_ref, a_ref, o_ref):
        o_ref[...] = ((a_ref[0].astype(F32) + a_ref[1].astype(F32)) + a_ref[2].astype(F32)) + a_ref[3].astype(F32)

    return pl.pallas_call(
        body, name=name,
        grid_spec=pltpu.PrefetchScalarGridSpec(
            num_scalar_prefetch=1, grid=(nr, C // tc),
            in_specs=[pl.BlockSpec((NCHIP, tr, tc), lambda i, j, c_ref: (0, i, j))],
            out_specs=pl.BlockSpec((tr, tc), lambda i, j, c_ref: (c_ref[0] * nr + i, j))),
        out_shape=jax.ShapeDtypeStruct((2 * r, C), F32),
        compiler_params=_cp(("parallel", "parallel")))(c_arr, buf)


def pair_share(bufs, name):
    n = len(bufs)

    def body(*refs):
        dst = refs[n:2 * n]
        send, recv = refs[2 * n:]
        x, y, c = _place()
        cps = []
        for b in range(n):
            rows = bufs[b].shape[0] // 2
            here = dst[b].at[pl.ds(c * rows, rows), :]
            cp = pltpu.make_async_remote_copy(src_ref=here, dst_ref=here, send_sem=send.at[b], recv_sem=recv.at[b],
                                              device_id=(x, y, 1 - c), device_id_type=MESH)
            cp.start()
            cps.append((cp, b))
        for cp, b in cps:
            rows = bufs[b].shape[0] // 2
            there = dst[b].at[pl.ds((1 - c) * rows, rows), :]
            cp.wait_send()
            pltpu.make_async_remote_copy(src_ref=there, dst_ref=there, send_sem=send.at[b], recv_sem=recv.at[b],
                                         device_id=(x, y, 1 - c), device_id_type=MESH).wait_recv()

    return pl.pallas_call(
        body, name=name, in_specs=[ANY] * n, out_specs=[ANY] * n,
        out_shape=[jax.ShapeDtypeStruct(b.shape, b.dtype) for b in bufs],
        input_output_aliases={b: b for b in range(n)},
        scratch_shapes=[pltpu.SemaphoreType.DMA((n,)), pltpu.SemaphoreType.DMA((n,))],
        compiler_params=pltpu.CompilerParams())(*bufs)


def _tile(kind, buf):
    return {"A": (256, buf.shape[2]), "B": (buf.shape[1], 1024), "C": (128, 256), "V": (40, 256),
            "E": (40, 1024)}[kind]


def reduce_scatter(parts, tag):
    names = list(parts)
    got = pair_exchange([parts[k] for k in names], tag + "_pair_exchange")
    sums = [add_half(parts[k], got[i], *_tile(k[0], got[i]), name=tag + "_add_pair_" + k) for i, k in enumerate(names)]
    landed = chip_exchange(sums, tag + "_chip_exchange")
    halves = [sum_slots(landed[i], *_tile(k[0], landed[i]), name=tag + "_sum_chips_" + k) for i, k in enumerate(names)]
    full = pair_share(halves, tag + "_pair_share")
    return dict(zip(names, full))


HBM = pl.BlockSpec(memory_space=pltpu.HBM)
SEMS = pl.BlockSpec(memory_space=pltpu.SEMAPHORE)
FLOWS = pltpu.SideEffectType.DATAFLOW_SIDE_EFFECTING


def _in_hbm(a):
    return pltpu.with_memory_space_constraint(a, pltpu.HBM)


def _other_chips():
    x, y, c = _place()
    return 2 * x + y, c, [(1 - x, y), (x, 1 - y), (1 - x, 1 - y)]


def own_slots(bufs):
    me = 2 * lax.axis_index("x") + lax.axis_index("y")
    return [lax.dynamic_update_slice(lax.empty((NCHIP,) + b.shape, b.dtype), b[None], (me, 0, 0)) for b in bufs]


def gather_start(slots, after, name):
    n = len(slots)

    def body(*refs):
        ins = refs[:n]
        send, recv = refs[n + 1], refs[n + 2]
        token = refs[2 * n + 3]
        me, c, chips = _other_chips()
        for b in range(n):
            rows = slots[b].shape[1] // 2
            own = ins[b].at[me, pl.ds(c * rows, rows), :]
            for j, (cx, cy) in enumerate(chips):
                pltpu.make_async_remote_copy(src_ref=own, dst_ref=own, send_sem=send.at[3 * b + j],
                                             recv_sem=recv.at[3 * b + j], device_id=(cx, cy, c),
                                             device_id_type=MESH).start()
        token[...] = jnp.zeros_like(token)

    out = pl.pallas_call(
        body, name=name, in_specs=[HBM] * n + [ANY],
        out_specs=[SEMS, SEMS] + [HBM] * n + [pl.BlockSpec(memory_space=pltpu.VMEM)],
        out_shape=[pltpu.SemaphoreType.DMA((3 * n,)), pltpu.SemaphoreType.DMA((3 * n,))]
        + [pltpu.HBM(s.shape, s.dtype) for s in slots] + [jax.ShapeDtypeStruct((8, 128), F32)],
        input_output_aliases={b: b + 2 for b in range(n)},
        compiler_params=pltpu.CompilerParams(has_side_effects=FLOWS))(*[_in_hbm(s) for s in slots], after)
    return out[0], out[1], list(out[2:2 + n]), out[2 + n]


def gather_wait(send, recv, slots, picks, after, name):
    n = len(slots)

    def body(*refs):
        ins = refs[:n]
        send_, recv_ = refs[n], refs[n + 1]
        me, c, chips = _other_chips()
        for i, b in enumerate(picks):
            rows = slots[i].shape[1] // 2
            own = ins[i].at[me, pl.ds(c * rows, rows), :]
            for j, (cx, cy) in enumerate(chips):
                got = ins[i].at[2 * cx + cy, pl.ds(c * rows, rows), :]
                pltpu.make_async_remote_copy(src_ref=own, dst_ref=own, send_sem=send_.at[3 * b + j],
                                             recv_sem=recv_.at[3 * b + j], device_id=(cx, cy, c),
                                             device_id_type=MESH).wait_send()
                pltpu.make_async_remote_copy(src_ref=got, dst_ref=got, send_sem=send_.at[3 * b + j],
                                             recv_sem=recv_.at[3 * b + j], device_id=(cx, cy, c),
                                             device_id_type=MESH).wait_recv()

    return pl.pallas_call(
        body, name=name, in_specs=[HBM] * n + [SEMS, SEMS, ANY], out_specs=[HBM] * n,
        out_shape=[pltpu.HBM(s.shape, s.dtype) for s in slots],
        input_output_aliases={b: b for b in range(n)},
        compiler_params=pltpu.CompilerParams(has_side_effects=FLOWS))(*slots, send, recv, after)


def gather_pass(slots, name):
    n = len(slots)

    def body(*refs):
        dst = refs[n:2 * n]
        send, recv = refs[2 * n:]
        x, y, c = _place()
        sib = (x, y, 1 - c)
        chips = [(1 - x, y), (x, 1 - y), (1 - x, 1 - y)]

        def half(b, slot, hc):
            rows = slots[b].shape[1] // 2
            return dst[b].at[slot, pl.ds(hc * rows, rows), :]

        passed = []
        for b in range(n):
            for j, (cx, cy) in enumerate(chips):
                slot = 2 * cx + cy
                cp = pltpu.make_async_remote_copy(src_ref=half(b, slot, c), dst_ref=half(b, slot, c),
                                                  send_sem=send.at[3 * b + j], recv_sem=recv.at[3 * b + j],
                                                  device_id=sib, device_id_type=MESH)
                cp.start()
                passed.append(cp)
        for b in range(n):
            for j, (cx, cy) in enumerate(chips):
                slot = 2 * cx + cy
                pltpu.make_async_remote_copy(src_ref=half(b, slot, 1 - c), dst_ref=half(b, slot, 1 - c),
                                             send_sem=send.at[3 * b + j], recv_sem=recv.at[3 * b + j],
                                             device_id=sib, device_id_type=MESH).wait_recv()
        for cp in passed:
            cp.wait_send()

    return pl.pallas_call(
        body, name=name, in_specs=[ANY] * n, out_specs=[ANY] * n,
        out_shape=[jax.ShapeDtypeStruct(s.shape, s.dtype) for s in slots],
        input_output_aliases={b: b for b in range(n)},
        scratch_shapes=[pltpu.SemaphoreType.DMA((3 * n,)), pltpu.SemaphoreType.DMA((3 * n,))],
        compiler_params=pltpu.CompilerParams())(*slots)


def chip_exchange_start(sums, name):
    n = len(sums)
    me_ = 2 * lax.axis_index("x") + lax.axis_index("y")
    landing = [lax.dynamic_update_slice(lax.empty(s.shape, s.dtype),
                                        lax.dynamic_slice(s, (me_, 0, 0), (1,) + s.shape[1:]), (me_, 0, 0)) for s in sums]

    def body(*refs):
        src, land = refs[:n], refs[n:2 * n]
        send, recv = refs[2 * n], refs[2 * n + 1]
        token = refs[4 * n + 2]
        me, c, chips = _other_chips()
        for b in range(n):
            for j, (cx, cy) in enumerate(chips):
                pltpu.make_async_remote_copy(src_ref=src[b].at[2 * cx + cy], dst_ref=land[b].at[me],
                                             send_sem=send.at[3 * b + j], recv_sem=recv.at[3 * b + j],
                                             device_id=(cx, cy, c), device_id_type=MESH).start()
        token[...] = jnp.zeros_like(token)

    out = pl.pallas_call(
        body, name=name, in_specs=[HBM] * (2 * n),
        out_specs=[SEMS, SEMS] + [HBM] * (2 * n) + [pl.BlockSpec(memory_space=pltpu.VMEM)],
        out_shape=[pltpu.SemaphoreType.DMA((3 * n,)), pltpu.SemaphoreType.DMA((3 * n,))]
        + [pltpu.HBM(s.shape, s.dtype) for s in sums + landing] + [jax.ShapeDtypeStruct((8, 128), F32)],
        input_output_aliases={b: b + 2 for b in range(2 * n)},
        compiler_params=pltpu.CompilerParams(has_side_effects=FLOWS))(*[_in_hbm(s) for s in sums + landing])
    return out[0], out[1], list(out[2:2 + n]), list(out[2 + n:2 + 2 * n]), out[2 + 2 * n]


def chip_exchange_wait(send, recv, sums, landing, after, name):
    n = len(sums)

    def body(*refs):
        src, land = refs[:n], refs[n:2 * n]
        send_, recv_ = refs[2 * n], refs[2 * n + 1]
        me, c, chips = _other_chips()
        for b in range(n):
            for j, (cx, cy) in enumerate(chips):
                slot = 2 * cx + cy
                pltpu.make_async_remote_copy(src_ref=src[b].at[slot], dst_ref=land[b].at[me],
                                             send_sem=send_.at[3 * b + j], recv_sem=recv_.at[3 * b + j],
                                             device_id=(cx, cy, c), device_id_type=MESH).wait_send()
                pltpu.make_async_remote_copy(src_ref=src[b].at[me], dst_ref=land[b].at[slot],
                                             send_sem=send_.at[3 * b + j], recv_sem=recv_.at[3 * b + j],
                                             device_id=(cx, cy, c), device_id_type=MESH).wait_recv()

    out = pl.pallas_call(
        body, name=name, in_specs=[HBM] * (2 * n) + [SEMS, SEMS, ANY], out_specs=[HBM] * (2 * n),
        out_shape=[pltpu.HBM(s.shape, s.dtype) for s in sums + landing],
        input_output_aliases={b: b for b in range(2 * n)},
        compiler_params=pltpu.CompilerParams(has_side_effects=FLOWS))(*sums, *landing, send, recv, after)
    return list(out[n:])


def _row(a, l):
    return a[l:l + 1]


def local_step(x, p, tgt, small, weights_of, vecs, a_ws, a_bs, grads_ready):
    T = x.shape[0]
    bsf = jnp.broadcast_to(a_bs[:, :, None], (AG_N, CHUNK, AGW))
    vrow = lambda r: vecs[r:r + 1]
    saved = []
    W = []
    h = x
    GA1 = GB1 = GA = GB = GC = bgrp = None

    def ff_fwd(h, xn, l, which, post, g_next, tok=None):
        wa, wb = (GA1, GB1) if which == 1 else (GA, GB)
        tag = "ff%d_l%d" % (which, l)
        gu, a = ff_gateup(xn, wa, 0, tag + "_gateup")
        gp = _row(post, l) if tok is None else _row(post, l) + tok
        out = mm_rs_post(a, wb, 0, FB, FB, h, gp, 0.5, tag + "_down", g_next=g_next)
        return out[1], (out[2] if g_next is not None else None), (h, xn, gu, a, out[0])

    xn = rms_fwd(h, _row(small["ff1_pre_g"], 0), BF16, "ff1_l0_pre")
    for l in range(4):
        rec = {}
        GA1, GB1, atok = weights_of(l, "a", h)
        g_mix = _row(small["mix_pre_g"], l) if l >= 2 else None
        h, hn, rec["ff1"] = ff_fwd(h, xn, l, 1, small["ff1_post_g"], g_mix, atok)
        GA, GB, GC, wtok = weights_of(l, "b", h)
        W.append((GA1, GB1, GA, GB, GC))
        if l == 1:
            bgrp = GC[:, C_BGRP:C_BGRP + 256, :].reshape(NCHIP, 4, 64, 256).transpose(1, 0, 2, 3).reshape(4, 256, 256)
        tag = "mix_l%d" % l
        h_in = h
        g_ff2 = _row(small["ff2_pre_g"], l)
        if l == 1:
            hn = rms_fwd(h, _row(small["mix_pre_g"], l), F32, tag + "_pre")
            pooled, ypre, f = pool_fwd(hn, bgrp, vrow(V_BSCALE), tag + "_pool")
            rec["mix"] = (h_in, pooled, ypre, f)
            h = post_res(h, f, _row(small["mix_post_g"], l), 1.0, tag + "_post")
            xn = rms_fwd(h, g_ff2, BF16, "ff2_l1_pre")
        else:
            gpost = _row(small["mix_post_g"], l)
            if l == 0:
                g0 = _row(small["mix_pre_g"], l)
                hn = rms_fwd(h, g0 if wtok is None else g0 + wtok, BF16, tag + "_pre")
                zpre = mm_cs(hn, GA, A_AIN, 1536, 1536, BF16, tag + "_in")
                y = gmlp_mid_fwd(zpre, small["a_v_norm_g"], small["a_v_norm_b"], a_ws, bsf, tag + "_gate")
                f, h, xn = mm_rs_post(y, GB, B_AOUT, 768, 768, h, gpost, 1.0, tag + "_out", g_next=g_ff2)
                rec["mix"] = (h_in, hn, zpre, y, f)
            elif l == 2:
                ag = mm_cs(hn, GA, A_CIN, 512, 512, BF16, tag + "_pw1")
                zc = conf_conv_fwd(ag, vecs[V_CDW:V_CDW + 32], vrow(V_CBDW), tag + "_conv")
                zs = conf_ln_fwd(zc, vrow(V_CNG), vrow(V_CNB), tag + "_ln")
                f, h, xn = mm_rs_post(zs, GB, B_CPW2, 256, 256, h, gpost, 1.0, tag + "_pw2", g_next=g_ff2)
                rec["mix"] = (h_in, hn, ag, zc, zs, f)
            else:
                bgx = mm_cs(hn, GA, A_DIN, 768, 768, BF16, tag + "_in")
                y = sconv_fwd(bgx, vecs[V_DCONV:V_DCONV + 8], tag + "_conv")
                f, h, xn = mm_rs_post(y, GB, B_DOUT, 256, 256, h, gpost, 1.0, tag + "_out", g_next=g_ff2)
                rec["mix"] = (h_in, hn, bgx, y, f)
        h, xn, rec["ff2"] = ff_fwd(h, xn, l, 2, small["ff2_post_g"], _row(small["ple_gate_norm_g"], l))
        tag = "ple_l%d" % l
        pb = p[l].astype(BF16)
        h_in = h
        g_next = _row(small["ff1_pre_g"], l + 1) if l < 3 else None
        out = ple_fwd(h, xn, pb, GB, B_PLEG(l), GC, _row(small["ple_post_g"], l), tag, g_next=g_next)
        rec["ple"] = (h_in, xn, out[0], out[1], pb)
        h = out[2]
        xn = out[3] if l < 3 else None
        saved.append(rec)

    dh, loss_cols = loss_head(h, tgt, "loss_head")

    dA2 = dB2 = None
    layer_grads = [None] * 4
    tok = None
    gV = {}
    gains = {k: [None] * 4 for k in ("ff1_pre_g", "ff1_post_g", "mix_pre_g", "mix_post_g", "ff2_pre_g", "ff2_post_g",
                                      "ple_gate_norm_g", "ple_post_g")}
    extra = {}

    def ff_bwd(dh, l, which, pre, post, rec, after=None):
        wa, wb = (GA1, GB1) if which == 1 else (GA, GB)
        tag = "ff%d_l%d_b" % (which, l)
        h_in, xn, gu, a, f = rec
        gp = _row(post, l) if after is None else _row(post, l) + after
        df, dpost, dgu = ff_bwd_down(dh, f, gp, wb, 0, gu, tag + "_down")
        if which == 1:
            db = dw_rs(a, df, FB, FB, tag + "_dwdown")
        else:
            db = dw_rs(a, df, FB, FB, tag + "_dwdown", height=B2_ROWS(l), off=B_FF2D(l), into=dB2)
        dh_in, dpre = mm_cs_t_rms(dgu, wa, 0, 2 * FB, 2 * FB, h_in, _row(pre, l), dh, tag + "_gateup")
        da = dw_cs(xn, dgu, 2 * FB, 2 * FB, tag + "_dwgateup", width=None if which == 1 else A2_COLS(l))
        return dh_in, dpre, dpost, (da, db)

    for l in reversed(range(4)):
        rec = saved[l]
        GA1, GB1, GA, GB, GC = W[l]
        gC = {}
        tag = "ple_l%d_b" % l
        h_in, xn, zg, pe, pb = rec["ple"]
        gpost = _row(small["ple_post_g"], l)
        if tok is not None:
            gpost = gpost + tok
        dzg, dpe, dh, gains["ple_post_g"][l], gains["ple_gate_norm_g"][l] = ple_bwd(
            dh, zg, pe, h_in, GB, B_PLEG(l), gpost, _row(small["ple_gate_norm_g"], l), tag)
        gC[C_PROJ(l)] = dw_cs(pb, dpe, 256, 256, tag + "_dwproj")
        dB2 = dw_rs(xn, dzg, 256, 256, tag + "_dwgate", height=B2_ROWS(l), off=B_PLEG(l))

        dh, gains["ff2_pre_g"][l], gains["ff2_post_g"][l], (dA2, dB2) = ff_bwd(
            dh, l, 2, small["ff2_pre_g"], small["ff2_post_g"], rec["ff2"])

        tag = "mix_l%d_b" % l
        mix = rec["mix"]
        h_in, f = mix[0], mix[-1]
        if l == 1:
            _, pooled, ypre, _ = mix
            df, gains["mix_post_g"][l] = post_res_bwd(dh, f, _row(small["mix_post_g"], l), 1.0, F32, tag + "_post")
            dhn, dwg, dsc = pool_bwd(df, ypre, pooled, bgrp, vrow(V_BSCALE), tag + "_pool")
            gC[C_BGRP] = dwg.astype(BF16).reshape(4, NCHIP, 64, 256).transpose(1, 0, 2, 3).reshape(NCHIP, 256, 256)
            gV[V_BSCALE] = jnp.pad(dsc, ((0, 7), (0, 0)))
            dh, gains["mix_pre_g"][l] = rms_bwd(dhn, h_in, _row(small["mix_pre_g"], l), dh, tag + "_pre")
        else:
            gpre = _row(small["mix_pre_g"], l)
            df, gains["mix_post_g"][l] = post_res_bwd(dh, f, _row(small["mix_post_g"], l), 1.0, BF16, tag + "_post")
            if l == 0:
                _, hn, zpre, y, _ = mix
                dy = mm_rs_t(df, GB, B_AOUT, 768, 768, tag + "_out")
                dB2 = dw_rs(y, df, 768, 768, tag + "_dwout", height=B2_ROWS(l), off=B_AOUT, into=dB2)
                dz, dws, dbs, dvg, dvb = gmlp_mid_bwd(zpre, dy, small["a_v_norm_g"], small["a_v_norm_b"], a_ws, bsf,
                                                      tag + "_gate")
                extra.update(a_w_s=dws, a_b_s=dbs[:, 0, :], a_v_norm_g=dvg, a_v_norm_b=dvb)
                dA2 = dw_cs(hn, dz, 1536, 1536, tag + "_dwin", width=A2_COLS(l), off=A_AIN, into=dA2)
                dh, gains["mix_pre_g"][l] = mm_cs_t_rms(dz, GA, A_AIN, 1536, 1536, h_in, gpre, dh, tag + "_in")
            elif l == 2:
                _, hn, ag, zc, zs, _ = mix
                dzs = mm_rs_t(df, GB, B_CPW2, 256, 256, tag + "_pw2")
                dB2 = dw_rs(zs, df, 256, 256, tag + "_dwpw2", height=B2_ROWS(l), off=B_CPW2, into=dB2)
                dzc, dng, dnb = conf_ln_bwd(dzs, zc, vrow(V_CNG), vrow(V_CNB), tag + "_ln")
                da_, dg_, dwdw, dbdw = conf_conv_bwd(dzc, ag, vecs[V_CDW:V_CDW + 32], tag + "_conv")
                dag = merge_cols([da_, dg_], tag + "_merge")
                gV[V_CDW] = dwdw
                gV[V_CBDW] = jnp.pad(dbdw, ((0, 7), (0, 0)))
                gV[V_CNG] = jnp.pad(dng, ((0, 7), (0, 0)))
                gV[V_CNB] = jnp.pad(dnb, ((0, 7), (0, 0)))
                dA2 = dw_cs(hn, dag, 512, 512, tag + "_dwpw1", width=A2_COLS(l), off=A_CIN, into=dA2)
                dh, gains["mix_pre_g"][l] = mm_cs_t_rms(dag, GA, A_CIN, 512, 512, h_in, gpre, dh, tag + "_pw1")
            else:
                _, hn, bgx, y, _ = mix
                dy = mm_rs_t(df, GB, B_DOUT, 256, 256, tag + "_out")
                dB2 = dw_rs(y, df, 256, 256, tag + "_dwout", height=B2_ROWS(l), off=B_DOUT, into=dB2)
                db_, dc_, dx_, dwc = sconv_bwd(dy, bgx, vecs[V_DCONV:V_DCONV + 8], tag + "_conv")
                dbgx = merge_cols([db_, dc_, dx_], tag + "_merge")
                gV[V_DCONV] = dwc
                dA2 = dw_cs(hn, dbgx, 768, 768, tag + "_dwin", width=A2_COLS(l), off=A_DIN, into=dA2)
                dh, gains["mix_pre_g"][l] = mm_cs_t_rms(dbgx, GA, A_DIN, 768, 768, h_in, gpre, dh, tag + "_in")

        dC = jnp.concatenate([gC[C_PROJ(l)]] + ([gC[C_BGRP]] if l == 1 else []), axis=1)
        tok = grads_ready(l, "b", (dA2, dB2, dC), dh)
        dh, gains["ff1_pre_g"][l], gains["ff1_post_g"][l], (dA1, dB1) = ff_bwd(
            dh, l, 1, small["ff1_pre_g"], small["ff1_post_g"], rec["ff1"], after=tok)
        layer_grads[l] = (dA1, dB1, dA2, dB2, dC)
        tok = grads_ready(l, "a", (dA1, dB1), dh)

    return loss_cols, dh, layer_grads, gV, gains, extra


GAIN_NAMES = ("ff1_pre_g", "ff1_post_g", "mix_pre_g", "mix_post_g", "ff2_pre_g", "ff2_post_g", "ple_gate_norm_g",
              "ple_post_g")


def _pad_rows(a, rows):
    return jnp.pad(a, ((0, rows - a.shape[0]), (0, 0)))


def kernel(x, p, ff1_pre_g, ff1_w_gate, ff1_w_up, ff1_w_down, ff1_post_g, mix_pre_g, mix_post_g, ff2_pre_g, ff2_w_gate, ff2_w_up, ff2_w_down, ff2_post_g, ple_gate_norm_g, ple_w_gate, ple_w_proj, ple_post_g, a_w_in, a_v_norm_g, a_v_norm_b, a_w_s, a_b_s, a_w_out, b_w_grp, b_scale, c_w_pw1, c_w_dw, c_b_dw, c_norm_g, c_norm_b, c_w_pw2, d_w_in, d_w_conv, d_w_out, loss_target, m_ff1_pre_g, m_ff1_w_gate, m_ff1_w_up, m_ff1_w_down, m_ff1_post_g, m_mix_pre_g, m_mix_post_g, m_ff2_pre_g, m_ff2_w_gate, m_ff2_w_up, m_ff2_w_down, m_ff2_post_g, m_ple_gate_norm_g, m_ple_w_gate, m_ple_w_proj, m_ple_post_g, m_a_w_in, m_a_v_norm_g, m_a_v_norm_b, m_a_w_s, m_a_b_s, m_a_w_out, m_b_w_grp, m_b_scale, m_c_w_pw1, m_c_w_dw, m_c_b_dw, m_c_norm_g, m_c_norm_b, m_c_w_pw2, m_d_w_in, m_d_w_conv, m_d_w_out, v_ff1_pre_g, v_ff1_w_gate, v_ff1_w_up, v_ff1_w_down, v_ff1_post_g, v_mix_pre_g, v_mix_post_g, v_ff2_pre_g, v_ff2_w_gate, v_ff2_w_up, v_ff2_w_down, v_ff2_post_g, v_ple_gate_norm_g, v_ple_w_gate, v_ple_w_proj, v_ple_post_g, v_a_w_in, v_a_v_norm_g, v_a_v_norm_b, v_a_w_s, v_a_b_s, v_a_w_out, v_b_w_grp, v_b_scale, v_c_w_pw1, v_c_w_dw, v_c_b_dw, v_c_norm_g, v_c_norm_b, v_c_w_pw2, v_d_w_in, v_d_w_conv, v_d_w_out):
    args = dict(locals())
    wnames = ["ff1_pre_g", "ff1_w_gate", "ff1_w_up", "ff1_w_down", "ff1_post_g", "mix_pre_g", "mix_post_g",
              "ff2_pre_g", "ff2_w_gate", "ff2_w_up", "ff2_w_down", "ff2_post_g", "ple_gate_norm_g", "ple_w_gate",
              "ple_w_proj", "ple_post_g", "a_w_in", "a_v_norm_g", "a_v_norm_b", "a_w_s", "a_b_s", "a_w_out",
              "b_w_grp", "b_scale", "c_w_pw1", "c_w_dw", "c_b_dw", "c_norm_g", "c_norm_b", "c_w_pw2", "d_w_in",
              "d_w_conv", "d_w_out"]

    P = pack_weights(args)
    G0a = all_gather_chips([P[0][0], P[0][1], P[4]], "gather_l0a")
    vecs = G0a[2].transpose(1, 0, 2).reshape(V_ROWS, DM)
    flying = {}

    def start(key, bufs, after):
        send, recv, slots, token = gather_start(own_slots(list(bufs)), after, "gather_start_l" + key)
        flying[key] = (send, recv, slots)
        return token[0, 0]

    tok = start("0b", P[0][2:], G0a[0])
    arrived = {}

    def weights_of(l, part, h):
        if l == 0 and part == "a":
            return G0a[0], G0a[1], None
        key = "0b" if l == 0 else str(l)
        wtok = None
        if key not in arrived:
            send, recv, slots = flying[key]
            n = len(slots)
            landed = gather_wait(send, recv, slots, list(range(n)), h, "gather_wait_l" + key)
            arrived[key] = gather_pass(landed, "gather_pass_l" + key)
            if l < 3:
                wtok = start(str(l + 1), P[l + 1], arrived[key][0])
        got = arrived[key]
        if l == 0:
            return tuple(got) + (wtok,)
        return tuple(got[:2]) + (wtok,) if part == "a" else tuple(got[2:]) + (None,)

    pending = {}
    reduced = {}
    held = {}

    def finish(key, after):
        kinds, send, recv, sums, landing = pending.pop(key)
        landed = chip_exchange_wait(send, recv, sums, landing, after, "rs_wait_l" + key)
        halves = [sum_slots(landed[i], *_tile(k, landed[i]), name="rs_sum_chips_l%s_%d%s" % (key, i, k))
                  for i, k in enumerate(kinds)]
        reduced[key] = pair_share(halves, "rs_pair_share_l" + key)

    def grads_ready(l, part, bufs, dh):
        if part == "b" and l > 0:
            held[l] = list(bufs)
            return None
        if part == "a" and l > 0:
            key, kinds, parts = str(l), "ABABC", list(bufs) + held.pop(l)
        elif part == "b":
            key, kinds, parts = "0b", "ABC", list(bufs)
        else:
            finish("0b", dh)
            return None
        for other in list(pending):
            finish(other, dh)
        got = pair_exchange(parts, "rs_pair_exchange_l" + key)
        sums = [add_half(parts[i], got[i], *_tile(k, got[i]), name="rs_add_pair_l%s_%d%s" % (key, i, k))
                for i, k in enumerate(kinds)]
        send, recv, sums, landing, token = chip_exchange_start(sums, "rs_start_l" + key)
        pending[key] = (kinds, send, recv, sums, landing)
        return token[0, 0]

    small = {k: args[k] for k in GAIN_NAMES}
    small["ff1_pre_g"] = ff1_pre_g + tok
    small["a_v_norm_g"] = a_v_norm_g
    small["a_v_norm_b"] = a_v_norm_b
    loss_cols, grad_x, layer_grads, gV, gains, extra = local_step(
        x[0], p[:, 0], loss_target[0], small, weights_of, vecs, a_w_s[0], a_b_s[0], grads_ready)

    loss = lax.psum((0.5 / DM) * jnp.sum(loss_cols), ("x", "y", "c"))

    deltas, new_m, new_v = {}, {}, {}

    def update(k, g):
        if args[k].shape[-1] == FW:
            t = lambda a: jnp.swapaxes(a, 1, 2)
            outs = adamw(t(args[k]), t(g), t(args["m_" + k]), t(args["v_" + k]), "adamw_" + k)
            deltas[k], new_m[k], new_v[k] = (t(o) for o in outs)
        else:
            deltas[k], new_m[k], new_v[k] = adamw(args[k], g, args["m_" + k], args["v_" + k], "adamw_" + k)

    dV, dE = pack_small_grads(gV, gains, extra)
    red = reduce_scatter({"A": layer_grads[0][0], "B": layer_grads[0][1], "V": dV, "E": dE}, "rs_l0a")
    (gE,) = all_gather_chips([red["E"]], "gather_replicated_grads")
    per_layer = [[red["A"], red["B"]] + list(reduced["0b"])] + [list(reduced[str(l)]) for l in (1, 2, 3)]
    grads = unpack_grads(per_layer, red["V"], gE.reshape(E_ROWS, DM))
    for k in wnames:
        update(k, grads[k])
    return (loss, grad_x[None], *[grads[k] for k in wnames], *[deltas[k] for k in wnames],
            *[new_m[k] for k in wnames], *[new_v[k] for k in wnames])


def pack_weights(w):
    padc = lambda a: jnp.pad(a, ((0, 0), (0, FB - FW)))
    mix_in = [w["a_w_in"][0], None, w["c_w_pw1"][0], w["d_w_in"][0]]
    mix_out = [w["a_w_out"][0], None, w["c_w_pw2"][0], w["d_w_out"][0]]
    packed = []
    for l in range(4):
        a1 = jnp.concatenate([padc(w["ff1_w_gate"][l]), padc(w["ff1_w_up"][l])], axis=1).astype(BF16)
        b1 = _pad_rows(w["ff1_w_down"][l], FB).astype(BF16)
        cols = [padc(w["ff2_w_gate"][l]), padc(w["ff2_w_up"][l])]
        rows = [_pad_rows(w["ff2_w_down"][l], FB)]
        if l != 1:
            cols.append(mix_in[l])
            rows.append(mix_out[l])
        rows.append(w["ple_w_gate"][l])
        proj = [w["ple_w_proj"][l]] + ([w["b_w_grp"][0].reshape(256, 256)] if l == 1 else [])
        packed.append((a1, b1, jnp.concatenate(cols, axis=1).astype(BF16), jnp.concatenate(rows, axis=0).astype(BF16),
                       jnp.concatenate(proj, axis=0).astype(BF16)))
    PV = jnp.concatenate([_pad_rows(w["b_scale"], 8), _pad_rows(w["c_b_dw"], 8), _pad_rows(w["c_norm_g"], 8),
                          _pad_rows(w["c_norm_b"], 8), _pad_rows(w["d_w_conv"][0], 8), _pad_rows(w["c_w_dw"][0], 40)],
                         axis=0)
    return packed + [PV]


def pack_small_grads(gV, gains, extra):
    dVt = jnp.concatenate([gV[V_BSCALE], gV[V_CBDW], gV[V_CNG], gV[V_CNB], gV[V_DCONV], gV[V_CDW],
                           jnp.zeros((8, DM), F32)], axis=0)
    dV = dVt.reshape(V_ROWS, NCHIP, 256).transpose(1, 0, 2)
    rowsE = [_pad_rows(jnp.concatenate(gains[k], axis=0), 8) for k in GAIN_NAMES]
    rowsE += [_pad_rows(extra["a_v_norm_g"].reshape(3, DM), 8), _pad_rows(extra["a_v_norm_b"].reshape(3, DM), 8),
              jnp.pad(extra["a_b_s"].reshape(1536), (0, 8 * DM - 1536)).reshape(8, DM),
              extra["a_w_s"].reshape(192, DM)]
    dE = _pad_rows(jnp.concatenate(rowsE, axis=0), E_ROWS).reshape(NCHIP, E_ROWS // NCHIP, DM)
    return dV, dE


def unpack_grads(per_layer, RV, gE):
    grads = {}
    for i, k in enumerate(GAIN_NAMES):
        grads[k] = lambda i=i: gE[8 * i:8 * i + 4]
    grads["a_v_norm_g"] = lambda: gE[64:67].reshape(1, 3072)
    grads["a_v_norm_b"] = lambda: gE[72:75].reshape(1, 3072)
    grads["a_b_s"] = lambda: gE[80:88].reshape(8 * DM)[:1536].reshape(1, 12, 128)
    grads["a_w_s"] = lambda: gE[88:280].reshape(1, 12, 128, 128)
    col1 = lambda l, off, n: per_layer[l][0][:, off:off + n]
    col2 = lambda l, off, n: per_layer[l][2][:, off:off + n]
    grads["ff1_w_gate"] = lambda: jnp.stack([col1(l, A_FF(l, 0), FW) for l in range(4)])
    grads["ff1_w_up"] = lambda: jnp.stack([col1(l, A_FF(l, 1), FW) for l in range(4)])
    grads["ff2_w_gate"] = lambda: jnp.stack([col2(l, A_FF(l, 2), FW) for l in range(4)])
    grads["ff2_w_up"] = lambda: jnp.stack([col2(l, A_FF(l, 3), FW) for l in range(4)])
    grads["a_w_in"] = lambda: col2(0, A_AIN, 1536)[None]
    grads["c_w_pw1"] = lambda: col2(2, A_CIN, 512)[None]
    grads["d_w_in"] = lambda: col2(3, A_DIN, 768)[None]
    row2 = lambda l, off, n: per_layer[l][3][off:off + n]
    grads["ff1_w_down"] = lambda: jnp.stack([per_layer[l][1][:FW] for l in range(4)])
    grads["ff2_w_down"] = lambda: jnp.stack([row2(l, B_FF2D(l), FW) for l in range(4)])
    grads["ple_w_gate"] = lambda: jnp.stack([row2(l, B_PLEG(l), 256) for l in range(4)])
    grads["a_w_out"] = lambda: row2(0, B_AOUT, 768)[None]
    grads["c_w_pw2"] = lambda: row2(2, B_CPW2, 256)[None]
    grads["d_w_out"] = lambda: row2(3, B_DOUT, 256)[None]
    grads["ple_w_proj"] = lambda: jnp.stack([per_layer[l][4][C_PROJ(l):C_PROJ(l) + 256] for l in range(4)])
    grads["b_w_grp"] = lambda: per_layer[1][4][C_BGRP:C_BGRP + 256].reshape(1, 4, 64, 256)
    grads["b_scale"] = lambda: RV[V_BSCALE:V_BSCALE + 1]
    grads["c_b_dw"] = lambda: RV[V_CBDW:V_CBDW + 1]
    grads["c_norm_g"] = lambda: RV[V_CNG:V_CNG + 1]
    grads["c_norm_b"] = lambda: RV[V_CNB:V_CNB + 1]
    grads["d_w_conv"] = lambda: RV[V_DCONV:V_DCONV + 3][None]
    grads["c_w_dw"] = lambda: RV[V_CDW:V_CDW + 31][None]
    return {k: f() for k, f in grads.items()}
```

```python
import functools
import math

import jax
import jax.numpy as jnp
from jax import lax
from jax.experimental import pallas as pl
from jax.experimental.pallas import tpu as pltpu

F32, BF16 = jnp.float32, jnp.bfloat16
EPS = 1e-6
DM = 1024
FW = 704
FB = 768
NCHIP = 4
VMEM_LIMIT = 56 * 1024 * 1024
ANY = pl.BlockSpec(memory_space=pl.ANY)
MESH = pl.DeviceIdType.MESH

A_FF = lambda l, j: (j % 2) * FB
A_AIN = A_CIN = A_DIN = 2 * FB
A2_COLS = lambda l: 2 * FB + (1536, 0, 512, 768)[l]
B_FF1D = lambda l: 0
B_FF2D = lambda l: 0
B_AOUT = B_CPW2 = B_DOUT = FB
B_PLEG = lambda l: FB + (768, 0, 256, 256)[l]
B2_ROWS = lambda l: B_PLEG(l) + 256
C_PROJ = lambda l: 0
C_BGRP = 256
V_BSCALE, V_CBDW, V_CNG, V_CNB, V_DCONV, V_CDW, V_ROWS = 0, 8, 16, 24, 32, 40, 80
E_ROWS = 320

ADAM_LR, ADAM_B1, ADAM_B2, ADAM_EPS, ADAM_WD, ADAM_STEP = 0.001, 0.9, 0.999, 1e-08, 0.01, 10


def _cp(sem):
    return pltpu.CompilerParams(dimension_semantics=sem, vmem_limit_bytes=VMEM_LIMIT)


def _sig(x):
    return 0.5 * jnp.tanh(0.5 * x) + 0.5


_GC = math.sqrt(2.0 / math.pi)


def _gelu(x):
    return 0.5 * x * (1.0 + jnp.tanh(_GC * (x + 0.044715 * x * x * x)))


def _gelu_grad(x):
    t = jnp.tanh(_GC * (x + 0.044715 * x * x * x))
    return 0.5 * (1.0 + t) + 0.5 * x * (1.0 - t * t) * _GC * (1.0 + 3.0 * 0.044715 * x * x)


def _dot_nn(a, b):
    return lax.dot_general(a, b, (((1,), (0,)), ((), ())), preferred_element_type=F32)


def _dot_nt(a, b):
    return lax.dot_general(a, b, (((1,), (1,)), ((), ())), preferred_element_type=F32)


def _dot_tn(a, b):
    return lax.dot_general(a, b, (((0,), (0,)), ((), ())), preferred_element_type=F32)


def mm_cs(x, G, off, nb, tn, out_dtype, name, roff=0):
    T, K = x.shape
    tm = min(1024, T)
    nj, ob, rb_ = nb // tn, off // tn, roff // K
    assert nb % tn == 0 and off % tn == 0 and roff % K == 0

    def body(x_ref, w_ref, o_ref):
        o_ref[...] = _dot_nn(x_ref[...], w_ref[...]).astype(o_ref.dtype)

    return pl.pallas_call(
        body, name=name, grid=(T // tm, NCHIP, nj),
        in_specs=[pl.BlockSpec((tm, K), lambda i, s, j: (i, 0)),
                  pl.BlockSpec((None, K, tn), lambda i, s, j: (s, rb_, ob + j))],
        out_specs=pl.BlockSpec((tm, tn), lambda i, s, j: (i, s * nj + j)),
        out_shape=jax.ShapeDtypeStruct((T, NCHIP * nb), out_dtype),
        compiler_params=_cp(("parallel", "arbitrary", "arbitrary")))(x, G)


def mm_rs_t(dy, G, off, rb, tk, name):
    T, N = dy.shape
    tm = min(1024, T)
    nkk, ob = rb // tk, off // tk
    nk = NCHIP * nkk

    def body(dy_ref, w_ref, o_ref):
        o_ref[...] = _dot_nt(dy_ref[...], w_ref[...]).astype(o_ref.dtype)

    return pl.pallas_call(
        body, name=name, grid=(T // tm, nk),
        in_specs=[pl.BlockSpec((tm, N), lambda i, k: (i, 0)),
                  pl.BlockSpec((None, tk, N), lambda i, k: (k // nkk, ob + k % nkk, 0))],
        out_specs=pl.BlockSpec((tm, tk), lambda i, k: (i, k)),
        out_shape=jax.ShapeDtypeStruct((T, NCHIP * rb), BF16),
        compiler_params=_cp(("parallel", "arbitrary")))(dy, G)


def mm_tn(a, b, tmm, tn, out_shape, out_map, name, into=None):
    T, M = a.shape
    N = b.shape[1]
    tt = min(2048, T)
    nt = T // tt

    def body(a_ref, b_ref, o_ref, acc):
        t = pl.program_id(2)

        @pl.when(t == 0)
        def _():
            acc[...] = jnp.zeros_like(acc)

        acc[...] += _dot_tn(a_ref[...], b_ref[...])

        @pl.when(t == nt - 1)
        def _():
            o_ref[...] = acc[...].astype(o_ref.dtype)

    in_specs = [pl.BlockSpec((tt, tmm), lambda i, j, t: (t, i)), pl.BlockSpec((tt, tn), lambda i, j, t: (t, j))]
    operands = (a, b)
    if into is None:
        def kern(a_ref, b_ref, o_ref, acc):
            body(a_ref, b_ref, o_ref, acc)
        aliases = {}
    else:
        def kern(a_ref, b_ref, into_ref, o_ref, acc):
            body(a_ref, b_ref, o_ref, acc)
        in_specs.append(ANY)
        operands = (a, b, into)
        aliases = {2: 0}
        out_shape = into.shape
    return pl.pallas_call(
        kern, name=name, grid=(M // tmm, N // tn, nt), in_specs=in_specs,
        out_specs=pl.BlockSpec((None, tmm, tn), lambda i, j, t: out_map(i, j)),
        out_shape=jax.ShapeDtypeStruct(out_shape, BF16), input_output_aliases=aliases,
        scratch_shapes=[pltpu.VMEM((tmm, tn), F32)],
        compiler_params=_cp(("parallel", "parallel", "arbitrary")))(*operands)


def dw_cs(x, dy, nb, tn, name, width=None, off=0, into=None):
    K = x.shape[1]
    nj, ob = nb // tn, off // tn
    assert off % tn == 0
    return mm_tn(x, dy, K, tn, (NCHIP, K, width or nb), lambda i, j: (j // nj, 0, ob + j % nj), name, into)


def dw_rs(a, dy, rb, tr, name, height=None, off=0, into=None):
    N = dy.shape[1]
    ni, ob = rb // tr, off // tr
    assert off % tr == 0
    return mm_tn(a, dy, tr, N, (NCHIP, height or rb, N), lambda i, j: (i // ni, ob + i % ni, 0), name, into)


def ff_gateup(xn, GA, off, name):
    T, K = xn.shape
    tm = min(1024, T)
    ob = off // (2 * FB)
    assert off % (2 * FB) == 0

    sub = min(512, tm)

    def body(x_ref, w_ref, gu_ref, a_ref):
        for r0 in range(0, tm, sub):
            r = _dot_nn(x_ref[r0:r0 + sub, :], w_ref[...])
            g, u = r[:, :FB], r[:, FB:]
            gu_ref[r0:r0 + sub, :] = r.astype(gu_ref.dtype)
            a_ref[r0:r0 + sub, :] = (g * _sig(g) * u).astype(a_ref.dtype)

    return pl.pallas_call(
        body, name=name, grid=(T // tm, NCHIP),
        in_specs=[pl.BlockSpec((tm, K), lambda i, s: (i, 0)),
                  pl.BlockSpec((None, K, 2 * FB), lambda i, s: (s, 0, ob))],
        out_specs=[pl.BlockSpec((tm, 2 * FB), lambda i, s: (i, s)), pl.BlockSpec((tm, FB), lambda i, s: (i, s))],
        out_shape=[jax.ShapeDtypeStruct((T, NCHIP * 2 * FB), BF16), jax.ShapeDtypeStruct((T, NCHIP * FB), BF16)],
        compiler_params=_cp(("parallel", "arbitrary")))(xn, GA)


def mm_rs_post(a, G, off, rb, tk, h, g, scale, name, g_next=None):
    T = a.shape[0]
    N = G.shape[2]
    tm = min(512, T)
    sub = min(256, tm)
    assert off % rb == 0 and rb % tk == 0
    more = g_next is not None

    def body(a_ref, w_ref, h_ref, g_ref, *rest):
        if more:
            gn_ref, f_ref, o_ref, xn_ref = rest
        else:
            f_ref, o_ref = rest
        for r0 in range(0, tm, sub):
            f = _dot_nn(a_ref[r0:r0 + sub, :rb], w_ref[0])
            for s in range(1, NCHIP):
                f = f + _dot_nn(a_ref[r0:r0 + sub, s * rb:(s + 1) * rb], w_ref[s])
            f_ref[r0:r0 + sub, :] = f
            r = lax.rsqrt(jnp.mean(f * f, axis=-1, keepdims=True) + EPS)
            hn = h_ref[r0:r0 + sub, :] + scale * (f * r * g_ref[...])
            o_ref[r0:r0 + sub, :] = hn
            if more:
                rn = lax.rsqrt(jnp.mean(hn * hn, axis=-1, keepdims=True) + EPS)
                xn_ref[r0:r0 + sub, :] = (hn * rn * gn_ref[...]).astype(xn_ref.dtype)

    row = pl.BlockSpec((tm, N), lambda i: (i, 0))
    vec = pl.BlockSpec((1, N), lambda i: (0, 0))
    return pl.pallas_call(
        body, name=name, grid=(T // tm,),
        in_specs=[pl.BlockSpec((tm, NCHIP * rb), lambda i: (i, 0)),
                  pl.BlockSpec((NCHIP, rb, N), lambda i: (0, off // rb, 0), pipeline_mode=pl.Buffered(1)),
                  row, vec] + ([vec] if more else []),
        out_specs=[row, row] + ([row] if more else []),
        out_shape=[jax.ShapeDtypeStruct((T, N), F32), jax.ShapeDtypeStruct((T, N), F32)]
        + ([jax.ShapeDtypeStruct((T, N), BF16)] if more else []),
        compiler_params=_cp(("parallel",)))(*((a, G, h, g) + ((g_next,) if more else ())))


def ff_bwd_down(dh, f, g, GB, down, gu, name):
    T, N = dh.shape
    tm = min(512, T)
    sub = min(256, tm)
    ob = down // FB

    def body(d_ref, f_ref, g_ref, w_ref, gu_ref, df_ref, dg_ref, dgu_ref):
        i = pl.program_id(0)
        dg = jnp.zeros((1, N), F32)
        for r0 in range(0, tm, sub):
            f = f_ref[r0:r0 + sub, :]
            r = lax.rsqrt(jnp.mean(f * f, axis=-1, keepdims=True) + EPS)
            d = 0.5 * d_ref[r0:r0 + sub, :]
            t = d * g_ref[...]
            df = (r * t - f * (r * r * r * jnp.mean(t * f, axis=-1, keepdims=True))).astype(df_ref.dtype)
            df_ref[r0:r0 + sub, :] = df
            dg = dg + jnp.sum(d * f * r, axis=0, keepdims=True)
            for s in range(NCHIP):
                c0 = s * 2 * FB
                da = _dot_nt(df, w_ref[s])
                gt = gu_ref[r0:r0 + sub, c0:c0 + FB].astype(F32)
                u = gu_ref[r0:r0 + sub, c0 + FB:c0 + 2 * FB].astype(F32)
                sg = _sig(gt)
                silu = gt * sg
                dgu_ref[r0:r0 + sub, c0:c0 + FB] = (da * u * (sg + silu - silu * sg)).astype(dgu_ref.dtype)
                dgu_ref[r0:r0 + sub, c0 + FB:c0 + 2 * FB] = (da * silu).astype(dgu_ref.dtype)
        _acc_rows(dg_ref, i, dg)

    row = pl.BlockSpec((tm, N), lambda i: (i, 0))
    vec = pl.BlockSpec((1, N), lambda i: (0, 0))
    wide = pl.BlockSpec((tm, NCHIP * 2 * FB), lambda i: (i, 0))
    return pl.pallas_call(
        body, name=name, grid=(T // tm,),
        in_specs=[row, row, vec,
                  pl.BlockSpec((NCHIP, FB, N), lambda i: (0, ob, 0), pipeline_mode=pl.Buffered(1)), wide],
        out_specs=[row, vec, wide],
        out_shape=[jax.ShapeDtypeStruct((T, N), BF16), jax.ShapeDtypeStruct((1, N), F32),
                   jax.ShapeDtypeStruct((T, NCHIP * 2 * FB), BF16)],
        compiler_params=_cp(("arbitrary",)))(dh, f, g, GB, gu)


def mm_cs_t_rms(dy, G, off, nb, tn, h, g, skip, name):
    T = dy.shape[0]
    K = G.shape[1]
    tm = min(512, T)
    sub = min(256, tm)
    assert off % nb == 0 and nb % tn == 0

    def body(dy_ref, w_ref, h_ref, g_ref, s_ref, o_ref, dg_ref):
        i = pl.program_id(0)
        dg = jnp.zeros((1, K), F32)
        for r0 in range(0, tm, sub):
            d = _dot_nt(dy_ref[r0:r0 + sub, :nb], w_ref[0])
            for s in range(1, NCHIP):
                d = d + _dot_nt(dy_ref[r0:r0 + sub, s * nb:(s + 1) * nb], w_ref[s])
            x = h_ref[r0:r0 + sub, :]
            r = lax.rsqrt(jnp.mean(x * x, axis=-1, keepdims=True) + EPS)
            xh = x * r
            t = d * g_ref[...]
            o_ref[r0:r0 + sub, :] = s_ref[r0:r0 + sub, :] + r * (t - xh * jnp.mean(t * xh, axis=-1, keepdims=True))
            dg = dg + jnp.sum(d * xh, axis=0, keepdims=True)
        _acc_rows(dg_ref, i, dg)

    row = pl.BlockSpec((tm, K), lambda i: (i, 0))
    vec = pl.BlockSpec((1, K), lambda i: (0, 0))
    return pl.pallas_call(
        body, name=name, grid=(T // tm,),
        in_specs=[pl.BlockSpec((tm, NCHIP * nb), lambda i: (i, 0)),
                  pl.BlockSpec((NCHIP, K, nb), lambda i: (0, 0, off // nb), pipeline_mode=pl.Buffered(1)),
                  row, vec, row],
        out_specs=[row, vec],
        out_shape=[jax.ShapeDtypeStruct((T, K), F32), jax.ShapeDtypeStruct((1, K), F32)],
        compiler_params=_cp(("arbitrary",)))(dy, G, h, g, skip)


def _rows(tm, C):
    return pl.BlockSpec((tm, C), lambda i: (i, 0))


def _vec(C):
    return pl.BlockSpec((1, C), lambda i: (0, 0))


def _acc_rows(ref, i, val):
    @pl.when(i == 0)
    def _():
        ref[...] = val

    @pl.when(i > 0)
    def _():
        ref[...] += val


def rms_fwd(h, g, out_dtype, name):
    T, C = h.shape
    tm = min(512, T)

    def body(h_ref, g_ref, o_ref):
        x = h_ref[...]
        r = lax.rsqrt(jnp.mean(x * x, axis=-1, keepdims=True) + EPS)
        o_ref[...] = (x * r * g_ref[...]).astype(o_ref.dtype)

    return pl.pallas_call(
        body, name=name, grid=(T // tm,), in_specs=[_rows(tm, C), _vec(C)], out_specs=_rows(tm, C),
        out_shape=jax.ShapeDtypeStruct((T, C), out_dtype), compiler_params=_cp(("parallel",)))(h, g)


def rms_bwd(dxn, h, g, dh_skip, name):
    T, C = h.shape
    tm = min(512, T)

    def body(d_ref, h_ref, g_ref, s_ref, o_ref, dg_ref):
        i = pl.program_id(0)
        x = h_ref[...]
        r = lax.rsqrt(jnp.mean(x * x, axis=-1, keepdims=True) + EPS)
        xh = x * r
        d = d_ref[...].astype(F32)
        t = d * g_ref[...]
        o_ref[...] = s_ref[...] + r * (t - xh * jnp.mean(t * xh, axis=-1, keepdims=True))
        _acc_rows(dg_ref, i, jnp.sum(d * xh, axis=0, keepdims=True))

    return pl.pallas_call(
        body, name=name, grid=(T // tm,),
        in_specs=[_rows(tm, C), _rows(tm, C), _vec(C), _rows(tm, C)],
        out_specs=[_rows(tm, C), _vec(C)],
        out_shape=[jax.ShapeDtypeStruct((T, C), F32), jax.ShapeDtypeStruct((1, C), F32)],
        compiler_params=_cp(("arbitrary",)))(dxn, h, g, dh_skip)


def post_res(h, f, g, scale, name):
    T, C = h.shape
    tm = min(512, T)

    def body(h_ref, f_ref, g_ref, o_ref):
        f = f_ref[...]
        r = lax.rsqrt(jnp.mean(f * f, axis=-1, keepdims=True) + EPS)
        o_ref[...] = h_ref[...] + scale * (f * r * g_ref[...])

    return pl.pallas_call(
        body, name=name, grid=(T // tm,), in_specs=[_rows(tm, C), _rows(tm, C), _vec(C)],
        out_specs=_rows(tm, C), out_shape=jax.ShapeDtypeStruct((T, C), F32),
        compiler_params=_cp(("parallel",)))(h, f, g)


def post_res_bwd(dh, f, g, scale, out_dtype, name):
    T, C = dh.shape
    tm = min(512, T)

    def body(d_ref, f_ref, g_ref, o_ref, dg_ref):
        i = pl.program_id(0)
        f = f_ref[...]
        r = lax.rsqrt(jnp.mean(f * f, axis=-1, keepdims=True) + EPS)
        d = scale * d_ref[...]
        t = d * g_ref[...]
        o_ref[...] = (r * t - f * (r * r * r * jnp.mean(t * f, axis=-1, keepdims=True))).astype(o_ref.dtype)
        _acc_rows(dg_ref, i, jnp.sum(d * f * r, axis=0, keepdims=True))

    return pl.pallas_call(
        body, name=name, grid=(T // tm,), in_specs=[_rows(tm, C), _rows(tm, C), _vec(C)],
        out_specs=[_rows(tm, C), _vec(C)],
        out_shape=[jax.ShapeDtypeStruct((T, C), out_dtype), jax.ShapeDtypeStruct((1, C), F32)],
        compiler_params=_cp(("arbitrary",)))(dh, f, g)


PLE_W = 256


def ple_fwd(h, xn, pb, GB, goff, GC, g, name, g_next=None):
    T, C = h.shape
    tm = min(512, T)
    more = g_next is not None

    def body(h_ref, x_ref, p_ref, wg_ref, wp_ref, g_ref, *rest):
        if more:
            gn_ref, z_ref, pe_ref, o_ref, xn_ref = rest
        else:
            z_ref, pe_ref, o_ref = rest
        zg = _dot_nn(x_ref[:, :PLE_W], wg_ref[0])
        for s in range(1, NCHIP):
            zg = zg + _dot_nn(x_ref[:, s * PLE_W:(s + 1) * PLE_W], wg_ref[s])
        z_ref[...] = zg
        for s in range(NCHIP):
            pe_ref[:, s * PLE_W:(s + 1) * PLE_W] = _dot_nn(p_ref[...], wp_ref[s])
        e = pe_ref[...] * _sig(zg)
        r = lax.rsqrt(jnp.mean(e * e, axis=-1, keepdims=True) + EPS)
        hn = h_ref[...] + e * r * g_ref[...]
        o_ref[...] = hn
        if more:
            rn = lax.rsqrt(jnp.mean(hn * hn, axis=-1, keepdims=True) + EPS)
            xn_ref[...] = (hn * rn * gn_ref[...]).astype(xn_ref.dtype)

    return pl.pallas_call(
        body, name=name, grid=(T // tm,),
        in_specs=[_rows(tm, C), _rows(tm, C), _rows(tm, PLE_W),
                  pl.BlockSpec((NCHIP, PLE_W, C), lambda i: (0, goff // PLE_W, 0)),
                  pl.BlockSpec((NCHIP, PLE_W, PLE_W), lambda i: (0, 0, 0)), _vec(C)] + ([_vec(C)] if more else []),
        out_specs=[_rows(tm, C)] * (4 if more else 3),
        out_shape=[jax.ShapeDtypeStruct((T, C), F32)] * 3 + ([jax.ShapeDtypeStruct((T, C), BF16)] if more else []),
        compiler_params=_cp(("parallel",)))(*((h, xn, pb, GB, GC, g) + ((g_next,) if more else ())))


def ple_bwd(dh, zg, pe, h_in, GB, goff, g, g_pre, name):
    T, C = dh.shape
    tm = min(512, T)

    def body(d_ref, z_ref, p_ref, h_ref, wg_ref, g_ref, gp_ref, dz_ref, dp_ref, o_ref, dg_ref, dgp_ref, dxn):
        i = pl.program_id(0)
        s = _sig(z_ref[...])
        pe_ = p_ref[...]
        e = pe_ * s
        r = lax.rsqrt(jnp.mean(e * e, axis=-1, keepdims=True) + EPS)
        d = d_ref[...]
        t = d * g_ref[...]
        de = r * t - e * (r * r * r * jnp.mean(t * e, axis=-1, keepdims=True))
        dp_ref[...] = (de * s).astype(dp_ref.dtype)
        dz = (de * pe_ * s * (1.0 - s)).astype(dz_ref.dtype)
        dz_ref[...] = dz
        _acc_rows(dg_ref, i, jnp.sum(d * e * r, axis=0, keepdims=True))
        for k in range(NCHIP):
            dxn[:, k * PLE_W:(k + 1) * PLE_W] = _dot_nt(dz, wg_ref[k])
        x = h_ref[...]
        rx = lax.rsqrt(jnp.mean(x * x, axis=-1, keepdims=True) + EPS)
        xh = x * rx
        dx = dxn[...]
        tx = dx * gp_ref[...]
        o_ref[...] = d + rx * (tx - xh * jnp.mean(tx * xh, axis=-1, keepdims=True))
        _acc_rows(dgp_ref, i, jnp.sum(dx * xh, axis=0, keepdims=True))

    return pl.pallas_call(
        body, name=name, grid=(T // tm,),
        in_specs=[_rows(tm, C), _rows(tm, C), _rows(tm, C), _rows(tm, C),
                  pl.BlockSpec((NCHIP, PLE_W, C), lambda i: (0, goff // PLE_W, 0)), _vec(C), _vec(C)],
        out_specs=[_rows(tm, C), _rows(tm, C), _rows(tm, C), _vec(C), _vec(C)],
        out_shape=[jax.ShapeDtypeStruct((T, C), BF16), jax.ShapeDtypeStruct((T, C), BF16),
                   jax.ShapeDtypeStruct((T, C), F32), jax.ShapeDtypeStruct((1, C), F32),
                   jax.ShapeDtypeStruct((1, C), F32)],
        scratch_shapes=[pltpu.VMEM((tm, C), F32)],
        compiler_params=_cp(("arbitrary",)))(dh, zg, pe, h_in, GB, g, g_pre)


def loss_head(h, tgt, name):
    T, C = h.shape
    tm = min(512, T)

    def body(h_ref, t_ref, d_ref, l_ref):
        i = pl.program_id(0)
        e = h_ref[...] - t_ref[...]
        d_ref[...] = e * (1.0 / C)
        _acc_rows(l_ref, i, jnp.sum(e * e, axis=0, keepdims=True))

    return pl.pallas_call(
        body, name=name, grid=(T // tm,), in_specs=[_rows(tm, C), _rows(tm, C)],
        out_specs=[_rows(tm, C), _vec(C)],
        out_shape=[jax.ShapeDtypeStruct((T, C), F32), jax.ShapeDtypeStruct((1, C), F32)],
        compiler_params=_cp(("arbitrary",)))(h, tgt)


AH, AG_N, AGW, CHUNK = 3072, 12, 256, 128


def _tril_bf16(w):
    r = lax.broadcasted_iota(jnp.int32, (CHUNK, CHUNK), 0)
    c = lax.broadcasted_iota(jnp.int32, (CHUNK, CHUNK), 1)
    return jnp.where(r >= c, w, 0.0).astype(BF16)


def _ln_stats(vs_ref, width):
    v = vs_ref[...]
    mu = jnp.sum(v, axis=-1, keepdims=True) * (1.0 / width)
    vc = v - mu
    var = jnp.sum(vc * vc, axis=-1, keepdims=True) * (1.0 / width)
    return mu, lax.rsqrt(var + EPS)


def gmlp_mid_fwd(zpre, vg, vb, ws, bsf, name):
    T = zpre.shape[0]

    def body(z_ref, vg_ref, vb_ref, ws_ref, bs_ref, y_ref, vs_ref):
        for g in range(AG_N):
            vs_ref[:, g * AGW:(g + 1) * AGW] = _gelu(z_ref[:, AH + g * AGW:AH + (g + 1) * AGW].astype(F32))
        mu, rstd = _ln_stats(vs_ref, AH)
        for g in range(AG_N):
            sl = slice(g * AGW, (g + 1) * AGW)
            vn = ((vs_ref[:, sl] - mu) * rstd * vg_ref[:, sl] + vb_ref[:, sl]).astype(BF16)
            sv = _dot_nn(_tril_bf16(ws_ref[g]), vn) + bs_ref[g]
            u = _gelu(z_ref[:, sl].astype(F32))
            y_ref[:, sl] = (u * sv).astype(y_ref.dtype)

    return pl.pallas_call(
        body, name=name, grid=(T // CHUNK,),
        in_specs=[_rows(CHUNK, 2 * AH), _vec(AH), _vec(AH),
                  pl.BlockSpec((AG_N, CHUNK, CHUNK), lambda i: (0, 0, 0)),
                  pl.BlockSpec((AG_N, CHUNK, AGW), lambda i: (0, 0, 0))],
        out_specs=_rows(CHUNK, AH), out_shape=jax.ShapeDtypeStruct((T, AH), BF16),
        scratch_shapes=[pltpu.VMEM((CHUNK, AH), F32)],
        compiler_params=_cp(("parallel",)))(zpre, vg, vb, ws, bsf)


def gmlp_mid_bwd(zpre, dy, vg, vb, ws, bsf, name):
    T = zpre.shape[0]

    def body(z_ref, dy_ref, vg_ref, vb_ref, ws_ref, bs_ref, dz_ref, dws_ref, dbs_ref, dvg_ref, dvb_ref,
             vs_ref, dvn_ref):
        i = pl.program_id(0)

        @pl.when(i == 0)
        def _():
            dws_ref[...] = jnp.zeros_like(dws_ref)
            dbs_ref[...] = jnp.zeros_like(dbs_ref)
            dvg_ref[...] = jnp.zeros_like(dvg_ref)
            dvb_ref[...] = jnp.zeros_like(dvb_ref)

        for g in range(AG_N):
            vs_ref[:, g * AGW:(g + 1) * AGW] = _gelu(z_ref[:, AH + g * AGW:AH + (g + 1) * AGW].astype(F32))
        mu, rstd = _ln_stats(vs_ref, AH)
        r_i = lax.broadcasted_iota(jnp.int32, (CHUNK, CHUNK), 0)
        c_i = lax.broadcasted_iota(jnp.int32, (CHUNK, CHUNK), 1)
        ones8 = jnp.ones((8, AGW), F32)
        m1 = jnp.zeros((CHUNK, 1), F32)
        m2 = jnp.zeros((CHUNK, 1), F32)
        for g in range(AG_N):
            sl = slice(g * AGW, (g + 1) * AGW)
            vh = (vs_ref[:, sl] - mu) * rstd
            vn = (vh * vg_ref[:, sl] + vb_ref[:, sl]).astype(BF16)
            wm = _tril_bf16(ws_ref[g])
            sv = _dot_nn(wm, vn) + bs_ref[g]
            zu = z_ref[:, sl].astype(F32)
            u = _gelu(zu)
            dyg = dy_ref[:, sl].astype(F32)
            dz_ref[:, sl] = (dyg * sv * _gelu_grad(zu)).astype(dz_ref.dtype)
            dsv = dyg * u
            dsv_b = dsv.astype(BF16)
            dws_ref[g] += jnp.where(r_i >= c_i, _dot_nt(dsv_b, vn), 0.0)
            dbs_ref[g] += _dot_nt(ones8, dsv)
            dvn = _dot_tn(wm, dsv_b)
            dvn_ref[:, sl] = dvn
            dvh = dvn * vg_ref[:, sl]
            m1 = m1 + jnp.sum(dvh, axis=-1, keepdims=True)
            m2 = m2 + jnp.sum(dvh * vh, axis=-1, keepdims=True)
            dvg_ref[:, sl] += jnp.sum(dvn * vh, axis=0, keepdims=True)
            dvb_ref[:, sl] += jnp.sum(dvn, axis=0, keepdims=True)
        m1 = m1 * (1.0 / AH)
        m2 = m2 * (1.0 / AH)
        for g in range(AG_N):
            sl = slice(g * AGW, (g + 1) * AGW)
            vh = (vs_ref[:, sl] - mu) * rstd
            dv = rstd * (dvn_ref[:, sl] * vg_ref[:, sl] - m1 - vh * m2)
            zv = z_ref[:, AH + g * AGW:AH + (g + 1) * AGW].astype(F32)
            dz_ref[:, AH + g * AGW:AH + (g + 1) * AGW] = (dv * _gelu_grad(zv)).astype(dz_ref.dtype)

    full3 = lambda a, b, c: pl.BlockSpec((a, b, c), lambda i: (0, 0, 0))
    return pl.pallas_call(
        body, name=name, grid=(T // CHUNK,),
        in_specs=[_rows(CHUNK, 2 * AH), _rows(CHUNK, AH), _vec(AH), _vec(AH),
                  full3(AG_N, CHUNK, CHUNK), full3(AG_N, CHUNK, AGW)],
        out_specs=[_rows(CHUNK, 2 * AH), full3(AG_N, CHUNK, CHUNK), full3(AG_N, 8, CHUNK), _vec(AH), _vec(AH)],
        out_shape=[jax.ShapeDtypeStruct((T, 2 * AH), BF16), jax.ShapeDtypeStruct((AG_N, CHUNK, CHUNK), F32),
                   jax.ShapeDtypeStruct((AG_N, 8, CHUNK), F32), jax.ShapeDtypeStruct((1, AH), F32),
                   jax.ShapeDtypeStruct((1, AH), F32)],
        scratch_shapes=[pltpu.VMEM((CHUNK, AH), F32), pltpu.VMEM((CHUNK, AH), F32)],
        compiler_params=_cp(("arbitrary",)))(zpre, dy, vg, vb, ws, bsf)


SLAB = 256
NSLAB = DM // SLAB
RC = 256
PAD = 32


def _col(T, j):
    return pl.BlockSpec((T, SLAB), lambda c: (0, j * NSLAB + c))


def _chunks(T, fn):
    def step(i, carry):
        fn(pl.multiple_of(i * RC, RC))
        return carry
    lax.fori_loop(0, T // RC, step, 0)


def _conv_taps(K):
    return [(r, [q for q in range(4) if 8 * q + r < K]) for r in range(min(8, K))]


def _causal_conv(zpad_ref, wrow, K, r0):
    acc = None
    for r, qs in _conv_taps(K):
        a = None
        for q in qs:
            term = wrow(8 * q + r) * zpad_ref[pl.ds(r0 + (PAD - 8 - 8 * q), RC + 8), :]
            a = term if a is None else a + term
        a = a if r == 0 else pltpu.roll(a, r, 0)
        acc = a if acc is None else acc + a
    return acc[8:, :]


def _anticausal_conv(gpad_ref, wrow, K, r0):
    acc = None
    for r, qs in _conv_taps(K):
        b = None
        for q in qs:
            term = wrow(8 * q + r) * gpad_ref[pl.ds(r0 + 8 * q, RC + 8), :]
            b = term if b is None else b + term
        b = b if r == 0 else pltpu.roll(b, RC + 8 - r, 0)
        acc = b if acc is None else acc + b
    return acc[:RC, :]


def _conv_dw(gpad_ref, zpad_ref, dw_ref, K, r0):
    for r, qs in _conv_taps(K):
        gw = gpad_ref[pl.ds(r0, RC + 8), :]
        p = (gw if r == 0 else pltpu.roll(gw, RC + 8 - r, 0))[:RC, :]
        for q in qs:
            z = zpad_ref[pl.ds(r0 + (PAD - 8 * q), RC), :]
            dw_ref[8 * q + r] += jnp.sum((p * z).reshape(RC // 8, 8, SLAB), axis=0)


def _zero_rows(ref, start, n):
    ref[pl.ds(start, n), :] = jnp.zeros((n, SLAB), F32)


def pool_fwd(hn, wg, sc, name):
    T = hn.shape[0]

    def body(h_ref, w_ref, s_ref, p_ref, yp_ref, y_ref, xpad):
        g = pl.program_id(0)
        wf = jnp.left_shift(2, g).astype(F32)
        _zero_rows(xpad, 0, PAD)

        def fill(r0):
            xpad[pl.ds(r0 + PAD, RC), :] = h_ref[pl.ds(r0, RC), :]
        _chunks(T, fill)

        def step(r0):
            w = xpad[pl.ds(r0 + (PAD - 16), RC + 16), :]
            s2 = w + pltpu.roll(w, 1, 0)
            s4 = s2 + pltpu.roll(s2, 2, 0)
            s8 = s4 + pltpu.roll(s4, 4, 0)
            s16 = s8 + pltpu.roll(s8, 8, 0)
            sel = jnp.where(g == 0, s2, jnp.where(g == 1, s4, jnp.where(g == 2, s8, s16)))[16:, :]
            t1 = (r0 + 1 + lax.broadcasted_iota(jnp.int32, (RC, SLAB), 0)).astype(F32)
            pooled = (sel / jnp.minimum(t1, wf) - w[16:, :]).astype(BF16)
            p_ref[pl.ds(r0, RC), :] = pooled
            yp = _dot_nn(pooled, w_ref[...])
            yp_ref[pl.ds(r0, RC), :] = yp
            y_ref[pl.ds(r0, RC), :] = yp * s_ref[...]
        _chunks(T, step)

    slab = pl.BlockSpec((T, SLAB), lambda c: (0, c))
    return pl.pallas_call(
        body, name=name, grid=(NSLAB,),
        in_specs=[slab, pl.BlockSpec((None, SLAB, SLAB), lambda c: (c, 0, 0)), pl.BlockSpec((1, SLAB), lambda c: (0, c))],
        out_specs=[slab, slab, slab],
        out_shape=[jax.ShapeDtypeStruct((T, DM), BF16), jax.ShapeDtypeStruct((T, DM), F32),
                   jax.ShapeDtypeStruct((T, DM), F32)],
        scratch_shapes=[pltpu.VMEM((T + PAD, SLAB), F32)],
        compiler_params=_cp(("parallel",)))(hn, wg, sc)


def pool_bwd(dy, ypre, pooled, wg, sc, name):
    T = dy.shape[0]

    def body(d_ref, yp_ref, p_ref, w_ref, s_ref, dh_ref, dw_ref, ds_ref, qpad, dwacc, dsacc):
        g = pl.program_id(0)
        wf = jnp.left_shift(2, g).astype(F32)
        dwacc[...] = jnp.zeros_like(dwacc)
        dsacc[...] = jnp.zeros_like(dsacc)
        _zero_rows(qpad, T, PAD)

        def first(r0):
            d = d_ref[pl.ds(r0, RC), :]
            dsacc[...] += jnp.sum((d * yp_ref[pl.ds(r0, RC), :]).reshape(RC // 8, 8, SLAB), axis=0)
            dyp = (d * s_ref[...]).astype(BF16)
            dpool = _dot_nt(dyp, w_ref[...])
            dwacc[...] += _dot_tn(p_ref[pl.ds(r0, RC), :], dyp)
            t1 = (r0 + 1 + lax.broadcasted_iota(jnp.int32, (RC, SLAB), 0)).astype(F32)
            qpad[pl.ds(r0, RC), :] = dpool / jnp.minimum(t1, wf)
            dh_ref[pl.ds(r0, RC), :] = dpool
        _chunks(T, first)

        def second(r0):
            w = qpad[pl.ds(r0, RC + 16), :]
            n = RC + 16
            a2 = w + pltpu.roll(w, n - 1, 0)
            a4 = a2 + pltpu.roll(a2, n - 2, 0)
            a8 = a4 + pltpu.roll(a4, n - 4, 0)
            a16 = a8 + pltpu.roll(a8, n - 8, 0)
            sel = jnp.where(g == 0, a2, jnp.where(g == 1, a4, jnp.where(g == 2, a8, a16)))[:RC, :]
            dh_ref[pl.ds(r0, RC), :] = sel - dh_ref[pl.ds(r0, RC), :]
        _chunks(T, second)
        dw_ref[...] = dwacc[...]
        ds_ref[...] = jnp.sum(dsacc[...], axis=0, keepdims=True)

    slab = pl.BlockSpec((T, SLAB), lambda c: (0, c))
    wspec = pl.BlockSpec((None, SLAB, SLAB), lambda c: (c, 0, 0))
    vec = pl.BlockSpec((1, SLAB), lambda c: (0, c))
    return pl.pallas_call(
        body, name=name, grid=(NSLAB,),
        in_specs=[slab, slab, slab, wspec, vec],
        out_specs=[slab, wspec, vec],
        out_shape=[jax.ShapeDtypeStruct((T, DM), F32), jax.ShapeDtypeStruct((NSLAB, SLAB, SLAB), F32),
                   jax.ShapeDtypeStruct((1, DM), F32)],
        scratch_shapes=[pltpu.VMEM((T + PAD, SLAB), F32), pltpu.VMEM((SLAB, SLAB), F32), pltpu.VMEM((8, SLAB), F32)],
        compiler_params=_cp(("parallel",)))(dy, ypre, pooled, wg, sc)


KC = 31
KD = 3


def conf_conv_fwd(ag, wdw, bdw, name):
    T = ag.shape[0]

    def body(a_ref, g_ref, w_ref, b_ref, o_ref, zpad):
        _zero_rows(zpad, 0, PAD)

        def fill(r0):
            a = a_ref[pl.ds(r0, RC), :].astype(F32)
            gt = g_ref[pl.ds(r0, RC), :].astype(F32)
            zpad[pl.ds(r0 + PAD, RC), :] = a * _sig(gt)
        _chunks(T, fill)
        wrow = lambda j: w_ref[KC - 1 - j:KC - j, :]

        def step(r0):
            o_ref[pl.ds(r0, RC), :] = _causal_conv(zpad, wrow, KC, r0) + b_ref[...]
        _chunks(T, step)

    vec = pl.BlockSpec((1, SLAB), lambda c: (0, c))
    return pl.pallas_call(
        body, name=name, grid=(NSLAB,),
        in_specs=[_col(T, 0), _col(T, 1), pl.BlockSpec((32, SLAB), lambda c: (0, c)), vec],
        out_specs=pl.BlockSpec((T, SLAB), lambda c: (0, c)),
        out_shape=jax.ShapeDtypeStruct((T, DM), F32),
        scratch_shapes=[pltpu.VMEM((T + PAD, SLAB), F32)],
        compiler_params=_cp(("parallel",)))(ag, ag, wdw, bdw)


def conf_conv_bwd(dzc, ag, wdw, name):
    T = ag.shape[0]

    def body(d_ref, a_ref, g_ref, w_ref, da_ref, dg_ref, dw_ref, db_ref, zpad, gpad, dwacc, dbacc):
        _zero_rows(zpad, 0, PAD)
        _zero_rows(gpad, T, PAD)
        dwacc[...] = jnp.zeros_like(dwacc)
        dbacc[...] = jnp.zeros_like(dbacc)

        def fill(r0):
            a = a_ref[pl.ds(r0, RC), :].astype(F32)
            gt = g_ref[pl.ds(r0, RC), :].astype(F32)
            zpad[pl.ds(r0 + PAD, RC), :] = a * _sig(gt)
            d = d_ref[pl.ds(r0, RC), :]
            gpad[pl.ds(r0, RC), :] = d
            dbacc[...] += jnp.sum(d.reshape(RC // 8, 8, SLAB), axis=0)
        _chunks(T, fill)
        wrow = lambda j: w_ref[KC - 1 - j:KC - j, :]

        def step(r0):
            dz = _anticausal_conv(gpad, wrow, KC, r0)
            a = a_ref[pl.ds(r0, RC), :].astype(F32)
            s = _sig(g_ref[pl.ds(r0, RC), :].astype(F32))
            da_ref[pl.ds(r0, RC), :] = (dz * s).astype(da_ref.dtype)
            dg_ref[pl.ds(r0, RC), :] = (dz * a * s * (1.0 - s)).astype(dg_ref.dtype)
            _conv_dw(gpad, zpad, dwacc, KC, r0)
        _chunks(T, step)
        dw_ref[...] = jnp.zeros_like(dw_ref)
        for k in range(KC):
            dw_ref[k:k + 1, :] = jnp.sum(dwacc[KC - 1 - k], axis=0, keepdims=True)
        db_ref[...] = jnp.sum(dbacc[...], axis=0, keepdims=True)

    vec = pl.BlockSpec((1, SLAB), lambda c: (0, c))
    w32 = pl.BlockSpec((32, SLAB), lambda c: (0, c))
    return pl.pallas_call(
        body, name=name, grid=(NSLAB,),
        in_specs=[pl.BlockSpec((T, SLAB), lambda c: (0, c)), _col(T, 0), _col(T, 1), w32],
        out_specs=[_col(T, 0), _col(T, 0), w32, vec],
        out_shape=[jax.ShapeDtypeStruct((T, DM), BF16), jax.ShapeDtypeStruct((T, DM), BF16),
                   jax.ShapeDtypeStruct((32, DM), F32), jax.ShapeDtypeStruct((1, DM), F32)],
        scratch_shapes=[pltpu.VMEM((T + PAD, SLAB), F32), pltpu.VMEM((T + PAD, SLAB), F32),
                        pltpu.VMEM((32, 8, SLAB), F32), pltpu.VMEM((8, SLAB), F32)],
        compiler_params=_cp(("parallel",)))(dzc, ag, ag, wdw)


def conf_ln_fwd(zc, g, b, name):
    T, C = zc.shape
    tm = min(512, T)

    def body(z_ref, g_ref, b_ref, o_ref):
        x = z_ref[...]
        xc = x - jnp.mean(x, axis=-1, keepdims=True)
        r = lax.rsqrt(jnp.mean(xc * xc, axis=-1, keepdims=True) + EPS)
        zl = xc * r * g_ref[...] + b_ref[...]
        o_ref[...] = (zl * _sig(zl)).astype(o_ref.dtype)

    return pl.pallas_call(
        body, name=name, grid=(T // tm,), in_specs=[_rows(tm, C), _vec(C), _vec(C)], out_specs=_rows(tm, C),
        out_shape=jax.ShapeDtypeStruct((T, C), BF16), compiler_params=_cp(("parallel",)))(zc, g, b)


def conf_ln_bwd(dzs, zc, g, b, name):
    T, C = zc.shape
    tm = min(512, T)

    def body(d_ref, z_ref, g_ref, b_ref, o_ref, dg_ref, db_ref):
        i = pl.program_id(0)
        x = z_ref[...]
        xc = x - jnp.mean(x, axis=-1, keepdims=True)
        r = lax.rsqrt(jnp.mean(xc * xc, axis=-1, keepdims=True) + EPS)
        xh = xc * r
        zl = xh * g_ref[...] + b_ref[...]
        s = _sig(zl)
        dzl = d_ref[...].astype(F32) * (s * (1.0 + zl * (1.0 - s)))
        t = dzl * g_ref[...]
        o_ref[...] = r * (t - jnp.mean(t, axis=-1, keepdims=True) - xh * jnp.mean(t * xh, axis=-1, keepdims=True))
        _acc_rows(dg_ref, i, jnp.sum(dzl * xh, axis=0, keepdims=True))
        _acc_rows(db_ref, i, jnp.sum(dzl, axis=0, keepdims=True))

    return pl.pallas_call(
        body, name=name, grid=(T // tm,), in_specs=[_rows(tm, C), _rows(tm, C), _vec(C), _vec(C)],
        out_specs=[_rows(tm, C), _vec(C), _vec(C)],
        out_shape=[jax.ShapeDtypeStruct((T, C), F32), jax.ShapeDtypeStruct((1, C), F32),
                   jax.ShapeDtypeStruct((1, C), F32)],
        compiler_params=_cp(("arbitrary",)))(dzs, zc, g, b)


def sconv_fwd(bgx, wc, name):
    T = bgx.shape[0]

    def body(b_ref, c_ref, x_ref, w_ref, o_ref, zpad):
        _zero_rows(zpad, 0, PAD)

        def fill(r0):
            zpad[pl.ds(r0 + PAD, RC), :] = c_ref[pl.ds(r0, RC), :].astype(F32) * x_ref[pl.ds(r0, RC), :].astype(F32)
        _chunks(T, fill)
        wrow = lambda j: w_ref[KD - 1 - j:KD - j, :]

        def step(r0):
            qc = _causal_conv(zpad, wrow, KD, r0)
            o_ref[pl.ds(r0, RC), :] = (b_ref[pl.ds(r0, RC), :].astype(F32) * qc).astype(o_ref.dtype)
        _chunks(T, step)

    return pl.pallas_call(
        body, name=name, grid=(NSLAB,),
        in_specs=[_col(T, 0), _col(T, 1), _col(T, 2), pl.BlockSpec((8, SLAB), lambda c: (0, c))],
        out_specs=pl.BlockSpec((T, SLAB), lambda c: (0, c)),
        out_shape=jax.ShapeDtypeStruct((T, DM), BF16),
        scratch_shapes=[pltpu.VMEM((T + PAD, SLAB), F32)],
        compiler_params=_cp(("parallel",)))(bgx, bgx, bgx, wc)


def sconv_bwd(dy, bgx, wc, name):
    T = bgx.shape[0]

    def body(d_ref, b_ref, c_ref, x_ref, w_ref, db_ref, dc_ref, dx_ref, dw_ref, zpad, gpad, dwacc):
        _zero_rows(zpad, 0, PAD)
        _zero_rows(gpad, T, PAD)
        dwacc[...] = jnp.zeros_like(dwacc)

        def fill(r0):
            zpad[pl.ds(r0 + PAD, RC), :] = c_ref[pl.ds(r0, RC), :].astype(F32) * x_ref[pl.ds(r0, RC), :].astype(F32)
            gpad[pl.ds(r0, RC), :] = d_ref[pl.ds(r0, RC), :].astype(F32) * b_ref[pl.ds(r0, RC), :].astype(F32)
        _chunks(T, fill)
        wrow = lambda j: w_ref[KD - 1 - j:KD - j, :]

        def step(r0):
            qc = _causal_conv(zpad, wrow, KD, r0)
            db_ref[pl.ds(r0, RC), :] = (d_ref[pl.ds(r0, RC), :].astype(F32) * qc).astype(db_ref.dtype)
            dq = _anticausal_conv(gpad, wrow, KD, r0)
            dc_ref[pl.ds(r0, RC), :] = (dq * x_ref[pl.ds(r0, RC), :].astype(F32)).astype(dc_ref.dtype)
            dx_ref[pl.ds(r0, RC), :] = (dq * c_ref[pl.ds(r0, RC), :].astype(F32)).astype(dx_ref.dtype)
            _conv_dw(gpad, zpad, dwacc, KD, r0)
        _chunks(T, step)
        dw_ref[...] = jnp.zeros_like(dw_ref)
        for k in range(KD):
            dw_ref[k:k + 1, :] = jnp.sum(dwacc[KD - 1 - k], axis=0, keepdims=True)

    w8 = pl.BlockSpec((8, SLAB), lambda c: (0, c))
    return pl.pallas_call(
        body, name=name, grid=(NSLAB,),
        in_specs=[pl.BlockSpec((T, SLAB), lambda c: (0, c)), _col(T, 0), _col(T, 1), _col(T, 2), w8],
        out_specs=[_col(T, 0), _col(T, 0), _col(T, 0), w8],
        out_shape=[jax.ShapeDtypeStruct((T, DM), BF16)] * 3 + [jax.ShapeDtypeStruct((8, DM), F32)],
        scratch_shapes=[pltpu.VMEM((T + PAD, SLAB), F32), pltpu.VMEM((T + PAD, SLAB), F32),
                        pltpu.VMEM((8, 8, SLAB), F32)],
        compiler_params=_cp(("parallel",)))(dy, bgx, bgx, bgx, wc)


def merge_cols(parts, name):
    T = parts[0].shape[0]
    n = len(parts)
    C = n * DM
    tm = min(512, T)

    def body(*refs):
        o_ref = refs[n]
        for j in range(n):
            o_ref[:, j * DM:(j + 1) * DM] = refs[j][...]

    return pl.pallas_call(
        body, name=name, grid=(T // tm,),
        in_specs=[_rows(tm, DM) for j in range(n)],
        out_specs=_rows(tm, C), out_shape=jax.ShapeDtypeStruct((T, C), parts[0].dtype),
        compiler_params=_cp(("parallel",)))(*parts)


def adamw(w, g, m, v, name):
    shape = w.shape
    R, C = shape[-2], shape[-1]
    L = w.size // (R * C)
    w2, g2, m2, v2 = (a.reshape(L, R, C) for a in (w, g, m, v))
    tr = R
    while tr * C > 512 * 1024 and tr % 16 == 0:
        tr //= 2
    bc1 = 1.0 - ADAM_B1 ** ADAM_STEP
    bc2 = 1.0 - ADAM_B2 ** ADAM_STEP

    def body(w_ref, g_ref, m_ref, v_ref, d_ref, nm_ref, nv_ref):
        gg = g_ref[...]
        nm = ADAM_B1 * m_ref[...] + (1.0 - ADAM_B1) * gg
        nv = ADAM_B2 * v_ref[...] + (1.0 - ADAM_B2) * (gg * gg)
        nm_ref[...] = nm
        nv_ref[...] = nv
        d_ref[...] = -ADAM_LR * ((nm / bc1) / (jnp.sqrt(nv / bc2) + ADAM_EPS) + ADAM_WD * w_ref[...])

    spec = pl.BlockSpec((None, tr, C), lambda l, i: (l, i, 0))
    outs = pl.pallas_call(
        body, name=name, grid=(L, R // tr), in_specs=[spec] * 4, out_specs=[spec] * 3,
        out_shape=[jax.ShapeDtypeStruct((L, R, C), F32)] * 3,
        compiler_params=_cp(("parallel", "parallel")))(w2, g2, m2, v2)
    return tuple(o.reshape(shape) for o in outs)


def _place():
    x, y, c = lax.axis_index("x"), lax.axis_index("y"), lax.axis_index("c")
    return x, y, c


def all_gather_chips(bufs, name):
    n = len(bufs)
    me_ = 2 * lax.axis_index("x") + lax.axis_index("y")
    slots = [lax.dynamic_update_slice(lax.empty((NCHIP,) + b.shape, b.dtype), b[None], (me_, 0, 0)) for b in bufs]

    def body(*refs):
        dst = refs[n:2 * n]
        send, recv = refs[2 * n:]
        x, y, c = _place()
        me = 2 * x + y
        sib = (x, y, 1 - c)
        chips = [(1 - x, y), (x, 1 - y), (1 - x, 1 - y)]

        def half(b, slot, hc):
            rows = bufs[b].shape[0] // 2
            return dst[b].at[slot, pl.ds(hc * rows, rows), :]

        def remote(k, s, d, to):
            return pltpu.make_async_remote_copy(src_ref=s, dst_ref=d, send_sem=send.at[k], recv_sem=recv.at[k],
                                                device_id=to, device_id_type=MESH)

        first = []
        for b in range(n):
            for j, (cx, cy) in enumerate(chips):
                first.append(remote(b * 6 + j, half(b, me, c), half(b, me, c), (cx, cy, c)))
        for cp in first:
            cp.start()
        passed = []
        for b in range(n):
            for j, (cx, cy) in enumerate(chips):
                slot = 2 * cx + cy
                remote(b * 6 + j, half(b, slot, c), half(b, slot, c), (cx, cy, c)).wait_recv()
                fwd = remote(b * 6 + 3 + j, half(b, slot, c), half(b, slot, c), sib)
                fwd.start()
                passed.append(fwd)
        for b in range(n):
            for j, (cx, cy) in enumerate(chips):
                slot = 2 * cx + cy
                remote(b * 6 + 3 + j, half(b, slot, 1 - c), half(b, slot, 1 - c), sib).wait_recv()
        for cp in first + passed:
            cp.wait_send()

    return pl.pallas_call(
        body, name=name, in_specs=[ANY] * n, out_specs=[ANY] * n,
        out_shape=[jax.ShapeDtypeStruct(s.shape, s.dtype) for s in slots],
        input_output_aliases={b: b for b in range(n)},
        scratch_shapes=[pltpu.SemaphoreType.DMA((6 * n,)), pltpu.SemaphoreType.DMA((6 * n,))],
        compiler_params=pltpu.CompilerParams())(*slots)


def pair_exchange(bufs, name):
    n = len(bufs)

    def body(*refs):
        src, dst = refs[:n], refs[n:2 * n]
        send, recv = refs[2 * n:]
        x, y, c = _place()
        cps = []
        for b in range(n):
            rows = bufs[b].shape[1] // 2
            cp = pltpu.make_async_remote_copy(
                src_ref=src[b].at[:, pl.ds((1 - c) * rows, rows), :], dst_ref=dst[b],
                send_sem=send.at[b], recv_sem=recv.at[b], device_id=(x, y, 1 - c), device_id_type=MESH)
            cp.start()
            cps.append(cp)
        for cp in cps:
            cp.wait()

    return pl.pallas_call(
        body, name=name, in_specs=[ANY] * n, out_specs=[ANY] * n,
        out_shape=[jax.ShapeDtypeStruct((NCHIP, b.shape[1] // 2, b.shape[2]), b.dtype) for b in bufs],
        scratch_shapes=[pltpu.SemaphoreType.DMA((n,)), pltpu.SemaphoreType.DMA((n,))],
        compiler_params=pltpu.CompilerParams())(*bufs)


def add_half(full, got, tr, tc, name):
    _, R, C = full.shape
    rows = R // 2
    nr = rows // tr
    c_arr = lax.axis_index("c").astype(jnp.int32).reshape(1)

    def body(c_ref, a_ref, b_ref, o_ref):
        o_ref[...] = (a_ref[...].astype(F32) + b_ref[...].astype(F32)).astype(o_ref.dtype)

    return pl.pallas_call(
        body, name=name,
        grid_spec=pltpu.PrefetchScalarGridSpec(
            num_scalar_prefetch=1, grid=(NCHIP, nr, C // tc),
            in_specs=[pl.BlockSpec((None, tr, tc), lambda s, i, j, c_ref: (s, c_ref[0] * nr + i, j)),
                      pl.BlockSpec((None, tr, tc), lambda s, i, j, c_ref: (s, i, j))],
            out_specs=pl.BlockSpec((None, tr, tc), lambda s, i, j, c_ref: (s, i, j))),
        out_shape=jax.ShapeDtypeStruct((NCHIP, rows, C), full.dtype),
        compiler_params=_cp(("parallel", "parallel", "parallel")))(c_arr, full, got)


def chip_exchange(bufs, name):
    n = len(bufs)

    def body(*refs):
        src, dst = refs[:n], refs[n:2 * n]
        send, recv, lsem = refs[2 * n:]
        x, y, c = _place()
        me = 2 * x + y
        chips = [(1 - x, y), (x, 1 - y), (1 - x, 1 - y)]
        local = [pltpu.make_async_copy(src[b].at[me], dst[b].at[me], lsem.at[b]) for b in range(n)]
        for cp in local:
            cp.start()
        cps = []
        for b in range(n):
            for j, (cx, cy) in enumerate(chips):
                cp = pltpu.make_async_remote_copy(
                    src_ref=src[b].at[2 * cx + cy], dst_ref=dst[b].at[me],
                    send_sem=send.at[b * 3 + j], recv_sem=recv.at[b * 3 + j],
                    device_id=(cx, cy, c), device_id_type=MESH)
                cp.start()
                cps.append((cp, b, cx, cy, j))
        for cp, b, cx, cy, j in cps:
            cp.wait_send()
            pltpu.make_async_remote_copy(
                src_ref=src[b].at[me], dst_ref=dst[b].at[2 * cx + cy],
                send_sem=send.at[b * 3 + j], recv_sem=recv.at[b * 3 + j],
                device_id=(cx, cy, c), device_id_type=MESH).wait_recv()
        for cp in local:
            cp.wait()

    return pl.pallas_call(
        body, name=name, in_specs=[ANY] * n, out_specs=[ANY] * n,
        out_shape=[jax.ShapeDtypeStruct(b.shape, b.dtype) for b in bufs],
        scratch_shapes=[pltpu.SemaphoreType.DMA((3 * n,)), pltpu.SemaphoreType.DMA((3 * n,)),
                        pltpu.SemaphoreType.DMA((n,))],
        compiler_params=pltpu.CompilerParams())(*bufs)


def sum_slots(buf, tr, tc, name):
    _, r, C = buf.shape
    nr = r // tr
    c_arr = lax.axis_index("c").astype(jnp.int32).reshape(1)

    def body(c_ref, a_ref, o_ref):
        o_ref[...] = ((a_ref[0].astype(F32) + a_ref[1].astype(F32)) + a_ref[2].astype(F32)) + a_ref[3].astype(F32)

    return pl.pallas_call(
        body, name=name,
        grid_spec=pltpu.PrefetchScalarGridSpec(
            num_scalar_prefetch=1, grid=(nr, C // tc),
            in_specs=[pl.BlockSpec((NCHIP, tr, tc), lambda i, j, c_ref: (0, i, j))],
            out_specs=pl.BlockSpec((tr, tc), lambda i, j, c_ref: (c_ref[0] * nr + i, j))),
        out_shape=jax.ShapeDtypeStruct((2 * r, C), F32),
        compiler_params=_cp(("parallel", "parallel")))(c_arr, buf)


def pair_share(bufs, name):
    n = len(bufs)

    def body(*refs):
        dst = refs[n:2 * n]
        send, recv = refs[2 * n:]
        x, y, c = _place()
        cps = []
        for b in range(n):
            rows = bufs[b].shape[0] // 2
            here = dst[b].at[pl.ds(c * rows, rows), :]
            cp = pltpu.make_async_remote_copy(src_ref=here, dst_ref=here, send_sem=send.at[b], recv_sem=recv.at[b],
                                              device_id=(x, y, 1 - c), device_id_type=MESH)
            cp.start()
            cps.append((cp, b))
        for cp, b in cps:
            rows = bufs[b].shape[0] // 2
            there = dst[b].at[pl.ds((1 - c) * rows, rows), :]
            cp.wait_send()
            pltpu.make_async_remote_copy(src_ref=there, dst_ref=there, send_sem=send.at[b], recv_sem=recv.at[b],
                                         device_id=(x, y, 1 - c), device_id_type=MESH).wait_recv()

    return pl.pallas_call(
        body, name=name, in_specs=[ANY] * n, out_specs=[ANY] * n,
        out_shape=[jax.ShapeDtypeStruct(b.shape, b.dtype) for b in bufs],
        input_output_aliases={b: b for b in range(n)},
        scratch_shapes=[pltpu.SemaphoreType.DMA((n,)), pltpu.SemaphoreType.DMA((n,))],
        compiler_params=pltpu.CompilerParams())(*bufs)


def _tile(kind, buf):
    return {"A": (256, buf.shape[2]), "B": (buf.shape[1], 1024), "C": (128, 256), "V": (40, 256),
            "E": (40, 1024)}[kind]


def reduce_scatter(parts, tag):
    names = list(parts)
    got = pair_exchange([parts[k] for k in names], tag + "_pair_exchange")
    sums = [add_half(parts[k], got[i], *_tile(k[0], got[i]), name=tag + "_add_pair_" + k) for i, k in enumerate(names)]
    landed = chip_exchange(sums, tag + "_chip_exchange")
    halves = [sum_slots(landed[i], *_tile(k[0], landed[i]), name=tag + "_sum_chips_" + k) for i, k in enumerate(names)]
    full = pair_share(halves, tag + "_pair_share")
    return dict(zip(names, full))


HBM = pl.BlockSpec(memory_space=pltpu.HBM)
SEMS = pl.BlockSpec(memory_space=pltpu.SEMAPHORE)
FLOWS = pltpu.SideEffectType.DATAFLOW_SIDE_EFFECTING


def _in_hbm(a):
    return pltpu.with_memory_space_constraint(a, pltpu.HBM)


def _other_chips():
    x, y, c = _place()
    return 2 * x + y, c, [(1 - x, y), (x, 1 - y), (1 - x, 1 - y)]


def own_slots(bufs):
    me = 2 * lax.axis_index("x") + lax.axis_index("y")
    return [lax.dynamic_update_slice(lax.empty((NCHIP,) + b.shape, b.dtype), b[None], (me, 0, 0)) for b in bufs]


def gather_start(slots, after, name):
    n = len(slots)

    def body(*refs):
        ins = refs[:n]
        send, recv = refs[n + 1], refs[n + 2]
        token = refs[2 * n + 3]
        me, c, chips = _other_chips()
        for b in range(n):
            rows = slots[b].shape[1] // 2
            own = ins[b].at[me, pl.ds(c * rows, rows), :]
            for j, (cx, cy) in enumerate(chips):
                pltpu.make_async_remote_copy(src_ref=own, dst_ref=own, send_sem=send.at[3 * b + j],
                                             recv_sem=recv.at[3 * b + j], device_id=(cx, cy, c),
                                             device_id_type=MESH).start()
        token[...] = jnp.zeros_like(token)

    out = pl.pallas_call(
        body, name=name, in_specs=[HBM] * n + [ANY],
        out_specs=[SEMS, SEMS] + [HBM] * n + [pl.BlockSpec(memory_space=pltpu.VMEM)],
        out_shape=[pltpu.SemaphoreType.DMA((3 * n,)), pltpu.SemaphoreType.DMA((3 * n,))]
        + [pltpu.HBM(s.shape, s.dtype) for s in slots] + [jax.ShapeDtypeStruct((8, 128), F32)],
        input_output_aliases={b: b + 2 for b in range(n)},
        compiler_params=pltpu.CompilerParams(has_side_effects=FLOWS))(*[_in_hbm(s) for s in slots], after)
    return out[0], out[1], list(out[2:2 + n]), out[2 + n]


def gather_wait(send, recv, slots, picks, after, name):
    n = len(slots)

    def body(*refs):
        ins = refs[:n]
        send_, recv_ = refs[n], refs[n + 1]
        me, c, chips = _other_chips()
        for i, b in enumerate(picks):
            rows = slots[i].shape[1] // 2
            own = ins[i].at[me, pl.ds(c * rows, rows), :]
            for j, (cx, cy) in enumerate(chips):
                got = ins[i].at[2 * cx + cy, pl.ds(c * rows, rows), :]
                pltpu.make_async_remote_copy(src_ref=own, dst_ref=own, send_sem=send_.at[3 * b + j],
                                             recv_sem=recv_.at[3 * b + j], device_id=(cx, cy, c),
                                             device_id_type=MESH).wait_send()
                pltpu.make_async_remote_copy(src_ref=got, dst_ref=got, send_sem=send_.at[3 * b + j],
                                             recv_sem=recv_.at[3 * b + j], device_id=(cx, cy, c),
                                             device_id_type=MESH).wait_recv()

    return pl.pallas_call(
        body, name=name, in_specs=[HBM] * n + [SEMS, SEMS, ANY], out_specs=[HBM] * n,
        out_shape=[pltpu.HBM(s.shape, s.dtype) for s in slots],
        input_output_aliases={b: b for b in range(n)},
        compiler_params=pltpu.CompilerParams(has_side_effects=FLOWS))(*slots, send, recv, after)


def gather_pass(slots, name):
    n = len(slots)

    def body(*refs):
        dst = refs[n:2 * n]
        send, recv = refs[2 * n:]
        x, y, c = _place()
        sib = (x, y, 1 - c)
        chips = [(1 - x, y), (x, 1 - y), (1 - x, 1 - y)]

        def half(b, slot, hc):
            rows = slots[b].shape[1] // 2
            return dst[b].at[slot, pl.ds(hc * rows, rows), :]

        passed = []
        for b in range(n):
            for j, (cx, cy) in enumerate(chips):
                slot = 2 * cx + cy
                cp = pltpu.make_async_remote_copy(src_ref=half(b, slot, c), dst_ref=half(b, slot, c),
                                                  send_sem=send.at[3 * b + j], recv_sem=recv.at[3 * b + j],
                                                  device_id=sib, device_id_type=MESH)
                cp.start()
                passed.append(cp)
        for b in range(n):
            for j, (cx, cy) in enumerate(chips):
                slot = 2 * cx + cy
                pltpu.make_async_remote_copy(src_ref=half(b, slot, 1 - c), dst_ref=half(b, slot, 1 - c),
                                             send_sem=send.at[3 * b + j], recv_sem=recv.at[3 * b + j],
                                             device_id=sib, device_id_type=MESH).wait_recv()
        for cp in passed:
            cp.wait_send()

    return pl.pallas_call(
        body, name=name, in_specs=[ANY] * n, out_specs=[ANY] * n,
        out_shape=[jax.ShapeDtypeStruct(s.shape, s.dtype) for s in slots],
        input_output_aliases={b: b for b in range(n)},
        scratch_shapes=[pltpu.SemaphoreType.DMA((3 * n,)), pltpu.SemaphoreType.DMA((3 * n,))],
        compiler_params=pltpu.CompilerParams())(*slots)


def chip_exchange_start(sums, name):
    n = len(sums)
    me_ = 2 * lax.axis_index("x") + lax.axis_index("y")
    landing = [lax.dynamic_update_slice(lax.empty(s.shape, s.dtype),
                                        lax.dynamic_slice(s, (me_, 0, 0), (1,) + s.shape[1:]), (me_, 0, 0)) for s in sums]

    def body(*refs):
        src, land = refs[:n], refs[n:2 * n]
        send, recv = refs[2 * n], refs[2 * n + 1]
        token = refs[4 * n + 2]
        me, c, chips = _other_chips()
        for b in range(n):
            for j, (cx, cy) in enumerate(chips):
                pltpu.make_async_remote_copy(src_ref=src[b].at[2 * cx + cy], dst_ref=land[b].at[me],
                                             send_sem=send.at[3 * b + j], recv_sem=recv.at[3 * b + j],
                                             device_id=(cx, cy, c), device_id_type=MESH).start()
        token[...] = jnp.zeros_like(token)

    out = pl.pallas_call(
        body, name=name, in_specs=[HBM] * (2 * n),
        out_specs=[SEMS, SEMS] + [HBM] * (2 * n) + [pl.BlockSpec(memory_space=pltpu.VMEM)],
        out_shape=[pltpu.SemaphoreType.DMA((3 * n,)), pltpu.SemaphoreType.DMA((3 * n,))]
        + [pltpu.HBM(s.shape, s.dtype) for s in sums + landing] + [jax.ShapeDtypeStruct((8, 128), F32)],
        input_output_aliases={b: b + 2 for b in range(2 * n)},
        compiler_params=pltpu.CompilerParams(has_side_effects=FLOWS))(*[_in_hbm(s) for s in sums + landing])
    return out[0], out[1], list(out[2:2 + n]), list(out[2 + n:2 + 2 * n]), out[2 + 2 * n]


def chip_exchange_wait(send, recv, sums, landing, after, name):
    n = len(sums)

    def body(*refs):
        src, land = refs[:n], refs[n:2 * n]
        send_, recv_ = refs[2 * n], refs[2 * n + 1]
        me, c, chips = _other_chips()
        for b in range(n):
            for j, (cx, cy) in enumerate(chips):
                slot = 2 * cx + cy
                pltpu.make_async_remote_copy(src_ref=src[b].at[slot], dst_ref=land[b].at[me],
                                             send_sem=send_.at[3 * b + j], recv_sem=recv_.at[3 * b + j],
                                             device_id=(cx, cy, c), device_id_type=MESH).wait_send()
                pltpu.make_async_remote_copy(src_ref=src[b].at[me], dst_ref=land[b].at[slot],
                                             send_sem=send_.at[3 * b + j], recv_sem=recv_.at[3 * b + j],
                                             device_id=(cx, cy, c), device_id_type=MESH).wait_recv()

    out = pl.pallas_call(
        body, name=name, in_specs=[HBM] * (2 * n) + [SEMS, SEMS, ANY], out_specs=[HBM] * (2 * n),
        out_shape=[pltpu.HBM(s.shape, s.dtype) for s in sums + landing],
        input_output_aliases={b: b for b in range(2 * n)},
        compiler_params=pltpu.CompilerParams(has_side_effects=FLOWS))(*sums, *landing, send, recv, after)
    return list(out[n:])


def _row(a, l):
    return a[l:l + 1]


def local_step(x, p, tgt, small, weights_of, vecs, a_ws, a_bs, grads_ready):
    T = x.shape[0]
    bsf = jnp.broadcast_to(a_bs[:, :, None], (AG_N, CHUNK, AGW))
    vrow = lambda r: vecs[r:r + 1]
    saved = []
    W = []
    h = x
    GA1 = GB1 = GA = GB = GC = bgrp = None

    def ff_fwd(h, xn, l, which, post, g_next, tok=None):
        wa, wb = (GA1, GB1) if which == 1 else (GA, GB)
        tag = "ff%d_l%d" % (which, l)
        gu, a = ff_gateup(xn, wa, 0, tag + "_gateup")
        gp = _row(post, l) if tok is None else _row(post, l) + tok
        out = mm_rs_post(a, wb, 0, FB, FB, h, gp, 0.5, tag + "_down", g_next=g_next)
        return out[1], (out[2] if g_next is not None else None), (h, xn, gu, a, out[0])

    xn = rms_fwd(h, _row(small["ff1_pre_g"], 0), BF16, "ff1_l0_pre")
    for l in range(4):
        rec = {}
        GA1, GB1, atok = weights_of(l, "a", h)
        g_mix = _row(small["mix_pre_g"], l) if l >= 2 else None
        h, hn, rec["ff1"] = ff_fwd(h, xn, l, 1, small["ff1_post_g"], g_mix, atok)
        GA, GB, GC, wtok = weights_of(l, "b", h)
        W.append((GA1, GB1, GA, GB, GC))
        if l == 1:
            bgrp = GC[:, C_BGRP:C_BGRP + 256, :].reshape(NCHIP, 4, 64, 256).transpose(1, 0, 2, 3).reshape(4, 256, 256)
        tag = "mix_l%d" % l
        h_in = h
        g_ff2 = _row(small["ff2_pre_g"], l)
        if l == 1:
            hn = rms_fwd(h, _row(small["mix_pre_g"], l), F32, tag + "_pre")
            pooled, ypre, f = pool_fwd(hn, bgrp, vrow(V_BSCALE), tag + "_pool")
            rec["mix"] = (h_in, pooled, ypre, f)
            h = post_res(h, f, _row(small["mix_post_g"], l), 1.0, tag + "_post")
            xn = rms_fwd(h, g_ff2, BF16, "ff2_l1_pre")
        else:
            gpost = _row(small["mix_post_g"], l)
            if l == 0:
                g0 = _row(small["mix_pre_g"], l)
                hn = rms_fwd(h, g0 if wtok is None else g0 + wtok, BF16, tag + "_pre")
                zpre = mm_cs(hn, GA, A_AIN, 1536, 1536, BF16, tag + "_in")
                y = gmlp_mid_fwd(zpre, small["a_v_norm_g"], small["a_v_norm_b"], a_ws, bsf, tag + "_gate")
                f, h, xn = mm_rs_post(y, GB, B_AOUT, 768, 768, h, gpost, 1.0, tag + "_out", g_next=g_ff2)
                rec["mix"] = (h_in, hn, zpre, y, f)
            elif l == 2:
                ag = mm_cs(hn, GA, A_CIN, 512, 512, BF16, tag + "_pw1")
                zc = conf_conv_fwd(ag, vecs[V_CDW:V_CDW + 32], vrow(V_CBDW), tag + "_conv")
                zs = conf_ln_fwd(zc, vrow(V_CNG), vrow(V_CNB), tag + "_ln")
                f, h, xn = mm_rs_post(zs, GB, B_CPW2, 256, 256, h, gpost, 1.0, tag + "_pw2", g_next=g_ff2)
                rec["mix"] = (h_in, hn, ag, zc, zs, f)
            else:
                bgx = mm_cs(hn, GA, A_DIN, 768, 768, BF16, tag + "_in")
                y = sconv_fwd(bgx, vecs[V_DCONV:V_DCONV + 8], tag + "_conv")
                f, h, xn = mm_rs_post(y, GB, B_DOUT, 256, 256, h, gpost, 1.0, tag + "_out", g_next=g_ff2)
                rec["mix"] = (h_in, hn, bgx, y, f)
        h, xn, rec["ff2"] = ff_fwd(h, xn, l, 2, small["ff2_post_g"], _row(small["ple_gate_norm_g"], l))
        tag = "ple_l%d" % l
        pb = p[l].astype(BF16)
        h_in = h
        g_next = _row(small["ff1_pre_g"], l + 1) if l < 3 else None
        out = ple_fwd(h, xn, pb, GB, B_PLEG(l), GC, _row(small["ple_post_g"], l), tag, g_next=g_next)
        rec["ple"] = (h_in, xn, out[0], out[1], pb)
        h = out[2]
        xn = out[3] if l < 3 else None
        saved.append(rec)

    dh, loss_cols = loss_head(h, tgt, "loss_head")

    dA2 = dB2 = None
    layer_grads = [None] * 4
    tok = None
    gV = {}
    gains = {k: [None] * 4 for k in ("ff1_pre_g", "ff1_post_g", "mix_pre_g", "mix_post_g", "ff2_pre_g", "ff2_post_g",
                                      "ple_gate_norm_g", "ple_post_g")}
    extra = {}

    def ff_bwd(dh, l, which, pre, post, rec, after=None):
        wa, wb = (GA1, GB1) if which == 1 else (GA, GB)
        tag = "ff%d_l%d_b" % (which, l)
        h_in, xn, gu, a, f = rec
        gp = _row(post, l) if after is None else _row(post, l) + after
        df, dpost, dgu = ff_bwd_down(dh, f, gp, wb, 0, gu, tag + "_down")
        if which == 1:
            db = dw_rs(a, df, FB, FB, tag + "_dwdown")
        else:
            db = dw_rs(a, df, FB, FB, tag + "_dwdown", height=B2_ROWS(l), off=B_FF2D(l), into=dB2)
        dh_in, dpre = mm_cs_t_rms(dgu, wa, 0, 2 * FB, 2 * FB, h_in, _row(pre, l), dh, tag + "_gateup")
        da = dw_cs(xn, dgu, 2 * FB, 2 * FB, tag + "_dwgateup", width=None if which == 1 else A2_COLS(l))
        return dh_in, dpre, dpost, (da, db)

    for l in reversed(range(4)):
        rec = saved[l]
        GA1, GB1, GA, GB, GC = W[l]
        gC = {}
        tag = "ple_l%d_b" % l
        h_in, xn, zg, pe, pb = rec["ple"]
        gpost = _row(small["ple_post_g"], l)
        if tok is not None:
            gpost = gpost + tok
        dzg, dpe, dh, gains["ple_post_g"][l], gains["ple_gate_norm_g"][l] = ple_bwd(
            dh, zg, pe, h_in, GB, B_PLEG(l), gpost, _row(small["ple_gate_norm_g"], l), tag)
        gC[C_PROJ(l)] = dw_cs(pb, dpe, 256, 256, tag + "_dwproj")
        dB2 = dw_rs(xn, dzg, 256, 256, tag + "_dwgate", height=B2_ROWS(l), off=B_PLEG(l))

        dh, gains["ff2_pre_g"][l], gains["ff2_post_g"][l], (dA2, dB2) = ff_bwd(
            dh, l, 2, small["ff2_pre_g"], small["ff2_post_g"], rec["ff2"])

        tag = "mix_l%d_b" % l
        mix = rec["mix"]
        h_in, f = mix[0], mix[-1]
        if l == 1:
            _, pooled, ypre, _ = mix
            df, gains["mix_post_g"][l] = post_res_bwd(dh, f, _row(small["mix_post_g"], l), 1.0, F32, tag + "_post")
            dhn, dwg, dsc = pool_bwd(df, ypre, pooled, bgrp, vrow(V_BSCALE), tag + "_pool")
            gC[C_BGRP] = dwg.astype(BF16).reshape(4, NCHIP, 64, 256).transpose(1, 0, 2, 3).reshape(NCHIP, 256, 256)
            gV[V_BSCALE] = jnp.pad(dsc, ((0, 7), (0, 0)))
            dh, gains["mix_pre_g"][l] = rms_bwd(dhn, h_in, _row(small["mix_pre_g"], l), dh, tag + "_pre")
        else:
            gpre = _row(small["mix_pre_g"], l)
            df, gains["mix_post_g"][l] = post_res_bwd(dh, f, _row(small["mix_post_g"], l), 1.0, BF16, tag + "_post")
            if l == 0:
                _, hn, zpre, y, _ = mix
                dy = mm_rs_t(df, GB, B_AOUT, 768, 768, tag + "_out")
                dB2 = dw_rs(y, df, 768, 768, tag + "_dwout", height=B2_ROWS(l), off=B_AOUT, into=dB2)
                dz, dws, dbs, dvg, dvb = gmlp_mid_bwd(zpre, dy, small["a_v_norm_g"], small["a_v_norm_b"], a_ws, bsf,
                                                      tag + "_gate")
                extra.update(a_w_s=dws, a_b_s=dbs[:, 0, :], a_v_norm_g=dvg, a_v_norm_b=dvb)
                dA2 = dw_cs(hn, dz, 1536, 1536, tag + "_dwin", width=A2_COLS(l), off=A_AIN, into=dA2)
                dh, gains["mix_pre_g"][l] = mm_cs_t_rms(dz, GA, A_AIN, 1536, 1536, h_in, gpre, dh, tag + "_in")
            elif l == 2:
                _, hn, ag, zc, zs, _ = mix
                dzs = mm_rs_t(df, GB, B_CPW2, 256, 256, tag + "_pw2")
                dB2 = dw_rs(zs, df, 256, 256, tag + "_dwpw2", height=B2_ROWS(l), off=B_CPW2, into=dB2)
                dzc, dng, dnb = conf_ln_bwd(dzs, zc, vrow(V_CNG), vrow(V_CNB), tag + "_ln")
                da_, dg_, dwdw, dbdw = conf_conv_bwd(dzc, ag, vecs[V_CDW:V_CDW + 32], tag + "_conv")
                dag = merge_cols([da_, dg_], tag + "_merge")
                gV[V_CDW] = dwdw
                gV[V_CBDW] = jnp.pad(dbdw, ((0, 7), (0, 0)))
                gV[V_CNG] = jnp.pad(dng, ((0, 7), (0, 0)))
                gV[V_CNB] = jnp.pad(dnb, ((0, 7), (0, 0)))
                dA2 = dw_cs(hn, dag, 512, 512, tag + "_dwpw1", width=A2_COLS(l), off=A_CIN, into=dA2)
                dh, gains["mix_pre_g"][l] = mm_cs_t_rms(dag, GA, A_CIN, 512, 512, h_in, gpre, dh, tag + "_pw1")
            else:
                _, hn, bgx, y, _ = mix
                dy = mm_rs_t(df, GB, B_DOUT, 256, 256, tag + "_out")
                dB2 = dw_rs(y, df, 256, 256, tag + "_dwout", height=B2_ROWS(l), off=B_DOUT, into=dB2)
                db_, dc_, dx_, dwc = sconv_bwd(dy, bgx, vecs[V_DCONV:V_DCONV + 8], tag + "_conv")
                dbgx = merge_cols([db_, dc_, dx_], tag + "_merge")
                gV[V_DCONV] = dwc
                dA2 = dw_cs(hn, dbgx, 768, 768, tag + "_dwin", width=A2_COLS(l), off=A_DIN, into=dA2)
                dh, gains["mix_pre_g"][l] = mm_cs_t_rms(dbgx, GA, A_DIN, 768, 768, h_in, gpre, dh, tag + "_in")

        dC = jnp.concatenate([gC[C_PROJ(l)]] + ([gC[C_BGRP]] if l == 1 else []), axis=1)
        tok = grads_ready(l, "b", (dA2, dB2, dC), dh)
        dh, gains["ff1_pre_g"][l], gains["ff1_post_g"][l], (dA1, dB1) = ff_bwd(
            dh, l, 1, small["ff1_pre_g"], small["ff1_post_g"], rec["ff1"], after=tok)
        layer_grads[l] = (dA1, dB1, dA2, dB2, dC)
        tok = grads_ready(l, "a", (dA1, dB1), dh)

    return loss_cols, dh, layer_grads, gV, gains, extra


GAIN_NAMES = ("ff1_pre_g", "ff1_post_g", "mix_pre_g", "mix_post_g", "ff2_pre_g", "ff2_post_g", "ple_gate_norm_g",
              "ple_post_g")


def _pad_rows(a, rows):
    return jnp.pad(a, ((0, rows - a.shape[0]), (0, 0)))


def kernel(x, p, ff1_pre_g, ff1_w_gate, ff1_w_up, ff1_w_down, ff1_post_g, mix_pre_g, mix_post_g, ff2_pre_g, ff2_w_gate, ff2_w_up, ff2_w_down, ff2_post_g, ple_gate_norm_g, ple_w_gate, ple_w_proj, ple_post_g, a_w_in, a_v_norm_g, a_v_norm_b, a_w_s, a_b_s, a_w_out, b_w_grp, b_scale, c_w_pw1, c_w_dw, c_b_dw, c_norm_g, c_norm_b, c_w_pw2, d_w_in, d_w_conv, d_w_out, loss_target, m_ff1_pre_g, m_ff1_w_gate, m_ff1_w_up, m_ff1_w_down, m_ff1_post_g, m_mix_pre_g, m_mix_post_g, m_ff2_pre_g, m_ff2_w_gate, m_ff2_w_up, m_ff2_w_down, m_ff2_post_g, m_ple_gate_norm_g, m_ple_w_gate, m_ple_w_proj, m_ple_post_g, m_a_w_in, m_a_v_norm_g, m_a_v_norm_b, m_a_w_s, m_a_b_s, m_a_w_out, m_b_w_grp, m_b_scale, m_c_w_pw1, m_c_w_dw, m_c_b_dw, m_c_norm_g, m_c_norm_b, m_c_w_pw2, m_d_w_in, m_d_w_conv, m_d_w_out, v_ff1_pre_g, v_ff1_w_gate, v_ff1_w_up, v_ff1_w_down, v_ff1_post_g, v_mix_pre_g, v_mix_post_g, v_ff2_pre_g, v_ff2_w_gate, v_ff2_w_up, v_ff2_w_down, v_ff2_post_g, v_ple_gate_norm_g, v_ple_w_gate, v_ple_w_proj, v_ple_post_g, v_a_w_in, v_a_v_norm_g, v_a_v_norm_b, v_a_w_s, v_a_b_s, v_a_w_out, v_b_w_grp, v_b_scale, v_c_w_pw1, v_c_w_dw, v_c_b_dw, v_c_norm_g, v_c_norm_b, v_c_w_pw2, v_d_w_in, v_d_w_conv, v_d_w_out):
    args = dict(locals())
    wnames = ["ff1_pre_g", "ff1_w_gate", "ff1_w_up", "ff1_w_down", "ff1_post_g", "mix_pre_g", "mix_post_g",
              "ff2_pre_g", "ff2_w_gate", "ff2_w_up", "ff2_w_down", "ff2_post_g", "ple_gate_norm_g", "ple_w_gate",
              "ple_w_proj", "ple_post_g", "a_w_in", "a_v_norm_g", "a_v_norm_b", "a_w_s", "a_b_s", "a_w_out",
              "b_w_grp", "b_scale", "c_w_pw1", "c_w_dw", "c_b_dw", "c_norm_g", "c_norm_b", "c_w_pw2", "d_w_in",
              "d_w_conv", "d_w_out"]

    P = pack_weights(args)
    G0a = all_gather_chips([P[0][0], P[0][1], P[4]], "gather_l0a")
    vecs = G0a[2].transpose(1, 0, 2).reshape(V_ROWS, DM)
    flying = {}

    def start(key, bufs, after):
        send, recv, slots, token = gather_start(own_slots(list(bufs)), after, "gather_start_l" + key)
        flying[key] = (send, recv, slots)
        return token[0, 0]

    tok = start("0b", P[0][2:], G0a[0])
    arrived = {}

    def weights_of(l, part, h):
        if l == 0 and part == "a":
            return G0a[0], G0a[1], None
        key = "0b" if l == 0 else str(l)
        wtok = None
        if key not in arrived:
            send, recv, slots = flying[key]
            n = len(slots)
            landed = gather_wait(send, recv, slots, list(range(n)), h, "gather_wait_l" + key)
            arrived[key] = gather_pass(landed, "gather_pass_l" + key)
            if l < 3:
                wtok = start(str(l + 1), P[l + 1], arrived[key][0])
        got = arrived[key]
        if l == 0:
            return tuple(got) + (wtok,)
        return tuple(got[:2]) + (wtok,) if part == "a" else tuple(got[2:]) + (None,)

    pending = {}
    reduced = {}
    held = {}

    def finish(key, after):
        kinds, send, recv, sums, landing = pending.pop(key)
        landed = chip_exchange_wait(send, recv, sums, landing, after, "rs_wait_l" + key)
        halves = [sum_slots(landed[i], *_tile(k, landed[i]), name="rs_sum_chips_l%s_%d%s" % (key, i, k))
                  for i, k in enumerate(kinds)]
        reduced[key] = pair_share(halves, "rs_pair_share_l" + key)

    def grads_ready(l, part, bufs, dh):
        if part == "b" and l > 0:
            held[l] = list(bufs)
            return None
        if part == "a" and l > 0:
            key, kinds, parts = str(l), "ABABC", list(bufs) + held.pop(l)
        elif part == "b":
            key, kinds, parts = "0b", "ABC", list(bufs)
        else:
            finish("0b", dh)
            return None
        for other in list(pending):
            finish(other, dh)
        got = pair_exchange(parts, "rs_pair_exchange_l" + key)
        sums = [add_half(parts[i], got[i], *_tile(k, got[i]), name="rs_add_pair_l%s_%d%s" % (key, i, k))
                for i, k in enumerate(kinds)]
        send, recv, sums, landing, token = chip_exchange_start(sums, "rs_start_l" + key)
        pending[key] = (kinds, send, recv, sums, landing)
        return token[0, 0]

    small = {k: args[k] for k in GAIN_NAMES}
    small["ff1_pre_g"] = ff1_pre_g + tok
    small["a_v_norm_g"] = a_v_norm_g
    small["a_v_norm_b"] = a_v_norm_b
    loss_cols, grad_x, layer_grads, gV, gains, extra = local_step(
        x[0], p[:, 0], loss_target[0], small, weights_of, vecs, a_w_s[0], a_b_s[0], grads_ready)

    loss = lax.psum((0.5 / DM) * jnp.sum(loss_cols), ("x", "y", "c"))

    deltas, new_m, new_v = {}, {}, {}

    def update(k, g):
        if args[k].shape[-1] == FW:
            t = lambda a: jnp.swapaxes(a, 1, 2)
            outs = adamw(t(args[k]), t(g), t(args["m_" + k]), t(args["v_" + k]), "adamw_" + k)
            deltas[k], new_m[k], new_v[k] = (t(o) for o in outs)
        else:
            deltas[k], new_m[k], new_v[k] = adamw(args[k], g, args["m_" + k], args["v_" + k], "adamw_" + k)

    dV, dE = pack_small_grads(gV, gains, extra)
    red = reduce_scatter({"A": layer_grads[0][0], "B": layer_grads[0][1], "V": dV, "E": dE}, "rs_l0a")
    (gE,) = all_gather_chips([red["E"]], "gather_replicated_grads")
    per_layer = [[red["A"], red["B"]] + list(reduced["0b"])] + [list(reduced[str(l)]) for l in (1, 2, 3)]
    grads = unpack_grads(per_layer, red["V"], gE.reshape(E_ROWS, DM))
    for k in wnames:
        update(k, grads[k])
    return (loss, grad_x[None], *[grads[k] for k in wnames], *[deltas[k] for k in wnames],
            *[new_m[k] for k in wnames], *[new_v[k] for k in wnames])


def pack_weights(w):
    padc = lambda a: jnp.pad(a, ((0, 0), (0, FB - FW)))
    mix_in = [w["a_w_in"][0], None, w["c_w_pw1"][0], w["d_w_in"][0]]
    mix_out = [w["a_w_out"][0], None, w["c_w_pw2"][0], w["d_w_out"][0]]
    packed = []
    for l in range(4):
        a1 = jnp.concatenate([padc(w["ff1_w_gate"][l]), padc(w["ff1_w_up"][l])], axis=1).astype(BF16)
        b1 = _pad_rows(w["ff1_w_down"][l], FB).astype(BF16)
        cols = [padc(w["ff2_w_gate"][l]), padc(w["ff2_w_up"][l])]
        rows = [_pad_rows(w["ff2_w_down"][l], FB)]
        if l != 1:
            cols.append(mix_in[l])
            rows.append(mix_out[l])
        rows.append(w["ple_w_gate"][l])
        proj = [w["ple_w_proj"][l]] + ([w["b_w_grp"][0].reshape(256, 256)] if l == 1 else [])
        packed.append((a1, b1, jnp.concatenate(cols, axis=1).astype(BF16), jnp.concatenate(rows, axis=0).astype(BF16),
                       jnp.concatenate(proj, axis=0).astype(BF16)))
    PV = jnp.concatenate([_pad_rows(w["b_scale"], 8), _pad_rows(w["c_b_dw"], 8), _pad_rows(w["c_norm_g"], 8),
                          _pad_rows(w["c_norm_b"], 8), _pad_rows(w["d_w_conv"][0], 8), _pad_rows(w["c_w_dw"][0], 40)],
                         axis=0)
    return packed + [PV]


def pack_small_grads(gV, gains, extra):
    dVt = jnp.concatenate([gV[V_BSCALE], gV[V_CBDW], gV[V_CNG], gV[V_CNB], gV[V_DCONV], gV[V_CDW],
                           jnp.zeros((8, DM), F32)], axis=0)
    dV = dVt.reshape(V_ROWS, NCHIP, 256).transpose(1, 0, 2)
    rowsE = [_pad_rows(jnp.concatenate(gains[k], axis=0), 8) for k in GAIN_NAMES]
    rowsE += [_pad_rows(extra["a_v_norm_g"].reshape(3, DM), 8), _pad_rows(extra["a_v_norm_b"].reshape(3, DM), 8),
              jnp.pad(extra["a_b_s"].reshape(1536), (0, 8 * DM - 1536)).reshape(8, DM),
              extra["a_w_s"].reshape(192, DM)]
    dE = _pad_rows(jnp.concatenate(rowsE, axis=0), E_ROWS).reshape(NCHIP, E_ROWS // NCHIP, DM)
    return dV, dE


def unpack_grads(per_layer, RV, gE):
    grads = {}
    for i, k in enumerate(GAIN_NAMES):
        grads[k] = lambda i=i: gE[8 * i:8 * i + 4]
    grads["a_v_norm_g"] = lambda: gE[64:67].reshape(1, 3072)
    grads["a_v_norm_b"] = lambda: gE[72:75].reshape(1, 3072)
    grads["a_b_s"] = lambda: gE[80:88].reshape(8 * DM)[:1536].reshape(1, 12, 128)
    grads["a_w_s"] = lambda: gE[88:280].reshape(1, 12, 128, 128)
    col1 = lambda l, off, n: per_layer[l][0][:, off:off + n]
    col2 = lambda l, off, n: per_layer[l][2][:, off:off + n]
    grads["ff1_w_gate"] = lambda: jnp.stack([col1(l, A_FF(l, 0), FW) for l in range(4)])
    grads["ff1_w_up"] = lambda: jnp.stack([col1(l, A_FF(l, 1), FW) for l in range(4)])
    grads["ff2_w_gate"] = lambda: jnp.stack([col2(l, A_FF(l, 2), FW) for l in range(4)])
    grads["ff2_w_up"] = lambda: jnp.stack([col2(l, A_FF(l, 3), FW) for l in range(4)])
    grads["a_w_in"] = lambda: col2(0, A_AIN, 1536)[None]
    grads["c_w_pw1"] = lambda: col2(2, A_CIN, 512)[None]
    grads["d_w_in"] = lambda: col2(3, A_DIN, 768)[None]
    row2 = lambda l, off, n: per_layer[l][3][off:off + n]
    grads["ff1_w_down"] = lambda: jnp.stack([per_layer[l][1][:FW] for l in range(4)])
    grads["ff2_w_down"] = lambda: jnp.stack([row2(l, B_FF2D(l), FW) for l in range(4)])
    grads["ple_w_gate"] = lambda: jnp.stack([row2(l, B_PLEG(l), 256) for l in range(4)])
    grads["a_w_out"] = lambda: row2(0, B_AOUT, 768)[None]
    grads["c_w_pw2"] = lambda: row2(2, B_CPW2, 256)[None]
    grads["d_w_out"] = lambda: row2(3, B_DOUT, 256)[None]
    grads["ple_w_proj"] = lambda: jnp.stack([per_layer[l][4][C_PROJ(l):C_PROJ(l) + 256] for l in range(4)])
    grads["b_w_grp"] = lambda: per_layer[1][4][C_BGRP:C_BGRP + 256].reshape(1, 4, 64, 256)
    grads["b_scale"] = lambda: RV[V_BSCALE:V_BSCALE + 1]
    grads["c_b_dw"] = lambda: RV[V_CBDW:V_CBDW + 1]
    grads["c_norm_g"] = lambda: RV[V_CNG:V_CNG + 1]
    grads["c_norm_b"] = lambda: RV[V_CNB:V_CNB + 1]
    grads["d_w_conv"] = lambda: RV[V_DCONV:V_DCONV + 3][None]
    grads["c_w_dw"] = lambda: RV[V_CDW:V_CDW + 31][None]
    return {k: f() for k, f in grads.items()}
```
